```python
import jax, jax.numpy as jnp
from jax import lax
import numpy as np

D_MODEL = 1024
BATCH = 16
SEQ = 2048
DEPTH = 2

N_META = 16
POOL_WINDOWS = (2, 4, 8, 16)
POOL_GROUP = 128
POOL_WIDTH = POOL_GROUP * len(POOL_WINDOWS)
N_HEADS = 16
QK_NOPE = 64
QK_ROPE = 32
V_DIM = 64
Q_RANK = 256
KV_RANK = 128
QK_DIM = QK_NOPE + QK_ROPE
ATT_WIDTH = N_HEADS * V_DIM
SM_SCALE = QK_DIM ** -0.5
ROPE_THETA = 10000.0
Q_BLOCK = 128
D_FF = -(-8 * D_MODEL // (3 * 256)) * 256
NORM_EPS = 1e-6
MASK_VALUE = -1e30
IN_SIZES = (POOL_WIDTH, Q_RANK, KV_RANK, QK_ROPE, D_MODEL, D_MODEL)
D_IN = POOL_WIDTH + Q_RANK + KV_RANK + QK_ROPE + 2 * D_MODEL
IN_OFFSETS = (POOL_WIDTH,
              POOL_WIDTH + Q_RANK,
              POOL_WIDTH + Q_RANK + KV_RANK,
              POOL_WIDTH + Q_RANK + KV_RANK + QK_ROPE,
              POOL_WIDTH + Q_RANK + KV_RANK + QK_ROPE + D_MODEL)

kernel_name = "hybrid_pool_mla_gated_block"


def rmsnorm(x, g):
    xf = x.astype(jnp.float32)
    y = xf * lax.rsqrt(jnp.mean(xf * xf, axis=-1, keepdims=True) + NORM_EPS)
    return (y * g.astype(jnp.float32)).astype(x.dtype)


def rope_tables(length):
    inv = 1.0 / (ROPE_THETA ** (jnp.arange(0, QK_ROPE, 2, dtype=jnp.float32) / QK_ROPE))
    ang = jnp.arange(length, dtype=jnp.float32)[:, None] * inv[None, :]
    return jnp.cos(ang), jnp.sin(ang)


def apply_rope(x, cos, sin):
    xf = x.astype(jnp.float32)
    x1, x2 = jnp.split(xf, 2, axis=-1)
    out = jnp.concatenate([x1 * cos - x2 * sin, x1 * sin + x2 * cos], axis=-1)
    return out.astype(x.dtype)


def pool_mixer(u, pool_w, pool_scale):
    B, L, _ = u.shape
    cs = jnp.cumsum(u.astype(jnp.float32), axis=1)
    cs0 = jnp.concatenate([jnp.zeros((B, 1, POOL_WIDTH), jnp.float32), cs], axis=1)
    t = jnp.arange(L, dtype=jnp.float32)[:, None]
    groups = []
    for g, w in enumerate(POOL_WINDOWS):
        c = cs0[:, :, g * POOL_GROUP:(g + 1) * POOL_GROUP]
        prev = jnp.pad(c[:, :L + 1 - w], ((0, 0), (w, 0), (0, 0)))
        wsum = (c - prev)[:, 1:]
        count = jnp.minimum(t + 1.0, float(w))
        ug = u[:, :, g * POOL_GROUP:(g + 1) * POOL_GROUP].astype(jnp.float32)
        groups.append(wsum / count - ug)
    y = jnp.stack(groups, axis=2).astype(u.dtype)
    y = jnp.einsum('blgc,gcd->blgd', y, pool_w).reshape(B, L, POOL_WIDTH)
    return y * pool_scale


def mla_attention(q_nope, q_rope, k_nope, k_rope, v):
    L = q_nope.shape[1]
    outs = []
    for start in range(0, L, Q_BLOCK):
        end = min(start + Q_BLOCK, L)
        s = (jnp.einsum('bqhd,bkhd->bhqk', q_nope[:, start:end], k_nope[:, :end])
             + jnp.einsum('bqhr,bkr->bhqk', q_rope[:, start:end], k_rope[:, :end]))
        s = s.astype(jnp.float32) * SM_SCALE
        mask = jnp.arange(end)[None, :] <= jnp.arange(start, end)[:, None]
        s = jnp.where(mask[None, None], s, MASK_VALUE)
        p = jax.nn.softmax(s, axis=-1).astype(v.dtype)
        outs.append(jnp.einsum('bhqk,bkhd->bqhd', p, v[:, :end]))
    return jnp.concatenate(outs, axis=1)


def hybrid_layer(h, cos, sin, g_mix, w_in, pool_w, pool_scale, q_norm_g, kv_norm_g, w_uq, w_ukv,
                 w_pa, w_pb, w_o, g_ffn, w_gate, w_up, w_down):
    B, L, _ = h.shape
    hn = rmsnorm(h, g_mix)
    z = hn @ w_in
    u, c_q, c_kv, k_rope, gate_a, gate_b = jnp.split(z, IN_OFFSETS, axis=-1)
    a = pool_mixer(u, pool_w, pool_scale)
    q = (rmsnorm(c_q, q_norm_g) @ w_uq).reshape(B, L, N_HEADS, QK_DIM)
    q_nope, q_rope = q[..., :QK_NOPE], q[..., QK_NOPE:]
    q_rope = apply_rope(q_rope, cos[:, None, :], sin[:, None, :])
    kv = (rmsnorm(c_kv, kv_norm_g) @ w_ukv).reshape(B, L, N_HEADS, QK_NOPE + V_DIM)
    k_nope, v = kv[..., :QK_NOPE], kv[..., QK_NOPE:]
    k_rope = apply_rope(k_rope, cos, sin)
    b = mla_attention(q_nope, q_rope, k_nope, k_rope, v).reshape(B, L, ATT_WIDTH)
    merged = jax.nn.sigmoid(gate_a) * (a @ w_pa) + jax.nn.sigmoid(gate_b) * (b @ w_pb)
    h = h + merged @ w_o
    hn = rmsnorm(h, g_ffn)
    h = h + (jax.nn.silu(hn @ w_gate) * (hn @ w_up)) @ w_down
    return h


def _fwd_setup_inputs(seed: int = 0) -> dict:
    key = jax.random.key(seed)
    ks = jax.random.split(key, 20)

    def w(k, shape, fan_in):
        return jax.random.normal(k, shape, jnp.float32) * (fan_in ** -0.5)

    def gain(k, shape):
        return 1.0 + 0.05 * jax.random.normal(k, shape, jnp.float32)

    return {
        "x": jax.random.normal(ks[0], (BATCH, SEQ, D_MODEL), jnp.float32),
        "meta_tokens": jax.random.normal(ks[1], (N_META, D_MODEL), jnp.float32),
        "norm_mix_g": gain(ks[2], (DEPTH, D_MODEL)),
        "w_in": w(ks[3], (DEPTH, D_MODEL, D_IN), D_MODEL),
        "pool_w": w(ks[4], (DEPTH, len(POOL_WINDOWS), POOL_GROUP, POOL_GROUP), POOL_GROUP),
        "pool_scale": gain(ks[5], (DEPTH, POOL_WIDTH)),
        "q_norm_g": gain(ks[6], (DEPTH, Q_RANK)),
        "kv_norm_g": gain(ks[7], (DEPTH, KV_RANK)),
        "w_uq": w(ks[8], (DEPTH, Q_RANK, N_HEADS * QK_DIM), Q_RANK),
        "w_ukv": w(ks[9], (DEPTH, KV_RANK, N_HEADS * (QK_NOPE + V_DIM)), KV_RANK),
        "w_pa": w(ks[10], (DEPTH, POOL_WIDTH, D_MODEL), POOL_WIDTH),
        "w_pb": w(ks[11], (DEPTH, ATT_WIDTH, D_MODEL), ATT_WIDTH),
        "w_o": w(ks[12], (DEPTH, D_MODEL, D_MODEL), D_MODEL),
        "norm_ffn_g": gain(ks[13], (DEPTH, D_MODEL)),
        "w_gate": w(ks[14], (DEPTH, D_MODEL, D_FF), D_MODEL),
        "w_up": w(ks[15], (DEPTH, D_MODEL, D_FF), D_MODEL),
        "w_down": w(ks[16], (DEPTH, D_FF, D_MODEL), D_FF),
        "final_norm_g": gain(ks[17], (D_MODEL,)),
    }


def _fwd_reference(x, meta_tokens, norm_mix_g, w_in, pool_w, pool_scale, q_norm_g, kv_norm_g, w_uq, w_ukv,
              w_pa, w_pb, w_o, norm_ffn_g, w_gate, w_up, w_down, final_norm_g):
    B = x.shape[0]
    meta = jnp.broadcast_to(meta_tokens.astype(x.dtype)[None], (B, N_META, D_MODEL))
    h = jnp.concatenate([meta, x], axis=1)
    cos, sin = rope_tables(h.shape[1])
    for i in range(DEPTH):
        h = hybrid_layer(h, cos, sin, norm_mix_g[i], w_in[i], pool_w[i], pool_scale[i], q_norm_g[i],
                         kv_norm_g[i], w_uq[i], w_ukv[i], w_pa[i], w_pb[i], w_o[i], norm_ffn_g[i],
                         w_gate[i], w_up[i], w_down[i])
    return rmsnorm(h, final_norm_g)[:, N_META:]


import jax as _jax
import jax.numpy as _jnp

TWIN_FORMAT = 'train_step'
FWD_PARAMS = ['x', 'meta_tokens', 'norm_mix_g', 'w_in', 'pool_w', 'pool_scale', 'q_norm_g', 'kv_norm_g', 'w_uq', 'w_ukv', 'w_pa', 'w_pb', 'w_o', 'norm_ffn_g', 'w_gate', 'w_up', 'w_down', 'final_norm_g']
TWIN_WEIGHTS = ['meta_tokens', 'norm_mix_g', 'w_in', 'pool_w', 'pool_scale', 'q_norm_g', 'kv_norm_g', 'w_uq', 'w_ukv', 'w_pa', 'w_pb', 'w_o', 'norm_ffn_g', 'w_gate', 'w_up', 'w_down', 'final_norm_g']
TWIN_DIFF_INPUT = 'x'
TWIN_INPUTS = ['x', 'meta_tokens', 'norm_mix_g', 'w_in', 'pool_w', 'pool_scale', 'q_norm_g', 'kv_norm_g', 'w_uq', 'w_ukv', 'w_pa', 'w_pb', 'w_o', 'norm_ffn_g', 'w_gate', 'w_up', 'w_down', 'final_norm_g', 'loss_target', 'm_meta_tokens', 'm_norm_mix_g', 'm_w_in', 'm_pool_w', 'm_pool_scale', 'm_q_norm_g', 'm_kv_norm_g', 'm_w_uq', 'm_w_ukv', 'm_w_pa', 'm_w_pb', 'm_w_o', 'm_norm_ffn_g', 'm_w_gate', 'm_w_up', 'm_w_down', 'm_final_norm_g', 'v_meta_tokens', 'v_norm_mix_g', 'v_w_in', 'v_pool_w', 'v_pool_scale', 'v_q_norm_g', 'v_kv_norm_g', 'v_w_uq', 'v_w_ukv', 'v_w_pa', 'v_w_pb', 'v_w_o', 'v_norm_ffn_g', 'v_w_gate', 'v_w_up', 'v_w_down', 'v_final_norm_g']
TWIN_OUTPUTS = ['loss', 'grad_x', 'grad_meta_tokens', 'grad_norm_mix_g', 'grad_w_in', 'grad_pool_w', 'grad_pool_scale', 'grad_q_norm_g', 'grad_kv_norm_g', 'grad_w_uq', 'grad_w_ukv', 'grad_w_pa', 'grad_w_pb', 'grad_w_o', 'grad_norm_ffn_g', 'grad_w_gate', 'grad_w_up', 'grad_w_down', 'grad_final_norm_g', 'delta_meta_tokens', 'delta_norm_mix_g', 'delta_w_in', 'delta_pool_w', 'delta_pool_scale', 'delta_q_norm_g', 'delta_kv_norm_g', 'delta_w_uq', 'delta_w_ukv', 'delta_w_pa', 'delta_w_pb', 'delta_w_o', 'delta_norm_ffn_g', 'delta_w_gate', 'delta_w_up', 'delta_w_down', 'delta_final_norm_g', 'new_m_meta_tokens', 'new_m_norm_mix_g', 'new_m_w_in', 'new_m_pool_w', 'new_m_pool_scale', 'new_m_q_norm_g', 'new_m_kv_norm_g', 'new_m_w_uq', 'new_m_w_ukv', 'new_m_w_pa', 'new_m_w_pb', 'new_m_w_o', 'new_m_norm_ffn_g', 'new_m_w_gate', 'new_m_w_up', 'new_m_w_down', 'new_m_final_norm_g', 'new_v_meta_tokens', 'new_v_norm_mix_g', 'new_v_w_in', 'new_v_pool_w', 'new_v_pool_scale', 'new_v_q_norm_g', 'new_v_kv_norm_g', 'new_v_w_uq', 'new_v_w_ukv', 'new_v_w_pa', 'new_v_w_pb', 'new_v_w_o', 'new_v_norm_ffn_g', 'new_v_w_gate', 'new_v_w_up', 'new_v_w_down', 'new_v_final_norm_g']
TWIN_LEAF_KINDS = {'loss': 'loss', 'grad_x': 'grad_x', 'grad_meta_tokens': 'grad_w', 'grad_norm_mix_g': 'grad_w', 'grad_w_in': 'grad_w', 'grad_pool_w': 'grad_w', 'grad_pool_scale': 'grad_w', 'grad_q_norm_g': 'grad_w', 'grad_kv_norm_g': 'grad_w', 'grad_w_uq': 'grad_w', 'grad_w_ukv': 'grad_w', 'grad_w_pa': 'grad_w', 'grad_w_pb': 'grad_w', 'grad_w_o': 'grad_w', 'grad_norm_ffn_g': 'grad_w', 'grad_w_gate': 'grad_w', 'grad_w_up': 'grad_w', 'grad_w_down': 'grad_w', 'grad_final_norm_g': 'grad_w', 'delta_meta_tokens': 'delta_w', 'delta_norm_mix_g': 'delta_w', 'delta_w_in': 'delta_w', 'delta_pool_w': 'delta_w', 'delta_pool_scale': 'delta_w', 'delta_q_norm_g': 'delta_w', 'delta_kv_norm_g': 'delta_w', 'delta_w_uq': 'delta_w', 'delta_w_ukv': 'delta_w', 'delta_w_pa': 'delta_w', 'delta_w_pb': 'delta_w', 'delta_w_o': 'delta_w', 'delta_norm_ffn_g': 'delta_w', 'delta_w_gate': 'delta_w', 'delta_w_up': 'delta_w', 'delta_w_down': 'delta_w', 'delta_final_norm_g': 'delta_w', 'new_m_meta_tokens': 'new_m', 'new_m_norm_mix_g': 'new_m', 'new_m_w_in': 'new_m', 'new_m_pool_w': 'new_m', 'new_m_pool_scale': 'new_m', 'new_m_q_norm_g': 'new_m', 'new_m_kv_norm_g': 'new_m', 'new_m_w_uq': 'new_m', 'new_m_w_ukv': 'new_m', 'new_m_w_pa': 'new_m', 'new_m_w_pb': 'new_m', 'new_m_w_o': 'new_m', 'new_m_norm_ffn_g': 'new_m', 'new_m_w_gate': 'new_m', 'new_m_w_up': 'new_m', 'new_m_w_down': 'new_m', 'new_m_final_norm_g': 'new_m', 'new_v_meta_tokens': 'new_v', 'new_v_norm_mix_g': 'new_v', 'new_v_w_in': 'new_v', 'new_v_pool_w': 'new_v', 'new_v_pool_scale': 'new_v', 'new_v_q_norm_g': 'new_v', 'new_v_kv_norm_g': 'new_v', 'new_v_w_uq': 'new_v', 'new_v_w_ukv': 'new_v', 'new_v_w_pa': 'new_v', 'new_v_w_pb': 'new_v', 'new_v_w_o': 'new_v', 'new_v_norm_ffn_g': 'new_v', 'new_v_w_gate': 'new_v', 'new_v_w_up': 'new_v', 'new_v_w_down': 'new_v', 'new_v_final_norm_g': 'new_v'}


def _forward(args):
    return _fwd_reference(*[args[k] for k in FWD_PARAMS])


def _output_shape():
    out = _jax.eval_shape(lambda: _forward(_fwd_setup_inputs(0)))
    return out.shape, out.dtype

N_MICROBATCH = 1
ADAM_LR = 0.001
ADAM_B1 = 0.9
ADAM_B2 = 0.999
ADAM_EPS = 1e-08
ADAM_WD = 0.01
ADAM_STEP = 10
PER_EXAMPLE_BATCH_AXIS = {'x': 0, 'loss_target': 0}
SHARED_INPUTS = []
_WEIGHT_DTYPES = {'meta_tokens': _jnp.float32, 'norm_mix_g': _jnp.float32, 'w_in': _jnp.float32, 'pool_w': _jnp.float32, 'pool_scale': _jnp.float32, 'q_norm_g': _jnp.float32, 'kv_norm_g': _jnp.float32, 'w_uq': _jnp.float32, 'w_ukv': _jnp.float32, 'w_pa': _jnp.float32, 'w_pb': _jnp.float32, 'w_o': _jnp.float32, 'norm_ffn_g': _jnp.float32, 'w_gate': _jnp.float32, 'w_up': _jnp.float32, 'w_down': _jnp.float32, 'final_norm_g': _jnp.float32}
MOMENT_SCALE = {'meta_tokens': 4.451836e-03, 'norm_mix_g': 9.036186e-02, 'w_in': 5.399955e-02, 'pool_w': 1.176807e-01, 'pool_scale': 1.216031e-01, 'q_norm_g': 3.269853e-02, 'kv_norm_g': 6.458536e-02, 'w_uq': 1.284945e-02, 'w_ukv': 1.525491e-02, 'w_pa': 8.343570e-02, 'w_pb': 1.696449e-02, 'w_o': 8.361031e-02, 'norm_ffn_g': 1.231590e-01, 'w_gate': 5.198301e-02, 'w_up': 5.045017e-02, 'w_down': 8.395460e-02, 'final_norm_g': 3.209680e+01}


def _to_microbatches(a, axis):
    t = _jnp.moveaxis(a, axis, 0)
    t = t.reshape((N_MICROBATCH, t.shape[0] // N_MICROBATCH) + t.shape[1:])
    return _jnp.moveaxis(t, 1, axis + 1)


def setup_inputs(seed: int = 0) -> dict:
    inp = _fwd_setup_inputs(seed)
    key = _jax.random.fold_in(_jax.random.key(seed), 7919)
    shape, _ = _output_shape()
    out = dict(inp)
    out["loss_target"] = _jax.random.normal(_jax.random.fold_in(key, 0), shape, _jnp.float32)
    for i, name in enumerate(TWIN_WEIGHTS):
        w = inp[name].astype(_jnp.float32)
        if MOMENT_SCALE is None:
            s = _jnp.sqrt(_jnp.mean(_jnp.square(w)) + 1e-30)
        else:
            s = MOMENT_SCALE[name]
        km, kv = _jax.random.split(_jax.random.fold_in(key, i + 1))
        out[name] = w
        out["m_" + name] = s * _jax.random.normal(km, w.shape, _jnp.float32)
        out["v_" + name] = (s * s) * _jax.random.uniform(kv, w.shape, _jnp.float32, 0.5, 1.5)
    if N_MICROBATCH > 1:
        for name, axis in PER_EXAMPLE_BATCH_AXIS.items():
            out[name] = _to_microbatches(out[name], axis)
    return {'x': out['x'], 'meta_tokens': out['meta_tokens'], 'norm_mix_g': out['norm_mix_g'], 'w_in': out['w_in'], 'pool_w': out['pool_w'], 'pool_scale': out['pool_scale'], 'q_norm_g': out['q_norm_g'], 'kv_norm_g': out['kv_norm_g'], 'w_uq': out['w_uq'], 'w_ukv': out['w_ukv'], 'w_pa': out['w_pa'], 'w_pb': out['w_pb'], 'w_o': out['w_o'], 'norm_ffn_g': out['norm_ffn_g'], 'w_gate': out['w_gate'], 'w_up': out['w_up'], 'w_down': out['w_down'], 'final_norm_g': out['final_norm_g'], 'loss_target': out['loss_target'], 'm_meta_tokens': out['m_meta_tokens'], 'm_norm_mix_g': out['m_norm_mix_g'], 'm_w_in': out['m_w_in'], 'm_pool_w': out['m_pool_w'], 'm_pool_scale': out['m_pool_scale'], 'm_q_norm_g': out['m_q_norm_g'], 'm_kv_norm_g': out['m_kv_norm_g'], 'm_w_uq': out['m_w_uq'], 'm_w_ukv': out['m_w_ukv'], 'm_w_pa': out['m_w_pa'], 'm_w_pb': out['m_w_pb'], 'm_w_o': out['m_w_o'], 'm_norm_ffn_g': out['m_norm_ffn_g'], 'm_w_gate': out['m_w_gate'], 'm_w_up': out['m_w_up'], 'm_w_down': out['m_w_down'], 'm_final_norm_g': out['m_final_norm_g'], 'v_meta_tokens': out['v_meta_tokens'], 'v_norm_mix_g': out['v_norm_mix_g'], 'v_w_in': out['v_w_in'], 'v_pool_w': out['v_pool_w'], 'v_pool_scale': out['v_pool_scale'], 'v_q_norm_g': out['v_q_norm_g'], 'v_kv_norm_g': out['v_kv_norm_g'], 'v_w_uq': out['v_w_uq'], 'v_w_ukv': out['v_w_ukv'], 'v_w_pa': out['v_w_pa'], 'v_w_pb': out['v_w_pb'], 'v_w_o': out['v_w_o'], 'v_norm_ffn_g': out['v_norm_ffn_g'], 'v_w_gate': out['v_w_gate'], 'v_w_up': out['v_w_up'], 'v_w_down': out['v_w_down'], 'v_final_norm_g': out['v_final_norm_g']}


def _loss(weights, diff, rest, loss_target):
    with _jax.named_scope("forward"):
        args = {**rest, TWIN_DIFF_INPUT: diff, **{k: w.astype(_WEIGHT_DTYPES[k]) for k, w in weights.items()}}
        y = _forward(args)
    with _jax.named_scope("loss_head"):
        err = _jnp.square(y.astype(_jnp.float32) - loss_target)
        return 0.5 * _jnp.sum(_jnp.mean(err, axis=-1)) if err.ndim else 0.5 * err


def _adamw(w, g, m, v):
    m = ADAM_B1 * m + (1.0 - ADAM_B1) * g
    v = ADAM_B2 * v + (1.0 - ADAM_B2) * _jnp.square(g)
    m_hat = m / (1.0 - ADAM_B1 ** ADAM_STEP)
    v_hat = v / (1.0 - ADAM_B2 ** ADAM_STEP)
    delta = -ADAM_LR * (m_hat / (_jnp.sqrt(v_hat) + ADAM_EPS) + ADAM_WD * w)
    return delta, m, v


def reference(x, meta_tokens, norm_mix_g, w_in, pool_w, pool_scale, q_norm_g, kv_norm_g, w_uq, w_ukv, w_pa, w_pb, w_o, norm_ffn_g, w_gate, w_up, w_down, final_norm_g, loss_target, m_meta_tokens, m_norm_mix_g, m_w_in, m_pool_w, m_pool_scale, m_q_norm_g, m_kv_norm_g, m_w_uq, m_w_ukv, m_w_pa, m_w_pb, m_w_o, m_norm_ffn_g, m_w_gate, m_w_up, m_w_down, m_final_norm_g, v_meta_tokens, v_norm_mix_g, v_w_in, v_pool_w, v_pool_scale, v_q_norm_g, v_kv_norm_g, v_w_uq, v_w_ukv, v_w_pa, v_w_pb, v_w_o, v_norm_ffn_g, v_w_gate, v_w_up, v_w_down, v_final_norm_g):
    given = dict(x=x, meta_tokens=meta_tokens, norm_mix_g=norm_mix_g, w_in=w_in, pool_w=pool_w, pool_scale=pool_scale, q_norm_g=q_norm_g, kv_norm_g=kv_norm_g, w_uq=w_uq, w_ukv=w_ukv, w_pa=w_pa, w_pb=w_pb, w_o=w_o, norm_ffn_g=norm_ffn_g, w_gate=w_gate, w_up=w_up, w_down=w_down, final_norm_g=final_norm_g, loss_target=loss_target, m_meta_tokens=m_meta_tokens, m_norm_mix_g=m_norm_mix_g, m_w_in=m_w_in, m_pool_w=m_pool_w, m_pool_scale=m_pool_scale, m_q_norm_g=m_q_norm_g, m_kv_norm_g=m_kv_norm_g, m_w_uq=m_w_uq, m_w_ukv=m_w_ukv, m_w_pa=m_w_pa, m_w_pb=m_w_pb, m_w_o=m_w_o, m_norm_ffn_g=m_norm_ffn_g, m_w_gate=m_w_gate, m_w_up=m_w_up, m_w_down=m_w_down, m_final_norm_g=m_final_norm_g, v_meta_tokens=v_meta_tokens, v_norm_mix_g=v_norm_mix_g, v_w_in=v_w_in, v_pool_w=v_pool_w, v_pool_scale=v_pool_scale, v_q_norm_g=v_q_norm_g, v_kv_norm_g=v_kv_norm_g, v_w_uq=v_w_uq, v_w_ukv=v_w_ukv, v_w_pa=v_w_pa, v_w_pb=v_w_pb, v_w_o=v_w_o, v_norm_ffn_g=v_norm_ffn_g, v_w_gate=v_w_gate, v_w_up=v_w_up, v_w_down=v_w_down, v_final_norm_g=v_final_norm_g)
    weights = {n: given[n] for n in TWIN_WEIGHTS}
    shared = {n: given[n] for n in SHARED_INPUTS}
    per_example = {n: given[n] for n in ['x']}
    grad_fn = _jax.value_and_grad(_loss, argnums=(0, 1))

    def one_microbatch(ex, loss_target):
        ex = dict(ex)
        diff = ex.pop(TWIN_DIFF_INPUT)
        return grad_fn(weights, diff, {**shared, **ex}, loss_target)

    if N_MICROBATCH == 1:
        loss, (grad_w, grad_x) = one_microbatch(per_example, given["loss_target"])
    else:
        def body(carry, xs):
            loss_sum, grad_sum = carry
            l_k, (gw_k, gx_k) = one_microbatch(xs[0], xs[1])
            with _jax.named_scope("update"):
                return (loss_sum + l_k, _jax.tree.map(_jnp.add, grad_sum, gw_k)), gx_k

        init = (_jnp.zeros((), _jnp.float32), _jax.tree.map(_jnp.zeros_like, weights))
        (loss, grad_w), grad_x = _jax.lax.scan(body, init, (per_example, given["loss_target"]))
    with _jax.named_scope("update"):
        delta_w, new_m, new_v = {}, {}, {}
        for n in TWIN_WEIGHTS:
            delta_w[n], new_m[n], new_v[n] = _adamw(weights[n], grad_w[n], given["m_" + n], given["v_" + n])
    return (loss, grad_x, *[grad_w[n] for n in TWIN_WEIGHTS], *[delta_w[n] for n in TWIN_WEIGHTS],
            *[new_m[n] for n in TWIN_WEIGHTS], *[new_v[n] for n in TWIN_WEIGHTS])
```

```python
import functools

import jax
import jax.numpy as jnp
from jax import lax
from jax.experimental import pallas as pl
from jax.experimental.pallas import tpu as pltpu

F32, BF16 = jnp.float32, jnp.bfloat16

D_MODEL = 1024
N_META = 16
N_HEADS = 16
QK_NOPE, QK_ROPE, V_DIM = 64, 32, 64
HALF_ROPE = QK_ROPE // 2
Q_RANK, KV_RANK = 256, 128
POOL_WINDOWS = (2, 4, 8, 16)
POOL_GROUP = 128
POOL_WIDTH = POOL_GROUP * len(POOL_WINDOWS)
POOL_HALO = 16
D_FF = 2816
D_IN = 2976
NORM_EPS = 1e-6
SM_SCALE = (QK_NOPE + QK_ROPE) ** -0.5
MASK_VALUE = -1e30
ROPE_THETA = 10000.0
DEPTH = 2
N_DEV = 8

ADAM_LR, ADAM_B1, ADAM_B2, ADAM_EPS, ADAM_WD, ADAM_STEP = 0.001, 0.9, 0.999, 1e-08, 0.01, 10

LANES = 128
HEAD_SLOT = LANES
QK_WIDTH = N_HEADS * HEAD_SLOT
Z_CQ, Z_CKV, Z_KR, Z_GA, Z_GB, DZ = 512, 768, 896, 1024, 2048, 3072
TQ = TK = 256
VMEM_LIMIT = 56 * 1024 * 1024


def _cparams():
    return pltpu.CompilerParams(vmem_limit_bytes=VMEM_LIMIT)


def _rows(tm, width, col=0):
    return pl.BlockSpec((tm, width), lambda i: (i, col))


def _whole(shape):
    zeros = (0,) * len(shape)
    return pl.BlockSpec(shape, lambda i: zeros, pipeline_mode=pl.Buffered(1))


def _acc(shape):
    zeros = (0,) * len(shape)
    return pl.BlockSpec(shape, lambda i: zeros)


def _dot(a, b):
    return jnp.dot(a, b, preferred_element_type=F32)


def _dot_tn(a, b):
    return lax.dot_general(a, b, (((0,), (0,)), ((), ())), preferred_element_type=F32)


def _dot_nt(a, b):
    return lax.dot_general(a, b, (((1,), (1,)), ((), ())), preferred_element_type=F32)


def _rms(x):
    r = lax.rsqrt(jnp.mean(x * x, axis=-1, keepdims=True) + NORM_EPS)
    return x * r, r


def _rms_bwd(dy, xhat, r, g):
    dg = jnp.sum(dy * xhat, axis=0, keepdims=True)
    dxh = dy * g
    dx = r * (dxh - xhat * jnp.mean(dxh * xhat, axis=-1, keepdims=True))
    return dx, dg


def _sigmoid(x):
    return 1.0 / (1.0 + jnp.exp(-x))


def _rope_fwd(q, c, s1, s2):
    w = q.shape[1]
    return q * c + pltpu.roll(q, w - HALF_ROPE, 1) * s1 + pltpu.roll(q, HALF_ROPE, 1) * s2


def _rope_bwd(dq, c, s1, s2):
    w = dq.shape[1]
    return dq * c + pltpu.roll(dq * s1, HALF_ROPE, 1) + pltpu.roll(dq * s2, w - HALF_ROPE, 1)


def _rope_tables(rope, reps):
    c, cr, s1, s2 = (rope[:, k * LANES:(k + 1) * LANES] for k in range(4))
    if reps > 1:
        return jnp.tile(c, (1, reps)), jnp.tile(s1, (1, reps)), jnp.tile(s2, (1, reps))
    return cr, s1, s2


def _seq_pos(gi, lp, nb):
    pos = gi
    for b in range(1, nb):
        pos = jnp.where(gi >= b * lp, gi - b * lp, pos)
    return pos


def in_proj_fwd(h, g_mix, win, gq, gkv, wuq, wuk, wuv, rope, *, tm, name):
    t = h.shape[0]

    def body(h_ref, g_ref, win_ref, gq_ref, gkv_ref, wuq_ref, wuk_ref, wuv_ref, rope_ref, z_ref, q_ref, k_ref, v_ref):
        xhat, _ = _rms(h_ref[...])
        hn = (xhat * g_ref[...]).astype(BF16)
        z = _dot(hn, win_ref[...])
        z_ref[...] = z
        rope_t = rope_ref[...]
        xq, _ = _rms(z[:, Z_CQ:Z_CKV])
        cqn = (xq * gq_ref[...]).astype(BF16)
        q = _rope_fwd(_dot(cqn, wuq_ref[...]), *_rope_tables(rope_t, N_HEADS))
        q_ref[...] = q.astype(BF16)
        xkv, _ = _rms(z[:, Z_CKV:Z_KR])
        ckvn = (xkv * gkv_ref[...]).astype(BF16)
        kr = _rope_fwd(z[:, Z_KR:Z_GA], *_rope_tables(rope_t, 1))
        k_ref[...] = (_dot(ckvn, wuk_ref[...]) + jnp.tile(kr, (1, N_HEADS))).astype(BF16)
        v_ref[...] = _dot(ckvn, wuv_ref[...]).astype(BF16)

    return pl.pallas_call(
        body, name=name, grid=(t // tm,),
        in_specs=[_rows(tm, D_MODEL), _whole((1, D_MODEL)), _whole((D_MODEL, DZ)), _whole((1, Q_RANK)), _whole((1, KV_RANK)),
                  _whole((Q_RANK, QK_WIDTH)), _whole((KV_RANK, QK_WIDTH)), _whole((KV_RANK, D_MODEL)), _rows(tm, 4 * LANES)],
        out_specs=[_rows(tm, DZ), _rows(tm, QK_WIDTH), _rows(tm, QK_WIDTH), _rows(tm, D_MODEL)],
        out_shape=[jax.ShapeDtypeStruct((t, DZ), F32), jax.ShapeDtypeStruct((t, QK_WIDTH), BF16),
                   jax.ShapeDtypeStruct((t, QK_WIDTH), BF16), jax.ShapeDtypeStruct((t, D_MODEL), BF16)],
        compiler_params=_cparams(),
    )(h, g_mix, win, gq, gkv, wuq, wuk, wuv, rope)


def attn_fwd(q, k, v, *, nb, lp, name):
    t = q.shape[0]
    nq = lp // TQ

    def body(q_ref, k_ref, v_ref, o_ref, lse_ref):
        lane = lax.broadcasted_iota(jnp.int32, (TQ, LANES), 1)
        causal = lax.broadcasted_iota(jnp.int32, (TQ, TK), 1) <= lax.broadcasted_iota(jnp.int32, (TQ, TK), 0)

        def q_block(qi, carry):
            qs = pl.multiple_of(qi * TQ, TQ)
            outs, lses = [], []
            for hh in range(2):
                hs = slice(hh * HEAD_SLOT, (hh + 1) * HEAD_SLOT)
                qh = q_ref[pl.ds(qs, TQ), hs]

                def k_step(kj, c, masked, hs=hs, qh=qh):
                    m, l, acc = c
                    ks = pl.multiple_of(kj * TK, TK)
                    s = _dot_nt(qh, k_ref[pl.ds(ks, TK), hs]) * SM_SCALE
                    if masked:
                        s = jnp.where(causal, s, MASK_VALUE)
                    m_new = jnp.maximum(m, jnp.max(s, axis=1, keepdims=True))
                    p = jnp.exp(s - m_new)
                    alpha = jnp.exp(m - m_new)
                    l = alpha * l + jnp.sum(p, axis=1, keepdims=True)
                    acc = alpha * acc + _dot(p.astype(BF16), v_ref[pl.ds(ks, TK), :])
                    return m_new, l, acc

                init = (jnp.full((TQ, 1), MASK_VALUE, F32), jnp.zeros((TQ, 1), F32), jnp.zeros((TQ, LANES), F32))
                c = lax.fori_loop(0, qi, functools.partial(k_step, masked=False), init)
                m, l, acc = k_step(qi, c, True)
                outs.append(acc / l)
                lses.append(m + jnp.log(l))
            o_ref[pl.ds(qs, TQ), :] = jnp.where(lane < V_DIM, outs[0], outs[1]).astype(BF16)
            lse_ref[pl.ds(qs, TQ), :] = jnp.where(lane < V_DIM, lses[0], lses[1])
            return carry

        lax.fori_loop(0, nq, q_block, 0)

    pair = lambda w: pl.BlockSpec((lp, w), lambda b, hp: (b, hp))
    return pl.pallas_call(
        body, name=name, grid=(nb, N_HEADS // 2),
        in_specs=[pair(2 * HEAD_SLOT), pair(2 * HEAD_SLOT), pair(2 * V_DIM)],
        out_specs=[pair(2 * V_DIM), pair(2 * V_DIM)],
        out_shape=[jax.ShapeDtypeStruct((t, D_MODEL), BF16), jax.ShapeDtypeStruct((t, D_MODEL), F32)],
        compiler_params=_cparams(),
    )(q, k, v)


def _pool_band_fwd(i, tm, lp, nb):
    r = lax.broadcasted_iota(jnp.int32, (tm, POOL_HALO + tm), 0)
    e = lax.broadcasted_iota(jnp.int32, (tm, POOL_HALO + tm), 1)
    diff = r + POOL_HALO - e
    pos = _seq_pos(i * tm + lax.broadcasted_iota(jnp.int32, (tm, 1), 0), lp, nb)
    out = []
    for w in POOL_WINDOWS:
        cnt = jnp.minimum(pos + 1, w)
        band = jnp.where((diff >= 0) & (diff < cnt), 1.0, 0.0).astype(BF16)
        out.append((band, cnt.astype(F32)))
    return out


def merge_fwd(h, z, o, pw, ps, wpa, wpb, wo, *, tm, lp, nb, name):
    t = h.shape[0]
    hb = tm // POOL_HALO

    def body(h_ref, u_ref, uprev_ref, ga_ref, gb_ref, o_ref, pw_ref, ps_ref, wpa_ref, wpb_ref, wo_ref,
             h1_ref, pooled_ref, a_ref, pa_ref, pb_ref, mg_ref):
        i = pl.program_id(0)
        u = u_ref[...]
        uext = jnp.concatenate([uprev_ref[...], u], axis=0).astype(BF16)
        pooled, ys = [], []
        for g, (band, cnt) in enumerate(_pool_band_fwd(i, tm, lp, nb)):
            gs = slice(g * POOL_GROUP, (g + 1) * POOL_GROUP)
            pg = (_dot(band, uext[:, gs]) / cnt - u[:, gs]).astype(BF16)
            pooled.append(pg)
            ys.append(_dot(pg, pw_ref[g]))
        pooled_ref[...] = jnp.concatenate(pooled, axis=1)
        a = (jnp.concatenate(ys, axis=1) * ps_ref[...]).astype(BF16)
        a_ref[...] = a
        pa = _dot(a, wpa_ref[...])
        pb = _dot(o_ref[...], wpb_ref[...])
        pa_ref[...] = pa.astype(BF16)
        pb_ref[...] = pb.astype(BF16)
        mg = (_sigmoid(ga_ref[...]) * pa + _sigmoid(gb_ref[...]) * pb).astype(BF16)
        mg_ref[...] = mg
        h1_ref[...] = h_ref[...] + _dot(mg, wo_ref[...])

    halo = pl.BlockSpec((POOL_HALO, POOL_WIDTH), lambda i: (jnp.maximum(i * hb - 1, 0), 0))
    return pl.pallas_call(
        body, name=name, grid=(t // tm,),
        in_specs=[_rows(tm, D_MODEL), _rows(tm, POOL_WIDTH), halo, _rows(tm, D_MODEL, 1), _rows(tm, D_MODEL, 2), _rows(tm, D_MODEL),
                  _whole((4, POOL_GROUP, POOL_GROUP)), _whole((1, POOL_WIDTH)), _whole((POOL_WIDTH, D_MODEL)),
                  _whole((D_MODEL, D_MODEL)), _whole((D_MODEL, D_MODEL))],
        out_specs=[_rows(tm, D_MODEL), _rows(tm, POOL_WIDTH), _rows(tm, POOL_WIDTH), _rows(tm, D_MODEL), _rows(tm, D_MODEL),
                   _rows(tm, D_MODEL)],
        out_shape=[jax.ShapeDtypeStruct((t, D_MODEL), F32), jax.ShapeDtypeStruct((t, POOL_WIDTH), BF16),
                   jax.ShapeDtypeStruct((t, POOL_WIDTH), BF16), jax.ShapeDtypeStruct((t, D_MODEL), BF16),
                   jax.ShapeDtypeStruct((t, D_MODEL), BF16), jax.ShapeDtypeStruct((t, D_MODEL), BF16)],
        compiler_params=_cparams(),
    )(h, z, z, z, z, o, pw, ps, wpa, wpb, wo)


def ffn_fwd(h1, g, wg, wu, wd, *, tm, name):
    t = h1.shape[0]

    def body(h_ref, g_ref, wg_ref, wu_ref, wd_ref, h2_ref, gt_ref, up_ref):
        h = h_ref[...]
        xhat, _ = _rms(h)
        hn = (xhat * g_ref[...]).astype(BF16)
        gt = _dot(hn, wg_ref[...])
        up = _dot(hn, wu_ref[...])
        gt_ref[...] = gt.astype(BF16)
        up_ref[...] = up.astype(BF16)
        act = (gt * _sigmoid(gt) * up).astype(BF16)
        h2_ref[...] = h + _dot(act, wd_ref[...])

    return pl.pallas_call(
        body, name=name, grid=(t // tm,),
        in_specs=[_rows(tm, D_MODEL), _whole((1, D_MODEL)), _whole((D_MODEL, D_FF)), _whole((D_MODEL, D_FF)), _whole((D_FF, D_MODEL))],
        out_specs=[_rows(tm, D_MODEL), _rows(tm, D_FF), _rows(tm, D_FF)],
        out_shape=[jax.ShapeDtypeStruct((t, D_MODEL), F32), jax.ShapeDtypeStruct((t, D_FF), BF16), jax.ShapeDtypeStruct((t, D_FF), BF16)],
        compiler_params=_cparams(),
    )(h1, g, wg, wu, wd)


def loss_head(h, g, target, *, tm, lp, nb, seq, name):
    t = h.shape[0]
    nt = t // tm

    def body(h_ref, g_ref, t_ref, loss_ref, dh_ref, dg_ref):
        i = pl.program_id(0)
        pos = _seq_pos(i * tm + lax.broadcasted_iota(jnp.int32, (tm, 1), 0), lp, nb)
        real = (pos >= N_META) & (pos < N_META + seq)
        xhat, r = _rms(h_ref[...])
        gg = g_ref[...]
        err = jnp.where(real, xhat * gg - t_ref[...], 0.0)
        loss_ref[...] = jnp.full((8, LANES), 0.5 * jnp.sum(err * err) / D_MODEL, F32)
        dx, dg = _rms_bwd(err * (1.0 / D_MODEL), xhat, r, gg)
        dh_ref[...] = dx

        @pl.when(i == 0)
        def _():
            dg_ref[...] = jnp.zeros_like(dg_ref)

        dg_ref[...] += dg

    return pl.pallas_call(
        body, name=name, grid=(nt,),
        in_specs=[_rows(tm, D_MODEL), _whole((1, D_MODEL)), _rows(tm, D_MODEL)],
        out_specs=[pl.BlockSpec((8, LANES), lambda i: (i, 0)), _rows(tm, D_MODEL), _acc((1, D_MODEL))],
        out_shape=[jax.ShapeDtypeStruct((nt * 8, LANES), F32), jax.ShapeDtypeStruct((t, D_MODEL), F32),
                   jax.ShapeDtypeStruct((1, D_MODEL), F32)],
        compiler_params=_cparams(),
    )(h, g, target)


def wgrad(x, y, *, tm, name):
    t, m = x.shape
    n = y.shape[1]

    def body(x_ref, y_ref, o_ref):
        @pl.when(pl.program_id(0) == 0)
        def _():
            o_ref[...] = jnp.zeros_like(o_ref)

        o_ref[...] += _dot_tn(x_ref[...].astype(BF16), y_ref[...].astype(BF16))

    return pl.pallas_call(
        body, name=name, grid=(t // tm,),
        in_specs=[_rows(tm, m), _rows(tm, n)], out_specs=_acc((m, n)),
        out_shape=jax.ShapeDtypeStruct((m, n), F32), compiler_params=_cparams(),
    )(x, y)


def ffn_bwd(dh2, h1, g, gt, up, wgt, wut, wdt, *, tm, name):
    t = h1.shape[0]

    def body(dh2_ref, h_ref, g_ref, gt_ref, up_ref, wgt_ref, wut_ref, wdt_ref, dh1_ref, hn_ref, act_ref, dgt_ref, dup_ref, dg_ref):
        dh2 = dh2_ref[...]
        dact = _dot(dh2.astype(BF16), wdt_ref[...])
        gt = gt_ref[...].astype(F32)
        up = up_ref[...].astype(F32)
        sg = _sigmoid(gt)
        silu = gt * sg
        act_ref[...] = (silu * up).astype(BF16)
        dgt = (dact * up * (sg * (1.0 + gt * (1.0 - sg)))).astype(BF16)
        dup = (dact * silu).astype(BF16)
        dgt_ref[...] = dgt
        dup_ref[...] = dup
        dhn = _dot(dgt, wgt_ref[...]) + _dot(dup, wut_ref[...])
        xhat, r = _rms(h_ref[...])
        gg = g_ref[...]
        hn_ref[...] = (xhat * gg).astype(BF16)
        dx, dg = _rms_bwd(dhn, xhat, r, gg)
        dh1_ref[...] = dh2 + dx

        @pl.when(pl.program_id(0) == 0)
        def _():
            dg_ref[...] = jnp.zeros_like(dg_ref)

        dg_ref[...] += dg

    return pl.pallas_call(
        body, name=name, grid=(t // tm,),
        in_specs=[_rows(tm, D_MODEL), _rows(tm, D_MODEL), _whole((1, D_MODEL)), _rows(tm, D_FF), _rows(tm, D_FF),
                  _whole((D_FF, D_MODEL)), _whole((D_FF, D_MODEL)), _whole((D_MODEL, D_FF))],
        out_specs=[_rows(tm, D_MODEL), _rows(tm, D_MODEL), _rows(tm, D_FF), _rows(tm, D_FF), _rows(tm, D_FF), _acc((1, D_MODEL))],
        out_shape=[jax.ShapeDtypeStruct((t, D_MODEL), F32), jax.ShapeDtypeStruct((t, D_MODEL), BF16),
                   jax.ShapeDtypeStruct((t, D_FF), BF16), jax.ShapeDtypeStruct((t, D_FF), BF16),
                   jax.ShapeDtypeStruct((t, D_FF), BF16), jax.ShapeDtypeStruct((1, D_MODEL), F32)],
        compiler_params=_cparams(),
    )(dh2, h1, g, gt, up, wgt, wut, wdt)


def merge_bwd(dh1, z, pa, pb, pooled, pw, pwt, ps, wpat, wpbt, wot, *, tm, name):
    t = dh1.shape[0]

    def body(dh1_ref, ga_ref, gb_ref, pa_ref, pb_ref, pooled_ref, pw_ref, pwt_ref, ps_ref, wpat_ref, wpbt_ref, wot_ref,
             dga_ref, dgb_ref, dpa_ref, dpb_ref, do_ref, dpool_ref, dps_ref, dpw_ref):
        dmg = _dot(dh1_ref[...].astype(BF16), wot_ref[...])
        sa = _sigmoid(ga_ref[...])
        sb = _sigmoid(gb_ref[...])
        dga_ref[...] = (dmg * pa_ref[...].astype(F32) * sa * (1.0 - sa)).astype(BF16)
        dgb_ref[...] = (dmg * pb_ref[...].astype(F32) * sb * (1.0 - sb)).astype(BF16)
        dpa = (dmg * sa).astype(BF16)
        dpb = (dmg * sb).astype(BF16)
        dpa_ref[...] = dpa
        dpb_ref[...] = dpb
        do_ref[...] = _dot(dpb, wpbt_ref[...]).astype(BF16)
        da = _dot(dpa, wpat_ref[...])
        pooled = pooled_ref[...]
        ps = ps_ref[...]

        @pl.when(pl.program_id(0) == 0)
        def _():
            dps_ref[...] = jnp.zeros_like(dps_ref)
            dpw_ref[...] = jnp.zeros_like(dpw_ref)

        dps, dpool = [], []
        for g in range(len(POOL_WINDOWS)):
            gs = slice(g * POOL_GROUP, (g + 1) * POOL_GROUP)
            y = _dot(pooled[:, gs], pw_ref[g])
            dps.append(jnp.sum(da[:, gs] * y, axis=0, keepdims=True))
            dy = (da[:, gs] * ps[:, gs]).astype(BF16)
            dpool.append(_dot(dy, pwt_ref[g]))
            dpw_ref[g] += _dot_tn(pooled[:, gs], dy)
        dps_ref[...] += jnp.concatenate(dps, axis=1)
        dpool_ref[...] = jnp.concatenate(dpool, axis=1)

    return pl.pallas_call(
        body, name=name, grid=(t // tm,),
        in_specs=[_rows(tm, D_MODEL), _rows(tm, D_MODEL, 1), _rows(tm, D_MODEL, 2), _rows(tm, D_MODEL), _rows(tm, D_MODEL),
                  _rows(tm, POOL_WIDTH), _whole((4, POOL_GROUP, POOL_GROUP)), _whole((4, POOL_GROUP, POOL_GROUP)),
                  _whole((1, POOL_WIDTH)), _whole((D_MODEL, POOL_WIDTH)), _whole((D_MODEL, D_MODEL)), _whole((D_MODEL, D_MODEL))],
        out_specs=[_rows(tm, D_MODEL), _rows(tm, D_MODEL), _rows(tm, D_MODEL), _rows(tm, D_MODEL), _rows(tm, D_MODEL),
                   _rows(tm, POOL_WIDTH), _acc((1, POOL_WIDTH)), _acc((4, POOL_GROUP, POOL_GROUP))],
        out_shape=[jax.ShapeDtypeStruct((t, D_MODEL), BF16)] * 5
        + [jax.ShapeDtypeStruct((t, POOL_WIDTH), F32), jax.ShapeDtypeStruct((1, POOL_WIDTH), F32),
           jax.ShapeDtypeStruct((4, POOL_GROUP, POOL_GROUP), F32)],
        compiler_params=_cparams(),
    )(dh1, z, z, pa, pb, pooled, pw, pwt, ps, wpat, wpbt, wot)


def attn_bwd(q, k, v, o, do, lse, *, nb, lp, name):
    t = q.shape[0]
    nq = lp // TQ

    def body(q_ref, k_ref, v_ref, o_ref, do_ref, lse_ref, dq_ref, dk_ref, dv_ref, doh, delta):
        lane = lax.broadcasted_iota(jnp.int32, (lp, LANES), 1)
        first = lane < V_DIM
        do = do_ref[...]
        prod = do.astype(F32) * o_ref[...].astype(F32)
        d0 = jnp.sum(jnp.where(first, prod, 0.0), axis=1, keepdims=True)
        d1 = jnp.sum(jnp.where(first, 0.0, prod), axis=1, keepdims=True)
        delta[...] = jnp.where(first, d0, d1)
        doh[0] = jnp.where(first, do, jnp.zeros_like(do))
        doh[1] = jnp.where(first, jnp.zeros_like(do), do)
        dq_ref[...] = jnp.zeros_like(dq_ref)
        causal = lax.broadcasted_iota(jnp.int32, (TQ, TK), 1) <= lax.broadcasted_iota(jnp.int32, (TQ, TK), 0)

        def k_block(kj, carry):
            ks = pl.multiple_of(kj * TK, TK)
            vb = v_ref[pl.ds(ks, TK), :]
            dvs = []
            for hh in range(2):
                hs = slice(hh * HEAD_SLOT, (hh + 1) * HEAD_SLOT)
                one = slice(hh * V_DIM, hh * V_DIM + 1)
                kb = k_ref[pl.ds(ks, TK), hs]

                def q_step(qi, c, masked, hh=hh, hs=hs, one=one, kb=kb):
                    dk, dv = c
                    qs = pl.multiple_of(qi * TQ, TQ)
                    qh = q_ref[pl.ds(qs, TQ), hs]
                    s = _dot_nt(qh, kb) * SM_SCALE
                    if masked:
                        s = jnp.where(causal, s, MASK_VALUE)
                    p = jnp.exp(s - lse_ref[pl.ds(qs, TQ), one])
                    do_h = doh[hh, pl.ds(qs, TQ), :]
                    dp = _dot_nt(do_h, vb)
                    ds = (p * (dp - delta[pl.ds(qs, TQ), one]) * SM_SCALE).astype(BF16)
                    dv = dv + _dot_tn(p.astype(BF16), do_h)
                    dk = dk + _dot_tn(ds, qh)
                    dq_ref[pl.ds(qs, TQ), hs] += _dot(ds, kb)
                    return dk, dv

                zero = jnp.zeros((TK, LANES), F32)
                c = q_step(kj, (zero, zero), True)
                dk, dv = lax.fori_loop(kj + 1, nq, functools.partial(q_step, masked=False), c)
                dk_ref[pl.ds(ks, TK), hs] = dk
                dvs.append(dv)
            dv_ref[pl.ds(ks, TK), :] = dvs[0] + dvs[1]
            return carry

        lax.fori_loop(0, nq, k_block, 0)

    pair = lambda w: pl.BlockSpec((lp, w), lambda b, hp: (b, hp))
    return pl.pallas_call(
        body, name=name, grid=(nb, N_HEADS // 2),
        in_specs=[pair(2 * HEAD_SLOT), pair(2 * HEAD_SLOT), pair(2 * V_DIM), pair(2 * V_DIM), pair(2 * V_DIM), pair(2 * V_DIM)],
        out_specs=[pair(2 * HEAD_SLOT), pair(2 * HEAD_SLOT), pair(2 * V_DIM)],
        out_shape=[jax.ShapeDtypeStruct((t, QK_WIDTH), F32), jax.ShapeDtypeStruct((t, QK_WIDTH), F32),
                   jax.ShapeDtypeStruct((t, D_MODEL), F32)],
        scratch_shapes=[pltpu.VMEM((2, lp, LANES), BF16), pltpu.VMEM((lp, LANES), F32)],
        compiler_params=_cparams(),
    )(q, k, v, o, do, lse)


def in_proj_bwd(dh1, h, g_mix, z, dq, dk, dv, dga, dgb, dpool, wint, gq, gkv, wuqt, wukt, wuvt, rope, *, tm, lp, nb, name):
    t = h.shape[0]
    hb = tm // POOL_HALO
    last_halo = t // POOL_HALO - 1

    def body(dh1_ref, h_ref, g_ref, zcq_ref, zckv_ref, dq_ref, dk_ref, dv_ref, dga_ref, dgb_ref, dpool_ref, dnext_ref,
             wint_ref, gq_ref, gkv_ref, wuqt_ref, wukt_ref, wuvt_ref, rope_ref,
             dh_ref, hn_ref, dz_ref, cqn_ref, ckvn_ref, dqb_ref, dkb_ref, dvb_ref, dg_ref, dgq_ref, dgkv_ref):
        i = pl.program_id(0)
        rope_t = rope_ref[...]
        dqb = _rope_bwd(dq_ref[...], *_rope_tables(rope_t, N_HEADS)).astype(BF16)
        dqb_ref[...] = dqb
        xq, rq = _rms(zcq_ref[...])
        gq_v = gq_ref[...]
        cqn_ref[...] = (xq * gq_v).astype(BF16)
        dcq, dgq = _rms_bwd(_dot(dqb, wuqt_ref[...]), xq, rq, gq_v)
        dk = dk_ref[...]
        dkb = dk.astype(BF16)
        dvb = dv_ref[...].astype(BF16)
        dkb_ref[...] = dkb
        dvb_ref[...] = dvb
        xkv, rkv = _rms(zckv_ref[...])
        gkv_v = gkv_ref[...]
        ckvn_ref[...] = (xkv * gkv_v).astype(BF16)
        dckv, dgkv = _rms_bwd(_dot(dkb, wukt_ref[...]) + _dot(dvb, wuvt_ref[...]), xkv, rkv, gkv_v)
        dks = dk[:, :HEAD_SLOT]
        for hd in range(1, N_HEADS):
            dks = dks + dk[:, hd * HEAD_SLOT:(hd + 1) * HEAD_SLOT]
        dzk = _rope_bwd(dks, *_rope_tables(rope_t, 1))
        dp_cur = dpool_ref[...]
        dp_ext = jnp.concatenate([dp_cur, dnext_ref[...]], axis=0)
        r = lax.broadcasted_iota(jnp.int32, (tm, tm + POOL_HALO), 0)
        e = lax.broadcasted_iota(jnp.int32, (tm, tm + POOL_HALO), 1)
        gt_col = i * tm + lax.broadcasted_iota(jnp.int32, (1, tm + POOL_HALO), 1)
        pos_col = _seq_pos(gt_col, lp, nb)
        gt_row = i * tm + lax.broadcasted_iota(jnp.int32, (tm + POOL_HALO, 1), 0)
        pos_row = _seq_pos(gt_row, lp, nb)
        dus = []
        for g, w in enumerate(POOL_WINDOWS):
            gs = slice(g * POOL_GROUP, (g + 1) * POOL_GROUP)
            band = jnp.where((e - r >= 0) & (e - r < jnp.minimum(pos_col + 1, w)) & (gt_col < t), 1.0, 0.0).astype(BF16)
            scaled = jnp.where(gt_row < t, dp_ext[:, gs] / jnp.minimum(pos_row + 1, w).astype(F32), 0.0).astype(BF16)
            dus.append(_dot(band, scaled) - dp_cur[:, gs])
        dz = jnp.concatenate(dus + [dcq, dckv, dzk], axis=1).astype(BF16)
        dz = jnp.concatenate([dz, dga_ref[...], dgb_ref[...]], axis=1)
        dz_ref[...] = dz
        xhat, rr = _rms(h_ref[...])
        gg = g_ref[...]
        hn_ref[...] = (xhat * gg).astype(BF16)
        dx, dg = _rms_bwd(_dot(dz, wint_ref[...]), xhat, rr, gg)
        dh_ref[...] = dh1_ref[...] + dx

        @pl.when(i == 0)
        def _():
            dg_ref[...] = jnp.zeros_like(dg_ref)
            dgq_ref[...] = jnp.zeros_like(dgq_ref)
            dgkv_ref[...] = jnp.zeros_like(dgkv_ref)

        dg_ref[...] += dg
        dgq_ref[...] += dgq
        dgkv_ref[...] += dgkv

    nxt = pl.BlockSpec((POOL_HALO, POOL_WIDTH), lambda i: (jnp.minimum((i + 1) * hb, last_halo), 0))
    return pl.pallas_call(
        body, name=name, grid=(t // tm,),
        in_specs=[_rows(tm, D_MODEL), _rows(tm, D_MODEL), _whole((1, D_MODEL)), _rows(tm, Q_RANK, Z_CQ // Q_RANK),
                  _rows(tm, KV_RANK, Z_CKV // KV_RANK), _rows(tm, QK_WIDTH), _rows(tm, QK_WIDTH), _rows(tm, D_MODEL),
                  _rows(tm, D_MODEL), _rows(tm, D_MODEL), _rows(tm, POOL_WIDTH), nxt,
                  _whole((DZ, D_MODEL)), _whole((1, Q_RANK)), _whole((1, KV_RANK)), _whole((QK_WIDTH, Q_RANK)),
                  _whole((QK_WIDTH, KV_RANK)), _whole((D_MODEL, KV_RANK)), _rows(tm, 4 * LANES)],
        out_specs=[_rows(tm, D_MODEL), _rows(tm, D_MODEL), _rows(tm, DZ), _rows(tm, Q_RANK), _rows(tm, KV_RANK),
                   _rows(tm, QK_WIDTH), _rows(tm, QK_WIDTH), _rows(tm, D_MODEL),
                   _acc((1, D_MODEL)), _acc((1, Q_RANK)), _acc((1, KV_RANK))],
        out_shape=[jax.ShapeDtypeStruct((t, D_MODEL), F32), jax.ShapeDtypeStruct((t, D_MODEL), BF16),
                   jax.ShapeDtypeStruct((t, DZ), BF16), jax.ShapeDtypeStruct((t, Q_RANK), BF16),
                   jax.ShapeDtypeStruct((t, KV_RANK), BF16), jax.ShapeDtypeStruct((t, QK_WIDTH), BF16),
                   jax.ShapeDtypeStruct((t, QK_WIDTH), BF16), jax.ShapeDtypeStruct((t, D_MODEL), BF16),
                   jax.ShapeDtypeStruct((1, D_MODEL), F32), jax.ShapeDtypeStruct((1, Q_RANK), F32),
                   jax.ShapeDtypeStruct((1, KV_RANK), F32)],
        compiler_params=_cparams(),
    )(dh1, h, g_mix, z, z, dq, dk, dv, dga, dgb, dpool, dpool, wint, gq, gkv, wuqt, wukt, wuvt, rope)


_MESH = pl.DeviceIdType.MESH
_ANY = pl.BlockSpec(memory_space=pl.ANY)


def _place():
    x, y, c = lax.axis_index("x"), lax.axis_index("y"), lax.axis_index("c")
    return x, y, c, 4 * x + 2 * y + c


def _peer(x, y, c, k):
    px, py, pc = (1 - x) if k & 4 else x, (1 - y) if k & 2 else y, (1 - c) if k & 1 else c
    return (px, py, pc), 4 * px + 2 * py + pc


def _exchange_body(pairs, scatter):
    def run(send_sems, recv_sems, local_sems):
        x, y, c, me = _place()
        local = [pltpu.make_async_copy(src.at[me] if scatter else src, dst.at[me], local_sems.at[j])
                 for j, (src, dst) in enumerate(pairs)]
        for cp in local:
            cp.start()
        sends, recvs = [], []
        for k in range(1, N_DEV):
            peer, pidx = _peer(x, y, c, k)
            for j, (src, dst) in enumerate(pairs):
                mine = src.at[pidx] if scatter else src
                sems = dict(send_sem=send_sems.at[j, k - 1], recv_sem=recv_sems.at[j, k - 1], device_id=peer, device_id_type=_MESH)
                sends.append(pltpu.make_async_remote_copy(src_ref=mine, dst_ref=dst.at[me], **sems))
                recvs.append(pltpu.make_async_remote_copy(src_ref=mine, dst_ref=dst.at[pidx], **sems))
        for cp in sends:
            cp.start()
        for cp in recvs:
            cp.wait_recv()
        for cp in sends:
            cp.wait_send()
        for cp in local:
            cp.wait()
    return run


def _exchange(arrays, scatter, name):
    n = len(arrays)

    def body(*refs):
        srcs, dsts, sems = refs[:n], refs[n:2 * n], refs[2 * n:]
        _exchange_body(list(zip(srcs, dsts)), scatter)(*sems)

    shapes = [jax.ShapeDtypeStruct(a.shape if scatter else (N_DEV,) + a.shape, a.dtype) for a in arrays]
    return pl.pallas_call(
        body, name=name, in_specs=[_ANY] * n, out_specs=[_ANY] * n, out_shape=shapes,
        scratch_shapes=[pltpu.SemaphoreType.DMA((n, N_DEV - 1)), pltpu.SemaphoreType.DMA((n, N_DEV - 1)),
                        pltpu.SemaphoreType.DMA((n,))],
    )(*arrays)


def _adamw_math(w, g, m, v):
    m = ADAM_B1 * m + (1.0 - ADAM_B1) * g
    v = ADAM_B2 * v + (1.0 - ADAM_B2) * (g * g)
    m_hat = m / (1.0 - ADAM_B1 ** ADAM_STEP)
    v_hat = v / (1.0 - ADAM_B2 ** ADAM_STEP)
    return -ADAM_LR * (m_hat / (jnp.sqrt(v_hat) + ADAM_EPS) + ADAM_WD * w), m, v


def adamw(w, m, v, parts, *, tr, name):
    r = w.shape[0]

    def body(w_ref, m_ref, v_ref, p_ref, g_ref, d_ref, nm_ref, nv_ref):
        g = p_ref[0].astype(F32)
        for j in range(1, N_DEV):
            g = g + p_ref[j].astype(F32)
        g_ref[...] = g
        d_ref[...], nm_ref[...], nv_ref[...] = _adamw_math(w_ref[...], g, m_ref[...], v_ref[...])

    blk = pl.BlockSpec((tr, LANES), lambda i: (i, 0))
    return pl.pallas_call(
        body, name=name, grid=(r // tr,),
        in_specs=[blk, blk, blk, pl.BlockSpec((N_DEV, tr, LANES), lambda i: (0, i, 0))], out_specs=[blk] * 4,
        out_shape=[jax.ShapeDtypeStruct((r, LANES), F32)] * 4, compiler_params=_cparams(),
    )(w, m, v, parts)


BIG = (("w_in", 2), ("w_uq", 2), ("w_ukv", 2), ("w_pa", 2), ("w_pb", 1), ("w_o", 1), ("w_gate", 2), ("w_up", 2), ("w_down", 1))
SMALL = ("norm_mix_g", "pool_w", "pool_scale", "q_norm_g", "kv_norm_g", "norm_ffn_g", "final_norm_g")
WEIGHTS = ("meta_tokens", "norm_mix_g", "w_in", "pool_w", "pool_scale", "q_norm_g", "kv_norm_g", "w_uq", "w_ukv", "w_pa", "w_pb",
           "w_o", "norm_ffn_g", "w_gate", "w_up", "w_down", "final_norm_g")
ADAMW_ROWS = 1024


def _pack(flat_parts, row_multiple, dtype):
    flat = jnp.concatenate([p.reshape(-1).astype(dtype) for p in flat_parts])
    pad = -flat.shape[0] % (row_multiple * LANES)
    return jnp.pad(flat, (0, pad)).reshape(-1, LANES)


def _unpack(packed, shapes):
    flat, out, off = packed.reshape(-1), [], 0
    for s in shapes:
        n = 1
        for d in s:
            n *= d
        out.append(flat[off:off + n].reshape(s))
        off += n
    return out


def _gathered_to_full(g, shard_shape, axis):
    g = jnp.moveaxis(g, 0, axis)
    s = list(shard_shape)
    s[axis] *= N_DEV
    return g.reshape(s)


def _full_to_chunks(full, axis):
    s = list(full.shape)
    s[axis:axis + 1] = [N_DEV, s[axis] // N_DEV]
    return jnp.moveaxis(full.reshape(s), axis, 0).reshape(N_DEV, -1)


def _rope_table(lp, nb):
    inv = 1.0 / (ROPE_THETA ** (jnp.arange(0, QK_ROPE, 2, dtype=F32) / QK_ROPE))
    ang = jnp.arange(lp, dtype=F32)[:, None] * inv[None, :]
    cos, sin = jnp.cos(ang), jnp.sin(ang)
    z = lambda n: jnp.zeros((lp, n), F32)
    tail = LANES - QK_NOPE - QK_ROPE
    c = jnp.concatenate([jnp.ones((lp, QK_NOPE), F32), cos, cos, z(tail)], axis=1)
    cr = jnp.concatenate([z(QK_NOPE), cos, cos, z(tail)], axis=1)
    s1 = jnp.concatenate([z(QK_NOPE), -sin, z(HALF_ROPE), z(tail)], axis=1)
    s2 = jnp.concatenate([z(QK_NOPE), z(HALF_ROPE), sin, z(tail)], axis=1)
    return jnp.tile(jnp.concatenate([c, cr, s1, s2], axis=1), (nb, 1))


def _arrange(w):
    d = D_MODEL
    zc = lambda n: jnp.zeros((d, n), BF16)
    kr_end = Z_KR + QK_ROPE
    win = jnp.concatenate([w["w_in"][:, :Z_KR], zc(QK_NOPE), w["w_in"][:, Z_KR:kr_end], zc(LANES - QK_NOPE - QK_ROPE),
                           w["w_in"][:, kr_end:]], axis=1)
    wuq = jnp.pad(w["w_uq"].reshape(Q_RANK, N_HEADS, QK_NOPE + QK_ROPE), ((0, 0), (0, 0), (0, HEAD_SLOT - QK_NOPE - QK_ROPE)))
    kv = w["w_ukv"].reshape(KV_RANK, N_HEADS, QK_NOPE + V_DIM)
    wuk = jnp.pad(kv[:, :, :QK_NOPE], ((0, 0), (0, 0), (0, HEAD_SLOT - QK_NOPE)))
    out = dict(win=win, wuq=wuq.reshape(Q_RANK, QK_WIDTH), wuk=wuk.reshape(KV_RANK, QK_WIDTH),
               wuv=kv[:, :, QK_NOPE:].reshape(KV_RANK, N_HEADS * V_DIM), wpa=w["w_pa"], wpb=w["w_pb"], wo=w["w_o"],
               wg=w["w_gate"], wu=w["w_up"], wd=w["w_down"])
    out.update({k + "t": v.T for k, v in out.items()})
    return out


def _unarrange(dwin, dwuq, dwuk, dwuv):
    kr0 = Z_KR + QK_NOPE
    g_in = jnp.concatenate([dwin[:, :Z_KR], dwin[:, kr0:kr0 + QK_ROPE], dwin[:, Z_GA:]], axis=1)
    g_uq = dwuq.reshape(Q_RANK, N_HEADS, HEAD_SLOT)[:, :, :QK_NOPE + QK_ROPE].reshape(Q_RANK, -1)
    g_ukv = jnp.concatenate([dwuk.reshape(KV_RANK, N_HEADS, HEAD_SLOT)[:, :, :QK_NOPE], dwuv.reshape(KV_RANK, N_HEADS, V_DIM)],
                            axis=2).reshape(KV_RANK, -1)
    return g_in, g_uq, g_ukv


TM_FWD, TM_BWD, TM_WGRAD = 512, 256, 512


def _layer_fwd(h, p, rope, nb, lp, tag):
    z, q, k, v = in_proj_fwd(h, p["g_mix"], p["win"], p["gq"], p["gkv"], p["wuq"], p["wuk"], p["wuv"], rope, tm=TM_FWD,
                             name=f"in_proj_fwd_{tag}")
    o, lse = attn_fwd(q, k, v, nb=nb, lp=lp, name=f"attn_fwd_{tag}")
    h1, pooled, a, pa, pb, mg = merge_fwd(h, z, o, p["pw"], p["ps"], p["wpa"], p["wpb"], p["wo"], tm=TM_FWD, lp=lp, nb=nb,
                                          name=f"merge_fwd_{tag}")
    h2, gt, up = ffn_fwd(h1, p["g_ffn"], p["wg"], p["wu"], p["wd"], tm=TM_FWD, name=f"ffn_fwd_{tag}")
    return h2, dict(h=h, z=z, q=q, k=k, v=v, o=o, lse=lse, h1=h1, pooled=pooled, a=a, pa=pa, pb=pb, mg=mg, gt=gt, up=up)


def _layer_bwd(dh2, p, s, rope, nb, lp, tag):
    wg_ = lambda x, y, n: wgrad(x, y, tm=TM_WGRAD, name=f"wgrad_{n}_{tag}")
    dh1, hn2, act, dgt, dup, dg_ffn = ffn_bwd(dh2, s["h1"], p["g_ffn"], s["gt"], s["up"], p["wgt"], p["wut"], p["wdt"], tm=TM_BWD,
                                              name=f"ffn_bwd_{tag}")
    g = dict(w_gate=wg_(hn2, dgt, "gate"), w_up=wg_(hn2, dup, "up"), w_down=wg_(act, dh2, "down"), norm_ffn_g=dg_ffn[0])
    dga, dgb, dpa, dpb, do, dpool, dps, dpw = merge_bwd(dh1, s["z"], s["pa"], s["pb"], s["pooled"], p["pw"], p["pwt"], p["ps"],
                                                        p["wpat"], p["wpbt"], p["wot"], tm=TM_FWD, name=f"merge_bwd_{tag}")
    g.update(w_o=wg_(s["mg"], dh1, "o"), w_pa=wg_(s["a"], dpa, "pa"), w_pb=wg_(s["o"], dpb, "pb"), pool_scale=dps[0], pool_w=dpw)
    dq, dk, dv = attn_bwd(s["q"], s["k"], s["v"], s["o"], do, s["lse"], nb=nb, lp=lp, name=f"attn_bwd_{tag}")
    dh, hn, dz, cqn, ckvn, dqb, dkb, dvb, dg_mix, dgq, dgkv = in_proj_bwd(
        dh1, s["h"], p["g_mix"], s["z"], dq, dk, dv, dga, dgb, dpool, p["wint"], p["gq"], p["gkv"], p["wuqt"], p["wukt"], p["wuvt"],
        rope, tm=TM_BWD, lp=lp, nb=nb, name=f"in_proj_bwd_{tag}")
    g["w_in"], g["w_uq"], g["w_ukv"] = _unarrange(wg_(hn, dz, "in"), wg_(cqn, dqb, "uq"), wg_(ckvn, dkb, "uk"), wg_(ckvn, dvb, "uv"))
    g.update(norm_mix_g=dg_mix[0], q_norm_g=dgq[0], kv_norm_g=dgkv[0])
    return dh, g


def local_step(x, loss_target, meta, small, whole):
    nb, seq, d = x.shape
    lp = -(-(N_META + seq) // TQ) * TQ
    t = nb * lp
    assert t % TM_FWD == 0 and lp % TQ == 0
    rope = _rope_table(lp, nb)
    pad = jnp.zeros((nb, lp - N_META - seq, d), F32)
    h = jnp.concatenate([jnp.broadcast_to(meta[None], (nb, N_META, d)), x, pad], axis=1).reshape(t, d)
    target = jnp.concatenate([jnp.zeros((nb, N_META, d), F32), loss_target, pad], axis=1).reshape(t, d)
    params, saved = [], []
    for l in range(DEPTH):
        p = _arrange({n: whole[n][l] for n, _ in BIG})
        pw = small["pool_w"][l].astype(BF16)
        p.update(g_mix=small["norm_mix_g"][l][None], gq=small["q_norm_g"][l][None], gkv=small["kv_norm_g"][l][None],
                 g_ffn=small["norm_ffn_g"][l][None], ps=small["pool_scale"][l][None], pw=pw, pwt=jnp.swapaxes(pw, 1, 2))
        h, s = _layer_fwd(h, p, rope, nb, lp, l)
        params.append(p)
        saved.append(s)
    parts, dh, dgf = loss_head(h, small["final_norm_g"][None], target, tm=TM_FWD, lp=lp, nb=nb, seq=seq, name="loss_head")
    loss = jnp.sum(parts[::8, 0])
    layer_grads = [None] * DEPTH
    for l in reversed(range(DEPTH)):
        dh, layer_grads[l] = _layer_bwd(dh, params[l], saved[l], rope, nb, lp, l)
    dh = dh.reshape(nb, lp, d)
    grads = {n: jnp.stack([lg[n] for lg in layer_grads]) for n in layer_grads[0]}
    grads["final_norm_g"] = dgf[0]
    grads["meta_tokens"] = jnp.sum(dh[:, :N_META], axis=0)
    return loss, dh[:, N_META:N_META + seq], grads


def kernel(x, meta_tokens, norm_mix_g, w_in, pool_w, pool_scale, q_norm_g, kv_norm_g, w_uq, w_ukv, w_pa, w_pb, w_o, norm_ffn_g, w_gate, w_up, w_down, final_norm_g, loss_target, m_meta_tokens, m_norm_mix_g, m_w_in, m_pool_w, m_pool_scale, m_q_norm_g, m_kv_norm_g, m_w_uq, m_w_ukv, m_w_pa, m_w_pb, m_w_o, m_norm_ffn_g, m_w_gate, m_w_up, m_w_down, m_final_norm_g, v_meta_tokens, v_norm_mix_g, v_w_in, v_pool_w, v_pool_scale, v_q_norm_g, v_kv_norm_g, v_w_uq, v_w_ukv, v_w_pa, v_w_pb, v_w_o, v_norm_ffn_g, v_w_gate, v_w_up, v_w_down, v_final_norm_g):
    args = dict(locals())
    w = {n: args[n] for n in WEIGHTS}
    m = {n: args["m_" + n] for n in WEIGHTS}
    v = {n: args["v_" + n] for n in WEIGHTS}
    small = {n: w[n] for n in SMALL}

    g_big, g_meta = _exchange([_pack([w[n] for n, _ in BIG], 16, BF16), meta_tokens], False, "gather_weights")
    shard_shapes = [w[n].shape for n, _ in BIG]
    pieces = _unpack_rows(g_big, shard_shapes)
    whole = {n: _gathered_to_full(pc, w[n].shape, ax) for (n, ax), pc in zip(BIG, pieces)}
    meta = _gathered_to_full(g_meta, meta_tokens.shape, 1)

    loss, grad_x, grads = local_step(x, loss_target, meta, small, whole)
    loss = lax.psum(loss, ("x", "y", "c"))

    sharded = ("meta_tokens",) + tuple(n for n, _ in BIG)
    chunks = jnp.concatenate([_full_to_chunks(grads["meta_tokens"], 1)] + [_full_to_chunks(grads[n], ax) for n, ax in BIG], axis=1)
    n_sharded = chunks.shape[1]
    rows = -(-n_sharded // (ADAMW_ROWS * LANES)) * ADAMW_ROWS
    send = jnp.pad(chunks, ((0, 0), (0, rows * LANES - n_sharded))).astype(BF16).reshape(N_DEV, rows, LANES)
    small_send = _pack([grads[n] for n in SMALL], 8, F32)
    (recv,) = _exchange([send], True, "scatter_grads")
    (small_recv,) = _exchange([small_send], False, "gather_small_grads")

    pk = lambda d, names, mult: _pack([d[n] for n in names], mult, F32)
    out_sh = adamw(pk(w, sharded, ADAMW_ROWS), pk(m, sharded, ADAMW_ROWS), pk(v, sharded, ADAMW_ROWS), recv, tr=ADAMW_ROWS,
                   name="adamw_shards")
    srows = small_send.shape[0]
    out_sm = adamw(pk(w, SMALL, 8), pk(m, SMALL, 8), pk(v, SMALL, 8), small_recv, tr=srows, name="adamw_small")
    results = []
    for kind in range(4):
        got = dict(zip(sharded, _unpack(out_sh[kind], [w[n].shape for n in sharded])))
        got.update(zip(SMALL, _unpack(out_sm[kind], [w[n].shape for n in SMALL])))
        results.extend(got[n] for n in WEIGHTS)
    return (loss, grad_x, *results)


def _unpack_rows(gathered, shapes):
    flat, out, off = gathered.reshape(N_DEV, -1), [], 0
    for s in shapes:
        n = 1
        for d in s:
            n *= d
        out.append(flat[:, off:off + n].reshape((N_DEV,) + tuple(s)))
        off += n
    return out
```

```python
import functools

import jax
import jax.numpy as jnp
from jax import lax
from jax.experimental import pallas as pl
from jax.experimental.pallas import tpu as pltpu

F32, BF16 = jnp.float32, jnp.bfloat16

D_MODEL = 1024
N_META = 16
N_HEADS = 16
QK_NOPE, QK_ROPE, V_DIM = 64, 32, 64
HALF_ROPE = QK_ROPE // 2
Q_RANK, KV_RANK = 256, 128
POOL_WINDOWS = (2, 4, 8, 16)
POOL_GROUP = 128
POOL_WIDTH = POOL_GROUP * len(POOL_WINDOWS)
POOL_HALO = 16
D_FF = 2816
D_IN = 2976
NORM_EPS = 1e-6
SM_SCALE = (QK_NOPE + QK_ROPE) ** -0.5
MASK_VALUE = -1e30
ROPE_THETA = 10000.0
DEPTH = 2
N_DEV = 8

ADAM_LR, ADAM_B1, ADAM_B2, ADAM_EPS, ADAM_WD, ADAM_STEP = 0.001, 0.9, 0.999, 1e-08, 0.01, 10

LANES = 128
HEAD_SLOT = LANES
QK_WIDTH = N_HEADS * HEAD_SLOT
Z_CQ, Z_CKV, Z_KR, Z_GA, Z_GB, DZ = 512, 768, 896, 1024, 2048, 3072
TQ = TK = 256
VMEM_LIMIT = 56 * 1024 * 1024


def _cparams():
    return pltpu.CompilerParams(vmem_limit_bytes=VMEM_LIMIT)


def _rows(tm, width, col=0):
    return pl.BlockSpec((tm, width), lambda i: (i, col))


def _whole(shape):
    zeros = (0,) * len(shape)
    return pl.BlockSpec(shape, lambda i: zeros, pipeline_mode=pl.Buffered(1))


def _acc(shape):
    zeros = (0,) * len(shape)
    return pl.BlockSpec(shape, lambda i: zeros)


def _dot(a, b):
    return jnp.dot(a, b, preferred_element_type=F32)


def _dot_tn(a, b):
    return lax.dot_general(a, b, (((0,), (0,)), ((), ())), preferred_element_type=F32)


def _dot_nt(a, b):
    return lax.dot_general(a, b, (((1,), (1,)), ((), ())), preferred_element_type=F32)


def _rms(x):
    r = lax.rsqrt(jnp.mean(x * x, axis=-1, keepdims=True) + NORM_EPS)
    return x * r, r


def _rms_bwd(dy, xhat, r, g):
    dg = jnp.sum(dy * xhat, axis=0, keepdims=True)
    dxh = dy * g
    dx = r * (dxh - xhat * jnp.mean(dxh * xhat, axis=-1, keepdims=True))
    return dx, dg


def _sigmoid(x):
    return 1.0 / (1.0 + jnp.exp(-x))


def _rope_fwd(q, c, s1, s2):
    w = q.shape[1]
    return q * c + pltpu.roll(q, w - HALF_ROPE, 1) * s1 + pltpu.roll(q, HALF_ROPE, 1) * s2


def _rope_bwd(dq, c, s1, s2):
    w = dq.shape[1]
    return dq * c + pltpu.roll(dq * s1, HALF_ROPE, 1) + pltpu.roll(dq * s2, w - HALF_ROPE, 1)


def _rope_tables(rope, reps):
    c, cr, s1, s2 = (rope[:, k * LANES:(k + 1) * LANES] for k in range(4))
    if reps > 1:
        return jnp.tile(c, (1, reps)), jnp.tile(s1, (1, reps)), jnp.tile(s2, (1, reps))
    return cr, s1, s2


def _seq_pos(gi, lp, nb):
    pos = gi
    for b in range(1, nb):
        pos = jnp.where(gi >= b * lp, gi - b * lp, pos)
    return pos


def in_proj_fwd(h, g_mix, win, gq, gkv, wuq, wuk, wuv, rope, *, tm, name):
    t = h.shape[0]

    def body(h_ref, g_ref, win_ref, gq_ref, gkv_ref, wuq_ref, wuk_ref, wuv_ref, rope_ref, z_ref, q_ref, k_ref, v_ref):
        xhat, _ = _rms(h_ref[...])
        hn = (xhat * g_ref[...]).astype(BF16)
        z = _dot(hn, win_ref[...])
        z_ref[...] = z
        rope_t = rope_ref[...]
        xq, _ = _rms(z[:, Z_CQ:Z_CKV])
        cqn = (xq * gq_ref[...]).astype(BF16)
        q = _rope_fwd(_dot(cqn, wuq_ref[...]), *_rope_tables(rope_t, N_HEADS))
        q_ref[...] = q.astype(BF16)
        xkv, _ = _rms(z[:, Z_CKV:Z_KR])
        ckvn = (xkv * gkv_ref[...]).astype(BF16)
        kr = _rope_fwd(z[:, Z_KR:Z_GA], *_rope_tables(rope_t, 1))
        k_ref[...] = (_dot(ckvn, wuk_ref[...]) + jnp.tile(kr, (1, N_HEADS))).astype(BF16)
        v_ref[...] = _dot(ckvn, wuv_ref[...]).astype(BF16)

    return pl.pallas_call(
        body, name=name, grid=(t // tm,),
        in_specs=[_rows(tm, D_MODEL), _whole((1, D_MODEL)), _whole((D_MODEL, DZ)), _whole((1, Q_RANK)), _whole((1, KV_RANK)),
                  _whole((Q_RANK, QK_WIDTH)), _whole((KV_RANK, QK_WIDTH)), _whole((KV_RANK, D_MODEL)), _rows(tm, 4 * LANES)],
        out_specs=[_rows(tm, DZ), _rows(tm, QK_WIDTH), _rows(tm, QK_WIDTH), _rows(tm, D_MODEL)],
        out_shape=[jax.ShapeDtypeStruct((t, DZ), F32), jax.ShapeDtypeStruct((t, QK_WIDTH), BF16),
                   jax.ShapeDtypeStruct((t, QK_WIDTH), BF16), jax.ShapeDtypeStruct((t, D_MODEL), BF16)],
        compiler_params=_cparams(),
    )(h, g_mix, win, gq, gkv, wuq, wuk, wuv, rope)


def attn_fwd(q, k, v, *, nb, lp, name):
    t = q.shape[0]
    nq = lp // TQ

    def body(q_ref, k_ref, v_ref, o_ref, lse_ref):
        lane = lax.broadcasted_iota(jnp.int32, (TQ, LANES), 1)
        causal = lax.broadcasted_iota(jnp.int32, (TQ, TK), 1) <= lax.broadcasted_iota(jnp.int32, (TQ, TK), 0)

        def q_block(qi, carry):
            qs = pl.multiple_of(qi * TQ, TQ)
            outs, lses = [], []
            for hh in range(2):
                hs = slice(hh * HEAD_SLOT, (hh + 1) * HEAD_SLOT)
                qh = q_ref[pl.ds(qs, TQ), hs]

                def k_step(kj, c, masked, hs=hs, qh=qh):
                    m, l, acc = c
                    ks = pl.multiple_of(kj * TK, TK)
                    s = _dot_nt(qh, k_ref[pl.ds(ks, TK), hs]) * SM_SCALE
                    if masked:
                        s = jnp.where(causal, s, MASK_VALUE)
                    m_new = jnp.maximum(m, jnp.max(s, axis=1, keepdims=True))
                    p = jnp.exp(s - m_new)
                    alpha = jnp.exp(m - m_new)
                    l = alpha * l + jnp.sum(p, axis=1, keepdims=True)
                    acc = alpha * acc + _dot(p.astype(BF16), v_ref[pl.ds(ks, TK), :])
                    return m_new, l, acc

                init = (jnp.full((TQ, 1), MASK_VALUE, F32), jnp.zeros((TQ, 1), F32), jnp.zeros((TQ, LANES), F32))
                c = lax.fori_loop(0, qi, functools.partial(k_step, masked=False), init)
                m, l, acc = k_step(qi, c, True)
                outs.append(acc / l)
                lses.append(m + jnp.log(l))
            o_ref[pl.ds(qs, TQ), :] = jnp.where(lane < V_DIM, outs[0], outs[1]).astype(BF16)
            lse_ref[pl.ds(qs, TQ), :] = jnp.where(lane < V_DIM, lses[0], lses[1])
            return carry

        lax.fori_loop(0, nq, q_block, 0)

    pair = lambda w: pl.BlockSpec((lp, w), lambda b, hp: (b, hp))
    return pl.pallas_call(
        body, name=name, grid=(nb, N_HEADS // 2),
        in_specs=[pair(2 * HEAD_SLOT), pair(2 * HEAD_SLOT), pair(2 * V_DIM)],
        out_specs=[pair(2 * V_DIM), pair(2 * V_DIM)],
        out_shape=[jax.ShapeDtypeStruct((t, D_MODEL), BF16), jax.ShapeDtypeStruct((t, D_MODEL), F32)],
        compiler_params=_cparams(),
    )(q, k, v)


def _pool_band_fwd(i, tm, lp, nb):
    r = lax.broadcasted_iota(jnp.int32, (tm, POOL_HALO + tm), 0)
    e = lax.broadcasted_iota(jnp.int32, (tm, POOL_HALO + tm), 1)
    diff = r + POOL_HALO - e
    pos = _seq_pos(i * tm + lax.broadcasted_iota(jnp.int32, (tm, 1), 0), lp, nb)
    out = []
    for w in POOL_WINDOWS:
        cnt = jnp.minimum(pos + 1, w)
        band = jnp.where((diff >= 0) & (diff < cnt), 1.0, 0.0).astype(BF16)
        out.append((band, cnt.astype(F32)))
    return out


def merge_fwd(h, z, o, pw, ps, wpa, wpb, wo, *, tm, lp, nb, name):
    t = h.shape[0]
    hb = tm // POOL_HALO

    def body(h_ref, u_ref, uprev_ref, ga_ref, gb_ref, o_ref, pw_ref, ps_ref, wpa_ref, wpb_ref, wo_ref,
             h1_ref, pooled_ref, a_ref, pa_ref, pb_ref, mg_ref):
        i = pl.program_id(0)
        u = u_ref[...]
        uext = jnp.concatenate([uprev_ref[...], u], axis=0).astype(BF16)
        pooled, ys = [], []
        for g, (band, cnt) in enumerate(_pool_band_fwd(i, tm, lp, nb)):
            gs = slice(g * POOL_GROUP, (g + 1) * POOL_GROUP)
            pg = (_dot(band, uext[:, gs]) / cnt - u[:, gs]).astype(BF16)
            pooled.append(pg)
            ys.append(_dot(pg, pw_ref[g]))
        pooled_ref[...] = jnp.concatenate(pooled, axis=1)
        a = (jnp.concatenate(ys, axis=1) * ps_ref[...]).astype(BF16)
        a_ref[...] = a
        pa = _dot(a, wpa_ref[...])
        pb = _dot(o_ref[...], wpb_ref[...])
        pa_ref[...] = pa.astype(BF16)
        pb_ref[...] = pb.astype(BF16)
        mg = (_sigmoid(ga_ref[...]) * pa + _sigmoid(gb_ref[...]) * pb).astype(BF16)
        mg_ref[...] = mg
        h1_ref[...] = h_ref[...] + _dot(mg, wo_ref[...])

    halo = pl.BlockSpec((POOL_HALO, POOL_WIDTH), lambda i: (jnp.maximum(i * hb - 1, 0), 0))
    return pl.pallas_call(
        body, name=name, grid=(t // tm,),
        in_specs=[_rows(tm, D_MODEL), _rows(tm, POOL_WIDTH), halo, _rows(tm, D_MODEL, 1), _rows(tm, D_MODEL, 2), _rows(tm, D_MODEL),
                  _whole((4, POOL_GROUP, POOL_GROUP)), _whole((1, POOL_WIDTH)), _whole((POOL_WIDTH, D_MODEL)),
                  _whole((D_MODEL, D_MODEL)), _whole((D_MODEL, D_MODEL))],
        out_specs=[_rows(tm, D_MODEL), _rows(tm, POOL_WIDTH), _rows(tm, POOL_WIDTH), _rows(tm, D_MODEL), _rows(tm, D_MODEL),
                   _rows(tm, D_MODEL)],
        out_shape=[jax.ShapeDtypeStruct((t, D_MODEL), F32), jax.ShapeDtypeStruct((t, POOL_WIDTH), BF16),
                   jax.ShapeDtypeStruct((t, POOL_WIDTH), BF16), jax.ShapeDtypeStruct((t, D_MODEL), BF16),
                   jax.ShapeDtypeStruct((t, D_MODEL), BF16), jax.ShapeDtypeStruct((t, D_MODEL), BF16)],
        compiler_params=_cparams(),
    )(h, z, z, z, z, o, pw, ps, wpa, wpb, wo)


def ffn_fwd(h1, g, wg, wu, wd, *, tm, name):
    t = h1.shape[0]

    def body(h_ref, g_ref, wg_ref, wu_ref, wd_ref, h2_ref, gt_ref, up_ref):
        h = h_ref[...]
        xhat, _ = _rms(h)
        hn = (xhat * g_ref[...]).astype(BF16)
        gt = _dot(hn, wg_ref[...])
        up = _dot(hn, wu_ref[...])
        gt_ref[...] = gt.astype(BF16)
        up_ref[...] = up.astype(BF16)
        act = (gt * _sigmoid(gt) * up).astype(BF16)
        h2_ref[...] = h + _dot(act, wd_ref[...])

    return pl.pallas_call(
        body, name=name, grid=(t // tm,),
        in_specs=[_rows(tm, D_MODEL), _whole((1, D_MODEL)), _whole((D_MODEL, D_FF)), _whole((D_MODEL, D_FF)), _whole((D_FF, D_MODEL))],
        out_specs=[_rows(tm, D_MODEL), _rows(tm, D_FF), _rows(tm, D_FF)],
        out_shape=[jax.ShapeDtypeStruct((t, D_MODEL), F32), jax.ShapeDtypeStruct((t, D_FF), BF16), jax.ShapeDtypeStruct((t, D_FF), BF16)],
        compiler_params=_cparams(),
    )(h1, g, wg, wu, wd)


def loss_head(h, g, target, *, tm, lp, nb, seq, name):
    t = h.shape[0]
    nt = t // tm

    def body(h_ref, g_ref, t_ref, loss_ref, dh_ref, dg_ref):
        i = pl.program_id(0)
        pos = _seq_pos(i * tm + lax.broadcasted_iota(jnp.int32, (tm, 1), 0), lp, nb)
        real = (pos >= N_META) & (pos < N_META + seq)
        xhat, r = _rms(h_ref[...])
        gg = g_ref[...]
        err = jnp.where(real, xhat * gg - t_ref[...], 0.0)
        loss_ref[...] = jnp.full((8, LANES), 0.5 * jnp.sum(err * err) / D_MODEL, F32)
        dx, dg = _rms_bwd(err * (1.0 / D_MODEL), xhat, r, gg)
        dh_ref[...] = dx

        @pl.when(i == 0)
        def _():
            dg_ref[...] = jnp.zeros_like(dg_ref)

        dg_ref[...] += dg

    return pl.pallas_call(
        body, name=name, grid=(nt,),
        in_specs=[_rows(tm, D_MODEL), _whole((1, D_MODEL)), _rows(tm, D_MODEL)],
        out_specs=[pl.BlockSpec((8, LANES), lambda i: (i, 0)), _rows(tm, D_MODEL), _acc((1, D_MODEL))],
        out_shape=[jax.ShapeDtypeStruct((nt * 8, LANES), F32), jax.ShapeDtypeStruct((t, D_MODEL), F32),
                   jax.ShapeDtypeStruct((1, D_MODEL), F32)],
        compiler_params=_cparams(),
    )(h, g, target)


def wgrad(x, ys, chunk_fn, chunk_shape, *, tm, name):
    t, m = x.shape

    def body(x_ref, *refs):
        y_refs, o_ref, accs = refs[:len(ys)], refs[len(ys)], refs[len(ys) + 1:]
        i = pl.program_id(0)

        @pl.when(i == 0)
        def _():
            for acc in accs:
                acc[...] = jnp.zeros_like(acc)

        xb = x_ref[...].astype(BF16)
        for y_ref, acc in zip(y_refs, accs):
            acc[...] += _dot_tn(xb, y_ref[...].astype(BF16))

        @pl.when(i == t // tm - 1)
        def _():
            for p, chunk in enumerate(chunk_fn(*accs)):
                o_ref[p] = chunk.astype(BF16)

    out = (N_DEV,) + tuple(chunk_shape)
    return pl.pallas_call(
        body, name=name, grid=(t // tm,),
        in_specs=[_rows(tm, m)] + [_rows(tm, y.shape[1]) for y in ys], out_specs=_acc(out),
        out_shape=jax.ShapeDtypeStruct(out, BF16), scratch_shapes=[pltpu.VMEM((m, y.shape[1]), F32) for y in ys],
        compiler_params=_cparams(),
    )(x, *ys)


def ffn_bwd(dh2, h1, g, gt, up, wgt, wut, wdt, *, tm, name):
    t = h1.shape[0]

    def body(dh2_ref, h_ref, g_ref, gt_ref, up_ref, wgt_ref, wut_ref, wdt_ref, dh1_ref, hn_ref, act_ref, dgt_ref, dup_ref, dg_ref):
        dh2 = dh2_ref[...]
        dact = _dot(dh2.astype(BF16), wdt_ref[...])
        gt = gt_ref[...].astype(F32)
        up = up_ref[...].astype(F32)
        sg = _sigmoid(gt)
        silu = gt * sg
        act_ref[...] = (silu * up).astype(BF16)
        dgt = (dact * up * (sg * (1.0 + gt * (1.0 - sg)))).astype(BF16)
        dup = (dact * silu).astype(BF16)
        dgt_ref[...] = dgt
        dup_ref[...] = dup
        dhn = _dot(dgt, wgt_ref[...]) + _dot(dup, wut_ref[...])
        xhat, r = _rms(h_ref[...])
        gg = g_ref[...]
        hn_ref[...] = (xhat * gg).astype(BF16)
        dx, dg = _rms_bwd(dhn, xhat, r, gg)
        dh1_ref[...] = dh2 + dx

        @pl.when(pl.program_id(0) == 0)
        def _():
            dg_ref[...] = jnp.zeros_like(dg_ref)

        dg_ref[...] += dg

    return pl.pallas_call(
        body, name=name, grid=(t // tm,),
        in_specs=[_rows(tm, D_MODEL), _rows(tm, D_MODEL), _whole((1, D_MODEL)), _rows(tm, D_FF), _rows(tm, D_FF),
                  _whole((D_FF, D_MODEL)), _whole((D_FF, D_MODEL)), _whole((D_MODEL, D_FF))],
        out_specs=[_rows(tm, D_MODEL), _rows(tm, D_MODEL), _rows(tm, D_FF), _rows(tm, D_FF), _rows(tm, D_FF), _acc((1, D_MODEL))],
        out_shape=[jax.ShapeDtypeStruct((t, D_MODEL), F32), jax.ShapeDtypeStruct((t, D_MODEL), BF16),
                   jax.ShapeDtypeStruct((t, D_FF), BF16), jax.ShapeDtypeStruct((t, D_FF), BF16),
                   jax.ShapeDtypeStruct((t, D_FF), BF16), jax.ShapeDtypeStruct((1, D_MODEL), F32)],
        compiler_params=_cparams(),
    )(dh2, h1, g, gt, up, wgt, wut, wdt)


def merge_bwd(dh1, z, pa, pb, pooled, pw, pwt, ps, wpat, wpbt, wot, *, tm, name):
    t = dh1.shape[0]

    def body(dh1_ref, ga_ref, gb_ref, pa_ref, pb_ref, pooled_ref, pw_ref, pwt_ref, ps_ref, wpat_ref, wpbt_ref, wot_ref,
             dga_ref, dgb_ref, dpa_ref, dpb_ref, do_ref, dpool_ref, dps_ref, dpw_ref):
        dmg = _dot(dh1_ref[...].astype(BF16), wot_ref[...])
        sa = _sigmoid(ga_ref[...])
        sb = _sigmoid(gb_ref[...])
        dga_ref[...] = (dmg * pa_ref[...].astype(F32) * sa * (1.0 - sa)).astype(BF16)
        dgb_ref[...] = (dmg * pb_ref[...].astype(F32) * sb * (1.0 - sb)).astype(BF16)
        dpa = (dmg * sa).astype(BF16)
        dpb = (dmg * sb).astype(BF16)
        dpa_ref[...] = dpa
        dpb_ref[...] = dpb
        do_ref[...] = _dot(dpb, wpbt_ref[...]).astype(BF16)
        da = _dot(dpa, wpat_ref[...])
        pooled = pooled_ref[...]
        ps = ps_ref[...]

        @pl.when(pl.program_id(0) == 0)
        def _():
            dps_ref[...] = jnp.zeros_like(dps_ref)
            dpw_ref[...] = jnp.zeros_like(dpw_ref)

        dps, dpool = [], []
        for g in range(len(POOL_WINDOWS)):
            gs = slice(g * POOL_GROUP, (g + 1) * POOL_GROUP)
            y = _dot(pooled[:, gs], pw_ref[g])
            dps.append(jnp.sum(da[:, gs] * y, axis=0, keepdims=True))
            dy = (da[:, gs] * ps[:, gs]).astype(BF16)
            dpool.append(_dot(dy, pwt_ref[g]))
            dpw_ref[g] += _dot_tn(pooled[:, gs], dy)
        dps_ref[...] += jnp.concatenate(dps, axis=1)
        dpool_ref[...] = jnp.concatenate(dpool, axis=1)

    return pl.pallas_call(
        body, name=name, grid=(t // tm,),
        in_specs=[_rows(tm, D_MODEL), _rows(tm, D_MODEL, 1), _rows(tm, D_MODEL, 2), _rows(tm, D_MODEL), _rows(tm, D_MODEL),
                  _rows(tm, POOL_WIDTH), _whole((4, POOL_GROUP, POOL_GROUP)), _whole((4, POOL_GROUP, POOL_GROUP)),
                  _whole((1, POOL_WIDTH)), _whole((D_MODEL, POOL_WIDTH)), _whole((D_MODEL, D_MODEL)), _whole((D_MODEL, D_MODEL))],
        out_specs=[_rows(tm, D_MODEL), _rows(tm, D_MODEL), _rows(tm, D_MODEL), _rows(tm, D_MODEL), _rows(tm, D_MODEL),
                   _rows(tm, POOL_WIDTH), _acc((1, POOL_WIDTH)), _acc((4, POOL_GROUP, POOL_GROUP))],
        out_shape=[jax.ShapeDtypeStruct((t, D_MODEL), BF16)] * 5
        + [jax.ShapeDtypeStruct((t, POOL_WIDTH), F32), jax.ShapeDtypeStruct((1, POOL_WIDTH), F32),
           jax.ShapeDtypeStruct((4, POOL_GROUP, POOL_GROUP), F32)],
        compiler_params=_cparams(),
    )(dh1, z, z, pa, pb, pooled, pw, pwt, ps, wpat, wpbt, wot)


def attn_bwd(q, k, v, o, do, lse, *, nb, lp, name):
    t = q.shape[0]
    nq = lp // TQ

    def body(q_ref, k_ref, v_ref, o_ref, do_ref, lse_ref, dq_ref, dk_ref, dv_ref, doh, delta):
        lane = lax.broadcasted_iota(jnp.int32, (lp, LANES), 1)
        first = lane < V_DIM
        do = do_ref[...]
        prod = do.astype(F32) * o_ref[...].astype(F32)
        d0 = jnp.sum(jnp.where(first, prod, 0.0), axis=1, keepdims=True)
        d1 = jnp.sum(jnp.where(first, 0.0, prod), axis=1, keepdims=True)
        delta[...] = jnp.where(first, d0, d1)
        doh[0] = jnp.where(first, do, jnp.zeros_like(do))
        doh[1] = jnp.where(first, jnp.zeros_like(do), do)
        dq_ref[...] = jnp.zeros_like(dq_ref)
        causal = lax.broadcasted_iota(jnp.int32, (TQ, TK), 1) <= lax.broadcasted_iota(jnp.int32, (TQ, TK), 0)

        def k_block(kj, carry):
            ks = pl.multiple_of(kj * TK, TK)
            vb = v_ref[pl.ds(ks, TK), :]
            dvs = []
            for hh in range(2):
                hs = slice(hh * HEAD_SLOT, (hh + 1) * HEAD_SLOT)
                one = slice(hh * V_DIM, hh * V_DIM + 1)
                kb = k_ref[pl.ds(ks, TK), hs]

                def q_step(qi, c, masked, hh=hh, hs=hs, one=one, kb=kb):
                    dk, dv = c
                    qs = pl.multiple_of(qi * TQ, TQ)
                    qh = q_ref[pl.ds(qs, TQ), hs]
                    s = _dot_nt(qh, kb) * SM_SCALE
                    if masked:
                        s = jnp.where(causal, s, MASK_VALUE)
                    p = jnp.exp(s - lse_ref[pl.ds(qs, TQ), one])
                    do_h = doh[hh, pl.ds(qs, TQ), :]
                    dp = _dot_nt(do_h, vb)
                    ds = (p * (dp - delta[pl.ds(qs, TQ), one]) * SM_SCALE).astype(BF16)
                    dv = dv + _dot_tn(p.astype(BF16), do_h)
                    dk = dk + _dot_tn(ds, qh)
                    dq_ref[pl.ds(qs, TQ), hs] += _dot(ds, kb)
                    return dk, dv

                zero = jnp.zeros((TK, LANES), F32)
                c = q_step(kj, (zero, zero), True)
                dk, dv = lax.fori_loop(kj + 1, nq, functools.partial(q_step, masked=False), c)
                dk_ref[pl.ds(ks, TK), hs] = dk
                dvs.append(dv)
            dv_ref[pl.ds(ks, TK), :] = dvs[0] + dvs[1]
            return carry

        lax.fori_loop(0, nq, k_block, 0)

    pair = lambda w: pl.BlockSpec((lp, w), lambda b, hp: (b, hp))
    return pl.pallas_call(
        body, name=name, grid=(nb, N_HEADS // 2),
        in_specs=[pair(2 * HEAD_SLOT), pair(2 * HEAD_SLOT), pair(2 * V_DIM), pair(2 * V_DIM), pair(2 * V_DIM), pair(2 * V_DIM)],
        out_specs=[pair(2 * HEAD_SLOT), pair(2 * HEAD_SLOT), pair(2 * V_DIM)],
        out_shape=[jax.ShapeDtypeStruct((t, QK_WIDTH), F32), jax.ShapeDtypeStruct((t, QK_WIDTH), F32),
                   jax.ShapeDtypeStruct((t, D_MODEL), F32)],
        scratch_shapes=[pltpu.VMEM((2, lp, LANES), BF16), pltpu.VMEM((lp, LANES), F32)],
        compiler_params=_cparams(),
    )(q, k, v, o, do, lse)


def in_proj_bwd(dh1, h, g_mix, z, dq, dk, dv, dga, dgb, dpool, wint, gq, gkv, wuqt, wukt, wuvt, rope, *, tm, lp, nb, name):
    t = h.shape[0]
    hb = tm // POOL_HALO
    last_halo = t // POOL_HALO - 1

    def body(dh1_ref, h_ref, g_ref, zcq_ref, zckv_ref, dq_ref, dk_ref, dv_ref, dga_ref, dgb_ref, dpool_ref, dnext_ref,
             wint_ref, gq_ref, gkv_ref, wuqt_ref, wukt_ref, wuvt_ref, rope_ref,
             dh_ref, hn_ref, dz_ref, cqn_ref, ckvn_ref, dqb_ref, dkb_ref, dvb_ref, dg_ref, dgq_ref, dgkv_ref):
        i = pl.program_id(0)
        rope_t = rope_ref[...]
        dqb = _rope_bwd(dq_ref[...], *_rope_tables(rope_t, N_HEADS)).astype(BF16)
        dqb_ref[...] = dqb
        xq, rq = _rms(zcq_ref[...])
        gq_v = gq_ref[...]
        cqn_ref[...] = (xq * gq_v).astype(BF16)
        dcq, dgq = _rms_bwd(_dot(dqb, wuqt_ref[...]), xq, rq, gq_v)
        dk = dk_ref[...]
        dkb = dk.astype(BF16)
        dvb = dv_ref[...].astype(BF16)
        dkb_ref[...] = dkb
        dvb_ref[...] = dvb
        xkv, rkv = _rms(zckv_ref[...])
        gkv_v = gkv_ref[...]
        ckvn_ref[...] = (xkv * gkv_v).astype(BF16)
        dckv, dgkv = _rms_bwd(_dot(dkb, wukt_ref[...]) + _dot(dvb, wuvt_ref[...]), xkv, rkv, gkv_v)
        dks = dk[:, :HEAD_SLOT]
        for hd in range(1, N_HEADS):
            dks = dks + dk[:, hd * HEAD_SLOT:(hd + 1) * HEAD_SLOT]
        dzk = _rope_bwd(dks, *_rope_tables(rope_t, 1))
        dp_cur = dpool_ref[...]
        dp_ext = jnp.concatenate([dp_cur, dnext_ref[...]], axis=0)
        r = lax.broadcasted_iota(jnp.int32, (tm, tm + POOL_HALO), 0)
        e = lax.broadcasted_iota(jnp.int32, (tm, tm + POOL_HALO), 1)
        gt_col = i * tm + lax.broadcasted_iota(jnp.int32, (1, tm + POOL_HALO), 1)
        pos_col = _seq_pos(gt_col, lp, nb)
        gt_row = i * tm + lax.broadcasted_iota(jnp.int32, (tm + POOL_HALO, 1), 0)
        pos_row = _seq_pos(gt_row, lp, nb)
        dus = []
        for g, w in enumerate(POOL_WINDOWS):
            gs = slice(g * POOL_GROUP, (g + 1) * POOL_GROUP)
            band = jnp.where((e - r >= 0) & (e - r < jnp.minimum(pos_col + 1, w)) & (gt_col < t), 1.0, 0.0).astype(BF16)
            scaled = jnp.where(gt_row < t, dp_ext[:, gs] / jnp.minimum(pos_row + 1, w).astype(F32), 0.0).astype(BF16)
            dus.append(_dot(band, scaled) - dp_cur[:, gs])
        dz = jnp.concatenate(dus + [dcq, dckv, dzk], axis=1).astype(BF16)
        dz = jnp.concatenate([dz, dga_ref[...], dgb_ref[...]], axis=1)
        dz_ref[...] = dz
        xhat, rr = _rms(h_ref[...])
        gg = g_ref[...]
        hn_ref[...] = (xhat * gg).astype(BF16)
        dx, dg = _rms_bwd(_dot(dz, wint_ref[...]), xhat, rr, gg)
        dh_ref[...] = dh1_ref[...] + dx

        @pl.when(i == 0)
        def _():
            dg_ref[...] = jnp.zeros_like(dg_ref)
            dgq_ref[...] = jnp.zeros_like(dgq_ref)
            dgkv_ref[...] = jnp.zeros_like(dgkv_ref)

        dg_ref[...] += dg
        dgq_ref[...] += dgq
        dgkv_ref[...] += dgkv

    nxt = pl.BlockSpec((POOL_HALO, POOL_WIDTH), lambda i: (jnp.minimum((i + 1) * hb, last_halo), 0))
    return pl.pallas_call(
        body, name=name, grid=(t // tm,),
        in_specs=[_rows(tm, D_MODEL), _rows(tm, D_MODEL), _whole((1, D_MODEL)), _rows(tm, Q_RANK, Z_CQ // Q_RANK),
                  _rows(tm, KV_RANK, Z_CKV // KV_RANK), _rows(tm, QK_WIDTH), _rows(tm, QK_WIDTH), _rows(tm, D_MODEL),
                  _rows(tm, D_MODEL), _rows(tm, D_MODEL), _rows(tm, POOL_WIDTH), nxt,
                  _whole((DZ, D_MODEL)), _whole((1, Q_RANK)), _whole((1, KV_RANK)), _whole((QK_WIDTH, Q_RANK)),
                  _whole((QK_WIDTH, KV_RANK)), _whole((D_MODEL, KV_RANK)), _rows(tm, 4 * LANES)],
        out_specs=[_rows(tm, D_MODEL), _rows(tm, D_MODEL), _rows(tm, DZ), _rows(tm, Q_RANK), _rows(tm, KV_RANK),
                   _rows(tm, QK_WIDTH), _rows(tm, QK_WIDTH), _rows(tm, D_MODEL),
                   _acc((1, D_MODEL)), _acc((1, Q_RANK)), _acc((1, KV_RANK))],
        out_shape=[jax.ShapeDtypeStruct((t, D_MODEL), F32), jax.ShapeDtypeStruct((t, D_MODEL), BF16),
                   jax.ShapeDtypeStruct((t, DZ), BF16), jax.ShapeDtypeStruct((t, Q_RANK), BF16),
                   jax.ShapeDtypeStruct((t, KV_RANK), BF16), jax.ShapeDtypeStruct((t, QK_WIDTH), BF16),
                   jax.ShapeDtypeStruct((t, QK_WIDTH), BF16), jax.ShapeDtypeStruct((t, D_MODEL), BF16),
                   jax.ShapeDtypeStruct((1, D_MODEL), F32), jax.ShapeDtypeStruct((1, Q_RANK), F32),
                   jax.ShapeDtypeStruct((1, KV_RANK), F32)],
        compiler_params=_cparams(),
    )(dh1, h, g_mix, z, z, dq, dk, dv, dga, dgb, dpool, dpool, wint, gq, gkv, wuqt, wukt, wuvt, rope)


_MESH = pl.DeviceIdType.MESH
_ANY = pl.BlockSpec(memory_space=pl.ANY)


def _place():
    x, y, c = lax.axis_index("x"), lax.axis_index("y"), lax.axis_index("c")
    return x, y, c, 4 * x + 2 * y + c


def _peer(x, y, c, k):
    px, py, pc = (1 - x) if k & 4 else x, (1 - y) if k & 2 else y, (1 - c) if k & 1 else c
    return (px, py, pc), 4 * px + 2 * py + pc


def _exchange_body(pairs, scatter):
    def run(send_sems, recv_sems, local_sems):
        x, y, c, me = _place()
        local = [pltpu.make_async_copy(src.at[me] if scatter else src, dst.at[me], local_sems.at[j])
                 for j, (src, dst) in enumerate(pairs)]
        for cp in local:
            cp.start()
        sends, recvs = [], []
        for k in range(1, N_DEV):
            peer, pidx = _peer(x, y, c, k)
            for j, (src, dst) in enumerate(pairs):
                mine = src.at[pidx] if scatter else src
                sems = dict(send_sem=send_sems.at[j, k - 1], recv_sem=recv_sems.at[j, k - 1], device_id=peer, device_id_type=_MESH)
                sends.append(pltpu.make_async_remote_copy(src_ref=mine, dst_ref=dst.at[me], **sems))
                recvs.append(pltpu.make_async_remote_copy(src_ref=mine, dst_ref=dst.at[pidx], **sems))
        for cp in sends:
            cp.start()
        for cp in recvs:
            cp.wait_recv()
        for cp in sends:
            cp.wait_send()
        for cp in local:
            cp.wait()
    return run


def _exchange(arrays, scatter, name):
    n = len(arrays)

    def body(*refs):
        srcs, dsts, sems = refs[:n], refs[n:2 * n], refs[2 * n:]
        _exchange_body(list(zip(srcs, dsts)), scatter)(*sems)

    shapes = [jax.ShapeDtypeStruct(a.shape if scatter else (N_DEV,) + a.shape, a.dtype) for a in arrays]
    return pl.pallas_call(
        body, name=name, in_specs=[_ANY] * n, out_specs=[_ANY] * n, out_shape=shapes,
        scratch_shapes=[pltpu.SemaphoreType.DMA((n, N_DEV - 1)), pltpu.SemaphoreType.DMA((n, N_DEV - 1)),
                        pltpu.SemaphoreType.DMA((n,))],
    )(*arrays)


ADAMW_BLOCK_BYTES = 1 << 20


def _row_block(r, c):
    for rb in range(r, 0, -1):
        if r % rb == 0 and (rb % 16 == 0 or rb == r) and rb * c * 4 <= ADAMW_BLOCK_BYTES:
            return rb
    return r


def adamw(w, m, v, parts, *, layer, name):
    _, r, c = w.shape
    rb = _row_block(r, c)

    def body(w_ref, m_ref, v_ref, p_ref, g_ref, d_ref, nm_ref, nv_ref):
        g = p_ref[0].astype(F32)
        for j in range(1, N_DEV):
            g = g + p_ref[j].astype(F32)
        g_ref[...] = g
        m_new = ADAM_B1 * m_ref[...] + (1.0 - ADAM_B1) * g
        v_new = ADAM_B2 * v_ref[...] + (1.0 - ADAM_B2) * (g * g)
        m_hat = m_new / (1.0 - ADAM_B1 ** ADAM_STEP)
        v_hat = v_new / (1.0 - ADAM_B2 ** ADAM_STEP)
        d_ref[...] = -ADAM_LR * (m_hat / (jnp.sqrt(v_hat) + ADAM_EPS) + ADAM_WD * w_ref[...])
        nm_ref[...] = m_new
        nv_ref[...] = v_new

    wblk = pl.BlockSpec((None, rb, c), lambda i: (layer, i, 0))
    oblk = pl.BlockSpec((rb, c), lambda i: (i, 0))
    return pl.pallas_call(
        body, name=name, grid=(r // rb,),
        in_specs=[wblk, wblk, wblk, pl.BlockSpec((N_DEV, rb, c), lambda i: (0, i, 0))], out_specs=[oblk] * 4,
        out_shape=[jax.ShapeDtypeStruct((r, c), F32)] * 4, compiler_params=_cparams(),
    )(w, m, v, parts)


BIG = (("w_in", 2), ("w_uq", 2), ("w_ukv", 2), ("w_pa", 2), ("w_pb", 1), ("w_o", 1), ("w_gate", 2), ("w_up", 2), ("w_down", 1))
SMALL = ("norm_mix_g", "pool_w", "pool_scale", "q_norm_g", "kv_norm_g", "norm_ffn_g", "final_norm_g")
WEIGHTS = ("meta_tokens", "norm_mix_g", "w_in", "pool_w", "pool_scale", "q_norm_g", "kv_norm_g", "w_uq", "w_ukv", "w_pa", "w_pb",
           "w_o", "norm_ffn_g", "w_gate", "w_up", "w_down", "final_norm_g")
HEAD_QK = QK_NOPE + QK_ROPE
KR_END = Z_KR + QK_ROPE


def _cat_cols(parts):
    return [jnp.concatenate(parts, axis=1)]


def _cat_rows(parts):
    return [jnp.concatenate(parts, axis=0)]


def _arr_w_in(parts):
    full = jnp.concatenate(parts, axis=1)
    zc = lambda n: jnp.zeros((full.shape[0], n), full.dtype)
    return [jnp.concatenate([full[:, :Z_KR], zc(QK_NOPE), full[:, Z_KR:KR_END], zc(LANES - HEAD_QK), full[:, KR_END:]], axis=1)]


def _arr_w_uq(parts):
    full = jnp.concatenate(parts, axis=1)
    z = jnp.zeros((full.shape[0], HEAD_SLOT - HEAD_QK), full.dtype)
    pieces = []
    for hd in range(N_HEADS):
        pieces += [full[:, hd * HEAD_QK:(hd + 1) * HEAD_QK], z]
    return [jnp.concatenate(pieces, axis=1)]


def _arr_w_ukv(parts):
    full = jnp.concatenate(parts, axis=1)
    z = jnp.zeros((full.shape[0], HEAD_SLOT - QK_NOPE), full.dtype)
    wide = QK_NOPE + V_DIM
    k, v = [], []
    for hd in range(N_HEADS):
        k += [full[:, hd * wide:hd * wide + QK_NOPE], z]
        v.append(full[:, hd * wide + QK_NOPE:(hd + 1) * wide])
    return [jnp.concatenate(k, axis=1), jnp.concatenate(v, axis=1)]


def arrange(g, layer, fn, out_shapes, name):
    _, _, rows, cols = g.shape

    def body(g_ref, *o_refs):
        for o_ref, val in zip(o_refs, fn([g_ref[p] for p in range(N_DEV)])):
            o_ref[...] = val

    return pl.pallas_call(
        body, name=name, grid=(1,),
        in_specs=[pl.BlockSpec((N_DEV, None, rows, cols), lambda i: (0, layer, 0, 0))],
        out_specs=[pl.BlockSpec(s, lambda i: (0, 0)) for s in out_shapes],
        out_shape=[jax.ShapeDtypeStruct(s, g.dtype) for s in out_shapes], compiler_params=_cparams(),
    )(g)


def _arranged_ranges(lo, hi):
    out = []
    for a, b, shift in ((0, Z_KR, 0), (Z_KR, KR_END, QK_NOPE), (KR_END, D_IN, LANES - QK_ROPE)):
        s, e = max(lo, a), min(hi, b)
        if s < e:
            out.append((s + shift, e + shift))
    return out


def _chunks_w_in(acc):
    cs = D_IN // N_DEV
    return [jnp.concatenate([acc[:, a:b] for a, b in _arranged_ranges(p * cs, (p + 1) * cs)], axis=1) for p in range(N_DEV)]


def _chunks_w_uq(acc):
    per = N_HEADS // N_DEV
    return [jnp.concatenate([acc[:, hd * HEAD_SLOT:hd * HEAD_SLOT + HEAD_QK] for hd in range(p * per, (p + 1) * per)], axis=1)
            for p in range(N_DEV)]


def _chunks_w_ukv(acc_k, acc_v):
    per = N_HEADS // N_DEV
    out = []
    for p in range(N_DEV):
        pieces = []
        for hd in range(p * per, (p + 1) * per):
            pieces += [acc_k[:, hd * HEAD_SLOT:hd * HEAD_SLOT + QK_NOPE], acc_v[:, hd * V_DIM:(hd + 1) * V_DIM]]
        out.append(jnp.concatenate(pieces, axis=1))
    return out


def _chunks_cols(acc):
    cs = acc.shape[1] // N_DEV
    return [acc[:, p * cs:(p + 1) * cs] for p in range(N_DEV)]


def _chunks_rows(acc):
    rs = acc.shape[0] // N_DEV
    return [acc[p * rs:(p + 1) * rs, :] for p in range(N_DEV)]


def _pack(parts, row_multiple):
    flat = jnp.concatenate([p.reshape(-1) for p in parts])
    return jnp.pad(flat, (0, -flat.shape[0] % (row_multiple * LANES))).reshape(-1, LANES)


def _unpack(packed, shapes):
    flat, out, off = packed.reshape(-1), [], 0
    for s in shapes:
        n = 1
        for d in s:
            n *= d
        out.append(flat[off:off + n].reshape(s))
        off += n
    return out


def _rope_table(lp, nb):
    inv = 1.0 / (ROPE_THETA ** (jnp.arange(0, QK_ROPE, 2, dtype=F32) / QK_ROPE))
    ang = jnp.arange(lp, dtype=F32)[:, None] * inv[None, :]
    cos, sin = jnp.cos(ang), jnp.sin(ang)
    z = lambda n: jnp.zeros((lp, n), F32)
    tail = LANES - QK_NOPE - QK_ROPE
    c = jnp.concatenate([jnp.ones((lp, QK_NOPE), F32), cos, cos, z(tail)], axis=1)
    cr = jnp.concatenate([z(QK_NOPE), cos, cos, z(tail)], axis=1)
    s1 = jnp.concatenate([z(QK_NOPE), -sin, z(HALF_ROPE), z(tail)], axis=1)
    s2 = jnp.concatenate([z(QK_NOPE), z(HALF_ROPE), sin, z(tail)], axis=1)
    return jnp.tile(jnp.concatenate([c, cr, s1, s2], axis=1), (nb, 1))


def _layer_params(g, small, l):
    arr = lambda n, fn, shapes: arrange(g[n], l, fn, shapes, f"arrange_{n}_{l}")
    d = D_MODEL
    p = dict(win=arr("w_in", _arr_w_in, [(d, DZ)])[0], wuq=arr("w_uq", _arr_w_uq, [(Q_RANK, QK_WIDTH)])[0],
             wpa=arr("w_pa", _cat_cols, [(POOL_WIDTH, d)])[0], wpb=arr("w_pb", _cat_rows, [(d, d)])[0],
             wo=arr("w_o", _cat_rows, [(d, d)])[0], wg=arr("w_gate", _cat_cols, [(d, D_FF)])[0],
             wu=arr("w_up", _cat_cols, [(d, D_FF)])[0], wd=arr("w_down", _cat_rows, [(D_FF, d)])[0])
    p["wuk"], p["wuv"] = arr("w_ukv", _arr_w_ukv, [(KV_RANK, QK_WIDTH), (KV_RANK, d)])
    p.update({k + "t": v.T for k, v in p.items()})
    pw = small["pool_w"][l].astype(BF16)
    p.update(g_mix=small["norm_mix_g"][l][None], gq=small["q_norm_g"][l][None], gkv=small["kv_norm_g"][l][None],
             g_ffn=small["norm_ffn_g"][l][None], ps=small["pool_scale"][l][None], pw=pw, pwt=jnp.swapaxes(pw, 1, 2))
    return p


TM_FWD, TM_BWD, TM_WGRAD = 512, 256, 512


def _layer_fwd(h, p, rope, nb, lp, tag):
    z, q, k, v = in_proj_fwd(h, p["g_mix"], p["win"], p["gq"], p["gkv"], p["wuq"], p["wuk"], p["wuv"], rope, tm=TM_FWD,
                             name=f"in_proj_fwd_{tag}")
    o, lse = attn_fwd(q, k, v, nb=nb, lp=lp, name=f"attn_fwd_{tag}")
    h1, pooled, a, pa, pb, mg = merge_fwd(h, z, o, p["pw"], p["ps"], p["wpa"], p["wpb"], p["wo"], tm=TM_FWD, lp=lp, nb=nb,
                                          name=f"merge_fwd_{tag}")
    h2, gt, up = ffn_fwd(h1, p["g_ffn"], p["wg"], p["wu"], p["wd"], tm=TM_FWD, name=f"ffn_fwd_{tag}")
    return h2, dict(h=h, z=z, q=q, k=k, v=v, o=o, lse=lse, h1=h1, pooled=pooled, a=a, pa=pa, pb=pb, mg=mg, gt=gt, up=up)


def _layer_bwd(dh2, p, s, rope, nb, lp, tag):
    d = D_MODEL
    wg_ = lambda n, x, ys, fn, shape: wgrad(x, ys, fn, shape, tm=TM_WGRAD, name=f"wgrad_{n}_{tag}")
    dh1, hn2, act, dgt, dup, dg_ffn = ffn_bwd(dh2, s["h1"], p["g_ffn"], s["gt"], s["up"], p["wgt"], p["wut"], p["wdt"], tm=TM_BWD,
                                              name=f"ffn_bwd_{tag}")
    ff = D_FF // N_DEV
    chunks = dict(w_gate=wg_("gate", hn2, [dgt], _chunks_cols, (d, ff)), w_up=wg_("up", hn2, [dup], _chunks_cols, (d, ff)),
                  w_down=wg_("down", act, [dh2], _chunks_rows, (ff, d)))
    dga, dgb, dpa, dpb, do, dpool, dps, dpw = merge_bwd(dh1, s["z"], s["pa"], s["pb"], s["pooled"], p["pw"], p["pwt"], p["ps"],
                                                        p["wpat"], p["wpbt"], p["wot"], tm=TM_FWD, name=f"merge_bwd_{tag}")
    chunks.update(w_o=wg_("o", s["mg"], [dh1], _chunks_rows, (d // N_DEV, d)),
                  w_pa=wg_("pa", s["a"], [dpa], _chunks_cols, (POOL_WIDTH, d // N_DEV)),
                  w_pb=wg_("pb", s["o"], [dpb], _chunks_rows, (d // N_DEV, d)))
    dq, dk, dv = attn_bwd(s["q"], s["k"], s["v"], s["o"], do, s["lse"], nb=nb, lp=lp, name=f"attn_bwd_{tag}")
    dh, hn, dz, cqn, ckvn, dqb, dkb, dvb, dg_mix, dgq, dgkv = in_proj_bwd(
        dh1, s["h"], p["g_mix"], s["z"], dq, dk, dv, dga, dgb, dpool, p["wint"], p["gq"], p["gkv"], p["wuqt"], p["wukt"], p["wuvt"],
        rope, tm=TM_BWD, lp=lp, nb=nb, name=f"in_proj_bwd_{tag}")
    chunks.update(w_in=wg_("in", hn, [dz], _chunks_w_in, (d, D_IN // N_DEV)),
                  w_uq=wg_("uq", cqn, [dqb], _chunks_w_uq, (Q_RANK, N_HEADS * HEAD_QK // N_DEV)),
                  w_ukv=wg_("ukv", ckvn, [dkb, dvb], _chunks_w_ukv, (KV_RANK, N_HEADS * (QK_NOPE + V_DIM) // N_DEV)))
    small = dict(norm_ffn_g=dg_ffn[0], pool_scale=dps[0], pool_w=dpw, norm_mix_g=dg_mix[0], q_norm_g=dgq[0], kv_norm_g=dgkv[0])
    return dh, chunks, small


def local_step(x, loss_target, meta, small, gathered):
    nb, seq, d = x.shape
    lp = -(-(N_META + seq) // TQ) * TQ
    t = nb * lp
    assert t % TM_FWD == 0 and nb <= 2
    rope = _rope_table(lp, nb)
    pad = jnp.zeros((nb, lp - N_META - seq, d), F32)
    h = jnp.concatenate([jnp.broadcast_to(meta[None], (nb, N_META, d)), x, pad], axis=1).reshape(t, d)
    target = jnp.concatenate([jnp.zeros((nb, N_META, d), F32), loss_target, pad], axis=1).reshape(t, d)
    params, saved = [], []
    for l in range(DEPTH):
        params.append(_layer_params(gathered, small, l))
        h, s = _layer_fwd(h, params[l], rope, nb, lp, l)
        saved.append(s)
    parts, dh, dgf = loss_head(h, small["final_norm_g"][None], target, tm=TM_FWD, lp=lp, nb=nb, seq=seq, name="loss_head")
    loss = jnp.sum(parts[::8, 0])
    chunks, small_l = [None] * DEPTH, [None] * DEPTH
    for l in reversed(range(DEPTH)):
        dh, chunks[l], small_l[l] = _layer_bwd(dh, params[l], saved[l], rope, nb, lp, l)
    dh = dh.reshape(nb, lp, d)
    small_grads = {n: jnp.stack([sl[n] for sl in small_l]) for n in small_l[0]}
    small_grads["final_norm_g"] = dgf[0]
    return loss, dh[:, N_META:N_META + seq], jnp.sum(dh[:, :N_META], axis=0), chunks, small_grads


def kernel(x, meta_tokens, norm_mix_g, w_in, pool_w, pool_scale, q_norm_g, kv_norm_g, w_uq, w_ukv, w_pa, w_pb, w_o, norm_ffn_g, w_gate, w_up, w_down, final_norm_g, loss_target, m_meta_tokens, m_norm_mix_g, m_w_in, m_pool_w, m_pool_scale, m_q_norm_g, m_kv_norm_g, m_w_uq, m_w_ukv, m_w_pa, m_w_pb, m_w_o, m_norm_ffn_g, m_w_gate, m_w_up, m_w_down, m_final_norm_g, v_meta_tokens, v_norm_mix_g, v_w_in, v_pool_w, v_pool_scale, v_q_norm_g, v_kv_norm_g, v_w_uq, v_w_ukv, v_w_pa, v_w_pb, v_w_o, v_norm_ffn_g, v_w_gate, v_w_up, v_w_down, v_final_norm_g):
    args = dict(locals())
    w = {n: args[n] for n in WEIGHTS}
    m = {n: args["m_" + n] for n in WEIGHTS}
    v = {n: args["v_" + n] for n in WEIGHTS}
    small = {n: w[n] for n in SMALL}
    big = [n for n, _ in BIG]

    got = _exchange([w[n].astype(BF16) for n in big] + [meta_tokens], False, "gather_weights")
    gathered = dict(zip(big, got))
    meta = jnp.moveaxis(got[-1], 0, 1).reshape(N_META, D_MODEL)

    loss, grad_x, dmeta, chunks, small_grads = local_step(x, loss_target, meta, small, gathered)
    loss = lax.psum(loss, ("x", "y", "c"))

    meta_chunks = jnp.moveaxis(dmeta.reshape(N_META, N_DEV, D_MODEL // N_DEV), 1, 0).astype(BF16)
    recv = [_exchange([chunks[l][n] for n in big] + ([meta_chunks] if l == 0 else []), True, f"scatter_grads_{l}")
            for l in range(DEPTH)]
    small_send = _pack([small_grads[n] for n in SMALL], 8)
    (small_recv,) = _exchange([small_send], False, "gather_small_grads")

    out = {}
    for j, n in enumerate(big):
        per_layer = [adamw(w[n], m[n], v[n], recv[l][j], layer=l, name=f"adamw_{n}_{l}") for l in range(DEPTH)]
        out[n] = [jnp.stack([per_layer[l][kind] for l in range(DEPTH)]) for kind in range(4)]
    out["meta_tokens"] = adamw(meta_tokens[None], m["meta_tokens"][None], v["meta_tokens"][None], recv[0][-1], layer=0,
                               name="adamw_meta_tokens")
    pk = lambda d: _pack([d[n] for n in SMALL], 8)[None]
    packed = adamw(pk(w), pk(m), pk(v), small_recv, layer=0, name="adamw_small")
    shapes = [w[n].shape for n in SMALL]
    for n, *kinds in zip(SMALL, *[_unpack(packed[kind], shapes) for kind in range(4)]):
        out[n] = kinds
    return (loss, grad_x, *[out[n][kind] for kind in range(4) for n in WEIGHTS])
```

```python
import functools
import math

import jax
import jax.numpy as jnp
from jax import lax
from jax.experimental import pallas as pl
from jax.experimental.pallas import tpu as pltpu

F32, BF16 = jnp.float32, jnp.bfloat16

D_MODEL = 1024
N_META = 16
N_HEADS = 16
QK_NOPE, QK_ROPE, V_DIM = 64, 32, 64
HALF_ROPE = QK_ROPE // 2
Q_RANK, KV_RANK = 256, 128
POOL_WINDOWS = (2, 4, 8, 16)
POOL_GROUP = 128
POOL_WIDTH = POOL_GROUP * len(POOL_WINDOWS)
POOL_HALO = 16
D_FF = 2816
D_IN = 2976
NORM_EPS = 1e-6
SM_SCALE = (QK_NOPE + QK_ROPE) ** -0.5
LOG2E = math.log2(math.e)
EXP2_SCALE = SM_SCALE * LOG2E
MASK_VALUE = -1e30
ROPE_THETA = 10000.0
DEPTH = 2
N_DEV = 8

ADAM_LR, ADAM_B1, ADAM_B2, ADAM_EPS, ADAM_WD, ADAM_STEP = 0.001, 0.9, 0.999, 1e-08, 0.01, 10

LANES = 128
HEAD_SLOT = LANES
QK_WIDTH = N_HEADS * HEAD_SLOT
Z_CQ, Z_CKV, Z_KR, Z_GA, Z_GB, DZ = 512, 768, 896, 1024, 2048, 3072
TQ = TK = 256
VMEM_LIMIT = 56 * 1024 * 1024


def _cparams():
    return pltpu.CompilerParams(vmem_limit_bytes=VMEM_LIMIT)


def _rows(tm, width, col=0):
    return pl.BlockSpec((tm, width), lambda i: (i, col))


def _whole(shape):
    zeros = (0,) * len(shape)
    return pl.BlockSpec(shape, lambda i: zeros, pipeline_mode=pl.Buffered(1))


def _acc(shape):
    zeros = (0,) * len(shape)
    return pl.BlockSpec(shape, lambda i: zeros)


def _dot(a, b):
    return jnp.dot(a, b, preferred_element_type=F32)


def _dot_tn(a, b):
    return lax.dot_general(a, b, (((0,), (0,)), ((), ())), preferred_element_type=F32)


def _dot_nt(a, b):
    return lax.dot_general(a, b, (((1,), (1,)), ((), ())), preferred_element_type=F32)


def _rms(x):
    r = lax.rsqrt(jnp.mean(x * x, axis=-1, keepdims=True) + NORM_EPS)
    return x * r, r


def _rms_bwd(dy, xhat, r, g):
    dg = jnp.sum(dy * xhat, axis=0, keepdims=True)
    dxh = dy * g
    dx = r * (dxh - xhat * jnp.mean(dxh * xhat, axis=-1, keepdims=True))
    return dx, dg


def _sigmoid(x):
    return 1.0 / (1.0 + jnp.exp(-x))


def _rope_fwd(q, c, s1, s2):
    w = q.shape[1]
    return q * c + pltpu.roll(q, w - HALF_ROPE, 1) * s1 + pltpu.roll(q, HALF_ROPE, 1) * s2


def _rope_bwd(dq, c, s1, s2):
    w = dq.shape[1]
    return dq * c + pltpu.roll(dq * s1, HALF_ROPE, 1) + pltpu.roll(dq * s2, w - HALF_ROPE, 1)


def _rope_tables(rope, reps):
    c, cr, s1, s2 = (rope[:, k * LANES:(k + 1) * LANES] for k in range(4))
    if reps > 1:
        return jnp.tile(c, (1, reps)), jnp.tile(s1, (1, reps)), jnp.tile(s2, (1, reps))
    return cr, s1, s2


def _seq_pos(gi, lp, nb):
    pos = gi
    for b in range(1, nb):
        pos = jnp.where(gi >= b * lp, gi - b * lp, pos)
    return pos


def in_proj_fwd(h, g_mix, win, gq, gkv, wuq, wuk, wuv, rope, *, tm, name):
    t = h.shape[0]

    def body(h_ref, g_ref, win_ref, gq_ref, gkv_ref, wuq_ref, wuk_ref, wuv_ref, rope_ref, z_ref, q_ref, k_ref, v_ref):
        xhat, _ = _rms(h_ref[...])
        hn = (xhat * g_ref[...]).astype(BF16)
        z = _dot(hn, win_ref[...])
        z_ref[...] = z
        rope_t = rope_ref[...]
        xq, _ = _rms(z[:, Z_CQ:Z_CKV])
        cqn = (xq * gq_ref[...]).astype(BF16)
        q = _rope_fwd(_dot(cqn, wuq_ref[...]), *_rope_tables(rope_t, N_HEADS))
        q_ref[...] = q.astype(BF16)
        xkv, _ = _rms(z[:, Z_CKV:Z_KR])
        ckvn = (xkv * gkv_ref[...]).astype(BF16)
        kr = _rope_fwd(z[:, Z_KR:Z_GA], *_rope_tables(rope_t, 1))
        k_ref[...] = (_dot(ckvn, wuk_ref[...]) + jnp.tile(kr, (1, N_HEADS))).astype(BF16)
        v_ref[...] = _dot(ckvn, wuv_ref[...]).astype(BF16)

    return pl.pallas_call(
        body, name=name, grid=(t // tm,),
        in_specs=[_rows(tm, D_MODEL), _whole((1, D_MODEL)), _whole((D_MODEL, DZ)), _whole((1, Q_RANK)), _whole((1, KV_RANK)),
                  _whole((Q_RANK, QK_WIDTH)), _whole((KV_RANK, QK_WIDTH)), _whole((KV_RANK, D_MODEL)), _rows(tm, 4 * LANES)],
        out_specs=[_rows(tm, DZ), _rows(tm, QK_WIDTH), _rows(tm, QK_WIDTH), _rows(tm, D_MODEL)],
        out_shape=[jax.ShapeDtypeStruct((t, DZ), F32), jax.ShapeDtypeStruct((t, QK_WIDTH), BF16),
                   jax.ShapeDtypeStruct((t, QK_WIDTH), BF16), jax.ShapeDtypeStruct((t, D_MODEL), BF16)],
        compiler_params=_cparams(),
    )(h, g_mix, win, gq, gkv, wuq, wuk, wuv, rope)


def attn_fwd(q, k, v, *, nb, lp, hb, name):
    t = q.shape[0]
    nq = lp // TQ

    def body(q_ref, k_ref, v_ref, o_ref, lse_ref):
        lane = lax.broadcasted_iota(jnp.int32, (TQ, LANES), 1)
        causal = lax.broadcasted_iota(jnp.int32, (TQ, TK), 1) <= lax.broadcasted_iota(jnp.int32, (TQ, TK), 0)

        def q_block(qi, carry):
            qs = pl.multiple_of(qi * TQ, TQ)
            qh = [q_ref[pl.ds(qs, TQ), hd * HEAD_SLOT:(hd + 1) * HEAD_SLOT] for hd in range(hb)]

            def k_step(kj, c, masked):
                ks = pl.multiple_of(kj * TK, TK)
                out = []
                for hd in range(hb):
                    m, l, acc = c[hd]
                    s = _dot_nt(qh[hd], k_ref[pl.ds(ks, TK), hd * HEAD_SLOT:(hd + 1) * HEAD_SLOT])
                    if masked:
                        s = jnp.where(causal, s, MASK_VALUE)
                    m_new = jnp.maximum(m, jnp.max(s, axis=1, keepdims=True))
                    p = jnp.exp2((s - m_new) * EXP2_SCALE)
                    alpha = jnp.exp2((m - m_new) * EXP2_SCALE)
                    l = alpha * l + jnp.sum(p, axis=1, keepdims=True)
                    pr = hd // 2
                    acc = alpha * acc + _dot(p.astype(BF16), v_ref[pl.ds(ks, TK), pr * LANES:(pr + 1) * LANES])
                    out.append((m_new, l, acc))
                return tuple(out)

            init = tuple((jnp.full((TQ, 1), MASK_VALUE, F32), jnp.zeros((TQ, 1), F32), jnp.zeros((TQ, LANES), F32))
                         for _ in range(hb))
            c = lax.fori_loop(0, qi, functools.partial(k_step, masked=False), init)
            c = k_step(qi, c, True)
            for pr in range(hb // 2):
                (m0, l0, a0), (m1, l1, a1) = c[2 * pr], c[2 * pr + 1]
                ls = slice(pr * LANES, (pr + 1) * LANES)
                o_ref[pl.ds(qs, TQ), ls] = jnp.where(lane < V_DIM, a0 / l0, a1 / l1).astype(BF16)
                lse_ref[pl.ds(qs, TQ), ls] = jnp.where(lane < V_DIM, m0 * SM_SCALE + jnp.log(l0), m1 * SM_SCALE + jnp.log(l1))
            return carry

        lax.fori_loop(0, nq, q_block, 0)

    blk = lambda w: pl.BlockSpec((lp, w), lambda b, g: (b, g))
    return pl.pallas_call(
        body, name=name, grid=(nb, N_HEADS // hb),
        in_specs=[blk(hb * HEAD_SLOT), blk(hb * HEAD_SLOT), blk(hb * V_DIM)],
        out_specs=[blk(hb * V_DIM), blk(hb * V_DIM)],
        out_shape=[jax.ShapeDtypeStruct((t, D_MODEL), BF16), jax.ShapeDtypeStruct((t, D_MODEL), F32)],
        compiler_params=_cparams(),
    )(q, k, v)


def _pool_band_fwd(i, tm, lp, nb):
    r = lax.broadcasted_iota(jnp.int32, (tm, POOL_HALO + tm), 0)
    e = lax.broadcasted_iota(jnp.int32, (tm, POOL_HALO + tm), 1)
    diff = r + POOL_HALO - e
    pos = _seq_pos(i * tm + lax.broadcasted_iota(jnp.int32, (tm, 1), 0), lp, nb)
    out = []
    for w in POOL_WINDOWS:
        cnt = jnp.minimum(pos + 1, w)
        band = jnp.where((diff >= 0) & (diff < cnt), 1.0, 0.0).astype(BF16)
        out.append((band, cnt.astype(F32)))
    return out


def merge_fwd(h, z, o, pw, ps, wpa, wpb, wo, *, tm, lp, nb, name):
    t = h.shape[0]
    hb = tm // POOL_HALO

    def body(h_ref, u_ref, uprev_ref, ga_ref, gb_ref, o_ref, pw_ref, ps_ref, wpa_ref, wpb_ref, wo_ref,
             h1_ref, pooled_ref, a_ref, pa_ref, pb_ref, mg_ref):
        i = pl.program_id(0)
        u = u_ref[...]
        uext = jnp.concatenate([uprev_ref[...], u], axis=0).astype(BF16)
        pooled, ys = [], []
        for g, (band, cnt) in enumerate(_pool_band_fwd(i, tm, lp, nb)):
            gs = slice(g * POOL_GROUP, (g + 1) * POOL_GROUP)
            pg = (_dot(band, uext[:, gs]) / cnt - u[:, gs]).astype(BF16)
            pooled.append(pg)
            ys.append(_dot(pg, pw_ref[g]))
        pooled_ref[...] = jnp.concatenate(pooled, axis=1)
        a = (jnp.concatenate(ys, axis=1) * ps_ref[...]).astype(BF16)
        a_ref[...] = a
        pa = _dot(a, wpa_ref[...])
        pb = _dot(o_ref[...], wpb_ref[...])
        pa_ref[...] = pa.astype(BF16)
        pb_ref[...] = pb.astype(BF16)
        mg = (_sigmoid(ga_ref[...]) * pa + _sigmoid(gb_ref[...]) * pb).astype(BF16)
        mg_ref[...] = mg
        h1_ref[...] = h_ref[...] + _dot(mg, wo_ref[...])

    halo = pl.BlockSpec((POOL_HALO, POOL_WIDTH), lambda i: (jnp.maximum(i * hb - 1, 0), 0))
    return pl.pallas_call(
        body, name=name, grid=(t // tm,),
        in_specs=[_rows(tm, D_MODEL), _rows(tm, POOL_WIDTH), halo, _rows(tm, D_MODEL, 1), _rows(tm, D_MODEL, 2), _rows(tm, D_MODEL),
                  _whole((4, POOL_GROUP, POOL_GROUP)), _whole((1, POOL_WIDTH)), _whole((POOL_WIDTH, D_MODEL)),
                  _whole((D_MODEL, D_MODEL)), _whole((D_MODEL, D_MODEL))],
        out_specs=[_rows(tm, D_MODEL), _rows(tm, POOL_WIDTH), _rows(tm, POOL_WIDTH), _rows(tm, D_MODEL), _rows(tm, D_MODEL),
                   _rows(tm, D_MODEL)],
        out_shape=[jax.ShapeDtypeStruct((t, D_MODEL), F32), jax.ShapeDtypeStruct((t, POOL_WIDTH), BF16),
                   jax.ShapeDtypeStruct((t, POOL_WIDTH), BF16), jax.ShapeDtypeStruct((t, D_MODEL), BF16),
                   jax.ShapeDtypeStruct((t, D_MODEL), BF16), jax.ShapeDtypeStruct((t, D_MODEL), BF16)],
        compiler_params=_cparams(),
    )(h, z, z, z, z, o, pw, ps, wpa, wpb, wo)


def ffn_fwd(h1, g, wg, wu, wd, *, tm, name):
    t = h1.shape[0]

    def body(h_ref, g_ref, wg_ref, wu_ref, wd_ref, h2_ref, gt_ref, up_ref):
        h = h_ref[...]
        xhat, _ = _rms(h)
        hn = (xhat * g_ref[...]).astype(BF16)
        gt = _dot(hn, wg_ref[...])
        up = _dot(hn, wu_ref[...])
        gt_ref[...] = gt.astype(BF16)
        up_ref[...] = up.astype(BF16)
        act = (gt * _sigmoid(gt) * up).astype(BF16)
        h2_ref[...] = h + _dot(act, wd_ref[...])

    return pl.pallas_call(
        body, name=name, grid=(t // tm,),
        in_specs=[_rows(tm, D_MODEL), _whole((1, D_MODEL)), _whole((D_MODEL, D_FF)), _whole((D_MODEL, D_FF)), _whole((D_FF, D_MODEL))],
        out_specs=[_rows(tm, D_MODEL), _rows(tm, D_FF), _rows(tm, D_FF)],
        out_shape=[jax.ShapeDtypeStruct((t, D_MODEL), F32), jax.ShapeDtypeStruct((t, D_FF), BF16), jax.ShapeDtypeStruct((t, D_FF), BF16)],
        compiler_params=_cparams(),
    )(h1, g, wg, wu, wd)


def loss_head(h, g, target, *, tm, lp, nb, seq, name):
    t = h.shape[0]
    nt = t // tm

    def body(h_ref, g_ref, t_ref, loss_ref, dh_ref, dg_ref):
        i = pl.program_id(0)
        pos = _seq_pos(i * tm + lax.broadcasted_iota(jnp.int32, (tm, 1), 0), lp, nb)
        real = (pos >= N_META) & (pos < N_META + seq)
        xhat, r = _rms(h_ref[...])
        gg = g_ref[...]
        err = jnp.where(real, xhat * gg - t_ref[...], 0.0)
        loss_ref[...] = jnp.full((8, LANES), 0.5 * jnp.sum(err * err) / D_MODEL, F32)
        dx, dg = _rms_bwd(err * (1.0 / D_MODEL), xhat, r, gg)
        dh_ref[...] = dx

        @pl.when(i == 0)
        def _():
            dg_ref[...] = jnp.zeros_like(dg_ref)

        dg_ref[...] += dg

    return pl.pallas_call(
        body, name=name, grid=(nt,),
        in_specs=[_rows(tm, D_MODEL), _whole((1, D_MODEL)), _rows(tm, D_MODEL)],
        out_specs=[pl.BlockSpec((8, LANES), lambda i: (i, 0)), _rows(tm, D_MODEL), _acc((1, D_MODEL))],
        out_shape=[jax.ShapeDtypeStruct((nt * 8, LANES), F32), jax.ShapeDtypeStruct((t, D_MODEL), F32),
                   jax.ShapeDtypeStruct((1, D_MODEL), F32)],
        compiler_params=_cparams(),
    )(h, g, target)


def wgrad(x, ys, chunk_fn, chunk_shape, *, tm, name):
    t, m = x.shape

    def body(x_ref, *refs):
        y_refs, o_ref, accs = refs[:len(ys)], refs[len(ys)], refs[len(ys) + 1:]
        i = pl.program_id(0)

        @pl.when(i == 0)
        def _():
            for acc in accs:
                acc[...] = jnp.zeros_like(acc)

        xb = x_ref[...].astype(BF16)
        for y_ref, acc in zip(y_refs, accs):
            acc[...] += _dot_tn(xb, y_ref[...].astype(BF16))

        @pl.when(i == t // tm - 1)
        def _():
            for p, chunk in enumerate(chunk_fn(*accs)):
                o_ref[p % 2, p // 2] = chunk.astype(BF16)

    out = (2, N_DEV // 2) + tuple(chunk_shape)
    return pl.pallas_call(
        body, name=name, grid=(t // tm,),
        in_specs=[_rows(tm, m)] + [_rows(tm, y.shape[1]) for y in ys], out_specs=_acc(out),
        out_shape=jax.ShapeDtypeStruct(out, BF16), scratch_shapes=[pltpu.VMEM((m, y.shape[1]), F32) for y in ys],
        compiler_params=_cparams(),
    )(x, *ys)


def ffn_bwd(dh2, h1, g, gt, up, wgt, wut, wdt, *, tm, name):
    t = h1.shape[0]

    def body(dh2_ref, h_ref, g_ref, gt_ref, up_ref, wgt_ref, wut_ref, wdt_ref, dh1_ref, hn_ref, act_ref, dgt_ref, dup_ref, dg_ref):
        dh2 = dh2_ref[...]
        dact = _dot(dh2.astype(BF16), wdt_ref[...])
        gt = gt_ref[...].astype(F32)
        up = up_ref[...].astype(F32)
        sg = _sigmoid(gt)
        silu = gt * sg
        act_ref[...] = (silu * up).astype(BF16)
        dgt = (dact * up * (sg * (1.0 + gt * (1.0 - sg)))).astype(BF16)
        dup = (dact * silu).astype(BF16)
        dgt_ref[...] = dgt
        dup_ref[...] = dup
        dhn = _dot(dgt, wgt_ref[...]) + _dot(dup, wut_ref[...])
        xhat, r = _rms(h_ref[...])
        gg = g_ref[...]
        hn_ref[...] = (xhat * gg).astype(BF16)
        dx, dg = _rms_bwd(dhn, xhat, r, gg)
        dh1_ref[...] = dh2 + dx

        @pl.when(pl.program_id(0) == 0)
        def _():
            dg_ref[...] = jnp.zeros_like(dg_ref)

        dg_ref[...] += dg

    return pl.pallas_call(
        body, name=name, grid=(t // tm,),
        in_specs=[_rows(tm, D_MODEL), _rows(tm, D_MODEL), _whole((1, D_MODEL)), _rows(tm, D_FF), _rows(tm, D_FF),
                  _whole((D_FF, D_MODEL)), _whole((D_FF, D_MODEL)), _whole((D_MODEL, D_FF))],
        out_specs=[_rows(tm, D_MODEL), _rows(tm, D_MODEL), _rows(tm, D_FF), _rows(tm, D_FF), _rows(tm, D_FF), _acc((1, D_MODEL))],
        out_shape=[jax.ShapeDtypeStruct((t, D_MODEL), F32), jax.ShapeDtypeStruct((t, D_MODEL), BF16),
                   jax.ShapeDtypeStruct((t, D_FF), BF16), jax.ShapeDtypeStruct((t, D_FF), BF16),
                   jax.ShapeDtypeStruct((t, D_FF), BF16), jax.ShapeDtypeStruct((1, D_MODEL), F32)],
        compiler_params=_cparams(),
    )(dh2, h1, g, gt, up, wgt, wut, wdt)


def merge_bwd(dh1, z, pa, pb, pooled, pw, pwt, ps, wpat, wpbt, wot, *, tm, name):
    t = dh1.shape[0]

    def body(dh1_ref, ga_ref, gb_ref, pa_ref, pb_ref, pooled_ref, pw_ref, pwt_ref, ps_ref, wpat_ref, wpbt_ref, wot_ref,
             dga_ref, dgb_ref, dpa_ref, dpb_ref, do_ref, dpool_ref, dps_ref, dpw_ref):
        dmg = _dot(dh1_ref[...].astype(BF16), wot_ref[...])
        sa = _sigmoid(ga_ref[...])
        sb = _sigmoid(gb_ref[...])
        dga_ref[...] = (dmg * pa_ref[...].astype(F32) * sa * (1.0 - sa)).astype(BF16)
        dgb_ref[...] = (dmg * pb_ref[...].astype(F32) * sb * (1.0 - sb)).astype(BF16)
        dpa = (dmg * sa).astype(BF16)
        dpb = (dmg * sb).astype(BF16)
        dpa_ref[...] = dpa
        dpb_ref[...] = dpb
        do_ref[...] = _dot(dpb, wpbt_ref[...]).astype(BF16)
        da = _dot(dpa, wpat_ref[...])
        pooled = pooled_ref[...]
        ps = ps_ref[...]

        @pl.when(pl.program_id(0) == 0)
        def _():
            dps_ref[...] = jnp.zeros_like(dps_ref)
            dpw_ref[...] = jnp.zeros_like(dpw_ref)

        dps, dpool = [], []
        for g in range(len(POOL_WINDOWS)):
            gs = slice(g * POOL_GROUP, (g + 1) * POOL_GROUP)
            y = _dot(pooled[:, gs], pw_ref[g])
            dps.append(jnp.sum(da[:, gs] * y, axis=0, keepdims=True))
            dy = (da[:, gs] * ps[:, gs]).astype(BF16)
            dpool.append(_dot(dy, pwt_ref[g]))
            dpw_ref[g] += _dot_tn(pooled[:, gs], dy)
        dps_ref[...] += jnp.concatenate(dps, axis=1)
        dpool_ref[...] = jnp.concatenate(dpool, axis=1)

    return pl.pallas_call(
        body, name=name, grid=(t // tm,),
        in_specs=[_rows(tm, D_MODEL), _rows(tm, D_MODEL, 1), _rows(tm, D_MODEL, 2), _rows(tm, D_MODEL), _rows(tm, D_MODEL),
                  _rows(tm, POOL_WIDTH), _whole((4, POOL_GROUP, POOL_GROUP)), _whole((4, POOL_GROUP, POOL_GROUP)),
                  _whole((1, POOL_WIDTH)), _whole((D_MODEL, POOL_WIDTH)), _whole((D_MODEL, D_MODEL)), _whole((D_MODEL, D_MODEL))],
        out_specs=[_rows(tm, D_MODEL), _rows(tm, D_MODEL), _rows(tm, D_MODEL), _rows(tm, D_MODEL), _rows(tm, D_MODEL),
                   _rows(tm, POOL_WIDTH), _acc((1, POOL_WIDTH)), _acc((4, POOL_GROUP, POOL_GROUP))],
        out_shape=[jax.ShapeDtypeStruct((t, D_MODEL), BF16)] * 5
        + [jax.ShapeDtypeStruct((t, POOL_WIDTH), F32), jax.ShapeDtypeStruct((1, POOL_WIDTH), F32),
           jax.ShapeDtypeStruct((4, POOL_GROUP, POOL_GROUP), F32)],
        compiler_params=_cparams(),
    )(dh1, z, z, pa, pb, pooled, pw, pwt, ps, wpat, wpbt, wot)


def attn_bwd(q, k, v, o, do, lse, *, nb, lp, hb, name):
    t = q.shape[0]
    nq = lp // TQ

    def body(q_ref, k_ref, v_ref, o_ref, do_ref, lse_ref, dq_ref, dk_ref, dv_ref, kt, doh, lse_row, delta_row, dqt):
        lane = lax.broadcasted_iota(jnp.int32, (lp, LANES), 1)
        first = lane < V_DIM
        sub = lax.broadcasted_iota(jnp.int32, (LANES, lp), 0)
        for pr in range(hb // 2):
            ls = slice(pr * LANES, (pr + 1) * LANES)
            do = do_ref[:, ls]
            doh[2 * pr] = jnp.where(first, do, jnp.zeros_like(do))
            doh[2 * pr + 1] = jnp.where(first, jnp.zeros_like(do), do)
            prod_t = (do.astype(F32) * o_ref[:, ls].astype(F32)).T
            delta_row[2 * pr] = jnp.sum(jnp.where(sub < V_DIM, prod_t, 0.0), axis=0, keepdims=True)
            delta_row[2 * pr + 1] = jnp.sum(jnp.where(sub < V_DIM, 0.0, prod_t), axis=0, keepdims=True)
            lse_t = lse_ref[:, ls].T * LOG2E
            lse_row[2 * pr] = lse_t[0:1, :]
            lse_row[2 * pr + 1] = lse_t[V_DIM:V_DIM + 1, :]
        for hd in range(hb):
            kt[hd] = k_ref[:, hd * HEAD_SLOT:(hd + 1) * HEAD_SLOT].T
        dqt[...] = jnp.zeros(dqt.shape, F32)
        keep = lax.broadcasted_iota(jnp.int32, (TK, TQ), 0) <= lax.broadcasted_iota(jnp.int32, (TK, TQ), 1)

        def k_block(kj, carry):
            ks = pl.multiple_of(kj * TK, TK)

            def q_step(qi, c, masked):
                qs = pl.multiple_of(qi * TQ, TQ)
                out = []
                for hd in range(hb):
                    dk, dv = c[hd]
                    hs = slice(hd * HEAD_SLOT, (hd + 1) * HEAD_SLOT)
                    pr = hd // 2
                    qh = q_ref[pl.ds(qs, TQ), hs]
                    st = _dot_nt(k_ref[pl.ds(ks, TK), hs], qh)
                    if masked:
                        st = jnp.where(keep, st, MASK_VALUE)
                    pt = jnp.exp2(st * EXP2_SCALE - lse_row[hd, :, pl.ds(qs, TQ)])
                    do_h = doh[hd, pl.ds(qs, TQ), :]
                    dpt = _dot_nt(v_ref[pl.ds(ks, TK), pr * LANES:(pr + 1) * LANES], do_h)
                    dst = (pt * (dpt - delta_row[hd, :, pl.ds(qs, TQ)])).astype(BF16)
                    dv = dv + _dot(pt.astype(BF16), do_h)
                    dk = dk + _dot(dst, qh)
                    dqt[hd, :, pl.ds(qs, TQ)] += _dot(kt[hd, :, pl.ds(ks, TK)], dst)
                    out.append((dk, dv))
                return tuple(out)

            zero = jnp.zeros((TK, LANES), F32)
            c = q_step(kj, tuple((zero, zero) for _ in range(hb)), True)
            c = lax.fori_loop(kj + 1, nq, functools.partial(q_step, masked=False), c)
            for hd in range(hb):
                dk_ref[pl.ds(ks, TK), hd * HEAD_SLOT:(hd + 1) * HEAD_SLOT] = c[hd][0] * SM_SCALE
            for pr in range(hb // 2):
                dv_ref[pl.ds(ks, TK), pr * LANES:(pr + 1) * LANES] = c[2 * pr][1] + c[2 * pr + 1][1]
            return carry

        lax.fori_loop(0, nq, k_block, 0)
        for hd in range(hb):
            dq_ref[:, hd * HEAD_SLOT:(hd + 1) * HEAD_SLOT] = dqt[hd].T * SM_SCALE

    blk = lambda w: pl.BlockSpec((lp, w), lambda b, g: (b, g))
    return pl.pallas_call(
        body, name=name, grid=(nb, N_HEADS // hb),
        in_specs=[blk(hb * HEAD_SLOT), blk(hb * HEAD_SLOT), blk(hb * V_DIM), blk(hb * V_DIM), blk(hb * V_DIM), blk(hb * V_DIM)],
        out_specs=[blk(hb * HEAD_SLOT), blk(hb * HEAD_SLOT), blk(hb * V_DIM)],
        out_shape=[jax.ShapeDtypeStruct((t, QK_WIDTH), F32), jax.ShapeDtypeStruct((t, QK_WIDTH), F32),
                   jax.ShapeDtypeStruct((t, D_MODEL), F32)],
        scratch_shapes=[pltpu.VMEM((hb, HEAD_SLOT, lp), BF16), pltpu.VMEM((hb, lp, LANES), BF16), pltpu.VMEM((hb, 1, lp), F32),
                        pltpu.VMEM((hb, 1, lp), F32), pltpu.VMEM((hb, HEAD_SLOT, lp), F32)],
        compiler_params=_cparams(),
    )(q, k, v, o, do, lse)


def in_proj_bwd(dh1, h, g_mix, z, dq, dk, dv, dga, dgb, dpool, wint, gq, gkv, wuqt, wukt, wuvt, rope, *, tm, lp, nb, name):
    t = h.shape[0]
    hb = tm // POOL_HALO
    last_halo = t // POOL_HALO - 1

    def body(dh1_ref, h_ref, g_ref, zcq_ref, zckv_ref, dq_ref, dk_ref, dv_ref, dga_ref, dgb_ref, dpool_ref, dnext_ref,
             wint_ref, gq_ref, gkv_ref, wuqt_ref, wukt_ref, wuvt_ref, rope_ref,
             dh_ref, hn_ref, dz_ref, cqn_ref, ckvn_ref, dqb_ref, dkb_ref, dvb_ref, dg_ref, dgq_ref, dgkv_ref):
        i = pl.program_id(0)
        rope_t = rope_ref[...]
        dqb = _rope_bwd(dq_ref[...], *_rope_tables(rope_t, N_HEADS)).astype(BF16)
        dqb_ref[...] = dqb
        xq, rq = _rms(zcq_ref[...])
        gq_v = gq_ref[...]
        cqn_ref[...] = (xq * gq_v).astype(BF16)
        dcq, dgq = _rms_bwd(_dot(dqb, wuqt_ref[...]), xq, rq, gq_v)
        dk = dk_ref[...]
        dkb = dk.astype(BF16)
        dvb = dv_ref[...].astype(BF16)
        dkb_ref[...] = dkb
        dvb_ref[...] = dvb
        xkv, rkv = _rms(zckv_ref[...])
        gkv_v = gkv_ref[...]
        ckvn_ref[...] = (xkv * gkv_v).astype(BF16)
        dckv, dgkv = _rms_bwd(_dot(dkb, wukt_ref[...]) + _dot(dvb, wuvt_ref[...]), xkv, rkv, gkv_v)
        dks = dk[:, :HEAD_SLOT]
        for hd in range(1, N_HEADS):
            dks = dks + dk[:, hd * HEAD_SLOT:(hd + 1) * HEAD_SLOT]
        dzk = _rope_bwd(dks, *_rope_tables(rope_t, 1))
        dp_cur = dpool_ref[...]
        dp_ext = jnp.concatenate([dp_cur, dnext_ref[...]], axis=0)
        r = lax.broadcasted_iota(jnp.int32, (tm, tm + POOL_HALO), 0)
        e = lax.broadcasted_iota(jnp.int32, (tm, tm + POOL_HALO), 1)
        gt_col = i * tm + lax.broadcasted_iota(jnp.int32, (1, tm + POOL_HALO), 1)
        pos_col = _seq_pos(gt_col, lp, nb)
        gt_row = i * tm + lax.broadcasted_iota(jnp.int32, (tm + POOL_HALO, 1), 0)
        pos_row = _seq_pos(gt_row, lp, nb)
        dus = []
        for g, w in enumerate(POOL_WINDOWS):
            gs = slice(g * POOL_GROUP, (g + 1) * POOL_GROUP)
            band = jnp.where((e - r >= 0) & (e - r < jnp.minimum(pos_col + 1, w)) & (gt_col < t), 1.0, 0.0).astype(BF16)
            scaled = jnp.where(gt_row < t, dp_ext[:, gs] / jnp.minimum(pos_row + 1, w).astype(F32), 0.0).astype(BF16)
            dus.append(_dot(band, scaled) - dp_cur[:, gs])
        dz = jnp.concatenate(dus + [dcq, dckv, dzk], axis=1).astype(BF16)
        dz = jnp.concatenate([dz, dga_ref[...], dgb_ref[...]], axis=1)
        dz_ref[...] = dz
        xhat, rr = _rms(h_ref[...])
        gg = g_ref[...]
        hn_ref[...] = (xhat * gg).astype(BF16)
        dx, dg = _rms_bwd(_dot(dz, wint_ref[...]), xhat, rr, gg)
        dh_ref[...] = dh1_ref[...] + dx

        @pl.when(i == 0)
        def _():
            dg_ref[...] = jnp.zeros_like(dg_ref)
            dgq_ref[...] = jnp.zeros_like(dgq_ref)
            dgkv_ref[...] = jnp.zeros_like(dgkv_ref)

        dg_ref[...] += dg
        dgq_ref[...] += dgq
        dgkv_ref[...] += dgkv

    nxt = pl.BlockSpec((POOL_HALO, POOL_WIDTH), lambda i: (jnp.minimum((i + 1) * hb, last_halo), 0))
    return pl.pallas_call(
        body, name=name, grid=(t // tm,),
        in_specs=[_rows(tm, D_MODEL), _rows(tm, D_MODEL), _whole((1, D_MODEL)), _rows(tm, Q_RANK, Z_CQ // Q_RANK),
                  _rows(tm, KV_RANK, Z_CKV // KV_RANK), _rows(tm, QK_WIDTH), _rows(tm, QK_WIDTH), _rows(tm, D_MODEL),
                  _rows(tm, D_MODEL), _rows(tm, D_MODEL), _rows(tm, POOL_WIDTH), nxt,
                  _whole((DZ, D_MODEL)), _whole((1, Q_RANK)), _whole((1, KV_RANK)), _whole((QK_WIDTH, Q_RANK)),
                  _whole((QK_WIDTH, KV_RANK)), _whole((D_MODEL, KV_RANK)), _rows(tm, 4 * LANES)],
        out_specs=[_rows(tm, D_MODEL), _rows(tm, D_MODEL), _rows(tm, DZ), _rows(tm, Q_RANK), _rows(tm, KV_RANK),
                   _rows(tm, QK_WIDTH), _rows(tm, QK_WIDTH), _rows(tm, D_MODEL),
                   _acc((1, D_MODEL)), _acc((1, Q_RANK)), _acc((1, KV_RANK))],
        out_shape=[jax.ShapeDtypeStruct((t, D_MODEL), F32), jax.ShapeDtypeStruct((t, D_MODEL), BF16),
                   jax.ShapeDtypeStruct((t, DZ), BF16), jax.ShapeDtypeStruct((t, Q_RANK), BF16),
                   jax.ShapeDtypeStruct((t, KV_RANK), BF16), jax.ShapeDtypeStruct((t, QK_WIDTH), BF16),
                   jax.ShapeDtypeStruct((t, QK_WIDTH), BF16), jax.ShapeDtypeStruct((t, D_MODEL), BF16),
                   jax.ShapeDtypeStruct((1, D_MODEL), F32), jax.ShapeDtypeStruct((1, Q_RANK), F32),
                   jax.ShapeDtypeStruct((1, KV_RANK), F32)],
        compiler_params=_cparams(),
    )(dh1, h, g_mix, z, z, dq, dk, dv, dga, dgb, dpool, dpool, wint, gq, gkv, wuqt, wukt, wuvt, rope)


_MESH = pl.DeviceIdType.MESH
_ANY = pl.BlockSpec(memory_space=pl.ANY)


def _place():
    x, y, c = lax.axis_index("x"), lax.axis_index("y"), lax.axis_index("c")
    return x, y, c, 4 * x + 2 * y + c


def _peer(x, y, c, k):
    px, py, pc = (1 - x) if k & 4 else x, (1 - y) if k & 2 else y, (1 - c) if k & 1 else c
    return (px, py, pc), 4 * px + 2 * py + pc


ALL_PEERS = tuple(range(1, N_DEV))
CHIP_PEERS = (2, 4, 6)
N_CHIPS = N_DEV // 2


def _sem_scratch(n, m):
    return [pltpu.SemaphoreType.DMA((n, m)), pltpu.SemaphoreType.DMA((n, m)), pltpu.SemaphoreType.DMA((n,))]


def _exchange(arrays, scatter, peers, name):
    n = len(arrays)
    slots = N_DEV if peers == ALL_PEERS else N_CHIPS

    def body(*refs):
        srcs, dsts, (send_sems, recv_sems, local_sems) = refs[:n], refs[n:2 * n], refs[2 * n:]
        x, y, c, me = _place()
        mine = me if slots == N_DEV else 2 * x + y
        local = [pltpu.make_async_copy(src.at[mine] if scatter else src, dst.at[mine], local_sems.at[j])
                 for j, (src, dst) in enumerate(zip(srcs, dsts))]
        for cp in local:
            cp.start()
        sends, recvs = [], []
        for t, k in enumerate(peers):
            peer, pidx = _peer(x, y, c, k)
            theirs = 2 * peer[0] + peer[1] if slots == N_CHIPS else pidx
            for j, (src, dst) in enumerate(zip(srcs, dsts)):
                part = src.at[theirs] if scatter else src
                sems = dict(send_sem=send_sems.at[j, t], recv_sem=recv_sems.at[j, t], device_id=peer, device_id_type=_MESH)
                sends.append(pltpu.make_async_remote_copy(src_ref=part, dst_ref=dst.at[mine], **sems))
                recvs.append(pltpu.make_async_remote_copy(src_ref=part, dst_ref=dst.at[theirs], **sems))
        for cp in sends:
            cp.start()
        for cp in recvs:
            cp.wait_recv()
        for cp in sends:
            cp.wait_send()
        for cp in local:
            cp.wait()

    shapes = [jax.ShapeDtypeStruct(a.shape if scatter else (slots,) + a.shape, a.dtype) for a in arrays]
    return pl.pallas_call(body, name=name, in_specs=[_ANY] * n, out_specs=[_ANY] * n, out_shape=shapes,
                          scratch_shapes=_sem_scratch(n, len(peers)))(*arrays)


def _gather_two_level(arrays, name):
    n = len(arrays)

    def body(*refs):
        srcs, dsts, (send_sems, recv_sems, local_sems) = refs[:n], refs[n:2 * n], refs[2 * n:]
        x, y, c, me = _place()
        sibling, sidx = _peer(x, y, c, 1)

        def copy(j, sem, block, to, src=None):
            rows = dsts[j].at[block]
            return pltpu.make_async_remote_copy(src_ref=rows if src is None else src, dst_ref=rows, send_sem=send_sems.at[j, sem],
                                                recv_sem=recv_sems.at[j, sem], device_id=to, device_id_type=_MESH)

        local = [pltpu.make_async_copy(srcs[j], dsts[j].at[me], local_sems.at[j]) for j in range(n)]
        for cp in local:
            cp.start()
        first = [copy(j, 1 + t, me, _peer(x, y, c, k)[0], src=srcs[j]) for t, k in enumerate(CHIP_PEERS) for j in range(n)]
        first += [copy(j, 0, me, sibling, src=srcs[j]) for j in range(n)]
        for cp in first:
            cp.start()
        passed = []
        for t, k in enumerate(CHIP_PEERS):
            peer, pidx = _peer(x, y, c, k)
            for j in range(n):
                copy(j, 1 + t, pidx, peer).wait_recv()
                passed.append(copy(j, 4 + t, pidx, sibling))
                passed[-1].start()
        for j in range(n):
            copy(j, 0, sidx, sibling).wait_recv()
        for t, k in enumerate(CHIP_PEERS):
            _, pidx = _peer(x, y, c, k ^ 1)
            for j in range(n):
                copy(j, 4 + t, pidx, sibling).wait_recv()
        for cp in first + passed:
            cp.wait_send()
        for cp in local:
            cp.wait()

    shapes = [jax.ShapeDtypeStruct((N_DEV,) + a.shape, a.dtype) for a in arrays]
    return pl.pallas_call(body, name=name, in_specs=[_ANY] * n, out_specs=[_ANY] * n, out_shape=shapes,
                          scratch_shapes=_sem_scratch(n, 1 + 2 * len(CHIP_PEERS)))(*arrays)


def _to_sibling(arrays, name):
    n = len(arrays)

    def body(*refs):
        srcs, dsts, (send_sems, recv_sems) = refs[:n], refs[n:2 * n], refs[2 * n:]
        x, y, c, _ = _place()
        sibling, _ = _peer(x, y, c, 1)
        copies = [pltpu.make_async_remote_copy(src_ref=srcs[j].at[1 - c], dst_ref=dsts[j], send_sem=send_sems.at[j],
                                               recv_sem=recv_sems.at[j], device_id=sibling, device_id_type=_MESH) for j in range(n)]
        for cp in copies:
            cp.start()
        for cp in copies:
            cp.wait()

    shapes = [jax.ShapeDtypeStruct(a.shape[1:], a.dtype) for a in arrays]
    return pl.pallas_call(body, name=name, in_specs=[_ANY] * n, out_specs=[_ANY] * n, out_shape=shapes,
                          scratch_shapes=[pltpu.SemaphoreType.DMA((n,)), pltpu.SemaphoreType.DMA((n,))])(*arrays)


def pair_add(own, theirs, core, *, name):
    _, ns, r, c = own.shape
    rb = _row_block(r, c)

    def body(core_ref, a_ref, b_ref, o_ref):
        o_ref[...] = (a_ref[...].astype(F32) + b_ref[...].astype(F32)).astype(o_ref.dtype)

    return pl.pallas_call(
        body, name=name,
        grid_spec=pltpu.PrefetchScalarGridSpec(
            num_scalar_prefetch=1, grid=(ns, r // rb),
            in_specs=[pl.BlockSpec((None, None, rb, c), lambda i, j, core_ref: (core_ref[0], i, j, 0)),
                      pl.BlockSpec((None, rb, c), lambda i, j, core_ref: (i, j, 0))],
            out_specs=pl.BlockSpec((None, rb, c), lambda i, j, core_ref: (i, j, 0))),
        out_shape=jax.ShapeDtypeStruct((ns, r, c), own.dtype), compiler_params=_cparams(),
    )(core, own, theirs)


ADAMW_BLOCK_BYTES = 1 << 20


def _row_block(r, c):
    for rb in range(r, 0, -1):
        if r % rb == 0 and (rb % 16 == 0 or rb == r) and rb * c * 4 <= ADAMW_BLOCK_BYTES:
            return rb
    return r


def adamw(w, m, v, parts, *, layer, name):
    _, r, c = w.shape
    n_parts = parts.shape[0]
    rb = _row_block(r, c)

    def body(w_ref, m_ref, v_ref, p_ref, g_ref, d_ref, nm_ref, nv_ref):
        g = p_ref[0].astype(F32)
        for j in range(1, n_parts):
            g = g + p_ref[j].astype(F32)
        g_ref[...] = g
        m_new = ADAM_B1 * m_ref[...] + (1.0 - ADAM_B1) * g
        v_new = ADAM_B2 * v_ref[...] + (1.0 - ADAM_B2) * (g * g)
        m_hat = m_new / (1.0 - ADAM_B1 ** ADAM_STEP)
        v_hat = v_new / (1.0 - ADAM_B2 ** ADAM_STEP)
        d_ref[...] = -ADAM_LR * (m_hat / (jnp.sqrt(v_hat) + ADAM_EPS) + ADAM_WD * w_ref[...])
        nm_ref[...] = m_new
        nv_ref[...] = v_new

    wblk = pl.BlockSpec((None, rb, c), lambda i: (layer, i, 0))
    oblk = pl.BlockSpec((rb, c), lambda i: (i, 0))
    return pl.pallas_call(
        body, name=name, grid=(r // rb,),
        in_specs=[wblk, wblk, wblk, pl.BlockSpec((n_parts, rb, c), lambda i: (0, i, 0))], out_specs=[oblk] * 4,
        out_shape=[jax.ShapeDtypeStruct((r, c), F32)] * 4, compiler_params=_cparams(),
    )(w, m, v, parts)


BIG = (("w_in", 2), ("w_uq", 2), ("w_ukv", 2), ("w_pa", 2), ("w_pb", 1), ("w_o", 1), ("w_gate", 2), ("w_up", 2), ("w_down", 1))
SMALL = ("norm_mix_g", "pool_w", "pool_scale", "q_norm_g", "kv_norm_g", "norm_ffn_g", "final_norm_g")
WEIGHTS = ("meta_tokens", "norm_mix_g", "w_in", "pool_w", "pool_scale", "q_norm_g", "kv_norm_g", "w_uq", "w_ukv", "w_pa", "w_pb",
           "w_o", "norm_ffn_g", "w_gate", "w_up", "w_down", "final_norm_g")
HEAD_QK = QK_NOPE + QK_ROPE
KR_END = Z_KR + QK_ROPE


def _cat_cols(parts):
    return [jnp.concatenate(parts, axis=1)]


def _cat_rows(parts):
    return [jnp.concatenate(parts, axis=0)]


def _arr_w_in(parts):
    full = jnp.concatenate(parts, axis=1)
    zc = lambda n: jnp.zeros((full.shape[0], n), full.dtype)
    return [jnp.concatenate([full[:, :Z_KR], zc(QK_NOPE), full[:, Z_KR:KR_END], zc(LANES - HEAD_QK), full[:, KR_END:]], axis=1)]


def _arr_w_uq(parts):
    full = jnp.concatenate(parts, axis=1)
    z = jnp.zeros((full.shape[0], HEAD_SLOT - HEAD_QK), full.dtype)
    pieces = []
    for hd in range(N_HEADS):
        pieces += [full[:, hd * HEAD_QK:(hd + 1) * HEAD_QK], z]
    return [jnp.concatenate(pieces, axis=1)]


def _arr_w_ukv(parts):
    full = jnp.concatenate(parts, axis=1)
    z = jnp.zeros((full.shape[0], HEAD_SLOT - QK_NOPE), full.dtype)
    wide = QK_NOPE + V_DIM
    k, v = [], []
    for hd in range(N_HEADS):
        k += [full[:, hd * wide:hd * wide + QK_NOPE], z]
        v.append(full[:, hd * wide + QK_NOPE:(hd + 1) * wide])
    return [jnp.concatenate(k, axis=1), jnp.concatenate(v, axis=1)]


def arrange(g, layer, fn, out_shapes, name):
    _, _, rows, cols = g.shape

    def body(g_ref, *o_refs):
        for o_ref, val in zip(o_refs, fn([g_ref[p] for p in range(N_DEV)])):
            o_ref[...] = val

    return pl.pallas_call(
        body, name=name, grid=(1,),
        in_specs=[pl.BlockSpec((N_DEV, None, rows, cols), lambda i: (0, layer, 0, 0))],
        out_specs=[pl.BlockSpec(s, lambda i: (0, 0)) for s in out_shapes],
        out_shape=[jax.ShapeDtypeStruct(s, g.dtype) for s in out_shapes], compiler_params=_cparams(),
    )(g)


def _arranged_ranges(lo, hi):
    out = []
    for a, b, shift in ((0, Z_KR, 0), (Z_KR, KR_END, QK_NOPE), (KR_END, D_IN, LANES - QK_ROPE)):
        s, e = max(lo, a), min(hi, b)
        if s < e:
            out.append((s + shift, e + shift))
    return out


def _chunks_w_in(acc):
    cs = D_IN // N_DEV
    return [jnp.concatenate([acc[:, a:b] for a, b in _arranged_ranges(p * cs, (p + 1) * cs)], axis=1) for p in range(N_DEV)]


def _chunks_w_uq(acc):
    per = N_HEADS // N_DEV
    return [jnp.concatenate([acc[:, hd * HEAD_SLOT:hd * HEAD_SLOT + HEAD_QK] for hd in range(p * per, (p + 1) * per)], axis=1)
            for p in range(N_DEV)]


def _chunks_w_ukv(acc_k, acc_v):
    per = N_HEADS // N_DEV
    out = []
    for p in range(N_DEV):
        pieces = []
        for hd in range(p * per, (p + 1) * per):
            pieces += [acc_k[:, hd * HEAD_SLOT:hd * HEAD_SLOT + QK_NOPE], acc_v[:, hd * V_DIM:(hd + 1) * V_DIM]]
        out.append(jnp.concatenate(pieces, axis=1))
    return out


def _chunks_cols(acc):
    cs = acc.shape[1] // N_DEV
    return [acc[:, p * cs:(p + 1) * cs] for p in range(N_DEV)]


def _chunks_rows(acc):
    rs = acc.shape[0] // N_DEV
    return [acc[p * rs:(p + 1) * rs, :] for p in range(N_DEV)]


def _pack(parts, row_multiple):
    flat = jnp.concatenate([p.reshape(-1) for p in parts])
    return jnp.pad(flat, (0, -flat.shape[0] % (row_multiple * LANES))).reshape(-1, LANES)


def _unpack(packed, shapes):
    flat, out, off = packed.reshape(-1), [], 0
    for s in shapes:
        n = 1
        for d in s:
            n *= d
        out.append(flat[off:off + n].reshape(s))
        off += n
    return out


def _rope_table(lp, nb):
    inv = 1.0 / (ROPE_THETA ** (jnp.arange(0, QK_ROPE, 2, dtype=F32) / QK_ROPE))
    ang = jnp.arange(lp, dtype=F32)[:, None] * inv[None, :]
    cos, sin = jnp.cos(ang), jnp.sin(ang)
    z = lambda n: jnp.zeros((lp, n), F32)
    tail = LANES - QK_NOPE - QK_ROPE
    c = jnp.concatenate([jnp.ones((lp, QK_NOPE), F32), cos, cos, z(tail)], axis=1)
    cr = jnp.concatenate([z(QK_NOPE), cos, cos, z(tail)], axis=1)
    s1 = jnp.concatenate([z(QK_NOPE), -sin, z(HALF_ROPE), z(tail)], axis=1)
    s2 = jnp.concatenate([z(QK_NOPE), z(HALF_ROPE), sin, z(tail)], axis=1)
    return jnp.tile(jnp.concatenate([c, cr, s1, s2], axis=1), (nb, 1))


def _layer_params(g, small, l):
    arr = lambda n, fn, shapes: arrange(g[n], l, fn, shapes, f"arrange_{n}_{l}")
    d = D_MODEL
    p = dict(win=arr("w_in", _arr_w_in, [(d, DZ)])[0], wuq=arr("w_uq", _arr_w_uq, [(Q_RANK, QK_WIDTH)])[0],
             wpa=arr("w_pa", _cat_cols, [(POOL_WIDTH, d)])[0], wpb=arr("w_pb", _cat_rows, [(d, d)])[0],
             wo=arr("w_o", _cat_rows, [(d, d)])[0], wg=arr("w_gate", _cat_cols, [(d, D_FF)])[0],
             wu=arr("w_up", _cat_cols, [(d, D_FF)])[0], wd=arr("w_down", _cat_rows, [(D_FF, d)])[0])
    p["wuk"], p["wuv"] = arr("w_ukv", _arr_w_ukv, [(KV_RANK, QK_WIDTH), (KV_RANK, d)])
    p.update({k + "t": v.T for k, v in p.items()})
    pw = small["pool_w"][l].astype(BF16)
    p.update(g_mix=small["norm_mix_g"][l][None], gq=small["q_norm_g"][l][None], gkv=small["kv_norm_g"][l][None],
             g_ffn=small["norm_ffn_g"][l][None], ps=small["pool_scale"][l][None], pw=pw, pwt=jnp.swapaxes(pw, 1, 2))
    return p


TM_FWD, TM_BWD, TM_WGRAD = 512, 256, 512
HEADS_FWD, HEADS_BWD = 8, 4


def _layer_fwd(h, p, rope, nb, lp, tag):
    z, q, k, v = in_proj_fwd(h, p["g_mix"], p["win"], p["gq"], p["gkv"], p["wuq"], p["wuk"], p["wuv"], rope, tm=TM_FWD,
                             name=f"in_proj_fwd_{tag}")
    o, lse = attn_fwd(q, k, v, nb=nb, lp=lp, hb=HEADS_FWD, name=f"attn_fwd_{tag}")
    h1, pooled, a, pa, pb, mg = merge_fwd(h, z, o, p["pw"], p["ps"], p["wpa"], p["wpb"], p["wo"], tm=TM_FWD, lp=lp, nb=nb,
                                          name=f"merge_fwd_{tag}")
    h2, gt, up = ffn_fwd(h1, p["g_ffn"], p["wg"], p["wu"], p["wd"], tm=TM_FWD, name=f"ffn_fwd_{tag}")
    return h2, dict(h=h, z=z, q=q, k=k, v=v, o=o, lse=lse, h1=h1, pooled=pooled, a=a, pa=pa, pb=pb, mg=mg, gt=gt, up=up)


def _layer_bwd(dh2, p, s, rope, nb, lp, tag):
    d = D_MODEL
    wg_ = lambda n, x, ys, fn, shape: wgrad(x, ys, fn, shape, tm=TM_WGRAD, name=f"wgrad_{n}_{tag}")
    dh1, hn2, act, dgt, dup, dg_ffn = ffn_bwd(dh2, s["h1"], p["g_ffn"], s["gt"], s["up"], p["wgt"], p["wut"], p["wdt"], tm=TM_BWD,
                                              name=f"ffn_bwd_{tag}")
    ff = D_FF // N_DEV
    chunks = dict(w_gate=wg_("gate", hn2, [dgt], _chunks_cols, (d, ff)), w_up=wg_("up", hn2, [dup], _chunks_cols, (d, ff)),
                  w_down=wg_("down", act, [dh2], _chunks_rows, (ff, d)))
    dga, dgb, dpa, dpb, do, dpool, dps, dpw = merge_bwd(dh1, s["z"], s["pa"], s["pb"], s["pooled"], p["pw"], p["pwt"], p["ps"],
                                                        p["wpat"], p["wpbt"], p["wot"], tm=TM_FWD, name=f"merge_bwd_{tag}")
    chunks.update(w_o=wg_("o", s["mg"], [dh1], _chunks_rows, (d // N_DEV, d)),
                  w_pa=wg_("pa", s["a"], [dpa], _chunks_cols, (POOL_WIDTH, d // N_DEV)),
                  w_pb=wg_("pb", s["o"], [dpb], _chunks_rows, (d // N_DEV, d)))
    dq, dk, dv = attn_bwd(s["q"], s["k"], s["v"], s["o"], do, s["lse"], nb=nb, lp=lp, hb=HEADS_BWD,
                          name=f"attn_bwd_{tag}")
    dh, hn, dz, cqn, ckvn, dqb, dkb, dvb, dg_mix, dgq, dgkv = in_proj_bwd(
        dh1, s["h"], p["g_mix"], s["z"], dq, dk, dv, dga, dgb, dpool, p["wint"], p["gq"], p["gkv"], p["wuqt"], p["wukt"], p["wuvt"],
        rope, tm=TM_BWD, lp=lp, nb=nb, name=f"in_proj_bwd_{tag}")
    chunks.update(w_in=wg_("in", hn, [dz], _chunks_w_in, (d, D_IN // N_DEV)),
                  w_uq=wg_("uq", cqn, [dqb], _chunks_w_uq, (Q_RANK, N_HEADS * HEAD_QK // N_DEV)),
                  w_ukv=wg_("ukv", ckvn, [dkb, dvb], _chunks_w_ukv, (KV_RANK, N_HEADS * (QK_NOPE + V_DIM) // N_DEV)))
    small = dict(norm_ffn_g=dg_ffn[0], pool_scale=dps[0], pool_w=dpw, norm_mix_g=dg_mix[0], q_norm_g=dgq[0], kv_norm_g=dgkv[0])
    return dh, chunks, small


def local_step(x, loss_target, meta, small, gathered):
    nb, seq, d = x.shape
    lp = -(-(N_META + seq) // TQ) * TQ
    t = nb * lp
    assert t % TM_FWD == 0 and nb <= 2
    rope = _rope_table(lp, nb)
    pad = jnp.zeros((nb, lp - N_META - seq, d), F32)
    h = jnp.concatenate([jnp.broadcast_to(meta[None], (nb, N_META, d)), x, pad], axis=1).reshape(t, d)
    target = jnp.concatenate([jnp.zeros((nb, N_META, d), F32), loss_target, pad], axis=1).reshape(t, d)
    params, saved = [], []
    for l in range(DEPTH):
        params.append(_layer_params(gathered, small, l))
        h, s = _layer_fwd(h, params[l], rope, nb, lp, l)
        saved.append(s)
    parts, dh, dgf = loss_head(h, small["final_norm_g"][None], target, tm=TM_FWD, lp=lp, nb=nb, seq=seq, name="loss_head")
    loss = jnp.sum(parts[::8, 0])
    chunks, small_l = [None] * DEPTH, [None] * DEPTH
    for l in reversed(range(DEPTH)):
        dh, chunks[l], small_l[l] = _layer_bwd(dh, params[l], saved[l], rope, nb, lp, l)
    dh = dh.reshape(nb, lp, d)
    small_grads = {n: jnp.stack([sl[n] for sl in small_l]) for n in small_l[0]}
    small_grads["final_norm_g"] = dgf[0]
    return loss, dh[:, N_META:N_META + seq], jnp.sum(dh[:, :N_META], axis=0), chunks, small_grads


def kernel(x, meta_tokens, norm_mix_g, w_in, pool_w, pool_scale, q_norm_g, kv_norm_g, w_uq, w_ukv, w_pa, w_pb, w_o, norm_ffn_g, w_gate, w_up, w_down, final_norm_g, loss_target, m_meta_tokens, m_norm_mix_g, m_w_in, m_pool_w, m_pool_scale, m_q_norm_g, m_kv_norm_g, m_w_uq, m_w_ukv, m_w_pa, m_w_pb, m_w_o, m_norm_ffn_g, m_w_gate, m_w_up, m_w_down, m_final_norm_g, v_meta_tokens, v_norm_mix_g, v_w_in, v_pool_w, v_pool_scale, v_q_norm_g, v_kv_norm_g, v_w_uq, v_w_ukv, v_w_pa, v_w_pb, v_w_o, v_norm_ffn_g, v_w_gate, v_w_up, v_w_down, v_final_norm_g):
    args = dict(locals())
    w = {n: args[n] for n in WEIGHTS}
    m = {n: args["m_" + n] for n in WEIGHTS}
    v = {n: args["v_" + n] for n in WEIGHTS}
    small = {n: w[n] for n in SMALL}
    big = [n for n, _ in BIG]

    got = _gather_two_level([w[n].astype(BF16) for n in big] + [meta_tokens], "gather_weights")
    gathered = dict(zip(big, got))
    meta = jnp.moveaxis(got[-1], 0, 1).reshape(N_META, D_MODEL)

    loss, grad_x, dmeta, chunks, small_grads = local_step(x, loss_target, meta, small, gathered)
    loss = lax.psum(loss, ("x", "y", "c"))

    core = lax.axis_index("c").astype(jnp.int32).reshape(1)
    meta_chunks = jnp.transpose(dmeta.reshape(N_META, N_CHIPS, 2, D_MODEL // N_DEV), (2, 1, 0, 3)).astype(BF16)
    recv = []
    for l in range(DEPTH):
        own = [chunks[l][n] for n in big] + ([meta_chunks] if l == 0 else [])
        names = big + (["meta_tokens"] if l == 0 else [])
        theirs = _to_sibling(own, f"pair_grads_{l}")
        sums = [pair_add(a, b, core, name=f"pair_add_{n}_{l}") for n, a, b in zip(names, own, theirs)]
        recv.append(_exchange(sums, True, CHIP_PEERS, f"scatter_grads_{l}"))
    small_send = _pack([small_grads[n] for n in SMALL], 8)
    (small_recv,) = _exchange([small_send], False, ALL_PEERS, "gather_small_grads")

    out = {}
    for j, n in enumerate(big):
        per_layer = [adamw(w[n], m[n], v[n], recv[l][j], layer=l, name=f"adamw_{n}_{l}") for l in range(DEPTH)]
        out[n] = [jnp.stack([per_layer[l][kind] for l in range(DEPTH)]) for kind in range(4)]
    out["meta_tokens"] = adamw(meta_tokens[None], m["meta_tokens"][None], v["meta_tokens"][None], recv[0][-1], layer=0,
                               name="adamw_meta_tokens")
    pk = lambda d: _pack([d[n] for n in SMALL], 8)[None]
    packed = adamw(pk(w), pk(m), pk(v), small_recv, layer=0, name="adamw_small")
    shapes = [w[n].shape for n in SMALL]
    for n, *kinds in zip(SMALL, *[_unpack(packed[kind], shapes) for kind in range(4)]):
        out[n] = kinds
    return (loss, grad_x, *[out[n][kind] for kind in range(4) for n in WEIGHTS])
```

```python
import functools
import math

import jax
import jax.numpy as jnp
from jax import lax
from jax.experimental import pallas as pl
from jax.experimental.pallas import tpu as pltpu

F32, BF16 = jnp.float32, jnp.bfloat16

D_MODEL = 1024
N_META = 16
N_HEADS = 16
QK_NOPE, QK_ROPE, V_DIM = 64, 32, 64
HALF_ROPE = QK_ROPE // 2
Q_RANK, KV_RANK = 256, 128
POOL_WINDOWS = (2, 4, 8, 16)
POOL_GROUP = 128
POOL_WIDTH = POOL_GROUP * len(POOL_WINDOWS)
POOL_HALO = 16
D_FF = 2816
D_IN = 2976
NORM_EPS = 1e-6
SM_SCALE = (QK_NOPE + QK_ROPE) ** -0.5
LOG2E = math.log2(math.e)
EXP2_SCALE = SM_SCALE * LOG2E
MASK_VALUE = -1e30
ROPE_THETA = 10000.0
DEPTH = 2
N_DEV = 8

ADAM_LR, ADAM_B1, ADAM_B2, ADAM_EPS, ADAM_WD, ADAM_STEP = 0.001, 0.9, 0.999, 1e-08, 0.01, 10

LANES = 128
HEAD_SLOT = LANES
QK_WIDTH = N_HEADS * HEAD_SLOT
Z_CQ, Z_CKV, Z_KR, Z_GA, Z_GB, DZ = 512, 768, 896, 1024, 2048, 3072
TQ = TK = 256
VMEM_LIMIT = 56 * 1024 * 1024


def _cparams():
    return pltpu.CompilerParams(vmem_limit_bytes=VMEM_LIMIT)


def _rows(tm, width, col=0):
    return pl.BlockSpec((tm, width), lambda i: (i, col))


def _whole(shape):
    zeros = (0,) * len(shape)
    return pl.BlockSpec(shape, lambda i: zeros, pipeline_mode=pl.Buffered(1))


def _acc(shape):
    zeros = (0,) * len(shape)
    return pl.BlockSpec(shape, lambda i: zeros)


def _dot(a, b):
    return jnp.dot(a, b, preferred_element_type=F32)


def _dot_tn(a, b):
    return lax.dot_general(a, b, (((0,), (0,)), ((), ())), preferred_element_type=F32)


def _dot_nt(a, b):
    return lax.dot_general(a, b, (((1,), (1,)), ((), ())), preferred_element_type=F32)


def _rms(x):
    r = lax.rsqrt(jnp.mean(x * x, axis=-1, keepdims=True) + NORM_EPS)
    return x * r, r


def _rms_bwd(dy, xhat, r, g):
    dg = jnp.sum(dy * xhat, axis=0, keepdims=True)
    dxh = dy * g
    dx = r * (dxh - xhat * jnp.mean(dxh * xhat, axis=-1, keepdims=True))
    return dx, dg


def _sigmoid(x):
    return 1.0 / (1.0 + jnp.exp(-x))


def _rope_fwd(q, c, s1, s2):
    w = q.shape[1]
    return q * c + pltpu.roll(q, w - HALF_ROPE, 1) * s1 + pltpu.roll(q, HALF_ROPE, 1) * s2


def _rope_bwd(dq, c, s1, s2):
    w = dq.shape[1]
    return dq * c + pltpu.roll(dq * s1, HALF_ROPE, 1) + pltpu.roll(dq * s2, w - HALF_ROPE, 1)


def _rope_tables(rope, reps):
    c, cr, s1, s2 = (rope[:, k * LANES:(k + 1) * LANES] for k in range(4))
    if reps > 1:
        return jnp.tile(c, (1, reps)), jnp.tile(s1, (1, reps)), jnp.tile(s2, (1, reps))
    return cr, s1, s2


def _seq_pos(gi, lp, nb):
    pos = gi
    for b in range(1, nb):
        pos = jnp.where(gi >= b * lp, gi - b * lp, pos)
    return pos


_ANY = pl.BlockSpec(memory_space=pl.ANY)


def _carrying_call(body, ride, operands, *, name, grid, in_specs, out_specs, out_shape, scratch_shapes=()):
    n_in, n_out = len(in_specs), len(out_specs)
    if ride is None:
        out = pl.pallas_call(body, name=name, grid=grid, in_specs=in_specs, out_specs=out_specs, out_shape=out_shape,
                             scratch_shapes=list(scratch_shapes), compiler_params=_cparams())(*operands)
        return out, []
    ne = len(ride.arrays)

    def carrying(*refs):
        ins, r_in, rest = refs[:n_in], refs[n_in:n_in + ne], refs[n_in + ne:]
        outs, r_out, rest = rest[:n_out], rest[n_out:n_out + ne], rest[n_out + ne:]
        scratch, sems = rest[:len(scratch_shapes)], rest[len(scratch_shapes):]
        ids = [pl.program_id(a) for a in range(len(grid))]
        first = functools.reduce(jnp.logical_and, [i == 0 for i in ids])
        last = functools.reduce(jnp.logical_and, [i == g - 1 for i, g in zip(ids, grid)])

        @pl.when(first)
        def _():
            ride.start(r_in, r_out, sems)

        body(*ins, *outs, *scratch)

        @pl.when(last)
        def _():
            ride.wait(r_in, r_out, sems)

    out = pl.pallas_call(
        carrying, name=name, grid=grid, in_specs=list(in_specs) + [_ANY] * ne, out_specs=list(out_specs) + [_ANY] * ne,
        out_shape=list(out_shape) + ride.out_shapes, scratch_shapes=list(scratch_shapes) + ride.scratch,
        input_output_aliases=ride.aliases(n_in, n_out), compiler_params=_cparams(),
    )(*operands, *ride.arrays)
    return out[:n_out], out[n_out:]


def in_proj_fwd(h, g_mix, win, gq, gkv, wuq, wuk, wuv, rope, *, tm, name):
    t = h.shape[0]

    def body(h_ref, g_ref, win_ref, gq_ref, gkv_ref, wuq_ref, wuk_ref, wuv_ref, rope_ref, z_ref, q_ref, k_ref, v_ref):
        xhat, _ = _rms(h_ref[...])
        hn = (xhat * g_ref[...]).astype(BF16)
        z = _dot(hn, win_ref[...])
        z_ref[...] = z
        rope_t = rope_ref[...]
        xq, _ = _rms(z[:, Z_CQ:Z_CKV])
        cqn = (xq * gq_ref[...]).astype(BF16)
        q = _rope_fwd(_dot(cqn, wuq_ref[...]), *_rope_tables(rope_t, N_HEADS))
        q_ref[...] = q.astype(BF16)
        xkv, _ = _rms(z[:, Z_CKV:Z_KR])
        ckvn = (xkv * gkv_ref[...]).astype(BF16)
        kr = _rope_fwd(z[:, Z_KR:Z_GA], *_rope_tables(rope_t, 1))
        k_ref[...] = (_dot(ckvn, wuk_ref[...]) + jnp.tile(kr, (1, N_HEADS))).astype(BF16)
        v_ref[...] = _dot(ckvn, wuv_ref[...]).astype(BF16)

    return pl.pallas_call(
        body, name=name, grid=(t // tm,),
        in_specs=[_rows(tm, D_MODEL), _whole((1, D_MODEL)), _whole((D_MODEL, DZ)), _whole((1, Q_RANK)), _whole((1, KV_RANK)),
                  _whole((Q_RANK, QK_WIDTH)), _whole((KV_RANK, QK_WIDTH)), _whole((KV_RANK, D_MODEL)), _rows(tm, 4 * LANES)],
        out_specs=[_rows(tm, DZ), _rows(tm, QK_WIDTH), _rows(tm, QK_WIDTH), _rows(tm, D_MODEL)],
        out_shape=[jax.ShapeDtypeStruct((t, DZ), F32), jax.ShapeDtypeStruct((t, QK_WIDTH), BF16),
                   jax.ShapeDtypeStruct((t, QK_WIDTH), BF16), jax.ShapeDtypeStruct((t, D_MODEL), BF16)],
        compiler_params=_cparams(),
    )(h, g_mix, win, gq, gkv, wuq, wuk, wuv, rope)


def attn_fwd(q, k, v, *, nb, lp, hb, name, ride=None):
    t = q.shape[0]
    nq = lp // TQ

    def body(q_ref, k_ref, v_ref, o_ref, lse_ref):
        lane = lax.broadcasted_iota(jnp.int32, (TQ, LANES), 1)
        causal = lax.broadcasted_iota(jnp.int32, (TQ, TK), 1) <= lax.broadcasted_iota(jnp.int32, (TQ, TK), 0)

        def q_block(qi, carry):
            qs = pl.multiple_of(qi * TQ, TQ)
            qh = [q_ref[pl.ds(qs, TQ), hd * HEAD_SLOT:(hd + 1) * HEAD_SLOT] for hd in range(hb)]

            def k_step(kj, c, masked):
                ks = pl.multiple_of(kj * TK, TK)
                out = []
                for hd in range(hb):
                    m, l, acc = c[hd]
                    s = _dot_nt(qh[hd], k_ref[pl.ds(ks, TK), hd * HEAD_SLOT:(hd + 1) * HEAD_SLOT])
                    if masked:
                        s = jnp.where(causal, s, MASK_VALUE)
                    m_new = jnp.maximum(m, jnp.max(s, axis=1, keepdims=True))
                    p = jnp.exp2((s - m_new) * EXP2_SCALE)
                    alpha = jnp.exp2((m - m_new) * EXP2_SCALE)
                    l = alpha * l + jnp.sum(p, axis=1, keepdims=True)
                    pr = hd // 2
                    acc = alpha * acc + _dot(p.astype(BF16), v_ref[pl.ds(ks, TK), pr * LANES:(pr + 1) * LANES])
                    out.append((m_new, l, acc))
                return tuple(out)

            init = tuple((jnp.full((TQ, 1), MASK_VALUE, F32), jnp.zeros((TQ, 1), F32), jnp.zeros((TQ, LANES), F32))
                         for _ in range(hb))
            c = lax.fori_loop(0, qi, functools.partial(k_step, masked=False), init)
            c = k_step(qi, c, True)
            for pr in range(hb // 2):
                (m0, l0, a0), (m1, l1, a1) = c[2 * pr], c[2 * pr + 1]
                ls = slice(pr * LANES, (pr + 1) * LANES)
                o_ref[pl.ds(qs, TQ), ls] = jnp.where(lane < V_DIM, a0 / l0, a1 / l1).astype(BF16)
                lse_ref[pl.ds(qs, TQ), ls] = jnp.where(lane < V_DIM, m0 * SM_SCALE + jnp.log(l0), m1 * SM_SCALE + jnp.log(l1))
            return carry

        lax.fori_loop(0, nq, q_block, 0)

    blk = lambda w: pl.BlockSpec((lp, w), lambda b, g: (b, g))
    return _carrying_call(
        body, ride, (q, k, v), name=name, grid=(nb, N_HEADS // hb),
        in_specs=[blk(hb * HEAD_SLOT), blk(hb * HEAD_SLOT), blk(hb * V_DIM)],
        out_specs=[blk(hb * V_DIM), blk(hb * V_DIM)],
        out_shape=[jax.ShapeDtypeStruct((t, D_MODEL), BF16), jax.ShapeDtypeStruct((t, D_MODEL), F32)])


def _pool_band_fwd(i, tm, lp, nb):
    r = lax.broadcasted_iota(jnp.int32, (tm, POOL_HALO + tm), 0)
    e = lax.broadcasted_iota(jnp.int32, (tm, POOL_HALO + tm), 1)
    diff = r + POOL_HALO - e
    pos = _seq_pos(i * tm + lax.broadcasted_iota(jnp.int32, (tm, 1), 0), lp, nb)
    out = []
    for w in POOL_WINDOWS:
        cnt = jnp.minimum(pos + 1, w)
        band = jnp.where((diff >= 0) & (diff < cnt), 1.0, 0.0).astype(BF16)
        out.append((band, cnt.astype(F32)))
    return out


def merge_fwd(h, z, o, pw, ps, wpa, wpb, wo, *, tm, lp, nb, name, ride=None):
    t = h.shape[0]
    hb = tm // POOL_HALO

    def body(h_ref, u_ref, uprev_ref, ga_ref, gb_ref, o_ref, pw_ref, ps_ref, wpa_ref, wpb_ref, wo_ref,
             h1_ref, pooled_ref, a_ref, pa_ref, pb_ref, mg_ref):
        i = pl.program_id(0)
        u = u_ref[...]
        uext = jnp.concatenate([uprev_ref[...], u], axis=0).astype(BF16)
        pooled, ys = [], []
        for g, (band, cnt) in enumerate(_pool_band_fwd(i, tm, lp, nb)):
            gs = slice(g * POOL_GROUP, (g + 1) * POOL_GROUP)
            pg = (_dot(band, uext[:, gs]) / cnt - u[:, gs]).astype(BF16)
            pooled.append(pg)
            ys.append(_dot(pg, pw_ref[g]))
        pooled_ref[...] = jnp.concatenate(pooled, axis=1)
        a = (jnp.concatenate(ys, axis=1) * ps_ref[...]).astype(BF16)
        a_ref[...] = a
        pa = _dot(a, wpa_ref[...])
        pb = _dot(o_ref[...], wpb_ref[...])
        pa_ref[...] = pa.astype(BF16)
        pb_ref[...] = pb.astype(BF16)
        mg = (_sigmoid(ga_ref[...]) * pa + _sigmoid(gb_ref[...]) * pb).astype(BF16)
        mg_ref[...] = mg
        h1_ref[...] = h_ref[...] + _dot(mg, wo_ref[...])

    halo = pl.BlockSpec((POOL_HALO, POOL_WIDTH), lambda i: (jnp.maximum(i * hb - 1, 0), 0))
    return _carrying_call(
        body, ride, (h, z, z, z, z, o, pw, ps, wpa, wpb, wo), name=name, grid=(t // tm,),
        in_specs=[_rows(tm, D_MODEL), _rows(tm, POOL_WIDTH), halo, _rows(tm, D_MODEL, 1), _rows(tm, D_MODEL, 2), _rows(tm, D_MODEL),
                  _whole((4, POOL_GROUP, POOL_GROUP)), _whole((1, POOL_WIDTH)), _whole((POOL_WIDTH, D_MODEL)),
                  _whole((D_MODEL, D_MODEL)), _whole((D_MODEL, D_MODEL))],
        out_specs=[_rows(tm, D_MODEL), _rows(tm, POOL_WIDTH), _rows(tm, POOL_WIDTH), _rows(tm, D_MODEL), _rows(tm, D_MODEL),
                   _rows(tm, D_MODEL)],
        out_shape=[jax.ShapeDtypeStruct((t, D_MODEL), F32), jax.ShapeDtypeStruct((t, POOL_WIDTH), BF16),
                   jax.ShapeDtypeStruct((t, POOL_WIDTH), BF16), jax.ShapeDtypeStruct((t, D_MODEL), BF16),
                   jax.ShapeDtypeStruct((t, D_MODEL), BF16), jax.ShapeDtypeStruct((t, D_MODEL), BF16)])


def ffn_fwd(h1, g, wg, wu, wd, *, tm, name):
    t = h1.shape[0]

    def body(h_ref, g_ref, wg_ref, wu_ref, wd_ref, h2_ref, gt_ref, up_ref):
        h = h_ref[...]
        xhat, _ = _rms(h)
        hn = (xhat * g_ref[...]).astype(BF16)
        gt = _dot(hn, wg_ref[...])
        up = _dot(hn, wu_ref[...])
        gt_ref[...] = gt.astype(BF16)
        up_ref[...] = up.astype(BF16)
        act = (gt * _sigmoid(gt) * up).astype(BF16)
        h2_ref[...] = h + _dot(act, wd_ref[...])

    return pl.pallas_call(
        body, name=name, grid=(t // tm,),
        in_specs=[_rows(tm, D_MODEL), _whole((1, D_MODEL)), _whole((D_MODEL, D_FF)), _whole((D_MODEL, D_FF)), _whole((D_FF, D_MODEL))],
        out_specs=[_rows(tm, D_MODEL), _rows(tm, D_FF), _rows(tm, D_FF)],
        out_shape=[jax.ShapeDtypeStruct((t, D_MODEL), F32), jax.ShapeDtypeStruct((t, D_FF), BF16), jax.ShapeDtypeStruct((t, D_FF), BF16)],
        compiler_params=_cparams(),
    )(h1, g, wg, wu, wd)


def loss_head(h, g, target, *, tm, lp, nb, seq, name):
    t = h.shape[0]
    nt = t // tm

    def body(h_ref, g_ref, t_ref, loss_ref, dh_ref, dg_ref):
        i = pl.program_id(0)
        pos = _seq_pos(i * tm + lax.broadcasted_iota(jnp.int32, (tm, 1), 0), lp, nb)
        real = (pos >= N_META) & (pos < N_META + seq)
        xhat, r = _rms(h_ref[...])
        gg = g_ref[...]
        err = jnp.where(real, xhat * gg - t_ref[...], 0.0)
        loss_ref[...] = jnp.full((8, LANES), 0.5 * jnp.sum(err * err) / D_MODEL, F32)
        dx, dg = _rms_bwd(err * (1.0 / D_MODEL), xhat, r, gg)
        dh_ref[...] = dx

        @pl.when(i == 0)
        def _():
            dg_ref[...] = jnp.zeros_like(dg_ref)

        dg_ref[...] += dg

    return pl.pallas_call(
        body, name=name, grid=(nt,),
        in_specs=[_rows(tm, D_MODEL), _whole((1, D_MODEL)), _rows(tm, D_MODEL)],
        out_specs=[pl.BlockSpec((8, LANES), lambda i: (i, 0)), _rows(tm, D_MODEL), _acc((1, D_MODEL))],
        out_shape=[jax.ShapeDtypeStruct((nt * 8, LANES), F32), jax.ShapeDtypeStruct((t, D_MODEL), F32),
                   jax.ShapeDtypeStruct((1, D_MODEL), F32)],
        compiler_params=_cparams(),
    )(h, g, target)


def wgrad(x, ys, chunk_fn, chunk_shape, *, tm, name):
    t, m = x.shape

    def body(x_ref, *refs):
        y_refs, o_ref, accs = refs[:len(ys)], refs[len(ys)], refs[len(ys) + 1:]
        i = pl.program_id(0)

        @pl.when(i == 0)
        def _():
            for acc in accs:
                acc[...] = jnp.zeros_like(acc)

        xb = x_ref[...].astype(BF16)
        for y_ref, acc in zip(y_refs, accs):
            acc[...] += _dot_tn(xb, y_ref[...].astype(BF16))

        @pl.when(i == t // tm - 1)
        def _():
            for p, chunk in enumerate(chunk_fn(*accs)):
                o_ref[p % 2, p // 2] = chunk.astype(BF16)

    out = (2, N_DEV // 2) + tuple(chunk_shape)
    return pl.pallas_call(
        body, name=name, grid=(t // tm,),
        in_specs=[_rows(tm, m)] + [_rows(tm, y.shape[1]) for y in ys], out_specs=_acc(out),
        out_shape=jax.ShapeDtypeStruct(out, BF16), scratch_shapes=[pltpu.VMEM((m, y.shape[1]), F32) for y in ys],
        compiler_params=_cparams(),
    )(x, *ys)


def ffn_bwd(dh2, h1, g, gt, up, wgt, wut, wdt, *, tm, name):
    t = h1.shape[0]

    def body(dh2_ref, h_ref, g_ref, gt_ref, up_ref, wgt_ref, wut_ref, wdt_ref, dh1_ref, hn_ref, act_ref, dgt_ref, dup_ref, dg_ref):
        dh2 = dh2_ref[...]
        dact = _dot(dh2.astype(BF16), wdt_ref[...])
        gt = gt_ref[...].astype(F32)
        up = up_ref[...].astype(F32)
        sg = _sigmoid(gt)
        silu = gt * sg
        act_ref[...] = (silu * up).astype(BF16)
        dgt = (dact * up * (sg * (1.0 + gt * (1.0 - sg)))).astype(BF16)
        dup = (dact * silu).astype(BF16)
        dgt_ref[...] = dgt
        dup_ref[...] = dup
        dhn = _dot(dgt, wgt_ref[...]) + _dot(dup, wut_ref[...])
        xhat, r = _rms(h_ref[...])
        gg = g_ref[...]
        hn_ref[...] = (xhat * gg).astype(BF16)
        dx, dg = _rms_bwd(dhn, xhat, r, gg)
        dh1_ref[...] = dh2 + dx

        @pl.when(pl.program_id(0) == 0)
        def _():
            dg_ref[...] = jnp.zeros_like(dg_ref)

        dg_ref[...] += dg

    return pl.pallas_call(
        body, name=name, grid=(t // tm,),
        in_specs=[_rows(tm, D_MODEL), _rows(tm, D_MODEL), _whole((1, D_MODEL)), _rows(tm, D_FF), _rows(tm, D_FF),
                  _whole((D_FF, D_MODEL)), _whole((D_FF, D_MODEL)), _whole((D_MODEL, D_FF))],
        out_specs=[_rows(tm, D_MODEL), _rows(tm, D_MODEL), _rows(tm, D_FF), _rows(tm, D_FF), _rows(tm, D_FF), _acc((1, D_MODEL))],
        out_shape=[jax.ShapeDtypeStruct((t, D_MODEL), F32), jax.ShapeDtypeStruct((t, D_MODEL), BF16),
                   jax.ShapeDtypeStruct((t, D_FF), BF16), jax.ShapeDtypeStruct((t, D_FF), BF16),
                   jax.ShapeDtypeStruct((t, D_FF), BF16), jax.ShapeDtypeStruct((1, D_MODEL), F32)],
        compiler_params=_cparams(),
    )(dh2, h1, g, gt, up, wgt, wut, wdt)


def merge_bwd(dh1, z, pa, pb, pooled, pw, pwt, ps, wpat, wpbt, wot, *, tm, name):
    t = dh1.shape[0]

    def body(dh1_ref, ga_ref, gb_ref, pa_ref, pb_ref, pooled_ref, pw_ref, pwt_ref, ps_ref, wpat_ref, wpbt_ref, wot_ref,
             dga_ref, dgb_ref, dpa_ref, dpb_ref, do_ref, dpool_ref, dps_ref, dpw_ref):
        dmg = _dot(dh1_ref[...].astype(BF16), wot_ref[...])
        sa = _sigmoid(ga_ref[...])
        sb = _sigmoid(gb_ref[...])
        dga_ref[...] = (dmg * pa_ref[...].astype(F32) * sa * (1.0 - sa)).astype(BF16)
        dgb_ref[...] = (dmg * pb_ref[...].astype(F32) * sb * (1.0 - sb)).astype(BF16)
        dpa = (dmg * sa).astype(BF16)
        dpb = (dmg * sb).astype(BF16)
        dpa_ref[...] = dpa
        dpb_ref[...] = dpb
        do_ref[...] = _dot(dpb, wpbt_ref[...]).astype(BF16)
        da = _dot(dpa, wpat_ref[...])
        pooled = pooled_ref[...]
        ps = ps_ref[...]

        @pl.when(pl.program_id(0) == 0)
        def _():
            dps_ref[...] = jnp.zeros_like(dps_ref)
            dpw_ref[...] = jnp.zeros_like(dpw_ref)

        dps, dpool = [], []
        for g in range(len(POOL_WINDOWS)):
            gs = slice(g * POOL_GROUP, (g + 1) * POOL_GROUP)
            y = _dot(pooled[:, gs], pw_ref[g])
            dps.append(jnp.sum(da[:, gs] * y, axis=0, keepdims=True))
            dy = (da[:, gs] * ps[:, gs]).astype(BF16)
            dpool.append(_dot(dy, pwt_ref[g]))
            dpw_ref[g] += _dot_tn(pooled[:, gs], dy)
        dps_ref[...] += jnp.concatenate(dps, axis=1)
        dpool_ref[...] = jnp.concatenate(dpool, axis=1)

    return pl.pallas_call(
        body, name=name, grid=(t // tm,),
        in_specs=[_rows(tm, D_MODEL), _rows(tm, D_MODEL, 1), _rows(tm, D_MODEL, 2), _rows(tm, D_MODEL), _rows(tm, D_MODEL),
                  _rows(tm, POOL_WIDTH), _whole((4, POOL_GROUP, POOL_GROUP)), _whole((4, POOL_GROUP, POOL_GROUP)),
                  _whole((1, POOL_WIDTH)), _whole((D_MODEL, POOL_WIDTH)), _whole((D_MODEL, D_MODEL)), _whole((D_MODEL, D_MODEL))],
        out_specs=[_rows(tm, D_MODEL), _rows(tm, D_MODEL), _rows(tm, D_MODEL), _rows(tm, D_MODEL), _rows(tm, D_MODEL),
                   _rows(tm, POOL_WIDTH), _acc((1, POOL_WIDTH)), _acc((4, POOL_GROUP, POOL_GROUP))],
        out_shape=[jax.ShapeDtypeStruct((t, D_MODEL), BF16)] * 5
        + [jax.ShapeDtypeStruct((t, POOL_WIDTH), F32), jax.ShapeDtypeStruct((1, POOL_WIDTH), F32),
           jax.ShapeDtypeStruct((4, POOL_GROUP, POOL_GROUP), F32)],
        compiler_params=_cparams(),
    )(dh1, z, z, pa, pb, pooled, pw, pwt, ps, wpat, wpbt, wot)


def attn_bwd(q, k, v, o, do, lse, *, nb, lp, hb, name, ride=None):
    t = q.shape[0]
    nq = lp // TQ

    def body(q_ref, k_ref, v_ref, o_ref, do_ref, lse_ref, dq_ref, dk_ref, dv_ref, kt, doh, lse_row, delta_row, dqt):
        lane = lax.broadcasted_iota(jnp.int32, (lp, LANES), 1)
        first = lane < V_DIM
        sub = lax.broadcasted_iota(jnp.int32, (LANES, lp), 0)
        for pr in range(hb // 2):
            ls = slice(pr * LANES, (pr + 1) * LANES)
            do = do_ref[:, ls]
            doh[2 * pr] = jnp.where(first, do, jnp.zeros_like(do))
            doh[2 * pr + 1] = jnp.where(first, jnp.zeros_like(do), do)
            prod_t = (do.astype(F32) * o_ref[:, ls].astype(F32)).T
            delta_row[2 * pr] = jnp.sum(jnp.where(sub < V_DIM, prod_t, 0.0), axis=0, keepdims=True)
            delta_row[2 * pr + 1] = jnp.sum(jnp.where(sub < V_DIM, 0.0, prod_t), axis=0, keepdims=True)
            lse_t = lse_ref[:, ls].T * LOG2E
            lse_row[2 * pr] = lse_t[0:1, :]
            lse_row[2 * pr + 1] = lse_t[V_DIM:V_DIM + 1, :]
        for hd in range(hb):
            kt[hd] = k_ref[:, hd * HEAD_SLOT:(hd + 1) * HEAD_SLOT].T
        dqt[...] = jnp.zeros(dqt.shape, F32)
        keep = lax.broadcasted_iota(jnp.int32, (TK, TQ), 0) <= lax.broadcasted_iota(jnp.int32, (TK, TQ), 1)

        def k_block(kj, carry):
            ks = pl.multiple_of(kj * TK, TK)

            def q_step(qi, c, masked):
                qs = pl.multiple_of(qi * TQ, TQ)
                out = []
                for hd in range(hb):
                    dk, dv = c[hd]
                    hs = slice(hd * HEAD_SLOT, (hd + 1) * HEAD_SLOT)
                    pr = hd // 2
                    qh = q_ref[pl.ds(qs, TQ), hs]
                    st = _dot_nt(k_ref[pl.ds(ks, TK), hs], qh)
                    if masked:
                        st = jnp.where(keep, st, MASK_VALUE)
                    pt = jnp.exp2(st * EXP2_SCALE - lse_row[hd, :, pl.ds(qs, TQ)])
                    do_h = doh[hd, pl.ds(qs, TQ), :]
                    dpt = _dot_nt(v_ref[pl.ds(ks, TK), pr * LANES:(pr + 1) * LANES], do_h)
                    dst = (pt * (dpt - delta_row[hd, :, pl.ds(qs, TQ)])).astype(BF16)
                    dv = dv + _dot(pt.astype(BF16), do_h)
                    dk = dk + _dot(dst, qh)
                    dqt[hd, :, pl.ds(qs, TQ)] += _dot(kt[hd, :, pl.ds(ks, TK)], dst)
                    out.append((dk, dv))
                return tuple(out)

            zero = jnp.zeros((TK, LANES), F32)
            c = q_step(kj, tuple((zero, zero) for _ in range(hb)), True)
            c = lax.fori_loop(kj + 1, nq, functools.partial(q_step, masked=False), c)
            for hd in range(hb):
                dk_ref[pl.ds(ks, TK), hd * HEAD_SLOT:(hd + 1) * HEAD_SLOT] = c[hd][0] * SM_SCALE
            for pr in range(hb // 2):
                dv_ref[pl.ds(ks, TK), pr * LANES:(pr + 1) * LANES] = c[2 * pr][1] + c[2 * pr + 1][1]
            return carry

        lax.fori_loop(0, nq, k_block, 0)
        for hd in range(hb):
            dq_ref[:, hd * HEAD_SLOT:(hd + 1) * HEAD_SLOT] = dqt[hd].T * SM_SCALE

    blk = lambda w: pl.BlockSpec((lp, w), lambda b, g: (b, g))
    return _carrying_call(
        body, ride, (q, k, v, o, do, lse), name=name, grid=(nb, N_HEADS // hb),
        in_specs=[blk(hb * HEAD_SLOT), blk(hb * HEAD_SLOT), blk(hb * V_DIM), blk(hb * V_DIM), blk(hb * V_DIM), blk(hb * V_DIM)],
        out_specs=[blk(hb * HEAD_SLOT), blk(hb * HEAD_SLOT), blk(hb * V_DIM)],
        out_shape=[jax.ShapeDtypeStruct((t, QK_WIDTH), F32), jax.ShapeDtypeStruct((t, QK_WIDTH), F32),
                   jax.ShapeDtypeStruct((t, D_MODEL), F32)],
        scratch_shapes=[pltpu.VMEM((hb, HEAD_SLOT, lp), BF16), pltpu.VMEM((hb, lp, LANES), BF16), pltpu.VMEM((hb, 1, lp), F32),
                        pltpu.VMEM((hb, 1, lp), F32), pltpu.VMEM((hb, HEAD_SLOT, lp), F32)])


def in_proj_bwd(dh1, h, g_mix, z, dq, dk, dv, dga, dgb, dpool, wint, gq, gkv, wuqt, wukt, wuvt, rope, *, tm, lp, nb, name):
    t = h.shape[0]
    hb = tm // POOL_HALO
    last_halo = t // POOL_HALO - 1

    def body(dh1_ref, h_ref, g_ref, zcq_ref, zckv_ref, dq_ref, dk_ref, dv_ref, dga_ref, dgb_ref, dpool_ref, dnext_ref,
             wint_ref, gq_ref, gkv_ref, wuqt_ref, wukt_ref, wuvt_ref, rope_ref,
             dh_ref, hn_ref, dz_ref, cqn_ref, ckvn_ref, dqb_ref, dkb_ref, dvb_ref, dg_ref, dgq_ref, dgkv_ref):
        i = pl.program_id(0)
        rope_t = rope_ref[...]
        dqb = _rope_bwd(dq_ref[...], *_rope_tables(rope_t, N_HEADS)).astype(BF16)
        dqb_ref[...] = dqb
        xq, rq = _rms(zcq_ref[...])
        gq_v = gq_ref[...]
        cqn_ref[...] = (xq * gq_v).astype(BF16)
        dcq, dgq = _rms_bwd(_dot(dqb, wuqt_ref[...]), xq, rq, gq_v)
        dk = dk_ref[...]
        dkb = dk.astype(BF16)
        dvb = dv_ref[...].astype(BF16)
        dkb_ref[...] = dkb
        dvb_ref[...] = dvb
        xkv, rkv = _rms(zckv_ref[...])
        gkv_v = gkv_ref[...]
        ckvn_ref[...] = (xkv * gkv_v).astype(BF16)
        dckv, dgkv = _rms_bwd(_dot(dkb, wukt_ref[...]) + _dot(dvb, wuvt_ref[...]), xkv, rkv, gkv_v)
        dks = dk[:, :HEAD_SLOT]
        for hd in range(1, N_HEADS):
            dks = dks + dk[:, hd * HEAD_SLOT:(hd + 1) * HEAD_SLOT]
        dzk = _rope_bwd(dks, *_rope_tables(rope_t, 1))
        dp_cur = dpool_ref[...]
        dp_ext = jnp.concatenate([dp_cur, dnext_ref[...]], axis=0)
        r = lax.broadcasted_iota(jnp.int32, (tm, tm + POOL_HALO), 0)
        e = lax.broadcasted_iota(jnp.int32, (tm, tm + POOL_HALO), 1)
        gt_col = i * tm + lax.broadcasted_iota(jnp.int32, (1, tm + POOL_HALO), 1)
        pos_col = _seq_pos(gt_col, lp, nb)
        gt_row = i * tm + lax.broadcasted_iota(jnp.int32, (tm + POOL_HALO, 1), 0)
        pos_row = _seq_pos(gt_row, lp, nb)
        dus = []
        for g, w in enumerate(POOL_WINDOWS):
            gs = slice(g * POOL_GROUP, (g + 1) * POOL_GROUP)
            band = jnp.where((e - r >= 0) & (e - r < jnp.minimum(pos_col + 1, w)) & (gt_col < t), 1.0, 0.0).astype(BF16)
            scaled = jnp.where(gt_row < t, dp_ext[:, gs] / jnp.minimum(pos_row + 1, w).astype(F32), 0.0).astype(BF16)
            dus.append(_dot(band, scaled) - dp_cur[:, gs])
        dz = jnp.concatenate(dus + [dcq, dckv, dzk], axis=1).astype(BF16)
        dz = jnp.concatenate([dz, dga_ref[...], dgb_ref[...]], axis=1)
        dz_ref[...] = dz
        xhat, rr = _rms(h_ref[...])
        gg = g_ref[...]
        hn_ref[...] = (xhat * gg).astype(BF16)
        dx, dg = _rms_bwd(_dot(dz, wint_ref[...]), xhat, rr, gg)
        dh_ref[...] = dh1_ref[...] + dx

        @pl.when(i == 0)
        def _():
            dg_ref[...] = jnp.zeros_like(dg_ref)
            dgq_ref[...] = jnp.zeros_like(dgq_ref)
            dgkv_ref[...] = jnp.zeros_like(dgkv_ref)

        dg_ref[...] += dg
        dgq_ref[...] += dgq
        dgkv_ref[...] += dgkv

    nxt = pl.BlockSpec((POOL_HALO, POOL_WIDTH), lambda i: (jnp.minimum((i + 1) * hb, last_halo), 0))
    return pl.pallas_call(
        body, name=name, grid=(t // tm,),
        in_specs=[_rows(tm, D_MODEL), _rows(tm, D_MODEL), _whole((1, D_MODEL)), _rows(tm, Q_RANK, Z_CQ // Q_RANK),
                  _rows(tm, KV_RANK, Z_CKV // KV_RANK), _rows(tm, QK_WIDTH), _rows(tm, QK_WIDTH), _rows(tm, D_MODEL),
                  _rows(tm, D_MODEL), _rows(tm, D_MODEL), _rows(tm, POOL_WIDTH), nxt,
                  _whole((DZ, D_MODEL)), _whole((1, Q_RANK)), _whole((1, KV_RANK)), _whole((QK_WIDTH, Q_RANK)),
                  _whole((QK_WIDTH, KV_RANK)), _whole((D_MODEL, KV_RANK)), _rows(tm, 4 * LANES)],
        out_specs=[_rows(tm, D_MODEL), _rows(tm, D_MODEL), _rows(tm, DZ), _rows(tm, Q_RANK), _rows(tm, KV_RANK),
                   _rows(tm, QK_WIDTH), _rows(tm, QK_WIDTH), _rows(tm, D_MODEL),
                   _acc((1, D_MODEL)), _acc((1, Q_RANK)), _acc((1, KV_RANK))],
        out_shape=[jax.ShapeDtypeStruct((t, D_MODEL), F32), jax.ShapeDtypeStruct((t, D_MODEL), BF16),
                   jax.ShapeDtypeStruct((t, DZ), BF16), jax.ShapeDtypeStruct((t, Q_RANK), BF16),
                   jax.ShapeDtypeStruct((t, KV_RANK), BF16), jax.ShapeDtypeStruct((t, QK_WIDTH), BF16),
                   jax.ShapeDtypeStruct((t, QK_WIDTH), BF16), jax.ShapeDtypeStruct((t, D_MODEL), BF16),
                   jax.ShapeDtypeStruct((1, D_MODEL), F32), jax.ShapeDtypeStruct((1, Q_RANK), F32),
                   jax.ShapeDtypeStruct((1, KV_RANK), F32)],
        compiler_params=_cparams(),
    )(dh1, h, g_mix, z, z, dq, dk, dv, dga, dgb, dpool, dpool, wint, gq, gkv, wuqt, wukt, wuvt, rope)


_MESH = pl.DeviceIdType.MESH


def _place():
    x, y, c = lax.axis_index("x"), lax.axis_index("y"), lax.axis_index("c")
    return x, y, c, 4 * x + 2 * y + c


def _peer(x, y, c, k):
    px, py, pc = (1 - x) if k & 4 else x, (1 - y) if k & 2 else y, (1 - c) if k & 1 else c
    return (px, py, pc), 4 * px + 2 * py + pc


ALL_PEERS = tuple(range(1, N_DEV))
CHIP_PEERS = (2, 4, 6)
N_CHIPS = N_DEV // 2


def _sem_scratch(n, m):
    return [pltpu.SemaphoreType.DMA((n, m)), pltpu.SemaphoreType.DMA((n, m)), pltpu.SemaphoreType.DMA((n,))]


class Exchange:
    def __init__(self, arrays, out_shapes, sem_cols, plan, aliased=False):
        self.arrays, self.out_shapes, self.plan = list(arrays), list(out_shapes), plan
        self.scratch = _sem_scratch(len(self.arrays), sem_cols)
        self.aliased = aliased

    def split(self, refs):
        n = len(self.arrays)
        return refs[:n], refs[n:2 * n], refs[2 * n:]

    def start(self, srcs, dsts, sems):
        local, sends, _ = self.plan(srcs, dsts, *sems)
        for cp in local + sends:
            cp.start()

    def wait(self, srcs, dsts, sems):
        local, sends, recvs = self.plan(srcs, dsts, *sems)
        for cp in recvs:
            cp.wait_recv()
        for cp in sends:
            cp.wait_send()
        for cp in local:
            cp.wait()

    def aliases(self, first_in, first_out):
        return {first_in + j: first_out + j for j in range(len(self.arrays))} if self.aliased else {}

    def run(self, name):
        def body(*refs):
            srcs, dsts, sems = self.split(refs)
            self.start(srcs, dsts, sems)
            self.wait(srcs, dsts, sems)

        n = len(self.arrays)
        return pl.pallas_call(body, name=name, in_specs=[_ANY] * n, out_specs=[_ANY] * n, out_shape=self.out_shapes,
                              scratch_shapes=self.scratch, input_output_aliases=self.aliases(0, 0))(*self.arrays)


def exchange(arrays, scatter, peers, by_chip=False):
    slots = N_CHIPS if by_chip else N_DEV

    def plan(srcs, dsts, send_sems, recv_sems, local_sems):
        x, y, c, me = _place()
        mine = 2 * x + y if by_chip else me
        local = [pltpu.make_async_copy(src.at[mine] if scatter else src, dst.at[mine], local_sems.at[j])
                 for j, (src, dst) in enumerate(zip(srcs, dsts))]
        sends, recvs = [], []
        for t, k in enumerate(peers):
            peer, pidx = _peer(x, y, c, k)
            theirs = 2 * peer[0] + peer[1] if by_chip else pidx
            for j, (src, dst) in enumerate(zip(srcs, dsts)):
                part = src.at[theirs] if scatter else src
                sems = dict(send_sem=send_sems.at[j, t], recv_sem=recv_sems.at[j, t], device_id=peer, device_id_type=_MESH)
                sends.append(pltpu.make_async_remote_copy(src_ref=part, dst_ref=dst.at[mine], **sems))
                recvs.append(pltpu.make_async_remote_copy(src_ref=part, dst_ref=dst.at[theirs], **sems))
        return local, sends, recvs

    shapes = [jax.ShapeDtypeStruct(a.shape if scatter else (slots,) + a.shape, a.dtype) for a in arrays]
    return Exchange(arrays, shapes, len(peers), plan)


def second_hop(gathered):
    def plan(srcs, dsts, send_sems, recv_sems, local_sems):
        x, y, c, me = _place()
        sibling, _ = _peer(x, y, c, 1)
        sends, recvs = [], []
        for t, k in enumerate(CHIP_PEERS):
            _, landed = _peer(x, y, c, k)
            _, coming = _peer(x, y, c, k ^ 1)
            for j, buf in enumerate(dsts):
                sems = dict(send_sem=send_sems.at[j, t], recv_sem=recv_sems.at[j, t], device_id=sibling, device_id_type=_MESH)
                sends.append(pltpu.make_async_remote_copy(src_ref=buf.at[landed], dst_ref=buf.at[landed], **sems))
                recvs.append(pltpu.make_async_remote_copy(src_ref=buf.at[coming], dst_ref=buf.at[coming], **sems))
        return [], sends, recvs

    shapes = [jax.ShapeDtypeStruct(a.shape, a.dtype) for a in gathered]
    return Exchange(gathered, shapes, len(CHIP_PEERS), plan, aliased=True)


FIRST_HOP_PEERS = (1,) + CHIP_PEERS


def _gather_two_level(arrays, name):
    n = len(arrays)

    def body(*refs):
        srcs, dsts, (send_sems, recv_sems, local_sems) = refs[:n], refs[n:2 * n], refs[2 * n:]
        x, y, c, me = _place()
        sibling, sidx = _peer(x, y, c, 1)

        def copy(j, sem, block, to, src=None):
            rows = dsts[j].at[block]
            return pltpu.make_async_remote_copy(src_ref=rows if src is None else src, dst_ref=rows, send_sem=send_sems.at[j, sem],
                                                recv_sem=recv_sems.at[j, sem], device_id=to, device_id_type=_MESH)

        local = [pltpu.make_async_copy(srcs[j], dsts[j].at[me], local_sems.at[j]) for j in range(n)]
        for cp in local:
            cp.start()
        first = [copy(j, 1 + t, me, _peer(x, y, c, k)[0], src=srcs[j]) for t, k in enumerate(CHIP_PEERS) for j in range(n)]
        first += [copy(j, 0, me, sibling, src=srcs[j]) for j in range(n)]
        for cp in first:
            cp.start()
        passed = []
        for t, k in enumerate(CHIP_PEERS):
            peer, pidx = _peer(x, y, c, k)
            for j in range(n):
                copy(j, 1 + t, pidx, peer).wait_recv()
                passed.append(copy(j, 4 + t, pidx, sibling))
                passed[-1].start()
        for j in range(n):
            copy(j, 0, sidx, sibling).wait_recv()
        for t, k in enumerate(CHIP_PEERS):
            _, pidx = _peer(x, y, c, k ^ 1)
            for j in range(n):
                copy(j, 4 + t, pidx, sibling).wait_recv()
        for cp in first + passed:
            cp.wait_send()
        for cp in local:
            cp.wait()

    shapes = [jax.ShapeDtypeStruct((N_DEV,) + a.shape, a.dtype) for a in arrays]
    return pl.pallas_call(body, name=name, in_specs=[_ANY] * n, out_specs=[_ANY] * n, out_shape=shapes,
                          scratch_shapes=_sem_scratch(n, 1 + 2 * len(CHIP_PEERS)))(*arrays)


def _to_sibling(arrays, name):
    n = len(arrays)

    def body(*refs):
        srcs, dsts, (send_sems, recv_sems) = refs[:n], refs[n:2 * n], refs[2 * n:]
        x, y, c, _ = _place()
        sibling, _ = _peer(x, y, c, 1)
        copies = [pltpu.make_async_remote_copy(src_ref=srcs[j].at[1 - c], dst_ref=dsts[j], send_sem=send_sems.at[j],
                                               recv_sem=recv_sems.at[j], device_id=sibling, device_id_type=_MESH) for j in range(n)]
        for cp in copies:
            cp.start()
        for cp in copies:
            cp.wait()

    shapes = [jax.ShapeDtypeStruct(a.shape[1:], a.dtype) for a in arrays]
    return pl.pallas_call(body, name=name, in_specs=[_ANY] * n, out_specs=[_ANY] * n, out_shape=shapes,
                          scratch_shapes=[pltpu.SemaphoreType.DMA((n,)), pltpu.SemaphoreType.DMA((n,))])(*arrays)


def pair_add(own, theirs, core, *, name):
    _, ns, r, c = own.shape
    rb = _row_block(r, c)

    def body(core_ref, a_ref, b_ref, o_ref):
        o_ref[...] = (a_ref[...].astype(F32) + b_ref[...].astype(F32)).astype(o_ref.dtype)

    return pl.pallas_call(
        body, name=name,
        grid_spec=pltpu.PrefetchScalarGridSpec(
            num_scalar_prefetch=1, grid=(ns, r // rb),
            in_specs=[pl.BlockSpec((None, None, rb, c), lambda i, j, core_ref: (core_ref[0], i, j, 0)),
                      pl.BlockSpec((None, rb, c), lambda i, j, core_ref: (i, j, 0))],
            out_specs=pl.BlockSpec((None, rb, c), lambda i, j, core_ref: (i, j, 0))),
        out_shape=jax.ShapeDtypeStruct((ns, r, c), own.dtype), compiler_params=_cparams(),
    )(core, own, theirs)


ADAMW_BLOCK_BYTES = 1 << 20


def _row_block(r, c):
    for rb in range(r, 0, -1):
        if r % rb == 0 and (rb % 16 == 0 or rb == r) and rb * c * 4 <= ADAMW_BLOCK_BYTES:
            return rb
    return r


def adamw(w, m, v, parts, *, name):
    depth, r, c = w.shape
    n_parts = parts[0].shape[0]
    rb = _row_block(r, c)

    def body(w_ref, m_ref, v_ref, *refs):
        p_refs, (g_ref, d_ref, nm_ref, nv_ref) = refs[:depth], refs[depth:]

        def total(p_ref):
            g = p_ref[0].astype(F32)
            for j in range(1, n_parts):
                g = g + p_ref[j].astype(F32)
            return g

        g = total(p_refs[0])
        for l in range(1, depth):
            g = jnp.where(pl.program_id(0) == l, total(p_refs[l]), g)
        g_ref[...] = g
        m_new = ADAM_B1 * m_ref[...] + (1.0 - ADAM_B1) * g
        v_new = ADAM_B2 * v_ref[...] + (1.0 - ADAM_B2) * (g * g)
        m_hat = m_new / (1.0 - ADAM_B1 ** ADAM_STEP)
        v_hat = v_new / (1.0 - ADAM_B2 ** ADAM_STEP)
        d_ref[...] = -ADAM_LR * (m_hat / (jnp.sqrt(v_hat) + ADAM_EPS) + ADAM_WD * w_ref[...])
        nm_ref[...] = m_new
        nv_ref[...] = v_new

    wblk = pl.BlockSpec((None, rb, c), lambda l, i: (l, i, 0))
    pblk = pl.BlockSpec((n_parts, rb, c), lambda l, i: (0, i, 0))
    return pl.pallas_call(
        body, name=name, grid=(depth, r // rb),
        in_specs=[wblk, wblk, wblk] + [pblk] * depth, out_specs=[wblk] * 4,
        out_shape=[jax.ShapeDtypeStruct((depth, r, c), F32)] * 4, compiler_params=_cparams(),
    )(w, m, v, *parts)


BIG = (("w_in", 2), ("w_uq", 2), ("w_ukv", 2), ("w_pa", 2), ("w_pb", 1), ("w_o", 1), ("w_gate", 2), ("w_up", 2), ("w_down", 1))
SMALL = ("norm_mix_g", "pool_w", "pool_scale", "q_norm_g", "kv_norm_g", "norm_ffn_g", "final_norm_g")
WEIGHTS = ("meta_tokens", "norm_mix_g", "w_in", "pool_w", "pool_scale", "q_norm_g", "kv_norm_g", "w_uq", "w_ukv", "w_pa", "w_pb",
           "w_o", "norm_ffn_g", "w_gate", "w_up", "w_down", "final_norm_g")
HEAD_QK = QK_NOPE + QK_ROPE
KR_END = Z_KR + QK_ROPE


def _cat_cols(parts):
    return [jnp.concatenate(parts, axis=1)]


def _cat_rows(parts):
    return [jnp.concatenate(parts, axis=0)]


def _arr_w_in(parts):
    full = jnp.concatenate(parts, axis=1)
    zc = lambda n: jnp.zeros((full.shape[0], n), full.dtype)
    return [jnp.concatenate([full[:, :Z_KR], zc(QK_NOPE), full[:, Z_KR:KR_END], zc(LANES - HEAD_QK), full[:, KR_END:]], axis=1)]


def _arr_w_uq(parts):
    full = jnp.concatenate(parts, axis=1)
    z = jnp.zeros((full.shape[0], HEAD_SLOT - HEAD_QK), full.dtype)
    pieces = []
    for hd in range(N_HEADS):
        pieces += [full[:, hd * HEAD_QK:(hd + 1) * HEAD_QK], z]
    return [jnp.concatenate(pieces, axis=1)]


def _arr_w_ukv(parts):
    full = jnp.concatenate(parts, axis=1)
    z = jnp.zeros((full.shape[0], HEAD_SLOT - QK_NOPE), full.dtype)
    wide = QK_NOPE + V_DIM
    k, v = [], []
    for hd in range(N_HEADS):
        k += [full[:, hd * wide:hd * wide + QK_NOPE], z]
        v.append(full[:, hd * wide + QK_NOPE:(hd + 1) * wide])
    return [jnp.concatenate(k, axis=1), jnp.concatenate(v, axis=1)]


def arrange(g, fn, out_shapes, name):
    def body(g_ref, *o_refs):
        for o_ref, val in zip(o_refs, fn([g_ref[p] for p in range(N_DEV)])):
            o_ref[...] = val

    return pl.pallas_call(
        body, name=name, grid=(1,),
        in_specs=[pl.BlockSpec(g.shape, lambda i: (0, 0, 0))],
        out_specs=[pl.BlockSpec(s, lambda i: (0, 0)) for s in out_shapes],
        out_shape=[jax.ShapeDtypeStruct(s, g.dtype) for s in out_shapes], compiler_params=_cparams(),
    )(g)


def _arranged_ranges(lo, hi):
    out = []
    for a, b, shift in ((0, Z_KR, 0), (Z_KR, KR_END, QK_NOPE), (KR_END, D_IN, LANES - QK_ROPE)):
        s, e = max(lo, a), min(hi, b)
        if s < e:
            out.append((s + shift, e + shift))
    return out


def _chunks_w_in(acc):
    cs = D_IN // N_DEV
    return [jnp.concatenate([acc[:, a:b] for a, b in _arranged_ranges(p * cs, (p + 1) * cs)], axis=1) for p in range(N_DEV)]


def _chunks_w_uq(acc):
    per = N_HEADS // N_DEV
    return [jnp.concatenate([acc[:, hd * HEAD_SLOT:hd * HEAD_SLOT + HEAD_QK] for hd in range(p * per, (p + 1) * per)], axis=1)
            for p in range(N_DEV)]


def _chunks_w_ukv(acc_k, acc_v):
    per = N_HEADS // N_DEV
    out = []
    for p in range(N_DEV):
        pieces = []
        for hd in range(p * per, (p + 1) * per):
            pieces += [acc_k[:, hd * HEAD_SLOT:hd * HEAD_SLOT + QK_NOPE], acc_v[:, hd * V_DIM:(hd + 1) * V_DIM]]
        out.append(jnp.concatenate(pieces, axis=1))
    return out


def _chunks_cols(acc):
    cs = acc.shape[1] // N_DEV
    return [acc[:, p * cs:(p + 1) * cs] for p in range(N_DEV)]


def _chunks_rows(acc):
    rs = acc.shape[0] // N_DEV
    return [acc[p * rs:(p + 1) * rs, :] for p in range(N_DEV)]


def _pack(parts, row_multiple):
    flat = jnp.concatenate([p.reshape(-1) for p in parts])
    return jnp.pad(flat, (0, -flat.shape[0] % (row_multiple * LANES))).reshape(-1, LANES)


def _unpack(packed, shapes):
    flat, out, off = packed.reshape(-1), [], 0
    for s in shapes:
        n = 1
        for d in s:
            n *= d
        out.append(flat[off:off + n].reshape(s))
        off += n
    return out


def _rope_table(lp, nb):
    inv = 1.0 / (ROPE_THETA ** (jnp.arange(0, QK_ROPE, 2, dtype=F32) / QK_ROPE))
    ang = jnp.arange(lp, dtype=F32)[:, None] * inv[None, :]
    cos, sin = jnp.cos(ang), jnp.sin(ang)
    z = lambda n: jnp.zeros((lp, n), F32)
    tail = LANES - QK_NOPE - QK_ROPE
    c = jnp.concatenate([jnp.ones((lp, QK_NOPE), F32), cos, cos, z(tail)], axis=1)
    cr = jnp.concatenate([z(QK_NOPE), cos, cos, z(tail)], axis=1)
    s1 = jnp.concatenate([z(QK_NOPE), -sin, z(HALF_ROPE), z(tail)], axis=1)
    s2 = jnp.concatenate([z(QK_NOPE), z(HALF_ROPE), sin, z(tail)], axis=1)
    return jnp.tile(jnp.concatenate([c, cr, s1, s2], axis=1), (nb, 1))


MIX = ("w_in", "w_uq", "w_ukv", "w_pa", "w_pb", "w_o")
FFN = ("w_gate", "w_up", "w_down")
ARRANGERS = {
    "w_in": (_arr_w_in, (("win", (D_MODEL, DZ)),)), "w_uq": (_arr_w_uq, (("wuq", (Q_RANK, QK_WIDTH)),)),
    "w_ukv": (_arr_w_ukv, (("wuk", (KV_RANK, QK_WIDTH)), ("wuv", (KV_RANK, D_MODEL)))),
    "w_pa": (_cat_cols, (("wpa", (POOL_WIDTH, D_MODEL)),)), "w_pb": (_cat_rows, (("wpb", (D_MODEL, D_MODEL)),)),
    "w_o": (_cat_rows, (("wo", (D_MODEL, D_MODEL)),)), "w_gate": (_cat_cols, (("wg", (D_MODEL, D_FF)),)),
    "w_up": (_cat_cols, (("wu", (D_MODEL, D_FF)),)), "w_down": (_cat_rows, (("wd", (D_FF, D_MODEL)),)),
}


def _operands(gathered, names, l):
    p = {}
    for n in names:
        fn, outs = ARRANGERS[n]
        for (key, _), a in zip(outs, arrange(gathered[n], fn, [s for _, s in outs], f"arrange_{n}_{l}")):
            p[key], p[key + "t"] = a, a.T
    return p


def _small_operands(small, l):
    pw = small["pool_w"][l].astype(BF16)
    return dict(g_mix=small["norm_mix_g"][l][None], gq=small["q_norm_g"][l][None], gkv=small["kv_norm_g"][l][None],
                g_ffn=small["norm_ffn_g"][l][None], ps=small["pool_scale"][l][None], pw=pw, pwt=jnp.swapaxes(pw, 1, 2))


TM_FWD, TM_BWD, TM_WGRAD = 512, 256, 512
HEADS_FWD, HEADS_BWD = 8, 4


class MeshComm:
    def __init__(self, w, meta_tokens):
        self.src = lambda n, l: w[n][l].astype(BF16)
        self.meta_tokens = meta_tokens
        self.core = lax.axis_index("c").astype(jnp.int32).reshape(1)
        self.later = [(n, 0) for n in FFN] + [(n, 1) for n in MIX + FFN]

    def first_weights(self):
        got = _gather_two_level([self.src(n, 0) for n in MIX] + [self.meta_tokens], "gather_mix_0")
        return dict(zip(MIX, got)), jnp.moveaxis(got[-1], 0, 1).reshape(N_META, D_MODEL)

    def later_first_hop(self):
        return exchange([self.src(n, l) for n, l in self.later], False, FIRST_HOP_PEERS)

    def later_second_hop(self, landed):
        return second_hop(landed)

    def later_weights(self, full, names, l):
        return {n: full[self.later.index((n, l))] for n in names}

    def pair_sums(self, own, names, tag):
        theirs = _to_sibling(own, f"pair_grads_{tag}")
        return [pair_add(a, b, self.core, name=f"pair_add_{n}_{tag}") for n, a, b in zip(names, own, theirs)]

    def scatter(self, sums):
        return exchange(sums, True, CHIP_PEERS, by_chip=True)

    def scatter_now(self, sums, name):
        return self.scatter(sums).run(name)


TM_FWD, TM_BWD, TM_WGRAD = 512, 256, 512
HEADS_FWD, HEADS_BWD = 8, 4


def _ffn_bwd_part(dh2, p, s, tag):
    d, ff = D_MODEL, D_FF // N_DEV
    wg_ = lambda n, x, ys, fn, shape: wgrad(x, ys, fn, shape, tm=TM_WGRAD, name=f"wgrad_{n}_{tag}")
    dh1, hn2, act, dgt, dup, dg_ffn = ffn_bwd(dh2, s["h1"], p["g_ffn"], s["gt"], s["up"], p["wgt"], p["wut"], p["wdt"], tm=TM_BWD,
                                              name=f"ffn_bwd_{tag}")
    chunks = [wg_("gate", hn2, [dgt], _chunks_cols, (d, ff)), wg_("up", hn2, [dup], _chunks_cols, (d, ff)),
              wg_("down", act, [dh2], _chunks_rows, (ff, d))]
    return dh1, chunks, dict(norm_ffn_g=dg_ffn[0])


def _mix_bwd_part(dh1, p, s, rope, nb, lp, tag, ride):
    d = D_MODEL
    wg_ = lambda n, x, ys, fn, shape: wgrad(x, ys, fn, shape, tm=TM_WGRAD, name=f"wgrad_{n}_{tag}")
    dga, dgb, dpa, dpb, do, dpool, dps, dpw = merge_bwd(dh1, s["z"], s["pa"], s["pb"], s["pooled"], p["pw"], p["pwt"], p["ps"],
                                                        p["wpat"], p["wpbt"], p["wot"], tm=TM_FWD, name=f"merge_bwd_{tag}")
    c_o = wg_("o", s["mg"], [dh1], _chunks_rows, (d // N_DEV, d))
    c_pa = wg_("pa", s["a"], [dpa], _chunks_cols, (POOL_WIDTH, d // N_DEV))
    c_pb = wg_("pb", s["o"], [dpb], _chunks_rows, (d // N_DEV, d))
    (dq, dk, dv), brought = attn_bwd(s["q"], s["k"], s["v"], s["o"], do, s["lse"], nb=nb, lp=lp, hb=HEADS_BWD,
                                     name=f"attn_bwd_{tag}", ride=ride)
    dh, hn, dz, cqn, ckvn, dqb, dkb, dvb, dg_mix, dgq, dgkv = in_proj_bwd(
        dh1, s["h"], p["g_mix"], s["z"], dq, dk, dv, dga, dgb, dpool, p["wint"], p["gq"], p["gkv"], p["wuqt"], p["wukt"], p["wuvt"],
        rope, tm=TM_BWD, lp=lp, nb=nb, name=f"in_proj_bwd_{tag}")
    c_in = wg_("in", hn, [dz], _chunks_w_in, (d, D_IN // N_DEV))
    c_uq = wg_("uq", cqn, [dqb], _chunks_w_uq, (Q_RANK, N_HEADS * HEAD_QK // N_DEV))
    c_ukv = wg_("ukv", ckvn, [dkb, dvb], _chunks_w_ukv, (KV_RANK, N_HEADS * (QK_NOPE + V_DIM) // N_DEV))
    small = dict(pool_scale=dps[0], pool_w=dpw, norm_mix_g=dg_mix[0], q_norm_g=dgq[0], kv_norm_g=dgkv[0])
    return dh, [c_in, c_uq, c_ukv, c_pa, c_pb, c_o], small, brought


def train_step(x, loss_target, small, comm):
    nb, seq, d = x.shape
    lp = -(-(N_META + seq) // TQ) * TQ
    t = nb * lp
    assert t % TM_FWD == 0 and nb <= 2 and DEPTH == 2
    rope = _rope_table(lp, nb)
    gathered, meta = comm.first_weights()
    pad = jnp.zeros((nb, lp - N_META - seq, d), F32)
    h = jnp.concatenate([jnp.broadcast_to(meta[None], (nb, N_META, d)), x, pad], axis=1).reshape(t, d)
    target = jnp.concatenate([jnp.zeros((nb, N_META, d), F32), loss_target, pad], axis=1).reshape(t, d)

    params, saved = [], []
    for l in range(DEPTH):
        p = _small_operands(small, l)
        if l == 0:
            p.update(_operands(gathered, MIX, 0))
        else:
            p.update(_operands(comm.later_weights(full, MIX + FFN, 1), MIX + FFN, 1))
        z, q, k, v = in_proj_fwd(h, p["g_mix"], p["win"], p["gq"], p["gkv"], p["wuq"], p["wuk"], p["wuv"], rope, tm=TM_FWD,
                                 name=f"in_proj_fwd_{l}")
        (o, lse), landed = attn_fwd(q, k, v, nb=nb, lp=lp, hb=HEADS_FWD, name=f"attn_fwd_{l}",
                                    ride=comm.later_first_hop() if l == 0 else None)
        (h1, pooled, a, pa, pb, mg), got = merge_fwd(h, z, o, p["pw"], p["ps"], p["wpa"], p["wpb"], p["wo"], tm=TM_FWD, lp=lp,
                                                      nb=nb, name=f"merge_fwd_{l}",
                                                      ride=comm.later_second_hop(landed) if l == 0 else None)
        if l == 0:
            full = got
            p.update(_operands(comm.later_weights(full, FFN, 0), FFN, 0))
        h2, gt, up = ffn_fwd(h1, p["g_ffn"], p["wg"], p["wu"], p["wd"], tm=TM_FWD, name=f"ffn_fwd_{l}")
        params.append(p)
        saved.append(dict(h=h, z=z, q=q, k=k, v=v, o=o, lse=lse, h1=h1, pooled=pooled, a=a, pa=pa, pb=pb, mg=mg, gt=gt, up=up))
        h = h2
    parts, dh, dgf = loss_head(h, small["final_norm_g"][None], target, tm=TM_FWD, lp=lp, nb=nb, seq=seq, name="loss_head")
    loss = jnp.sum(parts[::8, 0])

    sums = {}
    dh, c_ffn1, small1 = _ffn_bwd_part(dh, params[1], saved[1], 1)
    s_ffn1 = comm.pair_sums(c_ffn1, FFN, "ffn_1")
    dh, c_mix1, sm, brought = _mix_bwd_part(dh, params[1], saved[1], rope, nb, lp, 1, comm.scatter(s_ffn1))
    small1.update(sm)
    sums.update({(n, 1): a for n, a in zip(FFN, brought)})
    s_mix1 = comm.pair_sums(c_mix1, MIX, "mix_1")
    dh, c_ffn0, small0 = _ffn_bwd_part(dh, params[0], saved[0], 0)
    s_ffn0 = comm.pair_sums(c_ffn0, FFN, "ffn_0")
    dh, c_mix0, sm, brought = _mix_bwd_part(dh, params[0], saved[0], rope, nb, lp, 0, comm.scatter(s_mix1 + s_ffn0))
    small0.update(sm)
    sums.update({(n, l): a for (n, l), a in zip([(n, 1) for n in MIX] + [(n, 0) for n in FFN], brought)})
    dh = dh.reshape(nb, lp, d)
    dmeta = jnp.sum(dh[:, :N_META], axis=0)
    meta_chunks = jnp.transpose(dmeta.reshape(N_META, N_CHIPS, 2, d // N_DEV), (2, 1, 0, 3)).astype(BF16)
    s_last = comm.pair_sums(c_mix0 + [meta_chunks], MIX + ("meta_tokens",), "mix_0")
    last = comm.scatter_now(s_last, "scatter_mix_0")
    sums.update({(n, 0): a for n, a in zip(MIX + ("meta_tokens",), last)})
    small_grads = {n: jnp.stack([small0[n], small1[n]]) for n in small0}
    small_grads["final_norm_g"] = dgf[0]
    return loss, dh[:, N_META:N_META + seq], sums, small_grads


def kernel(x, meta_tokens, norm_mix_g, w_in, pool_w, pool_scale, q_norm_g, kv_norm_g, w_uq, w_ukv, w_pa, w_pb, w_o, norm_ffn_g, w_gate, w_up, w_down, final_norm_g, loss_target, m_meta_tokens, m_norm_mix_g, m_w_in, m_pool_w, m_pool_scale, m_q_norm_g, m_kv_norm_g, m_w_uq, m_w_ukv, m_w_pa, m_w_pb, m_w_o, m_norm_ffn_g, m_w_gate, m_w_up, m_w_down, m_final_norm_g, v_meta_tokens, v_norm_mix_g, v_w_in, v_pool_w, v_pool_scale, v_q_norm_g, v_kv_norm_g, v_w_uq, v_w_ukv, v_w_pa, v_w_pb, v_w_o, v_norm_ffn_g, v_w_gate, v_w_up, v_w_down, v_final_norm_g):
    args = dict(locals())
    w = {n: args[n] for n in WEIGHTS}
    m = {n: args["m_" + n] for n in WEIGHTS}
    v = {n: args["v_" + n] for n in WEIGHTS}
    small = {n: w[n] for n in SMALL}

    loss, grad_x, sums, small_grads = train_step(x, loss_target, small, MeshComm(w, meta_tokens))
    loss = lax.psum(loss, ("x", "y", "c"))
    (small_recv,) = exchange([_pack([small_grads[n] for n in SMALL], 8)], False, ALL_PEERS).run("gather_small_grads")

    out = {n: adamw(w[n], m[n], v[n], [sums[(n, l)] for l in range(DEPTH)], name=f"adamw_{n}") for n, _ in BIG}
    out["meta_tokens"] = [a[0] for a in adamw(meta_tokens[None], m["meta_tokens"][None], v["meta_tokens"][None],
                                              [sums[("meta_tokens", 0)]], name="adamw_meta_tokens")]
    pk = lambda d: _pack([d[n] for n in SMALL], 8)[None]
    packed = adamw(pk(w), pk(m), pk(v), [small_recv], name="adamw_small")
    shapes = [w[n].shape for n in SMALL]
    for n, *kinds in zip(SMALL, *[_unpack(packed[kind][0], shapes) for kind in range(4)]):
        out[n] = kinds
    return (loss, grad_x, *[out[n][kind] for kind in range(4) for n in WEIGHTS])
```

```python
import functools
import math

import jax
import jax.numpy as jnp
from jax import lax
from jax.experimental import pallas as pl
from jax.experimental.pallas import tpu as pltpu

F32, BF16 = jnp.float32, jnp.bfloat16

D_MODEL = 1024
N_META = 16
N_HEADS = 16
QK_NOPE, QK_ROPE, V_DIM = 64, 32, 64
HALF_ROPE = QK_ROPE // 2
Q_RANK, KV_RANK = 256, 128
POOL_WINDOWS = (2, 4, 8, 16)
POOL_GROUP = 128
POOL_WIDTH = POOL_GROUP * len(POOL_WINDOWS)
POOL_HALO = 16
D_FF = 2816
D_IN = 2976
NORM_EPS = 1e-6
SM_SCALE = (QK_NOPE + QK_ROPE) ** -0.5
LOG2E = math.log2(math.e)
EXP2_SCALE = SM_SCALE * LOG2E
MASK_VALUE = -1e30
ROPE_THETA = 10000.0
DEPTH = 2
N_DEV = 8

ADAM_LR, ADAM_B1, ADAM_B2, ADAM_EPS, ADAM_WD, ADAM_STEP = 0.001, 0.9, 0.999, 1e-08, 0.01, 10

LANES = 128
HEAD_SLOT = LANES
QK_WIDTH = N_HEADS * HEAD_SLOT
Z_CQ, Z_CKV, Z_KR, Z_GA, Z_GB, DZ = 512, 768, 896, 1024, 2048, 3072
TQ = TK = 256
VMEM_LIMIT = 56 * 1024 * 1024


def _cparams():
    return pltpu.CompilerParams(vmem_limit_bytes=VMEM_LIMIT)


def _rows(tm, width, col=0):
    return pl.BlockSpec((tm, width), lambda i: (i, col))


def _whole(shape):
    zeros = (0,) * len(shape)
    return pl.BlockSpec(shape, lambda i: zeros, pipeline_mode=pl.Buffered(1))


def _acc(shape):
    zeros = (0,) * len(shape)
    return pl.BlockSpec(shape, lambda i: zeros)


def _dot(a, b):
    return jnp.dot(a, b, preferred_element_type=F32)


def _dot_tn(a, b):
    return lax.dot_general(a, b, (((0,), (0,)), ((), ())), preferred_element_type=F32)


def _dot_nt(a, b):
    return lax.dot_general(a, b, (((1,), (1,)), ((), ())), preferred_element_type=F32)


def _rms(x):
    r = lax.rsqrt(jnp.mean(x * x, axis=-1, keepdims=True) + NORM_EPS)
    return x * r, r


def _rms_bwd(dy, xhat, r, g):
    dg = jnp.sum(dy * xhat, axis=0, keepdims=True)
    dxh = dy * g
    dx = r * (dxh - xhat * jnp.mean(dxh * xhat, axis=-1, keepdims=True))
    return dx, dg


def _sigmoid(x):
    return 1.0 / (1.0 + jnp.exp(-x))


def _rope_fwd(q, c, s1, s2):
    w = q.shape[1]
    return q * c + pltpu.roll(q, w - HALF_ROPE, 1) * s1 + pltpu.roll(q, HALF_ROPE, 1) * s2


def _rope_bwd(dq, c, s1, s2):
    w = dq.shape[1]
    return dq * c + pltpu.roll(dq * s1, HALF_ROPE, 1) + pltpu.roll(dq * s2, w - HALF_ROPE, 1)


def _rope_tables(rope, reps):
    c, cr, s1, s2 = (rope[:, k * LANES:(k + 1) * LANES] for k in range(4))
    if reps > 1:
        return jnp.tile(c, (1, reps)), jnp.tile(s1, (1, reps)), jnp.tile(s2, (1, reps))
    return cr, s1, s2


def _seq_pos(gi, lp, nb):
    pos = gi
    for b in range(1, nb):
        pos = jnp.where(gi >= b * lp, gi - b * lp, pos)
    return pos


_ANY = pl.BlockSpec(memory_space=pl.ANY)


def _carrying_call(body, ride, operands, *, name, grid, in_specs, out_specs, out_shape, scratch_shapes=()):
    n_in, n_out = len(in_specs), len(out_specs)
    if ride is None:
        out = pl.pallas_call(body, name=name, grid=grid, in_specs=in_specs, out_specs=out_specs, out_shape=out_shape,
                             scratch_shapes=list(scratch_shapes), compiler_params=_cparams())(*operands)
        return out, []
    ne = len(ride.arrays)

    def carrying(*refs):
        ins, r_in, rest = refs[:n_in], refs[n_in:n_in + ne], refs[n_in + ne:]
        outs, r_out, rest = rest[:n_out], rest[n_out:n_out + ne], rest[n_out + ne:]
        scratch, sems = rest[:len(scratch_shapes)], rest[len(scratch_shapes):]
        ids = [pl.program_id(a) for a in range(len(grid))]
        first = functools.reduce(jnp.logical_and, [i == 0 for i in ids])
        last = functools.reduce(jnp.logical_and, [i == g - 1 for i, g in zip(ids, grid)])

        @pl.when(first)
        def _():
            ride.start(r_in, r_out, sems)

        body(*ins, *outs, *scratch)

        @pl.when(last)
        def _():
            ride.wait(r_in, r_out, sems)

    out = pl.pallas_call(
        carrying, name=name, grid=grid, in_specs=list(in_specs) + [_ANY] * ne, out_specs=list(out_specs) + [_ANY] * ne,
        out_shape=list(out_shape) + ride.out_shapes, scratch_shapes=list(scratch_shapes) + ride.scratch,
        input_output_aliases=ride.aliases(n_in, n_out), compiler_params=_cparams(),
    )(*operands, *ride.arrays)
    return out[:n_out], out[n_out:]


def in_proj_fwd(h, g_mix, win, gq, gkv, wuq, wuk, wuv, rope, *, tm, name):
    t = h.shape[0]

    def body(h_ref, g_ref, win_ref, gq_ref, gkv_ref, wuq_ref, wuk_ref, wuv_ref, rope_ref, z_ref, q_ref, k_ref, v_ref):
        xhat, _ = _rms(h_ref[...])
        hn = (xhat * g_ref[...]).astype(BF16)
        z = _dot(hn, win_ref[...])
        z_ref[...] = z
        rope_t = rope_ref[...]
        xq, _ = _rms(z[:, Z_CQ:Z_CKV])
        cqn = (xq * gq_ref[...]).astype(BF16)
        q = _rope_fwd(_dot(cqn, wuq_ref[...]), *_rope_tables(rope_t, N_HEADS))
        q_ref[...] = q.astype(BF16)
        xkv, _ = _rms(z[:, Z_CKV:Z_KR])
        ckvn = (xkv * gkv_ref[...]).astype(BF16)
        kr = _rope_fwd(z[:, Z_KR:Z_GA], *_rope_tables(rope_t, 1))
        k_ref[...] = (_dot(ckvn, wuk_ref[...]) + jnp.tile(kr, (1, N_HEADS))).astype(BF16)
        v_ref[...] = _dot(ckvn, wuv_ref[...]).astype(BF16)

    return pl.pallas_call(
        body, name=name, grid=(t // tm,),
        in_specs=[_rows(tm, D_MODEL), _whole((1, D_MODEL)), _whole((D_MODEL, DZ)), _whole((1, Q_RANK)), _whole((1, KV_RANK)),
                  _whole((Q_RANK, QK_WIDTH)), _whole((KV_RANK, QK_WIDTH)), _whole((KV_RANK, D_MODEL)), _rows(tm, 4 * LANES)],
        out_specs=[_rows(tm, DZ), _rows(tm, QK_WIDTH), _rows(tm, QK_WIDTH), _rows(tm, D_MODEL)],
        out_shape=[jax.ShapeDtypeStruct((t, DZ), F32), jax.ShapeDtypeStruct((t, QK_WIDTH), BF16),
                   jax.ShapeDtypeStruct((t, QK_WIDTH), BF16), jax.ShapeDtypeStruct((t, D_MODEL), BF16)],
        compiler_params=_cparams(),
    )(h, g_mix, win, gq, gkv, wuq, wuk, wuv, rope)


def attn_fwd(q, k, v, *, nb, lp, hb, name, ride=None):
    t = q.shape[0]
    nq = lp // TQ

    def body(q_ref, k_ref, v_ref, o_ref, lse_ref, vt):
        for pr in range(hb // 2):
            vt[pr] = v_ref[:, pr * LANES:(pr + 1) * LANES].T
        keep = lax.broadcasted_iota(jnp.int32, (TK, TQ), 0) <= lax.broadcasted_iota(jnp.int32, (TK, TQ), 1)
        sub = lax.broadcasted_iota(jnp.int32, (LANES, TQ), 0)

        def q_block(qi, carry):
            qs = pl.multiple_of(qi * TQ, TQ)
            qh = [q_ref[pl.ds(qs, TQ), hd * HEAD_SLOT:(hd + 1) * HEAD_SLOT] for hd in range(hb)]

            def k_step(kj, c, masked):
                ks = pl.multiple_of(kj * TK, TK)
                sts = [_dot_nt(k_ref[pl.ds(ks, TK), hd * HEAD_SLOT:(hd + 1) * HEAD_SLOT], qh[hd]) for hd in range(hb)]
                ps, stats = [], []
                for hd in range(hb):
                    m, l, _ = c[hd]
                    st = jnp.where(keep, sts[hd], MASK_VALUE) if masked else sts[hd]
                    m_new = jnp.maximum(m, jnp.max(st, axis=0, keepdims=True))
                    p = jnp.exp2((st - m_new) * EXP2_SCALE)
                    alpha = jnp.exp2((m - m_new) * EXP2_SCALE)
                    ps.append(p.astype(BF16))
                    stats.append((m_new, alpha * l + jnp.sum(p, axis=0, keepdims=True), alpha))
                pvs = [_dot(vt[hd // 2, :, pl.ds(ks, TK)], ps[hd]) for hd in range(hb)]
                return tuple((stats[hd][0], stats[hd][1], stats[hd][2] * c[hd][2] + pvs[hd]) for hd in range(hb))

            init = tuple((jnp.full((1, TQ), MASK_VALUE, F32), jnp.zeros((1, TQ), F32), jnp.zeros((LANES, TQ), F32))
                         for _ in range(hb))
            c = lax.fori_loop(0, qi, functools.partial(k_step, masked=False), init)
            c = k_step(qi, c, True)
            for pr in range(hb // 2):
                (m0, l0, a0), (m1, l1, a1) = c[2 * pr], c[2 * pr + 1]
                o_ref[pl.ds(qs, TQ), pr * LANES:(pr + 1) * LANES] = jnp.where(sub < V_DIM, a0 / l0, a1 / l1).T.astype(BF16)
                lse_ref[2 * pr, :, pl.ds(qs, TQ)] = m0 * SM_SCALE + jnp.log(l0)
                lse_ref[2 * pr + 1, :, pl.ds(qs, TQ)] = m1 * SM_SCALE + jnp.log(l1)
            return carry

        lax.fori_loop(0, nq, q_block, 0)

    blk = lambda w: pl.BlockSpec((lp, w), lambda b, g: (b, g))
    return _carrying_call(
        body, ride, (q, k, v), name=name, grid=(nb, N_HEADS // hb),
        in_specs=[blk(hb * HEAD_SLOT), blk(hb * HEAD_SLOT), blk(hb * V_DIM)],
        out_specs=[blk(hb * V_DIM), pl.BlockSpec((hb, 1, lp), lambda b, g: (g, 0, b))],
        out_shape=[jax.ShapeDtypeStruct((t, D_MODEL), BF16), jax.ShapeDtypeStruct((N_HEADS, 1, t), F32)],
        scratch_shapes=[pltpu.VMEM((hb // 2, LANES, lp), BF16)])


def _pool_band_fwd(i, tm, lp, nb):
    r = lax.broadcasted_iota(jnp.int32, (tm, POOL_HALO + tm), 0)
    e = lax.broadcasted_iota(jnp.int32, (tm, POOL_HALO + tm), 1)
    diff = r + POOL_HALO - e
    pos = _seq_pos(i * tm + lax.broadcasted_iota(jnp.int32, (tm, 1), 0), lp, nb)
    out = []
    for w in POOL_WINDOWS:
        cnt = jnp.minimum(pos + 1, w)
        band = jnp.where((diff >= 0) & (diff < cnt), 1.0, 0.0).astype(BF16)
        out.append((band, cnt.astype(F32)))
    return out


def merge_fwd(h, z, o, pw, ps, wpa, wpb, wo, *, tm, lp, nb, name, ride=None):
    t = h.shape[0]
    hb = tm // POOL_HALO

    def body(h_ref, u_ref, uprev_ref, ga_ref, gb_ref, o_ref, pw_ref, ps_ref, wpa_ref, wpb_ref, wo_ref,
             h1_ref, pooled_ref, a_ref, pa_ref, pb_ref, mg_ref):
        i = pl.program_id(0)
        u = u_ref[...]
        uext = jnp.concatenate([uprev_ref[...], u], axis=0).astype(BF16)
        pooled, ys = [], []
        for g, (band, cnt) in enumerate(_pool_band_fwd(i, tm, lp, nb)):
            gs = slice(g * POOL_GROUP, (g + 1) * POOL_GROUP)
            pg = (_dot(band, uext[:, gs]) / cnt - u[:, gs]).astype(BF16)
            pooled.append(pg)
            ys.append(_dot(pg, pw_ref[g]))
        pooled_ref[...] = jnp.concatenate(pooled, axis=1)
        a = (jnp.concatenate(ys, axis=1) * ps_ref[...]).astype(BF16)
        a_ref[...] = a
        pa = _dot(a, wpa_ref[...])
        pb = _dot(o_ref[...], wpb_ref[...])
        pa_ref[...] = pa.astype(BF16)
        pb_ref[...] = pb.astype(BF16)
        mg = (_sigmoid(ga_ref[...]) * pa + _sigmoid(gb_ref[...]) * pb).astype(BF16)
        mg_ref[...] = mg
        h1_ref[...] = h_ref[...] + _dot(mg, wo_ref[...])

    halo = pl.BlockSpec((POOL_HALO, POOL_WIDTH), lambda i: (jnp.maximum(i * hb - 1, 0), 0))
    return _carrying_call(
        body, ride, (h, z, z, z, z, o, pw, ps, wpa, wpb, wo), name=name, grid=(t // tm,),
        in_specs=[_rows(tm, D_MODEL), _rows(tm, POOL_WIDTH), halo, _rows(tm, D_MODEL, 1), _rows(tm, D_MODEL, 2), _rows(tm, D_MODEL),
                  _whole((4, POOL_GROUP, POOL_GROUP)), _whole((1, POOL_WIDTH)), _whole((POOL_WIDTH, D_MODEL)),
                  _whole((D_MODEL, D_MODEL)), _whole((D_MODEL, D_MODEL))],
        out_specs=[_rows(tm, D_MODEL), _rows(tm, POOL_WIDTH), _rows(tm, POOL_WIDTH), _rows(tm, D_MODEL), _rows(tm, D_MODEL),
                   _rows(tm, D_MODEL)],
        out_shape=[jax.ShapeDtypeStruct((t, D_MODEL), F32), jax.ShapeDtypeStruct((t, POOL_WIDTH), BF16),
                   jax.ShapeDtypeStruct((t, POOL_WIDTH), BF16), jax.ShapeDtypeStruct((t, D_MODEL), BF16),
                   jax.ShapeDtypeStruct((t, D_MODEL), BF16), jax.ShapeDtypeStruct((t, D_MODEL), BF16)])


def ffn_fwd(h1, g, wg, wu, wd, *, tm, name):
    t = h1.shape[0]

    def body(h_ref, g_ref, wg_ref, wu_ref, wd_ref, h2_ref, gt_ref, up_ref):
        h = h_ref[...]
        xhat, _ = _rms(h)
        hn = (xhat * g_ref[...]).astype(BF16)
        gt = _dot(hn, wg_ref[...])
        up = _dot(hn, wu_ref[...])
        gt_ref[...] = gt.astype(BF16)
        up_ref[...] = up.astype(BF16)
        act = (gt * _sigmoid(gt) * up).astype(BF16)
        h2_ref[...] = h + _dot(act, wd_ref[...])

    return pl.pallas_call(
        body, name=name, grid=(t // tm,),
        in_specs=[_rows(tm, D_MODEL), _whole((1, D_MODEL)), _whole((D_MODEL, D_FF)), _whole((D_MODEL, D_FF)), _whole((D_FF, D_MODEL))],
        out_specs=[_rows(tm, D_MODEL), _rows(tm, D_FF), _rows(tm, D_FF)],
        out_shape=[jax.ShapeDtypeStruct((t, D_MODEL), F32), jax.ShapeDtypeStruct((t, D_FF), BF16), jax.ShapeDtypeStruct((t, D_FF), BF16)],
        compiler_params=_cparams(),
    )(h1, g, wg, wu, wd)


def loss_head(h, g, target, *, tm, lp, nb, seq, name):
    t = h.shape[0]
    nt = t // tm

    def body(h_ref, g_ref, t_ref, loss_ref, dh_ref, dg_ref):
        i = pl.program_id(0)
        pos = _seq_pos(i * tm + lax.broadcasted_iota(jnp.int32, (tm, 1), 0), lp, nb)
        real = (pos >= N_META) & (pos < N_META + seq)
        xhat, r = _rms(h_ref[...])
        gg = g_ref[...]
        err = jnp.where(real, xhat * gg - t_ref[...], 0.0)
        loss_ref[...] = jnp.full((8, LANES), 0.5 * jnp.sum(err * err) / D_MODEL, F32)
        dx, dg = _rms_bwd(err * (1.0 / D_MODEL), xhat, r, gg)
        dh_ref[...] = dx

        @pl.when(i == 0)
        def _():
            dg_ref[...] = jnp.zeros_like(dg_ref)

        dg_ref[...] += dg

    return pl.pallas_call(
        body, name=name, grid=(nt,),
        in_specs=[_rows(tm, D_MODEL), _whole((1, D_MODEL)), _rows(tm, D_MODEL)],
        out_specs=[pl.BlockSpec((8, LANES), lambda i: (i, 0)), _rows(tm, D_MODEL), _acc((1, D_MODEL))],
        out_shape=[jax.ShapeDtypeStruct((nt * 8, LANES), F32), jax.ShapeDtypeStruct((t, D_MODEL), F32),
                   jax.ShapeDtypeStruct((1, D_MODEL), F32)],
        compiler_params=_cparams(),
    )(h, g, target)


def wgrad(x, ys, chunk_fn, chunk_shape, *, tm, name):
    t, m = x.shape

    def body(x_ref, *refs):
        y_refs, o_ref, accs = refs[:len(ys)], refs[len(ys)], refs[len(ys) + 1:]
        i = pl.program_id(0)

        @pl.when(i == 0)
        def _():
            for acc in accs:
                acc[...] = jnp.zeros_like(acc)

        xb = x_ref[...].astype(BF16)
        for y_ref, acc in zip(y_refs, accs):
            acc[...] += _dot_tn(xb, y_ref[...].astype(BF16))

        @pl.when(i == t // tm - 1)
        def _():
            for p, chunk in enumerate(chunk_fn(*accs)):
                o_ref[p % 2, p // 2] = chunk.astype(BF16)

    out = (2, N_DEV // 2) + tuple(chunk_shape)
    return pl.pallas_call(
        body, name=name, grid=(t // tm,),
        in_specs=[_rows(tm, m)] + [_rows(tm, y.shape[1]) for y in ys], out_specs=_acc(out),
        out_shape=jax.ShapeDtypeStruct(out, BF16), scratch_shapes=[pltpu.VMEM((m, y.shape[1]), F32) for y in ys],
        compiler_params=_cparams(),
    )(x, *ys)


def ffn_bwd(dh2, h1, g, gt, up, wgt, wut, wdt, *, tm, name):
    t = h1.shape[0]

    def body(dh2_ref, h_ref, g_ref, gt_ref, up_ref, wgt_ref, wut_ref, wdt_ref, dh1_ref, hn_ref, act_ref, dgt_ref, dup_ref, dg_ref):
        dh2 = dh2_ref[...]
        dact = _dot(dh2.astype(BF16), wdt_ref[...])
        gt = gt_ref[...].astype(F32)
        up = up_ref[...].astype(F32)
        sg = _sigmoid(gt)
        silu = gt * sg
        act_ref[...] = (silu * up).astype(BF16)
        dgt = (dact * up * (sg * (1.0 + gt * (1.0 - sg)))).astype(BF16)
        dup = (dact * silu).astype(BF16)
        dgt_ref[...] = dgt
        dup_ref[...] = dup
        dhn = _dot(dgt, wgt_ref[...]) + _dot(dup, wut_ref[...])
        xhat, r = _rms(h_ref[...])
        gg = g_ref[...]
        hn_ref[...] = (xhat * gg).astype(BF16)
        dx, dg = _rms_bwd(dhn, xhat, r, gg)
        dh1_ref[...] = dh2 + dx

        @pl.when(pl.program_id(0) == 0)
        def _():
            dg_ref[...] = jnp.zeros_like(dg_ref)

        dg_ref[...] += dg

    return pl.pallas_call(
        body, name=name, grid=(t // tm,),
        in_specs=[_rows(tm, D_MODEL), _rows(tm, D_MODEL), _whole((1, D_MODEL)), _rows(tm, D_FF), _rows(tm, D_FF),
                  _whole((D_FF, D_MODEL)), _whole((D_FF, D_MODEL)), _whole((D_MODEL, D_FF))],
        out_specs=[_rows(tm, D_MODEL), _rows(tm, D_MODEL), _rows(tm, D_FF), _rows(tm, D_FF), _rows(tm, D_FF), _acc((1, D_MODEL))],
        out_shape=[jax.ShapeDtypeStruct((t, D_MODEL), F32), jax.ShapeDtypeStruct((t, D_MODEL), BF16),
                   jax.ShapeDtypeStruct((t, D_FF), BF16), jax.ShapeDtypeStruct((t, D_FF), BF16),
                   jax.ShapeDtypeStruct((t, D_FF), BF16), jax.ShapeDtypeStruct((1, D_MODEL), F32)],
        compiler_params=_cparams(),
    )(dh2, h1, g, gt, up, wgt, wut, wdt)


def merge_bwd(dh1, z, pa, pb, pooled, pw, pwt, ps, wpat, wpbt, wot, *, tm, name):
    t = dh1.shape[0]

    def body(dh1_ref, ga_ref, gb_ref, pa_ref, pb_ref, pooled_ref, pw_ref, pwt_ref, ps_ref, wpat_ref, wpbt_ref, wot_ref,
             dga_ref, dgb_ref, dpa_ref, dpb_ref, do_ref, dpool_ref, dps_ref, dpw_ref):
        dmg = _dot(dh1_ref[...].astype(BF16), wot_ref[...])
        sa = _sigmoid(ga_ref[...])
        sb = _sigmoid(gb_ref[...])
        dga_ref[...] = (dmg * pa_ref[...].astype(F32) * sa * (1.0 - sa)).astype(BF16)
        dgb_ref[...] = (dmg * pb_ref[...].astype(F32) * sb * (1.0 - sb)).astype(BF16)
        dpa = (dmg * sa).astype(BF16)
        dpb = (dmg * sb).astype(BF16)
        dpa_ref[...] = dpa
        dpb_ref[...] = dpb
        do_ref[...] = _dot(dpb, wpbt_ref[...]).astype(BF16)
        da = _dot(dpa, wpat_ref[...])
        pooled = pooled_ref[...]
        ps = ps_ref[...]

        @pl.when(pl.program_id(0) == 0)
        def _():
            dps_ref[...] = jnp.zeros_like(dps_ref)
            dpw_ref[...] = jnp.zeros_like(dpw_ref)

        dps, dpool = [], []
        for g in range(len(POOL_WINDOWS)):
            gs = slice(g * POOL_GROUP, (g + 1) * POOL_GROUP)
            y = _dot(pooled[:, gs], pw_ref[g])
            dps.append(jnp.sum(da[:, gs] * y, axis=0, keepdims=True))
            dy = (da[:, gs] * ps[:, gs]).astype(BF16)
            dpool.append(_dot(dy, pwt_ref[g]))
            dpw_ref[g] += _dot_tn(pooled[:, gs], dy)
        dps_ref[...] += jnp.concatenate(dps, axis=1)
        dpool_ref[...] = jnp.concatenate(dpool, axis=1)

    return pl.pallas_call(
        body, name=name, grid=(t // tm,),
        in_specs=[_rows(tm, D_MODEL), _rows(tm, D_MODEL, 1), _rows(tm, D_MODEL, 2), _rows(tm, D_MODEL), _rows(tm, D_MODEL),
                  _rows(tm, POOL_WIDTH), _whole((4, POOL_GROUP, POOL_GROUP)), _whole((4, POOL_GROUP, POOL_GROUP)),
                  _whole((1, POOL_WIDTH)), _whole((D_MODEL, POOL_WIDTH)), _whole((D_MODEL, D_MODEL)), _whole((D_MODEL, D_MODEL))],
        out_specs=[_rows(tm, D_MODEL), _rows(tm, D_MODEL), _rows(tm, D_MODEL), _rows(tm, D_MODEL), _rows(tm, D_MODEL),
                   _rows(tm, POOL_WIDTH), _acc((1, POOL_WIDTH)), _acc((4, POOL_GROUP, POOL_GROUP))],
        out_shape=[jax.ShapeDtypeStruct((t, D_MODEL), BF16)] * 5
        + [jax.ShapeDtypeStruct((t, POOL_WIDTH), F32), jax.ShapeDtypeStruct((1, POOL_WIDTH), F32),
           jax.ShapeDtypeStruct((4, POOL_GROUP, POOL_GROUP), F32)],
        compiler_params=_cparams(),
    )(dh1, z, z, pa, pb, pooled, pw, pwt, ps, wpat, wpbt, wot)


def attn_bwd(q, k, v, o, do, lse, *, nb, lp, hb, name, ride=None):
    t = q.shape[0]
    nq = lp // TQ

    def body(q_ref, k_ref, v_ref, o_ref, do_ref, lse_ref, dq_ref, dk_ref, dv_ref, kt, doh, lse_row, delta_row, dqt):
        lane = lax.broadcasted_iota(jnp.int32, (lp, LANES), 1)
        first = lane < V_DIM
        sub = lax.broadcasted_iota(jnp.int32, (LANES, lp), 0)
        for pr in range(hb // 2):
            ls = slice(pr * LANES, (pr + 1) * LANES)
            do = do_ref[:, ls]
            doh[2 * pr] = jnp.where(first, do, jnp.zeros_like(do))
            doh[2 * pr + 1] = jnp.where(first, jnp.zeros_like(do), do)
            prod_t = (do.astype(F32) * o_ref[:, ls].astype(F32)).T
            delta_row[2 * pr] = jnp.sum(jnp.where(sub < V_DIM, prod_t, 0.0), axis=0, keepdims=True)
            delta_row[2 * pr + 1] = jnp.sum(jnp.where(sub < V_DIM, 0.0, prod_t), axis=0, keepdims=True)
        for hd in range(hb):
            lse_row[hd] = lse_ref[hd] * LOG2E
            kt[hd] = k_ref[:, hd * HEAD_SLOT:(hd + 1) * HEAD_SLOT].T
        dqt[...] = jnp.zeros(dqt.shape, F32)
        keep = lax.broadcasted_iota(jnp.int32, (TK, TQ), 0) <= lax.broadcasted_iota(jnp.int32, (TK, TQ), 1)

        def k_block(kj, carry):
            ks = pl.multiple_of(kj * TK, TK)

            def q_step(qi, c, masked):
                qs = pl.multiple_of(qi * TQ, TQ)
                heads = range(hb)
                hss = [slice(hd * HEAD_SLOT, (hd + 1) * HEAD_SLOT) for hd in heads]
                qhs = [q_ref[pl.ds(qs, TQ), hss[hd]] for hd in heads]
                dos = [doh[hd, pl.ds(qs, TQ), :] for hd in heads]
                sts = [_dot_nt(k_ref[pl.ds(ks, TK), hss[hd]], qhs[hd]) for hd in heads]
                dpts = [_dot_nt(v_ref[pl.ds(ks, TK), (hd // 2) * LANES:(hd // 2 + 1) * LANES], dos[hd]) for hd in heads]
                pts, dsts = [], []
                for hd in heads:
                    st = jnp.where(keep, sts[hd], MASK_VALUE) if masked else sts[hd]
                    pt = jnp.exp2(st * EXP2_SCALE - lse_row[hd, :, pl.ds(qs, TQ)])
                    dsts.append((pt * (dpts[hd] - delta_row[hd, :, pl.ds(qs, TQ)])).astype(BF16))
                    pts.append(pt.astype(BF16))
                dvs = [_dot(pts[hd], dos[hd]) for hd in heads]
                dks = [_dot(dsts[hd], qhs[hd]) for hd in heads]
                dqs = [_dot(kt[hd, :, pl.ds(ks, TK)], dsts[hd]) for hd in heads]
                for hd in heads:
                    dqt[hd, :, pl.ds(qs, TQ)] += dqs[hd]
                return tuple((c[hd][0] + dks[hd], c[hd][1] + dvs[hd]) for hd in heads)

            zero = jnp.zeros((TK, LANES), F32)
            c = q_step(kj, tuple((zero, zero) for _ in range(hb)), True)
            c = lax.fori_loop(kj + 1, nq, functools.partial(q_step, masked=False), c)
            for hd in range(hb):
                dk_ref[pl.ds(ks, TK), hd * HEAD_SLOT:(hd + 1) * HEAD_SLOT] = c[hd][0] * SM_SCALE
            for pr in range(hb // 2):
                dv_ref[pl.ds(ks, TK), pr * LANES:(pr + 1) * LANES] = c[2 * pr][1] + c[2 * pr + 1][1]
            return carry

        lax.fori_loop(0, nq, k_block, 0)
        for hd in range(hb):
            dq_ref[:, hd * HEAD_SLOT:(hd + 1) * HEAD_SLOT] = dqt[hd].T * SM_SCALE

    blk = lambda w: pl.BlockSpec((lp, w), lambda b, g: (b, g))
    return _carrying_call(
        body, ride, (q, k, v, o, do, lse), name=name, grid=(nb, N_HEADS // hb),
        in_specs=[blk(hb * HEAD_SLOT), blk(hb * HEAD_SLOT), blk(hb * V_DIM), blk(hb * V_DIM), blk(hb * V_DIM),
                  pl.BlockSpec((hb, 1, lp), lambda b, g: (g, 0, b))],
        out_specs=[blk(hb * HEAD_SLOT), blk(hb * HEAD_SLOT), blk(hb * V_DIM)],
        out_shape=[jax.ShapeDtypeStruct((t, QK_WIDTH), F32), jax.ShapeDtypeStruct((t, QK_WIDTH), F32),
                   jax.ShapeDtypeStruct((t, D_MODEL), F32)],
        scratch_shapes=[pltpu.VMEM((hb, HEAD_SLOT, lp), BF16), pltpu.VMEM((hb, lp, LANES), BF16), pltpu.VMEM((hb, 1, lp), F32),
                        pltpu.VMEM((hb, 1, lp), F32), pltpu.VMEM((hb, HEAD_SLOT, lp), F32)])


def in_proj_bwd(dh1, h, g_mix, z, dq, dk, dv, dga, dgb, dpool, wint, gq, gkv, wuqt, wukt, wuvt, rope, *, tm, lp, nb, name):
    t = h.shape[0]
    hb = tm // POOL_HALO
    last_halo = t // POOL_HALO - 1

    def body(dh1_ref, h_ref, g_ref, zcq_ref, zckv_ref, dq_ref, dk_ref, dv_ref, dga_ref, dgb_ref, dpool_ref, dnext_ref,
             wint_ref, gq_ref, gkv_ref, wuqt_ref, wukt_ref, wuvt_ref, rope_ref,
             dh_ref, hn_ref, dz_ref, cqn_ref, ckvn_ref, dqb_ref, dkb_ref, dvb_ref, dg_ref, dgq_ref, dgkv_ref):
        i = pl.program_id(0)
        rope_t = rope_ref[...]
        dqb = _rope_bwd(dq_ref[...], *_rope_tables(rope_t, N_HEADS)).astype(BF16)
        dqb_ref[...] = dqb
        xq, rq = _rms(zcq_ref[...])
        gq_v = gq_ref[...]
        cqn_ref[...] = (xq * gq_v).astype(BF16)
        dcq, dgq = _rms_bwd(_dot(dqb, wuqt_ref[...]), xq, rq, gq_v)
        dk = dk_ref[...]
        dkb = dk.astype(BF16)
        dvb = dv_ref[...].astype(BF16)
        dkb_ref[...] = dkb
        dvb_ref[...] = dvb
        xkv, rkv = _rms(zckv_ref[...])
        gkv_v = gkv_ref[...]
        ckvn_ref[...] = (xkv * gkv_v).astype(BF16)
        dckv, dgkv = _rms_bwd(_dot(dkb, wukt_ref[...]) + _dot(dvb, wuvt_ref[...]), xkv, rkv, gkv_v)
        dks = dk[:, :HEAD_SLOT]
        for hd in range(1, N_HEADS):
            dks = dks + dk[:, hd * HEAD_SLOT:(hd + 1) * HEAD_SLOT]
        dzk = _rope_bwd(dks, *_rope_tables(rope_t, 1))
        dp_cur = dpool_ref[...]
        dp_ext = jnp.concatenate([dp_cur, dnext_ref[...]], axis=0)
        r = lax.broadcasted_iota(jnp.int32, (tm, tm + POOL_HALO), 0)
        e = lax.broadcasted_iota(jnp.int32, (tm, tm + POOL_HALO), 1)
        gt_col = i * tm + lax.broadcasted_iota(jnp.int32, (1, tm + POOL_HALO), 1)
        pos_col = _seq_pos(gt_col, lp, nb)
        gt_row = i * tm + lax.broadcasted_iota(jnp.int32, (tm + POOL_HALO, 1), 0)
        pos_row = _seq_pos(gt_row, lp, nb)
        dus = []
        for g, w in enumerate(POOL_WINDOWS):
            gs = slice(g * POOL_GROUP, (g + 1) * POOL_GROUP)
            band = jnp.where((e - r >= 0) & (e - r < jnp.minimum(pos_col + 1, w)) & (gt_col < t), 1.0, 0.0).astype(BF16)
            scaled = jnp.where(gt_row < t, dp_ext[:, gs] / jnp.minimum(pos_row + 1, w).astype(F32), 0.0).astype(BF16)
            dus.append(_dot(band, scaled) - dp_cur[:, gs])
        dz = jnp.concatenate(dus + [dcq, dckv, dzk], axis=1).astype(BF16)
        dz = jnp.concatenate([dz, dga_ref[...], dgb_ref[...]], axis=1)
        dz_ref[...] = dz
        xhat, rr = _rms(h_ref[...])
        gg = g_ref[...]
        hn_ref[...] = (xhat * gg).astype(BF16)
        dx, dg = _rms_bwd(_dot(dz, wint_ref[...]), xhat, rr, gg)
        dh_ref[...] = dh1_ref[...] + dx

        @pl.when(i == 0)
        def _():
            dg_ref[...] = jnp.zeros_like(dg_ref)
            dgq_ref[...] = jnp.zeros_like(dgq_ref)
            dgkv_ref[...] = jnp.zeros_like(dgkv_ref)

        dg_ref[...] += dg
        dgq_ref[...] += dgq
        dgkv_ref[...] += dgkv

    nxt = pl.BlockSpec((POOL_HALO, POOL_WIDTH), lambda i: (jnp.minimum((i + 1) * hb, last_halo), 0))
    return pl.pallas_call(
        body, name=name, grid=(t // tm,),
        in_specs=[_rows(tm, D_MODEL), _rows(tm, D_MODEL), _whole((1, D_MODEL)), _rows(tm, Q_RANK, Z_CQ // Q_RANK),
                  _rows(tm, KV_RANK, Z_CKV // KV_RANK), _rows(tm, QK_WIDTH), _rows(tm, QK_WIDTH), _rows(tm, D_MODEL),
                  _rows(tm, D_MODEL), _rows(tm, D_MODEL), _rows(tm, POOL_WIDTH), nxt,
                  _whole((DZ, D_MODEL)), _whole((1, Q_RANK)), _whole((1, KV_RANK)), _whole((QK_WIDTH, Q_RANK)),
                  _whole((QK_WIDTH, KV_RANK)), _whole((D_MODEL, KV_RANK)), _rows(tm, 4 * LANES)],
        out_specs=[_rows(tm, D_MODEL), _rows(tm, D_MODEL), _rows(tm, DZ), _rows(tm, Q_RANK), _rows(tm, KV_RANK),
                   _rows(tm, QK_WIDTH), _rows(tm, QK_WIDTH), _rows(tm, D_MODEL),
                   _acc((1, D_MODEL)), _acc((1, Q_RANK)), _acc((1, KV_RANK))],
        out_shape=[jax.ShapeDtypeStruct((t, D_MODEL), F32), jax.ShapeDtypeStruct((t, D_MODEL), BF16),
                   jax.ShapeDtypeStruct((t, DZ), BF16), jax.ShapeDtypeStruct((t, Q_RANK), BF16),
                   jax.ShapeDtypeStruct((t, KV_RANK), BF16), jax.ShapeDtypeStruct((t, QK_WIDTH), BF16),
                   jax.ShapeDtypeStruct((t, QK_WIDTH), BF16), jax.ShapeDtypeStruct((t, D_MODEL), BF16),
                   jax.ShapeDtypeStruct((1, D_MODEL), F32), jax.ShapeDtypeStruct((1, Q_RANK), F32),
                   jax.ShapeDtypeStruct((1, KV_RANK), F32)],
        compiler_params=_cparams(),
    )(dh1, h, g_mix, z, z, dq, dk, dv, dga, dgb, dpool, dpool, wint, gq, gkv, wuqt, wukt, wuvt, rope)


_MESH = pl.DeviceIdType.MESH


def _place():
    x, y, c = lax.axis_index("x"), lax.axis_index("y"), lax.axis_index("c")
    return x, y, c, 4 * x + 2 * y + c


def _peer(x, y, c, k):
    px, py, pc = (1 - x) if k & 4 else x, (1 - y) if k & 2 else y, (1 - c) if k & 1 else c
    return (px, py, pc), 4 * px + 2 * py + pc


ALL_PEERS = tuple(range(1, N_DEV))
CHIP_PEERS = (2, 4, 6)
N_CHIPS = N_DEV // 2


def _sem_scratch(n, m):
    return [pltpu.SemaphoreType.DMA((n, m)), pltpu.SemaphoreType.DMA((n, m)), pltpu.SemaphoreType.DMA((n,))]


class Exchange:
    def __init__(self, arrays, out_shapes, sem_cols, plan, aliased=False):
        self.arrays, self.out_shapes, self.plan = list(arrays), list(out_shapes), plan
        self.scratch = _sem_scratch(len(self.arrays), sem_cols)
        self.aliased = aliased

    def split(self, refs):
        n = len(self.arrays)
        return refs[:n], refs[n:2 * n], refs[2 * n:]

    def start(self, srcs, dsts, sems):
        local, sends, _ = self.plan(srcs, dsts, *sems)
        for cp in local + sends:
            cp.start()

    def wait(self, srcs, dsts, sems):
        local, sends, recvs = self.plan(srcs, dsts, *sems)
        for cp in recvs:
            cp.wait_recv()
        for cp in sends:
            cp.wait_send()
        for cp in local:
            cp.wait()

    def aliases(self, first_in, first_out):
        return {first_in + j: first_out + j for j in range(len(self.arrays))} if self.aliased else {}

    def run(self, name):
        def body(*refs):
            srcs, dsts, sems = self.split(refs)
            self.start(srcs, dsts, sems)
            self.wait(srcs, dsts, sems)

        n = len(self.arrays)
        return pl.pallas_call(body, name=name, in_specs=[_ANY] * n, out_specs=[_ANY] * n, out_shape=self.out_shapes,
                              scratch_shapes=self.scratch, input_output_aliases=self.aliases(0, 0))(*self.arrays)


def exchange(arrays, scatter, peers, by_chip=False):
    slots = N_CHIPS if by_chip else N_DEV

    def plan(srcs, dsts, send_sems, recv_sems, local_sems):
        x, y, c, me = _place()
        mine = 2 * x + y if by_chip else me
        local = [pltpu.make_async_copy(src.at[mine] if scatter else src, dst.at[mine], local_sems.at[j])
                 for j, (src, dst) in enumerate(zip(srcs, dsts))]
        sends, recvs = [], []
        for t, k in enumerate(peers):
            peer, pidx = _peer(x, y, c, k)
            theirs = 2 * peer[0] + peer[1] if by_chip else pidx
            for j, (src, dst) in enumerate(zip(srcs, dsts)):
                part = src.at[theirs] if scatter else src
                sems = dict(send_sem=send_sems.at[j, t], recv_sem=recv_sems.at[j, t], device_id=peer, device_id_type=_MESH)
                sends.append(pltpu.make_async_remote_copy(src_ref=part, dst_ref=dst.at[mine], **sems))
                recvs.append(pltpu.make_async_remote_copy(src_ref=part, dst_ref=dst.at[theirs], **sems))
        return local, sends, recvs

    shapes = [jax.ShapeDtypeStruct(a.shape if scatter else (slots,) + a.shape, a.dtype) for a in arrays]
    return Exchange(arrays, shapes, len(peers), plan)


def second_hop(gathered):
    def plan(srcs, dsts, send_sems, recv_sems, local_sems):
        x, y, c, me = _place()
        sibling, _ = _peer(x, y, c, 1)
        sends, recvs = [], []
        for t, k in enumerate(CHIP_PEERS):
            _, landed = _peer(x, y, c, k)
            _, coming = _peer(x, y, c, k ^ 1)
            for j, buf in enumerate(dsts):
                sems = dict(send_sem=send_sems.at[j, t], recv_sem=recv_sems.at[j, t], device_id=sibling, device_id_type=_MESH)
                sends.append(pltpu.make_async_remote_copy(src_ref=buf.at[landed], dst_ref=buf.at[landed], **sems))
                recvs.append(pltpu.make_async_remote_copy(src_ref=buf.at[coming], dst_ref=buf.at[coming], **sems))
        return [], sends, recvs

    shapes = [jax.ShapeDtypeStruct(a.shape, a.dtype) for a in gathered]
    return Exchange(gathered, shapes, len(CHIP_PEERS), plan, aliased=True)


FIRST_HOP_PEERS = (1,) + CHIP_PEERS


def _gather_two_level(arrays, name):
    n = len(arrays)

    def body(*refs):
        srcs, dsts, (send_sems, recv_sems, local_sems) = refs[:n], refs[n:2 * n], refs[2 * n:]
        x, y, c, me = _place()
        sibling, sidx = _peer(x, y, c, 1)

        def copy(j, sem, block, to, src=None):
            rows = dsts[j].at[block]
            return pltpu.make_async_remote_copy(src_ref=rows if src is None else src, dst_ref=rows, send_sem=send_sems.at[j, sem],
                                                recv_sem=recv_sems.at[j, sem], device_id=to, device_id_type=_MESH)

        local = [pltpu.make_async_copy(srcs[j], dsts[j].at[me], local_sems.at[j]) for j in range(n)]
        for cp in local:
            cp.start()
        first = [copy(j, 1 + t, me, _peer(x, y, c, k)[0], src=srcs[j]) for t, k in enumerate(CHIP_PEERS) for j in range(n)]
        first += [copy(j, 0, me, sibling, src=srcs[j]) for j in range(n)]
        for cp in first:
            cp.start()
        passed = []
        for t, k in enumerate(CHIP_PEERS):
            peer, pidx = _peer(x, y, c, k)
            for j in range(n):
                copy(j, 1 + t, pidx, peer).wait_recv()
                passed.append(copy(j, 4 + t, pidx, sibling))
                passed[-1].start()
        for j in range(n):
            copy(j, 0, sidx, sibling).wait_recv()
        for t, k in enumerate(CHIP_PEERS):
            _, pidx = _peer(x, y, c, k ^ 1)
            for j in range(n):
                copy(j, 4 + t, pidx, sibling).wait_recv()
        for cp in first + passed:
            cp.wait_send()
        for cp in local:
            cp.wait()

    shapes = [jax.ShapeDtypeStruct((N_DEV,) + a.shape, a.dtype) for a in arrays]
    return pl.pallas_call(body, name=name, in_specs=[_ANY] * n, out_specs=[_ANY] * n, out_shape=shapes,
                          scratch_shapes=_sem_scratch(n, 1 + 2 * len(CHIP_PEERS)))(*arrays)


def _to_sibling(arrays, name):
    n = len(arrays)

    def body(*refs):
        srcs, dsts, (send_sems, recv_sems) = refs[:n], refs[n:2 * n], refs[2 * n:]
        x, y, c, _ = _place()
        sibling, _ = _peer(x, y, c, 1)
        copies = [pltpu.make_async_remote_copy(src_ref=srcs[j].at[1 - c], dst_ref=dsts[j], send_sem=send_sems.at[j],
                                               recv_sem=recv_sems.at[j], device_id=sibling, device_id_type=_MESH) for j in range(n)]
        for cp in copies:
            cp.start()
        for cp in copies:
            cp.wait()

    shapes = [jax.ShapeDtypeStruct(a.shape[1:], a.dtype) for a in arrays]
    return pl.pallas_call(body, name=name, in_specs=[_ANY] * n, out_specs=[_ANY] * n, out_shape=shapes,
                          scratch_shapes=[pltpu.SemaphoreType.DMA((n,)), pltpu.SemaphoreType.DMA((n,))])(*arrays)


def pair_add(own, theirs, core, *, name):
    _, ns, r, c = own.shape
    rb = _row_block(r, c)

    def body(core_ref, a_ref, b_ref, o_ref):
        o_ref[...] = (a_ref[...].astype(F32) + b_ref[...].astype(F32)).astype(o_ref.dtype)

    return pl.pallas_call(
        body, name=name,
        grid_spec=pltpu.PrefetchScalarGridSpec(
            num_scalar_prefetch=1, grid=(ns, r // rb),
            in_specs=[pl.BlockSpec((None, None, rb, c), lambda i, j, core_ref: (core_ref[0], i, j, 0)),
                      pl.BlockSpec((None, rb, c), lambda i, j, core_ref: (i, j, 0))],
            out_specs=pl.BlockSpec((None, rb, c), lambda i, j, core_ref: (i, j, 0))),
        out_shape=jax.ShapeDtypeStruct((ns, r, c), own.dtype), compiler_params=_cparams(),
    )(core, own, theirs)


ADAMW_BLOCK_BYTES = 1 << 20


def _row_block(r, c):
    for rb in range(r, 0, -1):
        if r % rb == 0 and (rb % 16 == 0 or rb == r) and rb * c * 4 <= ADAMW_BLOCK_BYTES:
            return rb
    return r


def adamw(w, m, v, parts, *, name):
    depth, r, c = w.shape
    n_parts = parts[0].shape[0]
    rb = _row_block(r, c)

    def body(w_ref, m_ref, v_ref, *refs):
        p_refs, (g_ref, d_ref, nm_ref, nv_ref) = refs[:depth], refs[depth:]

        def total(p_ref):
            g = p_ref[0].astype(F32)
            for j in range(1, n_parts):
                g = g + p_ref[j].astype(F32)
            return g

        g = total(p_refs[0])
        for l in range(1, depth):
            g = jnp.where(pl.program_id(0) == l, total(p_refs[l]), g)
        g_ref[...] = g
        m_new = ADAM_B1 * m_ref[...] + (1.0 - ADAM_B1) * g
        v_new = ADAM_B2 * v_ref[...] + (1.0 - ADAM_B2) * (g * g)
        m_hat = m_new / (1.0 - ADAM_B1 ** ADAM_STEP)
        v_hat = v_new / (1.0 - ADAM_B2 ** ADAM_STEP)
        d_ref[...] = -ADAM_LR * (m_hat / (jnp.sqrt(v_hat) + ADAM_EPS) + ADAM_WD * w_ref[...])
        nm_ref[...] = m_new
        nv_ref[...] = v_new

    wblk = pl.BlockSpec((None, rb, c), lambda l, i: (l, i, 0))
    pblk = pl.BlockSpec((n_parts, rb, c), lambda l, i: (0, i, 0))
    return pl.pallas_call(
        body, name=name, grid=(depth, r // rb),
        in_specs=[wblk, wblk, wblk] + [pblk] * depth, out_specs=[wblk] * 4,
        out_shape=[jax.ShapeDtypeStruct((depth, r, c), F32)] * 4, compiler_params=_cparams(),
    )(w, m, v, *parts)


BIG = (("w_in", 2), ("w_uq", 2), ("w_ukv", 2), ("w_pa", 2), ("w_pb", 1), ("w_o", 1), ("w_gate", 2), ("w_up", 2), ("w_down", 1))
SMALL = ("norm_mix_g", "pool_w", "pool_scale", "q_norm_g", "kv_norm_g", "norm_ffn_g", "final_norm_g")
WEIGHTS = ("meta_tokens", "norm_mix_g", "w_in", "pool_w", "pool_scale", "q_norm_g", "kv_norm_g", "w_uq", "w_ukv", "w_pa", "w_pb",
           "w_o", "norm_ffn_g", "w_gate", "w_up", "w_down", "final_norm_g")
HEAD_QK = QK_NOPE + QK_ROPE
KR_END = Z_KR + QK_ROPE


def _cat_cols(parts):
    return [jnp.concatenate(parts, axis=1)]


def _cat_rows(parts):
    return [jnp.concatenate(parts, axis=0)]


def _arr_w_in(parts):
    full = jnp.concatenate(parts, axis=1)
    zc = lambda n: jnp.zeros((full.shape[0], n), full.dtype)
    return [jnp.concatenate([full[:, :Z_KR], zc(QK_NOPE), full[:, Z_KR:KR_END], zc(LANES - HEAD_QK), full[:, KR_END:]], axis=1)]


def _arr_w_uq(parts):
    full = jnp.concatenate(parts, axis=1)
    z = jnp.zeros((full.shape[0], HEAD_SLOT - HEAD_QK), full.dtype)
    pieces = []
    for hd in range(N_HEADS):
        pieces += [full[:, hd * HEAD_QK:(hd + 1) * HEAD_QK], z]
    return [jnp.concatenate(pieces, axis=1)]


def _arr_w_ukv(parts):
    full = jnp.concatenate(parts, axis=1)
    z = jnp.zeros((full.shape[0], HEAD_SLOT - QK_NOPE), full.dtype)
    wide = QK_NOPE + V_DIM
    k, v = [], []
    for hd in range(N_HEADS):
        k += [full[:, hd * wide:hd * wide + QK_NOPE], z]
        v.append(full[:, hd * wide + QK_NOPE:(hd + 1) * wide])
    return [jnp.concatenate(k, axis=1), jnp.concatenate(v, axis=1)]


def arrange(g, fn, out_shapes, name):
    def body(g_ref, *o_refs):
        for o_ref, val in zip(o_refs, fn([g_ref[p] for p in range(N_DEV)])):
            o_ref[...] = val

    return pl.pallas_call(
        body, name=name, grid=(1,),
        in_specs=[pl.BlockSpec(g.shape, lambda i: (0, 0, 0))],
        out_specs=[pl.BlockSpec(s, lambda i: (0, 0)) for s in out_shapes],
        out_shape=[jax.ShapeDtypeStruct(s, g.dtype) for s in out_shapes], compiler_params=_cparams(),
    )(g)


def _arranged_ranges(lo, hi):
    out = []
    for a, b, shift in ((0, Z_KR, 0), (Z_KR, KR_END, QK_NOPE), (KR_END, D_IN, LANES - QK_ROPE)):
        s, e = max(lo, a), min(hi, b)
        if s < e:
            out.append((s + shift, e + shift))
    return out


def _chunks_w_in(acc):
    cs = D_IN // N_DEV
    return [jnp.concatenate([acc[:, a:b] for a, b in _arranged_ranges(p * cs, (p + 1) * cs)], axis=1) for p in range(N_DEV)]


def _chunks_w_uq(acc):
    per = N_HEADS // N_DEV
    return [jnp.concatenate([acc[:, hd * HEAD_SLOT:hd * HEAD_SLOT + HEAD_QK] for hd in range(p * per, (p + 1) * per)], axis=1)
            for p in range(N_DEV)]


def _chunks_w_ukv(acc_k, acc_v):
    per = N_HEADS // N_DEV
    out = []
    for p in range(N_DEV):
        pieces = []
        for hd in range(p * per, (p + 1) * per):
            pieces += [acc_k[:, hd * HEAD_SLOT:hd * HEAD_SLOT + QK_NOPE], acc_v[:, hd * V_DIM:(hd + 1) * V_DIM]]
        out.append(jnp.concatenate(pieces, axis=1))
    return out


def _chunks_cols(acc):
    cs = acc.shape[1] // N_DEV
    return [acc[:, p * cs:(p + 1) * cs] for p in range(N_DEV)]


def _chunks_rows(acc):
    rs = acc.shape[0] // N_DEV
    return [acc[p * rs:(p + 1) * rs, :] for p in range(N_DEV)]


def _pack(parts, row_multiple):
    flat = jnp.concatenate([p.reshape(-1) for p in parts])
    return jnp.pad(flat, (0, -flat.shape[0] % (row_multiple * LANES))).reshape(-1, LANES)


def _unpack(packed, shapes):
    flat, out, off = packed.reshape(-1), [], 0
    for s in shapes:
        n = 1
        for d in s:
            n *= d
        out.append(flat[off:off + n].reshape(s))
        off += n
    return out


def _rope_table(lp, nb):
    inv = 1.0 / (ROPE_THETA ** (jnp.arange(0, QK_ROPE, 2, dtype=F32) / QK_ROPE))
    ang = jnp.arange(lp, dtype=F32)[:, None] * inv[None, :]
    cos, sin = jnp.cos(ang), jnp.sin(ang)
    z = lambda n: jnp.zeros((lp, n), F32)
    tail = LANES - QK_NOPE - QK_ROPE
    c = jnp.concatenate([jnp.ones((lp, QK_NOPE), F32), cos, cos, z(tail)], axis=1)
    cr = jnp.concatenate([z(QK_NOPE), cos, cos, z(tail)], axis=1)
    s1 = jnp.concatenate([z(QK_NOPE), -sin, z(HALF_ROPE), z(tail)], axis=1)
    s2 = jnp.concatenate([z(QK_NOPE), z(HALF_ROPE), sin, z(tail)], axis=1)
    return jnp.tile(jnp.concatenate([c, cr, s1, s2], axis=1), (nb, 1))


MIX = ("w_in", "w_uq", "w_ukv", "w_pa", "w_pb", "w_o")
FFN = ("w_gate", "w_up", "w_down")
ARRANGERS = {
    "w_in": (_arr_w_in, (("win", (D_MODEL, DZ)),)), "w_uq": (_arr_w_uq, (("wuq", (Q_RANK, QK_WIDTH)),)),
    "w_ukv": (_arr_w_ukv, (("wuk", (KV_RANK, QK_WIDTH)), ("wuv", (KV_RANK, D_MODEL)))),
    "w_pa": (_cat_cols, (("wpa", (POOL_WIDTH, D_MODEL)),)), "w_pb": (_cat_rows, (("wpb", (D_MODEL, D_MODEL)),)),
    "w_o": (_cat_rows, (("wo", (D_MODEL, D_MODEL)),)), "w_gate": (_cat_cols, (("wg", (D_MODEL, D_FF)),)),
    "w_up": (_cat_cols, (("wu", (D_MODEL, D_FF)),)), "w_down": (_cat_rows, (("wd", (D_FF, D_MODEL)),)),
}


def _operands(gathered, names, l):
    p = {}
    for n in names:
        fn, outs = ARRANGERS[n]
        for (key, _), a in zip(outs, arrange(gathered[n], fn, [s for _, s in outs], f"arrange_{n}_{l}")):
            p[key], p[key + "t"] = a, a.T
    return p


def _small_operands(small, l):
    pw = small["pool_w"][l].astype(BF16)
    return dict(g_mix=small["norm_mix_g"][l][None], gq=small["q_norm_g"][l][None], gkv=small["kv_norm_g"][l][None],
                g_ffn=small["norm_ffn_g"][l][None], ps=small["pool_scale"][l][None], pw=pw, pwt=jnp.swapaxes(pw, 1, 2))


TM_FWD, TM_BWD, TM_WGRAD = 512, 256, 512
HEADS_FWD, HEADS_BWD = 8, 4


class MeshComm:
    def __init__(self, w, meta_tokens):
        self.src = lambda n, l: w[n][l].astype(BF16)
        self.meta_tokens = meta_tokens
        self.core = lax.axis_index("c").astype(jnp.int32).reshape(1)
        self.later = [(n, 0) for n in FFN] + [(n, 1) for n in MIX + FFN]

    def first_weights(self):
        got = _gather_two_level([self.src(n, 0) for n in MIX] + [self.meta_tokens], "gather_mix_0")
        return dict(zip(MIX, got)), jnp.moveaxis(got[-1], 0, 1).reshape(N_META, D_MODEL)

    def later_first_hop(self):
        return exchange([self.src(n, l) for n, l in self.later], False, FIRST_HOP_PEERS)

    def later_second_hop(self, landed):
        return second_hop(landed)

    def later_weights(self, full, names, l):
        return {n: full[self.later.index((n, l))] for n in names}

    def pair_sums(self, own, names, tag):
        theirs = _to_sibling(own, f"pair_grads_{tag}")
        return [pair_add(a, b, self.core, name=f"pair_add_{n}_{tag}") for n, a, b in zip(names, own, theirs)]

    def scatter(self, sums):
        return exchange(sums, True, CHIP_PEERS, by_chip=True)

    def scatter_now(self, sums, name):
        return self.scatter(sums).run(name)


TM_FWD, TM_BWD, TM_WGRAD = 512, 256, 512
HEADS_FWD, HEADS_BWD = 8, 4


def _ffn_bwd_part(dh2, p, s, tag):
    d, ff = D_MODEL, D_FF // N_DEV
    wg_ = lambda n, x, ys, fn, shape: wgrad(x, ys, fn, shape, tm=TM_WGRAD, name=f"wgrad_{n}_{tag}")
    dh1, hn2, act, dgt, dup, dg_ffn = ffn_bwd(dh2, s["h1"], p["g_ffn"], s["gt"], s["up"], p["wgt"], p["wut"], p["wdt"], tm=TM_BWD,
                                              name=f"ffn_bwd_{tag}")
    chunks = [wg_("gate", hn2, [dgt], _chunks_cols, (d, ff)), wg_("up", hn2, [dup], _chunks_cols, (d, ff)),
              wg_("down", act, [dh2], _chunks_rows, (ff, d))]
    return dh1, chunks, dict(norm_ffn_g=dg_ffn[0])


def _mix_bwd_part(dh1, p, s, rope, nb, lp, tag, ride):
    d = D_MODEL
    wg_ = lambda n, x, ys, fn, shape: wgrad(x, ys, fn, shape, tm=TM_WGRAD, name=f"wgrad_{n}_{tag}")
    dga, dgb, dpa, dpb, do, dpool, dps, dpw = merge_bwd(dh1, s["z"], s["pa"], s["pb"], s["pooled"], p["pw"], p["pwt"], p["ps"],
                                                        p["wpat"], p["wpbt"], p["wot"], tm=TM_FWD, name=f"merge_bwd_{tag}")
    c_o = wg_("o", s["mg"], [dh1], _chunks_rows, (d // N_DEV, d))
    c_pa = wg_("pa", s["a"], [dpa], _chunks_cols, (POOL_WIDTH, d // N_DEV))
    c_pb = wg_("pb", s["o"], [dpb], _chunks_rows, (d // N_DEV, d))
    (dq, dk, dv), brought = attn_bwd(s["q"], s["k"], s["v"], s["o"], do, s["lse"], nb=nb, lp=lp, hb=HEADS_BWD,
                                     name=f"attn_bwd_{tag}", ride=ride)
    dh, hn, dz, cqn, ckvn, dqb, dkb, dvb, dg_mix, dgq, dgkv = in_proj_bwd(
        dh1, s["h"], p["g_mix"], s["z"], dq, dk, dv, dga, dgb, dpool, p["wint"], p["gq"], p["gkv"], p["wuqt"], p["wukt"], p["wuvt"],
        rope, tm=TM_BWD, lp=lp, nb=nb, name=f"in_proj_bwd_{tag}")
    c_in = wg_("in", hn, [dz], _chunks_w_in, (d, D_IN // N_DEV))
    c_uq = wg_("uq", cqn, [dqb], _chunks_w_uq, (Q_RANK, N_HEADS * HEAD_QK // N_DEV))
    c_ukv = wg_("ukv", ckvn, [dkb, dvb], _chunks_w_ukv, (KV_RANK, N_HEADS * (QK_NOPE + V_DIM) // N_DEV))
    small = dict(pool_scale=dps[0], pool_w=dpw, norm_mix_g=dg_mix[0], q_norm_g=dgq[0], kv_norm_g=dgkv[0])
    return dh, [c_in, c_uq, c_ukv, c_pa, c_pb, c_o], small, brought


def train_step(x, loss_target, small, comm):
    nb, seq, d = x.shape
    lp = -(-(N_META + seq) // TQ) * TQ
    t = nb * lp
    assert t % TM_FWD == 0 and nb <= 2 and DEPTH == 2
    rope = _rope_table(lp, nb)
    gathered, meta = comm.first_weights()
    pad = jnp.zeros((nb, lp - N_META - seq, d), F32)
    h = jnp.concatenate([jnp.broadcast_to(meta[None], (nb, N_META, d)), x, pad], axis=1).reshape(t, d)
    target = jnp.concatenate([jnp.zeros((nb, N_META, d), F32), loss_target, pad], axis=1).reshape(t, d)

    params, saved = [], []
    for l in range(DEPTH):
        p = _small_operands(small, l)
        if l == 0:
            p.update(_operands(gathered, MIX, 0))
        else:
            p.update(_operands(comm.later_weights(full, MIX + FFN, 1), MIX + FFN, 1))
        z, q, k, v = in_proj_fwd(h, p["g_mix"], p["win"], p["gq"], p["gkv"], p["wuq"], p["wuk"], p["wuv"], rope, tm=TM_FWD,
                                 name=f"in_proj_fwd_{l}")
        (o, lse), landed = attn_fwd(q, k, v, nb=nb, lp=lp, hb=HEADS_FWD, name=f"attn_fwd_{l}",
                                    ride=comm.later_first_hop() if l == 0 else None)
        (h1, pooled, a, pa, pb, mg), got = merge_fwd(h, z, o, p["pw"], p["ps"], p["wpa"], p["wpb"], p["wo"], tm=TM_FWD, lp=lp,
                                                      nb=nb, name=f"merge_fwd_{l}",
                                                      ride=comm.later_second_hop(landed) if l == 0 else None)
        if l == 0:
            full = got
            p.update(_operands(comm.later_weights(full, FFN, 0), FFN, 0))
        h2, gt, up = ffn_fwd(h1, p["g_ffn"], p["wg"], p["wu"], p["wd"], tm=TM_FWD, name=f"ffn_fwd_{l}")
        params.append(p)
        saved.append(dict(h=h, z=z, q=q, k=k, v=v, o=o, lse=lse, h1=h1, pooled=pooled, a=a, pa=pa, pb=pb, mg=mg, gt=gt, up=up))
        h = h2
    parts, dh, dgf = loss_head(h, small["final_norm_g"][None], target, tm=TM_FWD, lp=lp, nb=nb, seq=seq, name="loss_head")
    loss = jnp.sum(parts[::8, 0])

    sums = {}
    dh, c_ffn1, small1 = _ffn_bwd_part(dh, params[1], saved[1], 1)
    s_ffn1 = comm.pair_sums(c_ffn1, FFN, "ffn_1")
    dh, c_mix1, sm, brought = _mix_bwd_part(dh, params[1], saved[1], rope, nb, lp, 1, comm.scatter(s_ffn1))
    small1.update(sm)
    sums.update({(n, 1): a for n, a in zip(FFN, brought)})
    s_mix1 = comm.pair_sums(c_mix1, MIX, "mix_1")
    dh, c_ffn0, small0 = _ffn_bwd_part(dh, params[0], saved[0], 0)
    s_ffn0 = comm.pair_sums(c_ffn0, FFN, "ffn_0")
    dh, c_mix0, sm, brought = _mix_bwd_part(dh, params[0], saved[0], rope, nb, lp, 0, comm.scatter(s_mix1 + s_ffn0))
    small0.update(sm)
    sums.update({(n, l): a for (n, l), a in zip([(n, 1) for n in MIX] + [(n, 0) for n in FFN], brought)})
    dh = dh.reshape(nb, lp, d)
    dmeta = jnp.sum(dh[:, :N_META], axis=0)
    meta_chunks = jnp.transpose(dmeta.reshape(N_META, N_CHIPS, 2, d // N_DEV), (2, 1, 0, 3)).astype(BF16)
    s_last = comm.pair_sums(c_mix0 + [meta_chunks], MIX + ("meta_tokens",), "mix_0")
    last = comm.scatter_now(s_last, "scatter_mix_0")
    sums.update({(n, 0): a for n, a in zip(MIX + ("meta_tokens",), last)})
    small_grads = {n: jnp.stack([small0[n], small1[n]]) for n in small0}
    small_grads["final_norm_g"] = dgf[0]
    return loss, dh[:, N_META:N_META + seq], sums, small_grads


def kernel(x, meta_tokens, norm_mix_g, w_in, pool_w, pool_scale, q_norm_g, kv_norm_g, w_uq, w_ukv, w_pa, w_pb, w_o, norm_ffn_g, w_gate, w_up, w_down, final_norm_g, loss_target, m_meta_tokens, m_norm_mix_g, m_w_in, m_pool_w, m_pool_scale, m_q_norm_g, m_kv_norm_g, m_w_uq, m_w_ukv, m_w_pa, m_w_pb, m_w_o, m_norm_ffn_g, m_w_gate, m_w_up, m_w_down, m_final_norm_g, v_meta_tokens, v_norm_mix_g, v_w_in, v_pool_w, v_pool_scale, v_q_norm_g, v_kv_norm_g, v_w_uq, v_w_ukv, v_w_pa, v_w_pb, v_w_o, v_norm_ffn_g, v_w_gate, v_w_up, v_w_down, v_final_norm_g):
    args = dict(locals())
    w = {n: args[n] for n in WEIGHTS}
    m = {n: args["m_" + n] for n in WEIGHTS}
    v = {n: args["v_" + n] for n in WEIGHTS}
    small = {n: w[n] for n in SMALL}

    loss, grad_x, sums, small_grads = train_step(x, loss_target, small, MeshComm(w, meta_tokens))
    loss = lax.psum(loss, ("x", "y", "c"))
    (small_recv,) = exchange([_pack([small_grads[n] for n in SMALL], 8)], False, ALL_PEERS).run("gather_small_grads")

    out = {n: adamw(w[n], m[n], v[n], [sums[(n, l)] for l in range(DEPTH)], name=f"adamw_{n}") for n, _ in BIG}
    out["meta_tokens"] = [a[0] for a in adamw(meta_tokens[None], m["meta_tokens"][None], v["meta_tokens"][None],
                                              [sums[("meta_tokens", 0)]], name="adamw_meta_tokens")]
    pk = lambda d: _pack([d[n] for n in SMALL], 8)[None]
    packed = adamw(pk(w), pk(m), pk(v), [small_recv], name="adamw_small")
    shapes = [w[n].shape for n in SMALL]
    for n, *kinds in zip(SMALL, *[_unpack(packed[kind][0], shapes) for kind in range(4)]):
        out[n] = kinds
    return (loss, grad_x, *[out[n][kind] for kind in range(4) for n in WEIGHTS])
```

```python
import functools
import math

import jax
import jax.numpy as jnp
from jax import lax
from jax.experimental import pallas as pl
from jax.experimental.pallas import tpu as pltpu

F32, BF16 = jnp.float32, jnp.bfloat16

D_MODEL = 1024
N_META = 16
N_HEADS = 16
QK_NOPE, QK_ROPE, V_DIM = 64, 32, 64
HALF_ROPE = QK_ROPE // 2
Q_RANK, KV_RANK = 256, 128
POOL_WINDOWS = (2, 4, 8, 16)
POOL_GROUP = 128
POOL_WIDTH = POOL_GROUP * len(POOL_WINDOWS)
POOL_HALO = 16
D_FF = 2816
D_IN = 2976
NORM_EPS = 1e-6
SM_SCALE = (QK_NOPE + QK_ROPE) ** -0.5
LOG2E = math.log2(math.e)
EXP2_SCALE = SM_SCALE * LOG2E
MASK_VALUE = -1e30
ROPE_THETA = 10000.0
DEPTH = 2
N_DEV = 8

ADAM_LR, ADAM_B1, ADAM_B2, ADAM_EPS, ADAM_WD, ADAM_STEP = 0.001, 0.9, 0.999, 1e-08, 0.01, 10

LANES = 128
HEAD_SLOT = LANES
QK_WIDTH = N_HEADS * HEAD_SLOT
Z_CQ, Z_CKV, Z_KR, Z_GA, Z_GB, DZ = 512, 768, 896, 1024, 2048, 3072
TQ = TK = 256
VMEM_LIMIT = 56 * 1024 * 1024


def _cparams():
    return pltpu.CompilerParams(vmem_limit_bytes=VMEM_LIMIT)


def _rows(tm, width, col=0):
    return pl.BlockSpec((tm, width), lambda i: (i, col))


def _whole(shape):
    zeros = (0,) * len(shape)
    return pl.BlockSpec(shape, lambda i: zeros, pipeline_mode=pl.Buffered(1))


def _acc(shape):
    zeros = (0,) * len(shape)
    return pl.BlockSpec(shape, lambda i: zeros)


def _dot(a, b):
    return jnp.dot(a, b, preferred_element_type=F32)


def _dot_tn(a, b):
    return lax.dot_general(a, b, (((0,), (0,)), ((), ())), preferred_element_type=F32)


def _dot_nt(a, b):
    return lax.dot_general(a, b, (((1,), (1,)), ((), ())), preferred_element_type=F32)


def _rms(x):
    r = lax.rsqrt(jnp.mean(x * x, axis=-1, keepdims=True) + NORM_EPS)
    return x * r, r


def _rms_bwd(dy, xhat, r, g):
    dg = jnp.sum(dy * xhat, axis=0, keepdims=True)
    dxh = dy * g
    dx = r * (dxh - xhat * jnp.mean(dxh * xhat, axis=-1, keepdims=True))
    return dx, dg


def _sigmoid(x):
    return 1.0 / (1.0 + jnp.exp(-x))


def _rope_fwd(q, c, s1, s2):
    w = q.shape[1]
    return q * c + pltpu.roll(q, w - HALF_ROPE, 1) * s1 + pltpu.roll(q, HALF_ROPE, 1) * s2


def _rope_bwd(dq, c, s1, s2):
    w = dq.shape[1]
    return dq * c + pltpu.roll(dq * s1, HALF_ROPE, 1) + pltpu.roll(dq * s2, w - HALF_ROPE, 1)


def _rope_tables(rope, reps):
    c, cr, s1, s2 = (rope[:, k * LANES:(k + 1) * LANES] for k in range(4))
    if reps > 1:
        return jnp.tile(c, (1, reps)), jnp.tile(s1, (1, reps)), jnp.tile(s2, (1, reps))
    return cr, s1, s2


def _seq_pos(gi, lp, nb):
    pos = gi
    for b in range(1, nb):
        pos = jnp.where(gi >= b * lp, gi - b * lp, pos)
    return pos


_ANY = pl.BlockSpec(memory_space=pl.ANY)


def _carrying_call(body, ride, operands, *, name, grid, in_specs, out_specs, out_shape, scratch_shapes=()):
    n_in, n_out = len(in_specs), len(out_specs)
    if ride is None:
        out = pl.pallas_call(body, name=name, grid=grid, in_specs=in_specs, out_specs=out_specs, out_shape=out_shape,
                             scratch_shapes=list(scratch_shapes), compiler_params=_cparams())(*operands)
        return out, []
    ne = len(ride.arrays)

    def carrying(*refs):
        ins, r_in, rest = refs[:n_in], refs[n_in:n_in + ne], refs[n_in + ne:]
        outs, r_out, rest = rest[:n_out], rest[n_out:n_out + ne], rest[n_out + ne:]
        scratch, sems = rest[:len(scratch_shapes)], rest[len(scratch_shapes):]
        ids = [pl.program_id(a) for a in range(len(grid))]
        first = functools.reduce(jnp.logical_and, [i == 0 for i in ids])
        last = functools.reduce(jnp.logical_and, [i == g - 1 for i, g in zip(ids, grid)])

        @pl.when(first)
        def _():
            ride.start(r_in, r_out, sems)

        body(*ins, *outs, *scratch)

        @pl.when(last)
        def _():
            ride.wait(r_in, r_out, sems)

    out = pl.pallas_call(
        carrying, name=name, grid=grid, in_specs=list(in_specs) + [_ANY] * ne, out_specs=list(out_specs) + [_ANY] * ne,
        out_shape=list(out_shape) + ride.out_shapes, scratch_shapes=list(scratch_shapes) + ride.scratch,
        input_output_aliases=ride.aliases(n_in, n_out), compiler_params=_cparams(),
    )(*operands, *ride.arrays)
    return out[:n_out], out[n_out:]


def in_proj_fwd(h, g_mix, win, gq, gkv, wuq, wuk, wuv, rope, *, tm, name):
    t = h.shape[0]

    def body(h_ref, g_ref, win_ref, gq_ref, gkv_ref, wuq_ref, wuk_ref, wuv_ref, rope_ref, z_ref, q_ref, k_ref, v_ref):
        xhat, _ = _rms(h_ref[...])
        hn = (xhat * g_ref[...]).astype(BF16)
        z = _dot(hn, win_ref[...])
        z_ref[...] = z
        rope_t = rope_ref[...]
        xq, _ = _rms(z[:, Z_CQ:Z_CKV])
        cqn = (xq * gq_ref[...]).astype(BF16)
        q = _rope_fwd(_dot(cqn, wuq_ref[...]), *_rope_tables(rope_t, N_HEADS))
        q_ref[...] = q.astype(BF16)
        xkv, _ = _rms(z[:, Z_CKV:Z_KR])
        ckvn = (xkv * gkv_ref[...]).astype(BF16)
        kr = _rope_fwd(z[:, Z_KR:Z_GA], *_rope_tables(rope_t, 1))
        k_ref[...] = (_dot(ckvn, wuk_ref[...]) + jnp.tile(kr, (1, N_HEADS))).astype(BF16)
        v_ref[...] = _dot(ckvn, wuv_ref[...]).astype(BF16)

    return pl.pallas_call(
        body, name=name, grid=(t // tm,),
        in_specs=[_rows(tm, D_MODEL), _whole((1, D_MODEL)), _whole((D_MODEL, DZ)), _whole((1, Q_RANK)), _whole((1, KV_RANK)),
                  _whole((Q_RANK, QK_WIDTH)), _whole((KV_RANK, QK_WIDTH)), _whole((KV_RANK, D_MODEL)), _rows(tm, 4 * LANES)],
        out_specs=[_rows(tm, DZ), _rows(tm, QK_WIDTH), _rows(tm, QK_WIDTH), _rows(tm, D_MODEL)],
        out_shape=[jax.ShapeDtypeStruct((t, DZ), F32), jax.ShapeDtypeStruct((t, QK_WIDTH), BF16),
                   jax.ShapeDtypeStruct((t, QK_WIDTH), BF16), jax.ShapeDtypeStruct((t, D_MODEL), BF16)],
        compiler_params=_cparams(),
    )(h, g_mix, win, gq, gkv, wuq, wuk, wuv, rope)


def attn_fwd(q, k, v, *, nb, lp, hb, name, ride=None):
    t = q.shape[0]
    nq, tail = lp // TQ, lp % TQ
    assert tail % LANES == 0

    def body(q_ref, k_ref, v_ref, o_ref, lse_ref, vt):
        for pr in range(hb // 2):
            vt[pr] = v_ref[:, pr * LANES:(pr + 1) * LANES].T

        def q_block(qs, tq, whole_k):
            qh = [q_ref[pl.ds(qs, tq), hd * HEAD_SLOT:(hd + 1) * HEAD_SLOT] for hd in range(hb)]
            keep = lax.broadcasted_iota(jnp.int32, (tq, tq), 0) <= lax.broadcasted_iota(jnp.int32, (tq, tq), 1)

            def k_step(ks, tk, c, masked):
                sts = [_dot_nt(k_ref[pl.ds(ks, tk), hd * HEAD_SLOT:(hd + 1) * HEAD_SLOT], qh[hd]) for hd in range(hb)]
                ps, stats = [], []
                for hd in range(hb):
                    m, l, _ = c[hd]
                    st = jnp.where(keep, sts[hd], MASK_VALUE) if masked else sts[hd]
                    m_new = jnp.maximum(m, jnp.max(st, axis=0, keepdims=True))
                    p = jnp.exp2((st - m_new) * EXP2_SCALE)
                    alpha = jnp.exp2((m - m_new) * EXP2_SCALE)
                    ps.append(p.astype(BF16))
                    stats.append((m_new, alpha * l + jnp.sum(p, axis=0, keepdims=True), alpha))
                pvs = [_dot(vt[hd // 2, :, pl.ds(ks, tk)], ps[hd]) for hd in range(hb)]
                return tuple((stats[hd][0], stats[hd][1], stats[hd][2] * c[hd][2] + pvs[hd]) for hd in range(hb))

            init = tuple((jnp.full((1, tq), MASK_VALUE, F32), jnp.zeros((1, tq), F32), jnp.zeros((LANES, tq), F32))
                         for _ in range(hb))
            c = lax.fori_loop(0, whole_k, lambda kj, c: k_step(pl.multiple_of(kj * TK, TK), TK, c, False), init)
            c = k_step(qs, tq, c, True)
            sub = lax.broadcasted_iota(jnp.int32, (LANES, tq), 0)
            for pr in range(hb // 2):
                (m0, l0, a0), (m1, l1, a1) = c[2 * pr], c[2 * pr + 1]
                o_ref[pl.ds(qs, tq), pr * LANES:(pr + 1) * LANES] = jnp.where(sub < V_DIM, a0 / l0, a1 / l1).T.astype(BF16)
                lse_ref[2 * pr, :, pl.ds(qs, tq)] = m0 * SM_SCALE + jnp.log(l0)
                lse_ref[2 * pr + 1, :, pl.ds(qs, tq)] = m1 * SM_SCALE + jnp.log(l1)

        def whole_q_block(qi, carry):
            q_block(pl.multiple_of(qi * TQ, TQ), TQ, qi)
            return carry

        lax.fori_loop(0, nq, whole_q_block, 0)
        if tail:
            q_block(nq * TQ, tail, nq)

    blk = lambda w: pl.BlockSpec((lp, w), lambda b, g: (b, g))
    return _carrying_call(
        body, ride, (q, k, v), name=name, grid=(nb, N_HEADS // hb),
        in_specs=[blk(hb * HEAD_SLOT), blk(hb * HEAD_SLOT), blk(hb * V_DIM)],
        out_specs=[blk(hb * V_DIM), pl.BlockSpec((hb, 1, lp), lambda b, g: (g, 0, b))],
        out_shape=[jax.ShapeDtypeStruct((t, D_MODEL), BF16), jax.ShapeDtypeStruct((N_HEADS, 1, t), F32)],
        scratch_shapes=[pltpu.VMEM((hb // 2, LANES, lp), BF16)])


def _pool_band_fwd(i, tm, lp, nb):
    r = lax.broadcasted_iota(jnp.int32, (tm, POOL_HALO + tm), 0)
    e = lax.broadcasted_iota(jnp.int32, (tm, POOL_HALO + tm), 1)
    diff = r + POOL_HALO - e
    pos = _seq_pos(i * tm + lax.broadcasted_iota(jnp.int32, (tm, 1), 0), lp, nb)
    out = []
    for w in POOL_WINDOWS:
        cnt = jnp.minimum(pos + 1, w)
        band = jnp.where((diff >= 0) & (diff < cnt), 1.0, 0.0).astype(BF16)
        out.append((band, cnt.astype(F32)))
    return out


def merge_fwd(h, z, o, pw, ps, wpa, wpb, wo, *, tm, lp, nb, name, ride=None):
    t = h.shape[0]
    hb = tm // POOL_HALO

    def body(h_ref, u_ref, uprev_ref, ga_ref, gb_ref, o_ref, pw_ref, ps_ref, wpa_ref, wpb_ref, wo_ref,
             h1_ref, pooled_ref, a_ref, pa_ref, pb_ref, mg_ref):
        i = pl.program_id(0)
        u = u_ref[...]
        uext = jnp.concatenate([uprev_ref[...], u], axis=0).astype(BF16)
        pooled, ys = [], []
        for g, (band, cnt) in enumerate(_pool_band_fwd(i, tm, lp, nb)):
            gs = slice(g * POOL_GROUP, (g + 1) * POOL_GROUP)
            pg = (_dot(band, uext[:, gs]) / cnt - u[:, gs]).astype(BF16)
            pooled.append(pg)
            ys.append(_dot(pg, pw_ref[g]))
        pooled_ref[...] = jnp.concatenate(pooled, axis=1)
        a = (jnp.concatenate(ys, axis=1) * ps_ref[...]).astype(BF16)
        a_ref[...] = a
        pa = _dot(a, wpa_ref[...])
        pb = _dot(o_ref[...], wpb_ref[...])
        pa_ref[...] = pa.astype(BF16)
        pb_ref[...] = pb.astype(BF16)
        mg = (_sigmoid(ga_ref[...]) * pa + _sigmoid(gb_ref[...]) * pb).astype(BF16)
        mg_ref[...] = mg
        h1_ref[...] = h_ref[...] + _dot(mg, wo_ref[...])

    halo = pl.BlockSpec((POOL_HALO, POOL_WIDTH), lambda i: (jnp.maximum(i * hb - 1, 0), 0))
    return _carrying_call(
        body, ride, (h, z, z, z, z, o, pw, ps, wpa, wpb, wo), name=name, grid=(t // tm,),
        in_specs=[_rows(tm, D_MODEL), _rows(tm, POOL_WIDTH), halo, _rows(tm, D_MODEL, 1), _rows(tm, D_MODEL, 2), _rows(tm, D_MODEL),
                  _whole((4, POOL_GROUP, POOL_GROUP)), _whole((1, POOL_WIDTH)), _whole((POOL_WIDTH, D_MODEL)),
                  _whole((D_MODEL, D_MODEL)), _whole((D_MODEL, D_MODEL))],
        out_specs=[_rows(tm, D_MODEL), _rows(tm, POOL_WIDTH), _rows(tm, POOL_WIDTH), _rows(tm, D_MODEL), _rows(tm, D_MODEL),
                   _rows(tm, D_MODEL)],
        out_shape=[jax.ShapeDtypeStruct((t, D_MODEL), F32), jax.ShapeDtypeStruct((t, POOL_WIDTH), BF16),
                   jax.ShapeDtypeStruct((t, POOL_WIDTH), BF16), jax.ShapeDtypeStruct((t, D_MODEL), BF16),
                   jax.ShapeDtypeStruct((t, D_MODEL), BF16), jax.ShapeDtypeStruct((t, D_MODEL), BF16)])


def ffn_fwd(h1, g, wg, wu, wd, *, tm, name):
    t = h1.shape[0]

    def body(h_ref, g_ref, wg_ref, wu_ref, wd_ref, h2_ref, gt_ref, up_ref):
        h = h_ref[...]
        xhat, _ = _rms(h)
        hn = (xhat * g_ref[...]).astype(BF16)
        gt = _dot(hn, wg_ref[...])
        up = _dot(hn, wu_ref[...])
        gt_ref[...] = gt.astype(BF16)
        up_ref[...] = up.astype(BF16)
        act = (gt * _sigmoid(gt) * up).astype(BF16)
        h2_ref[...] = h + _dot(act, wd_ref[...])

    return pl.pallas_call(
        body, name=name, grid=(t // tm,),
        in_specs=[_rows(tm, D_MODEL), _whole((1, D_MODEL)), _whole((D_MODEL, D_FF)), _whole((D_MODEL, D_FF)), _whole((D_FF, D_MODEL))],
        out_specs=[_rows(tm, D_MODEL), _rows(tm, D_FF), _rows(tm, D_FF)],
        out_shape=[jax.ShapeDtypeStruct((t, D_MODEL), F32), jax.ShapeDtypeStruct((t, D_FF), BF16), jax.ShapeDtypeStruct((t, D_FF), BF16)],
        compiler_params=_cparams(),
    )(h1, g, wg, wu, wd)


def loss_head(h, g, target, *, tm, lp, nb, seq, name):
    t = h.shape[0]
    nt = t // tm

    def body(h_ref, g_ref, t_ref, loss_ref, dh_ref, dg_ref):
        i = pl.program_id(0)
        pos = _seq_pos(i * tm + lax.broadcasted_iota(jnp.int32, (tm, 1), 0), lp, nb)
        real = (pos >= N_META) & (pos < N_META + seq)
        xhat, r = _rms(h_ref[...])
        gg = g_ref[...]
        err = jnp.where(real, xhat * gg - t_ref[...], 0.0)
        loss_ref[...] = jnp.full((8, LANES), 0.5 * jnp.sum(err * err) / D_MODEL, F32)
        dx, dg = _rms_bwd(err * (1.0 / D_MODEL), xhat, r, gg)
        dh_ref[...] = dx

        @pl.when(i == 0)
        def _():
            dg_ref[...] = jnp.zeros_like(dg_ref)

        dg_ref[...] += dg

    return pl.pallas_call(
        body, name=name, grid=(nt,),
        in_specs=[_rows(tm, D_MODEL), _whole((1, D_MODEL)), _rows(tm, D_MODEL)],
        out_specs=[pl.BlockSpec((8, LANES), lambda i: (i, 0)), _rows(tm, D_MODEL), _acc((1, D_MODEL))],
        out_shape=[jax.ShapeDtypeStruct((nt * 8, LANES), F32), jax.ShapeDtypeStruct((t, D_MODEL), F32),
                   jax.ShapeDtypeStruct((1, D_MODEL), F32)],
        compiler_params=_cparams(),
    )(h, g, target)


def wgrad(x, ys, chunk_fn, chunk_shape, *, tm, name):
    t, m = x.shape

    def body(x_ref, *refs):
        y_refs, o_ref, accs = refs[:len(ys)], refs[len(ys)], refs[len(ys) + 1:]
        i = pl.program_id(0)

        @pl.when(i == 0)
        def _():
            for acc in accs:
                acc[...] = jnp.zeros_like(acc)

        xb = x_ref[...].astype(BF16)
        for y_ref, acc in zip(y_refs, accs):
            acc[...] += _dot_tn(xb, y_ref[...].astype(BF16))

        @pl.when(i == t // tm - 1)
        def _():
            for p, chunk in enumerate(chunk_fn(*accs)):
                o_ref[p % 2, p // 2] = chunk.astype(BF16)

    out = (2, N_DEV // 2) + tuple(chunk_shape)
    return pl.pallas_call(
        body, name=name, grid=(t // tm,),
        in_specs=[_rows(tm, m)] + [_rows(tm, y.shape[1]) for y in ys], out_specs=_acc(out),
        out_shape=jax.ShapeDtypeStruct(out, BF16), scratch_shapes=[pltpu.VMEM((m, y.shape[1]), F32) for y in ys],
        compiler_params=_cparams(),
    )(x, *ys)


def ffn_bwd(dh2, h1, g, gt, up, wgt, wut, wdt, *, tm, name):
    t = h1.shape[0]

    def body(dh2_ref, h_ref, g_ref, gt_ref, up_ref, wgt_ref, wut_ref, wdt_ref, dh1_ref, hn_ref, act_ref, dgt_ref, dup_ref, dg_ref):
        dh2 = dh2_ref[...]
        dact = _dot(dh2.astype(BF16), wdt_ref[...])
        gt = gt_ref[...].astype(F32)
        up = up_ref[...].astype(F32)
        sg = _sigmoid(gt)
        silu = gt * sg
        act_ref[...] = (silu * up).astype(BF16)
        dgt = (dact * up * (sg * (1.0 + gt * (1.0 - sg)))).astype(BF16)
        dup = (dact * silu).astype(BF16)
        dgt_ref[...] = dgt
        dup_ref[...] = dup
        dhn = _dot(dgt, wgt_ref[...]) + _dot(dup, wut_ref[...])
        xhat, r = _rms(h_ref[...])
        gg = g_ref[...]
        hn_ref[...] = (xhat * gg).astype(BF16)
        dx, dg = _rms_bwd(dhn, xhat, r, gg)
        dh1_ref[...] = dh2 + dx

        @pl.when(pl.program_id(0) == 0)
        def _():
            dg_ref[...] = jnp.zeros_like(dg_ref)

        dg_ref[...] += dg

    return pl.pallas_call(
        body, name=name, grid=(t // tm,),
        in_specs=[_rows(tm, D_MODEL), _rows(tm, D_MODEL), _whole((1, D_MODEL)), _rows(tm, D_FF), _rows(tm, D_FF),
                  _whole((D_FF, D_MODEL)), _whole((D_FF, D_MODEL)), _whole((D_MODEL, D_FF))],
        out_specs=[_rows(tm, D_MODEL), _rows(tm, D_MODEL), _rows(tm, D_FF), _rows(tm, D_FF), _rows(tm, D_FF), _acc((1, D_MODEL))],
        out_shape=[jax.ShapeDtypeStruct((t, D_MODEL), F32), jax.ShapeDtypeStruct((t, D_MODEL), BF16),
                   jax.ShapeDtypeStruct((t, D_FF), BF16), jax.ShapeDtypeStruct((t, D_FF), BF16),
                   jax.ShapeDtypeStruct((t, D_FF), BF16), jax.ShapeDtypeStruct((1, D_MODEL), F32)],
        compiler_params=_cparams(),
    )(dh2, h1, g, gt, up, wgt, wut, wdt)


def merge_bwd(dh1, z, pa, pb, pooled, pw, pwt, ps, wpat, wpbt, wot, *, tm, name):
    t = dh1.shape[0]

    def body(dh1_ref, ga_ref, gb_ref, pa_ref, pb_ref, pooled_ref, pw_ref, pwt_ref, ps_ref, wpat_ref, wpbt_ref, wot_ref,
             dga_ref, dgb_ref, dpa_ref, dpb_ref, do_ref, dpool_ref, dps_ref, dpw_ref):
        dmg = _dot(dh1_ref[...].astype(BF16), wot_ref[...])
        sa = _sigmoid(ga_ref[...])
        sb = _sigmoid(gb_ref[...])
        dga_ref[...] = (dmg * pa_ref[...].astype(F32) * sa * (1.0 - sa)).astype(BF16)
        dgb_ref[...] = (dmg * pb_ref[...].astype(F32) * sb * (1.0 - sb)).astype(BF16)
        dpa = (dmg * sa).astype(BF16)
        dpb = (dmg * sb).astype(BF16)
        dpa_ref[...] = dpa
        dpb_ref[...] = dpb
        do_ref[...] = _dot(dpb, wpbt_ref[...]).astype(BF16)
        da = _dot(dpa, wpat_ref[...])
        pooled = pooled_ref[...]
        ps = ps_ref[...]

        @pl.when(pl.program_id(0) == 0)
        def _():
            dps_ref[...] = jnp.zeros_like(dps_ref)
            dpw_ref[...] = jnp.zeros_like(dpw_ref)

        dps, dpool = [], []
        for g in range(len(POOL_WINDOWS)):
            gs = slice(g * POOL_GROUP, (g + 1) * POOL_GROUP)
            y = _dot(pooled[:, gs], pw_ref[g])
            dps.append(jnp.sum(da[:, gs] * y, axis=0, keepdims=True))
            dy = (da[:, gs] * ps[:, gs]).astype(BF16)
            dpool.append(_dot(dy, pwt_ref[g]))
            dpw_ref[g] += _dot_tn(pooled[:, gs], dy)
        dps_ref[...] += jnp.concatenate(dps, axis=1)
        dpool_ref[...] = jnp.concatenate(dpool, axis=1)

    return pl.pallas_call(
        body, name=name, grid=(t // tm,),
        in_specs=[_rows(tm, D_MODEL), _rows(tm, D_MODEL, 1), _rows(tm, D_MODEL, 2), _rows(tm, D_MODEL), _rows(tm, D_MODEL),
                  _rows(tm, POOL_WIDTH), _whole((4, POOL_GROUP, POOL_GROUP)), _whole((4, POOL_GROUP, POOL_GROUP)),
                  _whole((1, POOL_WIDTH)), _whole((D_MODEL, POOL_WIDTH)), _whole((D_MODEL, D_MODEL)), _whole((D_MODEL, D_MODEL))],
        out_specs=[_rows(tm, D_MODEL), _rows(tm, D_MODEL), _rows(tm, D_MODEL), _rows(tm, D_MODEL), _rows(tm, D_MODEL),
                   _rows(tm, POOL_WIDTH), _acc((1, POOL_WIDTH)), _acc((4, POOL_GROUP, POOL_GROUP))],
        out_shape=[jax.ShapeDtypeStruct((t, D_MODEL), BF16)] * 5
        + [jax.ShapeDtypeStruct((t, POOL_WIDTH), F32), jax.ShapeDtypeStruct((1, POOL_WIDTH), F32),
           jax.ShapeDtypeStruct((4, POOL_GROUP, POOL_GROUP), F32)],
        compiler_params=_cparams(),
    )(dh1, z, z, pa, pb, pooled, pw, pwt, ps, wpat, wpbt, wot)


def attn_bwd(q, k, v, o, do, lse, *, nb, lp, hb, name, ride=None):
    t = q.shape[0]
    nq, tail = lp // TQ, lp % TQ
    assert tail % LANES == 0

    def body(q_ref, k_ref, v_ref, o_ref, do_ref, lse_ref, dq_ref, dk_ref, dv_ref, kt, doh, lse_row, delta_row, dqt):
        lane = lax.broadcasted_iota(jnp.int32, (lp, LANES), 1)
        first = lane < V_DIM
        sub = lax.broadcasted_iota(jnp.int32, (LANES, lp), 0)
        for pr in range(hb // 2):
            ls = slice(pr * LANES, (pr + 1) * LANES)
            do = do_ref[:, ls]
            doh[2 * pr] = jnp.where(first, do, jnp.zeros_like(do))
            doh[2 * pr + 1] = jnp.where(first, jnp.zeros_like(do), do)
            prod_t = (do.astype(F32) * o_ref[:, ls].astype(F32)).T
            delta_row[2 * pr] = jnp.sum(jnp.where(sub < V_DIM, prod_t, 0.0), axis=0, keepdims=True)
            delta_row[2 * pr + 1] = jnp.sum(jnp.where(sub < V_DIM, 0.0, prod_t), axis=0, keepdims=True)
        for hd in range(hb):
            lse_row[hd] = lse_ref[hd] * LOG2E
            kt[hd] = k_ref[:, hd * HEAD_SLOT:(hd + 1) * HEAD_SLOT].T
        dqt[...] = jnp.zeros(dqt.shape, F32)
        heads = range(hb)
        hss = [slice(hd * HEAD_SLOT, (hd + 1) * HEAD_SLOT) for hd in heads]

        def k_block(ks, tk, next_q):
            keep = lax.broadcasted_iota(jnp.int32, (tk, tk), 0) <= lax.broadcasted_iota(jnp.int32, (tk, tk), 1)

            def q_step(qs, tq, c, masked):
                qhs = [q_ref[pl.ds(qs, tq), hss[hd]] for hd in heads]
                dos = [doh[hd, pl.ds(qs, tq), :] for hd in heads]
                sts = [_dot_nt(k_ref[pl.ds(ks, tk), hss[hd]], qhs[hd]) for hd in heads]
                dpts = [_dot_nt(v_ref[pl.ds(ks, tk), (hd // 2) * LANES:(hd // 2 + 1) * LANES], dos[hd]) for hd in heads]
                pts, dsts = [], []
                for hd in heads:
                    st = jnp.where(keep, sts[hd], MASK_VALUE) if masked else sts[hd]
                    pt = jnp.exp2(st * EXP2_SCALE - lse_row[hd, :, pl.ds(qs, tq)])
                    dsts.append((pt * (dpts[hd] - delta_row[hd, :, pl.ds(qs, tq)])).astype(BF16))
                    pts.append(pt.astype(BF16))
                dvs = [_dot(pts[hd], dos[hd]) for hd in heads]
                dks = [_dot(dsts[hd], qhs[hd]) for hd in heads]
                dqs = [_dot(kt[hd, :, pl.ds(ks, tk)], dsts[hd]) for hd in heads]
                for hd in heads:
                    dqt[hd, :, pl.ds(qs, tq)] += dqs[hd]
                return tuple((c[hd][0] + dks[hd], c[hd][1] + dvs[hd]) for hd in heads)

            zero = jnp.zeros((tk, LANES), F32)
            c = q_step(ks, tk, tuple((zero, zero) for _ in heads), True)
            if next_q is not None:
                c = lax.fori_loop(next_q, nq, lambda qi, c: q_step(pl.multiple_of(qi * TQ, TQ), TQ, c, False), c)
                if tail:
                    c = q_step(nq * TQ, tail, c, False)
            for hd in heads:
                dk_ref[pl.ds(ks, tk), hss[hd]] = c[hd][0] * SM_SCALE
            for pr in range(hb // 2):
                dv_ref[pl.ds(ks, tk), pr * LANES:(pr + 1) * LANES] = c[2 * pr][1] + c[2 * pr + 1][1]

        def whole_k_block(kj, carry):
            k_block(pl.multiple_of(kj * TK, TK), TK, kj + 1)
            return carry

        lax.fori_loop(0, nq, whole_k_block, 0)
        if tail:
            k_block(nq * TQ, tail, None)
        for hd in range(hb):
            dq_ref[:, hd * HEAD_SLOT:(hd + 1) * HEAD_SLOT] = dqt[hd].T * SM_SCALE

    blk = lambda w: pl.BlockSpec((lp, w), lambda b, g: (b, g))
    return _carrying_call(
        body, ride, (q, k, v, o, do, lse), name=name, grid=(nb, N_HEADS // hb),
        in_specs=[blk(hb * HEAD_SLOT), blk(hb * HEAD_SLOT), blk(hb * V_DIM), blk(hb * V_DIM), blk(hb * V_DIM),
                  pl.BlockSpec((hb, 1, lp), lambda b, g: (g, 0, b))],
        out_specs=[blk(hb * HEAD_SLOT), blk(hb * HEAD_SLOT), blk(hb * V_DIM)],
        out_shape=[jax.ShapeDtypeStruct((t, QK_WIDTH), F32), jax.ShapeDtypeStruct((t, QK_WIDTH), F32),
                   jax.ShapeDtypeStruct((t, D_MODEL), F32)],
        scratch_shapes=[pltpu.VMEM((hb, HEAD_SLOT, lp), BF16), pltpu.VMEM((hb, lp, LANES), BF16), pltpu.VMEM((hb, 1, lp), F32),
                        pltpu.VMEM((hb, 1, lp), F32), pltpu.VMEM((hb, HEAD_SLOT, lp), F32)])


def in_proj_bwd(dh1, h, g_mix, z, dq, dk, dv, dga, dgb, dpool, wint, gq, gkv, wuqt, wukt, wuvt, rope, *, tm, lp, nb, name):
    t = h.shape[0]
    hb = tm // POOL_HALO
    last_halo = t // POOL_HALO - 1

    def body(dh1_ref, h_ref, g_ref, zcq_ref, zckv_ref, dq_ref, dk_ref, dv_ref, dga_ref, dgb_ref, dpool_ref, dnext_ref,
             wint_ref, gq_ref, gkv_ref, wuqt_ref, wukt_ref, wuvt_ref, rope_ref,
             dh_ref, hn_ref, dz_ref, cqn_ref, ckvn_ref, dqb_ref, dkb_ref, dvb_ref, dg_ref, dgq_ref, dgkv_ref):
        i = pl.program_id(0)
        rope_t = rope_ref[...]
        dqb = _rope_bwd(dq_ref[...], *_rope_tables(rope_t, N_HEADS)).astype(BF16)
        dqb_ref[...] = dqb
        xq, rq = _rms(zcq_ref[...])
        gq_v = gq_ref[...]
        cqn_ref[...] = (xq * gq_v).astype(BF16)
        dcq, dgq = _rms_bwd(_dot(dqb, wuqt_ref[...]), xq, rq, gq_v)
        dk = dk_ref[...]
        dkb = dk.astype(BF16)
        dvb = dv_ref[...].astype(BF16)
        dkb_ref[...] = dkb
        dvb_ref[...] = dvb
        xkv, rkv = _rms(zckv_ref[...])
        gkv_v = gkv_ref[...]
        ckvn_ref[...] = (xkv * gkv_v).astype(BF16)
        dckv, dgkv = _rms_bwd(_dot(dkb, wukt_ref[...]) + _dot(dvb, wuvt_ref[...]), xkv, rkv, gkv_v)
        dks = dk[:, :HEAD_SLOT]
        for hd in range(1, N_HEADS):
            dks = dks + dk[:, hd * HEAD_SLOT:(hd + 1) * HEAD_SLOT]
        dzk = _rope_bwd(dks, *_rope_tables(rope_t, 1))
        dp_cur = dpool_ref[...]
        dp_ext = jnp.concatenate([dp_cur, dnext_ref[...]], axis=0)
        r = lax.broadcasted_iota(jnp.int32, (tm, tm + POOL_HALO), 0)
        e = lax.broadcasted_iota(jnp.int32, (tm, tm + POOL_HALO), 1)
        gt_col = i * tm + lax.broadcasted_iota(jnp.int32, (1, tm + POOL_HALO), 1)
        pos_col = _seq_pos(gt_col, lp, nb)
        gt_row = i * tm + lax.broadcasted_iota(jnp.int32, (tm + POOL_HALO, 1), 0)
        pos_row = _seq_pos(gt_row, lp, nb)
        dus = []
        for g, w in enumerate(POOL_WINDOWS):
            gs = slice(g * POOL_GROUP, (g + 1) * POOL_GROUP)
            band = jnp.where((e - r >= 0) & (e - r < jnp.minimum(pos_col + 1, w)) & (gt_col < t), 1.0, 0.0).astype(BF16)
            scaled = jnp.where(gt_row < t, dp_ext[:, gs] / jnp.minimum(pos_row + 1, w).astype(F32), 0.0).astype(BF16)
            dus.append(_dot(band, scaled) - dp_cur[:, gs])
        dz = jnp.concatenate(dus + [dcq, dckv, dzk], axis=1).astype(BF16)
        dz = jnp.concatenate([dz, dga_ref[...], dgb_ref[...]], axis=1)
        dz_ref[...] = dz
        xhat, rr = _rms(h_ref[...])
        gg = g_ref[...]
        hn_ref[...] = (xhat * gg).astype(BF16)
        dx, dg = _rms_bwd(_dot(dz, wint_ref[...]), xhat, rr, gg)
        dh_ref[...] = dh1_ref[...] + dx

        @pl.when(i == 0)
        def _():
            dg_ref[...] = jnp.zeros_like(dg_ref)
            dgq_ref[...] = jnp.zeros_like(dgq_ref)
            dgkv_ref[...] = jnp.zeros_like(dgkv_ref)

        dg_ref[...] += dg
        dgq_ref[...] += dgq
        dgkv_ref[...] += dgkv

    nxt = pl.BlockSpec((POOL_HALO, POOL_WIDTH), lambda i: (jnp.minimum((i + 1) * hb, last_halo), 0))
    return pl.pallas_call(
        body, name=name, grid=(t // tm,),
        in_specs=[_rows(tm, D_MODEL), _rows(tm, D_MODEL), _whole((1, D_MODEL)), _rows(tm, Q_RANK, Z_CQ // Q_RANK),
                  _rows(tm, KV_RANK, Z_CKV // KV_RANK), _rows(tm, QK_WIDTH), _rows(tm, QK_WIDTH), _rows(tm, D_MODEL),
                  _rows(tm, D_MODEL), _rows(tm, D_MODEL), _rows(tm, POOL_WIDTH), nxt,
                  _whole((DZ, D_MODEL)), _whole((1, Q_RANK)), _whole((1, KV_RANK)), _whole((QK_WIDTH, Q_RANK)),
                  _whole((QK_WIDTH, KV_RANK)), _whole((D_MODEL, KV_RANK)), _rows(tm, 4 * LANES)],
        out_specs=[_rows(tm, D_MODEL), _rows(tm, D_MODEL), _rows(tm, DZ), _rows(tm, Q_RANK), _rows(tm, KV_RANK),
                   _rows(tm, QK_WIDTH), _rows(tm, QK_WIDTH), _rows(tm, D_MODEL),
                   _acc((1, D_MODEL)), _acc((1, Q_RANK)), _acc((1, KV_RANK))],
        out_shape=[jax.ShapeDtypeStruct((t, D_MODEL), F32), jax.ShapeDtypeStruct((t, D_MODEL), BF16),
                   jax.ShapeDtypeStruct((t, DZ), BF16), jax.ShapeDtypeStruct((t, Q_RANK), BF16),
                   jax.ShapeDtypeStruct((t, KV_RANK), BF16), jax.ShapeDtypeStruct((t, QK_WIDTH), BF16),
                   jax.ShapeDtypeStruct((t, QK_WIDTH), BF16), jax.ShapeDtypeStruct((t, D_MODEL), BF16),
                   jax.ShapeDtypeStruct((1, D_MODEL), F32), jax.ShapeDtypeStruct((1, Q_RANK), F32),
                   jax.ShapeDtypeStruct((1, KV_RANK), F32)],
        compiler_params=_cparams(),
    )(dh1, h, g_mix, z, z, dq, dk, dv, dga, dgb, dpool, dpool, wint, gq, gkv, wuqt, wukt, wuvt, rope)


_MESH = pl.DeviceIdType.MESH


def _place():
    x, y, c = lax.axis_index("x"), lax.axis_index("y"), lax.axis_index("c")
    return x, y, c, 4 * x + 2 * y + c


def _peer(x, y, c, k):
    px, py, pc = (1 - x) if k & 4 else x, (1 - y) if k & 2 else y, (1 - c) if k & 1 else c
    return (px, py, pc), 4 * px + 2 * py + pc


ALL_PEERS = tuple(range(1, N_DEV))
CHIP_PEERS = (2, 4, 6)
N_CHIPS = N_DEV // 2


def _sem_scratch(n, m):
    return [pltpu.SemaphoreType.DMA((n, m)), pltpu.SemaphoreType.DMA((n, m)), pltpu.SemaphoreType.DMA((n,))]


class Exchange:
    def __init__(self, arrays, out_shapes, sem_cols, plan, aliased=False):
        self.arrays, self.out_shapes, self.plan = list(arrays), list(out_shapes), plan
        self.scratch = _sem_scratch(len(self.arrays), sem_cols)
        self.aliased = aliased

    def split(self, refs):
        n = len(self.arrays)
        return refs[:n], refs[n:2 * n], refs[2 * n:]

    def start(self, srcs, dsts, sems):
        local, sends, _ = self.plan(srcs, dsts, *sems)
        for cp in local + sends:
            cp.start()

    def wait(self, srcs, dsts, sems):
        local, sends, recvs = self.plan(srcs, dsts, *sems)
        for cp in recvs:
            cp.wait_recv()
        for cp in sends:
            cp.wait_send()
        for cp in local:
            cp.wait()

    def aliases(self, first_in, first_out):
        return {first_in + j: first_out + j for j in range(len(self.arrays))} if self.aliased else {}

    def run(self, name):
        def body(*refs):
            srcs, dsts, sems = self.split(refs)
            self.start(srcs, dsts, sems)
            self.wait(srcs, dsts, sems)

        n = len(self.arrays)
        return pl.pallas_call(body, name=name, in_specs=[_ANY] * n, out_specs=[_ANY] * n, out_shape=self.out_shapes,
                              scratch_shapes=self.scratch, input_output_aliases=self.aliases(0, 0))(*self.arrays)


def exchange(arrays, scatter, peers, by_chip=False):
    slots = N_CHIPS if by_chip else N_DEV

    def plan(srcs, dsts, send_sems, recv_sems, local_sems):
        x, y, c, me = _place()
        mine = 2 * x + y if by_chip else me
        local = [pltpu.make_async_copy(src.at[mine] if scatter else src, dst.at[mine], local_sems.at[j])
                 for j, (src, dst) in enumerate(zip(srcs, dsts))]
        sends, recvs = [], []
        for t, k in enumerate(peers):
            peer, pidx = _peer(x, y, c, k)
            theirs = 2 * peer[0] + peer[1] if by_chip else pidx
            for j, (src, dst) in enumerate(zip(srcs, dsts)):
                part = src.at[theirs] if scatter else src
                sems = dict(send_sem=send_sems.at[j, t], recv_sem=recv_sems.at[j, t], device_id=peer, device_id_type=_MESH)
                sends.append(pltpu.make_async_remote_copy(src_ref=part, dst_ref=dst.at[mine], **sems))
                recvs.append(pltpu.make_async_remote_copy(src_ref=part, dst_ref=dst.at[theirs], **sems))
        return local, sends, recvs

    shapes = [jax.ShapeDtypeStruct(a.shape if scatter else (slots,) + a.shape, a.dtype) for a in arrays]
    return Exchange(arrays, shapes, len(peers), plan)


def second_hop(gathered):
    def plan(srcs, dsts, send_sems, recv_sems, local_sems):
        x, y, c, me = _place()
        sibling, _ = _peer(x, y, c, 1)
        sends, recvs = [], []
        for t, k in enumerate(CHIP_PEERS):
            _, landed = _peer(x, y, c, k)
            _, coming = _peer(x, y, c, k ^ 1)
            for j, buf in enumerate(dsts):
                sems = dict(send_sem=send_sems.at[j, t], recv_sem=recv_sems.at[j, t], device_id=sibling, device_id_type=_MESH)
                sends.append(pltpu.make_async_remote_copy(src_ref=buf.at[landed], dst_ref=buf.at[landed], **sems))
                recvs.append(pltpu.make_async_remote_copy(src_ref=buf.at[coming], dst_ref=buf.at[coming], **sems))
        return [], sends, recvs

    shapes = [jax.ShapeDtypeStruct(a.shape, a.dtype) for a in gathered]
    return Exchange(gathered, shapes, len(CHIP_PEERS), plan, aliased=True)


FIRST_HOP_PEERS = (1,) + CHIP_PEERS


def _gather_two_level(arrays, name):
    n = len(arrays)

    def body(*refs):
        srcs, dsts, (send_sems, recv_sems, local_sems) = refs[:n], refs[n:2 * n], refs[2 * n:]
        x, y, c, me = _place()
        sibling, sidx = _peer(x, y, c, 1)

        def copy(j, sem, block, to, src=None):
            rows = dsts[j].at[block]
            return pltpu.make_async_remote_copy(src_ref=rows if src is None else src, dst_ref=rows, send_sem=send_sems.at[j, sem],
                                                recv_sem=recv_sems.at[j, sem], device_id=to, device_id_type=_MESH)

        local = [pltpu.make_async_copy(srcs[j], dsts[j].at[me], local_sems.at[j]) for j in range(n)]
        for cp in local:
            cp.start()
        first = [copy(j, 1 + t, me, _peer(x, y, c, k)[0], src=srcs[j]) for t, k in enumerate(CHIP_PEERS) for j in range(n)]
        first += [copy(j, 0, me, sibling, src=srcs[j]) for j in range(n)]
        for cp in first:
            cp.start()
        passed = []
        for t, k in enumerate(CHIP_PEERS):
            peer, pidx = _peer(x, y, c, k)
            for j in range(n):
                copy(j, 1 + t, pidx, peer).wait_recv()
                passed.append(copy(j, 4 + t, pidx, sibling))
                passed[-1].start()
        for j in range(n):
            copy(j, 0, sidx, sibling).wait_recv()
        for t, k in enumerate(CHIP_PEERS):
            _, pidx = _peer(x, y, c, k ^ 1)
            for j in range(n):
                copy(j, 4 + t, pidx, sibling).wait_recv()
        for cp in first + passed:
            cp.wait_send()
        for cp in local:
            cp.wait()

    shapes = [jax.ShapeDtypeStruct((N_DEV,) + a.shape, a.dtype) for a in arrays]
    return pl.pallas_call(body, name=name, in_specs=[_ANY] * n, out_specs=[_ANY] * n, out_shape=shapes,
                          scratch_shapes=_sem_scratch(n, 1 + 2 * len(CHIP_PEERS)))(*arrays)


def _to_sibling(arrays, name):
    n = len(arrays)

    def body(*refs):
        srcs, dsts, (send_sems, recv_sems) = refs[:n], refs[n:2 * n], refs[2 * n:]
        x, y, c, _ = _place()
        sibling, _ = _peer(x, y, c, 1)
        copies = [pltpu.make_async_remote_copy(src_ref=srcs[j].at[1 - c], dst_ref=dsts[j], send_sem=send_sems.at[j],
                                               recv_sem=recv_sems.at[j], device_id=sibling, device_id_type=_MESH) for j in range(n)]
        for cp in copies:
            cp.start()
        for cp in copies:
            cp.wait()

    shapes = [jax.ShapeDtypeStruct(a.shape[1:], a.dtype) for a in arrays]
    return pl.pallas_call(body, name=name, in_specs=[_ANY] * n, out_specs=[_ANY] * n, out_shape=shapes,
                          scratch_shapes=[pltpu.SemaphoreType.DMA((n,)), pltpu.SemaphoreType.DMA((n,))])(*arrays)


def pair_add(own, theirs, core, *, name):
    _, ns, r, c = own.shape
    rb = _row_block(r, c)

    def body(core_ref, a_ref, b_ref, o_ref):
        o_ref[...] = (a_ref[...].astype(F32) + b_ref[...].astype(F32)).astype(o_ref.dtype)

    return pl.pallas_call(
        body, name=name,
        grid_spec=pltpu.PrefetchScalarGridSpec(
            num_scalar_prefetch=1, grid=(ns, r // rb),
            in_specs=[pl.BlockSpec((None, None, rb, c), lambda i, j, core_ref: (core_ref[0], i, j, 0)),
                      pl.BlockSpec((None, rb, c), lambda i, j, core_ref: (i, j, 0))],
            out_specs=pl.BlockSpec((None, rb, c), lambda i, j, core_ref: (i, j, 0))),
        out_shape=jax.ShapeDtypeStruct((ns, r, c), own.dtype), compiler_params=_cparams(),
    )(core, own, theirs)


ADAMW_BLOCK_BYTES = 1 << 20


def _row_block(r, c):
    for rb in range(r, 0, -1):
        if r % rb == 0 and (rb % 16 == 0 or rb == r) and rb * c * 4 <= ADAMW_BLOCK_BYTES:
            return rb
    return r


def adamw(w, m, v, parts, *, name):
    depth, r, c = w.shape
    n_parts = parts[0].shape[0]
    rb = _row_block(r, c)

    def body(w_ref, m_ref, v_ref, *refs):
        p_refs, (g_ref, d_ref, nm_ref, nv_ref) = refs[:depth], refs[depth:]

        def total(p_ref):
            g = p_ref[0].astype(F32)
            for j in range(1, n_parts):
                g = g + p_ref[j].astype(F32)
            return g

        g = total(p_refs[0])
        for l in range(1, depth):
            g = jnp.where(pl.program_id(0) == l, total(p_refs[l]), g)
        g_ref[...] = g
        m_new = ADAM_B1 * m_ref[...] + (1.0 - ADAM_B1) * g
        v_new = ADAM_B2 * v_ref[...] + (1.0 - ADAM_B2) * (g * g)
        m_hat = m_new / (1.0 - ADAM_B1 ** ADAM_STEP)
        v_hat = v_new / (1.0 - ADAM_B2 ** ADAM_STEP)
        d_ref[...] = -ADAM_LR * (m_hat / (jnp.sqrt(v_hat) + ADAM_EPS) + ADAM_WD * w_ref[...])
        nm_ref[...] = m_new
        nv_ref[...] = v_new

    wblk = pl.BlockSpec((None, rb, c), lambda l, i: (l, i, 0))
    pblk = pl.BlockSpec((n_parts, rb, c), lambda l, i: (0, i, 0))
    return pl.pallas_call(
        body, name=name, grid=(depth, r // rb),
        in_specs=[wblk, wblk, wblk] + [pblk] * depth, out_specs=[wblk] * 4,
        out_shape=[jax.ShapeDtypeStruct((depth, r, c), F32)] * 4, compiler_params=_cparams(),
    )(w, m, v, *parts)


BIG = (("w_in", 2), ("w_uq", 2), ("w_ukv", 2), ("w_pa", 2), ("w_pb", 1), ("w_o", 1), ("w_gate", 2), ("w_up", 2), ("w_down", 1))
SMALL = ("norm_mix_g", "pool_w", "pool_scale", "q_norm_g", "kv_norm_g", "norm_ffn_g", "final_norm_g")
WEIGHTS = ("meta_tokens", "norm_mix_g", "w_in", "pool_w", "pool_scale", "q_norm_g", "kv_norm_g", "w_uq", "w_ukv", "w_pa", "w_pb",
           "w_o", "norm_ffn_g", "w_gate", "w_up", "w_down", "final_norm_g")
HEAD_QK = QK_NOPE + QK_ROPE
KR_END = Z_KR + QK_ROPE


def _cat_cols(parts):
    return [jnp.concatenate(parts, axis=1)]


def _cat_rows(parts):
    return [jnp.concatenate(parts, axis=0)]


def _arr_w_in(parts):
    full = jnp.concatenate(parts, axis=1)
    zc = lambda n: jnp.zeros((full.shape[0], n), full.dtype)
    return [jnp.concatenate([full[:, :Z_KR], zc(QK_NOPE), full[:, Z_KR:KR_END], zc(LANES - HEAD_QK), full[:, KR_END:]], axis=1)]


def _arr_w_uq(parts):
    full = jnp.concatenate(parts, axis=1)
    z = jnp.zeros((full.shape[0], HEAD_SLOT - HEAD_QK), full.dtype)
    pieces = []
    for hd in range(N_HEADS):
        pieces += [full[:, hd * HEAD_QK:(hd + 1) * HEAD_QK], z]
    return [jnp.concatenate(pieces, axis=1)]


def _arr_w_ukv(parts):
    full = jnp.concatenate(parts, axis=1)
    z = jnp.zeros((full.shape[0], HEAD_SLOT - QK_NOPE), full.dtype)
    wide = QK_NOPE + V_DIM
    k, v = [], []
    for hd in range(N_HEADS):
        k += [full[:, hd * wide:hd * wide + QK_NOPE], z]
        v.append(full[:, hd * wide + QK_NOPE:(hd + 1) * wide])
    return [jnp.concatenate(k, axis=1), jnp.concatenate(v, axis=1)]


def arrange(g, fn, out_shapes, name):
    def body(g_ref, *o_refs):
        for o_ref, val in zip(o_refs, fn([g_ref[p] for p in range(N_DEV)])):
            o_ref[...] = val

    return pl.pallas_call(
        body, name=name, grid=(1,),
        in_specs=[pl.BlockSpec(g.shape, lambda i: (0, 0, 0))],
        out_specs=[pl.BlockSpec(s, lambda i: (0, 0)) for s in out_shapes],
        out_shape=[jax.ShapeDtypeStruct(s, g.dtype) for s in out_shapes], compiler_params=_cparams(),
    )(g)


def _arranged_ranges(lo, hi):
    out = []
    for a, b, shift in ((0, Z_KR, 0), (Z_KR, KR_END, QK_NOPE), (KR_END, D_IN, LANES - QK_ROPE)):
        s, e = max(lo, a), min(hi, b)
        if s < e:
            out.append((s + shift, e + shift))
    return out


def _chunks_w_in(acc):
    cs = D_IN // N_DEV
    return [jnp.concatenate([acc[:, a:b] for a, b in _arranged_ranges(p * cs, (p + 1) * cs)], axis=1) for p in range(N_DEV)]


def _chunks_w_uq(acc):
    per = N_HEADS // N_DEV
    return [jnp.concatenate([acc[:, hd * HEAD_SLOT:hd * HEAD_SLOT + HEAD_QK] for hd in range(p * per, (p + 1) * per)], axis=1)
            for p in range(N_DEV)]


def _chunks_w_ukv(acc_k, acc_v):
    per = N_HEADS // N_DEV
    out = []
    for p in range(N_DEV):
        pieces = []
        for hd in range(p * per, (p + 1) * per):
            pieces += [acc_k[:, hd * HEAD_SLOT:hd * HEAD_SLOT + QK_NOPE], acc_v[:, hd * V_DIM:(hd + 1) * V_DIM]]
        out.append(jnp.concatenate(pieces, axis=1))
    return out


def _chunks_cols(acc):
    cs = acc.shape[1] // N_DEV
    return [acc[:, p * cs:(p + 1) * cs] for p in range(N_DEV)]


def _chunks_rows(acc):
    rs = acc.shape[0] // N_DEV
    return [acc[p * rs:(p + 1) * rs, :] for p in range(N_DEV)]


def _pack(parts, row_multiple):
    flat = jnp.concatenate([p.reshape(-1) for p in parts])
    return jnp.pad(flat, (0, -flat.shape[0] % (row_multiple * LANES))).reshape(-1, LANES)


def _unpack(packed, shapes):
    flat, out, off = packed.reshape(-1), [], 0
    for s in shapes:
        n = 1
        for d in s:
            n *= d
        out.append(flat[off:off + n].reshape(s))
        off += n
    return out


def _rope_table(lp, nb):
    inv = 1.0 / (ROPE_THETA ** (jnp.arange(0, QK_ROPE, 2, dtype=F32) / QK_ROPE))
    ang = jnp.arange(lp, dtype=F32)[:, None] * inv[None, :]
    cos, sin = jnp.cos(ang), jnp.sin(ang)
    z = lambda n: jnp.zeros((lp, n), F32)
    tail = LANES - QK_NOPE - QK_ROPE
    c = jnp.concatenate([jnp.ones((lp, QK_NOPE), F32), cos, cos, z(tail)], axis=1)
    cr = jnp.concatenate([z(QK_NOPE), cos, cos, z(tail)], axis=1)
    s1 = jnp.concatenate([z(QK_NOPE), -sin, z(HALF_ROPE), z(tail)], axis=1)
    s2 = jnp.concatenate([z(QK_NOPE), z(HALF_ROPE), sin, z(tail)], axis=1)
    return jnp.tile(jnp.concatenate([c, cr, s1, s2], axis=1), (nb, 1))


MIX = ("w_in", "w_uq", "w_ukv", "w_pa", "w_pb", "w_o")
FFN = ("w_gate", "w_up", "w_down")
ARRANGERS = {
    "w_in": (_arr_w_in, (("win", (D_MODEL, DZ)),)), "w_uq": (_arr_w_uq, (("wuq", (Q_RANK, QK_WIDTH)),)),
    "w_ukv": (_arr_w_ukv, (("wuk", (KV_RANK, QK_WIDTH)), ("wuv", (KV_RANK, D_MODEL)))),
    "w_pa": (_cat_cols, (("wpa", (POOL_WIDTH, D_MODEL)),)), "w_pb": (_cat_rows, (("wpb", (D_MODEL, D_MODEL)),)),
    "w_o": (_cat_rows, (("wo", (D_MODEL, D_MODEL)),)), "w_gate": (_cat_cols, (("wg", (D_MODEL, D_FF)),)),
    "w_up": (_cat_cols, (("wu", (D_MODEL, D_FF)),)), "w_down": (_cat_rows, (("wd", (D_FF, D_MODEL)),)),
}


def _operands(gathered, names, l):
    p = {}
    for n in names:
        fn, outs = ARRANGERS[n]
        for (key, _), a in zip(outs, arrange(gathered[n], fn, [s for _, s in outs], f"arrange_{n}_{l}")):
            p[key], p[key + "t"] = a, a.T
    return p


def _small_operands(small, l):
    pw = small["pool_w"][l].astype(BF16)
    return dict(g_mix=small["norm_mix_g"][l][None], gq=small["q_norm_g"][l][None], gkv=small["kv_norm_g"][l][None],
                g_ffn=small["norm_ffn_g"][l][None], ps=small["pool_scale"][l][None], pw=pw, pwt=jnp.swapaxes(pw, 1, 2))


class MeshComm:
    def __init__(self, w, meta_tokens):
        self.src = lambda n, l: w[n][l].astype(BF16)
        self.meta_tokens = meta_tokens
        self.core = lax.axis_index("c").astype(jnp.int32).reshape(1)
        self.rides = {0: [(n, 0) for n in FFN] + [(n, 1) for n in MIX], 1: [(n, 1) for n in FFN]}

    def first_weights(self):
        got = _gather_two_level([self.src(n, 0) for n in MIX] + [self.meta_tokens], "gather_mix_0")
        return dict(zip(MIX, got)), jnp.moveaxis(got[-1], 0, 1).reshape(N_META, D_MODEL)

    def first_hop(self, l):
        return exchange([self.src(n, layer) for n, layer in self.rides[l]], False, FIRST_HOP_PEERS)

    def second_hop(self, l, landed):
        return second_hop(landed)

    def carried(self, l, full, names, layer):
        return {n: full[self.rides[l].index((n, layer))] for n in names}

    def pair_sums(self, own, names, tag):
        theirs = _to_sibling(own, f"pair_grads_{tag}")
        return [pair_add(a, b, self.core, name=f"pair_add_{n}_{tag}") for n, a, b in zip(names, own, theirs)]

    def scatter(self, sums):
        return exchange(sums, True, CHIP_PEERS, by_chip=True)

    def scatter_now(self, sums, name):
        return self.scatter(sums).run(name)


HEADS_FWD, HEADS_BWD = 8, 4
TILE_ROWS, TILE_ROWS_BWD = 512, 256


def _tile(t, target):
    n = max(1, -(-t // (target + target // 8)))
    while t % n or (t // n) % 16:
        n += 1
    return t // n


def _ffn_bwd_part(dh2, p, s, tag):
    d, ff = D_MODEL, D_FF // N_DEV
    t = dh2.shape[0]
    wg_ = lambda n, x, ys, fn, shape: wgrad(x, ys, fn, shape, tm=_tile(t, TILE_ROWS), name=f"wgrad_{n}_{tag}")
    dh1, hn2, act, dgt, dup, dg_ffn = ffn_bwd(dh2, s["h1"], p["g_ffn"], s["gt"], s["up"], p["wgt"], p["wut"], p["wdt"],
                                              tm=_tile(t, TILE_ROWS_BWD), name=f"ffn_bwd_{tag}")
    chunks = [wg_("gate", hn2, [dgt], _chunks_cols, (d, ff)), wg_("up", hn2, [dup], _chunks_cols, (d, ff)),
              wg_("down", act, [dh2], _chunks_rows, (ff, d))]
    return dh1, chunks, dict(norm_ffn_g=dg_ffn[0])


def _mix_bwd_part(dh1, p, s, rope, nb, lp, tag, ride):
    d = D_MODEL
    t = dh1.shape[0]
    wg_ = lambda n, x, ys, fn, shape: wgrad(x, ys, fn, shape, tm=_tile(t, TILE_ROWS), name=f"wgrad_{n}_{tag}")
    dga, dgb, dpa, dpb, do, dpool, dps, dpw = merge_bwd(dh1, s["z"], s["pa"], s["pb"], s["pooled"], p["pw"], p["pwt"], p["ps"],
                                                        p["wpat"], p["wpbt"], p["wot"], tm=_tile(t, TILE_ROWS),
                                                        name=f"merge_bwd_{tag}")
    c_o = wg_("o", s["mg"], [dh1], _chunks_rows, (d // N_DEV, d))
    c_pa = wg_("pa", s["a"], [dpa], _chunks_cols, (POOL_WIDTH, d // N_DEV))
    c_pb = wg_("pb", s["o"], [dpb], _chunks_rows, (d // N_DEV, d))
    (dq, dk, dv), brought = attn_bwd(s["q"], s["k"], s["v"], s["o"], do, s["lse"], nb=nb, lp=lp, hb=HEADS_BWD,
                                     name=f"attn_bwd_{tag}", ride=ride)
    dh, hn, dz, cqn, ckvn, dqb, dkb, dvb, dg_mix, dgq, dgkv = in_proj_bwd(
        dh1, s["h"], p["g_mix"], s["z"], dq, dk, dv, dga, dgb, dpool, p["wint"], p["gq"], p["gkv"], p["wuqt"], p["wukt"], p["wuvt"],
        rope, tm=_tile(t, TILE_ROWS_BWD), lp=lp, nb=nb, name=f"in_proj_bwd_{tag}")
    c_in = wg_("in", hn, [dz], _chunks_w_in, (d, D_IN // N_DEV))
    c_uq = wg_("uq", cqn, [dqb], _chunks_w_uq, (Q_RANK, N_HEADS * HEAD_QK // N_DEV))
    c_ukv = wg_("ukv", ckvn, [dkb, dvb], _chunks_w_ukv, (KV_RANK, N_HEADS * (QK_NOPE + V_DIM) // N_DEV))
    small = dict(pool_scale=dps[0], pool_w=dpw, norm_mix_g=dg_mix[0], q_norm_g=dgq[0], kv_norm_g=dgkv[0])
    return dh, [c_in, c_uq, c_ukv, c_pa, c_pb, c_o], small, brought


def train_step(x, loss_target, small, comm):
    nb, seq, d = x.shape
    lp = -(-(N_META + seq) // LANES) * LANES
    t = nb * lp
    assert nb <= 2 and DEPTH == 2
    tm = _tile(t, TILE_ROWS)
    rope = _rope_table(lp, nb)
    gathered, meta = comm.first_weights()
    pad = jnp.zeros((nb, lp - N_META - seq, d), F32)
    h = jnp.concatenate([jnp.broadcast_to(meta[None], (nb, N_META, d)), x, pad], axis=1).reshape(t, d)
    target = jnp.concatenate([jnp.zeros((nb, N_META, d), F32), loss_target, pad], axis=1).reshape(t, d)

    params, saved, full = [], [], {}
    for l in range(DEPTH):
        p = _small_operands(small, l)
        p.update(_operands(gathered if l == 0 else comm.carried(0, full[0], MIX, 1), MIX, l))
        z, q, k, v = in_proj_fwd(h, p["g_mix"], p["win"], p["gq"], p["gkv"], p["wuq"], p["wuk"], p["wuv"], rope, tm=tm,
                                 name=f"in_proj_fwd_{l}")
        (o, lse), landed = attn_fwd(q, k, v, nb=nb, lp=lp, hb=HEADS_FWD, name=f"attn_fwd_{l}", ride=comm.first_hop(l))
        (h1, pooled, a, pa, pb, mg), full[l] = merge_fwd(h, z, o, p["pw"], p["ps"], p["wpa"], p["wpb"], p["wo"], tm=tm, lp=lp,
                                                          nb=nb, name=f"merge_fwd_{l}", ride=comm.second_hop(l, landed))
        p.update(_operands(comm.carried(l, full[l], FFN, l), FFN, l))
        h2, gt, up = ffn_fwd(h1, p["g_ffn"], p["wg"], p["wu"], p["wd"], tm=tm, name=f"ffn_fwd_{l}")
        params.append(p)
        saved.append(dict(h=h, z=z, q=q, k=k, v=v, o=o, lse=lse, h1=h1, pooled=pooled, a=a, pa=pa, pb=pb, mg=mg, gt=gt, up=up))
        h = h2
    parts, dh, dgf = loss_head(h, small["final_norm_g"][None], target, tm=tm, lp=lp, nb=nb, seq=seq, name="loss_head")
    loss = jnp.sum(parts[::8, 0])

    sums = {}
    dh, c_ffn1, small1 = _ffn_bwd_part(dh, params[1], saved[1], 1)
    s_ffn1 = comm.pair_sums(c_ffn1, FFN, "ffn_1")
    dh, c_mix1, sm, brought = _mix_bwd_part(dh, params[1], saved[1], rope, nb, lp, 1, comm.scatter(s_ffn1))
    small1.update(sm)
    sums.update({(n, 1): a for n, a in zip(FFN, brought)})
    s_mix1 = comm.pair_sums(c_mix1, MIX, "mix_1")
    dh, c_ffn0, small0 = _ffn_bwd_part(dh, params[0], saved[0], 0)
    s_ffn0 = comm.pair_sums(c_ffn0, FFN, "ffn_0")
    dh, c_mix0, sm, brought = _mix_bwd_part(dh, params[0], saved[0], rope, nb, lp, 0, comm.scatter(s_mix1 + s_ffn0))
    small0.update(sm)
    sums.update({(n, l): a for (n, l), a in zip([(n, 1) for n in MIX] + [(n, 0) for n in FFN], brought)})
    dh = dh.reshape(nb, lp, d)
    dmeta = jnp.sum(dh[:, :N_META], axis=0)
    meta_chunks = jnp.transpose(dmeta.reshape(N_META, N_CHIPS, 2, d // N_DEV), (2, 1, 0, 3)).astype(BF16)
    s_last = comm.pair_sums(c_mix0 + [meta_chunks], MIX + ("meta_tokens",), "mix_0")
    last = comm.scatter_now(s_last, "scatter_mix_0")
    sums.update({(n, 0): a for n, a in zip(MIX + ("meta_tokens",), last)})
    small_grads = {n: jnp.stack([small0[n], small1[n]]) for n in small0}
    small_grads["final_norm_g"] = dgf[0]
    return loss, dh[:, N_META:N_META + seq], sums, small_grads


def kernel(x, meta_tokens, norm_mix_g, w_in, pool_w, pool_scale, q_norm_g, kv_norm_g, w_uq, w_ukv, w_pa, w_pb, w_o, norm_ffn_g, w_gate, w_up, w_down, final_norm_g, loss_target, m_meta_tokens, m_norm_mix_g, m_w_in, m_pool_w, m_pool_scale, m_q_norm_g, m_kv_norm_g, m_w_uq, m_w_ukv, m_w_pa, m_w_pb, m_w_o, m_norm_ffn_g, m_w_gate, m_w_up, m_w_down, m_final_norm_g, v_meta_tokens, v_norm_mix_g, v_w_in, v_pool_w, v_pool_scale, v_q_norm_g, v_kv_norm_g, v_w_uq, v_w_ukv, v_w_pa, v_w_pb, v_w_o, v_norm_ffn_g, v_w_gate, v_w_up, v_w_down, v_final_norm_g):
    args = dict(locals())
    w = {n: args[n] for n in WEIGHTS}
    m = {n: args["m_" + n] for n in WEIGHTS}
    v = {n: args["v_" + n] for n in WEIGHTS}
    small = {n: w[n] for n in SMALL}

    loss, grad_x, sums, small_grads = train_step(x, loss_target, small, MeshComm(w, meta_tokens))
    loss = lax.psum(loss, ("x", "y", "c"))
    (small_recv,) = exchange([_pack([small_grads[n] for n in SMALL], 8)], False, ALL_PEERS).run("gather_small_grads")

    out = {n: adamw(w[n], m[n], v[n], [sums[(n, l)] for l in range(DEPTH)], name=f"adamw_{n}") for n, _ in BIG}
    out["meta_tokens"] = [a[0] for a in adamw(meta_tokens[None], m["meta_tokens"][None], v["meta_tokens"][None],
                                              [sums[("meta_tokens", 0)]], name="adamw_meta_tokens")]
    pk = lambda d: _pack([d[n] for n in SMALL], 8)[None]
    packed = adamw(pk(w), pk(m), pk(v), [small_recv], name="adamw_small")
    shapes = [w[n].shape for n in SMALL]
    for n, *kinds in zip(SMALL, *[_unpack(packed[kind][0], shapes) for kind in range(4)]):
        out[n] = kinds
    return (loss, grad_x, *[out[n][kind] for kind in range(4) for n in WEIGHTS])
```

```python
import functools
import math

import jax
import jax.numpy as jnp
from jax import lax
from jax.experimental import pallas as pl
from jax.experimental.pallas import tpu as pltpu

F32, BF16 = jnp.float32, jnp.bfloat16

D_MODEL = 1024
N_META = 16
N_HEADS = 16
QK_NOPE, QK_ROPE, V_DIM = 64, 32, 64
HALF_ROPE = QK_ROPE // 2
Q_RANK, KV_RANK = 256, 128
POOL_WINDOWS = (2, 4, 8, 16)
POOL_GROUP = 128
POOL_WIDTH = POOL_GROUP * len(POOL_WINDOWS)
POOL_HALO = 16
D_FF = 2816
D_IN = 2976
NORM_EPS = 1e-6
SM_SCALE = (QK_NOPE + QK_ROPE) ** -0.5
LOG2E = math.log2(math.e)
EXP2_SCALE = SM_SCALE * LOG2E
MASK_VALUE = -1e30
ROPE_THETA = 10000.0
DEPTH = 2
N_DEV = 8

ADAM_LR, ADAM_B1, ADAM_B2, ADAM_EPS, ADAM_WD, ADAM_STEP = 0.001, 0.9, 0.999, 1e-08, 0.01, 10

LANES = 128
HEAD_SLOT = LANES
QK_WIDTH = N_HEADS * HEAD_SLOT
Z_CQ, Z_CKV, Z_KR, Z_GA, Z_GB, DZ = 512, 768, 896, 1024, 2048, 3072
TQ = TK = 256
VMEM_LIMIT = 56 * 1024 * 1024


def _cparams():
    return pltpu.CompilerParams(vmem_limit_bytes=VMEM_LIMIT)


def _rows(tm, width, col=0):
    return pl.BlockSpec((tm, width), lambda i: (i, col))


def _whole(shape):
    zeros = (0,) * len(shape)
    return pl.BlockSpec(shape, lambda i: zeros, pipeline_mode=pl.Buffered(1))


def _acc(shape):
    zeros = (0,) * len(shape)
    return pl.BlockSpec(shape, lambda i: zeros)


def _dot(a, b):
    return jnp.dot(a, b, preferred_element_type=F32)


def _dot_tn(a, b):
    return lax.dot_general(a, b, (((0,), (0,)), ((), ())), preferred_element_type=F32)


def _dot_nt(a, b):
    return lax.dot_general(a, b, (((1,), (1,)), ((), ())), preferred_element_type=F32)


def _rms(x):
    r = lax.rsqrt(jnp.mean(x * x, axis=-1, keepdims=True) + NORM_EPS)
    return x * r, r


def _rms_bwd(dy, xhat, r, g):
    dg = jnp.sum(dy * xhat, axis=0, keepdims=True)
    dxh = dy * g
    dx = r * (dxh - xhat * jnp.mean(dxh * xhat, axis=-1, keepdims=True))
    return dx, dg


def _sigmoid(x):
    return 1.0 / (1.0 + jnp.exp(-x))


def _rope_fwd(q, c, s1, s2):
    w = q.shape[1]
    return q * c + pltpu.roll(q, w - HALF_ROPE, 1) * s1 + pltpu.roll(q, HALF_ROPE, 1) * s2


def _rope_bwd(dq, c, s1, s2):
    w = dq.shape[1]
    return dq * c + pltpu.roll(dq * s1, HALF_ROPE, 1) + pltpu.roll(dq * s2, w - HALF_ROPE, 1)


def _rope_tables(rope, reps):
    c, cr, s1, s2 = (rope[:, k * LANES:(k + 1) * LANES] for k in range(4))
    if reps > 1:
        return jnp.tile(c, (1, reps)), jnp.tile(s1, (1, reps)), jnp.tile(s2, (1, reps))
    return cr, s1, s2


def _seq_pos(gi, lp, nb):
    pos = gi
    for b in range(1, nb):
        pos = jnp.where(gi >= b * lp, gi - b * lp, pos)
    return pos


_ANY = pl.BlockSpec(memory_space=pl.ANY)


def _carrying_call(body, ride, operands, *, name, grid, in_specs, out_specs, out_shape, scratch_shapes=()):
    n_in, n_out = len(in_specs), len(out_specs)
    if ride is None:
        out = pl.pallas_call(body, name=name, grid=grid, in_specs=in_specs, out_specs=out_specs, out_shape=out_shape,
                             scratch_shapes=list(scratch_shapes), compiler_params=_cparams())(*operands)
        return out, []
    ne = len(ride.arrays)

    def carrying(*refs):
        ins, r_in, rest = refs[:n_in], refs[n_in:n_in + ne], refs[n_in + ne:]
        outs, r_out, rest = rest[:n_out], rest[n_out:n_out + ne], rest[n_out + ne:]
        scratch, sems = rest[:len(scratch_shapes)], rest[len(scratch_shapes):]
        ids = [pl.program_id(a) for a in range(len(grid))]
        first = functools.reduce(jnp.logical_and, [i == 0 for i in ids])
        last = functools.reduce(jnp.logical_and, [i == g - 1 for i, g in zip(ids, grid)])

        @pl.when(first)
        def _():
            ride.start(r_in, r_out, sems)

        body(*ins, *outs, *scratch)

        @pl.when(last)
        def _():
            ride.wait(r_in, r_out, sems)

    out = pl.pallas_call(
        carrying, name=name, grid=grid, in_specs=list(in_specs) + [_ANY] * ne, out_specs=list(out_specs) + [_ANY] * ne,
        out_shape=list(out_shape) + ride.out_shapes, scratch_shapes=list(scratch_shapes) + ride.scratch,
        input_output_aliases=ride.aliases(n_in, n_out), compiler_params=_cparams(),
    )(*operands, *ride.arrays)
    return out[:n_out], out[n_out:]


def in_proj_fwd(h, g_mix, win, gq, gkv, wuq, wuk, wuv, rope, *, tm, name):
    t = h.shape[0]

    def body(h_ref, g_ref, win_ref, gq_ref, gkv_ref, wuq_ref, wuk_ref, wuv_ref, rope_ref, z_ref, q_ref, k_ref, v_ref):
        xhat, _ = _rms(h_ref[...])
        hn = (xhat * g_ref[...]).astype(BF16)
        z = _dot(hn, win_ref[...])
        z_ref[...] = z
        rope_t = rope_ref[...]
        xq, _ = _rms(z[:, Z_CQ:Z_CKV])
        cqn = (xq * gq_ref[...]).astype(BF16)
        q = _rope_fwd(_dot(cqn, wuq_ref[...]), *_rope_tables(rope_t, N_HEADS))
        q_ref[...] = q.astype(BF16)
        xkv, _ = _rms(z[:, Z_CKV:Z_KR])
        ckvn = (xkv * gkv_ref[...]).astype(BF16)
        kr = _rope_fwd(z[:, Z_KR:Z_GA], *_rope_tables(rope_t, 1))
        k_ref[...] = (_dot(ckvn, wuk_ref[...]) + jnp.tile(kr, (1, N_HEADS))).astype(BF16)
        v_ref[...] = _dot(ckvn, wuv_ref[...]).astype(BF16)

    return pl.pallas_call(
        body, name=name, grid=(t // tm,),
        in_specs=[_rows(tm, D_MODEL), _whole((1, D_MODEL)), _whole((D_MODEL, DZ)), _whole((1, Q_RANK)), _whole((1, KV_RANK)),
                  _whole((Q_RANK, QK_WIDTH)), _whole((KV_RANK, QK_WIDTH)), _whole((KV_RANK, D_MODEL)), _rows(tm, 4 * LANES)],
        out_specs=[_rows(tm, DZ), _rows(tm, QK_WIDTH), _rows(tm, QK_WIDTH), _rows(tm, D_MODEL)],
        out_shape=[jax.ShapeDtypeStruct((t, DZ), F32), jax.ShapeDtypeStruct((t, QK_WIDTH), BF16),
                   jax.ShapeDtypeStruct((t, QK_WIDTH), BF16), jax.ShapeDtypeStruct((t, D_MODEL), BF16)],
        compiler_params=_cparams(),
    )(h, g_mix, win, gq, gkv, wuq, wuk, wuv, rope)


def attn_fwd(q, k, v, *, nb, lp, hb, name, ride=None):
    t = q.shape[0]
    nq, tail = lp // TQ, lp % TQ
    assert tail % LANES == 0

    def body(q_ref, k_ref, v_ref, o_ref, lse_ref, vt):
        for pr in range(hb // 2):
            vt[pr] = v_ref[:, pr * LANES:(pr + 1) * LANES].T

        def q_block(qs, tq, whole_k):
            qh = [q_ref[pl.ds(qs, tq), hd * HEAD_SLOT:(hd + 1) * HEAD_SLOT] for hd in range(hb)]
            keep = lax.broadcasted_iota(jnp.int32, (tq, tq), 0) <= lax.broadcasted_iota(jnp.int32, (tq, tq), 1)

            def k_step(ks, tk, c, masked):
                sts = [_dot_nt(k_ref[pl.ds(ks, tk), hd * HEAD_SLOT:(hd + 1) * HEAD_SLOT], qh[hd]) for hd in range(hb)]
                ps, stats = [], []
                for hd in range(hb):
                    m, l, _ = c[hd]
                    st = jnp.where(keep, sts[hd], MASK_VALUE) if masked else sts[hd]
                    m_new = jnp.maximum(m, jnp.max(st, axis=0, keepdims=True))
                    p = jnp.exp2((st - m_new) * EXP2_SCALE)
                    alpha = jnp.exp2((m - m_new) * EXP2_SCALE)
                    ps.append(p.astype(BF16))
                    stats.append((m_new, alpha * l + jnp.sum(p, axis=0, keepdims=True), alpha))
                pvs = [_dot(vt[hd // 2, :, pl.ds(ks, tk)], ps[hd]) for hd in range(hb)]
                return tuple((stats[hd][0], stats[hd][1], stats[hd][2] * c[hd][2] + pvs[hd]) for hd in range(hb))

            init = tuple((jnp.full((1, tq), MASK_VALUE, F32), jnp.zeros((1, tq), F32), jnp.zeros((LANES, tq), F32))
                         for _ in range(hb))
            c = lax.fori_loop(0, whole_k, lambda kj, c: k_step(pl.multiple_of(kj * TK, TK), TK, c, False), init)
            c = k_step(qs, tq, c, True)
            sub = lax.broadcasted_iota(jnp.int32, (LANES, tq), 0)
            for pr in range(hb // 2):
                (m0, l0, a0), (m1, l1, a1) = c[2 * pr], c[2 * pr + 1]
                o_ref[pl.ds(qs, tq), pr * LANES:(pr + 1) * LANES] = jnp.where(sub < V_DIM, a0 / l0, a1 / l1).T.astype(BF16)
                lse_ref[2 * pr, :, pl.ds(qs, tq)] = m0 * SM_SCALE + jnp.log(l0)
                lse_ref[2 * pr + 1, :, pl.ds(qs, tq)] = m1 * SM_SCALE + jnp.log(l1)

        def whole_q_block(qi, carry):
            q_block(pl.multiple_of(qi * TQ, TQ), TQ, qi)
            return carry

        lax.fori_loop(0, nq, whole_q_block, 0)
        if tail:
            q_block(nq * TQ, tail, nq)

    blk = lambda w: pl.BlockSpec((lp, w), lambda b, g: (b, g))
    return _carrying_call(
        body, ride, (q, k, v), name=name, grid=(nb, N_HEADS // hb),
        in_specs=[blk(hb * HEAD_SLOT), blk(hb * HEAD_SLOT), blk(hb * V_DIM)],
        out_specs=[blk(hb * V_DIM), pl.BlockSpec((hb, 1, lp), lambda b, g: (g, 0, b))],
        out_shape=[jax.ShapeDtypeStruct((t, D_MODEL), BF16), jax.ShapeDtypeStruct((N_HEADS, 1, t), F32)],
        scratch_shapes=[pltpu.VMEM((hb // 2, LANES, lp), BF16)])


def _pool_band_fwd(i, tm, lp, nb):
    r = lax.broadcasted_iota(jnp.int32, (tm, POOL_HALO + tm), 0)
    e = lax.broadcasted_iota(jnp.int32, (tm, POOL_HALO + tm), 1)
    diff = r + POOL_HALO - e
    pos = _seq_pos(i * tm + lax.broadcasted_iota(jnp.int32, (tm, 1), 0), lp, nb)
    out = []
    for w in POOL_WINDOWS:
        cnt = jnp.minimum(pos + 1, w)
        band = jnp.where((diff >= 0) & (diff < cnt), 1.0, 0.0).astype(BF16)
        out.append((band, cnt.astype(F32)))
    return out


def merge_fwd(h, z, o, pw, ps, wpa, wpb, wo, *, tm, lp, nb, name, ride=None):
    t = h.shape[0]
    hb = tm // POOL_HALO

    def body(h_ref, u_ref, uprev_ref, ga_ref, gb_ref, o_ref, pw_ref, ps_ref, wpa_ref, wpb_ref, wo_ref,
             h1_ref, pooled_ref, a_ref, pa_ref, pb_ref, mg_ref):
        i = pl.program_id(0)
        u = u_ref[...]
        uext = jnp.concatenate([uprev_ref[...], u], axis=0).astype(BF16)
        pooled, ys = [], []
        for g, (band, cnt) in enumerate(_pool_band_fwd(i, tm, lp, nb)):
            gs = slice(g * POOL_GROUP, (g + 1) * POOL_GROUP)
            pg = (_dot(band, uext[:, gs]) / cnt - u[:, gs]).astype(BF16)
            pooled.append(pg)
            ys.append(_dot(pg, pw_ref[g]))
        pooled_ref[...] = jnp.concatenate(pooled, axis=1)
        a = (jnp.concatenate(ys, axis=1) * ps_ref[...]).astype(BF16)
        a_ref[...] = a
        pa = _dot(a, wpa_ref[...])
        pb = _dot(o_ref[...], wpb_ref[...])
        pa_ref[...] = pa.astype(BF16)
        pb_ref[...] = pb.astype(BF16)
        mg = (_sigmoid(ga_ref[...]) * pa + _sigmoid(gb_ref[...]) * pb).astype(BF16)
        mg_ref[...] = mg
        h1_ref[...] = h_ref[...] + _dot(mg, wo_ref[...])

    halo = pl.BlockSpec((POOL_HALO, POOL_WIDTH), lambda i: (jnp.maximum(i * hb - 1, 0), 0))
    return _carrying_call(
        body, ride, (h, z, z, z, z, o, pw, ps, wpa, wpb, wo), name=name, grid=(t // tm,),
        in_specs=[_rows(tm, D_MODEL), _rows(tm, POOL_WIDTH), halo, _rows(tm, D_MODEL, 1), _rows(tm, D_MODEL, 2), _rows(tm, D_MODEL),
                  _whole((4, POOL_GROUP, POOL_GROUP)), _whole((1, POOL_WIDTH)), _whole((POOL_WIDTH, D_MODEL)),
                  _whole((D_MODEL, D_MODEL)), _whole((D_MODEL, D_MODEL))],
        out_specs=[_rows(tm, D_MODEL), _rows(tm, POOL_WIDTH), _rows(tm, POOL_WIDTH), _rows(tm, D_MODEL), _rows(tm, D_MODEL),
                   _rows(tm, D_MODEL)],
        out_shape=[jax.ShapeDtypeStruct((t, D_MODEL), F32), jax.ShapeDtypeStruct((t, POOL_WIDTH), BF16),
                   jax.ShapeDtypeStruct((t, POOL_WIDTH), BF16), jax.ShapeDtypeStruct((t, D_MODEL), BF16),
                   jax.ShapeDtypeStruct((t, D_MODEL), BF16), jax.ShapeDtypeStruct((t, D_MODEL), BF16)])


def ffn_fwd(h1, g, wg, wu, wd, *, tm, name):
    t = h1.shape[0]

    def body(h_ref, g_ref, wg_ref, wu_ref, wd_ref, h2_ref, gt_ref, up_ref):
        h = h_ref[...]
        xhat, _ = _rms(h)
        hn = (xhat * g_ref[...]).astype(BF16)
        gt = _dot(hn, wg_ref[...])
        up = _dot(hn, wu_ref[...])
        gt_ref[...] = gt.astype(BF16)
        up_ref[...] = up.astype(BF16)
        act = (gt * _sigmoid(gt) * up).astype(BF16)
        h2_ref[...] = h + _dot(act, wd_ref[...])

    return pl.pallas_call(
        body, name=name, grid=(t // tm,),
        in_specs=[_rows(tm, D_MODEL), _whole((1, D_MODEL)), _whole((D_MODEL, D_FF)), _whole((D_MODEL, D_FF)), _whole((D_FF, D_MODEL))],
        out_specs=[_rows(tm, D_MODEL), _rows(tm, D_FF), _rows(tm, D_FF)],
        out_shape=[jax.ShapeDtypeStruct((t, D_MODEL), F32), jax.ShapeDtypeStruct((t, D_FF), BF16), jax.ShapeDtypeStruct((t, D_FF), BF16)],
        compiler_params=_cparams(),
    )(h1, g, wg, wu, wd)


def loss_head(h, g, target, *, tm, lp, nb, seq, name):
    t = h.shape[0]
    nt = t // tm

    def body(h_ref, g_ref, t_ref, loss_ref, dh_ref, dg_ref):
        i = pl.program_id(0)
        pos = _seq_pos(i * tm + lax.broadcasted_iota(jnp.int32, (tm, 1), 0), lp, nb)
        real = (pos >= N_META) & (pos < N_META + seq)
        xhat, r = _rms(h_ref[...])
        gg = g_ref[...]
        err = jnp.where(real, xhat * gg - t_ref[...], 0.0)
        loss_ref[...] = jnp.full((8, LANES), 0.5 * jnp.sum(err * err) / D_MODEL, F32)
        dx, dg = _rms_bwd(err * (1.0 / D_MODEL), xhat, r, gg)
        dh_ref[...] = dx

        @pl.when(i == 0)
        def _():
            dg_ref[...] = jnp.zeros_like(dg_ref)

        dg_ref[...] += dg

    return pl.pallas_call(
        body, name=name, grid=(nt,),
        in_specs=[_rows(tm, D_MODEL), _whole((1, D_MODEL)), _rows(tm, D_MODEL)],
        out_specs=[pl.BlockSpec((8, LANES), lambda i: (i, 0)), _rows(tm, D_MODEL), _acc((1, D_MODEL))],
        out_shape=[jax.ShapeDtypeStruct((nt * 8, LANES), F32), jax.ShapeDtypeStruct((t, D_MODEL), F32),
                   jax.ShapeDtypeStruct((1, D_MODEL), F32)],
        compiler_params=_cparams(),
    )(h, g, target)


def wgrad(x, ys, chunk_fn, chunk_shape, *, tm, name):
    t, m = x.shape

    def body(x_ref, *refs):
        y_refs, o_ref, accs = refs[:len(ys)], refs[len(ys)], refs[len(ys) + 1:]
        i = pl.program_id(0)

        @pl.when(i == 0)
        def _():
            for acc in accs:
                acc[...] = jnp.zeros_like(acc)

        xb = x_ref[...].astype(BF16)
        for y_ref, acc in zip(y_refs, accs):
            acc[...] += _dot_tn(xb, y_ref[...].astype(BF16))

        @pl.when(i == t // tm - 1)
        def _():
            for p, chunk in enumerate(chunk_fn(*accs)):
                o_ref[p % 2, p // 2] = chunk.astype(BF16)

    out = (2, N_DEV // 2) + tuple(chunk_shape)
    return pl.pallas_call(
        body, name=name, grid=(t // tm,),
        in_specs=[_rows(tm, m)] + [_rows(tm, y.shape[1]) for y in ys], out_specs=_acc(out),
        out_shape=jax.ShapeDtypeStruct(out, BF16), scratch_shapes=[pltpu.VMEM((m, y.shape[1]), F32) for y in ys],
        compiler_params=_cparams(),
    )(x, *ys)


def ffn_bwd(dh2, h1, g, gt, up, wgt, wut, wdt, *, tm, name):
    t = h1.shape[0]

    def body(dh2_ref, h_ref, g_ref, gt_ref, up_ref, wgt_ref, wut_ref, wdt_ref, dh1_ref, hn_ref, act_ref, dgt_ref, dup_ref, dg_ref):
        dh2 = dh2_ref[...]
        dact = _dot(dh2.astype(BF16), wdt_ref[...])
        gt = gt_ref[...].astype(F32)
        up = up_ref[...].astype(F32)
        sg = _sigmoid(gt)
        silu = gt * sg
        act_ref[...] = (silu * up).astype(BF16)
        dgt = (dact * up * (sg * (1.0 + gt * (1.0 - sg)))).astype(BF16)
        dup = (dact * silu).astype(BF16)
        dgt_ref[...] = dgt
        dup_ref[...] = dup
        dhn = _dot(dgt, wgt_ref[...]) + _dot(dup, wut_ref[...])
        xhat, r = _rms(h_ref[...])
        gg = g_ref[...]
        hn_ref[...] = (xhat * gg).astype(BF16)
        dx, dg = _rms_bwd(dhn, xhat, r, gg)
        dh1_ref[...] = dh2 + dx

        @pl.when(pl.program_id(0) == 0)
        def _():
            dg_ref[...] = jnp.zeros_like(dg_ref)

        dg_ref[...] += dg

    return pl.pallas_call(
        body, name=name, grid=(t // tm,),
        in_specs=[_rows(tm, D_MODEL), _rows(tm, D_MODEL), _whole((1, D_MODEL)), _rows(tm, D_FF), _rows(tm, D_FF),
                  _whole((D_FF, D_MODEL)), _whole((D_FF, D_MODEL)), _whole((D_MODEL, D_FF))],
        out_specs=[_rows(tm, D_MODEL), _rows(tm, D_MODEL), _rows(tm, D_FF), _rows(tm, D_FF), _rows(tm, D_FF), _acc((1, D_MODEL))],
        out_shape=[jax.ShapeDtypeStruct((t, D_MODEL), F32), jax.ShapeDtypeStruct((t, D_MODEL), BF16),
                   jax.ShapeDtypeStruct((t, D_FF), BF16), jax.ShapeDtypeStruct((t, D_FF), BF16),
                   jax.ShapeDtypeStruct((t, D_FF), BF16), jax.ShapeDtypeStruct((1, D_MODEL), F32)],
        compiler_params=_cparams(),
    )(dh2, h1, g, gt, up, wgt, wut, wdt)


def merge_bwd(dh1, z, pa, pb, pooled, pw, pwt, ps, wpat, wpbt, wot, *, tm, name):
    t = dh1.shape[0]

    def body(dh1_ref, ga_ref, gb_ref, pa_ref, pb_ref, pooled_ref, pw_ref, pwt_ref, ps_ref, wpat_ref, wpbt_ref, wot_ref,
             dga_ref, dgb_ref, dpa_ref, dpb_ref, do_ref, dpool_ref, dps_ref, dpw_ref):
        dmg = _dot(dh1_ref[...].astype(BF16), wot_ref[...])
        sa = _sigmoid(ga_ref[...])
        sb = _sigmoid(gb_ref[...])
        dga_ref[...] = (dmg * pa_ref[...].astype(F32) * sa * (1.0 - sa)).astype(BF16)
        dgb_ref[...] = (dmg * pb_ref[...].astype(F32) * sb * (1.0 - sb)).astype(BF16)
        dpa = (dmg * sa).astype(BF16)
        dpb = (dmg * sb).astype(BF16)
        dpa_ref[...] = dpa
        dpb_ref[...] = dpb
        do_ref[...] = _dot(dpb, wpbt_ref[...]).astype(BF16)
        da = _dot(dpa, wpat_ref[...])
        pooled = pooled_ref[...]
        ps = ps_ref[...]

        @pl.when(pl.program_id(0) == 0)
        def _():
            dps_ref[...] = jnp.zeros_like(dps_ref)
            dpw_ref[...] = jnp.zeros_like(dpw_ref)

        dps, dpool = [], []
        for g in range(len(POOL_WINDOWS)):
            gs = slice(g * POOL_GROUP, (g + 1) * POOL_GROUP)
            y = _dot(pooled[:, gs], pw_ref[g])
            dps.append(jnp.sum(da[:, gs] * y, axis=0, keepdims=True))
            dy = (da[:, gs] * ps[:, gs]).astype(BF16)
            dpool.append(_dot(dy, pwt_ref[g]))
            dpw_ref[g] += _dot_tn(pooled[:, gs], dy)
        dps_ref[...] += jnp.concatenate(dps, axis=1)
        dpool_ref[...] = jnp.concatenate(dpool, axis=1)

    return pl.pallas_call(
        body, name=name, grid=(t // tm,),
        in_specs=[_rows(tm, D_MODEL), _rows(tm, D_MODEL, 1), _rows(tm, D_MODEL, 2), _rows(tm, D_MODEL), _rows(tm, D_MODEL),
                  _rows(tm, POOL_WIDTH), _whole((4, POOL_GROUP, POOL_GROUP)), _whole((4, POOL_GROUP, POOL_GROUP)),
                  _whole((1, POOL_WIDTH)), _whole((D_MODEL, POOL_WIDTH)), _whole((D_MODEL, D_MODEL)), _whole((D_MODEL, D_MODEL))],
        out_specs=[_rows(tm, D_MODEL), _rows(tm, D_MODEL), _rows(tm, D_MODEL), _rows(tm, D_MODEL), _rows(tm, D_MODEL),
                   _rows(tm, POOL_WIDTH), _acc((1, POOL_WIDTH)), _acc((4, POOL_GROUP, POOL_GROUP))],
        out_shape=[jax.ShapeDtypeStruct((t, D_MODEL), BF16)] * 5
        + [jax.ShapeDtypeStruct((t, POOL_WIDTH), F32), jax.ShapeDtypeStruct((1, POOL_WIDTH), F32),
           jax.ShapeDtypeStruct((4, POOL_GROUP, POOL_GROUP), F32)],
        compiler_params=_cparams(),
    )(dh1, z, z, pa, pb, pooled, pw, pwt, ps, wpat, wpbt, wot)


def attn_bwd(q, k, v, o, do, lse, *, nb, lp, hb, name, ride=None):
    t = q.shape[0]
    nq, tail = lp // TQ, lp % TQ
    assert tail % LANES == 0

    def body(q_ref, k_ref, v_ref, o_ref, do_ref, lse_ref, dq_ref, dk_ref, dv_ref, kt, doh, lse_row, delta_row, dqt):
        lane = lax.broadcasted_iota(jnp.int32, (lp, LANES), 1)
        first = lane < V_DIM
        sub = lax.broadcasted_iota(jnp.int32, (LANES, lp), 0)
        for pr in range(hb // 2):
            ls = slice(pr * LANES, (pr + 1) * LANES)
            do = do_ref[:, ls]
            doh[2 * pr] = jnp.where(first, do, jnp.zeros_like(do))
            doh[2 * pr + 1] = jnp.where(first, jnp.zeros_like(do), do)
            prod_t = (do.astype(F32) * o_ref[:, ls].astype(F32)).T
            delta_row[2 * pr] = jnp.sum(jnp.where(sub < V_DIM, prod_t, 0.0), axis=0, keepdims=True)
            delta_row[2 * pr + 1] = jnp.sum(jnp.where(sub < V_DIM, 0.0, prod_t), axis=0, keepdims=True)
        for hd in range(hb):
            lse_row[hd] = lse_ref[hd] * LOG2E
            kt[hd] = k_ref[:, hd * HEAD_SLOT:(hd + 1) * HEAD_SLOT].T
        dqt[...] = jnp.zeros(dqt.shape, F32)
        heads = range(hb)
        hss = [slice(hd * HEAD_SLOT, (hd + 1) * HEAD_SLOT) for hd in heads]

        def k_block(ks, tk, next_q):
            keep = lax.broadcasted_iota(jnp.int32, (tk, tk), 0) <= lax.broadcasted_iota(jnp.int32, (tk, tk), 1)

            def q_step(qs, tq, c, masked):
                qhs = [q_ref[pl.ds(qs, tq), hss[hd]] for hd in heads]
                dos = [doh[hd, pl.ds(qs, tq), :] for hd in heads]
                sts = [_dot_nt(k_ref[pl.ds(ks, tk), hss[hd]], qhs[hd]) for hd in heads]
                dpts = [_dot_nt(v_ref[pl.ds(ks, tk), (hd // 2) * LANES:(hd // 2 + 1) * LANES], dos[hd]) for hd in heads]
                pts, dsts = [], []
                for hd in heads:
                    st = jnp.where(keep, sts[hd], MASK_VALUE) if masked else sts[hd]
                    pt = jnp.exp2(st * EXP2_SCALE - lse_row[hd, :, pl.ds(qs, tq)])
                    dsts.append((pt * (dpts[hd] - delta_row[hd, :, pl.ds(qs, tq)])).astype(BF16))
                    pts.append(pt.astype(BF16))
                dvs = [_dot(pts[hd], dos[hd]) for hd in heads]
                dks = [_dot(dsts[hd], qhs[hd]) for hd in heads]
                dqs = [_dot(kt[hd, :, pl.ds(ks, tk)], dsts[hd]) for hd in heads]
                for hd in heads:
                    dqt[hd, :, pl.ds(qs, tq)] += dqs[hd]
                return tuple((c[hd][0] + dks[hd], c[hd][1] + dvs[hd]) for hd in heads)

            zero = jnp.zeros((tk, LANES), F32)
            c = q_step(ks, tk, tuple((zero, zero) for _ in heads), True)
            if next_q is not None:
                c = lax.fori_loop(next_q, nq, lambda qi, c: q_step(pl.multiple_of(qi * TQ, TQ), TQ, c, False), c)
                if tail:
                    c = q_step(nq * TQ, tail, c, False)
            for hd in heads:
                dk_ref[pl.ds(ks, tk), hss[hd]] = c[hd][0] * SM_SCALE
            for pr in range(hb // 2):
                dv_ref[pl.ds(ks, tk), pr * LANES:(pr + 1) * LANES] = c[2 * pr][1] + c[2 * pr + 1][1]

        def whole_k_block(kj, carry):
            k_block(pl.multiple_of(kj * TK, TK), TK, kj + 1)
            return carry

        lax.fori_loop(0, nq, whole_k_block, 0)
        if tail:
            k_block(nq * TQ, tail, None)
        for hd in range(hb):
            dq_ref[:, hd * HEAD_SLOT:(hd + 1) * HEAD_SLOT] = dqt[hd].T * SM_SCALE

    blk = lambda w: pl.BlockSpec((lp, w), lambda b, g: (b, g))
    return _carrying_call(
        body, ride, (q, k, v, o, do, lse), name=name, grid=(nb, N_HEADS // hb),
        in_specs=[blk(hb * HEAD_SLOT), blk(hb * HEAD_SLOT), blk(hb * V_DIM), blk(hb * V_DIM), blk(hb * V_DIM),
                  pl.BlockSpec((hb, 1, lp), lambda b, g: (g, 0, b))],
        out_specs=[blk(hb * HEAD_SLOT), blk(hb * HEAD_SLOT), blk(hb * V_DIM)],
        out_shape=[jax.ShapeDtypeStruct((t, QK_WIDTH), F32), jax.ShapeDtypeStruct((t, QK_WIDTH), F32),
                   jax.ShapeDtypeStruct((t, D_MODEL), F32)],
        scratch_shapes=[pltpu.VMEM((hb, HEAD_SLOT, lp), BF16), pltpu.VMEM((hb, lp, LANES), BF16), pltpu.VMEM((hb, 1, lp), F32),
                        pltpu.VMEM((hb, 1, lp), F32), pltpu.VMEM((hb, HEAD_SLOT, lp), F32)])


def in_proj_bwd(dh1, h, g_mix, z, dq, dk, dv, dga, dgb, dpool, wint, gq, gkv, wuqt, wukt, wuvt, rope, *, tm, lp, nb, name):
    t = h.shape[0]
    hb = tm // POOL_HALO
    last_halo = t // POOL_HALO - 1

    def body(dh1_ref, h_ref, g_ref, zcq_ref, zckv_ref, dq_ref, dk_ref, dv_ref, dga_ref, dgb_ref, dpool_ref, dnext_ref,
             wint_ref, gq_ref, gkv_ref, wuqt_ref, wukt_ref, wuvt_ref, rope_ref,
             dh_ref, hn_ref, dz_ref, cqn_ref, ckvn_ref, dqb_ref, dkb_ref, dvb_ref, dg_ref, dgq_ref, dgkv_ref):
        i = pl.program_id(0)
        rope_t = rope_ref[...]
        dqb = _rope_bwd(dq_ref[...], *_rope_tables(rope_t, N_HEADS)).astype(BF16)
        dqb_ref[...] = dqb
        xq, rq = _rms(zcq_ref[...])
        gq_v = gq_ref[...]
        cqn_ref[...] = (xq * gq_v).astype(BF16)
        dcq, dgq = _rms_bwd(_dot(dqb, wuqt_ref[...]), xq, rq, gq_v)
        dk = dk_ref[...]
        dkb = dk.astype(BF16)
        dvb = dv_ref[...].astype(BF16)
        dkb_ref[...] = dkb
        dvb_ref[...] = dvb
        xkv, rkv = _rms(zckv_ref[...])
        gkv_v = gkv_ref[...]
        ckvn_ref[...] = (xkv * gkv_v).astype(BF16)
        dckv, dgkv = _rms_bwd(_dot(dkb, wukt_ref[...]) + _dot(dvb, wuvt_ref[...]), xkv, rkv, gkv_v)
        dks = dk[:, :HEAD_SLOT]
        for hd in range(1, N_HEADS):
            dks = dks + dk[:, hd * HEAD_SLOT:(hd + 1) * HEAD_SLOT]
        dzk = _rope_bwd(dks, *_rope_tables(rope_t, 1))
        dp_cur = dpool_ref[...]
        dp_ext = jnp.concatenate([dp_cur, dnext_ref[...]], axis=0)
        r = lax.broadcasted_iota(jnp.int32, (tm, tm + POOL_HALO), 0)
        e = lax.broadcasted_iota(jnp.int32, (tm, tm + POOL_HALO), 1)
        gt_col = i * tm + lax.broadcasted_iota(jnp.int32, (1, tm + POOL_HALO), 1)
        pos_col = _seq_pos(gt_col, lp, nb)
        gt_row = i * tm + lax.broadcasted_iota(jnp.int32, (tm + POOL_HALO, 1), 0)
        pos_row = _seq_pos(gt_row, lp, nb)
        dus = []
        for g, w in enumerate(POOL_WINDOWS):
            gs = slice(g * POOL_GROUP, (g + 1) * POOL_GROUP)
            band = jnp.where((e - r >= 0) & (e - r < jnp.minimum(pos_col + 1, w)) & (gt_col < t), 1.0, 0.0).astype(BF16)
            scaled = jnp.where(gt_row < t, dp_ext[:, gs] / jnp.minimum(pos_row + 1, w).astype(F32), 0.0).astype(BF16)
            dus.append(_dot(band, scaled) - dp_cur[:, gs])
        dz = jnp.concatenate(dus + [dcq, dckv, dzk], axis=1).astype(BF16)
        dz = jnp.concatenate([dz, dga_ref[...], dgb_ref[...]], axis=1)
        dz_ref[...] = dz
        xhat, rr = _rms(h_ref[...])
        gg = g_ref[...]
        hn_ref[...] = (xhat * gg).astype(BF16)
        dx, dg = _rms_bwd(_dot(dz, wint_ref[...]), xhat, rr, gg)
        dh_ref[...] = dh1_ref[...] + dx

        @pl.when(i == 0)
        def _():
            dg_ref[...] = jnp.zeros_like(dg_ref)
            dgq_ref[...] = jnp.zeros_like(dgq_ref)
            dgkv_ref[...] = jnp.zeros_like(dgkv_ref)

        dg_ref[...] += dg
        dgq_ref[...] += dgq
        dgkv_ref[...] += dgkv

    nxt = pl.BlockSpec((POOL_HALO, POOL_WIDTH), lambda i: (jnp.minimum((i + 1) * hb, last_halo), 0))
    return pl.pallas_call(
        body, name=name, grid=(t // tm,),
        in_specs=[_rows(tm, D_MODEL), _rows(tm, D_MODEL), _whole((1, D_MODEL)), _rows(tm, Q_RANK, Z_CQ // Q_RANK),
                  _rows(tm, KV_RANK, Z_CKV // KV_RANK), _rows(tm, QK_WIDTH), _rows(tm, QK_WIDTH), _rows(tm, D_MODEL),
                  _rows(tm, D_MODEL), _rows(tm, D_MODEL), _rows(tm, POOL_WIDTH), nxt,
                  _whole((DZ, D_MODEL)), _whole((1, Q_RANK)), _whole((1, KV_RANK)), _whole((QK_WIDTH, Q_RANK)),
                  _whole((QK_WIDTH, KV_RANK)), _whole((D_MODEL, KV_RANK)), _rows(tm, 4 * LANES)],
        out_specs=[_rows(tm, D_MODEL), _rows(tm, D_MODEL), _rows(tm, DZ), _rows(tm, Q_RANK), _rows(tm, KV_RANK),
                   _rows(tm, QK_WIDTH), _rows(tm, QK_WIDTH), _rows(tm, D_MODEL),
                   _acc((1, D_MODEL)), _acc((1, Q_RANK)), _acc((1, KV_RANK))],
        out_shape=[jax.ShapeDtypeStruct((t, D_MODEL), F32), jax.ShapeDtypeStruct((t, D_MODEL), BF16),
                   jax.ShapeDtypeStruct((t, DZ), BF16), jax.ShapeDtypeStruct((t, Q_RANK), BF16),
                   jax.ShapeDtypeStruct((t, KV_RANK), BF16), jax.ShapeDtypeStruct((t, QK_WIDTH), BF16),
                   jax.ShapeDtypeStruct((t, QK_WIDTH), BF16), jax.ShapeDtypeStruct((t, D_MODEL), BF16),
                   jax.ShapeDtypeStruct((1, D_MODEL), F32), jax.ShapeDtypeStruct((1, Q_RANK), F32),
                   jax.ShapeDtypeStruct((1, KV_RANK), F32)],
        compiler_params=_cparams(),
    )(dh1, h, g_mix, z, z, dq, dk, dv, dga, dgb, dpool, dpool, wint, gq, gkv, wuqt, wukt, wuvt, rope)


_MESH = pl.DeviceIdType.MESH


def _place():
    x, y, c = lax.axis_index("x"), lax.axis_index("y"), lax.axis_index("c")
    return x, y, c, 4 * x + 2 * y + c


def _peer(x, y, c, k):
    px, py, pc = (1 - x) if k & 4 else x, (1 - y) if k & 2 else y, (1 - c) if k & 1 else c
    return (px, py, pc), 4 * px + 2 * py + pc


ALL_PEERS = tuple(range(1, N_DEV))
CHIP_PEERS = (2, 4, 6)
N_CHIPS = N_DEV // 2


def _sem_scratch(n, m):
    return [pltpu.SemaphoreType.DMA((n, m)), pltpu.SemaphoreType.DMA((n, m)), pltpu.SemaphoreType.DMA((n,))]


class Exchange:
    def __init__(self, arrays, out_shapes, sem_cols, plan, aliased=False):
        self.arrays, self.out_shapes, self.plan = list(arrays), list(out_shapes), plan
        self.scratch = _sem_scratch(len(self.arrays), sem_cols)
        self.aliased = aliased

    def split(self, refs):
        n = len(self.arrays)
        return refs[:n], refs[n:2 * n], refs[2 * n:]

    def start(self, srcs, dsts, sems):
        local, sends, _ = self.plan(srcs, dsts, *sems)
        for cp in local + sends:
            cp.start()

    def wait(self, srcs, dsts, sems):
        local, sends, recvs = self.plan(srcs, dsts, *sems)
        for cp in recvs:
            cp.wait_recv()
        for cp in sends:
            cp.wait_send()
        for cp in local:
            cp.wait()

    def aliases(self, first_in, first_out):
        return {first_in + j: first_out + j for j in range(len(self.arrays))} if self.aliased else {}

    def run(self, name):
        def body(*refs):
            srcs, dsts, sems = self.split(refs)
            self.start(srcs, dsts, sems)
            self.wait(srcs, dsts, sems)

        n = len(self.arrays)
        return pl.pallas_call(body, name=name, in_specs=[_ANY] * n, out_specs=[_ANY] * n, out_shape=self.out_shapes,
                              scratch_shapes=self.scratch, input_output_aliases=self.aliases(0, 0))(*self.arrays)


def exchange(arrays, scatter, peers, by_chip=False):
    slots = N_CHIPS if by_chip else N_DEV

    def plan(srcs, dsts, send_sems, recv_sems, local_sems):
        x, y, c, me = _place()
        mine = 2 * x + y if by_chip else me
        local = [pltpu.make_async_copy(src.at[mine] if scatter else src, dst.at[mine], local_sems.at[j])
                 for j, (src, dst) in enumerate(zip(srcs, dsts))]
        sends, recvs = [], []
        for t, k in enumerate(peers):
            peer, pidx = _peer(x, y, c, k)
            theirs = 2 * peer[0] + peer[1] if by_chip else pidx
            for j, (src, dst) in enumerate(zip(srcs, dsts)):
                part = src.at[theirs] if scatter else src
                sems = dict(send_sem=send_sems.at[j, t], recv_sem=recv_sems.at[j, t], device_id=peer, device_id_type=_MESH)
                sends.append(pltpu.make_async_remote_copy(src_ref=part, dst_ref=dst.at[mine], **sems))
                recvs.append(pltpu.make_async_remote_copy(src_ref=part, dst_ref=dst.at[theirs], **sems))
        return local, sends, recvs

    shapes = [jax.ShapeDtypeStruct(a.shape if scatter else (slots,) + a.shape, a.dtype) for a in arrays]
    return Exchange(arrays, shapes, len(peers), plan)


def second_hop(gathered):
    def plan(srcs, dsts, send_sems, recv_sems, local_sems):
        x, y, c, me = _place()
        sibling, _ = _peer(x, y, c, 1)
        sends, recvs = [], []
        for t, k in enumerate(CHIP_PEERS):
            _, landed = _peer(x, y, c, k)
            _, coming = _peer(x, y, c, k ^ 1)
            for j, buf in enumerate(dsts):
                sems = dict(send_sem=send_sems.at[j, t], recv_sem=recv_sems.at[j, t], device_id=sibling, device_id_type=_MESH)
                sends.append(pltpu.make_async_remote_copy(src_ref=buf.at[landed], dst_ref=buf.at[landed], **sems))
                recvs.append(pltpu.make_async_remote_copy(src_ref=buf.at[coming], dst_ref=buf.at[coming], **sems))
        return [], sends, recvs

    shapes = [jax.ShapeDtypeStruct(a.shape, a.dtype) for a in gathered]
    return Exchange(gathered, shapes, len(CHIP_PEERS), plan, aliased=True)


FIRST_HOP_PEERS = (1,) + CHIP_PEERS


def _gather_two_level(arrays, name):
    n = len(arrays)

    def body(*refs):
        srcs, dsts, (send_sems, recv_sems, local_sems) = refs[:n], refs[n:2 * n], refs[2 * n:]
        x, y, c, me = _place()
        sibling, sidx = _peer(x, y, c, 1)

        def copy(j, sem, block, to, src=None):
            rows = dsts[j].at[block]
            return pltpu.make_async_remote_copy(src_ref=rows if src is None else src, dst_ref=rows, send_sem=send_sems.at[j, sem],
                                                recv_sem=recv_sems.at[j, sem], device_id=to, device_id_type=_MESH)

        local = [pltpu.make_async_copy(srcs[j], dsts[j].at[me], local_sems.at[j]) for j in range(n)]
        for cp in local:
            cp.start()
        first = [copy(j, 1 + t, me, _peer(x, y, c, k)[0], src=srcs[j]) for t, k in enumerate(CHIP_PEERS) for j in range(n)]
        first += [copy(j, 0, me, sibling, src=srcs[j]) for j in range(n)]
        for cp in first:
            cp.start()
        passed = []
        for t, k in enumerate(CHIP_PEERS):
            peer, pidx = _peer(x, y, c, k)
            for j in range(n):
                copy(j, 1 + t, pidx, peer).wait_recv()
                passed.append(copy(j, 4 + t, pidx, sibling))
                passed[-1].start()
        for j in range(n):
            copy(j, 0, sidx, sibling).wait_recv()
        for t, k in enumerate(CHIP_PEERS):
            _, pidx = _peer(x, y, c, k ^ 1)
            for j in range(n):
                copy(j, 4 + t, pidx, sibling).wait_recv()
        for cp in first + passed:
            cp.wait_send()
        for cp in local:
            cp.wait()

    shapes = [jax.ShapeDtypeStruct((N_DEV,) + a.shape, a.dtype) for a in arrays]
    return pl.pallas_call(body, name=name, in_specs=[_ANY] * n, out_specs=[_ANY] * n, out_shape=shapes,
                          scratch_shapes=_sem_scratch(n, 1 + 2 * len(CHIP_PEERS)))(*arrays)


def _to_sibling(arrays, name):
    n = len(arrays)

    def body(*refs):
        srcs, dsts, (send_sems, recv_sems) = refs[:n], refs[n:2 * n], refs[2 * n:]
        x, y, c, _ = _place()
        sibling, _ = _peer(x, y, c, 1)
        copies = [pltpu.make_async_remote_copy(src_ref=srcs[j].at[1 - c], dst_ref=dsts[j], send_sem=send_sems.at[j],
                                               recv_sem=recv_sems.at[j], device_id=sibling, device_id_type=_MESH) for j in range(n)]
        for cp in copies:
            cp.start()
        for cp in copies:
            cp.wait()

    shapes = [jax.ShapeDtypeStruct(a.shape[1:], a.dtype) for a in arrays]
    return pl.pallas_call(body, name=name, in_specs=[_ANY] * n, out_specs=[_ANY] * n, out_shape=shapes,
                          scratch_shapes=[pltpu.SemaphoreType.DMA((n,)), pltpu.SemaphoreType.DMA((n,))])(*arrays)


def pair_add(own, theirs, core, *, name):
    _, ns, r, c = own.shape
    rb = _row_block(r, c)

    def body(core_ref, a_ref, b_ref, o_ref):
        o_ref[...] = (a_ref[...].astype(F32) + b_ref[...].astype(F32)).astype(o_ref.dtype)

    return pl.pallas_call(
        body, name=name,
        grid_spec=pltpu.PrefetchScalarGridSpec(
            num_scalar_prefetch=1, grid=(ns, r // rb),
            in_specs=[pl.BlockSpec((None, None, rb, c), lambda i, j, core_ref: (core_ref[0], i, j, 0)),
                      pl.BlockSpec((None, rb, c), lambda i, j, core_ref: (i, j, 0))],
            out_specs=pl.BlockSpec((None, rb, c), lambda i, j, core_ref: (i, j, 0))),
        out_shape=jax.ShapeDtypeStruct((ns, r, c), own.dtype), compiler_params=_cparams(),
    )(core, own, theirs)


ADAMW_BLOCK_BYTES = 1 << 20


def _row_block(r, c):
    for rb in range(r, 0, -1):
        if r % rb == 0 and (rb % 16 == 0 or rb == r) and rb * c * 4 <= ADAMW_BLOCK_BYTES:
            return rb
    return r


def adamw(w, m, v, parts, *, name):
    depth, r, c = w.shape
    n_parts = parts[0].shape[0]
    rb = _row_block(r, c)

    def body(w_ref, m_ref, v_ref, *refs):
        p_refs, (g_ref, d_ref, nm_ref, nv_ref) = refs[:depth], refs[depth:]

        def total(p_ref):
            g = p_ref[0].astype(F32)
            for j in range(1, n_parts):
                g = g + p_ref[j].astype(F32)
            return g

        g = total(p_refs[0])
        for l in range(1, depth):
            g = jnp.where(pl.program_id(0) == l, total(p_refs[l]), g)
        g_ref[...] = g
        m_new = ADAM_B1 * m_ref[...] + (1.0 - ADAM_B1) * g
        v_new = ADAM_B2 * v_ref[...] + (1.0 - ADAM_B2) * (g * g)
        m_hat = m_new / (1.0 - ADAM_B1 ** ADAM_STEP)
        v_hat = v_new / (1.0 - ADAM_B2 ** ADAM_STEP)
        d_ref[...] = -ADAM_LR * (m_hat / (jnp.sqrt(v_hat) + ADAM_EPS) + ADAM_WD * w_ref[...])
        nm_ref[...] = m_new
        nv_ref[...] = v_new

    wblk = pl.BlockSpec((None, rb, c), lambda l, i: (l, i, 0))
    pblk = pl.BlockSpec((n_parts, rb, c), lambda l, i: (0, i, 0))
    return pl.pallas_call(
        body, name=name, grid=(depth, r // rb),
        in_specs=[wblk, wblk, wblk] + [pblk] * depth, out_specs=[wblk] * 4,
        out_shape=[jax.ShapeDtypeStruct((depth, r, c), F32)] * 4, compiler_params=_cparams(),
    )(w, m, v, *parts)


BIG = (("w_in", 2), ("w_uq", 2), ("w_ukv", 2), ("w_pa", 2), ("w_pb", 1), ("w_o", 1), ("w_gate", 2), ("w_up", 2), ("w_down", 1))
SMALL = ("norm_mix_g", "pool_w", "pool_scale", "q_norm_g", "kv_norm_g", "norm_ffn_g", "final_norm_g")
WEIGHTS = ("meta_tokens", "norm_mix_g", "w_in", "pool_w", "pool_scale", "q_norm_g", "kv_norm_g", "w_uq", "w_ukv", "w_pa", "w_pb",
           "w_o", "norm_ffn_g", "w_gate", "w_up", "w_down", "final_norm_g")
HEAD_QK = QK_NOPE + QK_ROPE
KR_END = Z_KR + QK_ROPE


def _cat_cols(parts):
    return [jnp.concatenate(parts, axis=1)]


def _cat_rows(parts):
    return [jnp.concatenate(parts, axis=0)]


def _arr_w_in(parts):
    full = jnp.concatenate(parts, axis=1)
    zc = lambda n: jnp.zeros((full.shape[0], n), full.dtype)
    return [jnp.concatenate([full[:, :Z_KR], zc(QK_NOPE), full[:, Z_KR:KR_END], zc(LANES - HEAD_QK), full[:, KR_END:]], axis=1)]


def _arr_w_uq(parts):
    full = jnp.concatenate(parts, axis=1)
    z = jnp.zeros((full.shape[0], HEAD_SLOT - HEAD_QK), full.dtype)
    pieces = []
    for hd in range(N_HEADS):
        pieces += [full[:, hd * HEAD_QK:(hd + 1) * HEAD_QK], z]
    return [jnp.concatenate(pieces, axis=1)]


def _arr_w_ukv(parts):
    full = jnp.concatenate(parts, axis=1)
    z = jnp.zeros((full.shape[0], HEAD_SLOT - QK_NOPE), full.dtype)
    wide = QK_NOPE + V_DIM
    k, v = [], []
    for hd in range(N_HEADS):
        k += [full[:, hd * wide:hd * wide + QK_NOPE], z]
        v.append(full[:, hd * wide + QK_NOPE:(hd + 1) * wide])
    return [jnp.concatenate(k, axis=1), jnp.concatenate(v, axis=1)]


def arrange(g, fn, out_shapes, name):
    def body(g_ref, *o_refs):
        for o_ref, val in zip(o_refs, fn([g_ref[p] for p in range(N_DEV)])):
            o_ref[...] = val

    return pl.pallas_call(
        body, name=name, grid=(1,),
        in_specs=[pl.BlockSpec(g.shape, lambda i: (0, 0, 0))],
        out_specs=[pl.BlockSpec(s, lambda i: (0, 0)) for s in out_shapes],
        out_shape=[jax.ShapeDtypeStruct(s, g.dtype) for s in out_shapes], compiler_params=_cparams(),
    )(g)


def _arranged_ranges(lo, hi):
    out = []
    for a, b, shift in ((0, Z_KR, 0), (Z_KR, KR_END, QK_NOPE), (KR_END, D_IN, LANES - QK_ROPE)):
        s, e = max(lo, a), min(hi, b)
        if s < e:
            out.append((s + shift, e + shift))
    return out


def _chunks_w_in(acc):
    cs = D_IN // N_DEV
    return [jnp.concatenate([acc[:, a:b] for a, b in _arranged_ranges(p * cs, (p + 1) * cs)], axis=1) for p in range(N_DEV)]


def _chunks_w_uq(acc):
    per = N_HEADS // N_DEV
    return [jnp.concatenate([acc[:, hd * HEAD_SLOT:hd * HEAD_SLOT + HEAD_QK] for hd in range(p * per, (p + 1) * per)], axis=1)
            for p in range(N_DEV)]


def _chunks_w_ukv(acc_k, acc_v):
    per = N_HEADS // N_DEV
    out = []
    for p in range(N_DEV):
        pieces = []
        for hd in range(p * per, (p + 1) * per):
            pieces += [acc_k[:, hd * HEAD_SLOT:hd * HEAD_SLOT + QK_NOPE], acc_v[:, hd * V_DIM:(hd + 1) * V_DIM]]
        out.append(jnp.concatenate(pieces, axis=1))
    return out


def _chunks_cols(acc):
    cs = acc.shape[1] // N_DEV
    return [acc[:, p * cs:(p + 1) * cs] for p in range(N_DEV)]


def _chunks_rows(acc):
    rs = acc.shape[0] // N_DEV
    return [acc[p * rs:(p + 1) * rs, :] for p in range(N_DEV)]


def _chunks_cols_transposed(acc):
    at = acc[...].T
    rs = at.shape[0] // N_DEV
    return [at[p * rs:(p + 1) * rs, :] for p in range(N_DEV)]


def _pack(parts, row_multiple):
    flat = jnp.concatenate([p.reshape(-1) for p in parts])
    return jnp.pad(flat, (0, -flat.shape[0] % (row_multiple * LANES))).reshape(-1, LANES)


def _unpack(packed, shapes):
    flat, out, off = packed.reshape(-1), [], 0
    for s in shapes:
        n = 1
        for d in s:
            n *= d
        out.append(flat[off:off + n].reshape(s))
        off += n
    return out


def _rope_table(lp, nb):
    inv = 1.0 / (ROPE_THETA ** (jnp.arange(0, QK_ROPE, 2, dtype=F32) / QK_ROPE))
    ang = jnp.arange(lp, dtype=F32)[:, None] * inv[None, :]
    cos, sin = jnp.cos(ang), jnp.sin(ang)
    z = lambda n: jnp.zeros((lp, n), F32)
    tail = LANES - QK_NOPE - QK_ROPE
    c = jnp.concatenate([jnp.ones((lp, QK_NOPE), F32), cos, cos, z(tail)], axis=1)
    cr = jnp.concatenate([z(QK_NOPE), cos, cos, z(tail)], axis=1)
    s1 = jnp.concatenate([z(QK_NOPE), -sin, z(HALF_ROPE), z(tail)], axis=1)
    s2 = jnp.concatenate([z(QK_NOPE), z(HALF_ROPE), sin, z(tail)], axis=1)
    return jnp.tile(jnp.concatenate([c, cr, s1, s2], axis=1), (nb, 1))


MIX = ("w_in", "w_uq", "w_ukv", "w_pa", "w_pb", "w_o")
FFN = ("w_gate", "w_up", "w_down")
TRANSPOSED = ("w_gate", "w_up")
ARRANGERS = {
    "w_in": (_arr_w_in, (("win", "wint", (D_MODEL, DZ)),)), "w_uq": (_arr_w_uq, (("wuq", "wuqt", (Q_RANK, QK_WIDTH)),)),
    "w_ukv": (_arr_w_ukv, (("wuk", "wukt", (KV_RANK, QK_WIDTH)), ("wuv", "wuvt", (KV_RANK, D_MODEL)))),
    "w_pa": (_cat_cols, (("wpa", "wpat", (POOL_WIDTH, D_MODEL)),)), "w_pb": (_cat_rows, (("wpb", "wpbt", (D_MODEL, D_MODEL)),)),
    "w_o": (_cat_rows, (("wo", "wot", (D_MODEL, D_MODEL)),)), "w_gate": (_cat_rows, (("wgt", "wg", (D_FF, D_MODEL)),)),
    "w_up": (_cat_rows, (("wut", "wu", (D_FF, D_MODEL)),)), "w_down": (_cat_rows, (("wd", "wdt", (D_FF, D_MODEL)),)),
}


def _operands(gathered, names, l):
    p = {}
    for n in names:
        fn, outs = ARRANGERS[n]
        for (key, key_t, _), a in zip(outs, arrange(gathered[n], fn, [s for _, _, s in outs], f"arrange_{n}_{l}")):
            p[key], p[key_t] = a, a.T
    return p


def _small_operands(small, l):
    pw = small["pool_w"][l].astype(BF16)
    return dict(g_mix=small["norm_mix_g"][l][None], gq=small["q_norm_g"][l][None], gkv=small["kv_norm_g"][l][None],
                g_ffn=small["norm_ffn_g"][l][None], ps=small["pool_scale"][l][None], pw=pw, pwt=jnp.swapaxes(pw, 1, 2))


class MeshComm:
    def __init__(self, w, meta_tokens):
        self.src = lambda n, l: w[n][l].astype(BF16)
        self.meta_tokens = meta_tokens
        self.core = lax.axis_index("c").astype(jnp.int32).reshape(1)
        self.rides = {0: [(n, 0) for n in FFN] + [(n, 1) for n in MIX], 1: [(n, 1) for n in FFN]}

    def first_weights(self):
        got = _gather_two_level([self.src(n, 0) for n in MIX] + [self.meta_tokens], "gather_mix_0")
        return dict(zip(MIX, got)), jnp.moveaxis(got[-1], 0, 1).reshape(N_META, D_MODEL)

    def first_hop(self, l):
        return exchange([self.src(n, layer) for n, layer in self.rides[l]], False, FIRST_HOP_PEERS)

    def second_hop(self, l, landed):
        return second_hop(landed)

    def carried(self, l, full, names, layer):
        return {n: full[self.rides[l].index((n, layer))] for n in names}

    def pair_sums(self, own, names, tag):
        theirs = _to_sibling(own, f"pair_grads_{tag}")
        return [pair_add(a, b, self.core, name=f"pair_add_{n}_{tag}") for n, a, b in zip(names, own, theirs)]

    def scatter(self, sums):
        return exchange(sums, True, CHIP_PEERS, by_chip=True)

    def scatter_now(self, sums, name):
        return self.scatter(sums).run(name)


HEADS_FWD, HEADS_BWD = 8, 4
TILE_ROWS, TILE_ROWS_BWD = 512, 256


def _tile(t, target):
    n = max(1, -(-t // (target + target // 8)))
    while t % n or (t // n) % 16:
        n += 1
    return t // n


def _wgrad_tile(t):
    return max(tm for tm in (2 * TQ, TQ, LANES) if t % tm == 0)


def _ffn_bwd_part(dh2, p, s, tag):
    d, ff = D_MODEL, D_FF // N_DEV
    t = dh2.shape[0]
    wg_ = lambda n, x, ys, fn, shape: wgrad(x, ys, fn, shape, tm=_wgrad_tile(t), name=f"wgrad_{n}_{tag}")
    dh1, hn2, act, dgt, dup, dg_ffn = ffn_bwd(dh2, s["h1"], p["g_ffn"], s["gt"], s["up"], p["wgt"], p["wut"], p["wdt"],
                                              tm=_tile(t, TILE_ROWS_BWD), name=f"ffn_bwd_{tag}")
    chunks = [wg_("gate", hn2, [dgt], _chunks_cols_transposed, (ff, d)), wg_("up", hn2, [dup], _chunks_cols_transposed, (ff, d)),
              wg_("down", act, [dh2], _chunks_rows, (ff, d))]
    return dh1, chunks, dict(norm_ffn_g=dg_ffn[0])


def _mix_bwd_part(dh1, p, s, rope, nb, lp, tag, ride):
    d = D_MODEL
    t = dh1.shape[0]
    wg_ = lambda n, x, ys, fn, shape: wgrad(x, ys, fn, shape, tm=_wgrad_tile(t), name=f"wgrad_{n}_{tag}")
    dga, dgb, dpa, dpb, do, dpool, dps, dpw = merge_bwd(dh1, s["z"], s["pa"], s["pb"], s["pooled"], p["pw"], p["pwt"], p["ps"],
                                                        p["wpat"], p["wpbt"], p["wot"], tm=_tile(t, TILE_ROWS),
                                                        name=f"merge_bwd_{tag}")
    c_o = wg_("o", s["mg"], [dh1], _chunks_rows, (d // N_DEV, d))
    c_pa = wg_("pa", s["a"], [dpa], _chunks_cols, (POOL_WIDTH, d // N_DEV))
    c_pb = wg_("pb", s["o"], [dpb], _chunks_rows, (d // N_DEV, d))
    (dq, dk, dv), brought = attn_bwd(s["q"], s["k"], s["v"], s["o"], do, s["lse"], nb=nb, lp=lp, hb=HEADS_BWD,
                                     name=f"attn_bwd_{tag}", ride=ride)
    dh, hn, dz, cqn, ckvn, dqb, dkb, dvb, dg_mix, dgq, dgkv = in_proj_bwd(
        dh1, s["h"], p["g_mix"], s["z"], dq, dk, dv, dga, dgb, dpool, p["wint"], p["gq"], p["gkv"], p["wuqt"], p["wukt"], p["wuvt"],
        rope, tm=_tile(t, TILE_ROWS_BWD), lp=lp, nb=nb, name=f"in_proj_bwd_{tag}")
    c_in = wg_("in", hn, [dz], _chunks_w_in, (d, D_IN // N_DEV))
    c_uq = wg_("uq", cqn, [dqb], _chunks_w_uq, (Q_RANK, N_HEADS * HEAD_QK // N_DEV))
    c_ukv = wg_("ukv", ckvn, [dkb, dvb], _chunks_w_ukv, (KV_RANK, N_HEADS * (QK_NOPE + V_DIM) // N_DEV))
    small = dict(pool_scale=dps[0], pool_w=dpw, norm_mix_g=dg_mix[0], q_norm_g=dgq[0], kv_norm_g=dgkv[0])
    return dh, [c_in, c_uq, c_ukv, c_pa, c_pb, c_o], small, brought


def train_step(x, loss_target, small, comm):
    nb, seq, d = x.shape
    lp = -(-(N_META + seq) // LANES) * LANES
    t = nb * lp
    assert nb <= 2 and DEPTH == 2
    tm = _tile(t, TILE_ROWS)
    rope = _rope_table(lp, nb)
    gathered, meta = comm.first_weights()
    pad = jnp.zeros((nb, lp - N_META - seq, d), F32)
    h = jnp.concatenate([jnp.broadcast_to(meta[None], (nb, N_META, d)), x, pad], axis=1).reshape(t, d)
    target = jnp.concatenate([jnp.zeros((nb, N_META, d), F32), loss_target, pad], axis=1).reshape(t, d)

    params, saved, full = [], [], {}
    for l in range(DEPTH):
        p = _small_operands(small, l)
        p.update(_operands(gathered if l == 0 else comm.carried(0, full[0], MIX, 1), MIX, l))
        z, q, k, v = in_proj_fwd(h, p["g_mix"], p["win"], p["gq"], p["gkv"], p["wuq"], p["wuk"], p["wuv"], rope, tm=tm,
                                 name=f"in_proj_fwd_{l}")
        (o, lse), landed = attn_fwd(q, k, v, nb=nb, lp=lp, hb=HEADS_FWD, name=f"attn_fwd_{l}", ride=comm.first_hop(l))
        (h1, pooled, a, pa, pb, mg), full[l] = merge_fwd(h, z, o, p["pw"], p["ps"], p["wpa"], p["wpb"], p["wo"], tm=tm, lp=lp,
                                                          nb=nb, name=f"merge_fwd_{l}", ride=comm.second_hop(l, landed))
        p.update(_operands(comm.carried(l, full[l], FFN, l), FFN, l))
        h2, gt, up = ffn_fwd(h1, p["g_ffn"], p["wg"], p["wu"], p["wd"], tm=tm, name=f"ffn_fwd_{l}")
        params.append(p)
        saved.append(dict(h=h, z=z, q=q, k=k, v=v, o=o, lse=lse, h1=h1, pooled=pooled, a=a, pa=pa, pb=pb, mg=mg, gt=gt, up=up))
        h = h2
    parts, dh, dgf = loss_head(h, small["final_norm_g"][None], target, tm=tm, lp=lp, nb=nb, seq=seq, name="loss_head")
    loss = jnp.sum(parts[::8, 0])

    sums = {}
    dh, c_ffn1, small1 = _ffn_bwd_part(dh, params[1], saved[1], 1)
    s_ffn1 = comm.pair_sums(c_ffn1, FFN, "ffn_1")
    dh, c_mix1, sm, brought = _mix_bwd_part(dh, params[1], saved[1], rope, nb, lp, 1, comm.scatter(s_ffn1))
    small1.update(sm)
    sums.update({(n, 1): a for n, a in zip(FFN, brought)})
    s_mix1 = comm.pair_sums(c_mix1, MIX, "mix_1")
    dh, c_ffn0, small0 = _ffn_bwd_part(dh, params[0], saved[0], 0)
    s_ffn0 = comm.pair_sums(c_ffn0, FFN, "ffn_0")
    dh, c_mix0, sm, brought = _mix_bwd_part(dh, params[0], saved[0], rope, nb, lp, 0, comm.scatter(s_mix1 + s_ffn0))
    small0.update(sm)
    sums.update({(n, l): a for (n, l), a in zip([(n, 1) for n in MIX] + [(n, 0) for n in FFN], brought)})
    dh = dh.reshape(nb, lp, d)
    dmeta = jnp.sum(dh[:, :N_META], axis=0)
    meta_chunks = jnp.transpose(dmeta.reshape(N_META, N_CHIPS, 2, d // N_DEV), (2, 1, 0, 3)).astype(BF16)
    s_last = comm.pair_sums(c_mix0 + [meta_chunks], MIX + ("meta_tokens",), "mix_0")
    last = comm.scatter_now(s_last, "scatter_mix_0")
    sums.update({(n, 0): a for n, a in zip(MIX + ("meta_tokens",), last)})
    small_grads = {n: jnp.stack([small0[n], small1[n]]) for n in small0}
    small_grads["final_norm_g"] = dgf[0]
    return loss, dh[:, N_META:N_META + seq], sums, small_grads


def kernel(x, meta_tokens, norm_mix_g, w_in, pool_w, pool_scale, q_norm_g, kv_norm_g, w_uq, w_ukv, w_pa, w_pb, w_o, norm_ffn_g, w_gate, w_up, w_down, final_norm_g, loss_target, m_meta_tokens, m_norm_mix_g, m_w_in, m_pool_w, m_pool_scale, m_q_norm_g, m_kv_norm_g, m_w_uq, m_w_ukv, m_w_pa, m_w_pb, m_w_o, m_norm_ffn_g, m_w_gate, m_w_up, m_w_down, m_final_norm_g, v_meta_tokens, v_norm_mix_g, v_w_in, v_pool_w, v_pool_scale, v_q_norm_g, v_kv_norm_g, v_w_uq, v_w_ukv, v_w_pa, v_w_pb, v_w_o, v_norm_ffn_g, v_w_gate, v_w_up, v_w_down, v_final_norm_g):
    args = dict(locals())
    w = {n: args[n] for n in WEIGHTS}
    m = {n: args["m_" + n] for n in WEIGHTS}
    v = {n: args["v_" + n] for n in WEIGHTS}
    small = {n: w[n] for n in SMALL}
    as_handled = lambda a, n: jnp.swapaxes(a, 1, 2) if n in TRANSPOSED else a
    wh, mh, vh = ({n: as_handled(d[n], n) for n, _ in BIG} for d in (w, m, v))

    loss, grad_x, sums, small_grads = train_step(x, loss_target, small, MeshComm(wh, meta_tokens))
    loss = lax.psum(loss, ("x", "y", "c"))
    (small_recv,) = exchange([_pack([small_grads[n] for n in SMALL], 8)], False, ALL_PEERS).run("gather_small_grads")

    out = {n: [as_handled(a, n) for a in adamw(wh[n], mh[n], vh[n], [sums[(n, l)] for l in range(DEPTH)], name=f"adamw_{n}")]
           for n, _ in BIG}
    out["meta_tokens"] = [a[0] for a in adamw(meta_tokens[None], m["meta_tokens"][None], v["meta_tokens"][None],
                                              [sums[("meta_tokens", 0)]], name="adamw_meta_tokens")]
    pk = lambda d: _pack([d[n] for n in SMALL], 8)[None]
    packed = adamw(pk(w), pk(m), pk(v), [small_recv], name="adamw_small")
    shapes = [w[n].shape for n in SMALL]
    for n, *kinds in zip(SMALL, *[_unpack(packed[kind][0], shapes) for kind in range(4)]):
        out[n] = kinds
    return (loss, grad_x, *[out[n][kind] for kind in range(4) for n in WEIGHTS])
```

```python
import functools
import math

import jax
import jax.numpy as jnp
from jax import lax
from jax.experimental import pallas as pl
from jax.experimental.pallas import tpu as pltpu

F32, BF16 = jnp.float32, jnp.bfloat16

D_MODEL = 1024
N_META = 16
N_HEADS = 16
QK_NOPE, QK_ROPE, V_DIM = 64, 32, 64
HALF_ROPE = QK_ROPE // 2
Q_RANK, KV_RANK = 256, 128
POOL_WINDOWS = (2, 4, 8, 16)
POOL_GROUP = 128
POOL_WIDTH = POOL_GROUP * len(POOL_WINDOWS)
POOL_HALO = 16
D_FF = 2816
D_IN = 2976
NORM_EPS = 1e-6
SM_SCALE = (QK_NOPE + QK_ROPE) ** -0.5
LOG2E = math.log2(math.e)
EXP2_SCALE = SM_SCALE * LOG2E
MASK_VALUE = -1e30
ROPE_THETA = 10000.0
DEPTH = 2
N_DEV = 8

ADAM_LR, ADAM_B1, ADAM_B2, ADAM_EPS, ADAM_WD, ADAM_STEP = 0.001, 0.9, 0.999, 1e-08, 0.01, 10

LANES = 128
HEAD_SLOT = LANES
QK_WIDTH = N_HEADS * HEAD_SLOT
Z_CQ, Z_CKV, Z_KR, Z_GA, Z_GB, DZ = 512, 768, 896, 1024, 2048, 3072
TQ = TK = 256
VMEM_LIMIT = 56 * 1024 * 1024


def _cparams():
    return pltpu.CompilerParams(vmem_limit_bytes=VMEM_LIMIT)


def _rows(tm, width, col=0):
    return pl.BlockSpec((tm, width), lambda i: (i, col))


def _whole(shape):
    zeros = (0,) * len(shape)
    return pl.BlockSpec(shape, lambda i: zeros, pipeline_mode=pl.Buffered(1))


def _acc(shape):
    zeros = (0,) * len(shape)
    return pl.BlockSpec(shape, lambda i: zeros)


def _dot(a, b):
    return jnp.dot(a, b, preferred_element_type=F32)


def _dot_tn(a, b):
    return lax.dot_general(a, b, (((0,), (0,)), ((), ())), preferred_element_type=F32)


def _dot_nt(a, b):
    return lax.dot_general(a, b, (((1,), (1,)), ((), ())), preferred_element_type=F32)


def _rms(x):
    r = lax.rsqrt(jnp.mean(x * x, axis=-1, keepdims=True) + NORM_EPS)
    return x * r, r


def _rms_bwd(dy, xhat, r, g):
    dg = jnp.sum(dy * xhat, axis=0, keepdims=True)
    dxh = dy * g
    dx = r * (dxh - xhat * jnp.mean(dxh * xhat, axis=-1, keepdims=True))
    return dx, dg


def _sigmoid(x):
    return 1.0 / (1.0 + jnp.exp(-x))


def _rope_fwd(q, c, s1, s2):
    w = q.shape[1]
    return q * c + pltpu.roll(q, w - HALF_ROPE, 1) * s1 + pltpu.roll(q, HALF_ROPE, 1) * s2


def _rope_bwd(dq, c, s1, s2):
    w = dq.shape[1]
    return dq * c + pltpu.roll(dq * s1, HALF_ROPE, 1) + pltpu.roll(dq * s2, w - HALF_ROPE, 1)


def _rope_tables(rope, reps):
    c, cr, s1, s2 = (rope[:, k * LANES:(k + 1) * LANES] for k in range(4))
    if reps > 1:
        return jnp.tile(c, (1, reps)), jnp.tile(s1, (1, reps)), jnp.tile(s2, (1, reps))
    return cr, s1, s2


def _seq_pos(gi, lp, nb):
    pos = gi
    for b in range(1, nb):
        pos = jnp.where(gi >= b * lp, gi - b * lp, pos)
    return pos


_ANY = pl.BlockSpec(memory_space=pl.ANY)


def _carrying_call(body, ride, operands, *, name, grid, in_specs, out_specs, out_shape, scratch_shapes=()):
    n_in, n_out = len(in_specs), len(out_specs)
    if ride is None:
        out = pl.pallas_call(body, name=name, grid=grid, in_specs=in_specs, out_specs=out_specs, out_shape=out_shape,
                             scratch_shapes=list(scratch_shapes), compiler_params=_cparams())(*operands)
        return out, []
    ne = len(ride.arrays)

    def carrying(*refs):
        ins, r_in, rest = refs[:n_in], refs[n_in:n_in + ne], refs[n_in + ne:]
        outs, r_out, rest = rest[:n_out], rest[n_out:n_out + ne], rest[n_out + ne:]
        scratch, sems = rest[:len(scratch_shapes)], rest[len(scratch_shapes):]
        ids = [pl.program_id(a) for a in range(len(grid))]
        first = functools.reduce(jnp.logical_and, [i == 0 for i in ids])
        last = functools.reduce(jnp.logical_and, [i == g - 1 for i, g in zip(ids, grid)])

        @pl.when(first)
        def _():
            ride.start(r_in, r_out, sems)

        body(*ins, *outs, *scratch)

        @pl.when(last)
        def _():
            ride.wait(r_in, r_out, sems)

    out = pl.pallas_call(
        carrying, name=name, grid=grid, in_specs=list(in_specs) + [_ANY] * ne, out_specs=list(out_specs) + [_ANY] * ne,
        out_shape=list(out_shape) + ride.out_shapes, scratch_shapes=list(scratch_shapes) + ride.scratch,
        input_output_aliases=ride.aliases(n_in, n_out), compiler_params=_cparams(),
    )(*operands, *ride.arrays)
    return out[:n_out], out[n_out:]


def in_proj_fwd(h, g_mix, win, gq, gkv, wuq, wuk, wuv, rope, *, tm, name):
    t = h.shape[0]

    def body(h_ref, g_ref, win_ref, gq_ref, gkv_ref, wuq_ref, wuk_ref, wuv_ref, rope_ref, z_ref, q_ref, k_ref, v_ref):
        xhat, _ = _rms(h_ref[...])
        hn = (xhat * g_ref[...]).astype(BF16)
        z = _dot(hn, win_ref[...])
        z_ref[...] = z
        rope_t = rope_ref[...]
        xq, _ = _rms(z[:, Z_CQ:Z_CKV])
        cqn = (xq * gq_ref[...]).astype(BF16)
        q = _rope_fwd(_dot(cqn, wuq_ref[...]), *_rope_tables(rope_t, N_HEADS))
        q_ref[...] = q.astype(BF16)
        xkv, _ = _rms(z[:, Z_CKV:Z_KR])
        ckvn = (xkv * gkv_ref[...]).astype(BF16)
        kr = _rope_fwd(z[:, Z_KR:Z_GA], *_rope_tables(rope_t, 1))
        k_ref[...] = (_dot(ckvn, wuk_ref[...]) + jnp.tile(kr, (1, N_HEADS))).astype(BF16)
        v_ref[...] = _dot(ckvn, wuv_ref[...]).astype(BF16)

    return pl.pallas_call(
        body, name=name, grid=(t // tm,),
        in_specs=[_rows(tm, D_MODEL), _whole((1, D_MODEL)), _whole((D_MODEL, DZ)), _whole((1, Q_RANK)), _whole((1, KV_RANK)),
                  _whole((Q_RANK, QK_WIDTH)), _whole((KV_RANK, QK_WIDTH)), _whole((KV_RANK, D_MODEL)), _rows(tm, 4 * LANES)],
        out_specs=[_rows(tm, DZ), _rows(tm, QK_WIDTH), _rows(tm, QK_WIDTH), _rows(tm, D_MODEL)],
        out_shape=[jax.ShapeDtypeStruct((t, DZ), F32), jax.ShapeDtypeStruct((t, QK_WIDTH), BF16),
                   jax.ShapeDtypeStruct((t, QK_WIDTH), BF16), jax.ShapeDtypeStruct((t, D_MODEL), BF16)],
        compiler_params=_cparams(),
    )(h, g_mix, win, gq, gkv, wuq, wuk, wuv, rope)


def attn_fwd(q, k, v, *, nb, lp, hb, name, ride=None):
    t = q.shape[0]
    nq, tail = lp // TQ, lp % TQ
    assert tail % LANES == 0

    def body(q_ref, k_ref, v_ref, o_ref, lse_ref, vt):
        for pr in range(hb // 2):
            vt[pr] = v_ref[:, pr * LANES:(pr + 1) * LANES].T

        def q_block(qs, tq, whole_k):
            qh = [q_ref[pl.ds(qs, tq), hd * HEAD_SLOT:(hd + 1) * HEAD_SLOT] for hd in range(hb)]
            keep = lax.broadcasted_iota(jnp.int32, (tq, tq), 0) <= lax.broadcasted_iota(jnp.int32, (tq, tq), 1)

            def k_step(ks, tk, c, masked):
                sts = [_dot_nt(k_ref[pl.ds(ks, tk), hd * HEAD_SLOT:(hd + 1) * HEAD_SLOT], qh[hd]) for hd in range(hb)]
                ps, stats = [], []
                for hd in range(hb):
                    m, l, _ = c[hd]
                    st = jnp.where(keep, sts[hd], MASK_VALUE) if masked else sts[hd]
                    m_new = jnp.maximum(m, jnp.max(st, axis=0, keepdims=True))
                    p = jnp.exp2((st - m_new) * EXP2_SCALE)
                    alpha = jnp.exp2((m - m_new) * EXP2_SCALE)
                    ps.append(p.astype(BF16))
                    stats.append((m_new, alpha * l + jnp.sum(p, axis=0, keepdims=True), alpha))
                pvs = [_dot(vt[hd // 2, :, pl.ds(ks, tk)], ps[hd]) for hd in range(hb)]
                return tuple((stats[hd][0], stats[hd][1], stats[hd][2] * c[hd][2] + pvs[hd]) for hd in range(hb))

            init = tuple((jnp.full((1, tq), MASK_VALUE, F32), jnp.zeros((1, tq), F32), jnp.zeros((LANES, tq), F32))
                         for _ in range(hb))
            c = lax.fori_loop(0, whole_k, lambda kj, c: k_step(pl.multiple_of(kj * TK, TK), TK, c, False), init)
            c = k_step(qs, tq, c, True)
            sub = lax.broadcasted_iota(jnp.int32, (LANES, tq), 0)
            for pr in range(hb // 2):
                (m0, l0, a0), (m1, l1, a1) = c[2 * pr], c[2 * pr + 1]
                o_ref[pl.ds(qs, tq), pr * LANES:(pr + 1) * LANES] = jnp.where(sub < V_DIM, a0 / l0, a1 / l1).T.astype(BF16)
                lse_ref[2 * pr, :, pl.ds(qs, tq)] = m0 * SM_SCALE + jnp.log(l0)
                lse_ref[2 * pr + 1, :, pl.ds(qs, tq)] = m1 * SM_SCALE + jnp.log(l1)

        def whole_q_block(qi, carry):
            q_block(pl.multiple_of(qi * TQ, TQ), TQ, qi)
            return carry

        lax.fori_loop(0, nq, whole_q_block, 0)
        if tail:
            q_block(nq * TQ, tail, nq)

    blk = lambda w: pl.BlockSpec((lp, w), lambda b, g: (b, g))
    return _carrying_call(
        body, ride, (q, k, v), name=name, grid=(nb, N_HEADS // hb),
        in_specs=[blk(hb * HEAD_SLOT), blk(hb * HEAD_SLOT), blk(hb * V_DIM)],
        out_specs=[blk(hb * V_DIM), pl.BlockSpec((hb, 1, lp), lambda b, g: (g, 0, b))],
        out_shape=[jax.ShapeDtypeStruct((t, D_MODEL), BF16), jax.ShapeDtypeStruct((N_HEADS, 1, t), F32)],
        scratch_shapes=[pltpu.VMEM((hb // 2, LANES, lp), BF16)])


def _pool_band_fwd(i, tm, lp, nb):
    r = lax.broadcasted_iota(jnp.int32, (tm, POOL_HALO + tm), 0)
    e = lax.broadcasted_iota(jnp.int32, (tm, POOL_HALO + tm), 1)
    diff = r + POOL_HALO - e
    pos = _seq_pos(i * tm + lax.broadcasted_iota(jnp.int32, (tm, 1), 0), lp, nb)
    out = []
    for w in POOL_WINDOWS:
        cnt = jnp.minimum(pos + 1, w)
        band = jnp.where((diff >= 0) & (diff < cnt), 1.0, 0.0).astype(BF16)
        out.append((band, cnt.astype(F32)))
    return out


def merge_fwd(h, z, o, pw, ps, wpa, wpb, wo, *, tm, lp, nb, name, ride=None):
    t = h.shape[0]
    hb = tm // POOL_HALO

    def body(h_ref, u_ref, uprev_ref, ga_ref, gb_ref, o_ref, pw_ref, ps_ref, wpa_ref, wpb_ref, wo_ref,
             h1_ref, pooled_ref, a_ref, pa_ref, pb_ref, mg_ref):
        i = pl.program_id(0)
        u = u_ref[...]
        uext = jnp.concatenate([uprev_ref[...], u], axis=0).astype(BF16)
        pooled, ys = [], []
        for g, (band, cnt) in enumerate(_pool_band_fwd(i, tm, lp, nb)):
            gs = slice(g * POOL_GROUP, (g + 1) * POOL_GROUP)
            pg = (_dot(band, uext[:, gs]) / cnt - u[:, gs]).astype(BF16)
            pooled.append(pg)
            ys.append(_dot(pg, pw_ref[g]))
        pooled_ref[...] = jnp.concatenate(pooled, axis=1)
        a = (jnp.concatenate(ys, axis=1) * ps_ref[...]).astype(BF16)
        a_ref[...] = a
        pa = _dot(a, wpa_ref[...])
        pb = _dot(o_ref[...], wpb_ref[...])
        pa_ref[...] = pa.astype(BF16)
        pb_ref[...] = pb.astype(BF16)
        mg = (_sigmoid(ga_ref[...]) * pa + _sigmoid(gb_ref[...]) * pb).astype(BF16)
        mg_ref[...] = mg
        h1_ref[...] = h_ref[...] + _dot(mg, wo_ref[...])

    halo = pl.BlockSpec((POOL_HALO, POOL_WIDTH), lambda i: (jnp.maximum(i * hb - 1, 0), 0))
    return _carrying_call(
        body, ride, (h, z, z, z, z, o, pw, ps, wpa, wpb, wo), name=name, grid=(t // tm,),
        in_specs=[_rows(tm, D_MODEL), _rows(tm, POOL_WIDTH), halo, _rows(tm, D_MODEL, 1), _rows(tm, D_MODEL, 2), _rows(tm, D_MODEL),
                  _whole((4, POOL_GROUP, POOL_GROUP)), _whole((1, POOL_WIDTH)), _whole((POOL_WIDTH, D_MODEL)),
                  _whole((D_MODEL, D_MODEL)), _whole((D_MODEL, D_MODEL))],
        out_specs=[_rows(tm, D_MODEL), _rows(tm, POOL_WIDTH), _rows(tm, POOL_WIDTH), _rows(tm, D_MODEL), _rows(tm, D_MODEL),
                   _rows(tm, D_MODEL)],
        out_shape=[jax.ShapeDtypeStruct((t, D_MODEL), F32), jax.ShapeDtypeStruct((t, POOL_WIDTH), BF16),
                   jax.ShapeDtypeStruct((t, POOL_WIDTH), BF16), jax.ShapeDtypeStruct((t, D_MODEL), BF16),
                   jax.ShapeDtypeStruct((t, D_MODEL), BF16), jax.ShapeDtypeStruct((t, D_MODEL), BF16)])


def ffn_fwd(h1, g, wgt, wut, wd, *, tm, name):
    t = h1.shape[0]

    def body(h_ref, g_ref, wgt_ref, wut_ref, wd_ref, h2_ref, gt_ref, up_ref):
        h = h_ref[...]
        xhat, _ = _rms(h)
        hn = (xhat * g_ref[...]).astype(BF16)
        gt = _dot_nt(hn, wgt_ref[...])
        up = _dot_nt(hn, wut_ref[...])
        gt_ref[...] = gt.astype(BF16)
        up_ref[...] = up.astype(BF16)
        act = (gt * _sigmoid(gt) * up).astype(BF16)
        h2_ref[...] = h + _dot(act, wd_ref[...])

    return pl.pallas_call(
        body, name=name, grid=(t // tm,),
        in_specs=[_rows(tm, D_MODEL), _whole((1, D_MODEL)), _whole((D_FF, D_MODEL)), _whole((D_FF, D_MODEL)), _whole((D_FF, D_MODEL))],
        out_specs=[_rows(tm, D_MODEL), _rows(tm, D_FF), _rows(tm, D_FF)],
        out_shape=[jax.ShapeDtypeStruct((t, D_MODEL), F32), jax.ShapeDtypeStruct((t, D_FF), BF16), jax.ShapeDtypeStruct((t, D_FF), BF16)],
        compiler_params=_cparams(),
    )(h1, g, wgt, wut, wd)


def loss_head(h, g, target, *, tm, lp, nb, seq, name):
    t = h.shape[0]
    nt = t // tm

    def body(h_ref, g_ref, t_ref, loss_ref, dh_ref, dg_ref):
        i = pl.program_id(0)
        pos = _seq_pos(i * tm + lax.broadcasted_iota(jnp.int32, (tm, 1), 0), lp, nb)
        real = (pos >= N_META) & (pos < N_META + seq)
        xhat, r = _rms(h_ref[...])
        gg = g_ref[...]
        err = jnp.where(real, xhat * gg - t_ref[...], 0.0)
        loss_ref[...] = jnp.full((8, LANES), 0.5 * jnp.sum(err * err) / D_MODEL, F32)
        dx, dg = _rms_bwd(err * (1.0 / D_MODEL), xhat, r, gg)
        dh_ref[...] = dx

        @pl.when(i == 0)
        def _():
            dg_ref[...] = jnp.zeros_like(dg_ref)

        dg_ref[...] += dg

    return pl.pallas_call(
        body, name=name, grid=(nt,),
        in_specs=[_rows(tm, D_MODEL), _whole((1, D_MODEL)), _rows(tm, D_MODEL)],
        out_specs=[pl.BlockSpec((8, LANES), lambda i: (i, 0)), _rows(tm, D_MODEL), _acc((1, D_MODEL))],
        out_shape=[jax.ShapeDtypeStruct((nt * 8, LANES), F32), jax.ShapeDtypeStruct((t, D_MODEL), F32),
                   jax.ShapeDtypeStruct((1, D_MODEL), F32)],
        compiler_params=_cparams(),
    )(h, g, target)


def wgrad(x, ys, chunk_fn, chunk_shape, *, tm, name):
    t, m = x.shape

    def body(x_ref, *refs):
        y_refs, o_ref, accs = refs[:len(ys)], refs[len(ys)], refs[len(ys) + 1:]
        i = pl.program_id(0)

        @pl.when(i == 0)
        def _():
            for acc in accs:
                acc[...] = jnp.zeros_like(acc)

        xb = x_ref[...].astype(BF16)
        for y_ref, acc in zip(y_refs, accs):
            acc[...] += _dot_tn(xb, y_ref[...].astype(BF16))

        @pl.when(i == t // tm - 1)
        def _():
            for p, chunk in enumerate(chunk_fn(*accs)):
                o_ref[p % 2, p // 2] = chunk.astype(BF16)

    out = (2, N_DEV // 2) + tuple(chunk_shape)
    return pl.pallas_call(
        body, name=name, grid=(t // tm,),
        in_specs=[_rows(tm, m)] + [_rows(tm, y.shape[1]) for y in ys], out_specs=_acc(out),
        out_shape=jax.ShapeDtypeStruct(out, BF16), scratch_shapes=[pltpu.VMEM((m, y.shape[1]), F32) for y in ys],
        compiler_params=_cparams(),
    )(x, *ys)


def ffn_bwd(dh2, h1, g, gt, up, wgt, wut, wd, *, tm, name):
    t = h1.shape[0]

    def body(dh2_ref, h_ref, g_ref, gt_ref, up_ref, wgt_ref, wut_ref, wd_ref, dh1_ref, hn_ref, act_ref, dgt_ref, dup_ref, dg_ref):
        dh2 = dh2_ref[...]
        dact = _dot_nt(dh2.astype(BF16), wd_ref[...])
        gt = gt_ref[...].astype(F32)
        up = up_ref[...].astype(F32)
        sg = _sigmoid(gt)
        silu = gt * sg
        act_ref[...] = (silu * up).astype(BF16)
        dgt = (dact * up * (sg * (1.0 + gt * (1.0 - sg)))).astype(BF16)
        dup = (dact * silu).astype(BF16)
        dgt_ref[...] = dgt
        dup_ref[...] = dup
        dhn = _dot(dgt, wgt_ref[...]) + _dot(dup, wut_ref[...])
        xhat, r = _rms(h_ref[...])
        gg = g_ref[...]
        hn_ref[...] = (xhat * gg).astype(BF16)
        dx, dg = _rms_bwd(dhn, xhat, r, gg)
        dh1_ref[...] = dh2 + dx

        @pl.when(pl.program_id(0) == 0)
        def _():
            dg_ref[...] = jnp.zeros_like(dg_ref)

        dg_ref[...] += dg

    return pl.pallas_call(
        body, name=name, grid=(t // tm,),
        in_specs=[_rows(tm, D_MODEL), _rows(tm, D_MODEL), _whole((1, D_MODEL)), _rows(tm, D_FF), _rows(tm, D_FF),
                  _whole((D_FF, D_MODEL)), _whole((D_FF, D_MODEL)), _whole((D_FF, D_MODEL))],
        out_specs=[_rows(tm, D_MODEL), _rows(tm, D_MODEL), _rows(tm, D_FF), _rows(tm, D_FF), _rows(tm, D_FF), _acc((1, D_MODEL))],
        out_shape=[jax.ShapeDtypeStruct((t, D_MODEL), F32), jax.ShapeDtypeStruct((t, D_MODEL), BF16),
                   jax.ShapeDtypeStruct((t, D_FF), BF16), jax.ShapeDtypeStruct((t, D_FF), BF16),
                   jax.ShapeDtypeStruct((t, D_FF), BF16), jax.ShapeDtypeStruct((1, D_MODEL), F32)],
        compiler_params=_cparams(),
    )(dh2, h1, g, gt, up, wgt, wut, wd)


def merge_bwd(dh1, z, pa, pb, pooled, pw, ps, wpa, wpb, wo, *, tm, name):
    t = dh1.shape[0]

    def body(dh1_ref, ga_ref, gb_ref, pa_ref, pb_ref, pooled_ref, pw_ref, ps_ref, wpa_ref, wpb_ref, wo_ref,
             dga_ref, dgb_ref, dpa_ref, dpb_ref, do_ref, dpool_ref, dps_ref, dpw_ref):
        dmg = _dot_nt(dh1_ref[...].astype(BF16), wo_ref[...])
        sa = _sigmoid(ga_ref[...])
        sb = _sigmoid(gb_ref[...])
        dga_ref[...] = (dmg * pa_ref[...].astype(F32) * sa * (1.0 - sa)).astype(BF16)
        dgb_ref[...] = (dmg * pb_ref[...].astype(F32) * sb * (1.0 - sb)).astype(BF16)
        dpa = (dmg * sa).astype(BF16)
        dpb = (dmg * sb).astype(BF16)
        dpa_ref[...] = dpa
        dpb_ref[...] = dpb
        do_ref[...] = _dot_nt(dpb, wpb_ref[...]).astype(BF16)
        da = _dot_nt(dpa, wpa_ref[...])
        pooled = pooled_ref[...]
        ps = ps_ref[...]

        @pl.when(pl.program_id(0) == 0)
        def _():
            dps_ref[...] = jnp.zeros_like(dps_ref)
            dpw_ref[...] = jnp.zeros_like(dpw_ref)

        dps, dpool = [], []
        for g in range(len(POOL_WINDOWS)):
            gs = slice(g * POOL_GROUP, (g + 1) * POOL_GROUP)
            y = _dot(pooled[:, gs], pw_ref[g])
            dps.append(jnp.sum(da[:, gs] * y, axis=0, keepdims=True))
            dy = (da[:, gs] * ps[:, gs]).astype(BF16)
            dpool.append(_dot_nt(dy, pw_ref[g]))
            dpw_ref[g] += _dot_tn(pooled[:, gs], dy)
        dps_ref[...] += jnp.concatenate(dps, axis=1)
        dpool_ref[...] = jnp.concatenate(dpool, axis=1)

    return pl.pallas_call(
        body, name=name, grid=(t // tm,),
        in_specs=[_rows(tm, D_MODEL), _rows(tm, D_MODEL, 1), _rows(tm, D_MODEL, 2), _rows(tm, D_MODEL), _rows(tm, D_MODEL),
                  _rows(tm, POOL_WIDTH), _whole((4, POOL_GROUP, POOL_GROUP)),
                  _whole((1, POOL_WIDTH)), _whole((POOL_WIDTH, D_MODEL)), _whole((D_MODEL, D_MODEL)), _whole((D_MODEL, D_MODEL))],
        out_specs=[_rows(tm, D_MODEL), _rows(tm, D_MODEL), _rows(tm, D_MODEL), _rows(tm, D_MODEL), _rows(tm, D_MODEL),
                   _rows(tm, POOL_WIDTH), _acc((1, POOL_WIDTH)), _acc((4, POOL_GROUP, POOL_GROUP))],
        out_shape=[jax.ShapeDtypeStruct((t, D_MODEL), BF16)] * 5
        + [jax.ShapeDtypeStruct((t, POOL_WIDTH), F32), jax.ShapeDtypeStruct((1, POOL_WIDTH), F32),
           jax.ShapeDtypeStruct((4, POOL_GROUP, POOL_GROUP), F32)],
        compiler_params=_cparams(),
    )(dh1, z, z, pa, pb, pooled, pw, ps, wpa, wpb, wo)


def attn_bwd(q, k, v, o, do, lse, *, nb, lp, hb, name, ride=None):
    t = q.shape[0]
    nq, tail = lp // TQ, lp % TQ
    assert tail % LANES == 0

    def body(q_ref, k_ref, v_ref, o_ref, do_ref, lse_ref, dq_ref, dk_ref, dv_ref, kt, doh, lse_row, delta_row, dqt):
        lane = lax.broadcasted_iota(jnp.int32, (lp, LANES), 1)
        first = lane < V_DIM
        sub = lax.broadcasted_iota(jnp.int32, (LANES, lp), 0)
        for pr in range(hb // 2):
            ls = slice(pr * LANES, (pr + 1) * LANES)
            do = do_ref[:, ls]
            doh[2 * pr] = jnp.where(first, do, jnp.zeros_like(do))
            doh[2 * pr + 1] = jnp.where(first, jnp.zeros_like(do), do)
            prod_t = (do.astype(F32) * o_ref[:, ls].astype(F32)).T
            delta_row[2 * pr] = jnp.sum(jnp.where(sub < V_DIM, prod_t, 0.0), axis=0, keepdims=True)
            delta_row[2 * pr + 1] = jnp.sum(jnp.where(sub < V_DIM, 0.0, prod_t), axis=0, keepdims=True)
        for hd in range(hb):
            lse_row[hd] = lse_ref[hd] * LOG2E
            kt[hd] = k_ref[:, hd * HEAD_SLOT:(hd + 1) * HEAD_SLOT].T
        dqt[...] = jnp.zeros(dqt.shape, F32)
        heads = range(hb)
        hss = [slice(hd * HEAD_SLOT, (hd + 1) * HEAD_SLOT) for hd in heads]

        def k_block(ks, tk, next_q):
            keep = lax.broadcasted_iota(jnp.int32, (tk, tk), 0) <= lax.broadcasted_iota(jnp.int32, (tk, tk), 1)

            def q_step(qs, tq, c, masked):
                qhs = [q_ref[pl.ds(qs, tq), hss[hd]] for hd in heads]
                dos = [doh[hd, pl.ds(qs, tq), :] for hd in heads]
                sts = [_dot_nt(k_ref[pl.ds(ks, tk), hss[hd]], qhs[hd]) for hd in heads]
                dpts = [_dot_nt(v_ref[pl.ds(ks, tk), (hd // 2) * LANES:(hd // 2 + 1) * LANES], dos[hd]) for hd in heads]
                pts, dsts = [], []
                for hd in heads:
                    st = jnp.where(keep, sts[hd], MASK_VALUE) if masked else sts[hd]
                    pt = jnp.exp2(st * EXP2_SCALE - lse_row[hd, :, pl.ds(qs, tq)])
                    dsts.append((pt * (dpts[hd] - delta_row[hd, :, pl.ds(qs, tq)])).astype(BF16))
                    pts.append(pt.astype(BF16))
                dvs = [_dot(pts[hd], dos[hd]) for hd in heads]
                dks = [_dot(dsts[hd], qhs[hd]) for hd in heads]
                dqs = [_dot(kt[hd, :, pl.ds(ks, tk)], dsts[hd]) for hd in heads]
                for hd in heads:
                    dqt[hd, :, pl.ds(qs, tq)] += dqs[hd]
                return tuple((c[hd][0] + dks[hd], c[hd][1] + dvs[hd]) for hd in heads)

            zero = jnp.zeros((tk, LANES), F32)
            c = q_step(ks, tk, tuple((zero, zero) for _ in heads), True)
            if next_q is not None:
                c = lax.fori_loop(next_q, nq, lambda qi, c: q_step(pl.multiple_of(qi * TQ, TQ), TQ, c, False), c)
                if tail:
                    c = q_step(nq * TQ, tail, c, False)
            for hd in heads:
                dk_ref[pl.ds(ks, tk), hss[hd]] = c[hd][0] * SM_SCALE
            for pr in range(hb // 2):
                dv_ref[pl.ds(ks, tk), pr * LANES:(pr + 1) * LANES] = c[2 * pr][1] + c[2 * pr + 1][1]

        def whole_k_block(kj, carry):
            k_block(pl.multiple_of(kj * TK, TK), TK, kj + 1)
            return carry

        lax.fori_loop(0, nq, whole_k_block, 0)
        if tail:
            k_block(nq * TQ, tail, None)
        for hd in range(hb):
            dq_ref[:, hd * HEAD_SLOT:(hd + 1) * HEAD_SLOT] = dqt[hd].T * SM_SCALE

    blk = lambda w: pl.BlockSpec((lp, w), lambda b, g: (b, g))
    return _carrying_call(
        body, ride, (q, k, v, o, do, lse), name=name, grid=(nb, N_HEADS // hb),
        in_specs=[blk(hb * HEAD_SLOT), blk(hb * HEAD_SLOT), blk(hb * V_DIM), blk(hb * V_DIM), blk(hb * V_DIM),
                  pl.BlockSpec((hb, 1, lp), lambda b, g: (g, 0, b))],
        out_specs=[blk(hb * HEAD_SLOT), blk(hb * HEAD_SLOT), blk(hb * V_DIM)],
        out_shape=[jax.ShapeDtypeStruct((t, QK_WIDTH), F32), jax.ShapeDtypeStruct((t, QK_WIDTH), F32),
                   jax.ShapeDtypeStruct((t, D_MODEL), F32)],
        scratch_shapes=[pltpu.VMEM((hb, HEAD_SLOT, lp), BF16), pltpu.VMEM((hb, lp, LANES), BF16), pltpu.VMEM((hb, 1, lp), F32),
                        pltpu.VMEM((hb, 1, lp), F32), pltpu.VMEM((hb, HEAD_SLOT, lp), F32)])


def in_proj_bwd(dh1, h, g_mix, z, dq, dk, dv, dga, dgb, dpool, win, gq, gkv, wuq, wuk, wuv, rope, *, tm, lp, nb, name):
    t = h.shape[0]
    hb = tm // POOL_HALO
    last_halo = t // POOL_HALO - 1

    def body(dh1_ref, h_ref, g_ref, zcq_ref, zckv_ref, dq_ref, dk_ref, dv_ref, dga_ref, dgb_ref, dpool_ref, dnext_ref,
             win_ref, gq_ref, gkv_ref, wuq_ref, wuk_ref, wuv_ref, rope_ref,
             dh_ref, hn_ref, dz_ref, cqn_ref, ckvn_ref, dqb_ref, dkb_ref, dvb_ref, dg_ref, dgq_ref, dgkv_ref):
        i = pl.program_id(0)
        rope_t = rope_ref[...]
        dqb = _rope_bwd(dq_ref[...], *_rope_tables(rope_t, N_HEADS)).astype(BF16)
        dqb_ref[...] = dqb
        xq, rq = _rms(zcq_ref[...])
        gq_v = gq_ref[...]
        cqn_ref[...] = (xq * gq_v).astype(BF16)
        dcq, dgq = _rms_bwd(_dot_nt(dqb, wuq_ref[...]), xq, rq, gq_v)
        dk = dk_ref[...]
        dkb = dk.astype(BF16)
        dvb = dv_ref[...].astype(BF16)
        dkb_ref[...] = dkb
        dvb_ref[...] = dvb
        xkv, rkv = _rms(zckv_ref[...])
        gkv_v = gkv_ref[...]
        ckvn_ref[...] = (xkv * gkv_v).astype(BF16)
        dckv, dgkv = _rms_bwd(_dot_nt(dkb, wuk_ref[...]) + _dot_nt(dvb, wuv_ref[...]), xkv, rkv, gkv_v)
        dks = dk[:, :HEAD_SLOT]
        for hd in range(1, N_HEADS):
            dks = dks + dk[:, hd * HEAD_SLOT:(hd + 1) * HEAD_SLOT]
        dzk = _rope_bwd(dks, *_rope_tables(rope_t, 1))
        dp_cur = dpool_ref[...]
        dp_ext = jnp.concatenate([dp_cur, dnext_ref[...]], axis=0)
        r = lax.broadcasted_iota(jnp.int32, (tm, tm + POOL_HALO), 0)
        e = lax.broadcasted_iota(jnp.int32, (tm, tm + POOL_HALO), 1)
        gt_col = i * tm + lax.broadcasted_iota(jnp.int32, (1, tm + POOL_HALO), 1)
        pos_col = _seq_pos(gt_col, lp, nb)
        gt_row = i * tm + lax.broadcasted_iota(jnp.int32, (tm + POOL_HALO, 1), 0)
        pos_row = _seq_pos(gt_row, lp, nb)
        dus = []
        for g, w in enumerate(POOL_WINDOWS):
            gs = slice(g * POOL_GROUP, (g + 1) * POOL_GROUP)
            band = jnp.where((e - r >= 0) & (e - r < jnp.minimum(pos_col + 1, w)) & (gt_col < t), 1.0, 0.0).astype(BF16)
            scaled = jnp.where(gt_row < t, dp_ext[:, gs] / jnp.minimum(pos_row + 1, w).astype(F32), 0.0).astype(BF16)
            dus.append(_dot(band, scaled) - dp_cur[:, gs])
        dz = jnp.concatenate(dus + [dcq, dckv, dzk], axis=1).astype(BF16)
        dz = jnp.concatenate([dz, dga_ref[...], dgb_ref[...]], axis=1)
        dz_ref[...] = dz
        xhat, rr = _rms(h_ref[...])
        gg = g_ref[...]
        hn_ref[...] = (xhat * gg).astype(BF16)
        dx, dg = _rms_bwd(_dot_nt(dz, win_ref[...]), xhat, rr, gg)
        dh_ref[...] = dh1_ref[...] + dx

        @pl.when(i == 0)
        def _():
            dg_ref[...] = jnp.zeros_like(dg_ref)
            dgq_ref[...] = jnp.zeros_like(dgq_ref)
            dgkv_ref[...] = jnp.zeros_like(dgkv_ref)

        dg_ref[...] += dg
        dgq_ref[...] += dgq
        dgkv_ref[...] += dgkv

    nxt = pl.BlockSpec((POOL_HALO, POOL_WIDTH), lambda i: (jnp.minimum((i + 1) * hb, last_halo), 0))
    return pl.pallas_call(
        body, name=name, grid=(t // tm,),
        in_specs=[_rows(tm, D_MODEL), _rows(tm, D_MODEL), _whole((1, D_MODEL)), _rows(tm, Q_RANK, Z_CQ // Q_RANK),
                  _rows(tm, KV_RANK, Z_CKV // KV_RANK), _rows(tm, QK_WIDTH), _rows(tm, QK_WIDTH), _rows(tm, D_MODEL),
                  _rows(tm, D_MODEL), _rows(tm, D_MODEL), _rows(tm, POOL_WIDTH), nxt,
                  _whole((D_MODEL, DZ)), _whole((1, Q_RANK)), _whole((1, KV_RANK)), _whole((Q_RANK, QK_WIDTH)),
                  _whole((KV_RANK, QK_WIDTH)), _whole((KV_RANK, D_MODEL)), _rows(tm, 4 * LANES)],
        out_specs=[_rows(tm, D_MODEL), _rows(tm, D_MODEL), _rows(tm, DZ), _rows(tm, Q_RANK), _rows(tm, KV_RANK),
                   _rows(tm, QK_WIDTH), _rows(tm, QK_WIDTH), _rows(tm, D_MODEL),
                   _acc((1, D_MODEL)), _acc((1, Q_RANK)), _acc((1, KV_RANK))],
        out_shape=[jax.ShapeDtypeStruct((t, D_MODEL), F32), jax.ShapeDtypeStruct((t, D_MODEL), BF16),
                   jax.ShapeDtypeStruct((t, DZ), BF16), jax.ShapeDtypeStruct((t, Q_RANK), BF16),
                   jax.ShapeDtypeStruct((t, KV_RANK), BF16), jax.ShapeDtypeStruct((t, QK_WIDTH), BF16),
                   jax.ShapeDtypeStruct((t, QK_WIDTH), BF16), jax.ShapeDtypeStruct((t, D_MODEL), BF16),
                   jax.ShapeDtypeStruct((1, D_MODEL), F32), jax.ShapeDtypeStruct((1, Q_RANK), F32),
                   jax.ShapeDtypeStruct((1, KV_RANK), F32)],
        compiler_params=_cparams(),
    )(dh1, h, g_mix, z, z, dq, dk, dv, dga, dgb, dpool, dpool, win, gq, gkv, wuq, wuk, wuv, rope)


_MESH = pl.DeviceIdType.MESH


def _place():
    x, y, c = lax.axis_index("x"), lax.axis_index("y"), lax.axis_index("c")
    return x, y, c, 4 * x + 2 * y + c


def _peer(x, y, c, k):
    px, py, pc = (1 - x) if k & 4 else x, (1 - y) if k & 2 else y, (1 - c) if k & 1 else c
    return (px, py, pc), 4 * px + 2 * py + pc


ALL_PEERS = tuple(range(1, N_DEV))
CHIP_PEERS = (2, 4, 6)
N_CHIPS = N_DEV // 2


def _sem_scratch(n, m):
    return [pltpu.SemaphoreType.DMA((n, m)), pltpu.SemaphoreType.DMA((n, m)), pltpu.SemaphoreType.DMA((n,))]


class Exchange:
    def __init__(self, arrays, out_shapes, sem_cols, plan, aliased=False):
        self.arrays, self.out_shapes, self.plan = list(arrays), list(out_shapes), plan
        self.scratch = _sem_scratch(len(self.arrays), sem_cols)
        self.aliased = aliased

    def split(self, refs):
        n = len(self.arrays)
        return refs[:n], refs[n:2 * n], refs[2 * n:]

    def start(self, srcs, dsts, sems):
        local, sends, _ = self.plan(srcs, dsts, *sems)
        for cp in local + sends:
            cp.start()

    def wait(self, srcs, dsts, sems):
        local, sends, recvs = self.plan(srcs, dsts, *sems)
        for cp in recvs:
            cp.wait_recv()
        for cp in sends:
            cp.wait_send()
        for cp in local:
            cp.wait()

    def aliases(self, first_in, first_out):
        return {first_in + j: first_out + j for j in range(len(self.arrays))} if self.aliased else {}

    def run(self, name):
        def body(*refs):
            srcs, dsts, sems = self.split(refs)
            self.start(srcs, dsts, sems)
            self.wait(srcs, dsts, sems)

        n = len(self.arrays)
        return pl.pallas_call(body, name=name, in_specs=[_ANY] * n, out_specs=[_ANY] * n, out_shape=self.out_shapes,
                              scratch_shapes=self.scratch, input_output_aliases=self.aliases(0, 0))(*self.arrays)


def exchange(arrays, scatter, peers, by_chip=False):
    slots = N_CHIPS if by_chip else N_DEV

    def plan(srcs, dsts, send_sems, recv_sems, local_sems):
        x, y, c, me = _place()
        mine = 2 * x + y if by_chip else me
        local = [pltpu.make_async_copy(src.at[mine] if scatter else src, dst.at[mine], local_sems.at[j])
                 for j, (src, dst) in enumerate(zip(srcs, dsts))]
        sends, recvs = [], []
        for t, k in enumerate(peers):
            peer, pidx = _peer(x, y, c, k)
            theirs = 2 * peer[0] + peer[1] if by_chip else pidx
            for j, (src, dst) in enumerate(zip(srcs, dsts)):
                part = src.at[theirs] if scatter else src
                sems = dict(send_sem=send_sems.at[j, t], recv_sem=recv_sems.at[j, t], device_id=peer, device_id_type=_MESH)
                sends.append(pltpu.make_async_remote_copy(src_ref=part, dst_ref=dst.at[mine], **sems))
                recvs.append(pltpu.make_async_remote_copy(src_ref=part, dst_ref=dst.at[theirs], **sems))
        return local, sends, recvs

    shapes = [jax.ShapeDtypeStruct(a.shape if scatter else (slots,) + a.shape, a.dtype) for a in arrays]
    return Exchange(arrays, shapes, len(peers), plan)


def second_hop(gathered):
    def plan(srcs, dsts, send_sems, recv_sems, local_sems):
        x, y, c, me = _place()
        sibling, _ = _peer(x, y, c, 1)
        sends, recvs = [], []
        for t, k in enumerate(CHIP_PEERS):
            _, landed = _peer(x, y, c, k)
            _, coming = _peer(x, y, c, k ^ 1)
            for j, buf in enumerate(dsts):
                sems = dict(send_sem=send_sems.at[j, t], recv_sem=recv_sems.at[j, t], device_id=sibling, device_id_type=_MESH)
                sends.append(pltpu.make_async_remote_copy(src_ref=buf.at[landed], dst_ref=buf.at[landed], **sems))
                recvs.append(pltpu.make_async_remote_copy(src_ref=buf.at[coming], dst_ref=buf.at[coming], **sems))
        return [], sends, recvs

    shapes = [jax.ShapeDtypeStruct(a.shape, a.dtype) for a in gathered]
    return Exchange(gathered, shapes, len(CHIP_PEERS), plan, aliased=True)


FIRST_HOP_PEERS = (1,) + CHIP_PEERS


def _gather_two_level(arrays, name):
    n = len(arrays)

    def body(*refs):
        srcs, dsts, (send_sems, recv_sems, local_sems) = refs[:n], refs[n:2 * n], refs[2 * n:]
        x, y, c, me = _place()
        sibling, sidx = _peer(x, y, c, 1)

        def copy(j, sem, block, to, src=None):
            rows = dsts[j].at[block]
            return pltpu.make_async_remote_copy(src_ref=rows if src is None else src, dst_ref=rows, send_sem=send_sems.at[j, sem],
                                                recv_sem=recv_sems.at[j, sem], device_id=to, device_id_type=_MESH)

        local = [pltpu.make_async_copy(srcs[j], dsts[j].at[me], local_sems.at[j]) for j in range(n)]
        for cp in local:
            cp.start()
        first = [copy(j, 1 + t, me, _peer(x, y, c, k)[0], src=srcs[j]) for t, k in enumerate(CHIP_PEERS) for j in range(n)]
        first += [copy(j, 0, me, sibling, src=srcs[j]) for j in range(n)]
        for cp in first:
            cp.start()
        passed = []
        for t, k in enumerate(CHIP_PEERS):
            peer, pidx = _peer(x, y, c, k)
            for j in range(n):
                copy(j, 1 + t, pidx, peer).wait_recv()
                passed.append(copy(j, 4 + t, pidx, sibling))
                passed[-1].start()
        for j in range(n):
            copy(j, 0, sidx, sibling).wait_recv()
        for t, k in enumerate(CHIP_PEERS):
            _, pidx = _peer(x, y, c, k ^ 1)
            for j in range(n):
                copy(j, 4 + t, pidx, sibling).wait_recv()
        for cp in first + passed:
            cp.wait_send()
        for cp in local:
            cp.wait()

    shapes = [jax.ShapeDtypeStruct((N_DEV,) + a.shape, a.dtype) for a in arrays]
    return pl.pallas_call(body, name=name, in_specs=[_ANY] * n, out_specs=[_ANY] * n, out_shape=shapes,
                          scratch_shapes=_sem_scratch(n, 1 + 2 * len(CHIP_PEERS)))(*arrays)


def _to_sibling(arrays, name):
    n = len(arrays)

    def body(*refs):
        srcs, dsts, (send_sems, recv_sems) = refs[:n], refs[n:2 * n], refs[2 * n:]
        x, y, c, _ = _place()
        sibling, _ = _peer(x, y, c, 1)
        copies = [pltpu.make_async_remote_copy(src_ref=srcs[j].at[1 - c], dst_ref=dsts[j], send_sem=send_sems.at[j],
                                               recv_sem=recv_sems.at[j], device_id=sibling, device_id_type=_MESH) for j in range(n)]
        for cp in copies:
            cp.start()
        for cp in copies:
            cp.wait()

    shapes = [jax.ShapeDtypeStruct(a.shape[1:], a.dtype) for a in arrays]
    return pl.pallas_call(body, name=name, in_specs=[_ANY] * n, out_specs=[_ANY] * n, out_shape=shapes,
                          scratch_shapes=[pltpu.SemaphoreType.DMA((n,)), pltpu.SemaphoreType.DMA((n,))])(*arrays)


def pair_add(own, theirs, core, *, name):
    _, ns, r, c = own.shape
    rb = _row_block(r, c)

    def body(core_ref, a_ref, b_ref, o_ref):
        o_ref[...] = (a_ref[...].astype(F32) + b_ref[...].astype(F32)).astype(o_ref.dtype)

    return pl.pallas_call(
        body, name=name,
        grid_spec=pltpu.PrefetchScalarGridSpec(
            num_scalar_prefetch=1, grid=(ns, r // rb),
            in_specs=[pl.BlockSpec((None, None, rb, c), lambda i, j, core_ref: (core_ref[0], i, j, 0)),
                      pl.BlockSpec((None, rb, c), lambda i, j, core_ref: (i, j, 0))],
            out_specs=pl.BlockSpec((None, rb, c), lambda i, j, core_ref: (i, j, 0))),
        out_shape=jax.ShapeDtypeStruct((ns, r, c), own.dtype), compiler_params=_cparams(),
    )(core, own, theirs)


ADAMW_BLOCK_BYTES = 1 << 20


def _row_block(r, c):
    for rb in range(r, 0, -1):
        if r % rb == 0 and (rb % 16 == 0 or rb == r) and rb * c * 4 <= ADAMW_BLOCK_BYTES:
            return rb
    return r


def adamw(w, m, v, parts, *, name):
    depth, r, c = w.shape
    n_parts = parts[0].shape[0]
    rb = _row_block(r, c)

    def body(w_ref, m_ref, v_ref, *refs):
        p_refs, (g_ref, d_ref, nm_ref, nv_ref) = refs[:depth], refs[depth:]

        def total(p_ref):
            g = p_ref[0].astype(F32)
            for j in range(1, n_parts):
                g = g + p_ref[j].astype(F32)
            return g

        g = total(p_refs[0])
        for l in range(1, depth):
            g = jnp.where(pl.program_id(0) == l, total(p_refs[l]), g)
        g_ref[...] = g
        m_new = ADAM_B1 * m_ref[...] + (1.0 - ADAM_B1) * g
        v_new = ADAM_B2 * v_ref[...] + (1.0 - ADAM_B2) * (g * g)
        m_hat = m_new / (1.0 - ADAM_B1 ** ADAM_STEP)
        v_hat = v_new / (1.0 - ADAM_B2 ** ADAM_STEP)
        d_ref[...] = -ADAM_LR * (m_hat / (jnp.sqrt(v_hat) + ADAM_EPS) + ADAM_WD * w_ref[...])
        nm_ref[...] = m_new
        nv_ref[...] = v_new

    wblk = pl.BlockSpec((None, rb, c), lambda l, i: (l, i, 0))
    pblk = pl.BlockSpec((n_parts, rb, c), lambda l, i: (0, i, 0))
    return pl.pallas_call(
        body, name=name, grid=(depth, r // rb),
        in_specs=[wblk, wblk, wblk] + [pblk] * depth, out_specs=[wblk] * 4,
        out_shape=[jax.ShapeDtypeStruct((depth, r, c), F32)] * 4, compiler_params=_cparams(),
    )(w, m, v, *parts)


BIG = (("w_in", 2), ("w_uq", 2), ("w_ukv", 2), ("w_pa", 2), ("w_pb", 1), ("w_o", 1), ("w_gate", 2), ("w_up", 2), ("w_down", 1))
SMALL = ("norm_mix_g", "pool_w", "pool_scale", "q_norm_g", "kv_norm_g", "norm_ffn_g", "final_norm_g")
WEIGHTS = ("meta_tokens", "norm_mix_g", "w_in", "pool_w", "pool_scale", "q_norm_g", "kv_norm_g", "w_uq", "w_ukv", "w_pa", "w_pb",
           "w_o", "norm_ffn_g", "w_gate", "w_up", "w_down", "final_norm_g")
HEAD_QK = QK_NOPE + QK_ROPE
KR_END = Z_KR + QK_ROPE


def _cat_cols(parts):
    return [jnp.concatenate(parts, axis=1)]


def _cat_rows(parts):
    return [jnp.concatenate(parts, axis=0)]


def _arr_w_in(parts):
    full = jnp.concatenate(parts, axis=1)
    zc = lambda n: jnp.zeros((full.shape[0], n), full.dtype)
    return [jnp.concatenate([full[:, :Z_KR], zc(QK_NOPE), full[:, Z_KR:KR_END], zc(LANES - HEAD_QK), full[:, KR_END:]], axis=1)]


def _arr_w_uq(parts):
    full = jnp.concatenate(parts, axis=1)
    z = jnp.zeros((full.shape[0], HEAD_SLOT - HEAD_QK), full.dtype)
    pieces = []
    for hd in range(N_HEADS):
        pieces += [full[:, hd * HEAD_QK:(hd + 1) * HEAD_QK], z]
    return [jnp.concatenate(pieces, axis=1)]


def _arr_w_ukv(parts):
    full = jnp.concatenate(parts, axis=1)
    z = jnp.zeros((full.shape[0], HEAD_SLOT - QK_NOPE), full.dtype)
    wide = QK_NOPE + V_DIM
    k, v = [], []
    for hd in range(N_HEADS):
        k += [full[:, hd * wide:hd * wide + QK_NOPE], z]
        v.append(full[:, hd * wide + QK_NOPE:(hd + 1) * wide])
    return [jnp.concatenate(k, axis=1), jnp.concatenate(v, axis=1)]


def arrange(g, fn, out_shapes, name):
    def body(g_ref, *o_refs):
        for o_ref, val in zip(o_refs, fn([g_ref[p] for p in range(N_DEV)])):
            o_ref[...] = val

    return pl.pallas_call(
        body, name=name, grid=(1,),
        in_specs=[pl.BlockSpec(g.shape, lambda i: (0, 0, 0))],
        out_specs=[pl.BlockSpec(s, lambda i: (0, 0)) for s in out_shapes],
        out_shape=[jax.ShapeDtypeStruct(s, g.dtype) for s in out_shapes], compiler_params=_cparams(),
    )(g)


def _arranged_ranges(lo, hi):
    out = []
    for a, b, shift in ((0, Z_KR, 0), (Z_KR, KR_END, QK_NOPE), (KR_END, D_IN, LANES - QK_ROPE)):
        s, e = max(lo, a), min(hi, b)
        if s < e:
            out.append((s + shift, e + shift))
    return out


def _chunks_w_in(acc):
    cs = D_IN // N_DEV
    return [jnp.concatenate([acc[:, a:b] for a, b in _arranged_ranges(p * cs, (p + 1) * cs)], axis=1) for p in range(N_DEV)]


def _chunks_w_uq(acc):
    per = N_HEADS // N_DEV
    return [jnp.concatenate([acc[:, hd * HEAD_SLOT:hd * HEAD_SLOT + HEAD_QK] for hd in range(p * per, (p + 1) * per)], axis=1)
            for p in range(N_DEV)]


def _chunks_w_ukv(acc_k, acc_v):
    per = N_HEADS // N_DEV
    out = []
    for p in range(N_DEV):
        pieces = []
        for hd in range(p * per, (p + 1) * per):
            pieces += [acc_k[:, hd * HEAD_SLOT:hd * HEAD_SLOT + QK_NOPE], acc_v[:, hd * V_DIM:(hd + 1) * V_DIM]]
        out.append(jnp.concatenate(pieces, axis=1))
    return out


def _chunks_cols(acc):
    cs = acc.shape[1] // N_DEV
    return [acc[:, p * cs:(p + 1) * cs] for p in range(N_DEV)]


def _chunks_rows(acc):
    rs = acc.shape[0] // N_DEV
    return [acc[p * rs:(p + 1) * rs, :] for p in range(N_DEV)]


def _chunks_cols_transposed(acc):
    at = acc[...].T
    rs = at.shape[0] // N_DEV
    return [at[p * rs:(p + 1) * rs, :] for p in range(N_DEV)]


def _pack(parts, row_multiple):
    flat = jnp.concatenate([p.reshape(-1) for p in parts])
    return jnp.pad(flat, (0, -flat.shape[0] % (row_multiple * LANES))).reshape(-1, LANES)


def _unpack(packed, shapes):
    flat, out, off = packed.reshape(-1), [], 0
    for s in shapes:
        n = 1
        for d in s:
            n *= d
        out.append(flat[off:off + n].reshape(s))
        off += n
    return out


def _rope_table(lp, nb):
    inv = 1.0 / (ROPE_THETA ** (jnp.arange(0, QK_ROPE, 2, dtype=F32) / QK_ROPE))
    ang = jnp.arange(lp, dtype=F32)[:, None] * inv[None, :]
    cos, sin = jnp.cos(ang), jnp.sin(ang)
    z = lambda n: jnp.zeros((lp, n), F32)
    tail = LANES - QK_NOPE - QK_ROPE
    c = jnp.concatenate([jnp.ones((lp, QK_NOPE), F32), cos, cos, z(tail)], axis=1)
    cr = jnp.concatenate([z(QK_NOPE), cos, cos, z(tail)], axis=1)
    s1 = jnp.concatenate([z(QK_NOPE), -sin, z(HALF_ROPE), z(tail)], axis=1)
    s2 = jnp.concatenate([z(QK_NOPE), z(HALF_ROPE), sin, z(tail)], axis=1)
    return jnp.tile(jnp.concatenate([c, cr, s1, s2], axis=1), (nb, 1))


MIX = ("w_in", "w_uq", "w_ukv", "w_pa", "w_pb", "w_o")
FFN = ("w_gate", "w_up", "w_down")
TRANSPOSED = ("w_gate", "w_up")
ARRANGERS = {
    "w_in": (_arr_w_in, (("win", (D_MODEL, DZ)),)), "w_uq": (_arr_w_uq, (("wuq", (Q_RANK, QK_WIDTH)),)),
    "w_ukv": (_arr_w_ukv, (("wuk", (KV_RANK, QK_WIDTH)), ("wuv", (KV_RANK, D_MODEL)))),
    "w_pa": (_cat_cols, (("wpa", (POOL_WIDTH, D_MODEL)),)), "w_pb": (_cat_rows, (("wpb", (D_MODEL, D_MODEL)),)),
    "w_o": (_cat_rows, (("wo", (D_MODEL, D_MODEL)),)), "w_gate": (_cat_rows, (("wgt", (D_FF, D_MODEL)),)),
    "w_up": (_cat_rows, (("wut", (D_FF, D_MODEL)),)), "w_down": (_cat_rows, (("wd", (D_FF, D_MODEL)),)),
}


def _operands(gathered, names, l):
    p = {}
    for n in names:
        fn, outs = ARRANGERS[n]
        for (key, _), a in zip(outs, arrange(gathered[n], fn, [s for _, s in outs], f"arrange_{n}_{l}")):
            p[key] = a
    return p


def _small_operands(small, l):
    pw = small["pool_w"][l].astype(BF16)
    return dict(g_mix=small["norm_mix_g"][l][None], gq=small["q_norm_g"][l][None], gkv=small["kv_norm_g"][l][None],
                g_ffn=small["norm_ffn_g"][l][None], ps=small["pool_scale"][l][None], pw=pw)


class MeshComm:
    def __init__(self, w, meta_tokens):
        self.src = lambda n, l: w[n][l].astype(BF16)
        self.meta_tokens = meta_tokens
        self.core = lax.axis_index("c").astype(jnp.int32).reshape(1)
        self.rides = {0: [(n, 0) for n in FFN] + [(n, 1) for n in MIX], 1: [(n, 1) for n in FFN]}

    def first_weights(self):
        got = _gather_two_level([self.src(n, 0) for n in MIX] + [self.meta_tokens], "gather_mix_0")
        return dict(zip(MIX, got)), jnp.moveaxis(got[-1], 0, 1).reshape(N_META, D_MODEL)

    def first_hop(self, l):
        return exchange([self.src(n, layer) for n, layer in self.rides[l]], False, FIRST_HOP_PEERS)

    def second_hop(self, l, landed):
        return second_hop(landed)

    def carried(self, l, full, names, layer):
        return {n: full[self.rides[l].index((n, layer))] for n in names}

    def pair_sums(self, own, names, tag):
        theirs = _to_sibling(own, f"pair_grads_{tag}")
        return [pair_add(a, b, self.core, name=f"pair_add_{n}_{tag}") for n, a, b in zip(names, own, theirs)]

    def scatter(self, sums):
        return exchange(sums, True, CHIP_PEERS, by_chip=True)

    def scatter_now(self, sums, name):
        return self.scatter(sums).run(name)


HEADS_FWD, HEADS_BWD = 8, 4
TILE_ROWS, TILE_ROWS_BWD = 512, 256


def _tile(t, target):
    n = max(1, -(-t // (target + target // 8)))
    while t % n or (t // n) % 16:
        n += 1
    return t // n


def _wgrad_tile(t):
    return max(tm for tm in (2 * TQ, TQ, LANES) if t % tm == 0)


def _ffn_bwd_part(dh2, p, s, tag):
    d, ff = D_MODEL, D_FF // N_DEV
    t = dh2.shape[0]
    wg_ = lambda n, x, ys, fn, shape: wgrad(x, ys, fn, shape, tm=_wgrad_tile(t), name=f"wgrad_{n}_{tag}")
    dh1, hn2, act, dgt, dup, dg_ffn = ffn_bwd(dh2, s["h1"], p["g_ffn"], s["gt"], s["up"], p["wgt"], p["wut"], p["wd"],
                                              tm=_tile(t, TILE_ROWS_BWD), name=f"ffn_bwd_{tag}")
    chunks = [wg_("gate", hn2, [dgt], _chunks_cols_transposed, (ff, d)), wg_("up", hn2, [dup], _chunks_cols_transposed, (ff, d)),
              wg_("down", act, [dh2], _chunks_rows, (ff, d))]
    return dh1, chunks, dict(norm_ffn_g=dg_ffn[0])


def _mix_bwd_part(dh1, p, s, rope, nb, lp, tag, ride):
    d = D_MODEL
    t = dh1.shape[0]
    wg_ = lambda n, x, ys, fn, shape: wgrad(x, ys, fn, shape, tm=_wgrad_tile(t), name=f"wgrad_{n}_{tag}")
    dga, dgb, dpa, dpb, do, dpool, dps, dpw = merge_bwd(dh1, s["z"], s["pa"], s["pb"], s["pooled"], p["pw"], p["ps"],
                                                        p["wpa"], p["wpb"], p["wo"], tm=_tile(t, TILE_ROWS),
                                                        name=f"merge_bwd_{tag}")
    c_o = wg_("o", s["mg"], [dh1], _chunks_rows, (d // N_DEV, d))
    c_pa = wg_("pa", s["a"], [dpa], _chunks_cols, (POOL_WIDTH, d // N_DEV))
    c_pb = wg_("pb", s["o"], [dpb], _chunks_rows, (d // N_DEV, d))
    (dq, dk, dv), brought = attn_bwd(s["q"], s["k"], s["v"], s["o"], do, s["lse"], nb=nb, lp=lp, hb=HEADS_BWD,
                                     name=f"attn_bwd_{tag}", ride=ride)
    dh, hn, dz, cqn, ckvn, dqb, dkb, dvb, dg_mix, dgq, dgkv = in_proj_bwd(
        dh1, s["h"], p["g_mix"], s["z"], dq, dk, dv, dga, dgb, dpool, p["win"], p["gq"], p["gkv"], p["wuq"], p["wuk"], p["wuv"],
        rope, tm=_tile(t, TILE_ROWS_BWD), lp=lp, nb=nb, name=f"in_proj_bwd_{tag}")
    c_in = wg_("in", hn, [dz], _chunks_w_in, (d, D_IN // N_DEV))
    c_uq = wg_("uq", cqn, [dqb], _chunks_w_uq, (Q_RANK, N_HEADS * HEAD_QK // N_DEV))
    c_ukv = wg_("ukv", ckvn, [dkb, dvb], _chunks_w_ukv, (KV_RANK, N_HEADS * (QK_NOPE + V_DIM) // N_DEV))
    small = dict(pool_scale=dps[0], pool_w=dpw, norm_mix_g=dg_mix[0], q_norm_g=dgq[0], kv_norm_g=dgkv[0])
    return dh, [c_in, c_uq, c_ukv, c_pa, c_pb, c_o], small, brought


def train_step(x, loss_target, small, comm):
    nb, seq, d = x.shape
    lp = -(-(N_META + seq) // LANES) * LANES
    t = nb * lp
    assert nb <= 2 and DEPTH == 2
    tm = _tile(t, TILE_ROWS)
    rope = _rope_table(lp, nb)
    gathered, meta = comm.first_weights()
    pad = jnp.zeros((nb, lp - N_META - seq, d), F32)
    h = jnp.concatenate([jnp.broadcast_to(meta[None], (nb, N_META, d)), x, pad], axis=1).reshape(t, d)
    target = jnp.concatenate([jnp.zeros((nb, N_META, d), F32), loss_target, pad], axis=1).reshape(t, d)

    params, saved, full = [], [], {}
    for l in range(DEPTH):
        p = _small_operands(small, l)
        p.update(_operands(gathered if l == 0 else comm.carried(0, full[0], MIX, 1), MIX, l))
        z, q, k, v = in_proj_fwd(h, p["g_mix"], p["win"], p["gq"], p["gkv"], p["wuq"], p["wuk"], p["wuv"], rope, tm=tm,
                                 name=f"in_proj_fwd_{l}")
        (o, lse), landed = attn_fwd(q, k, v, nb=nb, lp=lp, hb=HEADS_FWD, name=f"attn_fwd_{l}", ride=comm.first_hop(l))
        (h1, pooled, a, pa, pb, mg), full[l] = merge_fwd(h, z, o, p["pw"], p["ps"], p["wpa"], p["wpb"], p["wo"], tm=tm, lp=lp,
                                                          nb=nb, name=f"merge_fwd_{l}", ride=comm.second_hop(l, landed))
        p.update(_operands(comm.carried(l, full[l], FFN, l), FFN, l))
        h2, gt, up = ffn_fwd(h1, p["g_ffn"], p["wgt"], p["wut"], p["wd"], tm=tm, name=f"ffn_fwd_{l}")
        params.append(p)
        saved.append(dict(h=h, z=z, q=q, k=k, v=v, o=o, lse=lse, h1=h1, pooled=pooled, a=a, pa=pa, pb=pb, mg=mg, gt=gt, up=up))
        h = h2
    parts, dh, dgf = loss_head(h, small["final_norm_g"][None], target, tm=tm, lp=lp, nb=nb, seq=seq, name="loss_head")
    loss = jnp.sum(parts[::8, 0])

    sums = {}
    dh, c_ffn1, small1 = _ffn_bwd_part(dh, params[1], saved[1], 1)
    s_ffn1 = comm.pair_sums(c_ffn1, FFN, "ffn_1")
    dh, c_mix1, sm, brought = _mix_bwd_part(dh, params[1], saved[1], rope, nb, lp, 1, comm.scatter(s_ffn1))
    small1.update(sm)
    sums.update({(n, 1): a for n, a in zip(FFN, brought)})
    s_mix1 = comm.pair_sums(c_mix1, MIX, "mix_1")
    dh, c_ffn0, small0 = _ffn_bwd_part(dh, params[0], saved[0], 0)
    s_ffn0 = comm.pair_sums(c_ffn0, FFN, "ffn_0")
    dh, c_mix0, sm, brought = _mix_bwd_part(dh, params[0], saved[0], rope, nb, lp, 0, comm.scatter(s_mix1 + s_ffn0))
    small0.update(sm)
    sums.update({(n, l): a for (n, l), a in zip([(n, 1) for n in MIX] + [(n, 0) for n in FFN], brought)})
    dh = dh.reshape(nb, lp, d)
    dmeta = jnp.sum(dh[:, :N_META], axis=0)
    meta_chunks = jnp.transpose(dmeta.reshape(N_META, N_CHIPS, 2, d // N_DEV), (2, 1, 0, 3)).astype(BF16)
    s_last = comm.pair_sums(c_mix0 + [meta_chunks], MIX + ("meta_tokens",), "mix_0")
    last = comm.scatter_now(s_last, "scatter_mix_0")
    sums.update({(n, 0): a for n, a in zip(MIX + ("meta_tokens",), last)})
    small_grads = {n: jnp.stack([small0[n], small1[n]]) for n in small0}
    small_grads["final_norm_g"] = dgf[0]
    return loss, dh[:, N_META:N_META + seq], sums, small_grads


def kernel(x, meta_tokens, norm_mix_g, w_in, pool_w, pool_scale, q_norm_g, kv_norm_g, w_uq, w_ukv, w_pa, w_pb, w_o, norm_ffn_g, w_gate, w_up, w_down, final_norm_g, loss_target, m_meta_tokens, m_norm_mix_g, m_w_in, m_pool_w, m_pool_scale, m_q_norm_g, m_kv_norm_g, m_w_uq, m_w_ukv, m_w_pa, m_w_pb, m_w_o, m_norm_ffn_g, m_w_gate, m_w_up, m_w_down, m_final_norm_g, v_meta_tokens, v_norm_mix_g, v_w_in, v_pool_w, v_pool_scale, v_q_norm_g, v_kv_norm_g, v_w_uq, v_w_ukv, v_w_pa, v_w_pb, v_w_o, v_norm_ffn_g, v_w_gate, v_w_up, v_w_down, v_final_norm_g):
    args = dict(locals())
    w = {n: args[n] for n in WEIGHTS}
    m = {n: args["m_" + n] for n in WEIGHTS}
    v = {n: args["v_" + n] for n in WEIGHTS}
    small = {n: w[n] for n in SMALL}
    as_handled = lambda a, n: jnp.swapaxes(a, 1, 2) if n in TRANSPOSED else a
    wh, mh, vh = ({n: as_handled(d[n], n) for n, _ in BIG} for d in (w, m, v))

    loss, grad_x, sums, small_grads = train_step(x, loss_target, small, MeshComm(wh, meta_tokens))
    loss = lax.psum(loss, ("x", "y", "c"))
    (small_recv,) = exchange([_pack([small_grads[n] for n in SMALL], 8)], False, ALL_PEERS).run("gather_small_grads")

    out = {n: [as_handled(a, n) for a in adamw(wh[n], mh[n], vh[n], [sums[(n, l)] for l in range(DEPTH)], name=f"adamw_{n}")]
           for n, _ in BIG}
    out["meta_tokens"] = [a[0] for a in adamw(meta_tokens[None], m["meta_tokens"][None], v["meta_tokens"][None],
                                              [sums[("meta_tokens", 0)]], name="adamw_meta_tokens")]
    pk = lambda d: _pack([d[n] for n in SMALL], 8)[None]
    packed = adamw(pk(w), pk(m), pk(v), [small_recv], name="adamw_small")
    shapes = [w[n].shape for n in SMALL]
    for n, *kinds in zip(SMALL, *[_unpack(packed[kind][0], shapes) for kind in range(4)]):
        out[n] = kinds
    return (loss, grad_x, *[out[n][kind] for kind in range(4) for n in WEIGHTS])
```

```python
import functools
import math

import jax
import jax.numpy as jnp
from jax import lax
from jax.experimental import pallas as pl
from jax.experimental.pallas import tpu as pltpu

F32, BF16 = jnp.float32, jnp.bfloat16

D_MODEL = 1024
N_META = 16
N_HEADS = 16
QK_NOPE, QK_ROPE, V_DIM = 64, 32, 64
HALF_ROPE = QK_ROPE // 2
Q_RANK, KV_RANK = 256, 128
POOL_WINDOWS = (2, 4, 8, 16)
POOL_GROUP = 128
POOL_WIDTH = POOL_GROUP * len(POOL_WINDOWS)
POOL_HALO = 16
D_FF = 2816
D_IN = 2976
NORM_EPS = 1e-6
SM_SCALE = (QK_NOPE + QK_ROPE) ** -0.5
LOG2E = math.log2(math.e)
EXP2_SCALE = SM_SCALE * LOG2E
MASK_VALUE = -1e30
ROPE_THETA = 10000.0
DEPTH = 2
N_DEV = 8

ADAM_LR, ADAM_B1, ADAM_B2, ADAM_EPS, ADAM_WD, ADAM_STEP = 0.001, 0.9, 0.999, 1e-08, 0.01, 10

LANES = 128
HEAD_SLOT = LANES
QK_WIDTH = N_HEADS * HEAD_SLOT
Z_CQ, Z_CKV, Z_KR, Z_GA, Z_GB, DZ = 512, 768, 896, 1024, 2048, 3072
TQ = TK = 256
VMEM_LIMIT = 56 * 1024 * 1024


def _cparams():
    return pltpu.CompilerParams(vmem_limit_bytes=VMEM_LIMIT)


def _rows(tm, width, col=0):
    return pl.BlockSpec((tm, width), lambda i: (i, col))


def _whole(shape):
    zeros = (0,) * len(shape)
    return pl.BlockSpec(shape, lambda i: zeros, pipeline_mode=pl.Buffered(1))


def _acc(shape):
    zeros = (0,) * len(shape)
    return pl.BlockSpec(shape, lambda i: zeros)


def _dot(a, b):
    return jnp.dot(a, b, preferred_element_type=F32)


def _dot_tn(a, b):
    return lax.dot_general(a, b, (((0,), (0,)), ((), ())), preferred_element_type=F32)


def _dot_nt(a, b):
    return lax.dot_general(a, b, (((1,), (1,)), ((), ())), preferred_element_type=F32)


def _rms(x):
    r = lax.rsqrt(jnp.mean(x * x, axis=-1, keepdims=True) + NORM_EPS)
    return x * r, r


def _rms_bwd(dy, xhat, r, g):
    dg = jnp.sum(dy * xhat, axis=0, keepdims=True)
    dxh = dy * g
    dx = r * (dxh - xhat * jnp.mean(dxh * xhat, axis=-1, keepdims=True))
    return dx, dg


def _sigmoid(x):
    return 1.0 / (1.0 + jnp.exp(-x))


def _rope_fwd(q, c, s1, s2):
    w = q.shape[1]
    return q * c + pltpu.roll(q, w - HALF_ROPE, 1) * s1 + pltpu.roll(q, HALF_ROPE, 1) * s2


def _rope_bwd(dq, c, s1, s2):
    w = dq.shape[1]
    return dq * c + pltpu.roll(dq * s1, HALF_ROPE, 1) + pltpu.roll(dq * s2, w - HALF_ROPE, 1)


def _rope_tables(rope, reps):
    c, cr, s1, s2 = (rope[:, k * LANES:(k + 1) * LANES] for k in range(4))
    if reps > 1:
        return jnp.tile(c, (1, reps)), jnp.tile(s1, (1, reps)), jnp.tile(s2, (1, reps))
    return cr, s1, s2


def _seq_pos(gi, lp, nb):
    pos = gi
    for b in range(1, nb):
        pos = jnp.where(gi >= b * lp, gi - b * lp, pos)
    return pos


_ANY = pl.BlockSpec(memory_space=pl.ANY)


def _carrying_call(body, ride, operands, *, name, grid, in_specs, out_specs, out_shape, scratch_shapes=()):
    n_in, n_out = len(in_specs), len(out_specs)
    if ride is None:
        out = pl.pallas_call(body, name=name, grid=grid, in_specs=in_specs, out_specs=out_specs, out_shape=out_shape,
                             scratch_shapes=list(scratch_shapes), compiler_params=_cparams())(*operands)
        return out, []
    ne = len(ride.arrays)

    def carrying(*refs):
        ins, r_in, rest = refs[:n_in], refs[n_in:n_in + ne], refs[n_in + ne:]
        outs, r_out, rest = rest[:n_out], rest[n_out:n_out + ne], rest[n_out + ne:]
        scratch, sems = rest[:len(scratch_shapes)], rest[len(scratch_shapes):]
        ids = [pl.program_id(a) for a in range(len(grid))]
        first = functools.reduce(jnp.logical_and, [i == 0 for i in ids])
        last = functools.reduce(jnp.logical_and, [i == g - 1 for i, g in zip(ids, grid)])

        @pl.when(first)
        def _():
            ride.start(r_in, r_out, sems)

        body(*ins, *outs, *scratch)

        @pl.when(last)
        def _():
            ride.wait(r_in, r_out, sems)

    out = pl.pallas_call(
        carrying, name=name, grid=grid, in_specs=list(in_specs) + [_ANY] * ne, out_specs=list(out_specs) + [_ANY] * ne,
        out_shape=list(out_shape) + ride.out_shapes, scratch_shapes=list(scratch_shapes) + ride.scratch,
        input_output_aliases=ride.aliases(n_in, n_out), compiler_params=_cparams(),
    )(*operands, *ride.arrays)
    return out[:n_out], out[n_out:]


def in_proj_fwd(h, g_mix, win, gq, gkv, wuq, wuk, wuv, rope, *, tm, name):
    t = h.shape[0]

    def body(h_ref, g_ref, win_ref, gq_ref, gkv_ref, wuq_ref, wuk_ref, wuv_ref, rope_ref, z_ref, q_ref, k_ref, v_ref):
        xhat, _ = _rms(h_ref[...])
        hn = (xhat * g_ref[...]).astype(BF16)
        z = _dot(hn, win_ref[...])
        z_ref[...] = z
        rope_t = rope_ref[...]
        xq, _ = _rms(z[:, Z_CQ:Z_CKV])
        cqn = (xq * gq_ref[...]).astype(BF16)
        q = _rope_fwd(_dot(cqn, wuq_ref[...]), *_rope_tables(rope_t, N_HEADS))
        q_ref[...] = q.astype(BF16)
        xkv, _ = _rms(z[:, Z_CKV:Z_KR])
        ckvn = (xkv * gkv_ref[...]).astype(BF16)
        kr = _rope_fwd(z[:, Z_KR:Z_GA], *_rope_tables(rope_t, 1))
        k_ref[...] = (_dot(ckvn, wuk_ref[...]) + jnp.tile(kr, (1, N_HEADS))).astype(BF16)
        v_ref[...] = _dot(ckvn, wuv_ref[...]).astype(BF16)

    return pl.pallas_call(
        body, name=name, grid=(t // tm,),
        in_specs=[_rows(tm, D_MODEL), _whole((1, D_MODEL)), _whole((D_MODEL, DZ)), _whole((1, Q_RANK)), _whole((1, KV_RANK)),
                  _whole((Q_RANK, QK_WIDTH)), _whole((KV_RANK, QK_WIDTH)), _whole((KV_RANK, D_MODEL)), _rows(tm, 4 * LANES)],
        out_specs=[_rows(tm, DZ), _rows(tm, QK_WIDTH), _rows(tm, QK_WIDTH), _rows(tm, D_MODEL)],
        out_shape=[jax.ShapeDtypeStruct((t, DZ), F32), jax.ShapeDtypeStruct((t, QK_WIDTH), BF16),
                   jax.ShapeDtypeStruct((t, QK_WIDTH), BF16), jax.ShapeDtypeStruct((t, D_MODEL), BF16)],
        compiler_params=_cparams(),
    )(h, g_mix, win, gq, gkv, wuq, wuk, wuv, rope)


def attn_fwd(q, k, v, *, nb, lp, hb, name, ride=None):
    t = q.shape[0]
    nq, tail = lp // TQ, lp % TQ
    assert tail % LANES == 0

    def body(q_ref, k_ref, v_ref, o_ref, lse_ref, vt):
        for pr in range(hb // 2):
            vt[pr] = v_ref[:, pr * LANES:(pr + 1) * LANES].T

        def q_block(qs, tq, whole_k):
            qh = [q_ref[pl.ds(qs, tq), hd * HEAD_SLOT:(hd + 1) * HEAD_SLOT] for hd in range(hb)]
            keep = lax.broadcasted_iota(jnp.int32, (tq, tq), 0) <= lax.broadcasted_iota(jnp.int32, (tq, tq), 1)

            def k_steps(blocks, c, masked):
                sts = [[_dot_nt(k_ref[pl.ds(ks, tk), hd * HEAD_SLOT:(hd + 1) * HEAD_SLOT], qh[hd]) for hd in range(hb)]
                       for ks, tk in blocks]
                for (ks, tk), st_b in zip(blocks, sts):
                    ps, stats = [], []
                    for hd in range(hb):
                        m, l, _ = c[hd]
                        st = jnp.where(keep, st_b[hd], MASK_VALUE) if masked else st_b[hd]
                        m_new = jnp.maximum(m, jnp.max(st, axis=0, keepdims=True))
                        p = jnp.exp2((st - m_new) * EXP2_SCALE)
                        alpha = jnp.exp2((m - m_new) * EXP2_SCALE)
                        ps.append(p.astype(BF16))
                        stats.append((m_new, alpha * l + jnp.sum(p, axis=0, keepdims=True), alpha))
                    pvs = [_dot(vt[hd // 2, :, pl.ds(ks, tk)], ps[hd]) for hd in range(hb)]
                    c = tuple((stats[hd][0], stats[hd][1], stats[hd][2] * c[hd][2] + pvs[hd]) for hd in range(hb))
                return c

            def two_blocks(i, c):
                ks = pl.multiple_of(2 * i * TK, TK)
                return k_steps([(ks, TK), (ks + TK, TK)], c, False)

            init = tuple((jnp.full((1, tq), MASK_VALUE, F32), jnp.zeros((1, tq), F32), jnp.zeros((LANES, tq), F32))
                         for _ in range(hb))
            pairs = lax.div(whole_k, 2)
            c = lax.fori_loop(0, pairs, two_blocks, init)
            c = lax.fori_loop(2 * pairs, whole_k, lambda kj, c: k_steps([(pl.multiple_of(kj * TK, TK), TK)], c, False), c)
            c = k_steps([(qs, tq)], c, True)
            sub = lax.broadcasted_iota(jnp.int32, (LANES, tq), 0)
            for pr in range(hb // 2):
                (m0, l0, a0), (m1, l1, a1) = c[2 * pr], c[2 * pr + 1]
                o_ref[pl.ds(qs, tq), pr * LANES:(pr + 1) * LANES] = jnp.where(sub < V_DIM, a0 / l0, a1 / l1).T.astype(BF16)
                lse_ref[2 * pr, :, pl.ds(qs, tq)] = m0 * SM_SCALE + jnp.log(l0)
                lse_ref[2 * pr + 1, :, pl.ds(qs, tq)] = m1 * SM_SCALE + jnp.log(l1)

        def whole_q_block(qi, carry):
            q_block(pl.multiple_of(qi * TQ, TQ), TQ, qi)
            return carry

        lax.fori_loop(0, nq, whole_q_block, 0)
        if tail:
            q_block(nq * TQ, tail, nq)

    blk = lambda w: pl.BlockSpec((lp, w), lambda b, g: (b, g))
    return _carrying_call(
        body, ride, (q, k, v), name=name, grid=(nb, N_HEADS // hb),
        in_specs=[blk(hb * HEAD_SLOT), blk(hb * HEAD_SLOT), blk(hb * V_DIM)],
        out_specs=[blk(hb * V_DIM), pl.BlockSpec((hb, 1, lp), lambda b, g: (g, 0, b))],
        out_shape=[jax.ShapeDtypeStruct((t, D_MODEL), BF16), jax.ShapeDtypeStruct((N_HEADS, 1, t), F32)],
        scratch_shapes=[pltpu.VMEM((hb // 2, LANES, lp), BF16)])


def _pool_band_fwd(i, tm, lp, nb):
    r = lax.broadcasted_iota(jnp.int32, (tm, POOL_HALO + tm), 0)
    e = lax.broadcasted_iota(jnp.int32, (tm, POOL_HALO + tm), 1)
    diff = r + POOL_HALO - e
    pos = _seq_pos(i * tm + lax.broadcasted_iota(jnp.int32, (tm, 1), 0), lp, nb)
    out = []
    for w in POOL_WINDOWS:
        cnt = jnp.minimum(pos + 1, w)
        band = jnp.where((diff >= 0) & (diff < cnt), 1.0, 0.0).astype(BF16)
        out.append((band, cnt.astype(F32)))
    return out


def merge_fwd(h, z, o, pw, ps, wpa, wpb, wo, *, tm, lp, nb, name, ride=None):
    t = h.shape[0]
    hb = tm // POOL_HALO

    def body(h_ref, u_ref, uprev_ref, ga_ref, gb_ref, o_ref, pw_ref, ps_ref, wpa_ref, wpb_ref, wo_ref,
             h1_ref, pooled_ref, a_ref, pa_ref, pb_ref, mg_ref):
        i = pl.program_id(0)
        u = u_ref[...]
        uext = jnp.concatenate([uprev_ref[...], u], axis=0).astype(BF16)
        pooled, ys = [], []
        for g, (band, cnt) in enumerate(_pool_band_fwd(i, tm, lp, nb)):
            gs = slice(g * POOL_GROUP, (g + 1) * POOL_GROUP)
            pg = (_dot(band, uext[:, gs]) / cnt - u[:, gs]).astype(BF16)
            pooled.append(pg)
            ys.append(_dot(pg, pw_ref[g]))
        pooled_ref[...] = jnp.concatenate(pooled, axis=1)
        a = (jnp.concatenate(ys, axis=1) * ps_ref[...]).astype(BF16)
        a_ref[...] = a
        pa = _dot(a, wpa_ref[...])
        pb = _dot(o_ref[...], wpb_ref[...])
        pa_ref[...] = pa.astype(BF16)
        pb_ref[...] = pb.astype(BF16)
        mg = (_sigmoid(ga_ref[...]) * pa + _sigmoid(gb_ref[...]) * pb).astype(BF16)
        mg_ref[...] = mg
        h1_ref[...] = h_ref[...] + _dot(mg, wo_ref[...])

    halo = pl.BlockSpec((POOL_HALO, POOL_WIDTH), lambda i: (jnp.maximum(i * hb - 1, 0), 0))
    return _carrying_call(
        body, ride, (h, z, z, z, z, o, pw, ps, wpa, wpb, wo), name=name, grid=(t // tm,),
        in_specs=[_rows(tm, D_MODEL), _rows(tm, POOL_WIDTH), halo, _rows(tm, D_MODEL, 1), _rows(tm, D_MODEL, 2), _rows(tm, D_MODEL),
                  _whole((4, POOL_GROUP, POOL_GROUP)), _whole((1, POOL_WIDTH)), _whole((POOL_WIDTH, D_MODEL)),
                  _whole((D_MODEL, D_MODEL)), _whole((D_MODEL, D_MODEL))],
        out_specs=[_rows(tm, D_MODEL), _rows(tm, POOL_WIDTH), _rows(tm, POOL_WIDTH), _rows(tm, D_MODEL), _rows(tm, D_MODEL),
                   _rows(tm, D_MODEL)],
        out_shape=[jax.ShapeDtypeStruct((t, D_MODEL), F32), jax.ShapeDtypeStruct((t, POOL_WIDTH), BF16),
                   jax.ShapeDtypeStruct((t, POOL_WIDTH), BF16), jax.ShapeDtypeStruct((t, D_MODEL), BF16),
                   jax.ShapeDtypeStruct((t, D_MODEL), BF16), jax.ShapeDtypeStruct((t, D_MODEL), BF16)])


def ffn_fwd(h1, g, wgt, wut, wd, *, tm, name):
    t = h1.shape[0]

    def body(h_ref, g_ref, wgt_ref, wut_ref, wd_ref, h2_ref, gt_ref, up_ref):
        h = h_ref[...]
        xhat, _ = _rms(h)
        hn = (xhat * g_ref[...]).astype(BF16)
        gt = _dot_nt(hn, wgt_ref[...])
        up = _dot_nt(hn, wut_ref[...])
        gt_ref[...] = gt.astype(BF16)
        up_ref[...] = up.astype(BF16)
        act = (gt * _sigmoid(gt) * up).astype(BF16)
        h2_ref[...] = h + _dot(act, wd_ref[...])

    return pl.pallas_call(
        body, name=name, grid=(t // tm,),
        in_specs=[_rows(tm, D_MODEL), _whole((1, D_MODEL)), _whole((D_FF, D_MODEL)), _whole((D_FF, D_MODEL)), _whole((D_FF, D_MODEL))],
        out_specs=[_rows(tm, D_MODEL), _rows(tm, D_FF), _rows(tm, D_FF)],
        out_shape=[jax.ShapeDtypeStruct((t, D_MODEL), F32), jax.ShapeDtypeStruct((t, D_FF), BF16), jax.ShapeDtypeStruct((t, D_FF), BF16)],
        compiler_params=_cparams(),
    )(h1, g, wgt, wut, wd)


def loss_head(h, g, target, *, tm, lp, nb, seq, name):
    t = h.shape[0]
    nt = t // tm

    def body(h_ref, g_ref, t_ref, loss_ref, dh_ref, dg_ref):
        i = pl.program_id(0)
        pos = _seq_pos(i * tm + lax.broadcasted_iota(jnp.int32, (tm, 1), 0), lp, nb)
        real = (pos >= N_META) & (pos < N_META + seq)
        xhat, r = _rms(h_ref[...])
        gg = g_ref[...]
        err = jnp.where(real, xhat * gg - t_ref[...], 0.0)
        loss_ref[...] = jnp.full((8, LANES), 0.5 * jnp.sum(err * err) / D_MODEL, F32)
        dx, dg = _rms_bwd(err * (1.0 / D_MODEL), xhat, r, gg)
        dh_ref[...] = dx

        @pl.when(i == 0)
        def _():
            dg_ref[...] = jnp.zeros_like(dg_ref)

        dg_ref[...] += dg

    return pl.pallas_call(
        body, name=name, grid=(nt,),
        in_specs=[_rows(tm, D_MODEL), _whole((1, D_MODEL)), _rows(tm, D_MODEL)],
        out_specs=[pl.BlockSpec((8, LANES), lambda i: (i, 0)), _rows(tm, D_MODEL), _acc((1, D_MODEL))],
        out_shape=[jax.ShapeDtypeStruct((nt * 8, LANES), F32), jax.ShapeDtypeStruct((t, D_MODEL), F32),
                   jax.ShapeDtypeStruct((1, D_MODEL), F32)],
        compiler_params=_cparams(),
    )(h, g, target)


def wgrad(x, ys, chunk_fn, chunk_shape, *, tm, name):
    t, m = x.shape

    def body(x_ref, *refs):
        y_refs, o_ref, accs = refs[:len(ys)], refs[len(ys)], refs[len(ys) + 1:]
        i = pl.program_id(0)

        @pl.when(i == 0)
        def _():
            for acc in accs:
                acc[...] = jnp.zeros_like(acc)

        xb = x_ref[...].astype(BF16)
        for y_ref, acc in zip(y_refs, accs):
            acc[...] += _dot_tn(xb, y_ref[...].astype(BF16))

        @pl.when(i == t // tm - 1)
        def _():
            for p, chunk in enumerate(chunk_fn(*accs)):
                o_ref[p % 2, p // 2] = chunk.astype(BF16)

    out = (2, N_DEV // 2) + tuple(chunk_shape)
    return pl.pallas_call(
        body, name=name, grid=(t // tm,),
        in_specs=[_rows(tm, m)] + [_rows(tm, y.shape[1]) for y in ys], out_specs=_acc(out),
        out_shape=jax.ShapeDtypeStruct(out, BF16), scratch_shapes=[pltpu.VMEM((m, y.shape[1]), F32) for y in ys],
        compiler_params=_cparams(),
    )(x, *ys)


def ffn_bwd(dh2, h1, g, gt, up, wgt, wut, wd, *, tm, name):
    t = h1.shape[0]

    def body(dh2_ref, h_ref, g_ref, gt_ref, up_ref, wgt_ref, wut_ref, wd_ref, dh1_ref, hn_ref, act_ref, dgt_ref, dup_ref, dg_ref):
        dh2 = dh2_ref[...]
        dact = _dot_nt(dh2.astype(BF16), wd_ref[...])
        gt = gt_ref[...].astype(F32)
        up = up_ref[...].astype(F32)
        sg = _sigmoid(gt)
        silu = gt * sg
        act_ref[...] = (silu * up).astype(BF16)
        dgt = (dact * up * (sg * (1.0 + gt * (1.0 - sg)))).astype(BF16)
        dup = (dact * silu).astype(BF16)
        dgt_ref[...] = dgt
        dup_ref[...] = dup
        dhn = _dot(dgt, wgt_ref[...]) + _dot(dup, wut_ref[...])
        xhat, r = _rms(h_ref[...])
        gg = g_ref[...]
        hn_ref[...] = (xhat * gg).astype(BF16)
        dx, dg = _rms_bwd(dhn, xhat, r, gg)
        dh1_ref[...] = dh2 + dx

        @pl.when(pl.program_id(0) == 0)
        def _():
            dg_ref[...] = jnp.zeros_like(dg_ref)

        dg_ref[...] += dg

    return pl.pallas_call(
        body, name=name, grid=(t // tm,),
        in_specs=[_rows(tm, D_MODEL), _rows(tm, D_MODEL), _whole((1, D_MODEL)), _rows(tm, D_FF), _rows(tm, D_FF),
                  _whole((D_FF, D_MODEL)), _whole((D_FF, D_MODEL)), _whole((D_FF, D_MODEL))],
        out_specs=[_rows(tm, D_MODEL), _rows(tm, D_MODEL), _rows(tm, D_FF), _rows(tm, D_FF), _rows(tm, D_FF), _acc((1, D_MODEL))],
        out_shape=[jax.ShapeDtypeStruct((t, D_MODEL), F32), jax.ShapeDtypeStruct((t, D_MODEL), BF16),
                   jax.ShapeDtypeStruct((t, D_FF), BF16), jax.ShapeDtypeStruct((t, D_FF), BF16),
                   jax.ShapeDtypeStruct((t, D_FF), BF16), jax.ShapeDtypeStruct((1, D_MODEL), F32)],
        compiler_params=_cparams(),
    )(dh2, h1, g, gt, up, wgt, wut, wd)


def merge_bwd(dh1, z, pa, pb, pooled, pw, ps, wpa, wpb, wo, *, tm, name):
    t = dh1.shape[0]

    def body(dh1_ref, ga_ref, gb_ref, pa_ref, pb_ref, pooled_ref, pw_ref, ps_ref, wpa_ref, wpb_ref, wo_ref,
             dga_ref, dgb_ref, dpa_ref, dpb_ref, do_ref, dpool_ref, dps_ref, dpw_ref):
        dmg = _dot_nt(dh1_ref[...].astype(BF16), wo_ref[...])
        sa = _sigmoid(ga_ref[...])
        sb = _sigmoid(gb_ref[...])
        dga_ref[...] = (dmg * pa_ref[...].astype(F32) * sa * (1.0 - sa)).astype(BF16)
        dgb_ref[...] = (dmg * pb_ref[...].astype(F32) * sb * (1.0 - sb)).astype(BF16)
        dpa = (dmg * sa).astype(BF16)
        dpb = (dmg * sb).astype(BF16)
        dpa_ref[...] = dpa
        dpb_ref[...] = dpb
        do_ref[...] = _dot_nt(dpb, wpb_ref[...]).astype(BF16)
        da = _dot_nt(dpa, wpa_ref[...])
        pooled = pooled_ref[...]
        ps = ps_ref[...]

        @pl.when(pl.program_id(0) == 0)
        def _():
            dps_ref[...] = jnp.zeros_like(dps_ref)
            dpw_ref[...] = jnp.zeros_like(dpw_ref)

        dps, dpool = [], []
        for g in range(len(POOL_WINDOWS)):
            gs = slice(g * POOL_GROUP, (g + 1) * POOL_GROUP)
            y = _dot(pooled[:, gs], pw_ref[g])
            dps.append(jnp.sum(da[:, gs] * y, axis=0, keepdims=True))
            dy = (da[:, gs] * ps[:, gs]).astype(BF16)
            dpool.append(_dot_nt(dy, pw_ref[g]))
            dpw_ref[g] += _dot_tn(pooled[:, gs], dy)
        dps_ref[...] += jnp.concatenate(dps, axis=1)
        dpool_ref[...] = jnp.concatenate(dpool, axis=1)

    return pl.pallas_call(
        body, name=name, grid=(t // tm,),
        in_specs=[_rows(tm, D_MODEL), _rows(tm, D_MODEL, 1), _rows(tm, D_MODEL, 2), _rows(tm, D_MODEL), _rows(tm, D_MODEL),
                  _rows(tm, POOL_WIDTH), _whole((4, POOL_GROUP, POOL_GROUP)),
                  _whole((1, POOL_WIDTH)), _whole((POOL_WIDTH, D_MODEL)), _whole((D_MODEL, D_MODEL)), _whole((D_MODEL, D_MODEL))],
        out_specs=[_rows(tm, D_MODEL), _rows(tm, D_MODEL), _rows(tm, D_MODEL), _rows(tm, D_MODEL), _rows(tm, D_MODEL),
                   _rows(tm, POOL_WIDTH), _acc((1, POOL_WIDTH)), _acc((4, POOL_GROUP, POOL_GROUP))],
        out_shape=[jax.ShapeDtypeStruct((t, D_MODEL), BF16)] * 5
        + [jax.ShapeDtypeStruct((t, POOL_WIDTH), F32), jax.ShapeDtypeStruct((1, POOL_WIDTH), F32),
           jax.ShapeDtypeStruct((4, POOL_GROUP, POOL_GROUP), F32)],
        compiler_params=_cparams(),
    )(dh1, z, z, pa, pb, pooled, pw, ps, wpa, wpb, wo)


def attn_bwd(q, k, v, o, do, lse, *, nb, lp, hb, name, ride=None):
    t = q.shape[0]
    nq, tail = lp // TQ, lp % TQ
    assert tail % LANES == 0

    def body(q_ref, k_ref, v_ref, o_ref, do_ref, lse_ref, dq_ref, dk_ref, dv_ref, kt, doh, lse_row, delta_row, dqt):
        lane = lax.broadcasted_iota(jnp.int32, (lp, LANES), 1)
        first = lane < V_DIM
        sub = lax.broadcasted_iota(jnp.int32, (LANES, lp), 0)
        for pr in range(hb // 2):
            ls = slice(pr * LANES, (pr + 1) * LANES)
            do = do_ref[:, ls]
            doh[2 * pr] = jnp.where(first, do, jnp.zeros_like(do))
            doh[2 * pr + 1] = jnp.where(first, jnp.zeros_like(do), do)
            prod_t = (do.astype(F32) * o_ref[:, ls].astype(F32)).T
            delta_row[2 * pr] = jnp.sum(jnp.where(sub < V_DIM, prod_t, 0.0), axis=0, keepdims=True)
            delta_row[2 * pr + 1] = jnp.sum(jnp.where(sub < V_DIM, 0.0, prod_t), axis=0, keepdims=True)
        for hd in range(hb):
            lse_row[hd] = lse_ref[hd] * LOG2E
            kt[hd] = k_ref[:, hd * HEAD_SLOT:(hd + 1) * HEAD_SLOT].T
        dqt[...] = jnp.zeros(dqt.shape, F32)
        heads = range(hb)
        hss = [slice(hd * HEAD_SLOT, (hd + 1) * HEAD_SLOT) for hd in heads]

        def k_block(ks, tk, next_q):
            keep = lax.broadcasted_iota(jnp.int32, (tk, tk), 0) <= lax.broadcasted_iota(jnp.int32, (tk, tk), 1)

            def q_steps(blocks, c, masked):
                work = [(qs, tq, hd) for qs, tq in blocks for hd in heads]
                qhs = [q_ref[pl.ds(qs, tq), hss[hd]] for qs, tq, hd in work]
                dos = [doh[hd, pl.ds(qs, tq), :] for qs, tq, hd in work]
                sts = [_dot_nt(k_ref[pl.ds(ks, tk), hss[hd]], qhs[i]) for i, (_, _, hd) in enumerate(work)]
                dpts = [_dot_nt(v_ref[pl.ds(ks, tk), (hd // 2) * LANES:(hd // 2 + 1) * LANES], dos[i])
                        for i, (_, _, hd) in enumerate(work)]
                pts, dsts = [], []
                for i, (qs, tq, hd) in enumerate(work):
                    st = jnp.where(keep, sts[i], MASK_VALUE) if masked else sts[i]
                    pt = jnp.exp2(st * EXP2_SCALE - lse_row[hd, :, pl.ds(qs, tq)])
                    dsts.append((pt * (dpts[i] - delta_row[hd, :, pl.ds(qs, tq)])).astype(BF16))
                    pts.append(pt.astype(BF16))
                dvs = [_dot(pts[i], dos[i]) for i in range(len(work))]
                dks = [_dot(dsts[i], qhs[i]) for i in range(len(work))]
                dqs = [_dot(kt[hd, :, pl.ds(ks, tk)], dsts[i]) for i, (_, _, hd) in enumerate(work)]
                c = list(c)
                for i, (qs, tq, hd) in enumerate(work):
                    dqt[hd, :, pl.ds(qs, tq)] += dqs[i]
                    c[hd] = (c[hd][0] + dks[i], c[hd][1] + dvs[i])
                return tuple(c)

            zero = jnp.zeros((tk, LANES), F32)
            c = q_steps([(ks, tk)], tuple((zero, zero) for _ in heads), True)
            if next_q is not None:
                def two_blocks(i, c):
                    qs = pl.multiple_of((next_q + 2 * i) * TQ, TQ)
                    return q_steps([(qs, TQ), (qs + TQ, TQ)], c, False)

                pairs = lax.div(nq - next_q, 2)
                c = lax.fori_loop(0, pairs, two_blocks, c)
                c = lax.fori_loop(next_q + 2 * pairs, nq, lambda qi, c: q_steps([(pl.multiple_of(qi * TQ, TQ), TQ)], c, False), c)
                if tail:
                    c = q_steps([(nq * TQ, tail)], c, False)
            for hd in heads:
                dk_ref[pl.ds(ks, tk), hss[hd]] = c[hd][0] * SM_SCALE
            for pr in range(hb // 2):
                dv_ref[pl.ds(ks, tk), pr * LANES:(pr + 1) * LANES] = c[2 * pr][1] + c[2 * pr + 1][1]

        def whole_k_block(kj, carry):
            k_block(pl.multiple_of(kj * TK, TK), TK, kj + 1)
            return carry

        lax.fori_loop(0, nq, whole_k_block, 0)
        if tail:
            k_block(nq * TQ, tail, None)
        for hd in range(hb):
            dq_ref[:, hd * HEAD_SLOT:(hd + 1) * HEAD_SLOT] = dqt[hd].T * SM_SCALE

    blk = lambda w: pl.BlockSpec((lp, w), lambda b, g: (b, g))
    return _carrying_call(
        body, ride, (q, k, v, o, do, lse), name=name, grid=(nb, N_HEADS // hb),
        in_specs=[blk(hb * HEAD_SLOT), blk(hb * HEAD_SLOT), blk(hb * V_DIM), blk(hb * V_DIM), blk(hb * V_DIM),
                  pl.BlockSpec((hb, 1, lp), lambda b, g: (g, 0, b))],
        out_specs=[blk(hb * HEAD_SLOT), blk(hb * HEAD_SLOT), blk(hb * V_DIM)],
        out_shape=[jax.ShapeDtypeStruct((t, QK_WIDTH), F32), jax.ShapeDtypeStruct((t, QK_WIDTH), F32),
                   jax.ShapeDtypeStruct((t, D_MODEL), F32)],
        scratch_shapes=[pltpu.VMEM((hb, HEAD_SLOT, lp), BF16), pltpu.VMEM((hb, lp, LANES), BF16), pltpu.VMEM((hb, 1, lp), F32),
                        pltpu.VMEM((hb, 1, lp), F32), pltpu.VMEM((hb, HEAD_SLOT, lp), F32)])


def in_proj_bwd(dh1, h, g_mix, z, dq, dk, dv, dga, dgb, dpool, win, gq, gkv, wuq, wuk, wuv, rope, *, tm, lp, nb, name):
    t = h.shape[0]
    hb = tm // POOL_HALO
    last_halo = t // POOL_HALO - 1

    def body(dh1_ref, h_ref, g_ref, zcq_ref, zckv_ref, dq_ref, dk_ref, dv_ref, dga_ref, dgb_ref, dpool_ref, dnext_ref,
             win_ref, gq_ref, gkv_ref, wuq_ref, wuk_ref, wuv_ref, rope_ref,
             dh_ref, hn_ref, dz_ref, cqn_ref, ckvn_ref, dqb_ref, dkb_ref, dvb_ref, dg_ref, dgq_ref, dgkv_ref):
        i = pl.program_id(0)
        rope_t = rope_ref[...]
        dqb = _rope_bwd(dq_ref[...], *_rope_tables(rope_t, N_HEADS)).astype(BF16)
        dqb_ref[...] = dqb
        xq, rq = _rms(zcq_ref[...])
        gq_v = gq_ref[...]
        cqn_ref[...] = (xq * gq_v).astype(BF16)
        dcq, dgq = _rms_bwd(_dot_nt(dqb, wuq_ref[...]), xq, rq, gq_v)
        dk = dk_ref[...]
        dkb = dk.astype(BF16)
        dvb = dv_ref[...].astype(BF16)
        dkb_ref[...] = dkb
        dvb_ref[...] = dvb
        xkv, rkv = _rms(zckv_ref[...])
        gkv_v = gkv_ref[...]
        ckvn_ref[...] = (xkv * gkv_v).astype(BF16)
        dckv, dgkv = _rms_bwd(_dot_nt(dkb, wuk_ref[...]) + _dot_nt(dvb, wuv_ref[...]), xkv, rkv, gkv_v)
        dks = dk[:, :HEAD_SLOT]
        for hd in range(1, N_HEADS):
            dks = dks + dk[:, hd * HEAD_SLOT:(hd + 1) * HEAD_SLOT]
        dzk = _rope_bwd(dks, *_rope_tables(rope_t, 1))
        dp_cur = dpool_ref[...]
        dp_ext = jnp.concatenate([dp_cur, dnext_ref[...]], axis=0)
        r = lax.broadcasted_iota(jnp.int32, (tm, tm + POOL_HALO), 0)
        e = lax.broadcasted_iota(jnp.int32, (tm, tm + POOL_HALO), 1)
        gt_col = i * tm + lax.broadcasted_iota(jnp.int32, (1, tm + POOL_HALO), 1)
        pos_col = _seq_pos(gt_col, lp, nb)
        gt_row = i * tm + lax.broadcasted_iota(jnp.int32, (tm + POOL_HALO, 1), 0)
        pos_row = _seq_pos(gt_row, lp, nb)
        dus = []
        for g, w in enumerate(POOL_WINDOWS):
            gs = slice(g * POOL_GROUP, (g + 1) * POOL_GROUP)
            band = jnp.where((e - r >= 0) & (e - r < jnp.minimum(pos_col + 1, w)) & (gt_col < t), 1.0, 0.0).astype(BF16)
            scaled = jnp.where(gt_row < t, dp_ext[:, gs] / jnp.minimum(pos_row + 1, w).astype(F32), 0.0).astype(BF16)
            dus.append(_dot(band, scaled) - dp_cur[:, gs])
        dz = jnp.concatenate(dus + [dcq, dckv, dzk], axis=1).astype(BF16)
        dz = jnp.concatenate([dz, dga_ref[...], dgb_ref[...]], axis=1)
        dz_ref[...] = dz
        xhat, rr = _rms(h_ref[...])
        gg = g_ref[...]
        hn_ref[...] = (xhat * gg).astype(BF16)
        dx, dg = _rms_bwd(_dot_nt(dz, win_ref[...]), xhat, rr, gg)
        dh_ref[...] = dh1_ref[...] + dx

        @pl.when(i == 0)
        def _():
            dg_ref[...] = jnp.zeros_like(dg_ref)
            dgq_ref[...] = jnp.zeros_like(dgq_ref)
            dgkv_ref[...] = jnp.zeros_like(dgkv_ref)

        dg_ref[...] += dg
        dgq_ref[...] += dgq
        dgkv_ref[...] += dgkv

    nxt = pl.BlockSpec((POOL_HALO, POOL_WIDTH), lambda i: (jnp.minimum((i + 1) * hb, last_halo), 0))
    return pl.pallas_call(
        body, name=name, grid=(t // tm,),
        in_specs=[_rows(tm, D_MODEL), _rows(tm, D_MODEL), _whole((1, D_MODEL)), _rows(tm, Q_RANK, Z_CQ // Q_RANK),
                  _rows(tm, KV_RANK, Z_CKV // KV_RANK), _rows(tm, QK_WIDTH), _rows(tm, QK_WIDTH), _rows(tm, D_MODEL),
                  _rows(tm, D_MODEL), _rows(tm, D_MODEL), _rows(tm, POOL_WIDTH), nxt,
                  _whole((D_MODEL, DZ)), _whole((1, Q_RANK)), _whole((1, KV_RANK)), _whole((Q_RANK, QK_WIDTH)),
                  _whole((KV_RANK, QK_WIDTH)), _whole((KV_RANK, D_MODEL)), _rows(tm, 4 * LANES)],
        out_specs=[_rows(tm, D_MODEL), _rows(tm, D_MODEL), _rows(tm, DZ), _rows(tm, Q_RANK), _rows(tm, KV_RANK),
                   _rows(tm, QK_WIDTH), _rows(tm, QK_WIDTH), _rows(tm, D_MODEL),
                   _acc((1, D_MODEL)), _acc((1, Q_RANK)), _acc((1, KV_RANK))],
        out_shape=[jax.ShapeDtypeStruct((t, D_MODEL), F32), jax.ShapeDtypeStruct((t, D_MODEL), BF16),
                   jax.ShapeDtypeStruct((t, DZ), BF16), jax.ShapeDtypeStruct((t, Q_RANK), BF16),
                   jax.ShapeDtypeStruct((t, KV_RANK), BF16), jax.ShapeDtypeStruct((t, QK_WIDTH), BF16),
                   jax.ShapeDtypeStruct((t, QK_WIDTH), BF16), jax.ShapeDtypeStruct((t, D_MODEL), BF16),
                   jax.ShapeDtypeStruct((1, D_MODEL), F32), jax.ShapeDtypeStruct((1, Q_RANK), F32),
                   jax.ShapeDtypeStruct((1, KV_RANK), F32)],
        compiler_params=_cparams(),
    )(dh1, h, g_mix, z, z, dq, dk, dv, dga, dgb, dpool, dpool, win, gq, gkv, wuq, wuk, wuv, rope)


_MESH = pl.DeviceIdType.MESH


def _place():
    x, y, c = lax.axis_index("x"), lax.axis_index("y"), lax.axis_index("c")
    return x, y, c, 4 * x + 2 * y + c


def _peer(x, y, c, k):
    px, py, pc = (1 - x) if k & 4 else x, (1 - y) if k & 2 else y, (1 - c) if k & 1 else c
    return (px, py, pc), 4 * px + 2 * py + pc


ALL_PEERS = tuple(range(1, N_DEV))
CHIP_PEERS = (2, 4, 6)
N_CHIPS = N_DEV // 2


def _sem_scratch(n, m):
    return [pltpu.SemaphoreType.DMA((n, m)), pltpu.SemaphoreType.DMA((n, m)), pltpu.SemaphoreType.DMA((n,))]


class Exchange:
    def __init__(self, arrays, out_shapes, sem_cols, plan, aliased=False):
        self.arrays, self.out_shapes, self.plan = list(arrays), list(out_shapes), plan
        self.scratch = _sem_scratch(len(self.arrays), sem_cols)
        self.aliased = aliased

    def split(self, refs):
        n = len(self.arrays)
        return refs[:n], refs[n:2 * n], refs[2 * n:]

    def start(self, srcs, dsts, sems):
        local, sends, _ = self.plan(srcs, dsts, *sems)
        for cp in local + sends:
            cp.start()

    def wait(self, srcs, dsts, sems):
        local, sends, recvs = self.plan(srcs, dsts, *sems)
        for cp in recvs:
            cp.wait_recv()
        for cp in sends:
            cp.wait_send()
        for cp in local:
            cp.wait()

    def aliases(self, first_in, first_out):
        return {first_in + j: first_out + j for j in range(len(self.arrays))} if self.aliased else {}

    def run(self, name):
        def body(*refs):
            srcs, dsts, sems = self.split(refs)
            self.start(srcs, dsts, sems)
            self.wait(srcs, dsts, sems)

        n = len(self.arrays)
        return pl.pallas_call(body, name=name, in_specs=[_ANY] * n, out_specs=[_ANY] * n, out_shape=self.out_shapes,
                              scratch_shapes=self.scratch, input_output_aliases=self.aliases(0, 0))(*self.arrays)


def exchange(arrays, scatter, peers, by_chip=False):
    slots = N_CHIPS if by_chip else N_DEV

    def plan(srcs, dsts, send_sems, recv_sems, local_sems):
        x, y, c, me = _place()
        mine = 2 * x + y if by_chip else me
        local = [pltpu.make_async_copy(src.at[mine] if scatter else src, dst.at[mine], local_sems.at[j])
                 for j, (src, dst) in enumerate(zip(srcs, dsts))]
        sends, recvs = [], []
        for t, k in enumerate(peers):
            peer, pidx = _peer(x, y, c, k)
            theirs = 2 * peer[0] + peer[1] if by_chip else pidx
            for j, (src, dst) in enumerate(zip(srcs, dsts)):
                part = src.at[theirs] if scatter else src
                sems = dict(send_sem=send_sems.at[j, t], recv_sem=recv_sems.at[j, t], device_id=peer, device_id_type=_MESH)
                sends.append(pltpu.make_async_remote_copy(src_ref=part, dst_ref=dst.at[mine], **sems))
                recvs.append(pltpu.make_async_remote_copy(src_ref=part, dst_ref=dst.at[theirs], **sems))
        return local, sends, recvs

    shapes = [jax.ShapeDtypeStruct(a.shape if scatter else (slots,) + a.shape, a.dtype) for a in arrays]
    return Exchange(arrays, shapes, len(peers), plan)


def second_hop(gathered):
    def plan(srcs, dsts, send_sems, recv_sems, local_sems):
        x, y, c, me = _place()
        sibling, _ = _peer(x, y, c, 1)
        sends, recvs = [], []
        for t, k in enumerate(CHIP_PEERS):
            _, landed = _peer(x, y, c, k)
            _, coming = _peer(x, y, c, k ^ 1)
            for j, buf in enumerate(dsts):
                sems = dict(send_sem=send_sems.at[j, t], recv_sem=recv_sems.at[j, t], device_id=sibling, device_id_type=_MESH)
                sends.append(pltpu.make_async_remote_copy(src_ref=buf.at[landed], dst_ref=buf.at[landed], **sems))
                recvs.append(pltpu.make_async_remote_copy(src_ref=buf.at[coming], dst_ref=buf.at[coming], **sems))
        return [], sends, recvs

    shapes = [jax.ShapeDtypeStruct(a.shape, a.dtype) for a in gathered]
    return Exchange(gathered, shapes, len(CHIP_PEERS), plan, aliased=True)


FIRST_HOP_PEERS = (1,) + CHIP_PEERS


def _gather_two_level(arrays, name):
    n = len(arrays)

    def body(*refs):
        srcs, dsts, (send_sems, recv_sems, local_sems) = refs[:n], refs[n:2 * n], refs[2 * n:]
        x, y, c, me = _place()
        sibling, sidx = _peer(x, y, c, 1)

        def copy(j, sem, block, to, src=None):
            rows = dsts[j].at[block]
            return pltpu.make_async_remote_copy(src_ref=rows if src is None else src, dst_ref=rows, send_sem=send_sems.at[j, sem],
                                                recv_sem=recv_sems.at[j, sem], device_id=to, device_id_type=_MESH)

        local = [pltpu.make_async_copy(srcs[j], dsts[j].at[me], local_sems.at[j]) for j in range(n)]
        for cp in local:
            cp.start()
        first = [copy(j, 1 + t, me, _peer(x, y, c, k)[0], src=srcs[j]) for t, k in enumerate(CHIP_PEERS) for j in range(n)]
        first += [copy(j, 0, me, sibling, src=srcs[j]) for j in range(n)]
        for cp in first:
            cp.start()
        passed = []
        for t, k in enumerate(CHIP_PEERS):
            peer, pidx = _peer(x, y, c, k)
            for j in range(n):
                copy(j, 1 + t, pidx, peer).wait_recv()
                passed.append(copy(j, 4 + t, pidx, sibling))
                passed[-1].start()
        for j in range(n):
            copy(j, 0, sidx, sibling).wait_recv()
        for t, k in enumerate(CHIP_PEERS):
            _, pidx = _peer(x, y, c, k ^ 1)
            for j in range(n):
                copy(j, 4 + t, pidx, sibling).wait_recv()
        for cp in first + passed:
            cp.wait_send()
        for cp in local:
            cp.wait()

    shapes = [jax.ShapeDtypeStruct((N_DEV,) + a.shape, a.dtype) for a in arrays]
    return pl.pallas_call(body, name=name, in_specs=[_ANY] * n, out_specs=[_ANY] * n, out_shape=shapes,
                          scratch_shapes=_sem_scratch(n, 1 + 2 * len(CHIP_PEERS)))(*arrays)


def _to_sibling(arrays, name):
    n = len(arrays)

    def body(*refs):
        srcs, dsts, (send_sems, recv_sems) = refs[:n], refs[n:2 * n], refs[2 * n:]
        x, y, c, _ = _place()
        sibling, _ = _peer(x, y, c, 1)
        copies = [pltpu.make_async_remote_copy(src_ref=srcs[j].at[1 - c], dst_ref=dsts[j], send_sem=send_sems.at[j],
                                               recv_sem=recv_sems.at[j], device_id=sibling, device_id_type=_MESH) for j in range(n)]
        for cp in copies:
            cp.start()
        for cp in copies:
            cp.wait()

    shapes = [jax.ShapeDtypeStruct(a.shape[1:], a.dtype) for a in arrays]
    return pl.pallas_call(body, name=name, in_specs=[_ANY] * n, out_specs=[_ANY] * n, out_shape=shapes,
                          scratch_shapes=[pltpu.SemaphoreType.DMA((n,)), pltpu.SemaphoreType.DMA((n,))])(*arrays)


def pair_add(own, theirs, core, *, name):
    _, ns, r, c = own.shape
    rb = _row_block(r, c)

    def body(core_ref, a_ref, b_ref, o_ref):
        o_ref[...] = (a_ref[...].astype(F32) + b_ref[...].astype(F32)).astype(o_ref.dtype)

    return pl.pallas_call(
        body, name=name,
        grid_spec=pltpu.PrefetchScalarGridSpec(
            num_scalar_prefetch=1, grid=(ns, r // rb),
            in_specs=[pl.BlockSpec((None, None, rb, c), lambda i, j, core_ref: (core_ref[0], i, j, 0)),
                      pl.BlockSpec((None, rb, c), lambda i, j, core_ref: (i, j, 0))],
            out_specs=pl.BlockSpec((None, rb, c), lambda i, j, core_ref: (i, j, 0))),
        out_shape=jax.ShapeDtypeStruct((ns, r, c), own.dtype), compiler_params=_cparams(),
    )(core, own, theirs)


ADAMW_BLOCK_BYTES = 1 << 20


def _row_block(r, c):
    for rb in range(r, 0, -1):
        if r % rb == 0 and (rb % 16 == 0 or rb == r) and rb * c * 4 <= ADAMW_BLOCK_BYTES:
            return rb
    return r


def adamw(w, m, v, parts, *, name):
    depth, r, c = w.shape
    n_parts = parts[0].shape[0]
    rb = _row_block(r, c)

    def body(w_ref, m_ref, v_ref, *refs):
        p_refs, (g_ref, d_ref, nm_ref, nv_ref) = refs[:depth], refs[depth:]

        def total(p_ref):
            g = p_ref[0].astype(F32)
            for j in range(1, n_parts):
                g = g + p_ref[j].astype(F32)
            return g

        g = total(p_refs[0])
        for l in range(1, depth):
            g = jnp.where(pl.program_id(0) == l, total(p_refs[l]), g)
        g_ref[...] = g
        m_new = ADAM_B1 * m_ref[...] + (1.0 - ADAM_B1) * g
        v_new = ADAM_B2 * v_ref[...] + (1.0 - ADAM_B2) * (g * g)
        m_hat = m_new / (1.0 - ADAM_B1 ** ADAM_STEP)
        v_hat = v_new / (1.0 - ADAM_B2 ** ADAM_STEP)
        d_ref[...] = -ADAM_LR * (m_hat / (jnp.sqrt(v_hat) + ADAM_EPS) + ADAM_WD * w_ref[...])
        nm_ref[...] = m_new
        nv_ref[...] = v_new

    wblk = pl.BlockSpec((None, rb, c), lambda l, i: (l, i, 0))
    pblk = pl.BlockSpec((n_parts, rb, c), lambda l, i: (0, i, 0))
    return pl.pallas_call(
        body, name=name, grid=(depth, r // rb),
        in_specs=[wblk, wblk, wblk] + [pblk] * depth, out_specs=[wblk] * 4,
        out_shape=[jax.ShapeDtypeStruct((depth, r, c), F32)] * 4, compiler_params=_cparams(),
    )(w, m, v, *parts)


BIG = (("w_in", 2), ("w_uq", 2), ("w_ukv", 2), ("w_pa", 2), ("w_pb", 1), ("w_o", 1), ("w_gate", 2), ("w_up", 2), ("w_down", 1))
SMALL = ("norm_mix_g", "pool_w", "pool_scale", "q_norm_g", "kv_norm_g", "norm_ffn_g", "final_norm_g")
WEIGHTS = ("meta_tokens", "norm_mix_g", "w_in", "pool_w", "pool_scale", "q_norm_g", "kv_norm_g", "w_uq", "w_ukv", "w_pa", "w_pb",
           "w_o", "norm_ffn_g", "w_gate", "w_up", "w_down", "final_norm_g")
HEAD_QK = QK_NOPE + QK_ROPE
KR_END = Z_KR + QK_ROPE


def _cat_cols(parts):
    return [jnp.concatenate(parts, axis=1)]


def _cat_rows(parts):
    return [jnp.concatenate(parts, axis=0)]


def _arr_w_in(parts):
    full = jnp.concatenate(parts, axis=1)
    zc = lambda n: jnp.zeros((full.shape[0], n), full.dtype)
    return [jnp.concatenate([full[:, :Z_KR], zc(QK_NOPE), full[:, Z_KR:KR_END], zc(LANES - HEAD_QK), full[:, KR_END:]], axis=1)]


def _arr_w_uq(parts):
    full = jnp.concatenate(parts, axis=1)
    z = jnp.zeros((full.shape[0], HEAD_SLOT - HEAD_QK), full.dtype)
    pieces = []
    for hd in range(N_HEADS):
        pieces += [full[:, hd * HEAD_QK:(hd + 1) * HEAD_QK], z]
    return [jnp.concatenate(pieces, axis=1)]


def _arr_w_ukv(parts):
    full = jnp.concatenate(parts, axis=1)
    z = jnp.zeros((full.shape[0], HEAD_SLOT - QK_NOPE), full.dtype)
    wide = QK_NOPE + V_DIM
    k, v = [], []
    for hd in range(N_HEADS):
        k += [full[:, hd * wide:hd * wide + QK_NOPE], z]
        v.append(full[:, hd * wide + QK_NOPE:(hd + 1) * wide])
    return [jnp.concatenate(k, axis=1), jnp.concatenate(v, axis=1)]


def arrange(g, fn, out_shapes, name):
    def body(g_ref, *o_refs):
        for o_ref, val in zip(o_refs, fn([g_ref[p] for p in range(N_DEV)])):
            o_ref[...] = val

    return pl.pallas_call(
        body, name=name, grid=(1,),
        in_specs=[pl.BlockSpec(g.shape, lambda i: (0, 0, 0))],
        out_specs=[pl.BlockSpec(s, lambda i: (0, 0)) for s in out_shapes],
        out_shape=[jax.ShapeDtypeStruct(s, g.dtype) for s in out_shapes], compiler_params=_cparams(),
    )(g)


def _arranged_ranges(lo, hi):
    out = []
    for a, b, shift in ((0, Z_KR, 0), (Z_KR, KR_END, QK_NOPE), (KR_END, D_IN, LANES - QK_ROPE)):
        s, e = max(lo, a), min(hi, b)
        if s < e:
            out.append((s + shift, e + shift))
    return out


def _chunks_w_in(acc):
    cs = D_IN // N_DEV
    return [jnp.concatenate([acc[:, a:b] for a, b in _arranged_ranges(p * cs, (p + 1) * cs)], axis=1) for p in range(N_DEV)]


def _chunks_w_uq(acc):
    per = N_HEADS // N_DEV
    return [jnp.concatenate([acc[:, hd * HEAD_SLOT:hd * HEAD_SLOT + HEAD_QK] for hd in range(p * per, (p + 1) * per)], axis=1)
            for p in range(N_DEV)]


def _chunks_w_ukv(acc_k, acc_v):
    per = N_HEADS // N_DEV
    out = []
    for p in range(N_DEV):
        pieces = []
        for hd in range(p * per, (p + 1) * per):
            pieces += [acc_k[:, hd * HEAD_SLOT:hd * HEAD_SLOT + QK_NOPE], acc_v[:, hd * V_DIM:(hd + 1) * V_DIM]]
        out.append(jnp.concatenate(pieces, axis=1))
    return out


def _chunks_cols(acc):
    cs = acc.shape[1] // N_DEV
    return [acc[:, p * cs:(p + 1) * cs] for p in range(N_DEV)]


def _chunks_rows(acc):
    rs = acc.shape[0] // N_DEV
    return [acc[p * rs:(p + 1) * rs, :] for p in range(N_DEV)]


def _chunks_cols_transposed(acc):
    at = acc[...].T
    rs = at.shape[0] // N_DEV
    return [at[p * rs:(p + 1) * rs, :] for p in range(N_DEV)]


def _pack(parts, row_multiple):
    flat = jnp.concatenate([p.reshape(-1) for p in parts])
    return jnp.pad(flat, (0, -flat.shape[0] % (row_multiple * LANES))).reshape(-1, LANES)


def _unpack(packed, shapes):
    flat, out, off = packed.reshape(-1), [], 0
    for s in shapes:
        n = 1
        for d in s:
            n *= d
        out.append(flat[off:off + n].reshape(s))
        off += n
    return out


def _rope_table(lp, nb):
    inv = 1.0 / (ROPE_THETA ** (jnp.arange(0, QK_ROPE, 2, dtype=F32) / QK_ROPE))
    ang = jnp.arange(lp, dtype=F32)[:, None] * inv[None, :]
    cos, sin = jnp.cos(ang), jnp.sin(ang)
    z = lambda n: jnp.zeros((lp, n), F32)
    tail = LANES - QK_NOPE - QK_ROPE
    c = jnp.concatenate([jnp.ones((lp, QK_NOPE), F32), cos, cos, z(tail)], axis=1)
    cr = jnp.concatenate([z(QK_NOPE), cos, cos, z(tail)], axis=1)
    s1 = jnp.concatenate([z(QK_NOPE), -sin, z(HALF_ROPE), z(tail)], axis=1)
    s2 = jnp.concatenate([z(QK_NOPE), z(HALF_ROPE), sin, z(tail)], axis=1)
    return jnp.tile(jnp.concatenate([c, cr, s1, s2], axis=1), (nb, 1))


MIX = ("w_in", "w_uq", "w_ukv", "w_pa", "w_pb", "w_o")
FFN = ("w_gate", "w_up", "w_down")
TRANSPOSED = ("w_gate", "w_up")
ARRANGERS = {
    "w_in": (_arr_w_in, (("win", (D_MODEL, DZ)),)), "w_uq": (_arr_w_uq, (("wuq", (Q_RANK, QK_WIDTH)),)),
    "w_ukv": (_arr_w_ukv, (("wuk", (KV_RANK, QK_WIDTH)), ("wuv", (KV_RANK, D_MODEL)))),
    "w_pa": (_cat_cols, (("wpa", (POOL_WIDTH, D_MODEL)),)), "w_pb": (_cat_rows, (("wpb", (D_MODEL, D_MODEL)),)),
    "w_o": (_cat_rows, (("wo", (D_MODEL, D_MODEL)),)), "w_gate": (_cat_rows, (("wgt", (D_FF, D_MODEL)),)),
    "w_up": (_cat_rows, (("wut", (D_FF, D_MODEL)),)), "w_down": (_cat_rows, (("wd", (D_FF, D_MODEL)),)),
}


def _operands(gathered, names, l):
    p = {}
    for n in names:
        fn, outs = ARRANGERS[n]
        for (key, _), a in zip(outs, arrange(gathered[n], fn, [s for _, s in outs], f"arrange_{n}_{l}")):
            p[key] = a
    return p


def _small_operands(small, l):
    pw = small["pool_w"][l].astype(BF16)
    return dict(g_mix=small["norm_mix_g"][l][None], gq=small["q_norm_g"][l][None], gkv=small["kv_norm_g"][l][None],
                g_ffn=small["norm_ffn_g"][l][None], ps=small["pool_scale"][l][None], pw=pw)


class MeshComm:
    def __init__(self, w, meta_tokens):
        self.src = lambda n, l: w[n][l].astype(BF16)
        self.meta_tokens = meta_tokens
        self.core = lax.axis_index("c").astype(jnp.int32).reshape(1)
        self.rides = {0: [(n, 0) for n in FFN] + [(n, 1) for n in MIX], 1: [(n, 1) for n in FFN]}

    def first_weights(self):
        got = _gather_two_level([self.src(n, 0) for n in MIX] + [self.meta_tokens], "gather_mix_0")
        return dict(zip(MIX, got)), jnp.moveaxis(got[-1], 0, 1).reshape(N_META, D_MODEL)

    def first_hop(self, l):
        return exchange([self.src(n, layer) for n, layer in self.rides[l]], False, FIRST_HOP_PEERS)

    def second_hop(self, l, landed):
        return second_hop(landed)

    def carried(self, l, full, names, layer):
        return {n: full[self.rides[l].index((n, layer))] for n in names}

    def pair_sums(self, own, names, tag):
        theirs = _to_sibling(own, f"pair_grads_{tag}")
        return [pair_add(a, b, self.core, name=f"pair_add_{n}_{tag}") for n, a, b in zip(names, own, theirs)]

    def scatter(self, sums):
        return exchange(sums, True, CHIP_PEERS, by_chip=True)

    def scatter_now(self, sums, name):
        return self.scatter(sums).run(name)


HEADS_FWD, HEADS_BWD = 8, 4
TILE_ROWS, TILE_ROWS_BWD = 512, 256


def _tile(t, target):
    n = max(1, -(-t // (target + target // 8)))
    while t % n or (t // n) % 16:
        n += 1
    return t // n


def _wgrad_tile(t):
    return max(tm for tm in (2 * TQ, TQ, LANES) if t % tm == 0)


def _ffn_bwd_part(dh2, p, s, tag):
    d, ff = D_MODEL, D_FF // N_DEV
    t = dh2.shape[0]
    wg_ = lambda n, x, ys, fn, shape: wgrad(x, ys, fn, shape, tm=_wgrad_tile(t), name=f"wgrad_{n}_{tag}")
    dh1, hn2, act, dgt, dup, dg_ffn = ffn_bwd(dh2, s["h1"], p["g_ffn"], s["gt"], s["up"], p["wgt"], p["wut"], p["wd"],
                                              tm=_tile(t, TILE_ROWS_BWD), name=f"ffn_bwd_{tag}")
    chunks = [wg_("gate", hn2, [dgt], _chunks_cols_transposed, (ff, d)), wg_("up", hn2, [dup], _chunks_cols_transposed, (ff, d)),
              wg_("down", act, [dh2], _chunks_rows, (ff, d))]
    return dh1, chunks, dict(norm_ffn_g=dg_ffn[0])


def _mix_bwd_part(dh1, p, s, rope, nb, lp, tag, ride):
    d = D_MODEL
    t = dh1.shape[0]
    wg_ = lambda n, x, ys, fn, shape: wgrad(x, ys, fn, shape, tm=_wgrad_tile(t), name=f"wgrad_{n}_{tag}")
    dga, dgb, dpa, dpb, do, dpool, dps, dpw = merge_bwd(dh1, s["z"], s["pa"], s["pb"], s["pooled"], p["pw"], p["ps"],
                                                        p["wpa"], p["wpb"], p["wo"], tm=_tile(t, TILE_ROWS),
                                                        name=f"merge_bwd_{tag}")
    c_o = wg_("o", s["mg"], [dh1], _chunks_rows, (d // N_DEV, d))
    c_pa = wg_("pa", s["a"], [dpa], _chunks_cols, (POOL_WIDTH, d // N_DEV))
    c_pb = wg_("pb", s["o"], [dpb], _chunks_rows, (d // N_DEV, d))
    (dq, dk, dv), brought = attn_bwd(s["q"], s["k"], s["v"], s["o"], do, s["lse"], nb=nb, lp=lp, hb=HEADS_BWD,
                                     name=f"attn_bwd_{tag}", ride=ride)
    dh, hn, dz, cqn, ckvn, dqb, dkb, dvb, dg_mix, dgq, dgkv = in_proj_bwd(
        dh1, s["h"], p["g_mix"], s["z"], dq, dk, dv, dga, dgb, dpool, p["win"], p["gq"], p["gkv"], p["wuq"], p["wuk"], p["wuv"],
        rope, tm=_tile(t, TILE_ROWS_BWD), lp=lp, nb=nb, name=f"in_proj_bwd_{tag}")
    c_in = wg_("in", hn, [dz], _chunks_w_in, (d, D_IN // N_DEV))
    c_uq = wg_("uq", cqn, [dqb], _chunks_w_uq, (Q_RANK, N_HEADS * HEAD_QK // N_DEV))
    c_ukv = wg_("ukv", ckvn, [dkb, dvb], _chunks_w_ukv, (KV_RANK, N_HEADS * (QK_NOPE + V_DIM) // N_DEV))
    small = dict(pool_scale=dps[0], pool_w=dpw, norm_mix_g=dg_mix[0], q_norm_g=dgq[0], kv_norm_g=dgkv[0])
    return dh, [c_in, c_uq, c_ukv, c_pa, c_pb, c_o], small, brought


def train_step(x, loss_target, small, comm):
    nb, seq, d = x.shape
    lp = -(-(N_META + seq) // LANES) * LANES
    t = nb * lp
    assert nb <= 2 and DEPTH == 2
    tm = _tile(t, TILE_ROWS)
    rope = _rope_table(lp, nb)
    gathered, meta = comm.first_weights()
    pad = jnp.zeros((nb, lp - N_META - seq, d), F32)
    h = jnp.concatenate([jnp.broadcast_to(meta[None], (nb, N_META, d)), x, pad], axis=1).reshape(t, d)
    target = jnp.concatenate([jnp.zeros((nb, N_META, d), F32), loss_target, pad], axis=1).reshape(t, d)

    params, saved, full = [], [], {}
    for l in range(DEPTH):
        p = _small_operands(small, l)
        p.update(_operands(gathered if l == 0 else comm.carried(0, full[0], MIX, 1), MIX, l))
        z, q, k, v = in_proj_fwd(h, p["g_mix"], p["win"], p["gq"], p["gkv"], p["wuq"], p["wuk"], p["wuv"], rope, tm=tm,
                                 name=f"in_proj_fwd_{l}")
        (o, lse), landed = attn_fwd(q, k, v, nb=nb, lp=lp, hb=HEADS_FWD, name=f"attn_fwd_{l}", ride=comm.first_hop(l))
        (h1, pooled, a, pa, pb, mg), full[l] = merge_fwd(h, z, o, p["pw"], p["ps"], p["wpa"], p["wpb"], p["wo"], tm=tm, lp=lp,
                                                          nb=nb, name=f"merge_fwd_{l}", ride=comm.second_hop(l, landed))
        p.update(_operands(comm.carried(l, full[l], FFN, l), FFN, l))
        h2, gt, up = ffn_fwd(h1, p["g_ffn"], p["wgt"], p["wut"], p["wd"], tm=tm, name=f"ffn_fwd_{l}")
        params.append(p)
        saved.append(dict(h=h, z=z, q=q, k=k, v=v, o=o, lse=lse, h1=h1, pooled=pooled, a=a, pa=pa, pb=pb, mg=mg, gt=gt, up=up))
        h = h2
    parts, dh, dgf = loss_head(h, small["final_norm_g"][None], target, tm=tm, lp=lp, nb=nb, seq=seq, name="loss_head")
    loss = jnp.sum(parts[::8, 0])

    sums = {}
    dh, c_ffn1, small1 = _ffn_bwd_part(dh, params[1], saved[1], 1)
    s_ffn1 = comm.pair_sums(c_ffn1, FFN, "ffn_1")
    dh, c_mix1, sm, brought = _mix_bwd_part(dh, params[1], saved[1], rope, nb, lp, 1, comm.scatter(s_ffn1))
    small1.update(sm)
    sums.update({(n, 1): a for n, a in zip(FFN, brought)})
    s_mix1 = comm.pair_sums(c_mix1, MIX, "mix_1")
    dh, c_ffn0, small0 = _ffn_bwd_part(dh, params[0], saved[0], 0)
    s_ffn0 = comm.pair_sums(c_ffn0, FFN, "ffn_0")
    dh, c_mix0, sm, brought = _mix_bwd_part(dh, params[0], saved[0], rope, nb, lp, 0, comm.scatter(s_mix1 + s_ffn0))
    small0.update(sm)
    sums.update({(n, l): a for (n, l), a in zip([(n, 1) for n in MIX] + [(n, 0) for n in FFN], brought)})
    dh = dh.reshape(nb, lp, d)
    dmeta = jnp.sum(dh[:, :N_META], axis=0)
    meta_chunks = jnp.transpose(dmeta.reshape(N_META, N_CHIPS, 2, d // N_DEV), (2, 1, 0, 3)).astype(BF16)
    s_last = comm.pair_sums(c_mix0 + [meta_chunks], MIX + ("meta_tokens",), "mix_0")
    last = comm.scatter_now(s_last, "scatter_mix_0")
    sums.update({(n, 0): a for n, a in zip(MIX + ("meta_tokens",), last)})
    small_grads = {n: jnp.stack([small0[n], small1[n]]) for n in small0}
    small_grads["final_norm_g"] = dgf[0]
    return loss, dh[:, N_META:N_META + seq], sums, small_grads


def kernel(x, meta_tokens, norm_mix_g, w_in, pool_w, pool_scale, q_norm_g, kv_norm_g, w_uq, w_ukv, w_pa, w_pb, w_o, norm_ffn_g, w_gate, w_up, w_down, final_norm_g, loss_target, m_meta_tokens, m_norm_mix_g, m_w_in, m_pool_w, m_pool_scale, m_q_norm_g, m_kv_norm_g, m_w_uq, m_w_ukv, m_w_pa, m_w_pb, m_w_o, m_norm_ffn_g, m_w_gate, m_w_up, m_w_down, m_final_norm_g, v_meta_tokens, v_norm_mix_g, v_w_in, v_pool_w, v_pool_scale, v_q_norm_g, v_kv_norm_g, v_w_uq, v_w_ukv, v_w_pa, v_w_pb, v_w_o, v_norm_ffn_g, v_w_gate, v_w_up, v_w_down, v_final_norm_g):
    args = dict(locals())
    w = {n: args[n] for n in WEIGHTS}
    m = {n: args["m_" + n] for n in WEIGHTS}
    v = {n: args["v_" + n] for n in WEIGHTS}
    small = {n: w[n] for n in SMALL}
    as_handled = lambda a, n: jnp.swapaxes(a, 1, 2) if n in TRANSPOSED else a
    wh, mh, vh = ({n: as_handled(d[n], n) for n, _ in BIG} for d in (w, m, v))

    loss, grad_x, sums, small_grads = train_step(x, loss_target, small, MeshComm(wh, meta_tokens))
    loss = lax.psum(loss, ("x", "y", "c"))
    (small_recv,) = exchange([_pack([small_grads[n] for n in SMALL], 8)], False, ALL_PEERS).run("gather_small_grads")

    out = {n: [as_handled(a, n) for a in adamw(wh[n], mh[n], vh[n], [sums[(n, l)] for l in range(DEPTH)], name=f"adamw_{n}")]
           for n, _ in BIG}
    out["meta_tokens"] = [a[0] for a in adamw(meta_tokens[None], m["meta_tokens"][None], v["meta_tokens"][None],
                                              [sums[("meta_tokens", 0)]], name="adamw_meta_tokens")]
    pk = lambda d: _pack([d[n] for n in SMALL], 8)[None]
    packed = adamw(pk(w), pk(m), pk(v), [small_recv], name="adamw_small")
    shapes = [w[n].shape for n in SMALL]
    for n, *kinds in zip(SMALL, *[_unpack(packed[kind][0], shapes) for kind in range(4)]):
        out[n] = kinds
    return (loss, grad_x, *[out[n][kind] for kind in range(4) for n in WEIGHTS])
```

```python
import functools
import math

import jax
import jax.numpy as jnp
from jax import lax
from jax.experimental import pallas as pl
from jax.experimental.pallas import tpu as pltpu

F32, BF16 = jnp.float32, jnp.bfloat16

D_MODEL = 1024
N_META = 16
N_HEADS = 16
QK_NOPE, QK_ROPE, V_DIM = 64, 32, 64
HALF_ROPE = QK_ROPE // 2
Q_RANK, KV_RANK = 256, 128
POOL_WINDOWS = (2, 4, 8, 16)
POOL_GROUP = 128
POOL_WIDTH = POOL_GROUP * len(POOL_WINDOWS)
POOL_HALO = 16
D_FF = 2816
D_IN = 2976
NORM_EPS = 1e-6
SM_SCALE = (QK_NOPE + QK_ROPE) ** -0.5
LOG2E = math.log2(math.e)
EXP2_SCALE = SM_SCALE * LOG2E
MASK_VALUE = -1e30
ROPE_THETA = 10000.0
DEPTH = 2
N_DEV = 8

ADAM_LR, ADAM_B1, ADAM_B2, ADAM_EPS, ADAM_WD, ADAM_STEP = 0.001, 0.9, 0.999, 1e-08, 0.01, 10

LANES = 128
HEAD_SLOT = LANES
QK_WIDTH = N_HEADS * HEAD_SLOT
Z_CQ, Z_CKV, Z_KR, Z_GA, Z_GB, DZ = 512, 768, 896, 1024, 2048, 3072
TQ = TK = 256
VMEM_LIMIT = 56 * 1024 * 1024


def _cparams():
    return pltpu.CompilerParams(vmem_limit_bytes=VMEM_LIMIT)


def _rows(tm, width, col=0):
    return pl.BlockSpec((tm, width), lambda i: (i, col))


def _whole(shape):
    zeros = (0,) * len(shape)
    return pl.BlockSpec(shape, lambda i: zeros, pipeline_mode=pl.Buffered(1))


def _acc(shape):
    zeros = (0,) * len(shape)
    return pl.BlockSpec(shape, lambda i: zeros)


def _dot(a, b):
    return jnp.dot(a, b, preferred_element_type=F32)


def _dot_tn(a, b):
    return lax.dot_general(a, b, (((0,), (0,)), ((), ())), preferred_element_type=F32)


def _dot_nt(a, b):
    return lax.dot_general(a, b, (((1,), (1,)), ((), ())), preferred_element_type=F32)


def _rms(x):
    r = lax.rsqrt(jnp.mean(x * x, axis=-1, keepdims=True) + NORM_EPS)
    return x * r, r


def _rms_bwd(dy, xhat, r, g):
    dg = jnp.sum(dy * xhat, axis=0, keepdims=True)
    dxh = dy * g
    dx = r * (dxh - xhat * jnp.mean(dxh * xhat, axis=-1, keepdims=True))
    return dx, dg


def _sigmoid(x):
    return 1.0 / (1.0 + jnp.exp(-x))


def _rope_fwd(q, c, s1, s2):
    w = q.shape[1]
    return q * c + pltpu.roll(q, w - HALF_ROPE, 1) * s1 + pltpu.roll(q, HALF_ROPE, 1) * s2


def _rope_bwd(dq, c, s1, s2):
    w = dq.shape[1]
    return dq * c + pltpu.roll(dq * s1, HALF_ROPE, 1) + pltpu.roll(dq * s2, w - HALF_ROPE, 1)


def _rope_tables(rope, reps):
    c, cr, s1, s2 = (rope[:, k * LANES:(k + 1) * LANES] for k in range(4))
    if reps > 1:
        return jnp.tile(c, (1, reps)), jnp.tile(s1, (1, reps)), jnp.tile(s2, (1, reps))
    return cr, s1, s2


def _seq_pos(gi, lp, nb):
    pos = gi
    for b in range(1, nb):
        pos = jnp.where(gi >= b * lp, gi - b * lp, pos)
    return pos


_ANY = pl.BlockSpec(memory_space=pl.ANY)


def _carrying_call(body, ride, operands, *, name, grid, in_specs, out_specs, out_shape, scratch_shapes=()):
    n_in, n_out = len(in_specs), len(out_specs)
    if ride is None:
        out = pl.pallas_call(body, name=name, grid=grid, in_specs=in_specs, out_specs=out_specs, out_shape=out_shape,
                             scratch_shapes=list(scratch_shapes), compiler_params=_cparams())(*operands)
        return out, []
    ne = len(ride.arrays)

    def carrying(*refs):
        ins, r_in, rest = refs[:n_in], refs[n_in:n_in + ne], refs[n_in + ne:]
        outs, r_out, rest = rest[:n_out], rest[n_out:n_out + ne], rest[n_out + ne:]
        scratch, sems = rest[:len(scratch_shapes)], rest[len(scratch_shapes):]
        ids = [pl.program_id(a) for a in range(len(grid))]
        first = functools.reduce(jnp.logical_and, [i == 0 for i in ids])
        last = functools.reduce(jnp.logical_and, [i == g - 1 for i, g in zip(ids, grid)])

        @pl.when(first)
        def _():
            ride.start(r_in, r_out, sems)

        body(*ins, *outs, *scratch)

        @pl.when(last)
        def _():
            ride.wait(r_in, r_out, sems)

    out = pl.pallas_call(
        carrying, name=name, grid=grid, in_specs=list(in_specs) + [_ANY] * ne, out_specs=list(out_specs) + [_ANY] * ne,
        out_shape=list(out_shape) + ride.out_shapes, scratch_shapes=list(scratch_shapes) + ride.scratch,
        input_output_aliases=ride.aliases(n_in, n_out), compiler_params=_cparams(),
    )(*operands, *ride.arrays)
    return out[:n_out], out[n_out:]


def in_proj_fwd(h, g_mix, win, gq, gkv, wuq, wuk, wuv, rope, *, tm, name):
    t = h.shape[0]

    def body(h_ref, g_ref, win_ref, gq_ref, gkv_ref, wuq_ref, wuk_ref, wuv_ref, rope_ref, z_ref, q_ref, k_ref, v_ref):
        xhat, _ = _rms(h_ref[...])
        hn = (xhat * g_ref[...]).astype(BF16)
        z = _dot(hn, win_ref[...])
        z_ref[...] = z
        rope_t = rope_ref[...]
        xq, _ = _rms(z[:, Z_CQ:Z_CKV])
        cqn = (xq * gq_ref[...]).astype(BF16)
        q = _rope_fwd(_dot(cqn, wuq_ref[...]), *_rope_tables(rope_t, N_HEADS))
        q_ref[...] = q.astype(BF16)
        xkv, _ = _rms(z[:, Z_CKV:Z_KR])
        ckvn = (xkv * gkv_ref[...]).astype(BF16)
        kr = _rope_fwd(z[:, Z_KR:Z_GA], *_rope_tables(rope_t, 1))
        k_ref[...] = (_dot(ckvn, wuk_ref[...]) + jnp.tile(kr, (1, N_HEADS))).astype(BF16)
        v_ref[...] = _dot(ckvn, wuv_ref[...]).astype(BF16)

    return pl.pallas_call(
        body, name=name, grid=(t // tm,),
        in_specs=[_rows(tm, D_MODEL), _whole((1, D_MODEL)), _whole((D_MODEL, DZ)), _whole((1, Q_RANK)), _whole((1, KV_RANK)),
                  _whole((Q_RANK, QK_WIDTH)), _whole((KV_RANK, QK_WIDTH)), _whole((KV_RANK, D_MODEL)), _rows(tm, 4 * LANES)],
        out_specs=[_rows(tm, DZ), _rows(tm, QK_WIDTH), _rows(tm, QK_WIDTH), _rows(tm, D_MODEL)],
        out_shape=[jax.ShapeDtypeStruct((t, DZ), F32), jax.ShapeDtypeStruct((t, QK_WIDTH), BF16),
                   jax.ShapeDtypeStruct((t, QK_WIDTH), BF16), jax.ShapeDtypeStruct((t, D_MODEL), BF16)],
        compiler_params=_cparams(),
    )(h, g_mix, win, gq, gkv, wuq, wuk, wuv, rope)


def attn_fwd(q, k, v, *, nb, lp, hb, name, ride=None):
    t = q.shape[0]
    nq, tail = lp // TQ, lp % TQ
    assert tail % LANES == 0

    def body(q_ref, k_ref, v_ref, o_ref, lse_ref, vt):
        for pr in range(hb // 2):
            vt[pr] = v_ref[:, pr * LANES:(pr + 1) * LANES].T

        def q_block(qs, tq, whole_k):
            qh = [q_ref[pl.ds(qs, tq), hd * HEAD_SLOT:(hd + 1) * HEAD_SLOT] for hd in range(hb)]
            keep = lax.broadcasted_iota(jnp.int32, (tq, tq), 0) <= lax.broadcasted_iota(jnp.int32, (tq, tq), 1)

            def k_steps(blocks, c, masked):
                sts = [[_dot_nt(k_ref[pl.ds(ks, tk), hd * HEAD_SLOT:(hd + 1) * HEAD_SLOT], qh[hd]) for hd in range(hb)]
                       for ks, tk in blocks]
                for (ks, tk), st_b in zip(blocks, sts):
                    ps, stats = [], []
                    for hd in range(hb):
                        m, l, _ = c[hd]
                        st = jnp.where(keep, st_b[hd], MASK_VALUE) if masked else st_b[hd]
                        m_new = jnp.maximum(m, jnp.max(st, axis=0, keepdims=True))
                        p = jnp.exp2((st - m_new) * EXP2_SCALE)
                        alpha = jnp.exp2((m - m_new) * EXP2_SCALE)
                        ps.append(p.astype(BF16))
                        stats.append((m_new, alpha * l + jnp.sum(p, axis=0, keepdims=True), alpha))
                    pvs = [_dot(vt[hd // 2, :, pl.ds(ks, tk)], ps[hd]) for hd in range(hb)]
                    c = tuple((stats[hd][0], stats[hd][1], stats[hd][2] * c[hd][2] + pvs[hd]) for hd in range(hb))
                return c

            def two_blocks(i, c):
                ks = pl.multiple_of(2 * i * TK, TK)
                return k_steps([(ks, TK), (ks + TK, TK)], c, False)

            init = tuple((jnp.full((1, tq), MASK_VALUE, F32), jnp.zeros((1, tq), F32), jnp.zeros((LANES, tq), F32))
                         for _ in range(hb))
            pairs = lax.div(whole_k, 2)
            c = lax.fori_loop(0, pairs, two_blocks, init)
            c = lax.fori_loop(2 * pairs, whole_k, lambda kj, c: k_steps([(pl.multiple_of(kj * TK, TK), TK)], c, False), c)
            c = k_steps([(qs, tq)], c, True)
            sub = lax.broadcasted_iota(jnp.int32, (LANES, tq), 0)
            for pr in range(hb // 2):
                (m0, l0, a0), (m1, l1, a1) = c[2 * pr], c[2 * pr + 1]
                o_ref[pl.ds(qs, tq), pr * LANES:(pr + 1) * LANES] = jnp.where(sub < V_DIM, a0 / l0, a1 / l1).T.astype(BF16)
                lse_ref[2 * pr, :, pl.ds(qs, tq)] = m0 * SM_SCALE + jnp.log(l0)
                lse_ref[2 * pr + 1, :, pl.ds(qs, tq)] = m1 * SM_SCALE + jnp.log(l1)

        def whole_q_block(qi, carry):
            q_block(pl.multiple_of(qi * TQ, TQ), TQ, qi)
            return carry

        lax.fori_loop(0, nq, whole_q_block, 0)
        if tail:
            q_block(nq * TQ, tail, nq)

    blk = lambda w: pl.BlockSpec((lp, w), lambda b, g: (b, g))
    return _carrying_call(
        body, ride, (q, k, v), name=name, grid=(nb, N_HEADS // hb),
        in_specs=[blk(hb * HEAD_SLOT), blk(hb * HEAD_SLOT), blk(hb * V_DIM)],
        out_specs=[blk(hb * V_DIM), pl.BlockSpec((hb, 1, lp), lambda b, g: (g, 0, b))],
        out_shape=[jax.ShapeDtypeStruct((t, D_MODEL), BF16), jax.ShapeDtypeStruct((N_HEADS, 1, t), F32)],
        scratch_shapes=[pltpu.VMEM((hb // 2, LANES, lp), BF16)])


def _pool_band_fwd(i, tm, lp, nb):
    r = lax.broadcasted_iota(jnp.int32, (tm, POOL_HALO + tm), 0)
    e = lax.broadcasted_iota(jnp.int32, (tm, POOL_HALO + tm), 1)
    diff = r + POOL_HALO - e
    pos = _seq_pos(i * tm + lax.broadcasted_iota(jnp.int32, (tm, 1), 0), lp, nb)
    out = []
    for w in POOL_WINDOWS:
        cnt = jnp.minimum(pos + 1, w)
        band = jnp.where((diff >= 0) & (diff < cnt), 1.0, 0.0).astype(BF16)
        out.append((band, cnt.astype(F32)))
    return out


def merge_fwd(h, z, o, pw, ps, wpa, wpb, wo, *, tm, lp, nb, name, ride=None):
    t = h.shape[0]
    hb = tm // POOL_HALO

    def body(h_ref, u_ref, uprev_ref, ga_ref, gb_ref, o_ref, pw_ref, ps_ref, wpa_ref, wpb_ref, wo_ref,
             h1_ref, pooled_ref, a_ref, pa_ref, pb_ref, mg_ref):
        i = pl.program_id(0)
        u = u_ref[...]
        uext = jnp.concatenate([uprev_ref[...], u], axis=0).astype(BF16)
        pooled, ys = [], []
        for g, (band, cnt) in enumerate(_pool_band_fwd(i, tm, lp, nb)):
            gs = slice(g * POOL_GROUP, (g + 1) * POOL_GROUP)
            pg = (_dot(band, uext[:, gs]) / cnt - u[:, gs]).astype(BF16)
            pooled.append(pg)
            ys.append(_dot(pg, pw_ref[g]))
        pooled_ref[...] = jnp.concatenate(pooled, axis=1)
        a = (jnp.concatenate(ys, axis=1) * ps_ref[...]).astype(BF16)
        a_ref[...] = a
        pa = _dot(a, wpa_ref[...])
        pb = _dot(o_ref[...], wpb_ref[...])
        pa_ref[...] = pa.astype(BF16)
        pb_ref[...] = pb.astype(BF16)
        mg = (_sigmoid(ga_ref[...]) * pa + _sigmoid(gb_ref[...]) * pb).astype(BF16)
        mg_ref[...] = mg
        h1_ref[...] = h_ref[...] + _dot(mg, wo_ref[...])

    halo = pl.BlockSpec((POOL_HALO, POOL_WIDTH), lambda i: (jnp.maximum(i * hb - 1, 0), 0))
    return _carrying_call(
        body, ride, (h, z, z, z, z, o, pw, ps, wpa, wpb, wo), name=name, grid=(t // tm,),
        in_specs=[_rows(tm, D_MODEL), _rows(tm, POOL_WIDTH), halo, _rows(tm, D_MODEL, 1), _rows(tm, D_MODEL, 2), _rows(tm, D_MODEL),
                  _whole((4, POOL_GROUP, POOL_GROUP)), _whole((1, POOL_WIDTH)), _whole((POOL_WIDTH, D_MODEL)),
                  _whole((D_MODEL, D_MODEL)), _whole((D_MODEL, D_MODEL))],
        out_specs=[_rows(tm, D_MODEL), _rows(tm, POOL_WIDTH), _rows(tm, POOL_WIDTH), _rows(tm, D_MODEL), _rows(tm, D_MODEL),
                   _rows(tm, D_MODEL)],
        out_shape=[jax.ShapeDtypeStruct((t, D_MODEL), F32), jax.ShapeDtypeStruct((t, POOL_WIDTH), BF16),
                   jax.ShapeDtypeStruct((t, POOL_WIDTH), BF16), jax.ShapeDtypeStruct((t, D_MODEL), BF16),
                   jax.ShapeDtypeStruct((t, D_MODEL), BF16), jax.ShapeDtypeStruct((t, D_MODEL), BF16)])


def ffn_fwd(h1, g, wgt, wut, wd, *, tm, name):
    t = h1.shape[0]

    def body(h_ref, g_ref, wgt_ref, wut_ref, wd_ref, h2_ref, gt_ref, up_ref):
        h = h_ref[...]
        xhat, _ = _rms(h)
        hn = (xhat * g_ref[...]).astype(BF16)
        gt = _dot_nt(hn, wgt_ref[...])
        up = _dot_nt(hn, wut_ref[...])
        gt_ref[...] = gt.astype(BF16)
        up_ref[...] = up.astype(BF16)
        act = (gt * _sigmoid(gt) * up).astype(BF16)
        h2_ref[...] = h + _dot(act, wd_ref[...])

    return pl.pallas_call(
        body, name=name, grid=(t // tm,),
        in_specs=[_rows(tm, D_MODEL), _whole((1, D_MODEL)), _whole((D_FF, D_MODEL)), _whole((D_FF, D_MODEL)), _whole((D_FF, D_MODEL))],
        out_specs=[_rows(tm, D_MODEL), _rows(tm, D_FF), _rows(tm, D_FF)],
        out_shape=[jax.ShapeDtypeStruct((t, D_MODEL), F32), jax.ShapeDtypeStruct((t, D_FF), BF16), jax.ShapeDtypeStruct((t, D_FF), BF16)],
        compiler_params=_cparams(),
    )(h1, g, wgt, wut, wd)


def loss_head(h, g, target, *, tm, lp, nb, seq, name):
    t = h.shape[0]
    nt = t // tm

    def body(h_ref, g_ref, t_ref, loss_ref, dh_ref, dg_ref):
        i = pl.program_id(0)
        pos = _seq_pos(i * tm + lax.broadcasted_iota(jnp.int32, (tm, 1), 0), lp, nb)
        real = (pos >= N_META) & (pos < N_META + seq)
        xhat, r = _rms(h_ref[...])
        gg = g_ref[...]
        err = jnp.where(real, xhat * gg - t_ref[...], 0.0)
        loss_ref[...] = jnp.full((8, LANES), 0.5 * jnp.sum(err * err) / D_MODEL, F32)
        dx, dg = _rms_bwd(err * (1.0 / D_MODEL), xhat, r, gg)
        dh_ref[...] = dx

        @pl.when(i == 0)
        def _():
            dg_ref[...] = jnp.zeros_like(dg_ref)

        dg_ref[...] += dg

    return pl.pallas_call(
        body, name=name, grid=(nt,),
        in_specs=[_rows(tm, D_MODEL), _whole((1, D_MODEL)), _rows(tm, D_MODEL)],
        out_specs=[pl.BlockSpec((8, LANES), lambda i: (i, 0)), _rows(tm, D_MODEL), _acc((1, D_MODEL))],
        out_shape=[jax.ShapeDtypeStruct((nt * 8, LANES), F32), jax.ShapeDtypeStruct((t, D_MODEL), F32),
                   jax.ShapeDtypeStruct((1, D_MODEL), F32)],
        compiler_params=_cparams(),
    )(h, g, target)


def wgrad(x, ys, chunk_fn, chunk_shape, *, tm, name):
    t, m = x.shape
    tiles = t // tm
    steps = -(-tiles // 2)

    def body(*refs):
        ins, o_ref, accs = refs[:2 * (1 + len(ys))], refs[2 * (1 + len(ys))], refs[2 * (1 + len(ys)) + 1:]
        i = pl.program_id(0)

        @pl.when(i == 0)
        def _():
            for acc in accs:
                acc[...] = jnp.zeros_like(acc)

        def both(first, second, mask):
            b = second[...].astype(BF16)
            if mask and tiles % 2:
                b = jnp.where(2 * i + 1 < tiles, b, jnp.zeros_like(b))
            return jnp.concatenate([first[...].astype(BF16), b], axis=0)

        xb = both(ins[0], ins[1], True)
        for j, acc in enumerate(accs):
            acc[...] += _dot_tn(xb, both(ins[2 + 2 * j], ins[3 + 2 * j], False))

        @pl.when(i == steps - 1)
        def _():
            for p, chunk in enumerate(chunk_fn(*accs)):
                o_ref[p % 2, p // 2] = chunk.astype(BF16)

    def two_tiles(width):
        return [pl.BlockSpec((tm, width), lambda i: (2 * i, 0)),
                pl.BlockSpec((tm, width), lambda i: (jnp.minimum(2 * i + 1, tiles - 1), 0))]

    out = (2, N_DEV // 2) + tuple(chunk_shape)
    operands = [x, x] + [a for y in ys for a in (y, y)]
    return pl.pallas_call(
        body, name=name, grid=(steps,),
        in_specs=two_tiles(m) + [s for y in ys for s in two_tiles(y.shape[1])], out_specs=_acc(out),
        out_shape=jax.ShapeDtypeStruct(out, BF16), scratch_shapes=[pltpu.VMEM((m, y.shape[1]), F32) for y in ys],
        compiler_params=_cparams(),
    )(*operands)


def ffn_bwd(dh2, h1, g, gt, up, wgt, wut, wd, *, tm, name):
    t = h1.shape[0]

    def body(dh2_ref, h_ref, g_ref, gt_ref, up_ref, wgt_ref, wut_ref, wd_ref, dh1_ref, hn_ref, act_ref, dgt_ref, dup_ref, dg_ref):
        dh2 = dh2_ref[...]
        dact = _dot_nt(dh2.astype(BF16), wd_ref[...])
        gt = gt_ref[...].astype(F32)
        up = up_ref[...].astype(F32)
        sg = _sigmoid(gt)
        silu = gt * sg
        act_ref[...] = (silu * up).astype(BF16)
        dgt = (dact * up * (sg * (1.0 + gt * (1.0 - sg)))).astype(BF16)
        dup = (dact * silu).astype(BF16)
        dgt_ref[...] = dgt
        dup_ref[...] = dup
        dhn = _dot(dgt, wgt_ref[...]) + _dot(dup, wut_ref[...])
        xhat, r = _rms(h_ref[...])
        gg = g_ref[...]
        hn_ref[...] = (xhat * gg).astype(BF16)
        dx, dg = _rms_bwd(dhn, xhat, r, gg)
        dh1_ref[...] = dh2 + dx

        @pl.when(pl.program_id(0) == 0)
        def _():
            dg_ref[...] = jnp.zeros_like(dg_ref)

        dg_ref[...] += dg

    return pl.pallas_call(
        body, name=name, grid=(t // tm,),
        in_specs=[_rows(tm, D_MODEL), _rows(tm, D_MODEL), _whole((1, D_MODEL)), _rows(tm, D_FF), _rows(tm, D_FF),
                  _whole((D_FF, D_MODEL)), _whole((D_FF, D_MODEL)), _whole((D_FF, D_MODEL))],
        out_specs=[_rows(tm, D_MODEL), _rows(tm, D_MODEL), _rows(tm, D_FF), _rows(tm, D_FF), _rows(tm, D_FF), _acc((1, D_MODEL))],
        out_shape=[jax.ShapeDtypeStruct((t, D_MODEL), F32), jax.ShapeDtypeStruct((t, D_MODEL), BF16),
                   jax.ShapeDtypeStruct((t, D_FF), BF16), jax.ShapeDtypeStruct((t, D_FF), BF16),
                   jax.ShapeDtypeStruct((t, D_FF), BF16), jax.ShapeDtypeStruct((1, D_MODEL), F32)],
        compiler_params=_cparams(),
    )(dh2, h1, g, gt, up, wgt, wut, wd)


def merge_bwd(dh1, z, pa, pb, pooled, pw, ps, wpa, wpb, wo, *, tm, name):
    t = dh1.shape[0]

    def body(dh1_ref, ga_ref, gb_ref, pa_ref, pb_ref, pooled_ref, pw_ref, ps_ref, wpa_ref, wpb_ref, wo_ref,
             dga_ref, dgb_ref, dpa_ref, dpb_ref, do_ref, dpool_ref, dps_ref, dpw_ref):
        dmg = _dot_nt(dh1_ref[...].astype(BF16), wo_ref[...])
        sa = _sigmoid(ga_ref[...])
        sb = _sigmoid(gb_ref[...])
        dga_ref[...] = (dmg * pa_ref[...].astype(F32) * sa * (1.0 - sa)).astype(BF16)
        dgb_ref[...] = (dmg * pb_ref[...].astype(F32) * sb * (1.0 - sb)).astype(BF16)
        dpa = (dmg * sa).astype(BF16)
        dpb = (dmg * sb).astype(BF16)
        dpa_ref[...] = dpa
        dpb_ref[...] = dpb
        do_ref[...] = _dot_nt(dpb, wpb_ref[...]).astype(BF16)
        da = _dot_nt(dpa, wpa_ref[...])
        pooled = pooled_ref[...]
        ps = ps_ref[...]

        @pl.when(pl.program_id(0) == 0)
        def _():
            dps_ref[...] = jnp.zeros_like(dps_ref)
            dpw_ref[...] = jnp.zeros_like(dpw_ref)

        dps, dpool = [], []
        for g in range(len(POOL_WINDOWS)):
            gs = slice(g * POOL_GROUP, (g + 1) * POOL_GROUP)
            y = _dot(pooled[:, gs], pw_ref[g])
            dps.append(jnp.sum(da[:, gs] * y, axis=0, keepdims=True))
            dy = (da[:, gs] * ps[:, gs]).astype(BF16)
            dpool.append(_dot_nt(dy, pw_ref[g]))
            dpw_ref[g] += _dot_tn(pooled[:, gs], dy)
        dps_ref[...] += jnp.concatenate(dps, axis=1)
        dpool_ref[...] = jnp.concatenate(dpool, axis=1)

    return pl.pallas_call(
        body, name=name, grid=(t // tm,),
        in_specs=[_rows(tm, D_MODEL), _rows(tm, D_MODEL, 1), _rows(tm, D_MODEL, 2), _rows(tm, D_MODEL), _rows(tm, D_MODEL),
                  _rows(tm, POOL_WIDTH), _whole((4, POOL_GROUP, POOL_GROUP)),
                  _whole((1, POOL_WIDTH)), _whole((POOL_WIDTH, D_MODEL)), _whole((D_MODEL, D_MODEL)), _whole((D_MODEL, D_MODEL))],
        out_specs=[_rows(tm, D_MODEL), _rows(tm, D_MODEL), _rows(tm, D_MODEL), _rows(tm, D_MODEL), _rows(tm, D_MODEL),
                   _rows(tm, POOL_WIDTH), _acc((1, POOL_WIDTH)), _acc((4, POOL_GROUP, POOL_GROUP))],
        out_shape=[jax.ShapeDtypeStruct((t, D_MODEL), BF16)] * 5
        + [jax.ShapeDtypeStruct((t, POOL_WIDTH), F32), jax.ShapeDtypeStruct((1, POOL_WIDTH), F32),
           jax.ShapeDtypeStruct((4, POOL_GROUP, POOL_GROUP), F32)],
        compiler_params=_cparams(),
    )(dh1, z, z, pa, pb, pooled, pw, ps, wpa, wpb, wo)


def attn_bwd(q, k, v, o, do, lse, *, nb, lp, hb, name, ride=None):
    t = q.shape[0]
    nq, tail = lp // TQ, lp % TQ
    assert tail % LANES == 0

    def body(q_ref, k_ref, v_ref, o_ref, do_ref, lse_ref, dq_ref, dk_ref, dv_ref, kt, doh, lse_row, delta_row, dqt):
        lane = lax.broadcasted_iota(jnp.int32, (lp, LANES), 1)
        first = lane < V_DIM
        sub = lax.broadcasted_iota(jnp.int32, (LANES, lp), 0)
        for pr in range(hb // 2):
            ls = slice(pr * LANES, (pr + 1) * LANES)
            do = do_ref[:, ls]
            doh[2 * pr] = jnp.where(first, do, jnp.zeros_like(do))
            doh[2 * pr + 1] = jnp.where(first, jnp.zeros_like(do), do)
            prod_t = (do.astype(F32) * o_ref[:, ls].astype(F32)).T
            delta_row[2 * pr] = jnp.sum(jnp.where(sub < V_DIM, prod_t, 0.0), axis=0, keepdims=True)
            delta_row[2 * pr + 1] = jnp.sum(jnp.where(sub < V_DIM, 0.0, prod_t), axis=0, keepdims=True)
        for hd in range(hb):
            lse_row[hd] = lse_ref[hd] * LOG2E
            kt[hd] = k_ref[:, hd * HEAD_SLOT:(hd + 1) * HEAD_SLOT].T
        dqt[...] = jnp.zeros(dqt.shape, F32)
        heads = range(hb)
        hss = [slice(hd * HEAD_SLOT, (hd + 1) * HEAD_SLOT) for hd in heads]

        def k_block(ks, tk, next_q):
            keep = lax.broadcasted_iota(jnp.int32, (tk, tk), 0) <= lax.broadcasted_iota(jnp.int32, (tk, tk), 1)

            def q_steps(blocks, c, masked):
                work = [(qs, tq, hd) for qs, tq in blocks for hd in heads]
                qhs = [q_ref[pl.ds(qs, tq), hss[hd]] for qs, tq, hd in work]
                dos = [doh[hd, pl.ds(qs, tq), :] for qs, tq, hd in work]
                sts = [_dot_nt(k_ref[pl.ds(ks, tk), hss[hd]], qhs[i]) for i, (_, _, hd) in enumerate(work)]
                dpts = [_dot_nt(v_ref[pl.ds(ks, tk), (hd // 2) * LANES:(hd // 2 + 1) * LANES], dos[i])
                        for i, (_, _, hd) in enumerate(work)]
                pts, dsts = [], []
                for i, (qs, tq, hd) in enumerate(work):
                    st = jnp.where(keep, sts[i], MASK_VALUE) if masked else sts[i]
                    pt = jnp.exp2(st * EXP2_SCALE - lse_row[hd, :, pl.ds(qs, tq)])
                    dsts.append((pt * (dpts[i] - delta_row[hd, :, pl.ds(qs, tq)])).astype(BF16))
                    pts.append(pt.astype(BF16))
                dvs = [_dot(pts[i], dos[i]) for i in range(len(work))]
                dks = [_dot(dsts[i], qhs[i]) for i in range(len(work))]
                dqs = [_dot(kt[hd, :, pl.ds(ks, tk)], dsts[i]) for i, (_, _, hd) in enumerate(work)]
                c = list(c)
                for i, (qs, tq, hd) in enumerate(work):
                    dqt[hd, :, pl.ds(qs, tq)] += dqs[i]
                    c[hd] = (c[hd][0] + dks[i], c[hd][1] + dvs[i])
                return tuple(c)

            zero = jnp.zeros((tk, LANES), F32)
            c = q_steps([(ks, tk)], tuple((zero, zero) for _ in heads), True)
            if next_q is not None:
                def two_blocks(i, c):
                    qs = pl.multiple_of((next_q + 2 * i) * TQ, TQ)
                    return q_steps([(qs, TQ), (qs + TQ, TQ)], c, False)

                pairs = lax.div(nq - next_q, 2)
                c = lax.fori_loop(0, pairs, two_blocks, c)
                c = lax.fori_loop(next_q + 2 * pairs, nq, lambda qi, c: q_steps([(pl.multiple_of(qi * TQ, TQ), TQ)], c, False), c)
                if tail:
                    c = q_steps([(nq * TQ, tail)], c, False)
            for hd in heads:
                dk_ref[pl.ds(ks, tk), hss[hd]] = c[hd][0] * SM_SCALE
            for pr in range(hb // 2):
                dv_ref[pl.ds(ks, tk), pr * LANES:(pr + 1) * LANES] = c[2 * pr][1] + c[2 * pr + 1][1]

        def whole_k_block(kj, carry):
            k_block(pl.multiple_of(kj * TK, TK), TK, kj + 1)
            return carry

        lax.fori_loop(0, nq, whole_k_block, 0)
        if tail:
            k_block(nq * TQ, tail, None)
        for hd in range(hb):
            dq_ref[:, hd * HEAD_SLOT:(hd + 1) * HEAD_SLOT] = dqt[hd].T * SM_SCALE

    blk = lambda w: pl.BlockSpec((lp, w), lambda b, g: (b, g))
    return _carrying_call(
        body, ride, (q, k, v, o, do, lse), name=name, grid=(nb, N_HEADS // hb),
        in_specs=[blk(hb * HEAD_SLOT), blk(hb * HEAD_SLOT), blk(hb * V_DIM), blk(hb * V_DIM), blk(hb * V_DIM),
                  pl.BlockSpec((hb, 1, lp), lambda b, g: (g, 0, b))],
        out_specs=[blk(hb * HEAD_SLOT), blk(hb * HEAD_SLOT), blk(hb * V_DIM)],
        out_shape=[jax.ShapeDtypeStruct((t, QK_WIDTH), F32), jax.ShapeDtypeStruct((t, QK_WIDTH), F32),
                   jax.ShapeDtypeStruct((t, D_MODEL), F32)],
        scratch_shapes=[pltpu.VMEM((hb, HEAD_SLOT, lp), BF16), pltpu.VMEM((hb, lp, LANES), BF16), pltpu.VMEM((hb, 1, lp), F32),
                        pltpu.VMEM((hb, 1, lp), F32), pltpu.VMEM((hb, HEAD_SLOT, lp), F32)])


def in_proj_bwd(dh1, h, g_mix, z, dq, dk, dv, dga, dgb, dpool, win, gq, gkv, wuq, wuk, wuv, rope, *, tm, lp, nb, name):
    t = h.shape[0]
    hb = tm // POOL_HALO
    last_halo = t // POOL_HALO - 1

    def body(dh1_ref, h_ref, g_ref, zcq_ref, zckv_ref, dq_ref, dk_ref, dv_ref, dga_ref, dgb_ref, dpool_ref, dnext_ref,
             win_ref, gq_ref, gkv_ref, wuq_ref, wuk_ref, wuv_ref, rope_ref,
             dh_ref, hn_ref, dz_ref, cqn_ref, ckvn_ref, dqb_ref, dkb_ref, dvb_ref, dg_ref, dgq_ref, dgkv_ref):
        i = pl.program_id(0)
        rope_t = rope_ref[...]
        dqb = _rope_bwd(dq_ref[...], *_rope_tables(rope_t, N_HEADS)).astype(BF16)
        dqb_ref[...] = dqb
        xq, rq = _rms(zcq_ref[...])
        gq_v = gq_ref[...]
        cqn_ref[...] = (xq * gq_v).astype(BF16)
        dcq, dgq = _rms_bwd(_dot_nt(dqb, wuq_ref[...]), xq, rq, gq_v)
        dk = dk_ref[...]
        dkb = dk.astype(BF16)
        dvb = dv_ref[...].astype(BF16)
        dkb_ref[...] = dkb
        dvb_ref[...] = dvb
        xkv, rkv = _rms(zckv_ref[...])
        gkv_v = gkv_ref[...]
        ckvn_ref[...] = (xkv * gkv_v).astype(BF16)
        dckv, dgkv = _rms_bwd(_dot_nt(dkb, wuk_ref[...]) + _dot_nt(dvb, wuv_ref[...]), xkv, rkv, gkv_v)
        dks = dk[:, :HEAD_SLOT]
        for hd in range(1, N_HEADS):
            dks = dks + dk[:, hd * HEAD_SLOT:(hd + 1) * HEAD_SLOT]
        dzk = _rope_bwd(dks, *_rope_tables(rope_t, 1))
        dp_cur = dpool_ref[...]
        dp_ext = jnp.concatenate([dp_cur, dnext_ref[...]], axis=0)
        r = lax.broadcasted_iota(jnp.int32, (tm, tm + POOL_HALO), 0)
        e = lax.broadcasted_iota(jnp.int32, (tm, tm + POOL_HALO), 1)
        gt_col = i * tm + lax.broadcasted_iota(jnp.int32, (1, tm + POOL_HALO), 1)
        pos_col = _seq_pos(gt_col, lp, nb)
        gt_row = i * tm + lax.broadcasted_iota(jnp.int32, (tm + POOL_HALO, 1), 0)
        pos_row = _seq_pos(gt_row, lp, nb)
        dus = []
        for g, w in enumerate(POOL_WINDOWS):
            gs = slice(g * POOL_GROUP, (g + 1) * POOL_GROUP)
            band = jnp.where((e - r >= 0) & (e - r < jnp.minimum(pos_col + 1, w)) & (gt_col < t), 1.0, 0.0).astype(BF16)
            scaled = jnp.where(gt_row < t, dp_ext[:, gs] / jnp.minimum(pos_row + 1, w).astype(F32), 0.0).astype(BF16)
            dus.append(_dot(band, scaled) - dp_cur[:, gs])
        dz = jnp.concatenate(dus + [dcq, dckv, dzk], axis=1).astype(BF16)
        dz = jnp.concatenate([dz, dga_ref[...], dgb_ref[...]], axis=1)
        dz_ref[...] = dz
        xhat, rr = _rms(h_ref[...])
        gg = g_ref[...]
        hn_ref[...] = (xhat * gg).astype(BF16)
        dx, dg = _rms_bwd(_dot_nt(dz, win_ref[...]), xhat, rr, gg)
        dh_ref[...] = dh1_ref[...] + dx

        @pl.when(i == 0)
        def _():
            dg_ref[...] = jnp.zeros_like(dg_ref)
            dgq_ref[...] = jnp.zeros_like(dgq_ref)
            dgkv_ref[...] = jnp.zeros_like(dgkv_ref)

        dg_ref[...] += dg
        dgq_ref[...] += dgq
        dgkv_ref[...] += dgkv

    nxt = pl.BlockSpec((POOL_HALO, POOL_WIDTH), lambda i: (jnp.minimum((i + 1) * hb, last_halo), 0))
    return pl.pallas_call(
        body, name=name, grid=(t // tm,),
        in_specs=[_rows(tm, D_MODEL), _rows(tm, D_MODEL), _whole((1, D_MODEL)), _rows(tm, Q_RANK, Z_CQ // Q_RANK),
                  _rows(tm, KV_RANK, Z_CKV // KV_RANK), _rows(tm, QK_WIDTH), _rows(tm, QK_WIDTH), _rows(tm, D_MODEL),
                  _rows(tm, D_MODEL), _rows(tm, D_MODEL), _rows(tm, POOL_WIDTH), nxt,
                  _whole((D_MODEL, DZ)), _whole((1, Q_RANK)), _whole((1, KV_RANK)), _whole((Q_RANK, QK_WIDTH)),
                  _whole((KV_RANK, QK_WIDTH)), _whole((KV_RANK, D_MODEL)), _rows(tm, 4 * LANES)],
        out_specs=[_rows(tm, D_MODEL), _rows(tm, D_MODEL), _rows(tm, DZ), _rows(tm, Q_RANK), _rows(tm, KV_RANK),
                   _rows(tm, QK_WIDTH), _rows(tm, QK_WIDTH), _rows(tm, D_MODEL),
                   _acc((1, D_MODEL)), _acc((1, Q_RANK)), _acc((1, KV_RANK))],
        out_shape=[jax.ShapeDtypeStruct((t, D_MODEL), F32), jax.ShapeDtypeStruct((t, D_MODEL), BF16),
                   jax.ShapeDtypeStruct((t, DZ), BF16), jax.ShapeDtypeStruct((t, Q_RANK), BF16),
                   jax.ShapeDtypeStruct((t, KV_RANK), BF16), jax.ShapeDtypeStruct((t, QK_WIDTH), BF16),
                   jax.ShapeDtypeStruct((t, QK_WIDTH), BF16), jax.ShapeDtypeStruct((t, D_MODEL), BF16),
                   jax.ShapeDtypeStruct((1, D_MODEL), F32), jax.ShapeDtypeStruct((1, Q_RANK), F32),
                   jax.ShapeDtypeStruct((1, KV_RANK), F32)],
        compiler_params=_cparams(),
    )(dh1, h, g_mix, z, z, dq, dk, dv, dga, dgb, dpool, dpool, win, gq, gkv, wuq, wuk, wuv, rope)


_MESH = pl.DeviceIdType.MESH


def _place():
    x, y, c = lax.axis_index("x"), lax.axis_index("y"), lax.axis_index("c")
    return x, y, c, 4 * x + 2 * y + c


def _peer(x, y, c, k):
    px, py, pc = (1 - x) if k & 4 else x, (1 - y) if k & 2 else y, (1 - c) if k & 1 else c
    return (px, py, pc), 4 * px + 2 * py + pc


ALL_PEERS = tuple(range(1, N_DEV))
CHIP_PEERS = (2, 4, 6)
N_CHIPS = N_DEV // 2


def _sem_scratch(n, m):
    return [pltpu.SemaphoreType.DMA((n, m)), pltpu.SemaphoreType.DMA((n, m)), pltpu.SemaphoreType.DMA((n,))]


class Exchange:
    def __init__(self, arrays, out_shapes, sem_cols, plan, aliased=False):
        self.arrays, self.out_shapes, self.plan = list(arrays), list(out_shapes), plan
        self.scratch = _sem_scratch(len(self.arrays), sem_cols)
        self.aliased = aliased

    def split(self, refs):
        n = len(self.arrays)
        return refs[:n], refs[n:2 * n], refs[2 * n:]

    def start(self, srcs, dsts, sems):
        local, sends, _ = self.plan(srcs, dsts, *sems)
        for cp in local + sends:
            cp.start()

    def wait(self, srcs, dsts, sems):
        local, sends, recvs = self.plan(srcs, dsts, *sems)
        for cp in recvs:
            cp.wait_recv()
        for cp in sends:
            cp.wait_send()
        for cp in local:
            cp.wait()

    def aliases(self, first_in, first_out):
        return {first_in + j: first_out + j for j in range(len(self.arrays))} if self.aliased else {}

    def run(self, name):
        def body(*refs):
            srcs, dsts, sems = self.split(refs)
            self.start(srcs, dsts, sems)
            self.wait(srcs, dsts, sems)

        n = len(self.arrays)
        return pl.pallas_call(body, name=name, in_specs=[_ANY] * n, out_specs=[_ANY] * n, out_shape=self.out_shapes,
                              scratch_shapes=self.scratch, input_output_aliases=self.aliases(0, 0))(*self.arrays)


def exchange(arrays, scatter, peers, by_chip=False):
    slots = N_CHIPS if by_chip else N_DEV

    def plan(srcs, dsts, send_sems, recv_sems, local_sems):
        x, y, c, me = _place()
        mine = 2 * x + y if by_chip else me
        local = [pltpu.make_async_copy(src.at[mine] if scatter else src, dst.at[mine], local_sems.at[j])
                 for j, (src, dst) in enumerate(zip(srcs, dsts))]
        sends, recvs = [], []
        for t, k in enumerate(peers):
            peer, pidx = _peer(x, y, c, k)
            theirs = 2 * peer[0] + peer[1] if by_chip else pidx
            for j, (src, dst) in enumerate(zip(srcs, dsts)):
                part = src.at[theirs] if scatter else src
                sems = dict(send_sem=send_sems.at[j, t], recv_sem=recv_sems.at[j, t], device_id=peer, device_id_type=_MESH)
                sends.append(pltpu.make_async_remote_copy(src_ref=part, dst_ref=dst.at[mine], **sems))
                recvs.append(pltpu.make_async_remote_copy(src_ref=part, dst_ref=dst.at[theirs], **sems))
        return local, sends, recvs

    shapes = [jax.ShapeDtypeStruct(a.shape if scatter else (slots,) + a.shape, a.dtype) for a in arrays]
    return Exchange(arrays, shapes, len(peers), plan)


def second_hop(gathered):
    def plan(srcs, dsts, send_sems, recv_sems, local_sems):
        x, y, c, me = _place()
        sibling, _ = _peer(x, y, c, 1)
        sends, recvs = [], []
        for t, k in enumerate(CHIP_PEERS):
            _, landed = _peer(x, y, c, k)
            _, coming = _peer(x, y, c, k ^ 1)
            for j, buf in enumerate(dsts):
                sems = dict(send_sem=send_sems.at[j, t], recv_sem=recv_sems.at[j, t], device_id=sibling, device_id_type=_MESH)
                sends.append(pltpu.make_async_remote_copy(src_ref=buf.at[landed], dst_ref=buf.at[landed], **sems))
                recvs.append(pltpu.make_async_remote_copy(src_ref=buf.at[coming], dst_ref=buf.at[coming], **sems))
        return [], sends, recvs

    shapes = [jax.ShapeDtypeStruct(a.shape, a.dtype) for a in gathered]
    return Exchange(gathered, shapes, len(CHIP_PEERS), plan, aliased=True)


FIRST_HOP_PEERS = (1,) + CHIP_PEERS


def _gather_two_level(arrays, name):
    n = len(arrays)

    def body(*refs):
        srcs, dsts, (send_sems, recv_sems, local_sems) = refs[:n], refs[n:2 * n], refs[2 * n:]
        x, y, c, me = _place()
        sibling, sidx = _peer(x, y, c, 1)

        def copy(j, sem, block, to, src=None):
            rows = dsts[j].at[block]
            return pltpu.make_async_remote_copy(src_ref=rows if src is None else src, dst_ref=rows, send_sem=send_sems.at[j, sem],
                                                recv_sem=recv_sems.at[j, sem], device_id=to, device_id_type=_MESH)

        local = [pltpu.make_async_copy(srcs[j], dsts[j].at[me], local_sems.at[j]) for j in range(n)]
        for cp in local:
            cp.start()
        first = [copy(j, 1 + t, me, _peer(x, y, c, k)[0], src=srcs[j]) for t, k in enumerate(CHIP_PEERS) for j in range(n)]
        first += [copy(j, 0, me, sibling, src=srcs[j]) for j in range(n)]
        for cp in first:
            cp.start()
        passed = []
        for t, k in enumerate(CHIP_PEERS):
            peer, pidx = _peer(x, y, c, k)
            for j in range(n):
                copy(j, 1 + t, pidx, peer).wait_recv()
                passed.append(copy(j, 4 + t, pidx, sibling))
                passed[-1].start()
        for j in range(n):
            copy(j, 0, sidx, sibling).wait_recv()
        for t, k in enumerate(CHIP_PEERS):
            _, pidx = _peer(x, y, c, k ^ 1)
            for j in range(n):
                copy(j, 4 + t, pidx, sibling).wait_recv()
        for cp in first + passed:
            cp.wait_send()
        for cp in local:
            cp.wait()

    shapes = [jax.ShapeDtypeStruct((N_DEV,) + a.shape, a.dtype) for a in arrays]
    return pl.pallas_call(body, name=name, in_specs=[_ANY] * n, out_specs=[_ANY] * n, out_shape=shapes,
                          scratch_shapes=_sem_scratch(n, 1 + 2 * len(CHIP_PEERS)))(*arrays)


def _to_sibling(arrays, name):
    n = len(arrays)

    def body(*refs):
        srcs, dsts, (send_sems, recv_sems) = refs[:n], refs[n:2 * n], refs[2 * n:]
        x, y, c, _ = _place()
        sibling, _ = _peer(x, y, c, 1)
        copies = [pltpu.make_async_remote_copy(src_ref=srcs[j].at[1 - c], dst_ref=dsts[j], send_sem=send_sems.at[j],
                                               recv_sem=recv_sems.at[j], device_id=sibling, device_id_type=_MESH) for j in range(n)]
        for cp in copies:
            cp.start()
        for cp in copies:
            cp.wait()

    shapes = [jax.ShapeDtypeStruct(a.shape[1:], a.dtype) for a in arrays]
    return pl.pallas_call(body, name=name, in_specs=[_ANY] * n, out_specs=[_ANY] * n, out_shape=shapes,
                          scratch_shapes=[pltpu.SemaphoreType.DMA((n,)), pltpu.SemaphoreType.DMA((n,))])(*arrays)


def pair_add(own, theirs, core, *, name):
    _, ns, r, c = own.shape
    rb = _row_block(r, c)

    def body(core_ref, a_ref, b_ref, o_ref):
        o_ref[...] = (a_ref[...].astype(F32) + b_ref[...].astype(F32)).astype(o_ref.dtype)

    return pl.pallas_call(
        body, name=name,
        grid_spec=pltpu.PrefetchScalarGridSpec(
            num_scalar_prefetch=1, grid=(ns, r // rb),
            in_specs=[pl.BlockSpec((None, None, rb, c), lambda i, j, core_ref: (core_ref[0], i, j, 0)),
                      pl.BlockSpec((None, rb, c), lambda i, j, core_ref: (i, j, 0))],
            out_specs=pl.BlockSpec((None, rb, c), lambda i, j, core_ref: (i, j, 0))),
        out_shape=jax.ShapeDtypeStruct((ns, r, c), own.dtype), compiler_params=_cparams(),
    )(core, own, theirs)


ADAMW_BLOCK_BYTES = 1 << 20


def _row_block(r, c):
    for rb in range(r, 0, -1):
        if r % rb == 0 and (rb % 16 == 0 or rb == r) and rb * c * 4 <= ADAMW_BLOCK_BYTES:
            return rb
    return r


def adamw(w, m, v, parts, *, name):
    depth, r, c = w.shape
    n_parts = parts[0].shape[0]
    rb = _row_block(r, c)

    def body(w_ref, m_ref, v_ref, *refs):
        p_refs, (g_ref, d_ref, nm_ref, nv_ref) = refs[:depth], refs[depth:]

        def total(p_ref):
            g = p_ref[0].astype(F32)
            for j in range(1, n_parts):
                g = g + p_ref[j].astype(F32)
            return g

        g = total(p_refs[0])
        for l in range(1, depth):
            g = jnp.where(pl.program_id(0) == l, total(p_refs[l]), g)
        g_ref[...] = g
        m_new = ADAM_B1 * m_ref[...] + (1.0 - ADAM_B1) * g
        v_new = ADAM_B2 * v_ref[...] + (1.0 - ADAM_B2) * (g * g)
        m_hat = m_new / (1.0 - ADAM_B1 ** ADAM_STEP)
        v_hat = v_new / (1.0 - ADAM_B2 ** ADAM_STEP)
        d_ref[...] = -ADAM_LR * (m_hat / (jnp.sqrt(v_hat) + ADAM_EPS) + ADAM_WD * w_ref[...])
        nm_ref[...] = m_new
        nv_ref[...] = v_new

    wblk = pl.BlockSpec((None, rb, c), lambda l, i: (l, i, 0))
    pblk = pl.BlockSpec((n_parts, rb, c), lambda l, i: (0, i, 0))
    return pl.pallas_call(
        body, name=name, grid=(depth, r // rb),
        in_specs=[wblk, wblk, wblk] + [pblk] * depth, out_specs=[wblk] * 4,
        out_shape=[jax.ShapeDtypeStruct((depth, r, c), F32)] * 4, compiler_params=_cparams(),
    )(w, m, v, *parts)


BIG = (("w_in", 2), ("w_uq", 2), ("w_ukv", 2), ("w_pa", 2), ("w_pb", 1), ("w_o", 1), ("w_gate", 2), ("w_up", 2), ("w_down", 1))
SMALL = ("norm_mix_g", "pool_w", "pool_scale", "q_norm_g", "kv_norm_g", "norm_ffn_g", "final_norm_g")
WEIGHTS = ("meta_tokens", "norm_mix_g", "w_in", "pool_w", "pool_scale", "q_norm_g", "kv_norm_g", "w_uq", "w_ukv", "w_pa", "w_pb",
           "w_o", "norm_ffn_g", "w_gate", "w_up", "w_down", "final_norm_g")
HEAD_QK = QK_NOPE + QK_ROPE
KR_END = Z_KR + QK_ROPE


def _cat_cols(parts):
    return [jnp.concatenate(parts, axis=1)]


def _cat_rows(parts):
    return [jnp.concatenate(parts, axis=0)]


def _arr_w_in(parts):
    full = jnp.concatenate(parts, axis=1)
    zc = lambda n: jnp.zeros((full.shape[0], n), full.dtype)
    return [jnp.concatenate([full[:, :Z_KR], zc(QK_NOPE), full[:, Z_KR:KR_END], zc(LANES - HEAD_QK), full[:, KR_END:]], axis=1)]


def _arr_w_uq(parts):
    full = jnp.concatenate(parts, axis=1)
    z = jnp.zeros((full.shape[0], HEAD_SLOT - HEAD_QK), full.dtype)
    pieces = []
    for hd in range(N_HEADS):
        pieces += [full[:, hd * HEAD_QK:(hd + 1) * HEAD_QK], z]
    return [jnp.concatenate(pieces, axis=1)]


def _arr_w_ukv(parts):
    full = jnp.concatenate(parts, axis=1)
    z = jnp.zeros((full.shape[0], HEAD_SLOT - QK_NOPE), full.dtype)
    wide = QK_NOPE + V_DIM
    k, v = [], []
    for hd in range(N_HEADS):
        k += [full[:, hd * wide:hd * wide + QK_NOPE], z]
        v.append(full[:, hd * wide + QK_NOPE:(hd + 1) * wide])
    return [jnp.concatenate(k, axis=1), jnp.concatenate(v, axis=1)]


def arrange(g, fn, out_shapes, name):
    def body(g_ref, *o_refs):
        for o_ref, val in zip(o_refs, fn([g_ref[p] for p in range(N_DEV)])):
            o_ref[...] = val

    return pl.pallas_call(
        body, name=name, grid=(1,),
        in_specs=[pl.BlockSpec(g.shape, lambda i: (0, 0, 0))],
        out_specs=[pl.BlockSpec(s, lambda i: (0, 0)) for s in out_shapes],
        out_shape=[jax.ShapeDtypeStruct(s, g.dtype) for s in out_shapes], compiler_params=_cparams(),
    )(g)


def _arranged_ranges(lo, hi):
    out = []
    for a, b, shift in ((0, Z_KR, 0), (Z_KR, KR_END, QK_NOPE), (KR_END, D_IN, LANES - QK_ROPE)):
        s, e = max(lo, a), min(hi, b)
        if s < e:
            out.append((s + shift, e + shift))
    return out


def _chunks_w_in(acc):
    cs = D_IN // N_DEV
    return [jnp.concatenate([acc[:, a:b] for a, b in _arranged_ranges(p * cs, (p + 1) * cs)], axis=1) for p in range(N_DEV)]


def _chunks_w_uq(acc):
    per = N_HEADS // N_DEV
    return [jnp.concatenate([acc[:, hd * HEAD_SLOT:hd * HEAD_SLOT + HEAD_QK] for hd in range(p * per, (p + 1) * per)], axis=1)
            for p in range(N_DEV)]


def _chunks_w_ukv(acc_k, acc_v):
    per = N_HEADS // N_DEV
    out = []
    for p in range(N_DEV):
        pieces = []
        for hd in range(p * per, (p + 1) * per):
            pieces += [acc_k[:, hd * HEAD_SLOT:hd * HEAD_SLOT + QK_NOPE], acc_v[:, hd * V_DIM:(hd + 1) * V_DIM]]
        out.append(jnp.concatenate(pieces, axis=1))
    return out


def _chunks_cols(acc):
    cs = acc.shape[1] // N_DEV
    return [acc[:, p * cs:(p + 1) * cs] for p in range(N_DEV)]


def _chunks_rows(acc):
    rs = acc.shape[0] // N_DEV
    return [acc[p * rs:(p + 1) * rs, :] for p in range(N_DEV)]


def _chunks_cols_transposed(acc):
    at = acc[...].T
    rs = at.shape[0] // N_DEV
    return [at[p * rs:(p + 1) * rs, :] for p in range(N_DEV)]


def _pack(parts, row_multiple):
    flat = jnp.concatenate([p.reshape(-1) for p in parts])
    return jnp.pad(flat, (0, -flat.shape[0] % (row_multiple * LANES))).reshape(-1, LANES)


def _unpack(packed, shapes):
    flat, out, off = packed.reshape(-1), [], 0
    for s in shapes:
        n = 1
        for d in s:
            n *= d
        out.append(flat[off:off + n].reshape(s))
        off += n
    return out


def _rope_table(lp, nb):
    inv = 1.0 / (ROPE_THETA ** (jnp.arange(0, QK_ROPE, 2, dtype=F32) / QK_ROPE))
    ang = jnp.arange(lp, dtype=F32)[:, None] * inv[None, :]
    cos, sin = jnp.cos(ang), jnp.sin(ang)
    z = lambda n: jnp.zeros((lp, n), F32)
    tail = LANES - QK_NOPE - QK_ROPE
    c = jnp.concatenate([jnp.ones((lp, QK_NOPE), F32), cos, cos, z(tail)], axis=1)
    cr = jnp.concatenate([z(QK_NOPE), cos, cos, z(tail)], axis=1)
    s1 = jnp.concatenate([z(QK_NOPE), -sin, z(HALF_ROPE), z(tail)], axis=1)
    s2 = jnp.concatenate([z(QK_NOPE), z(HALF_ROPE), sin, z(tail)], axis=1)
    return jnp.tile(jnp.concatenate([c, cr, s1, s2], axis=1), (nb, 1))


MIX = ("w_in", "w_uq", "w_ukv", "w_pa", "w_pb", "w_o")
FFN = ("w_gate", "w_up", "w_down")
TRANSPOSED = ("w_gate", "w_up")
ARRANGERS = {
    "w_in": (_arr_w_in, (("win", (D_MODEL, DZ)),)), "w_uq": (_arr_w_uq, (("wuq", (Q_RANK, QK_WIDTH)),)),
    "w_ukv": (_arr_w_ukv, (("wuk", (KV_RANK, QK_WIDTH)), ("wuv", (KV_RANK, D_MODEL)))),
    "w_pa": (_cat_cols, (("wpa", (POOL_WIDTH, D_MODEL)),)), "w_pb": (_cat_rows, (("wpb", (D_MODEL, D_MODEL)),)),
    "w_o": (_cat_rows, (("wo", (D_MODEL, D_MODEL)),)), "w_gate": (_cat_rows, (("wgt", (D_FF, D_MODEL)),)),
    "w_up": (_cat_rows, (("wut", (D_FF, D_MODEL)),)), "w_down": (_cat_rows, (("wd", (D_FF, D_MODEL)),)),
}


def _operands(gathered, names, l):
    p = {}
    for n in names:
        fn, outs = ARRANGERS[n]
        for (key, _), a in zip(outs, arrange(gathered[n], fn, [s for _, s in outs], f"arrange_{n}_{l}")):
            p[key] = a
    return p


def _small_operands(small, l):
    pw = small["pool_w"][l].astype(BF16)
    return dict(g_mix=small["norm_mix_g"][l][None], gq=small["q_norm_g"][l][None], gkv=small["kv_norm_g"][l][None],
                g_ffn=small["norm_ffn_g"][l][None], ps=small["pool_scale"][l][None], pw=pw)


class MeshComm:
    def __init__(self, w, meta_tokens):
        self.src = lambda n, l: w[n][l].astype(BF16)
        self.meta_tokens = meta_tokens
        self.core = lax.axis_index("c").astype(jnp.int32).reshape(1)
        self.rides = {0: [(n, 0) for n in FFN] + [(n, 1) for n in MIX], 1: [(n, 1) for n in FFN]}

    def first_weights(self):
        got = _gather_two_level([self.src(n, 0) for n in MIX] + [self.meta_tokens], "gather_mix_0")
        return dict(zip(MIX, got)), jnp.moveaxis(got[-1], 0, 1).reshape(N_META, D_MODEL)

    def first_hop(self, l):
        return exchange([self.src(n, layer) for n, layer in self.rides[l]], False, FIRST_HOP_PEERS)

    def second_hop(self, l, landed):
        return second_hop(landed)

    def carried(self, l, full, names, layer):
        return {n: full[self.rides[l].index((n, layer))] for n in names}

    def pair_sums(self, own, names, tag):
        theirs = _to_sibling(own, f"pair_grads_{tag}")
        return [pair_add(a, b, self.core, name=f"pair_add_{n}_{tag}") for n, a, b in zip(names, own, theirs)]

    def scatter(self, sums):
        return exchange(sums, True, CHIP_PEERS, by_chip=True)

    def scatter_now(self, sums, name):
        return self.scatter(sums).run(name)


HEADS_FWD, HEADS_BWD = 8, 4
TILE_ROWS, TILE_ROWS_BWD = 512, 256


def _tile(t, target):
    n = max(1, -(-t // (target + target // 8)))
    while t % n or (t // n) % 16:
        n += 1
    return t // n


def _wgrad_tile(t):
    return max(tm for tm in (2 * TQ, TQ, LANES) if t % tm == 0)


def _ffn_bwd_part(dh2, p, s, tag):
    d, ff = D_MODEL, D_FF // N_DEV
    t = dh2.shape[0]
    wg_ = lambda n, x, ys, fn, shape: wgrad(x, ys, fn, shape, tm=_wgrad_tile(t), name=f"wgrad_{n}_{tag}")
    dh1, hn2, act, dgt, dup, dg_ffn = ffn_bwd(dh2, s["h1"], p["g_ffn"], s["gt"], s["up"], p["wgt"], p["wut"], p["wd"],
                                              tm=_tile(t, TILE_ROWS_BWD), name=f"ffn_bwd_{tag}")
    chunks = [wg_("gate", hn2, [dgt], _chunks_cols_transposed, (ff, d)), wg_("up", hn2, [dup], _chunks_cols_transposed, (ff, d)),
              wg_("down", act, [dh2], _chunks_rows, (ff, d))]
    return dh1, chunks, dict(norm_ffn_g=dg_ffn[0])


def _mix_bwd_part(dh1, p, s, rope, nb, lp, tag, ride):
    d = D_MODEL
    t = dh1.shape[0]
    wg_ = lambda n, x, ys, fn, shape: wgrad(x, ys, fn, shape, tm=_wgrad_tile(t), name=f"wgrad_{n}_{tag}")
    dga, dgb, dpa, dpb, do, dpool, dps, dpw = merge_bwd(dh1, s["z"], s["pa"], s["pb"], s["pooled"], p["pw"], p["ps"],
                                                        p["wpa"], p["wpb"], p["wo"], tm=_tile(t, TILE_ROWS),
                                                        name=f"merge_bwd_{tag}")
    c_o = wg_("o", s["mg"], [dh1], _chunks_rows, (d // N_DEV, d))
    c_pa = wg_("pa", s["a"], [dpa], _chunks_cols, (POOL_WIDTH, d // N_DEV))
    c_pb = wg_("pb", s["o"], [dpb], _chunks_rows, (d // N_DEV, d))
    (dq, dk, dv), brought = attn_bwd(s["q"], s["k"], s["v"], s["o"], do, s["lse"], nb=nb, lp=lp, hb=HEADS_BWD,
                                     name=f"attn_bwd_{tag}", ride=ride)
    dh, hn, dz, cqn, ckvn, dqb, dkb, dvb, dg_mix, dgq, dgkv = in_proj_bwd(
        dh1, s["h"], p["g_mix"], s["z"], dq, dk, dv, dga, dgb, dpool, p["win"], p["gq"], p["gkv"], p["wuq"], p["wuk"], p["wuv"],
        rope, tm=_tile(t, TILE_ROWS_BWD), lp=lp, nb=nb, name=f"in_proj_bwd_{tag}")
    c_in = wg_("in", hn, [dz], _chunks_w_in, (d, D_IN // N_DEV))
    c_uq = wg_("uq", cqn, [dqb], _chunks_w_uq, (Q_RANK, N_HEADS * HEAD_QK // N_DEV))
    c_ukv = wg_("ukv", ckvn, [dkb, dvb], _chunks_w_ukv, (KV_RANK, N_HEADS * (QK_NOPE + V_DIM) // N_DEV))
    small = dict(pool_scale=dps[0], pool_w=dpw, norm_mix_g=dg_mix[0], q_norm_g=dgq[0], kv_norm_g=dgkv[0])
    return dh, [c_in, c_uq, c_ukv, c_pa, c_pb, c_o], small, brought


def train_step(x, loss_target, small, comm):
    nb, seq, d = x.shape
    lp = -(-(N_META + seq) // LANES) * LANES
    t = nb * lp
    assert nb <= 2 and DEPTH == 2
    tm = _tile(t, TILE_ROWS)
    rope = _rope_table(lp, nb)
    gathered, meta = comm.first_weights()
    pad = jnp.zeros((nb, lp - N_META - seq, d), F32)
    h = jnp.concatenate([jnp.broadcast_to(meta[None], (nb, N_META, d)), x, pad], axis=1).reshape(t, d)
    target = jnp.concatenate([jnp.zeros((nb, N_META, d), F32), loss_target, pad], axis=1).reshape(t, d)

    params, saved, full = [], [], {}
    for l in range(DEPTH):
        p = _small_operands(small, l)
        p.update(_operands(gathered if l == 0 else comm.carried(0, full[0], MIX, 1), MIX, l))
        z, q, k, v = in_proj_fwd(h, p["g_mix"], p["win"], p["gq"], p["gkv"], p["wuq"], p["wuk"], p["wuv"], rope, tm=tm,
                                 name=f"in_proj_fwd_{l}")
        (o, lse), landed = attn_fwd(q, k, v, nb=nb, lp=lp, hb=HEADS_FWD, name=f"attn_fwd_{l}", ride=comm.first_hop(l))
        (h1, pooled, a, pa, pb, mg), full[l] = merge_fwd(h, z, o, p["pw"], p["ps"], p["wpa"], p["wpb"], p["wo"], tm=tm, lp=lp,
                                                          nb=nb, name=f"merge_fwd_{l}", ride=comm.second_hop(l, landed))
        p.update(_operands(comm.carried(l, full[l], FFN, l), FFN, l))
        h2, gt, up = ffn_fwd(h1, p["g_ffn"], p["wgt"], p["wut"], p["wd"], tm=tm, name=f"ffn_fwd_{l}")
        params.append(p)
        saved.append(dict(h=h, z=z, q=q, k=k, v=v, o=o, lse=lse, h1=h1, pooled=pooled, a=a, pa=pa, pb=pb, mg=mg, gt=gt, up=up))
        h = h2
    parts, dh, dgf = loss_head(h, small["final_norm_g"][None], target, tm=tm, lp=lp, nb=nb, seq=seq, name="loss_head")
    loss = jnp.sum(parts[::8, 0])

    sums = {}
    dh, c_ffn1, small1 = _ffn_bwd_part(dh, params[1], saved[1], 1)
    s_ffn1 = comm.pair_sums(c_ffn1, FFN, "ffn_1")
    dh, c_mix1, sm, brought = _mix_bwd_part(dh, params[1], saved[1], rope, nb, lp, 1, comm.scatter(s_ffn1))
    small1.update(sm)
    sums.update({(n, 1): a for n, a in zip(FFN, brought)})
    s_mix1 = comm.pair_sums(c_mix1, MIX, "mix_1")
    dh, c_ffn0, small0 = _ffn_bwd_part(dh, params[0], saved[0], 0)
    s_ffn0 = comm.pair_sums(c_ffn0, FFN, "ffn_0")
    dh, c_mix0, sm, brought = _mix_bwd_part(dh, params[0], saved[0], rope, nb, lp, 0, comm.scatter(s_mix1 + s_ffn0))
    small0.update(sm)
    sums.update({(n, l): a for (n, l), a in zip([(n, 1) for n in MIX] + [(n, 0) for n in FFN], brought)})
    dh = dh.reshape(nb, lp, d)
    dmeta = jnp.sum(dh[:, :N_META], axis=0)
    meta_chunks = jnp.transpose(dmeta.reshape(N_META, N_CHIPS, 2, d // N_DEV), (2, 1, 0, 3)).astype(BF16)
    s_last = comm.pair_sums(c_mix0 + [meta_chunks], MIX + ("meta_tokens",), "mix_0")
    last = comm.scatter_now(s_last, "scatter_mix_0")
    sums.update({(n, 0): a for n, a in zip(MIX + ("meta_tokens",), last)})
    small_grads = {n: jnp.stack([small0[n], small1[n]]) for n in small0}
    small_grads["final_norm_g"] = dgf[0]
    return loss, dh[:, N_META:N_META + seq], sums, small_grads


def kernel(x, meta_tokens, norm_mix_g, w_in, pool_w, pool_scale, q_norm_g, kv_norm_g, w_uq, w_ukv, w_pa, w_pb, w_o, norm_ffn_g, w_gate, w_up, w_down, final_norm_g, loss_target, m_meta_tokens, m_norm_mix_g, m_w_in, m_pool_w, m_pool_scale, m_q_norm_g, m_kv_norm_g, m_w_uq, m_w_ukv, m_w_pa, m_w_pb, m_w_o, m_norm_ffn_g, m_w_gate, m_w_up, m_w_down, m_final_norm_g, v_meta_tokens, v_norm_mix_g, v_w_in, v_pool_w, v_pool_scale, v_q_norm_g, v_kv_norm_g, v_w_uq, v_w_ukv, v_w_pa, v_w_pb, v_w_o, v_norm_ffn_g, v_w_gate, v_w_up, v_w_down, v_final_norm_g):
    args = dict(locals())
    w = {n: args[n] for n in WEIGHTS}
    m = {n: args["m_" + n] for n in WEIGHTS}
    v = {n: args["v_" + n] for n in WEIGHTS}
    small = {n: w[n] for n in SMALL}
    as_handled = lambda a, n: jnp.swapaxes(a, 1, 2) if n in TRANSPOSED else a
    wh, mh, vh = ({n: as_handled(d[n], n) for n, _ in BIG} for d in (w, m, v))

    loss, grad_x, sums, small_grads = train_step(x, loss_target, small, MeshComm(wh, meta_tokens))
    loss = lax.psum(loss, ("x", "y", "c"))
    (small_recv,) = exchange([_pack([small_grads[n] for n in SMALL], 8)], False, ALL_PEERS).run("gather_small_grads")

    out = {n: [as_handled(a, n) for a in adamw(wh[n], mh[n], vh[n], [sums[(n, l)] for l in range(DEPTH)], name=f"adamw_{n}")]
           for n, _ in BIG}
    out["meta_tokens"] = [a[0] for a in adamw(meta_tokens[None], m["meta_tokens"][None], v["meta_tokens"][None],
                                              [sums[("meta_tokens", 0)]], name="adamw_meta_tokens")]
    pk = lambda d: _pack([d[n] for n in SMALL], 8)[None]
    packed = adamw(pk(w), pk(m), pk(v), [small_recv], name="adamw_small")
    shapes = [w[n].shape for n in SMALL]
    for n, *kinds in zip(SMALL, *[_unpack(packed[kind][0], shapes) for kind in range(4)]):
        out[n] = kinds
    return (loss, grad_x, *[out[n][kind] for kind in range(4) for n in WEIGHTS])
```

```python
import functools
import math

import jax
import jax.numpy as jnp
from jax import lax
from jax.experimental import pallas as pl
from jax.experimental.pallas import tpu as pltpu

F32, BF16 = jnp.float32, jnp.bfloat16

D_MODEL = 1024
N_META = 16
N_HEADS = 16
QK_NOPE, QK_ROPE, V_DIM = 64, 32, 64
HALF_ROPE = QK_ROPE // 2
Q_RANK, KV_RANK = 256, 128
POOL_WINDOWS = (2, 4, 8, 16)
POOL_GROUP = 128
POOL_WIDTH = POOL_GROUP * len(POOL_WINDOWS)
POOL_HALO = 16
D_FF = 2816
D_IN = 2976
NORM_EPS = 1e-6
SM_SCALE = (QK_NOPE + QK_ROPE) ** -0.5
LOG2E = math.log2(math.e)
EXP2_SCALE = SM_SCALE * LOG2E
MASK_VALUE = -1e30
ROPE_THETA = 10000.0
DEPTH = 2
N_DEV = 8

ADAM_LR, ADAM_B1, ADAM_B2, ADAM_EPS, ADAM_WD, ADAM_STEP = 0.001, 0.9, 0.999, 1e-08, 0.01, 10

LANES = 128
HEAD_SLOT = LANES
QK_WIDTH = N_HEADS * HEAD_SLOT
Z_CQ, Z_CKV, Z_KR, Z_GA, Z_GB, DZ = 512, 768, 896, 1024, 2048, 3072
TQ = TK = 256
VMEM_LIMIT = 56 * 1024 * 1024


def _cparams():
    return pltpu.CompilerParams(vmem_limit_bytes=VMEM_LIMIT)


def _rows(tm, width, col=0):
    return pl.BlockSpec((tm, width), lambda i: (i, col))


def _whole(shape):
    zeros = (0,) * len(shape)
    return pl.BlockSpec(shape, lambda i: zeros, pipeline_mode=pl.Buffered(1))


def _acc(shape):
    zeros = (0,) * len(shape)
    return pl.BlockSpec(shape, lambda i: zeros)


def _dot(a, b):
    return jnp.dot(a, b, preferred_element_type=F32)


def _dot_tn(a, b):
    return lax.dot_general(a, b, (((0,), (0,)), ((), ())), preferred_element_type=F32)


def _dot_nt(a, b):
    return lax.dot_general(a, b, (((1,), (1,)), ((), ())), preferred_element_type=F32)


def _rms(x):
    r = lax.rsqrt(jnp.mean(x * x, axis=-1, keepdims=True) + NORM_EPS)
    return x * r, r


def _rms_bwd(dy, xhat, r, g):
    dg = jnp.sum(dy * xhat, axis=0, keepdims=True)
    dxh = dy * g
    dx = r * (dxh - xhat * jnp.mean(dxh * xhat, axis=-1, keepdims=True))
    return dx, dg


def _sigmoid(x):
    return 1.0 / (1.0 + jnp.exp(-x))


def _rope_fwd(q, c, s1, s2):
    w = q.shape[1]
    return q * c + pltpu.roll(q, w - HALF_ROPE, 1) * s1 + pltpu.roll(q, HALF_ROPE, 1) * s2


def _rope_bwd(dq, c, s1, s2):
    w = dq.shape[1]
    return dq * c + pltpu.roll(dq * s1, HALF_ROPE, 1) + pltpu.roll(dq * s2, w - HALF_ROPE, 1)


def _rope_tables(rope, reps):
    c, cr, s1, s2 = (rope[:, k * LANES:(k + 1) * LANES] for k in range(4))
    if reps > 1:
        return jnp.tile(c, (1, reps)), jnp.tile(s1, (1, reps)), jnp.tile(s2, (1, reps))
    return cr, s1, s2


def _seq_pos(gi, lp, nb):
    pos = gi
    for b in range(1, nb):
        pos = jnp.where(gi >= b * lp, gi - b * lp, pos)
    return pos


_ANY = pl.BlockSpec(memory_space=pl.ANY)


def _carrying_call(body, ride, operands, *, name, grid, in_specs, out_specs, out_shape, scratch_shapes=()):
    n_in, n_out = len(in_specs), len(out_specs)
    if ride is None:
        out = pl.pallas_call(body, name=name, grid=grid, in_specs=in_specs, out_specs=out_specs, out_shape=out_shape,
                             scratch_shapes=list(scratch_shapes), compiler_params=_cparams())(*operands)
        return out, []
    ne = len(ride.arrays)

    def carrying(*refs):
        ins, r_in, rest = refs[:n_in], refs[n_in:n_in + ne], refs[n_in + ne:]
        outs, r_out, rest = rest[:n_out], rest[n_out:n_out + ne], rest[n_out + ne:]
        scratch, sems = rest[:len(scratch_shapes)], rest[len(scratch_shapes):]
        ids = [pl.program_id(a) for a in range(len(grid))]
        first = functools.reduce(jnp.logical_and, [i == 0 for i in ids])
        last = functools.reduce(jnp.logical_and, [i == g - 1 for i, g in zip(ids, grid)])

        @pl.when(first)
        def _():
            ride.start(r_in, r_out, sems)

        body(*ins, *outs, *scratch)

        @pl.when(last)
        def _():
            ride.wait(r_in, r_out, sems)

    out = pl.pallas_call(
        carrying, name=name, grid=grid, in_specs=list(in_specs) + [_ANY] * ne, out_specs=list(out_specs) + [_ANY] * ne,
        out_shape=list(out_shape) + ride.out_shapes, scratch_shapes=list(scratch_shapes) + ride.scratch,
        input_output_aliases=ride.aliases(n_in, n_out), compiler_params=_cparams(),
    )(*operands, *ride.arrays)
    return out[:n_out], out[n_out:]


def in_proj_fwd(h, g_mix, win, gq, gkv, wuq, wuk, wuv, rope, *, tm, name):
    t = h.shape[0]

    def body(h_ref, g_ref, win_ref, gq_ref, gkv_ref, wuq_ref, wuk_ref, wuv_ref, rope_ref, z_ref, q_ref, k_ref, v_ref):
        xhat, _ = _rms(h_ref[...])
        hn = (xhat * g_ref[...]).astype(BF16)
        z = _dot(hn, win_ref[...])
        z_ref[...] = z
        rope_t = rope_ref[...]
        xq, _ = _rms(z[:, Z_CQ:Z_CKV])
        cqn = (xq * gq_ref[...]).astype(BF16)
        q = _rope_fwd(_dot(cqn, wuq_ref[...]), *_rope_tables(rope_t, N_HEADS))
        q_ref[...] = q.astype(BF16)
        xkv, _ = _rms(z[:, Z_CKV:Z_KR])
        ckvn = (xkv * gkv_ref[...]).astype(BF16)
        kr = _rope_fwd(z[:, Z_KR:Z_GA], *_rope_tables(rope_t, 1))
        k_ref[...] = (_dot(ckvn, wuk_ref[...]) + jnp.tile(kr, (1, N_HEADS))).astype(BF16)
        v_ref[...] = _dot(ckvn, wuv_ref[...]).astype(BF16)

    return pl.pallas_call(
        body, name=name, grid=(t // tm,),
        in_specs=[_rows(tm, D_MODEL), _whole((1, D_MODEL)), _whole((D_MODEL, DZ)), _whole((1, Q_RANK)), _whole((1, KV_RANK)),
                  _whole((Q_RANK, QK_WIDTH)), _whole((KV_RANK, QK_WIDTH)), _whole((KV_RANK, D_MODEL)), _rows(tm, 4 * LANES)],
        out_specs=[_rows(tm, DZ), _rows(tm, QK_WIDTH), _rows(tm, QK_WIDTH), _rows(tm, D_MODEL)],
        out_shape=[jax.ShapeDtypeStruct((t, DZ), F32), jax.ShapeDtypeStruct((t, QK_WIDTH), BF16),
                   jax.ShapeDtypeStruct((t, QK_WIDTH), BF16), jax.ShapeDtypeStruct((t, D_MODEL), BF16)],
        compiler_params=_cparams(),
    )(h, g_mix, win, gq, gkv, wuq, wuk, wuv, rope)


def attn_fwd(q, k, v, *, nb, lp, hb, name, ride=None):
    t = q.shape[0]
    nq, tail = lp // TQ, lp % TQ
    assert tail % LANES == 0

    def body(q_ref, k_ref, v_ref, o_ref, lse_ref, vt):
        for pr in range(hb // 2):
            vt[pr] = v_ref[:, pr * LANES:(pr + 1) * LANES].T

        def q_block(qs, tq, whole_k):
            qh = [q_ref[pl.ds(qs, tq), hd * HEAD_SLOT:(hd + 1) * HEAD_SLOT] for hd in range(hb)]
            keep = lax.broadcasted_iota(jnp.int32, (tq, tq), 0) <= lax.broadcasted_iota(jnp.int32, (tq, tq), 1)

            def k_steps(blocks, c, masked):
                sts = [[_dot_nt(k_ref[pl.ds(ks, tk), hd * HEAD_SLOT:(hd + 1) * HEAD_SLOT], qh[hd]) for hd in range(hb)]
                       for ks, tk in blocks]
                for (ks, tk), st_b in zip(blocks, sts):
                    ps, stats = [], []
                    for hd in range(hb):
                        m, l, _ = c[hd]
                        st = jnp.where(keep, st_b[hd], MASK_VALUE) if masked else st_b[hd]
                        m_new = jnp.maximum(m, jnp.max(st, axis=0, keepdims=True))
                        p = jnp.exp2((st - m_new) * EXP2_SCALE)
                        alpha = jnp.exp2((m - m_new) * EXP2_SCALE)
                        ps.append(p.astype(BF16))
                        stats.append((m_new, alpha * l + jnp.sum(p, axis=0, keepdims=True), alpha))
                    pvs = [_dot(vt[hd // 2, :, pl.ds(ks, tk)], ps[hd]) for hd in range(hb)]
                    c = tuple((stats[hd][0], stats[hd][1], stats[hd][2] * c[hd][2] + pvs[hd]) for hd in range(hb))
                return c

            def two_blocks(i, c):
                ks = pl.multiple_of(2 * i * TK, TK)
                return k_steps([(ks, TK), (ks + TK, TK)], c, False)

            init = tuple((jnp.full((1, tq), MASK_VALUE, F32), jnp.zeros((1, tq), F32), jnp.zeros((LANES, tq), F32))
                         for _ in range(hb))
            pairs = lax.div(whole_k, 2)
            c = lax.fori_loop(0, pairs, two_blocks, init)
            c = lax.fori_loop(2 * pairs, whole_k, lambda kj, c: k_steps([(pl.multiple_of(kj * TK, TK), TK)], c, False), c)
            c = k_steps([(qs, tq)], c, True)
            sub = lax.broadcasted_iota(jnp.int32, (LANES, tq), 0)
            for pr in range(hb // 2):
                (m0, l0, a0), (m1, l1, a1) = c[2 * pr], c[2 * pr + 1]
                o_ref[pl.ds(qs, tq), pr * LANES:(pr + 1) * LANES] = jnp.where(sub < V_DIM, a0 / l0, a1 / l1).T.astype(BF16)
                lse_ref[2 * pr, :, pl.ds(qs, tq)] = m0 * SM_SCALE + jnp.log(l0)
                lse_ref[2 * pr + 1, :, pl.ds(qs, tq)] = m1 * SM_SCALE + jnp.log(l1)

        def whole_q_block(qi, carry):
            q_block(pl.multiple_of(qi * TQ, TQ), TQ, qi)
            return carry

        lax.fori_loop(0, nq, whole_q_block, 0)
        if tail:
            q_block(nq * TQ, tail, nq)

    blk = lambda w: pl.BlockSpec((lp, w), lambda b, g: (b, g))
    return _carrying_call(
        body, ride, (q, k, v), name=name, grid=(nb, N_HEADS // hb),
        in_specs=[blk(hb * HEAD_SLOT), blk(hb * HEAD_SLOT), blk(hb * V_DIM)],
        out_specs=[blk(hb * V_DIM), pl.BlockSpec((hb, 1, lp), lambda b, g: (g, 0, b))],
        out_shape=[jax.ShapeDtypeStruct((t, D_MODEL), BF16), jax.ShapeDtypeStruct((N_HEADS, 1, t), F32)],
        scratch_shapes=[pltpu.VMEM((hb // 2, LANES, lp), BF16)])


def _pool_band_fwd(i, tm, lp, nb):
    r = lax.broadcasted_iota(jnp.int32, (tm, POOL_HALO + tm), 0)
    e = lax.broadcasted_iota(jnp.int32, (tm, POOL_HALO + tm), 1)
    diff = r + POOL_HALO - e
    pos = _seq_pos(i * tm + lax.broadcasted_iota(jnp.int32, (tm, 1), 0), lp, nb)
    out = []
    for w in POOL_WINDOWS:
        cnt = jnp.minimum(pos + 1, w)
        band = jnp.where((diff >= 0) & (diff < cnt), 1.0, 0.0).astype(BF16)
        out.append((band, cnt.astype(F32)))
    return out


def merge_fwd(h, z, o, pw, ps, wpa, wpb, wo, *, tm, lp, nb, name, ride=None):
    t = h.shape[0]
    hb = tm // POOL_HALO

    def body(h_ref, u_ref, uprev_ref, ga_ref, gb_ref, o_ref, pw_ref, ps_ref, wpa_ref, wpb_ref, wo_ref,
             h1_ref, pooled_ref, a_ref, pa_ref, pb_ref, mg_ref):
        i = pl.program_id(0)
        u = u_ref[...]
        uext = jnp.concatenate([uprev_ref[...], u], axis=0).astype(BF16)
        pooled, ys = [], []
        for g, (band, cnt) in enumerate(_pool_band_fwd(i, tm, lp, nb)):
            gs = slice(g * POOL_GROUP, (g + 1) * POOL_GROUP)
            pg = (_dot(band, uext[:, gs]) / cnt - u[:, gs]).astype(BF16)
            pooled.append(pg)
            ys.append(_dot(pg, pw_ref[g]))
        pooled_ref[...] = jnp.concatenate(pooled, axis=1)
        a = (jnp.concatenate(ys, axis=1) * ps_ref[...]).astype(BF16)
        a_ref[...] = a
        pa = _dot(a, wpa_ref[...])
        pb = _dot(o_ref[...], wpb_ref[...])
        pa_ref[...] = pa.astype(BF16)
        pb_ref[...] = pb.astype(BF16)
        mg = (_sigmoid(ga_ref[...]) * pa + _sigmoid(gb_ref[...]) * pb).astype(BF16)
        mg_ref[...] = mg
        h1_ref[...] = h_ref[...] + _dot(mg, wo_ref[...])

    halo = pl.BlockSpec((POOL_HALO, POOL_WIDTH), lambda i: (jnp.maximum(i * hb - 1, 0), 0))
    return _carrying_call(
        body, ride, (h, z, z, z, z, o, pw, ps, wpa, wpb, wo), name=name, grid=(t // tm,),
        in_specs=[_rows(tm, D_MODEL), _rows(tm, POOL_WIDTH), halo, _rows(tm, D_MODEL, 1), _rows(tm, D_MODEL, 2), _rows(tm, D_MODEL),
                  _whole((4, POOL_GROUP, POOL_GROUP)), _whole((1, POOL_WIDTH)), _whole((POOL_WIDTH, D_MODEL)),
                  _whole((D_MODEL, D_MODEL)), _whole((D_MODEL, D_MODEL))],
        out_specs=[_rows(tm, D_MODEL), _rows(tm, POOL_WIDTH), _rows(tm, POOL_WIDTH), _rows(tm, D_MODEL), _rows(tm, D_MODEL),
                   _rows(tm, D_MODEL)],
        out_shape=[jax.ShapeDtypeStruct((t, D_MODEL), F32), jax.ShapeDtypeStruct((t, POOL_WIDTH), BF16),
                   jax.ShapeDtypeStruct((t, POOL_WIDTH), BF16), jax.ShapeDtypeStruct((t, D_MODEL), BF16),
                   jax.ShapeDtypeStruct((t, D_MODEL), BF16), jax.ShapeDtypeStruct((t, D_MODEL), BF16)])


def ffn_fwd(h1, g, wgt, wut, wd, *, tm, name):
    t = h1.shape[0]

    def body(h_ref, g_ref, wgt_ref, wut_ref, wd_ref, h2_ref, gt_ref, up_ref):
        h = h_ref[...]
        xhat, _ = _rms(h)
        hn = (xhat * g_ref[...]).astype(BF16)
        gt = _dot_nt(hn, wgt_ref[...])
        up = _dot_nt(hn, wut_ref[...])
        gt_ref[...] = gt.astype(BF16)
        up_ref[...] = up.astype(BF16)
        act = (gt * _sigmoid(gt) * up).astype(BF16)
        h2_ref[...] = h + _dot(act, wd_ref[...])

    return pl.pallas_call(
        body, name=name, grid=(t // tm,),
        in_specs=[_rows(tm, D_MODEL), _whole((1, D_MODEL)), _whole((D_FF, D_MODEL)), _whole((D_FF, D_MODEL)), _whole((D_FF, D_MODEL))],
        out_specs=[_rows(tm, D_MODEL), _rows(tm, D_FF), _rows(tm, D_FF)],
        out_shape=[jax.ShapeDtypeStruct((t, D_MODEL), F32), jax.ShapeDtypeStruct((t, D_FF), BF16), jax.ShapeDtypeStruct((t, D_FF), BF16)],
        compiler_params=_cparams(),
    )(h1, g, wgt, wut, wd)


def loss_head(h, g, target, *, tm, lp, nb, seq, name):
    t = h.shape[0]
    nt = t // tm

    def body(h_ref, g_ref, t_ref, loss_ref, dh_ref, dg_ref):
        i = pl.program_id(0)
        pos = _seq_pos(i * tm + lax.broadcasted_iota(jnp.int32, (tm, 1), 0), lp, nb)
        real = (pos >= N_META) & (pos < N_META + seq)
        xhat, r = _rms(h_ref[...])
        gg = g_ref[...]
        err = jnp.where(real, xhat * gg - t_ref[...], 0.0)
        loss_ref[...] = jnp.full((8, LANES), 0.5 * jnp.sum(err * err) / D_MODEL, F32)
        dx, dg = _rms_bwd(err * (1.0 / D_MODEL), xhat, r, gg)
        dh_ref[...] = dx

        @pl.when(i == 0)
        def _():
            dg_ref[...] = jnp.zeros_like(dg_ref)

        dg_ref[...] += dg

    return pl.pallas_call(
        body, name=name, grid=(nt,),
        in_specs=[_rows(tm, D_MODEL), _whole((1, D_MODEL)), _rows(tm, D_MODEL)],
        out_specs=[pl.BlockSpec((8, LANES), lambda i: (i, 0)), _rows(tm, D_MODEL), _acc((1, D_MODEL))],
        out_shape=[jax.ShapeDtypeStruct((nt * 8, LANES), F32), jax.ShapeDtypeStruct((t, D_MODEL), F32),
                   jax.ShapeDtypeStruct((1, D_MODEL), F32)],
        compiler_params=_cparams(),
    )(h, g, target)


def wgrad(x, ys, chunk_fn, chunk_shape, *, tm, name):
    t, m = x.shape
    tiles = t // tm
    steps = -(-tiles // 2)

    def body(*refs):
        ins, o_ref, accs = refs[:2 * (1 + len(ys))], refs[2 * (1 + len(ys))], refs[2 * (1 + len(ys)) + 1:]
        i = pl.program_id(0)

        @pl.when(i == 0)
        def _():
            for acc in accs:
                acc[...] = jnp.zeros_like(acc)

        def both(first, second, mask):
            b = second[...].astype(BF16)
            if mask and tiles % 2:
                b = jnp.where(2 * i + 1 < tiles, b, jnp.zeros_like(b))
            return jnp.concatenate([first[...].astype(BF16), b], axis=0)

        xb = both(ins[0], ins[1], True)
        for j, acc in enumerate(accs):
            acc[...] += _dot_tn(xb, both(ins[2 + 2 * j], ins[3 + 2 * j], False))

        @pl.when(i == steps - 1)
        def _():
            for p, chunk in enumerate(chunk_fn(*accs)):
                o_ref[p % 2, p // 2] = chunk.astype(BF16)

    def two_tiles(width):
        return [pl.BlockSpec((tm, width), lambda i: (2 * i, 0)),
                pl.BlockSpec((tm, width), lambda i: (jnp.minimum(2 * i + 1, tiles - 1), 0))]

    out = (2, N_DEV // 2) + tuple(chunk_shape)
    operands = [x, x] + [a for y in ys for a in (y, y)]
    return pl.pallas_call(
        body, name=name, grid=(steps,),
        in_specs=two_tiles(m) + [s for y in ys for s in two_tiles(y.shape[1])], out_specs=_acc(out),
        out_shape=jax.ShapeDtypeStruct(out, BF16), scratch_shapes=[pltpu.VMEM((m, y.shape[1]), F32) for y in ys],
        compiler_params=_cparams(),
    )(*operands)


def ffn_bwd(dh2, h1, g, gt, up, wgt, wut, wd, *, tm, name, ride=None):
    t = h1.shape[0]

    def body(dh2_ref, h_ref, g_ref, gt_ref, up_ref, wgt_ref, wut_ref, wd_ref, dh1_ref, hn_ref, act_ref, dgt_ref, dup_ref, dg_ref):
        dh2 = dh2_ref[...]
        dact = _dot_nt(dh2.astype(BF16), wd_ref[...])
        gt = gt_ref[...].astype(F32)
        up = up_ref[...].astype(F32)
        sg = _sigmoid(gt)
        silu = gt * sg
        act_ref[...] = (silu * up).astype(BF16)
        dgt = (dact * up * (sg * (1.0 + gt * (1.0 - sg)))).astype(BF16)
        dup = (dact * silu).astype(BF16)
        dgt_ref[...] = dgt
        dup_ref[...] = dup
        dhn = _dot(dgt, wgt_ref[...]) + _dot(dup, wut_ref[...])
        xhat, r = _rms(h_ref[...])
        gg = g_ref[...]
        hn_ref[...] = (xhat * gg).astype(BF16)
        dx, dg = _rms_bwd(dhn, xhat, r, gg)
        dh1_ref[...] = dh2 + dx

        @pl.when(pl.program_id(0) == 0)
        def _():
            dg_ref[...] = jnp.zeros_like(dg_ref)

        dg_ref[...] += dg

    return _carrying_call(
        body, ride, (dh2, h1, g, gt, up, wgt, wut, wd), name=name, grid=(t // tm,),
        in_specs=[_rows(tm, D_MODEL), _rows(tm, D_MODEL), _whole((1, D_MODEL)), _rows(tm, D_FF), _rows(tm, D_FF),
                  _whole((D_FF, D_MODEL)), _whole((D_FF, D_MODEL)), _whole((D_FF, D_MODEL))],
        out_specs=[_rows(tm, D_MODEL), _rows(tm, D_MODEL), _rows(tm, D_FF), _rows(tm, D_FF), _rows(tm, D_FF), _acc((1, D_MODEL))],
        out_shape=[jax.ShapeDtypeStruct((t, D_MODEL), F32), jax.ShapeDtypeStruct((t, D_MODEL), BF16),
                   jax.ShapeDtypeStruct((t, D_FF), BF16), jax.ShapeDtypeStruct((t, D_FF), BF16),
                   jax.ShapeDtypeStruct((t, D_FF), BF16), jax.ShapeDtypeStruct((1, D_MODEL), F32)])


def merge_bwd(dh1, z, pa, pb, pooled, pw, ps, wpa, wpb, wo, *, tm, name, ride=None):
    t = dh1.shape[0]

    def body(dh1_ref, ga_ref, gb_ref, pa_ref, pb_ref, pooled_ref, pw_ref, ps_ref, wpa_ref, wpb_ref, wo_ref,
             dga_ref, dgb_ref, dpa_ref, dpb_ref, do_ref, dpool_ref, dps_ref, dpw_ref):
        dmg = _dot_nt(dh1_ref[...].astype(BF16), wo_ref[...])
        sa = _sigmoid(ga_ref[...])
        sb = _sigmoid(gb_ref[...])
        dga_ref[...] = (dmg * pa_ref[...].astype(F32) * sa * (1.0 - sa)).astype(BF16)
        dgb_ref[...] = (dmg * pb_ref[...].astype(F32) * sb * (1.0 - sb)).astype(BF16)
        dpa = (dmg * sa).astype(BF16)
        dpb = (dmg * sb).astype(BF16)
        dpa_ref[...] = dpa
        dpb_ref[...] = dpb
        do_ref[...] = _dot_nt(dpb, wpb_ref[...]).astype(BF16)
        da = _dot_nt(dpa, wpa_ref[...])
        pooled = pooled_ref[...]
        ps = ps_ref[...]

        @pl.when(pl.program_id(0) == 0)
        def _():
            dps_ref[...] = jnp.zeros_like(dps_ref)
            dpw_ref[...] = jnp.zeros_like(dpw_ref)

        dps, dpool = [], []
        for g in range(len(POOL_WINDOWS)):
            gs = slice(g * POOL_GROUP, (g + 1) * POOL_GROUP)
            y = _dot(pooled[:, gs], pw_ref[g])
            dps.append(jnp.sum(da[:, gs] * y, axis=0, keepdims=True))
            dy = (da[:, gs] * ps[:, gs]).astype(BF16)
            dpool.append(_dot_nt(dy, pw_ref[g]))
            dpw_ref[g] += _dot_tn(pooled[:, gs], dy)
        dps_ref[...] += jnp.concatenate(dps, axis=1)
        dpool_ref[...] = jnp.concatenate(dpool, axis=1)

    return _carrying_call(
        body, ride, (dh1, z, z, pa, pb, pooled, pw, ps, wpa, wpb, wo), name=name, grid=(t // tm,),
        in_specs=[_rows(tm, D_MODEL), _rows(tm, D_MODEL, 1), _rows(tm, D_MODEL, 2), _rows(tm, D_MODEL), _rows(tm, D_MODEL),
                  _rows(tm, POOL_WIDTH), _whole((4, POOL_GROUP, POOL_GROUP)),
                  _whole((1, POOL_WIDTH)), _whole((POOL_WIDTH, D_MODEL)), _whole((D_MODEL, D_MODEL)), _whole((D_MODEL, D_MODEL))],
        out_specs=[_rows(tm, D_MODEL), _rows(tm, D_MODEL), _rows(tm, D_MODEL), _rows(tm, D_MODEL), _rows(tm, D_MODEL),
                   _rows(tm, POOL_WIDTH), _acc((1, POOL_WIDTH)), _acc((4, POOL_GROUP, POOL_GROUP))],
        out_shape=[jax.ShapeDtypeStruct((t, D_MODEL), BF16)] * 5
        + [jax.ShapeDtypeStruct((t, POOL_WIDTH), F32), jax.ShapeDtypeStruct((1, POOL_WIDTH), F32),
           jax.ShapeDtypeStruct((4, POOL_GROUP, POOL_GROUP), F32)])


def attn_bwd(q, k, v, o, do, lse, *, nb, lp, hb, name, ride=None):
    t = q.shape[0]
    nq, tail = lp // TQ, lp % TQ
    assert tail % LANES == 0

    def body(q_ref, k_ref, v_ref, o_ref, do_ref, lse_ref, dq_ref, dk_ref, dv_ref, kt, doh, lse_row, delta_row, dqt):
        lane = lax.broadcasted_iota(jnp.int32, (lp, LANES), 1)
        first = lane < V_DIM
        sub = lax.broadcasted_iota(jnp.int32, (LANES, lp), 0)
        for pr in range(hb // 2):
            ls = slice(pr * LANES, (pr + 1) * LANES)
            do = do_ref[:, ls]
            doh[2 * pr] = jnp.where(first, do, jnp.zeros_like(do))
            doh[2 * pr + 1] = jnp.where(first, jnp.zeros_like(do), do)
            prod_t = (do.astype(F32) * o_ref[:, ls].astype(F32)).T
            delta_row[2 * pr] = jnp.sum(jnp.where(sub < V_DIM, prod_t, 0.0), axis=0, keepdims=True)
            delta_row[2 * pr + 1] = jnp.sum(jnp.where(sub < V_DIM, 0.0, prod_t), axis=0, keepdims=True)
        for hd in range(hb):
            lse_row[hd] = lse_ref[hd] * LOG2E
            kt[hd] = k_ref[:, hd * HEAD_SLOT:(hd + 1) * HEAD_SLOT].T
        dqt[...] = jnp.zeros(dqt.shape, F32)
        heads = range(hb)
        hss = [slice(hd * HEAD_SLOT, (hd + 1) * HEAD_SLOT) for hd in heads]

        def k_block(ks, tk, next_q):
            keep = lax.broadcasted_iota(jnp.int32, (tk, tk), 0) <= lax.broadcasted_iota(jnp.int32, (tk, tk), 1)

            def q_steps(blocks, c, masked):
                work = [(qs, tq, hd) for qs, tq in blocks for hd in heads]
                qhs = [q_ref[pl.ds(qs, tq), hss[hd]] for qs, tq, hd in work]
                dos = [doh[hd, pl.ds(qs, tq), :] for qs, tq, hd in work]
                sts = [_dot_nt(k_ref[pl.ds(ks, tk), hss[hd]], qhs[i]) for i, (_, _, hd) in enumerate(work)]
                dpts = [_dot_nt(v_ref[pl.ds(ks, tk), (hd // 2) * LANES:(hd // 2 + 1) * LANES], dos[i])
                        for i, (_, _, hd) in enumerate(work)]
                pts, dsts = [], []
                for i, (qs, tq, hd) in enumerate(work):
                    st = jnp.where(keep, sts[i], MASK_VALUE) if masked else sts[i]
                    pt = jnp.exp2(st * EXP2_SCALE - lse_row[hd, :, pl.ds(qs, tq)])
                    dsts.append((pt * (dpts[i] - delta_row[hd, :, pl.ds(qs, tq)])).astype(BF16))
                    pts.append(pt.astype(BF16))
                dvs = [_dot(pts[i], dos[i]) for i in range(len(work))]
                dks = [_dot(dsts[i], qhs[i]) for i in range(len(work))]
                dqs = [_dot(kt[hd, :, pl.ds(ks, tk)], dsts[i]) for i, (_, _, hd) in enumerate(work)]
                c = list(c)
                for i, (qs, tq, hd) in enumerate(work):
                    dqt[hd, :, pl.ds(qs, tq)] += dqs[i]
                    c[hd] = (c[hd][0] + dks[i], c[hd][1] + dvs[i])
                return tuple(c)

            zero = jnp.zeros((tk, LANES), F32)
            c = q_steps([(ks, tk)], tuple((zero, zero) for _ in heads), True)
            if next_q is not None:
                def two_blocks(i, c):
                    qs = pl.multiple_of((next_q + 2 * i) * TQ, TQ)
                    return q_steps([(qs, TQ), (qs + TQ, TQ)], c, False)

                pairs = lax.div(nq - next_q, 2)
                c = lax.fori_loop(0, pairs, two_blocks, c)
                c = lax.fori_loop(next_q + 2 * pairs, nq, lambda qi, c: q_steps([(pl.multiple_of(qi * TQ, TQ), TQ)], c, False), c)
                if tail:
                    c = q_steps([(nq * TQ, tail)], c, False)
            for hd in heads:
                dk_ref[pl.ds(ks, tk), hss[hd]] = c[hd][0] * SM_SCALE
            for pr in range(hb // 2):
                dv_ref[pl.ds(ks, tk), pr * LANES:(pr + 1) * LANES] = c[2 * pr][1] + c[2 * pr + 1][1]

        def whole_k_block(kj, carry):
            k_block(pl.multiple_of(kj * TK, TK), TK, kj + 1)
            return carry

        lax.fori_loop(0, nq, whole_k_block, 0)
        if tail:
            k_block(nq * TQ, tail, None)
        for hd in range(hb):
            dq_ref[:, hd * HEAD_SLOT:(hd + 1) * HEAD_SLOT] = dqt[hd].T * SM_SCALE

    blk = lambda w: pl.BlockSpec((lp, w), lambda b, g: (b, g))
    return _carrying_call(
        body, ride, (q, k, v, o, do, lse), name=name, grid=(nb, N_HEADS // hb),
        in_specs=[blk(hb * HEAD_SLOT), blk(hb * HEAD_SLOT), blk(hb * V_DIM), blk(hb * V_DIM), blk(hb * V_DIM),
                  pl.BlockSpec((hb, 1, lp), lambda b, g: (g, 0, b))],
        out_specs=[blk(hb * HEAD_SLOT), blk(hb * HEAD_SLOT), blk(hb * V_DIM)],
        out_shape=[jax.ShapeDtypeStruct((t, QK_WIDTH), F32), jax.ShapeDtypeStruct((t, QK_WIDTH), F32),
                   jax.ShapeDtypeStruct((t, D_MODEL), F32)],
        scratch_shapes=[pltpu.VMEM((hb, HEAD_SLOT, lp), BF16), pltpu.VMEM((hb, lp, LANES), BF16), pltpu.VMEM((hb, 1, lp), F32),
                        pltpu.VMEM((hb, 1, lp), F32), pltpu.VMEM((hb, HEAD_SLOT, lp), F32)])


def in_proj_bwd(dh1, h, g_mix, z, dq, dk, dv, dga, dgb, dpool, win, gq, gkv, wuq, wuk, wuv, rope, *, tm, lp, nb, name):
    t = h.shape[0]
    hb = tm // POOL_HALO
    last_halo = t // POOL_HALO - 1

    def body(dh1_ref, h_ref, g_ref, zcq_ref, zckv_ref, dq_ref, dk_ref, dv_ref, dga_ref, dgb_ref, dpool_ref, dnext_ref,
             win_ref, gq_ref, gkv_ref, wuq_ref, wuk_ref, wuv_ref, rope_ref,
             dh_ref, hn_ref, dz_ref, cqn_ref, ckvn_ref, dqb_ref, dkb_ref, dvb_ref, dg_ref, dgq_ref, dgkv_ref):
        i = pl.program_id(0)
        rope_t = rope_ref[...]
        dqb = _rope_bwd(dq_ref[...], *_rope_tables(rope_t, N_HEADS)).astype(BF16)
        dqb_ref[...] = dqb
        xq, rq = _rms(zcq_ref[...])
        gq_v = gq_ref[...]
        cqn_ref[...] = (xq * gq_v).astype(BF16)
        dcq, dgq = _rms_bwd(_dot_nt(dqb, wuq_ref[...]), xq, rq, gq_v)
        dk = dk_ref[...]
        dkb = dk.astype(BF16)
        dvb = dv_ref[...].astype(BF16)
        dkb_ref[...] = dkb
        dvb_ref[...] = dvb
        xkv, rkv = _rms(zckv_ref[...])
        gkv_v = gkv_ref[...]
        ckvn_ref[...] = (xkv * gkv_v).astype(BF16)
        dckv, dgkv = _rms_bwd(_dot_nt(dkb, wuk_ref[...]) + _dot_nt(dvb, wuv_ref[...]), xkv, rkv, gkv_v)
        dks = dk[:, :HEAD_SLOT]
        for hd in range(1, N_HEADS):
            dks = dks + dk[:, hd * HEAD_SLOT:(hd + 1) * HEAD_SLOT]
        dzk = _rope_bwd(dks, *_rope_tables(rope_t, 1))
        dp_cur = dpool_ref[...]
        dp_ext = jnp.concatenate([dp_cur, dnext_ref[...]], axis=0)
        r = lax.broadcasted_iota(jnp.int32, (tm, tm + POOL_HALO), 0)
        e = lax.broadcasted_iota(jnp.int32, (tm, tm + POOL_HALO), 1)
        gt_col = i * tm + lax.broadcasted_iota(jnp.int32, (1, tm + POOL_HALO), 1)
        pos_col = _seq_pos(gt_col, lp, nb)
        gt_row = i * tm + lax.broadcasted_iota(jnp.int32, (tm + POOL_HALO, 1), 0)
        pos_row = _seq_pos(gt_row, lp, nb)
        dus = []
        for g, w in enumerate(POOL_WINDOWS):
            gs = slice(g * POOL_GROUP, (g + 1) * POOL_GROUP)
            band = jnp.where((e - r >= 0) & (e - r < jnp.minimum(pos_col + 1, w)) & (gt_col < t), 1.0, 0.0).astype(BF16)
            scaled = jnp.where(gt_row < t, dp_ext[:, gs] / jnp.minimum(pos_row + 1, w).astype(F32), 0.0).astype(BF16)
            dus.append(_dot(band, scaled) - dp_cur[:, gs])
        dz = jnp.concatenate(dus + [dcq, dckv, dzk], axis=1).astype(BF16)
        dz = jnp.concatenate([dz, dga_ref[...], dgb_ref[...]], axis=1)
        dz_ref[...] = dz
        xhat, rr = _rms(h_ref[...])
        gg = g_ref[...]
        hn_ref[...] = (xhat * gg).astype(BF16)
        dx, dg = _rms_bwd(_dot_nt(dz, win_ref[...]), xhat, rr, gg)
        dh_ref[...] = dh1_ref[...] + dx

        @pl.when(i == 0)
        def _():
            dg_ref[...] = jnp.zeros_like(dg_ref)
            dgq_ref[...] = jnp.zeros_like(dgq_ref)
            dgkv_ref[...] = jnp.zeros_like(dgkv_ref)

        dg_ref[...] += dg
        dgq_ref[...] += dgq
        dgkv_ref[...] += dgkv

    nxt = pl.BlockSpec((POOL_HALO, POOL_WIDTH), lambda i: (jnp.minimum((i + 1) * hb, last_halo), 0))
    return pl.pallas_call(
        body, name=name, grid=(t // tm,),
        in_specs=[_rows(tm, D_MODEL), _rows(tm, D_MODEL), _whole((1, D_MODEL)), _rows(tm, Q_RANK, Z_CQ // Q_RANK),
                  _rows(tm, KV_RANK, Z_CKV // KV_RANK), _rows(tm, QK_WIDTH), _rows(tm, QK_WIDTH), _rows(tm, D_MODEL),
                  _rows(tm, D_MODEL), _rows(tm, D_MODEL), _rows(tm, POOL_WIDTH), nxt,
                  _whole((D_MODEL, DZ)), _whole((1, Q_RANK)), _whole((1, KV_RANK)), _whole((Q_RANK, QK_WIDTH)),
                  _whole((KV_RANK, QK_WIDTH)), _whole((KV_RANK, D_MODEL)), _rows(tm, 4 * LANES)],
        out_specs=[_rows(tm, D_MODEL), _rows(tm, D_MODEL), _rows(tm, DZ), _rows(tm, Q_RANK), _rows(tm, KV_RANK),
                   _rows(tm, QK_WIDTH), _rows(tm, QK_WIDTH), _rows(tm, D_MODEL),
                   _acc((1, D_MODEL)), _acc((1, Q_RANK)), _acc((1, KV_RANK))],
        out_shape=[jax.ShapeDtypeStruct((t, D_MODEL), F32), jax.ShapeDtypeStruct((t, D_MODEL), BF16),
                   jax.ShapeDtypeStruct((t, DZ), BF16), jax.ShapeDtypeStruct((t, Q_RANK), BF16),
                   jax.ShapeDtypeStruct((t, KV_RANK), BF16), jax.ShapeDtypeStruct((t, QK_WIDTH), BF16),
                   jax.ShapeDtypeStruct((t, QK_WIDTH), BF16), jax.ShapeDtypeStruct((t, D_MODEL), BF16),
                   jax.ShapeDtypeStruct((1, D_MODEL), F32), jax.ShapeDtypeStruct((1, Q_RANK), F32),
                   jax.ShapeDtypeStruct((1, KV_RANK), F32)],
        compiler_params=_cparams(),
    )(dh1, h, g_mix, z, z, dq, dk, dv, dga, dgb, dpool, dpool, win, gq, gkv, wuq, wuk, wuv, rope)


_MESH = pl.DeviceIdType.MESH


def _place():
    x, y, c = lax.axis_index("x"), lax.axis_index("y"), lax.axis_index("c")
    return x, y, c, 4 * x + 2 * y + c


def _peer(x, y, c, k):
    px, py, pc = (1 - x) if k & 4 else x, (1 - y) if k & 2 else y, (1 - c) if k & 1 else c
    return (px, py, pc), 4 * px + 2 * py + pc


ALL_PEERS = tuple(range(1, N_DEV))
CHIP_PEERS = (2, 4, 6)
N_CHIPS = N_DEV // 2


def _sem_scratch(n, m):
    return [pltpu.SemaphoreType.DMA((n, m)), pltpu.SemaphoreType.DMA((n, m)), pltpu.SemaphoreType.DMA((n,))]


class Exchange:
    def __init__(self, arrays, out_shapes, sem_cols, plan, aliased=False):
        self.arrays, self.out_shapes, self.plan = list(arrays), list(out_shapes), plan
        self.scratch = _sem_scratch(len(self.arrays), sem_cols)
        self.aliased = aliased

    def split(self, refs):
        n = len(self.arrays)
        return refs[:n], refs[n:2 * n], refs[2 * n:]

    def start(self, srcs, dsts, sems):
        local, sends, _ = self.plan(srcs, dsts, *sems)
        for cp in local + sends:
            cp.start()

    def wait(self, srcs, dsts, sems):
        local, sends, recvs = self.plan(srcs, dsts, *sems)
        for cp in recvs:
            cp.wait_recv()
        for cp in sends:
            cp.wait_send()
        for cp in local:
            cp.wait()

    def aliases(self, first_in, first_out):
        return {first_in + j: first_out + j for j in range(len(self.arrays))} if self.aliased else {}

    def run(self, name):
        def body(*refs):
            srcs, dsts, sems = self.split(refs)
            self.start(srcs, dsts, sems)
            self.wait(srcs, dsts, sems)

        n = len(self.arrays)
        return pl.pallas_call(body, name=name, in_specs=[_ANY] * n, out_specs=[_ANY] * n, out_shape=self.out_shapes,
                              scratch_shapes=self.scratch, input_output_aliases=self.aliases(0, 0))(*self.arrays)


def exchange(arrays, scatter, peers, by_chip=False):
    slots = N_CHIPS if by_chip else N_DEV

    def plan(srcs, dsts, send_sems, recv_sems, local_sems):
        x, y, c, me = _place()
        mine = 2 * x + y if by_chip else me
        local = [pltpu.make_async_copy(src.at[mine] if scatter else src, dst.at[mine], local_sems.at[j])
                 for j, (src, dst) in enumerate(zip(srcs, dsts))]
        sends, recvs = [], []
        for t, k in enumerate(peers):
            peer, pidx = _peer(x, y, c, k)
            theirs = 2 * peer[0] + peer[1] if by_chip else pidx
            for j, (src, dst) in enumerate(zip(srcs, dsts)):
                part = src.at[theirs] if scatter else src
                sems = dict(send_sem=send_sems.at[j, t], recv_sem=recv_sems.at[j, t], device_id=peer, device_id_type=_MESH)
                sends.append(pltpu.make_async_remote_copy(src_ref=part, dst_ref=dst.at[mine], **sems))
                recvs.append(pltpu.make_async_remote_copy(src_ref=part, dst_ref=dst.at[theirs], **sems))
        return local, sends, recvs

    shapes = [jax.ShapeDtypeStruct(a.shape if scatter else (slots,) + a.shape, a.dtype) for a in arrays]
    return Exchange(arrays, shapes, len(peers), plan)


def second_hop(gathered):
    def plan(srcs, dsts, send_sems, recv_sems, local_sems):
        x, y, c, me = _place()
        sibling, _ = _peer(x, y, c, 1)
        sends, recvs = [], []
        for t, k in enumerate(CHIP_PEERS):
            _, landed = _peer(x, y, c, k)
            _, coming = _peer(x, y, c, k ^ 1)
            for j, buf in enumerate(dsts):
                sems = dict(send_sem=send_sems.at[j, t], recv_sem=recv_sems.at[j, t], device_id=sibling, device_id_type=_MESH)
                sends.append(pltpu.make_async_remote_copy(src_ref=buf.at[landed], dst_ref=buf.at[landed], **sems))
                recvs.append(pltpu.make_async_remote_copy(src_ref=buf.at[coming], dst_ref=buf.at[coming], **sems))
        return [], sends, recvs

    shapes = [jax.ShapeDtypeStruct(a.shape, a.dtype) for a in gathered]
    return Exchange(gathered, shapes, len(CHIP_PEERS), plan, aliased=True)


FIRST_HOP_PEERS = (1,) + CHIP_PEERS


def _gather_two_level(arrays, name):
    n = len(arrays)

    def body(*refs):
        srcs, dsts, (send_sems, recv_sems, local_sems) = refs[:n], refs[n:2 * n], refs[2 * n:]
        x, y, c, me = _place()
        sibling, sidx = _peer(x, y, c, 1)

        def copy(j, sem, block, to, src=None):
            rows = dsts[j].at[block]
            return pltpu.make_async_remote_copy(src_ref=rows if src is None else src, dst_ref=rows, send_sem=send_sems.at[j, sem],
                                                recv_sem=recv_sems.at[j, sem], device_id=to, device_id_type=_MESH)

        local = [pltpu.make_async_copy(srcs[j], dsts[j].at[me], local_sems.at[j]) for j in range(n)]
        for cp in local:
            cp.start()
        first = [copy(j, 1 + t, me, _peer(x, y, c, k)[0], src=srcs[j]) for t, k in enumerate(CHIP_PEERS) for j in range(n)]
        first += [copy(j, 0, me, sibling, src=srcs[j]) for j in range(n)]
        for cp in first:
            cp.start()
        passed = []
        for t, k in enumerate(CHIP_PEERS):
            peer, pidx = _peer(x, y, c, k)
            for j in range(n):
                copy(j, 1 + t, pidx, peer).wait_recv()
                passed.append(copy(j, 4 + t, pidx, sibling))
                passed[-1].start()
        for j in range(n):
            copy(j, 0, sidx, sibling).wait_recv()
        for t, k in enumerate(CHIP_PEERS):
            _, pidx = _peer(x, y, c, k ^ 1)
            for j in range(n):
                copy(j, 4 + t, pidx, sibling).wait_recv()
        for cp in first + passed:
            cp.wait_send()
        for cp in local:
            cp.wait()

    shapes = [jax.ShapeDtypeStruct((N_DEV,) + a.shape, a.dtype) for a in arrays]
    return pl.pallas_call(body, name=name, in_specs=[_ANY] * n, out_specs=[_ANY] * n, out_shape=shapes,
                          scratch_shapes=_sem_scratch(n, 1 + 2 * len(CHIP_PEERS)))(*arrays)


def to_sibling(arrays):
    def plan(srcs, dsts, send_sems, recv_sems, local_sems):
        x, y, c, _ = _place()
        sibling, _ = _peer(x, y, c, 1)
        copies = [pltpu.make_async_remote_copy(src_ref=src.at[1 - c], dst_ref=dst, send_sem=send_sems.at[j, 0],
                                               recv_sem=recv_sems.at[j, 0], device_id=sibling, device_id_type=_MESH)
                  for j, (src, dst) in enumerate(zip(srcs, dsts))]
        return [], copies, copies

    return Exchange(arrays, [jax.ShapeDtypeStruct(a.shape[1:], a.dtype) for a in arrays], 1, plan)


def combine(a, b):
    assert not (a.aliased or b.aliased)
    na, nsem = len(a.arrays), len(a.scratch)

    def plan(srcs, dsts, *sems):
        return tuple(u + v for u, v in zip(a.plan(srcs[:na], dsts[:na], *sems[:nsem]), b.plan(srcs[na:], dsts[na:], *sems[nsem:])))

    both = Exchange(a.arrays + b.arrays, a.out_shapes + b.out_shapes, 1, plan)
    both.scratch = a.scratch + b.scratch
    return both


def pair_add(own, theirs, core, *, name):
    _, ns, r, c = own.shape
    rb = _row_block(r, c)

    def body(core_ref, a_ref, b_ref, o_ref):
        o_ref[...] = (a_ref[...].astype(F32) + b_ref[...].astype(F32)).astype(o_ref.dtype)

    return pl.pallas_call(
        body, name=name,
        grid_spec=pltpu.PrefetchScalarGridSpec(
            num_scalar_prefetch=1, grid=(ns, r // rb),
            in_specs=[pl.BlockSpec((None, None, rb, c), lambda i, j, core_ref: (core_ref[0], i, j, 0)),
                      pl.BlockSpec((None, rb, c), lambda i, j, core_ref: (i, j, 0))],
            out_specs=pl.BlockSpec((None, rb, c), lambda i, j, core_ref: (i, j, 0))),
        out_shape=jax.ShapeDtypeStruct((ns, r, c), own.dtype), compiler_params=_cparams(),
    )(core, own, theirs)


ADAMW_BLOCK_BYTES = 1 << 20


def _row_block(r, c):
    for rb in range(r, 0, -1):
        if r % rb == 0 and (rb % 16 == 0 or rb == r) and rb * c * 4 <= ADAMW_BLOCK_BYTES:
            return rb
    return r


def adamw(w, m, v, parts, *, name):
    depth, r, c = w.shape
    n_parts = parts[0].shape[0]
    rb = _row_block(r, c)

    def body(w_ref, m_ref, v_ref, *refs):
        p_refs, (g_ref, d_ref, nm_ref, nv_ref) = refs[:depth], refs[depth:]

        def total(p_ref):
            g = p_ref[0].astype(F32)
            for j in range(1, n_parts):
                g = g + p_ref[j].astype(F32)
            return g

        g = total(p_refs[0])
        for l in range(1, depth):
            g = jnp.where(pl.program_id(0) == l, total(p_refs[l]), g)
        g_ref[...] = g
        m_new = ADAM_B1 * m_ref[...] + (1.0 - ADAM_B1) * g
        v_new = ADAM_B2 * v_ref[...] + (1.0 - ADAM_B2) * (g * g)
        m_hat = m_new / (1.0 - ADAM_B1 ** ADAM_STEP)
        v_hat = v_new / (1.0 - ADAM_B2 ** ADAM_STEP)
        d_ref[...] = -ADAM_LR * (m_hat / (jnp.sqrt(v_hat) + ADAM_EPS) + ADAM_WD * w_ref[...])
        nm_ref[...] = m_new
        nv_ref[...] = v_new

    wblk = pl.BlockSpec((None, rb, c), lambda l, i: (l, i, 0))
    pblk = pl.BlockSpec((n_parts, rb, c), lambda l, i: (0, i, 0))
    return pl.pallas_call(
        body, name=name, grid=(depth, r // rb),
        in_specs=[wblk, wblk, wblk] + [pblk] * depth, out_specs=[wblk] * 4,
        out_shape=[jax.ShapeDtypeStruct((depth, r, c), F32)] * 4, compiler_params=_cparams(),
    )(w, m, v, *parts)


BIG = (("w_in", 2), ("w_uq", 2), ("w_ukv", 2), ("w_pa", 2), ("w_pb", 1), ("w_o", 1), ("w_gate", 2), ("w_up", 2), ("w_down", 1))
SMALL = ("norm_mix_g", "pool_w", "pool_scale", "q_norm_g", "kv_norm_g", "norm_ffn_g", "final_norm_g")
WEIGHTS = ("meta_tokens", "norm_mix_g", "w_in", "pool_w", "pool_scale", "q_norm_g", "kv_norm_g", "w_uq", "w_ukv", "w_pa", "w_pb",
           "w_o", "norm_ffn_g", "w_gate", "w_up", "w_down", "final_norm_g")
HEAD_QK = QK_NOPE + QK_ROPE
KR_END = Z_KR + QK_ROPE


def _cat_cols(parts):
    return [jnp.concatenate(parts, axis=1)]


def _cat_rows(parts):
    return [jnp.concatenate(parts, axis=0)]


def _arr_w_in(parts):
    full = jnp.concatenate(parts, axis=1)
    zc = lambda n: jnp.zeros((full.shape[0], n), full.dtype)
    return [jnp.concatenate([full[:, :Z_KR], zc(QK_NOPE), full[:, Z_KR:KR_END], zc(LANES - HEAD_QK), full[:, KR_END:]], axis=1)]


def _arr_w_uq(parts):
    full = jnp.concatenate(parts, axis=1)
    z = jnp.zeros((full.shape[0], HEAD_SLOT - HEAD_QK), full.dtype)
    pieces = []
    for hd in range(N_HEADS):
        pieces += [full[:, hd * HEAD_QK:(hd + 1) * HEAD_QK], z]
    return [jnp.concatenate(pieces, axis=1)]


def _arr_w_ukv(parts):
    full = jnp.concatenate(parts, axis=1)
    z = jnp.zeros((full.shape[0], HEAD_SLOT - QK_NOPE), full.dtype)
    wide = QK_NOPE + V_DIM
    k, v = [], []
    for hd in range(N_HEADS):
        k += [full[:, hd * wide:hd * wide + QK_NOPE], z]
        v.append(full[:, hd * wide + QK_NOPE:(hd + 1) * wide])
    return [jnp.concatenate(k, axis=1), jnp.concatenate(v, axis=1)]


def arrange(g, fn, out_shapes, name):
    def body(g_ref, *o_refs):
        for o_ref, val in zip(o_refs, fn([g_ref[p] for p in range(N_DEV)])):
            o_ref[...] = val

    return pl.pallas_call(
        body, name=name, grid=(1,),
        in_specs=[pl.BlockSpec(g.shape, lambda i: (0, 0, 0))],
        out_specs=[pl.BlockSpec(s, lambda i: (0, 0)) for s in out_shapes],
        out_shape=[jax.ShapeDtypeStruct(s, g.dtype) for s in out_shapes], compiler_params=_cparams(),
    )(g)


def _arranged_ranges(lo, hi):
    out = []
    for a, b, shift in ((0, Z_KR, 0), (Z_KR, KR_END, QK_NOPE), (KR_END, D_IN, LANES - QK_ROPE)):
        s, e = max(lo, a), min(hi, b)
        if s < e:
            out.append((s + shift, e + shift))
    return out


def _chunks_w_in(acc):
    cs = D_IN // N_DEV
    return [jnp.concatenate([acc[:, a:b] for a, b in _arranged_ranges(p * cs, (p + 1) * cs)], axis=1) for p in range(N_DEV)]


def _chunks_w_uq(acc):
    per = N_HEADS // N_DEV
    return [jnp.concatenate([acc[:, hd * HEAD_SLOT:hd * HEAD_SLOT + HEAD_QK] for hd in range(p * per, (p + 1) * per)], axis=1)
            for p in range(N_DEV)]


def _chunks_w_ukv(acc_k, acc_v):
    per = N_HEADS // N_DEV
    out = []
    for p in range(N_DEV):
        pieces = []
        for hd in range(p * per, (p + 1) * per):
            pieces += [acc_k[:, hd * HEAD_SLOT:hd * HEAD_SLOT + QK_NOPE], acc_v[:, hd * V_DIM:(hd + 1) * V_DIM]]
        out.append(jnp.concatenate(pieces, axis=1))
    return out


def _chunks_cols(acc):
    cs = acc.shape[1] // N_DEV
    return [acc[:, p * cs:(p + 1) * cs] for p in range(N_DEV)]


def _chunks_rows(acc):
    rs = acc.shape[0] // N_DEV
    return [acc[p * rs:(p + 1) * rs, :] for p in range(N_DEV)]


def _chunks_cols_transposed(acc):
    at = acc[...].T
    rs = at.shape[0] // N_DEV
    return [at[p * rs:(p + 1) * rs, :] for p in range(N_DEV)]


def _pack(parts, row_multiple):
    flat = jnp.concatenate([p.reshape(-1) for p in parts])
    return jnp.pad(flat, (0, -flat.shape[0] % (row_multiple * LANES))).reshape(-1, LANES)


def _unpack(packed, shapes):
    flat, out, off = packed.reshape(-1), [], 0
    for s in shapes:
        n = 1
        for d in s:
            n *= d
        out.append(flat[off:off + n].reshape(s))
        off += n
    return out


def _rope_table(lp, nb):
    inv = 1.0 / (ROPE_THETA ** (jnp.arange(0, QK_ROPE, 2, dtype=F32) / QK_ROPE))
    ang = jnp.arange(lp, dtype=F32)[:, None] * inv[None, :]
    cos, sin = jnp.cos(ang), jnp.sin(ang)
    z = lambda n: jnp.zeros((lp, n), F32)
    tail = LANES - QK_NOPE - QK_ROPE
    c = jnp.concatenate([jnp.ones((lp, QK_NOPE), F32), cos, cos, z(tail)], axis=1)
    cr = jnp.concatenate([z(QK_NOPE), cos, cos, z(tail)], axis=1)
    s1 = jnp.concatenate([z(QK_NOPE), -sin, z(HALF_ROPE), z(tail)], axis=1)
    s2 = jnp.concatenate([z(QK_NOPE), z(HALF_ROPE), sin, z(tail)], axis=1)
    return jnp.tile(jnp.concatenate([c, cr, s1, s2], axis=1), (nb, 1))


MIX = ("w_in", "w_uq", "w_ukv", "w_pa", "w_pb", "w_o")
FFN = ("w_gate", "w_up", "w_down")
TRANSPOSED = ("w_gate", "w_up")
ARRANGERS = {
    "w_in": (_arr_w_in, (("win", (D_MODEL, DZ)),)), "w_uq": (_arr_w_uq, (("wuq", (Q_RANK, QK_WIDTH)),)),
    "w_ukv": (_arr_w_ukv, (("wuk", (KV_RANK, QK_WIDTH)), ("wuv", (KV_RANK, D_MODEL)))),
    "w_pa": (_cat_cols, (("wpa", (POOL_WIDTH, D_MODEL)),)), "w_pb": (_cat_rows, (("wpb", (D_MODEL, D_MODEL)),)),
    "w_o": (_cat_rows, (("wo", (D_MODEL, D_MODEL)),)), "w_gate": (_cat_rows, (("wgt", (D_FF, D_MODEL)),)),
    "w_up": (_cat_rows, (("wut", (D_FF, D_MODEL)),)), "w_down": (_cat_rows, (("wd", (D_FF, D_MODEL)),)),
}


def _operands(gathered, names, l):
    p = {}
    for n in names:
        fn, outs = ARRANGERS[n]
        if fn is _cat_rows:
            p[outs[0][0]] = gathered[n].reshape(outs[0][1])
            continue
        for (key, _), a in zip(outs, arrange(gathered[n], fn, [s for _, s in outs], f"arrange_{n}_{l}")):
            p[key] = a
    return p


def _small_operands(small, l):
    pw = small["pool_w"][l].astype(BF16)
    return dict(g_mix=small["norm_mix_g"][l][None], gq=small["q_norm_g"][l][None], gkv=small["kv_norm_g"][l][None],
                g_ffn=small["norm_ffn_g"][l][None], ps=small["pool_scale"][l][None], pw=pw)


class MeshComm:
    def __init__(self, w, meta_tokens):
        self.src = lambda n, l: w[n][l].astype(BF16)
        self.meta_tokens = meta_tokens
        self.core = lax.axis_index("c").astype(jnp.int32).reshape(1)
        self.rides = {0: [(n, 0) for n in FFN] + [(n, 1) for n in MIX], 1: [(n, 1) for n in FFN]}

    def first_weights(self):
        got = _gather_two_level([self.src(n, 0) for n in MIX] + [self.meta_tokens], "gather_mix_0")
        return dict(zip(MIX, got)), jnp.moveaxis(got[-1], 0, 1).reshape(N_META, D_MODEL)

    def first_hop(self, l):
        return exchange([self.src(n, layer) for n, layer in self.rides[l]], False, FIRST_HOP_PEERS)

    def second_hop(self, l, landed):
        return second_hop(landed)

    def carried(self, l, full, names, layer):
        return {n: full[self.rides[l].index((n, layer))] for n in names}

    def pair_exchange(self, own):
        return to_sibling(own)

    def pair_add(self, own, theirs, names, tag):
        return [pair_add(a, b, self.core, name=f"pair_add_{n}_{tag}") for n, a, b in zip(names, own, theirs)]

    def last_pair_exchange(self, own, small):
        got = combine(to_sibling(own), exchange([small], False, ALL_PEERS)).run("pair_grads_mix_0")
        return got[:-1], got[-1]

    def scatter(self, sums):
        return exchange(sums, True, CHIP_PEERS, by_chip=True)

    def scatter_now(self, sums, name):
        return self.scatter(sums).run(name)


HEADS_FWD, HEADS_BWD = 8, 4
TILE_ROWS, TILE_ROWS_BWD = 512, 256


def _tile(t, target):
    n = max(1, -(-t // (target + target // 8)))
    while t % n or (t // n) % 16:
        n += 1
    return t // n


def _wgrad_tile(t):
    return max(tm for tm in (2 * TQ, TQ, LANES) if t % tm == 0)


def _ffn_bwd_part(dh2, p, s, tag, ride):
    d, ff = D_MODEL, D_FF // N_DEV
    t = dh2.shape[0]
    wg_ = lambda n, x, ys, fn, shape: wgrad(x, ys, fn, shape, tm=_wgrad_tile(t), name=f"wgrad_{n}_{tag}")
    (dh1, hn2, act, dgt, dup, dg_ffn), brought = ffn_bwd(dh2, s["h1"], p["g_ffn"], s["gt"], s["up"], p["wgt"], p["wut"], p["wd"],
                                                         tm=_tile(t, TILE_ROWS_BWD), name=f"ffn_bwd_{tag}", ride=ride)
    chunks = [wg_("gate", hn2, [dgt], _chunks_cols_transposed, (ff, d)), wg_("up", hn2, [dup], _chunks_cols_transposed, (ff, d)),
              wg_("down", act, [dh2], _chunks_rows, (ff, d))]
    return dh1, chunks, dict(norm_ffn_g=dg_ffn[0]), brought


def _mix_bwd_part(dh1, p, s, rope, nb, lp, tag, ride, next_ride):
    d = D_MODEL
    t = dh1.shape[0]
    wg_ = lambda n, x, ys, fn, shape: wgrad(x, ys, fn, shape, tm=_wgrad_tile(t), name=f"wgrad_{n}_{tag}")
    (dga, dgb, dpa, dpb, do, dpool, dps, dpw), first = merge_bwd(dh1, s["z"], s["pa"], s["pb"], s["pooled"], p["pw"], p["ps"],
                                                                   p["wpa"], p["wpb"], p["wo"], tm=_tile(t, TILE_ROWS),
                                                                   name=f"merge_bwd_{tag}", ride=ride)
    c_o = wg_("o", s["mg"], [dh1], _chunks_rows, (d // N_DEV, d))
    c_pa = wg_("pa", s["a"], [dpa], _chunks_cols, (POOL_WIDTH, d // N_DEV))
    c_pb = wg_("pb", s["o"], [dpb], _chunks_rows, (d // N_DEV, d))
    (dq, dk, dv), brought = attn_bwd(s["q"], s["k"], s["v"], s["o"], do, s["lse"], nb=nb, lp=lp, hb=HEADS_BWD,
                                     name=f"attn_bwd_{tag}", ride=next_ride(first))
    dh, hn, dz, cqn, ckvn, dqb, dkb, dvb, dg_mix, dgq, dgkv = in_proj_bwd(
        dh1, s["h"], p["g_mix"], s["z"], dq, dk, dv, dga, dgb, dpool, p["win"], p["gq"], p["gkv"], p["wuq"], p["wuk"], p["wuv"],
        rope, tm=_tile(t, TILE_ROWS_BWD), lp=lp, nb=nb, name=f"in_proj_bwd_{tag}")
    c_in = wg_("in", hn, [dz], _chunks_w_in, (d, D_IN // N_DEV))
    c_uq = wg_("uq", cqn, [dqb], _chunks_w_uq, (Q_RANK, N_HEADS * HEAD_QK // N_DEV))
    c_ukv = wg_("ukv", ckvn, [dkb, dvb], _chunks_w_ukv, (KV_RANK, N_HEADS * (QK_NOPE + V_DIM) // N_DEV))
    small = dict(pool_scale=dps[0], pool_w=dpw, norm_mix_g=dg_mix[0], q_norm_g=dgq[0], kv_norm_g=dgkv[0])
    return dh, [c_in, c_uq, c_ukv, c_pa, c_pb, c_o], small, brought


def train_step(x, loss_target, small, comm):
    nb, seq, d = x.shape
    lp = -(-(N_META + seq) // LANES) * LANES
    t = nb * lp
    assert nb <= 2 and DEPTH == 2
    tm = _tile(t, TILE_ROWS)
    rope = _rope_table(lp, nb)
    gathered, meta = comm.first_weights()
    pad = jnp.zeros((nb, lp - N_META - seq, d), F32)
    h = jnp.concatenate([jnp.broadcast_to(meta[None], (nb, N_META, d)), x, pad], axis=1).reshape(t, d)
    target = jnp.concatenate([jnp.zeros((nb, N_META, d), F32), loss_target, pad], axis=1).reshape(t, d)

    params, saved, full = [], [], {}
    for l in range(DEPTH):
        p = _small_operands(small, l)
        p.update(_operands(gathered if l == 0 else comm.carried(0, full[0], MIX, 1), MIX, l))
        z, q, k, v = in_proj_fwd(h, p["g_mix"], p["win"], p["gq"], p["gkv"], p["wuq"], p["wuk"], p["wuv"], rope, tm=tm,
                                 name=f"in_proj_fwd_{l}")
        (o, lse), landed = attn_fwd(q, k, v, nb=nb, lp=lp, hb=HEADS_FWD, name=f"attn_fwd_{l}", ride=comm.first_hop(l))
        (h1, pooled, a, pa, pb, mg), full[l] = merge_fwd(h, z, o, p["pw"], p["ps"], p["wpa"], p["wpb"], p["wo"], tm=tm, lp=lp,
                                                          nb=nb, name=f"merge_fwd_{l}", ride=comm.second_hop(l, landed))
        p.update(_operands(comm.carried(l, full[l], FFN, l), FFN, l))
        h2, gt, up = ffn_fwd(h1, p["g_ffn"], p["wgt"], p["wut"], p["wd"], tm=tm, name=f"ffn_fwd_{l}")
        params.append(p)
        saved.append(dict(h=h, z=z, q=q, k=k, v=v, o=o, lse=lse, h1=h1, pooled=pooled, a=a, pa=pa, pb=pb, mg=mg, gt=gt, up=up))
        h = h2
    parts, dh, dgf = loss_head(h, small["final_norm_g"][None], target, tm=tm, lp=lp, nb=nb, seq=seq, name="loss_head")
    loss = jnp.sum(parts[::8, 0])

    sums = {}
    dh, c_ffn1, small1, _ = _ffn_bwd_part(dh, params[1], saved[1], 1, None)
    dh, c_mix1, sm, brought = _mix_bwd_part(
        dh, params[1], saved[1], rope, nb, lp, 1, comm.pair_exchange(c_ffn1),
        lambda theirs: comm.scatter(comm.pair_add(c_ffn1, theirs, FFN, "ffn_1")))
    small1.update(sm)
    sums.update({(n, 1): a for n, a in zip(FFN, brought)})
    dh, c_ffn0, small0, theirs = _ffn_bwd_part(dh, params[0], saved[0], 0, comm.pair_exchange(c_mix1))
    s_mix1 = comm.pair_add(c_mix1, theirs, MIX, "mix_1")
    dh, c_mix0, sm, brought = _mix_bwd_part(
        dh, params[0], saved[0], rope, nb, lp, 0, comm.pair_exchange(c_ffn0),
        lambda theirs: comm.scatter(s_mix1 + comm.pair_add(c_ffn0, theirs, FFN, "ffn_0")))
    small0.update(sm)
    sums.update({(n, l): a for (n, l), a in zip([(n, 1) for n in MIX] + [(n, 0) for n in FFN], brought)})
    dh = dh.reshape(nb, lp, d)
    dmeta = jnp.sum(dh[:, :N_META], axis=0)
    meta_chunks = jnp.transpose(dmeta.reshape(N_META, N_CHIPS, 2, d // N_DEV), (2, 1, 0, 3)).astype(BF16)
    small_grads = {n: jnp.stack([small0[n], small1[n]]) for n in small0}
    small_grads["final_norm_g"] = dgf[0]
    last_names = MIX + ("meta_tokens",)
    theirs, small_parts = comm.last_pair_exchange(c_mix0 + [meta_chunks], _pack([small_grads[n] for n in SMALL], 8))
    last = comm.scatter_now(comm.pair_add(c_mix0 + [meta_chunks], theirs, last_names, "mix_0"), "scatter_mix_0")
    sums.update({(n, 0): a for n, a in zip(last_names, last)})
    return loss, dh[:, N_META:N_META + seq], sums, small_grads, small_parts


def kernel(x, meta_tokens, norm_mix_g, w_in, pool_w, pool_scale, q_norm_g, kv_norm_g, w_uq, w_ukv, w_pa, w_pb, w_o, norm_ffn_g, w_gate, w_up, w_down, final_norm_g, loss_target, m_meta_tokens, m_norm_mix_g, m_w_in, m_pool_w, m_pool_scale, m_q_norm_g, m_kv_norm_g, m_w_uq, m_w_ukv, m_w_pa, m_w_pb, m_w_o, m_norm_ffn_g, m_w_gate, m_w_up, m_w_down, m_final_norm_g, v_meta_tokens, v_norm_mix_g, v_w_in, v_pool_w, v_pool_scale, v_q_norm_g, v_kv_norm_g, v_w_uq, v_w_ukv, v_w_pa, v_w_pb, v_w_o, v_norm_ffn_g, v_w_gate, v_w_up, v_w_down, v_final_norm_g):
    args = dict(locals())
    w = {n: args[n] for n in WEIGHTS}
    m = {n: args["m_" + n] for n in WEIGHTS}
    v = {n: args["v_" + n] for n in WEIGHTS}
    small = {n: w[n] for n in SMALL}
    as_handled = lambda a, n: jnp.swapaxes(a, 1, 2) if n in TRANSPOSED else a
    wh, mh, vh = ({n: as_handled(d[n], n) for n, _ in BIG} for d in (w, m, v))

    loss, grad_x, sums, _, small_recv = train_step(x, loss_target, small, MeshComm(wh, meta_tokens))
    loss = lax.psum(loss, ("x", "y", "c"))

    out = {n: [as_handled(a, n) for a in adamw(wh[n], mh[n], vh[n], [sums[(n, l)] for l in range(DEPTH)], name=f"adamw_{n}")]
           for n, _ in BIG}
    out["meta_tokens"] = [a[0] for a in adamw(meta_tokens[None], m["meta_tokens"][None], v["meta_tokens"][None],
                                              [sums[("meta_tokens", 0)]], name="adamw_meta_tokens")]
    pk = lambda d: _pack([d[n] for n in SMALL], 8)[None]
    packed = adamw(pk(w), pk(m), pk(v), [small_recv], name="adamw_small")
    shapes = [w[n].shape for n in SMALL]
    for n, *kinds in zip(SMALL, *[_unpack(packed[kind][0], shapes) for kind in range(4)]):
        out[n] = kinds
    return (loss, grad_x, *[out[n][kind] for kind in range(4) for n in WEIGHTS])
```

```python
import functools
import math

import jax
import jax.numpy as jnp
from jax import lax
from jax.experimental import pallas as pl
from jax.experimental.pallas import tpu as pltpu

F32, BF16 = jnp.float32, jnp.bfloat16

D_MODEL = 1024
N_META = 16
N_HEADS = 16
QK_NOPE, QK_ROPE, V_DIM = 64, 32, 64
HALF_ROPE = QK_ROPE // 2
Q_RANK, KV_RANK = 256, 128
POOL_WINDOWS = (2, 4, 8, 16)
POOL_GROUP = 128
POOL_WIDTH = POOL_GROUP * len(POOL_WINDOWS)
POOL_HALO = 16
D_FF = 2816
D_IN = 2976
NORM_EPS = 1e-6
SM_SCALE = (QK_NOPE + QK_ROPE) ** -0.5
LOG2E = math.log2(math.e)
EXP2_SCALE = SM_SCALE * LOG2E
MASK_VALUE = -1e30
ROPE_THETA = 10000.0
DEPTH = 2
N_DEV = 8

ADAM_LR, ADAM_B1, ADAM_B2, ADAM_EPS, ADAM_WD, ADAM_STEP = 0.001, 0.9, 0.999, 1e-08, 0.01, 10

LANES = 128
HEAD_SLOT = LANES
QK_WIDTH = N_HEADS * HEAD_SLOT
Z_CQ, Z_CKV, Z_KR, Z_GA, Z_GB, DZ = 512, 768, 896, 1024, 2048, 3072
TQ = TK = 256
VMEM_LIMIT = 56 * 1024 * 1024


def _cparams():
    return pltpu.CompilerParams(vmem_limit_bytes=VMEM_LIMIT)


def _rows(tm, width, col=0):
    return pl.BlockSpec((tm, width), lambda i: (i, col))


def _whole(shape):
    zeros = (0,) * len(shape)
    return pl.BlockSpec(shape, lambda i: zeros, pipeline_mode=pl.Buffered(1))


def _acc(shape):
    zeros = (0,) * len(shape)
    return pl.BlockSpec(shape, lambda i: zeros)


def _dot(a, b):
    return jnp.dot(a, b, preferred_element_type=F32)


def _dot_tn(a, b):
    return lax.dot_general(a, b, (((0,), (0,)), ((), ())), preferred_element_type=F32)


def _dot_nt(a, b):
    return lax.dot_general(a, b, (((1,), (1,)), ((), ())), preferred_element_type=F32)


def _rms(x):
    r = lax.rsqrt(jnp.mean(x * x, axis=-1, keepdims=True) + NORM_EPS)
    return x * r, r


def _rms_bwd(dy, xhat, r, g):
    dg = jnp.sum(dy * xhat, axis=0, keepdims=True)
    dxh = dy * g
    dx = r * (dxh - xhat * jnp.mean(dxh * xhat, axis=-1, keepdims=True))
    return dx, dg


def _sigmoid(x):
    return 1.0 / (1.0 + jnp.exp(-x))


def _rope_fwd(q, c, s1, s2):
    w = q.shape[1]
    return q * c + pltpu.roll(q, w - HALF_ROPE, 1) * s1 + pltpu.roll(q, HALF_ROPE, 1) * s2


def _rope_bwd(dq, c, s1, s2):
    w = dq.shape[1]
    return dq * c + pltpu.roll(dq * s1, HALF_ROPE, 1) + pltpu.roll(dq * s2, w - HALF_ROPE, 1)


def _rope_tables(rope, reps):
    c, cr, s1, s2 = (rope[:, k * LANES:(k + 1) * LANES] for k in range(4))
    if reps > 1:
        return jnp.tile(c, (1, reps)), jnp.tile(s1, (1, reps)), jnp.tile(s2, (1, reps))
    return cr, s1, s2


def _seq_pos(gi, lp, nb):
    pos = gi
    for b in range(1, nb):
        pos = jnp.where(gi >= b * lp, gi - b * lp, pos)
    return pos


_ANY = pl.BlockSpec(memory_space=pl.ANY)


def _carrying_call(body, ride, operands, *, name, grid, in_specs, out_specs, out_shape, scratch_shapes=()):
    n_in, n_out = len(in_specs), len(out_specs)
    if ride is None:
        out = pl.pallas_call(body, name=name, grid=grid, in_specs=in_specs, out_specs=out_specs, out_shape=out_shape,
                             scratch_shapes=list(scratch_shapes), compiler_params=_cparams())(*operands)
        return out, []
    ne = len(ride.arrays)

    def carrying(*refs):
        ins, r_in, rest = refs[:n_in], refs[n_in:n_in + ne], refs[n_in + ne:]
        outs, r_out, rest = rest[:n_out], rest[n_out:n_out + ne], rest[n_out + ne:]
        scratch, sems = rest[:len(scratch_shapes)], rest[len(scratch_shapes):]
        ids = [pl.program_id(a) for a in range(len(grid))]
        first = functools.reduce(jnp.logical_and, [i == 0 for i in ids])
        last = functools.reduce(jnp.logical_and, [i == g - 1 for i, g in zip(ids, grid)])

        @pl.when(first)
        def _():
            ride.start(r_in, r_out, sems)

        body(*ins, *outs, *scratch)

        @pl.when(last)
        def _():
            ride.wait(r_in, r_out, sems)

    out = pl.pallas_call(
        carrying, name=name, grid=grid, in_specs=list(in_specs) + [_ANY] * ne, out_specs=list(out_specs) + [_ANY] * ne,
        out_shape=list(out_shape) + ride.out_shapes, scratch_shapes=list(scratch_shapes) + ride.scratch,
        input_output_aliases=ride.aliases(n_in, n_out), compiler_params=_cparams(),
    )(*operands, *ride.arrays)
    return out[:n_out], out[n_out:]


def padded_rows(head, tokens, lp, name):
    nb, seq, d = tokens.shape
    pad = lp - N_META - seq

    def body(head_ref, tokens_ref, o_ref, zeros, sems):
        zeros[...] = jnp.zeros(zeros.shape, F32)
        copies = []
        for b in range(nb):
            copies += [pltpu.make_async_copy(head_ref, o_ref.at[pl.ds(b * lp, N_META)], sems.at[3 * b]),
                       pltpu.make_async_copy(tokens_ref.at[b], o_ref.at[pl.ds(b * lp + N_META, seq)], sems.at[3 * b + 1])]
            if pad:
                copies.append(pltpu.make_async_copy(zeros.at[pl.ds(0, pad)], o_ref.at[pl.ds(b * lp + N_META + seq, pad)],
                                                    sems.at[3 * b + 2]))
        for cp in copies:
            cp.start()
        for cp in copies:
            cp.wait()

    return pl.pallas_call(
        body, name=name, in_specs=[pl.BlockSpec(memory_space=pltpu.VMEM), _ANY], out_specs=_ANY,
        out_shape=jax.ShapeDtypeStruct((nb * lp, d), F32),
        scratch_shapes=[pltpu.VMEM((max(pad, 8), d), F32), pltpu.SemaphoreType.DMA((3 * nb,))],
    )(head, tokens)


def token_rows(padded, seq, name):
    nb, lp, d = padded.shape

    def body(p_ref, o_ref, sems):
        copies = [pltpu.make_async_copy(p_ref.at[b, pl.ds(N_META, seq)], o_ref.at[b], sems.at[b]) for b in range(nb)]
        for cp in copies:
            cp.start()
        for cp in copies:
            cp.wait()

    return pl.pallas_call(body, name=name, in_specs=[_ANY], out_specs=_ANY, out_shape=jax.ShapeDtypeStruct((nb, seq, d), F32),
                          scratch_shapes=[pltpu.SemaphoreType.DMA((nb,))])(padded)


def in_proj_fwd(h, g_mix, win, gq, gkv, wuq, wuk, wuv, rope, *, tm, name):
    t = h.shape[0]

    def body(h_ref, g_ref, win_ref, gq_ref, gkv_ref, wuq_ref, wuk_ref, wuv_ref, rope_ref, z_ref, q_ref, k_ref, v_ref):
        xhat, _ = _rms(h_ref[...])
        hn = (xhat * g_ref[...]).astype(BF16)
        z = _dot(hn, win_ref[...])
        z_ref[...] = z
        rope_t = rope_ref[...]
        xq, _ = _rms(z[:, Z_CQ:Z_CKV])
        cqn = (xq * gq_ref[...]).astype(BF16)
        q = _rope_fwd(_dot(cqn, wuq_ref[...]), *_rope_tables(rope_t, N_HEADS))
        q_ref[...] = q.astype(BF16)
        xkv, _ = _rms(z[:, Z_CKV:Z_KR])
        ckvn = (xkv * gkv_ref[...]).astype(BF16)
        kr = _rope_fwd(z[:, Z_KR:Z_GA], *_rope_tables(rope_t, 1))
        k_ref[...] = (_dot(ckvn, wuk_ref[...]) + jnp.tile(kr, (1, N_HEADS))).astype(BF16)
        v_ref[...] = _dot(ckvn, wuv_ref[...]).astype(BF16)

    return pl.pallas_call(
        body, name=name, grid=(t // tm,),
        in_specs=[_rows(tm, D_MODEL), _whole((1, D_MODEL)), _whole((D_MODEL, DZ)), _whole((1, Q_RANK)), _whole((1, KV_RANK)),
                  _whole((Q_RANK, QK_WIDTH)), _whole((KV_RANK, QK_WIDTH)), _whole((KV_RANK, D_MODEL)), _rows(tm, 4 * LANES)],
        out_specs=[_rows(tm, DZ), _rows(tm, QK_WIDTH), _rows(tm, QK_WIDTH), _rows(tm, D_MODEL)],
        out_shape=[jax.ShapeDtypeStruct((t, DZ), F32), jax.ShapeDtypeStruct((t, QK_WIDTH), BF16),
                   jax.ShapeDtypeStruct((t, QK_WIDTH), BF16), jax.ShapeDtypeStruct((t, D_MODEL), BF16)],
        compiler_params=_cparams(),
    )(h, g_mix, win, gq, gkv, wuq, wuk, wuv, rope)


def attn_fwd(q, k, v, *, nb, lp, hb, name, ride=None):
    t = q.shape[0]
    nq, tail = lp // TQ, lp % TQ
    assert tail % LANES == 0

    def body(q_ref, k_ref, v_ref, o_ref, lse_ref, vt):
        for pr in range(hb // 2):
            vt[pr] = v_ref[:, pr * LANES:(pr + 1) * LANES].T

        def q_block(qs, tq, whole_k):
            qh = [q_ref[pl.ds(qs, tq), hd * HEAD_SLOT:(hd + 1) * HEAD_SLOT] for hd in range(hb)]
            keep = lax.broadcasted_iota(jnp.int32, (tq, tq), 0) <= lax.broadcasted_iota(jnp.int32, (tq, tq), 1)

            def k_steps(blocks, c, masked):
                sts = [[_dot_nt(k_ref[pl.ds(ks, tk), hd * HEAD_SLOT:(hd + 1) * HEAD_SLOT], qh[hd]) for hd in range(hb)]
                       for ks, tk in blocks]
                for (ks, tk), st_b in zip(blocks, sts):
                    ps, stats = [], []
                    for hd in range(hb):
                        m, l, _ = c[hd]
                        st = jnp.where(keep, st_b[hd], MASK_VALUE) if masked else st_b[hd]
                        m_new = jnp.maximum(m, jnp.max(st, axis=0, keepdims=True))
                        p = jnp.exp2((st - m_new) * EXP2_SCALE)
                        alpha = jnp.exp2((m - m_new) * EXP2_SCALE)
                        ps.append(p.astype(BF16))
                        stats.append((m_new, alpha * l + jnp.sum(p, axis=0, keepdims=True), alpha))
                    pvs = [_dot(vt[hd // 2, :, pl.ds(ks, tk)], ps[hd]) for hd in range(hb)]
                    c = tuple((stats[hd][0], stats[hd][1], stats[hd][2] * c[hd][2] + pvs[hd]) for hd in range(hb))
                return c

            def two_blocks(i, c):
                ks = pl.multiple_of(2 * i * TK, TK)
                return k_steps([(ks, TK), (ks + TK, TK)], c, False)

            init = tuple((jnp.full((1, tq), MASK_VALUE, F32), jnp.zeros((1, tq), F32), jnp.zeros((LANES, tq), F32))
                         for _ in range(hb))
            pairs = lax.div(whole_k, 2)
            c = lax.fori_loop(0, pairs, two_blocks, init)
            c = lax.fori_loop(2 * pairs, whole_k, lambda kj, c: k_steps([(pl.multiple_of(kj * TK, TK), TK)], c, False), c)
            c = k_steps([(qs, tq)], c, True)
            sub = lax.broadcasted_iota(jnp.int32, (LANES, tq), 0)
            for pr in range(hb // 2):
                (m0, l0, a0), (m1, l1, a1) = c[2 * pr], c[2 * pr + 1]
                o_ref[pl.ds(qs, tq), pr * LANES:(pr + 1) * LANES] = jnp.where(sub < V_DIM, a0 / l0, a1 / l1).T.astype(BF16)
                lse_ref[2 * pr, :, pl.ds(qs, tq)] = m0 * SM_SCALE + jnp.log(l0)
                lse_ref[2 * pr + 1, :, pl.ds(qs, tq)] = m1 * SM_SCALE + jnp.log(l1)

        def whole_q_block(qi, carry):
            q_block(pl.multiple_of(qi * TQ, TQ), TQ, qi)
            return carry

        lax.fori_loop(0, nq, whole_q_block, 0)
        if tail:
            q_block(nq * TQ, tail, nq)

    blk = lambda w: pl.BlockSpec((lp, w), lambda b, g: (b, g))
    return _carrying_call(
        body, ride, (q, k, v), name=name, grid=(nb, N_HEADS // hb),
        in_specs=[blk(hb * HEAD_SLOT), blk(hb * HEAD_SLOT), blk(hb * V_DIM)],
        out_specs=[blk(hb * V_DIM), pl.BlockSpec((hb, 1, lp), lambda b, g: (g, 0, b))],
        out_shape=[jax.ShapeDtypeStruct((t, D_MODEL), BF16), jax.ShapeDtypeStruct((N_HEADS, 1, t), F32)],
        scratch_shapes=[pltpu.VMEM((hb // 2, LANES, lp), BF16)])


def _pool_band_fwd(i, tm, lp, nb):
    r = lax.broadcasted_iota(jnp.int32, (tm, POOL_HALO + tm), 0)
    e = lax.broadcasted_iota(jnp.int32, (tm, POOL_HALO + tm), 1)
    diff = r + POOL_HALO - e
    pos = _seq_pos(i * tm + lax.broadcasted_iota(jnp.int32, (tm, 1), 0), lp, nb)
    out = []
    for w in POOL_WINDOWS:
        cnt = jnp.minimum(pos + 1, w)
        band = jnp.where((diff >= 0) & (diff < cnt), 1.0, 0.0).astype(BF16)
        out.append((band, cnt.astype(F32)))
    return out


def merge_fwd(h, z, o, pw, ps, wpa, wpb, wo, *, tm, lp, nb, name, ride=None):
    t = h.shape[0]
    hb = tm // POOL_HALO

    def body(h_ref, u_ref, uprev_ref, ga_ref, gb_ref, o_ref, pw_ref, ps_ref, wpa_ref, wpb_ref, wo_ref,
             h1_ref, pooled_ref, a_ref, pa_ref, pb_ref, mg_ref):
        i = pl.program_id(0)
        u = u_ref[...]
        uext = jnp.concatenate([uprev_ref[...], u], axis=0).astype(BF16)
        pooled, ys = [], []
        for g, (band, cnt) in enumerate(_pool_band_fwd(i, tm, lp, nb)):
            gs = slice(g * POOL_GROUP, (g + 1) * POOL_GROUP)
            pg = (_dot(band, uext[:, gs]) / cnt - u[:, gs]).astype(BF16)
            pooled.append(pg)
            ys.append(_dot(pg, pw_ref[g]))
        pooled_ref[...] = jnp.concatenate(pooled, axis=1)
        a = (jnp.concatenate(ys, axis=1) * ps_ref[...]).astype(BF16)
        a_ref[...] = a
        pa = _dot(a, wpa_ref[...])
        pb = _dot(o_ref[...], wpb_ref[...])
        pa_ref[...] = pa.astype(BF16)
        pb_ref[...] = pb.astype(BF16)
        mg = (_sigmoid(ga_ref[...]) * pa + _sigmoid(gb_ref[...]) * pb).astype(BF16)
        mg_ref[...] = mg
        h1_ref[...] = h_ref[...] + _dot(mg, wo_ref[...])

    halo = pl.BlockSpec((POOL_HALO, POOL_WIDTH), lambda i: (jnp.maximum(i * hb - 1, 0), 0))
    return _carrying_call(
        body, ride, (h, z, z, z, z, o, pw, ps, wpa, wpb, wo), name=name, grid=(t // tm,),
        in_specs=[_rows(tm, D_MODEL), _rows(tm, POOL_WIDTH), halo, _rows(tm, D_MODEL, 1), _rows(tm, D_MODEL, 2), _rows(tm, D_MODEL),
                  _whole((4, POOL_GROUP, POOL_GROUP)), _whole((1, POOL_WIDTH)), _whole((POOL_WIDTH, D_MODEL)),
                  _whole((D_MODEL, D_MODEL)), _whole((D_MODEL, D_MODEL))],
        out_specs=[_rows(tm, D_MODEL), _rows(tm, POOL_WIDTH), _rows(tm, POOL_WIDTH), _rows(tm, D_MODEL), _rows(tm, D_MODEL),
                   _rows(tm, D_MODEL)],
        out_shape=[jax.ShapeDtypeStruct((t, D_MODEL), F32), jax.ShapeDtypeStruct((t, POOL_WIDTH), BF16),
                   jax.ShapeDtypeStruct((t, POOL_WIDTH), BF16), jax.ShapeDtypeStruct((t, D_MODEL), BF16),
                   jax.ShapeDtypeStruct((t, D_MODEL), BF16), jax.ShapeDtypeStruct((t, D_MODEL), BF16)])


def ffn_fwd(h1, g, wgt, wut, wd, *, tm, name):
    t = h1.shape[0]

    def body(h_ref, g_ref, wgt_ref, wut_ref, wd_ref, h2_ref, gt_ref, up_ref):
        h = h_ref[...]
        xhat, _ = _rms(h)
        hn = (xhat * g_ref[...]).astype(BF16)
        gt = _dot_nt(hn, wgt_ref[...])
        up = _dot_nt(hn, wut_ref[...])
        gt_ref[...] = gt.astype(BF16)
        up_ref[...] = up.astype(BF16)
        act = (gt * _sigmoid(gt) * up).astype(BF16)
        h2_ref[...] = h + _dot(act, wd_ref[...])

    return pl.pallas_call(
        body, name=name, grid=(t // tm,),
        in_specs=[_rows(tm, D_MODEL), _whole((1, D_MODEL)), _whole((D_FF, D_MODEL)), _whole((D_FF, D_MODEL)), _whole((D_FF, D_MODEL))],
        out_specs=[_rows(tm, D_MODEL), _rows(tm, D_FF), _rows(tm, D_FF)],
        out_shape=[jax.ShapeDtypeStruct((t, D_MODEL), F32), jax.ShapeDtypeStruct((t, D_FF), BF16), jax.ShapeDtypeStruct((t, D_FF), BF16)],
        compiler_params=_cparams(),
    )(h1, g, wgt, wut, wd)


def loss_head(h, g, target, *, tm, lp, nb, seq, name):
    t = h.shape[0]
    nt = t // tm

    def body(h_ref, g_ref, t_ref, loss_ref, dh_ref, dg_ref):
        i = pl.program_id(0)
        pos = _seq_pos(i * tm + lax.broadcasted_iota(jnp.int32, (tm, 1), 0), lp, nb)
        real = (pos >= N_META) & (pos < N_META + seq)
        xhat, r = _rms(h_ref[...])
        gg = g_ref[...]
        err = jnp.where(real, xhat * gg - t_ref[...], 0.0)
        loss_ref[...] = jnp.full((8, LANES), 0.5 * jnp.sum(err * err) / D_MODEL, F32)
        dx, dg = _rms_bwd(err * (1.0 / D_MODEL), xhat, r, gg)
        dh_ref[...] = dx

        @pl.when(i == 0)
        def _():
            dg_ref[...] = jnp.zeros_like(dg_ref)

        dg_ref[...] += dg

    return pl.pallas_call(
        body, name=name, grid=(nt,),
        in_specs=[_rows(tm, D_MODEL), _whole((1, D_MODEL)), _rows(tm, D_MODEL)],
        out_specs=[pl.BlockSpec((8, LANES), lambda i: (i, 0)), _rows(tm, D_MODEL), _acc((1, D_MODEL))],
        out_shape=[jax.ShapeDtypeStruct((nt * 8, LANES), F32), jax.ShapeDtypeStruct((t, D_MODEL), F32),
                   jax.ShapeDtypeStruct((1, D_MODEL), F32)],
        compiler_params=_cparams(),
    )(h, g, target)


def wgrad(x, ys, chunk_fn, chunk_shape, *, tm, name):
    t, m = x.shape
    tiles = t // tm
    steps = -(-tiles // 2)

    def body(*refs):
        ins, o_ref, accs = refs[:2 * (1 + len(ys))], refs[2 * (1 + len(ys))], refs[2 * (1 + len(ys)) + 1:]
        i = pl.program_id(0)

        @pl.when(i == 0)
        def _():
            for acc in accs:
                acc[...] = jnp.zeros_like(acc)

        def both(first, second, mask):
            b = second[...].astype(BF16)
            if mask and tiles % 2:
                b = jnp.where(2 * i + 1 < tiles, b, jnp.zeros_like(b))
            return jnp.concatenate([first[...].astype(BF16), b], axis=0)

        xb = both(ins[0], ins[1], True)
        for j, acc in enumerate(accs):
            acc[...] += _dot_tn(xb, both(ins[2 + 2 * j], ins[3 + 2 * j], False))

        @pl.when(i == steps - 1)
        def _():
            for p, chunk in enumerate(chunk_fn(*accs)):
                o_ref[p % 2, p // 2] = chunk.astype(BF16)

    def two_tiles(width):
        return [pl.BlockSpec((tm, width), lambda i: (2 * i, 0)),
                pl.BlockSpec((tm, width), lambda i: (jnp.minimum(2 * i + 1, tiles - 1), 0))]

    out = (2, N_DEV // 2) + tuple(chunk_shape)
    operands = [x, x] + [a for y in ys for a in (y, y)]
    return pl.pallas_call(
        body, name=name, grid=(steps,),
        in_specs=two_tiles(m) + [s for y in ys for s in two_tiles(y.shape[1])], out_specs=_acc(out),
        out_shape=jax.ShapeDtypeStruct(out, BF16), scratch_shapes=[pltpu.VMEM((m, y.shape[1]), F32) for y in ys],
        compiler_params=_cparams(),
    )(*operands)


def ffn_bwd(dh2, h1, g, gt, up, wgt, wut, wd, *, tm, name, ride=None):
    t = h1.shape[0]

    def body(dh2_ref, h_ref, g_ref, gt_ref, up_ref, wgt_ref, wut_ref, wd_ref, dh1_ref, hn_ref, act_ref, dgt_ref, dup_ref, dg_ref):
        dh2 = dh2_ref[...]
        dact = _dot_nt(dh2.astype(BF16), wd_ref[...])
        gt = gt_ref[...].astype(F32)
        up = up_ref[...].astype(F32)
        sg = _sigmoid(gt)
        silu = gt * sg
        act_ref[...] = (silu * up).astype(BF16)
        dgt = (dact * up * (sg * (1.0 + gt * (1.0 - sg)))).astype(BF16)
        dup = (dact * silu).astype(BF16)
        dgt_ref[...] = dgt
        dup_ref[...] = dup
        dhn = _dot(dgt, wgt_ref[...]) + _dot(dup, wut_ref[...])
        xhat, r = _rms(h_ref[...])
        gg = g_ref[...]
        hn_ref[...] = (xhat * gg).astype(BF16)
        dx, dg = _rms_bwd(dhn, xhat, r, gg)
        dh1_ref[...] = dh2 + dx

        @pl.when(pl.program_id(0) == 0)
        def _():
            dg_ref[...] = jnp.zeros_like(dg_ref)

        dg_ref[...] += dg

    return _carrying_call(
        body, ride, (dh2, h1, g, gt, up, wgt, wut, wd), name=name, grid=(t // tm,),
        in_specs=[_rows(tm, D_MODEL), _rows(tm, D_MODEL), _whole((1, D_MODEL)), _rows(tm, D_FF), _rows(tm, D_FF),
                  _whole((D_FF, D_MODEL)), _whole((D_FF, D_MODEL)), _whole((D_FF, D_MODEL))],
        out_specs=[_rows(tm, D_MODEL), _rows(tm, D_MODEL), _rows(tm, D_FF), _rows(tm, D_FF), _rows(tm, D_FF), _acc((1, D_MODEL))],
        out_shape=[jax.ShapeDtypeStruct((t, D_MODEL), F32), jax.ShapeDtypeStruct((t, D_MODEL), BF16),
                   jax.ShapeDtypeStruct((t, D_FF), BF16), jax.ShapeDtypeStruct((t, D_FF), BF16),
                   jax.ShapeDtypeStruct((t, D_FF), BF16), jax.ShapeDtypeStruct((1, D_MODEL), F32)])


def merge_bwd(dh1, z, pa, pb, pooled, pw, ps, wpa, wpb, wo, *, tm, name, ride=None):
    t = dh1.shape[0]

    def body(dh1_ref, ga_ref, gb_ref, pa_ref, pb_ref, pooled_ref, pw_ref, ps_ref, wpa_ref, wpb_ref, wo_ref,
             dga_ref, dgb_ref, dpa_ref, dpb_ref, do_ref, dpool_ref, dps_ref, dpw_ref):
        dmg = _dot_nt(dh1_ref[...].astype(BF16), wo_ref[...])
        sa = _sigmoid(ga_ref[...])
        sb = _sigmoid(gb_ref[...])
        dga_ref[...] = (dmg * pa_ref[...].astype(F32) * sa * (1.0 - sa)).astype(BF16)
        dgb_ref[...] = (dmg * pb_ref[...].astype(F32) * sb * (1.0 - sb)).astype(BF16)
        dpa = (dmg * sa).astype(BF16)
        dpb = (dmg * sb).astype(BF16)
        dpa_ref[...] = dpa
        dpb_ref[...] = dpb
        do_ref[...] = _dot_nt(dpb, wpb_ref[...]).astype(BF16)
        da = _dot_nt(dpa, wpa_ref[...])
        pooled = pooled_ref[...]
        ps = ps_ref[...]

        @pl.when(pl.program_id(0) == 0)
        def _():
            dps_ref[...] = jnp.zeros_like(dps_ref)
            dpw_ref[...] = jnp.zeros_like(dpw_ref)

        dps, dpool = [], []
        for g in range(len(POOL_WINDOWS)):
            gs = slice(g * POOL_GROUP, (g + 1) * POOL_GROUP)
            y = _dot(pooled[:, gs], pw_ref[g])
            dps.append(jnp.sum(da[:, gs] * y, axis=0, keepdims=True))
            dy = (da[:, gs] * ps[:, gs]).astype(BF16)
            dpool.append(_dot_nt(dy, pw_ref[g]))
            dpw_ref[g] += _dot_tn(pooled[:, gs], dy)
        dps_ref[...] += jnp.concatenate(dps, axis=1)
        dpool_ref[...] = jnp.concatenate(dpool, axis=1)

    return _carrying_call(
        body, ride, (dh1, z, z, pa, pb, pooled, pw, ps, wpa, wpb, wo), name=name, grid=(t // tm,),
        in_specs=[_rows(tm, D_MODEL), _rows(tm, D_MODEL, 1), _rows(tm, D_MODEL, 2), _rows(tm, D_MODEL), _rows(tm, D_MODEL),
                  _rows(tm, POOL_WIDTH), _whole((4, POOL_GROUP, POOL_GROUP)),
                  _whole((1, POOL_WIDTH)), _whole((POOL_WIDTH, D_MODEL)), _whole((D_MODEL, D_MODEL)), _whole((D_MODEL, D_MODEL))],
        out_specs=[_rows(tm, D_MODEL), _rows(tm, D_MODEL), _rows(tm, D_MODEL), _rows(tm, D_MODEL), _rows(tm, D_MODEL),
                   _rows(tm, POOL_WIDTH), _acc((1, POOL_WIDTH)), _acc((4, POOL_GROUP, POOL_GROUP))],
        out_shape=[jax.ShapeDtypeStruct((t, D_MODEL), BF16)] * 5
        + [jax.ShapeDtypeStruct((t, POOL_WIDTH), F32), jax.ShapeDtypeStruct((1, POOL_WIDTH), F32),
           jax.ShapeDtypeStruct((4, POOL_GROUP, POOL_GROUP), F32)])


def attn_bwd(q, k, v, o, do, lse, *, nb, lp, hb, name, ride=None):
    t = q.shape[0]
    nq, tail = lp // TQ, lp % TQ
    assert tail % LANES == 0

    def body(q_ref, k_ref, v_ref, o_ref, do_ref, lse_ref, dq_ref, dk_ref, dv_ref, kt, doh, lse_row, delta_row, dqt):
        lane = lax.broadcasted_iota(jnp.int32, (lp, LANES), 1)
        first = lane < V_DIM
        sub = lax.broadcasted_iota(jnp.int32, (LANES, lp), 0)
        for pr in range(hb // 2):
            ls = slice(pr * LANES, (pr + 1) * LANES)
            do = do_ref[:, ls]
            doh[2 * pr] = jnp.where(first, do, jnp.zeros_like(do))
            doh[2 * pr + 1] = jnp.where(first, jnp.zeros_like(do), do)
            prod_t = (do.astype(F32) * o_ref[:, ls].astype(F32)).T
            delta_row[2 * pr] = jnp.sum(jnp.where(sub < V_DIM, prod_t, 0.0), axis=0, keepdims=True)
            delta_row[2 * pr + 1] = jnp.sum(jnp.where(sub < V_DIM, 0.0, prod_t), axis=0, keepdims=True)
        for hd in range(hb):
            lse_row[hd] = lse_ref[hd] * LOG2E
            kt[hd] = k_ref[:, hd * HEAD_SLOT:(hd + 1) * HEAD_SLOT].T
        dqt[...] = jnp.zeros(dqt.shape, F32)
        heads = range(hb)
        hss = [slice(hd * HEAD_SLOT, (hd + 1) * HEAD_SLOT) for hd in heads]

        def k_block(ks, tk, next_q):
            keep = lax.broadcasted_iota(jnp.int32, (tk, tk), 0) <= lax.broadcasted_iota(jnp.int32, (tk, tk), 1)

            def q_steps(blocks, c, masked):
                work = [(qs, tq, hd) for qs, tq in blocks for hd in heads]
                qhs = [q_ref[pl.ds(qs, tq), hss[hd]] for qs, tq, hd in work]
                dos = [doh[hd, pl.ds(qs, tq), :] for qs, tq, hd in work]
                sts = [_dot_nt(k_ref[pl.ds(ks, tk), hss[hd]], qhs[i]) for i, (_, _, hd) in enumerate(work)]
                dpts = [_dot_nt(v_ref[pl.ds(ks, tk), (hd // 2) * LANES:(hd // 2 + 1) * LANES], dos[i])
                        for i, (_, _, hd) in enumerate(work)]
                pts, dsts = [], []
                for i, (qs, tq, hd) in enumerate(work):
                    st = jnp.where(keep, sts[i], MASK_VALUE) if masked else sts[i]
                    pt = jnp.exp2(st * EXP2_SCALE - lse_row[hd, :, pl.ds(qs, tq)])
                    dsts.append((pt * (dpts[i] - delta_row[hd, :, pl.ds(qs, tq)])).astype(BF16))
                    pts.append(pt.astype(BF16))
                dvs = [_dot(pts[i], dos[i]) for i in range(len(work))]
                dks = [_dot(dsts[i], qhs[i]) for i in range(len(work))]
                dqs = [_dot(kt[hd, :, pl.ds(ks, tk)], dsts[i]) for i, (_, _, hd) in enumerate(work)]
                c = list(c)
                for i, (qs, tq, hd) in enumerate(work):
                    dqt[hd, :, pl.ds(qs, tq)] += dqs[i]
                    c[hd] = (c[hd][0] + dks[i], c[hd][1] + dvs[i])
                return tuple(c)

            zero = jnp.zeros((tk, LANES), F32)
            c = q_steps([(ks, tk)], tuple((zero, zero) for _ in heads), True)
            if next_q is not None:
                def two_blocks(i, c):
                    qs = pl.multiple_of((next_q + 2 * i) * TQ, TQ)
                    return q_steps([(qs, TQ), (qs + TQ, TQ)], c, False)

                pairs = lax.div(nq - next_q, 2)
                c = lax.fori_loop(0, pairs, two_blocks, c)
                c = lax.fori_loop(next_q + 2 * pairs, nq, lambda qi, c: q_steps([(pl.multiple_of(qi * TQ, TQ), TQ)], c, False), c)
                if tail:
                    c = q_steps([(nq * TQ, tail)], c, False)
            for hd in heads:
                dk_ref[pl.ds(ks, tk), hss[hd]] = c[hd][0] * SM_SCALE
            for pr in range(hb // 2):
                dv_ref[pl.ds(ks, tk), pr * LANES:(pr + 1) * LANES] = c[2 * pr][1] + c[2 * pr + 1][1]

        def whole_k_block(kj, carry):
            k_block(pl.multiple_of(kj * TK, TK), TK, kj + 1)
            return carry

        lax.fori_loop(0, nq, whole_k_block, 0)
        if tail:
            k_block(nq * TQ, tail, None)
        for hd in range(hb):
            dq_ref[:, hd * HEAD_SLOT:(hd + 1) * HEAD_SLOT] = dqt[hd].T * SM_SCALE

    blk = lambda w: pl.BlockSpec((lp, w), lambda b, g: (b, g))
    return _carrying_call(
        body, ride, (q, k, v, o, do, lse), name=name, grid=(nb, N_HEADS // hb),
        in_specs=[blk(hb * HEAD_SLOT), blk(hb * HEAD_SLOT), blk(hb * V_DIM), blk(hb * V_DIM), blk(hb * V_DIM),
                  pl.BlockSpec((hb, 1, lp), lambda b, g: (g, 0, b))],
        out_specs=[blk(hb * HEAD_SLOT), blk(hb * HEAD_SLOT), blk(hb * V_DIM)],
        out_shape=[jax.ShapeDtypeStruct((t, QK_WIDTH), F32), jax.ShapeDtypeStruct((t, QK_WIDTH), F32),
                   jax.ShapeDtypeStruct((t, D_MODEL), F32)],
        scratch_shapes=[pltpu.VMEM((hb, HEAD_SLOT, lp), BF16), pltpu.VMEM((hb, lp, LANES), BF16), pltpu.VMEM((hb, 1, lp), F32),
                        pltpu.VMEM((hb, 1, lp), F32), pltpu.VMEM((hb, HEAD_SLOT, lp), F32)])


def in_proj_bwd(dh1, h, g_mix, z, dq, dk, dv, dga, dgb, dpool, win, gq, gkv, wuq, wuk, wuv, rope, *, tm, lp, nb, name):
    t = h.shape[0]
    hb = tm // POOL_HALO
    last_halo = t // POOL_HALO - 1

    def body(dh1_ref, h_ref, g_ref, zcq_ref, zckv_ref, dq_ref, dk_ref, dv_ref, dga_ref, dgb_ref, dpool_ref, dnext_ref,
             win_ref, gq_ref, gkv_ref, wuq_ref, wuk_ref, wuv_ref, rope_ref,
             dh_ref, hn_ref, dz_ref, cqn_ref, ckvn_ref, dqb_ref, dkb_ref, dvb_ref, dg_ref, dgq_ref, dgkv_ref):
        i = pl.program_id(0)
        rope_t = rope_ref[...]
        dqb = _rope_bwd(dq_ref[...], *_rope_tables(rope_t, N_HEADS)).astype(BF16)
        dqb_ref[...] = dqb
        xq, rq = _rms(zcq_ref[...])
        gq_v = gq_ref[...]
        cqn_ref[...] = (xq * gq_v).astype(BF16)
        dcq, dgq = _rms_bwd(_dot_nt(dqb, wuq_ref[...]), xq, rq, gq_v)
        dk = dk_ref[...]
        dkb = dk.astype(BF16)
        dvb = dv_ref[...].astype(BF16)
        dkb_ref[...] = dkb
        dvb_ref[...] = dvb
        xkv, rkv = _rms(zckv_ref[...])
        gkv_v = gkv_ref[...]
        ckvn_ref[...] = (xkv * gkv_v).astype(BF16)
        dckv, dgkv = _rms_bwd(_dot_nt(dkb, wuk_ref[...]) + _dot_nt(dvb, wuv_ref[...]), xkv, rkv, gkv_v)
        dks = dk[:, :HEAD_SLOT]
        for hd in range(1, N_HEADS):
            dks = dks + dk[:, hd * HEAD_SLOT:(hd + 1) * HEAD_SLOT]
        dzk = _rope_bwd(dks, *_rope_tables(rope_t, 1))
        dp_cur = dpool_ref[...]
        dp_ext = jnp.concatenate([dp_cur, dnext_ref[...]], axis=0)
        r = lax.broadcasted_iota(jnp.int32, (tm, tm + POOL_HALO), 0)
        e = lax.broadcasted_iota(jnp.int32, (tm, tm + POOL_HALO), 1)
        gt_col = i * tm + lax.broadcasted_iota(jnp.int32, (1, tm + POOL_HALO), 1)
        pos_col = _seq_pos(gt_col, lp, nb)
        gt_row = i * tm + lax.broadcasted_iota(jnp.int32, (tm + POOL_HALO, 1), 0)
        pos_row = _seq_pos(gt_row, lp, nb)
        dus = []
        for g, w in enumerate(POOL_WINDOWS):
            gs = slice(g * POOL_GROUP, (g + 1) * POOL_GROUP)
            band = jnp.where((e - r >= 0) & (e - r < jnp.minimum(pos_col + 1, w)) & (gt_col < t), 1.0, 0.0).astype(BF16)
            scaled = jnp.where(gt_row < t, dp_ext[:, gs] / jnp.minimum(pos_row + 1, w).astype(F32), 0.0).astype(BF16)
            dus.append(_dot(band, scaled) - dp_cur[:, gs])
        dz = jnp.concatenate(dus + [dcq, dckv, dzk], axis=1).astype(BF16)
        dz = jnp.concatenate([dz, dga_ref[...], dgb_ref[...]], axis=1)
        dz_ref[...] = dz
        xhat, rr = _rms(h_ref[...])
        gg = g_ref[...]
        hn_ref[...] = (xhat * gg).astype(BF16)
        dx, dg = _rms_bwd(_dot_nt(dz, win_ref[...]), xhat, rr, gg)
        dh_ref[...] = dh1_ref[...] + dx

        @pl.when(i == 0)
        def _():
            dg_ref[...] = jnp.zeros_like(dg_ref)
            dgq_ref[...] = jnp.zeros_like(dgq_ref)
            dgkv_ref[...] = jnp.zeros_like(dgkv_ref)

        dg_ref[...] += dg
        dgq_ref[...] += dgq
        dgkv_ref[...] += dgkv

    nxt = pl.BlockSpec((POOL_HALO, POOL_WIDTH), lambda i: (jnp.minimum((i + 1) * hb, last_halo), 0))
    return pl.pallas_call(
        body, name=name, grid=(t // tm,),
        in_specs=[_rows(tm, D_MODEL), _rows(tm, D_MODEL), _whole((1, D_MODEL)), _rows(tm, Q_RANK, Z_CQ // Q_RANK),
                  _rows(tm, KV_RANK, Z_CKV // KV_RANK), _rows(tm, QK_WIDTH), _rows(tm, QK_WIDTH), _rows(tm, D_MODEL),
                  _rows(tm, D_MODEL), _rows(tm, D_MODEL), _rows(tm, POOL_WIDTH), nxt,
                  _whole((D_MODEL, DZ)), _whole((1, Q_RANK)), _whole((1, KV_RANK)), _whole((Q_RANK, QK_WIDTH)),
                  _whole((KV_RANK, QK_WIDTH)), _whole((KV_RANK, D_MODEL)), _rows(tm, 4 * LANES)],
        out_specs=[_rows(tm, D_MODEL), _rows(tm, D_MODEL), _rows(tm, DZ), _rows(tm, Q_RANK), _rows(tm, KV_RANK),
                   _rows(tm, QK_WIDTH), _rows(tm, QK_WIDTH), _rows(tm, D_MODEL),
                   _acc((1, D_MODEL)), _acc((1, Q_RANK)), _acc((1, KV_RANK))],
        out_shape=[jax.ShapeDtypeStruct((t, D_MODEL), F32), jax.ShapeDtypeStruct((t, D_MODEL), BF16),
                   jax.ShapeDtypeStruct((t, DZ), BF16), jax.ShapeDtypeStruct((t, Q_RANK), BF16),
                   jax.ShapeDtypeStruct((t, KV_RANK), BF16), jax.ShapeDtypeStruct((t, QK_WIDTH), BF16),
                   jax.ShapeDtypeStruct((t, QK_WIDTH), BF16), jax.ShapeDtypeStruct((t, D_MODEL), BF16),
                   jax.ShapeDtypeStruct((1, D_MODEL), F32), jax.ShapeDtypeStruct((1, Q_RANK), F32),
                   jax.ShapeDtypeStruct((1, KV_RANK), F32)],
        compiler_params=_cparams(),
    )(dh1, h, g_mix, z, z, dq, dk, dv, dga, dgb, dpool, dpool, win, gq, gkv, wuq, wuk, wuv, rope)


_MESH = pl.DeviceIdType.MESH


def _place():
    x, y, c = lax.axis_index("x"), lax.axis_index("y"), lax.axis_index("c")
    return x, y, c, 4 * x + 2 * y + c


def _peer(x, y, c, k):
    px, py, pc = (1 - x) if k & 4 else x, (1 - y) if k & 2 else y, (1 - c) if k & 1 else c
    return (px, py, pc), 4 * px + 2 * py + pc


ALL_PEERS = tuple(range(1, N_DEV))
CHIP_PEERS = (2, 4, 6)
N_CHIPS = N_DEV // 2


def _sem_scratch(n, m):
    return [pltpu.SemaphoreType.DMA((n, m)), pltpu.SemaphoreType.DMA((n, m)), pltpu.SemaphoreType.DMA((n,))]


class Exchange:
    def __init__(self, arrays, out_shapes, sem_cols, plan, aliased=False):
        self.arrays, self.out_shapes, self.plan = list(arrays), list(out_shapes), plan
        self.scratch = _sem_scratch(len(self.arrays), sem_cols)
        self.aliased = aliased

    def split(self, refs):
        n = len(self.arrays)
        return refs[:n], refs[n:2 * n], refs[2 * n:]

    def start(self, srcs, dsts, sems):
        local, sends, _ = self.plan(srcs, dsts, *sems)
        for cp in local + sends:
            cp.start()

    def wait(self, srcs, dsts, sems):
        local, sends, recvs = self.plan(srcs, dsts, *sems)
        for cp in recvs:
            cp.wait_recv()
        for cp in sends:
            cp.wait_send()
        for cp in local:
            cp.wait()

    def aliases(self, first_in, first_out):
        return {first_in + j: first_out + j for j in range(len(self.arrays))} if self.aliased else {}

    def run(self, name):
        def body(*refs):
            srcs, dsts, sems = self.split(refs)
            self.start(srcs, dsts, sems)
            self.wait(srcs, dsts, sems)

        n = len(self.arrays)
        return pl.pallas_call(body, name=name, in_specs=[_ANY] * n, out_specs=[_ANY] * n, out_shape=self.out_shapes,
                              scratch_shapes=self.scratch, input_output_aliases=self.aliases(0, 0))(*self.arrays)


def exchange(arrays, scatter, peers, by_chip=False):
    slots = N_CHIPS if by_chip else N_DEV

    def plan(srcs, dsts, send_sems, recv_sems, local_sems):
        x, y, c, me = _place()
        mine = 2 * x + y if by_chip else me
        local = [pltpu.make_async_copy(src.at[mine] if scatter else src, dst.at[mine], local_sems.at[j])
                 for j, (src, dst) in enumerate(zip(srcs, dsts))]
        sends, recvs = [], []
        for t, k in enumerate(peers):
            peer, pidx = _peer(x, y, c, k)
            theirs = 2 * peer[0] + peer[1] if by_chip else pidx
            for j, (src, dst) in enumerate(zip(srcs, dsts)):
                part = src.at[theirs] if scatter else src
                sems = dict(send_sem=send_sems.at[j, t], recv_sem=recv_sems.at[j, t], device_id=peer, device_id_type=_MESH)
                sends.append(pltpu.make_async_remote_copy(src_ref=part, dst_ref=dst.at[mine], **sems))
                recvs.append(pltpu.make_async_remote_copy(src_ref=part, dst_ref=dst.at[theirs], **sems))
        return local, sends, recvs

    shapes = [jax.ShapeDtypeStruct(a.shape if scatter else (slots,) + a.shape, a.dtype) for a in arrays]
    return Exchange(arrays, shapes, len(peers), plan)


def second_hop(gathered):
    def plan(srcs, dsts, send_sems, recv_sems, local_sems):
        x, y, c, me = _place()
        sibling, _ = _peer(x, y, c, 1)
        sends, recvs = [], []
        for t, k in enumerate(CHIP_PEERS):
            _, landed = _peer(x, y, c, k)
            _, coming = _peer(x, y, c, k ^ 1)
            for j, buf in enumerate(dsts):
                sems = dict(send_sem=send_sems.at[j, t], recv_sem=recv_sems.at[j, t], device_id=sibling, device_id_type=_MESH)
                sends.append(pltpu.make_async_remote_copy(src_ref=buf.at[landed], dst_ref=buf.at[landed], **sems))
                recvs.append(pltpu.make_async_remote_copy(src_ref=buf.at[coming], dst_ref=buf.at[coming], **sems))
        return [], sends, recvs

    shapes = [jax.ShapeDtypeStruct(a.shape, a.dtype) for a in gathered]
    return Exchange(gathered, shapes, len(CHIP_PEERS), plan, aliased=True)


FIRST_HOP_PEERS = (1,) + CHIP_PEERS


def _gather_two_level(arrays, name):
    n = len(arrays)

    def body(*refs):
        srcs, dsts, (send_sems, recv_sems, local_sems) = refs[:n], refs[n:2 * n], refs[2 * n:]
        x, y, c, me = _place()
        sibling, sidx = _peer(x, y, c, 1)

        def copy(j, sem, block, to, src=None):
            rows = dsts[j].at[block]
            return pltpu.make_async_remote_copy(src_ref=rows if src is None else src, dst_ref=rows, send_sem=send_sems.at[j, sem],
                                                recv_sem=recv_sems.at[j, sem], device_id=to, device_id_type=_MESH)

        local = [pltpu.make_async_copy(srcs[j], dsts[j].at[me], local_sems.at[j]) for j in range(n)]
        for cp in local:
            cp.start()
        first = [copy(j, 1 + t, me, _peer(x, y, c, k)[0], src=srcs[j]) for t, k in enumerate(CHIP_PEERS) for j in range(n)]
        first += [copy(j, 0, me, sibling, src=srcs[j]) for j in range(n)]
        for cp in first:
            cp.start()
        passed = []
        for t, k in enumerate(CHIP_PEERS):
            peer, pidx = _peer(x, y, c, k)
            for j in range(n):
                copy(j, 1 + t, pidx, peer).wait_recv()
                passed.append(copy(j, 4 + t, pidx, sibling))
                passed[-1].start()
        for j in range(n):
            copy(j, 0, sidx, sibling).wait_recv()
        for t, k in enumerate(CHIP_PEERS):
            _, pidx = _peer(x, y, c, k ^ 1)
            for j in range(n):
                copy(j, 4 + t, pidx, sibling).wait_recv()
        for cp in first + passed:
            cp.wait_send()
        for cp in local:
            cp.wait()

    shapes = [jax.ShapeDtypeStruct((N_DEV,) + a.shape, a.dtype) for a in arrays]
    return pl.pallas_call(body, name=name, in_specs=[_ANY] * n, out_specs=[_ANY] * n, out_shape=shapes,
                          scratch_shapes=_sem_scratch(n, 1 + 2 * len(CHIP_PEERS)))(*arrays)


def to_sibling(arrays):
    def plan(srcs, dsts, send_sems, recv_sems, local_sems):
        x, y, c, _ = _place()
        sibling, _ = _peer(x, y, c, 1)
        copies = [pltpu.make_async_remote_copy(src_ref=src.at[1 - c], dst_ref=dst, send_sem=send_sems.at[j, 0],
                                               recv_sem=recv_sems.at[j, 0], device_id=sibling, device_id_type=_MESH)
                  for j, (src, dst) in enumerate(zip(srcs, dsts))]
        return [], copies, copies

    return Exchange(arrays, [jax.ShapeDtypeStruct(a.shape[1:], a.dtype) for a in arrays], 1, plan)


def combine(a, b):
    assert not (a.aliased or b.aliased)
    na, nsem = len(a.arrays), len(a.scratch)

    def plan(srcs, dsts, *sems):
        return tuple(u + v for u, v in zip(a.plan(srcs[:na], dsts[:na], *sems[:nsem]), b.plan(srcs[na:], dsts[na:], *sems[nsem:])))

    both = Exchange(a.arrays + b.arrays, a.out_shapes + b.out_shapes, 1, plan)
    both.scratch = a.scratch + b.scratch
    return both


def pair_add(own, theirs, core, *, name):
    _, ns, r, c = own.shape
    rb = _row_block(r, c // 2)

    def body(core_ref, a_ref, b_ref, o_ref):
        o_ref[...] = (a_ref[...].astype(F32) + b_ref[...].astype(F32)).astype(o_ref.dtype)

    return pl.pallas_call(
        body, name=name,
        grid_spec=pltpu.PrefetchScalarGridSpec(
            num_scalar_prefetch=1, grid=(ns, r // rb),
            in_specs=[pl.BlockSpec((None, None, rb, c), lambda i, j, core_ref: (core_ref[0], i, j, 0)),
                      pl.BlockSpec((None, rb, c), lambda i, j, core_ref: (i, j, 0))],
            out_specs=pl.BlockSpec((None, rb, c), lambda i, j, core_ref: (i, j, 0))),
        out_shape=jax.ShapeDtypeStruct((ns, r, c), own.dtype), compiler_params=_cparams(),
    )(core, own, theirs)


ADAMW_BLOCK_BYTES = 1 << 20


def _row_block(r, c):
    for rb in range(r, 0, -1):
        if r % rb == 0 and (rb % 16 == 0 or rb == r) and rb * c * 4 <= ADAMW_BLOCK_BYTES:
            return rb
    return r


def adamw(w, m, v, parts, *, name):
    depth, r, c = w.shape
    n_parts = parts[0].shape[0]
    rb = _row_block(r, c)

    def body(w_ref, m_ref, v_ref, *refs):
        p_refs, (g_ref, d_ref, nm_ref, nv_ref) = refs[:depth], refs[depth:]

        def total(p_ref):
            g = p_ref[0].astype(F32)
            for j in range(1, n_parts):
                g = g + p_ref[j].astype(F32)
            return g

        g = total(p_refs[0])
        for l in range(1, depth):
            g = jnp.where(pl.program_id(0) == l, total(p_refs[l]), g)
        g_ref[...] = g
        m_new = ADAM_B1 * m_ref[...] + (1.0 - ADAM_B1) * g
        v_new = ADAM_B2 * v_ref[...] + (1.0 - ADAM_B2) * (g * g)
        m_hat = m_new / (1.0 - ADAM_B1 ** ADAM_STEP)
        v_hat = v_new / (1.0 - ADAM_B2 ** ADAM_STEP)
        d_ref[...] = -ADAM_LR * (m_hat / (jnp.sqrt(v_hat) + ADAM_EPS) + ADAM_WD * w_ref[...])
        nm_ref[...] = m_new
        nv_ref[...] = v_new

    wblk = pl.BlockSpec((None, rb, c), lambda l, i: (l, i, 0))
    pblk = pl.BlockSpec((n_parts, rb, c), lambda l, i: (0, i, 0))
    return pl.pallas_call(
        body, name=name, grid=(depth, r // rb),
        in_specs=[wblk, wblk, wblk] + [pblk] * depth, out_specs=[wblk] * 4,
        out_shape=[jax.ShapeDtypeStruct((depth, r, c), F32)] * 4, compiler_params=_cparams(),
    )(w, m, v, *parts)


BIG = (("w_in", 2), ("w_uq", 2), ("w_ukv", 2), ("w_pa", 2), ("w_pb", 1), ("w_o", 1), ("w_gate", 2), ("w_up", 2), ("w_down", 1))
SMALL = ("norm_mix_g", "pool_w", "pool_scale", "q_norm_g", "kv_norm_g", "norm_ffn_g", "final_norm_g")
WEIGHTS = ("meta_tokens", "norm_mix_g", "w_in", "pool_w", "pool_scale", "q_norm_g", "kv_norm_g", "w_uq", "w_ukv", "w_pa", "w_pb",
           "w_o", "norm_ffn_g", "w_gate", "w_up", "w_down", "final_norm_g")
HEAD_QK = QK_NOPE + QK_ROPE
KR_END = Z_KR + QK_ROPE


def _cat_cols(parts):
    return [jnp.concatenate(parts, axis=1)]


def _cat_rows(parts):
    return [jnp.concatenate(parts, axis=0)]


def _arr_w_in(parts):
    full = jnp.concatenate(parts, axis=1)
    zc = lambda n: jnp.zeros((full.shape[0], n), full.dtype)
    return [jnp.concatenate([full[:, :Z_KR], zc(QK_NOPE), full[:, Z_KR:KR_END], zc(LANES - HEAD_QK), full[:, KR_END:]], axis=1)]


def _arr_w_uq(parts):
    full = jnp.concatenate(parts, axis=1)
    z = jnp.zeros((full.shape[0], HEAD_SLOT - HEAD_QK), full.dtype)
    pieces = []
    for hd in range(N_HEADS):
        pieces += [full[:, hd * HEAD_QK:(hd + 1) * HEAD_QK], z]
    return [jnp.concatenate(pieces, axis=1)]


def _arr_w_ukv(parts):
    full = jnp.concatenate(parts, axis=1)
    z = jnp.zeros((full.shape[0], HEAD_SLOT - QK_NOPE), full.dtype)
    wide = QK_NOPE + V_DIM
    k, v = [], []
    for hd in range(N_HEADS):
        k += [full[:, hd * wide:hd * wide + QK_NOPE], z]
        v.append(full[:, hd * wide + QK_NOPE:(hd + 1) * wide])
    return [jnp.concatenate(k, axis=1), jnp.concatenate(v, axis=1)]


def arrange(g, fn, out_shapes, name):
    def body(g_ref, *o_refs):
        for o_ref, val in zip(o_refs, fn([g_ref[p] for p in range(N_DEV)])):
            o_ref[...] = val

    return pl.pallas_call(
        body, name=name, grid=(1,),
        in_specs=[pl.BlockSpec(g.shape, lambda i: (0, 0, 0))],
        out_specs=[pl.BlockSpec(s, lambda i: (0, 0)) for s in out_shapes],
        out_shape=[jax.ShapeDtypeStruct(s, g.dtype) for s in out_shapes], compiler_params=_cparams(),
    )(g)


def _arranged_ranges(lo, hi):
    out = []
    for a, b, shift in ((0, Z_KR, 0), (Z_KR, KR_END, QK_NOPE), (KR_END, D_IN, LANES - QK_ROPE)):
        s, e = max(lo, a), min(hi, b)
        if s < e:
            out.append((s + shift, e + shift))
    return out


def _chunks_w_in(acc):
    cs = D_IN // N_DEV
    return [jnp.concatenate([acc[:, a:b] for a, b in _arranged_ranges(p * cs, (p + 1) * cs)], axis=1) for p in range(N_DEV)]


def _chunks_w_uq(acc):
    per = N_HEADS // N_DEV
    return [jnp.concatenate([acc[:, hd * HEAD_SLOT:hd * HEAD_SLOT + HEAD_QK] for hd in range(p * per, (p + 1) * per)], axis=1)
            for p in range(N_DEV)]


def _chunks_w_ukv(acc_k, acc_v):
    per = N_HEADS // N_DEV
    out = []
    for p in range(N_DEV):
        pieces = []
        for hd in range(p * per, (p + 1) * per):
            pieces += [acc_k[:, hd * HEAD_SLOT:hd * HEAD_SLOT + QK_NOPE], acc_v[:, hd * V_DIM:(hd + 1) * V_DIM]]
        out.append(jnp.concatenate(pieces, axis=1))
    return out


def _chunks_cols(acc):
    cs = acc.shape[1] // N_DEV
    return [acc[:, p * cs:(p + 1) * cs] for p in range(N_DEV)]


def _chunks_rows(acc):
    rs = acc.shape[0] // N_DEV
    return [acc[p * rs:(p + 1) * rs, :] for p in range(N_DEV)]


def _chunks_cols_transposed(acc):
    at = acc[...].T
    rs = at.shape[0] // N_DEV
    return [at[p * rs:(p + 1) * rs, :] for p in range(N_DEV)]


def _pack(parts, row_multiple):
    flat = jnp.concatenate([p.reshape(-1) for p in parts])
    return jnp.pad(flat, (0, -flat.shape[0] % (row_multiple * LANES))).reshape(-1, LANES)


def _unpack(packed, shapes):
    flat, out, off = packed.reshape(-1), [], 0
    for s in shapes:
        n = 1
        for d in s:
            n *= d
        out.append(flat[off:off + n].reshape(s))
        off += n
    return out


def _rope_table(lp, nb):
    inv = 1.0 / (ROPE_THETA ** (jnp.arange(0, QK_ROPE, 2, dtype=F32) / QK_ROPE))
    ang = jnp.arange(lp, dtype=F32)[:, None] * inv[None, :]
    cos, sin = jnp.cos(ang), jnp.sin(ang)
    z = lambda n: jnp.zeros((lp, n), F32)
    tail = LANES - QK_NOPE - QK_ROPE
    c = jnp.concatenate([jnp.ones((lp, QK_NOPE), F32), cos, cos, z(tail)], axis=1)
    cr = jnp.concatenate([z(QK_NOPE), cos, cos, z(tail)], axis=1)
    s1 = jnp.concatenate([z(QK_NOPE), -sin, z(HALF_ROPE), z(tail)], axis=1)
    s2 = jnp.concatenate([z(QK_NOPE), z(HALF_ROPE), sin, z(tail)], axis=1)
    return jnp.tile(jnp.concatenate([c, cr, s1, s2], axis=1), (nb, 1))


MIX = ("w_in", "w_uq", "w_ukv", "w_pa", "w_pb", "w_o")
FFN = ("w_gate", "w_up", "w_down")
TRANSPOSED = ("w_gate", "w_up")
ARRANGERS = {
    "w_in": (_arr_w_in, (("win", (D_MODEL, DZ)),)), "w_uq": (_arr_w_uq, (("wuq", (Q_RANK, QK_WIDTH)),)),
    "w_ukv": (_arr_w_ukv, (("wuk", (KV_RANK, QK_WIDTH)), ("wuv", (KV_RANK, D_MODEL)))),
    "w_pa": (_cat_cols, (("wpa", (POOL_WIDTH, D_MODEL)),)), "w_pb": (_cat_rows, (("wpb", (D_MODEL, D_MODEL)),)),
    "w_o": (_cat_rows, (("wo", (D_MODEL, D_MODEL)),)), "w_gate": (_cat_rows, (("wgt", (D_FF, D_MODEL)),)),
    "w_up": (_cat_rows, (("wut", (D_FF, D_MODEL)),)), "w_down": (_cat_rows, (("wd", (D_FF, D_MODEL)),)),
}


def _operands(gathered, names, l):
    p = {}
    for n in names:
        fn, outs = ARRANGERS[n]
        if fn is _cat_rows:
            p[outs[0][0]] = gathered[n].reshape(outs[0][1])
            continue
        for (key, _), a in zip(outs, arrange(gathered[n], fn, [s for _, s in outs], f"arrange_{n}_{l}")):
            p[key] = a
    return p


def _small_operands(small, l):
    pw = small["pool_w"][l].astype(BF16)
    return dict(g_mix=small["norm_mix_g"][l][None], gq=small["q_norm_g"][l][None], gkv=small["kv_norm_g"][l][None],
                g_ffn=small["norm_ffn_g"][l][None], ps=small["pool_scale"][l][None], pw=pw)


class MeshComm:
    def __init__(self, w, meta_tokens):
        self.src = lambda n, l: w[n][l].astype(BF16)
        self.meta_tokens = meta_tokens
        self.core = lax.axis_index("c").astype(jnp.int32).reshape(1)
        self.rides = {0: [(n, 0) for n in FFN] + [(n, 1) for n in MIX], 1: [(n, 1) for n in FFN]}

    def first_weights(self):
        got = _gather_two_level([self.src(n, 0) for n in MIX] + [self.meta_tokens], "gather_mix_0")
        return dict(zip(MIX, got)), jnp.moveaxis(got[-1], 0, 1).reshape(N_META, D_MODEL)

    def first_hop(self, l):
        return exchange([self.src(n, layer) for n, layer in self.rides[l]], False, FIRST_HOP_PEERS)

    def second_hop(self, l, landed):
        return second_hop(landed)

    def carried(self, l, full, names, layer):
        return {n: full[self.rides[l].index((n, layer))] for n in names}

    def pair_exchange(self, own):
        return to_sibling(own)

    def pair_add(self, own, theirs, names, tag):
        return [pair_add(a, b, self.core, name=f"pair_add_{n}_{tag}") for n, a, b in zip(names, own, theirs)]

    def last_pair_exchange(self, own, small):
        got = combine(to_sibling(own), exchange([small], False, ALL_PEERS)).run("pair_grads_mix_0")
        return got[:-1], got[-1]

    def scatter(self, sums):
        return exchange(sums, True, CHIP_PEERS, by_chip=True)

    def scatter_now(self, sums, name):
        return self.scatter(sums).run(name)


HEADS_FWD, HEADS_BWD = 8, 4
TILE_ROWS, TILE_ROWS_BWD = 512, 256


def _tile(t, target):
    n = max(1, -(-t // (target + target // 8)))
    while t % n or (t // n) % 16:
        n += 1
    return t // n


def _wgrad_tile(t):
    return max(tm for tm in (2 * TQ, TQ, LANES) if t % tm == 0)


def _ffn_bwd_part(dh2, p, s, tag, ride):
    d, ff = D_MODEL, D_FF // N_DEV
    t = dh2.shape[0]
    wg_ = lambda n, x, ys, fn, shape: wgrad(x, ys, fn, shape, tm=_wgrad_tile(t), name=f"wgrad_{n}_{tag}")
    (dh1, hn2, act, dgt, dup, dg_ffn), brought = ffn_bwd(dh2, s["h1"], p["g_ffn"], s["gt"], s["up"], p["wgt"], p["wut"], p["wd"],
                                                         tm=_tile(t, TILE_ROWS_BWD), name=f"ffn_bwd_{tag}", ride=ride)
    chunks = [wg_("gate", hn2, [dgt], _chunks_cols_transposed, (ff, d)), wg_("up", hn2, [dup], _chunks_cols_transposed, (ff, d)),
              wg_("down", act, [dh2], _chunks_rows, (ff, d))]
    return dh1, chunks, dict(norm_ffn_g=dg_ffn[0]), brought


def _mix_bwd_part(dh1, p, s, rope, nb, lp, tag, ride, next_ride):
    d = D_MODEL
    t = dh1.shape[0]
    wg_ = lambda n, x, ys, fn, shape: wgrad(x, ys, fn, shape, tm=_wgrad_tile(t), name=f"wgrad_{n}_{tag}")
    (dga, dgb, dpa, dpb, do, dpool, dps, dpw), first = merge_bwd(dh1, s["z"], s["pa"], s["pb"], s["pooled"], p["pw"], p["ps"],
                                                                   p["wpa"], p["wpb"], p["wo"], tm=_tile(t, TILE_ROWS),
                                                                   name=f"merge_bwd_{tag}", ride=ride)
    c_o = wg_("o", s["mg"], [dh1], _chunks_rows, (d // N_DEV, d))
    c_pa = wg_("pa", s["a"], [dpa], _chunks_cols, (POOL_WIDTH, d // N_DEV))
    c_pb = wg_("pb", s["o"], [dpb], _chunks_rows, (d // N_DEV, d))
    (dq, dk, dv), brought = attn_bwd(s["q"], s["k"], s["v"], s["o"], do, s["lse"], nb=nb, lp=lp, hb=HEADS_BWD,
                                     name=f"attn_bwd_{tag}", ride=next_ride(first))
    dh, hn, dz, cqn, ckvn, dqb, dkb, dvb, dg_mix, dgq, dgkv = in_proj_bwd(
        dh1, s["h"], p["g_mix"], s["z"], dq, dk, dv, dga, dgb, dpool, p["win"], p["gq"], p["gkv"], p["wuq"], p["wuk"], p["wuv"],
        rope, tm=_tile(t, TILE_ROWS_BWD), lp=lp, nb=nb, name=f"in_proj_bwd_{tag}")
    c_in = wg_("in", hn, [dz], _chunks_w_in, (d, D_IN // N_DEV))
    c_uq = wg_("uq", cqn, [dqb], _chunks_w_uq, (Q_RANK, N_HEADS * HEAD_QK // N_DEV))
    c_ukv = wg_("ukv", ckvn, [dkb, dvb], _chunks_w_ukv, (KV_RANK, N_HEADS * (QK_NOPE + V_DIM) // N_DEV))
    small = dict(pool_scale=dps[0], pool_w=dpw, norm_mix_g=dg_mix[0], q_norm_g=dgq[0], kv_norm_g=dgkv[0])
    return dh, [c_in, c_uq, c_ukv, c_pa, c_pb, c_o], small, brought


def train_step(x, loss_target, small, comm):
    nb, seq, d = x.shape
    lp = -(-(N_META + seq) // LANES) * LANES
    t = nb * lp
    assert nb <= 2 and DEPTH == 2
    tm = _tile(t, TILE_ROWS)
    rope = _rope_table(lp, nb)
    gathered, meta = comm.first_weights()
    h = padded_rows(meta, x, lp, "place_x")
    target = padded_rows(jnp.zeros((N_META, d), F32), loss_target, lp, "place_target")

    params, saved, full = [], [], {}
    for l in range(DEPTH):
        p = _small_operands(small, l)
        p.update(_operands(gathered if l == 0 else comm.carried(0, full[0], MIX, 1), MIX, l))
        z, q, k, v = in_proj_fwd(h, p["g_mix"], p["win"], p["gq"], p["gkv"], p["wuq"], p["wuk"], p["wuv"], rope, tm=tm,
                                 name=f"in_proj_fwd_{l}")
        (o, lse), landed = attn_fwd(q, k, v, nb=nb, lp=lp, hb=HEADS_FWD, name=f"attn_fwd_{l}", ride=comm.first_hop(l))
        (h1, pooled, a, pa, pb, mg), full[l] = merge_fwd(h, z, o, p["pw"], p["ps"], p["wpa"], p["wpb"], p["wo"], tm=tm, lp=lp,
                                                          nb=nb, name=f"merge_fwd_{l}", ride=comm.second_hop(l, landed))
        p.update(_operands(comm.carried(l, full[l], FFN, l), FFN, l))
        h2, gt, up = ffn_fwd(h1, p["g_ffn"], p["wgt"], p["wut"], p["wd"], tm=tm, name=f"ffn_fwd_{l}")
        params.append(p)
        saved.append(dict(h=h, z=z, q=q, k=k, v=v, o=o, lse=lse, h1=h1, pooled=pooled, a=a, pa=pa, pb=pb, mg=mg, gt=gt, up=up))
        h = h2
    parts, dh, dgf = loss_head(h, small["final_norm_g"][None], target, tm=tm, lp=lp, nb=nb, seq=seq, name="loss_head")
    loss = jnp.sum(parts[::8, 0])

    sums = {}
    dh, c_ffn1, small1, _ = _ffn_bwd_part(dh, params[1], saved[1], 1, None)
    dh, c_mix1, sm, brought = _mix_bwd_part(
        dh, params[1], saved[1], rope, nb, lp, 1, comm.pair_exchange(c_ffn1),
        lambda theirs: comm.scatter(comm.pair_add(c_ffn1, theirs, FFN, "ffn_1")))
    small1.update(sm)
    sums.update({(n, 1): a for n, a in zip(FFN, brought)})
    dh, c_ffn0, small0, theirs = _ffn_bwd_part(dh, params[0], saved[0], 0, comm.pair_exchange(c_mix1))
    s_mix1 = comm.pair_add(c_mix1, theirs, MIX, "mix_1")
    dh, c_mix0, sm, brought = _mix_bwd_part(
        dh, params[0], saved[0], rope, nb, lp, 0, comm.pair_exchange(c_ffn0),
        lambda theirs: comm.scatter(s_mix1 + comm.pair_add(c_ffn0, theirs, FFN, "ffn_0")))
    small0.update(sm)
    sums.update({(n, l): a for (n, l), a in zip([(n, 1) for n in MIX] + [(n, 0) for n in FFN], brought)})
    dh = dh.reshape(nb, lp, d)
    dmeta = jnp.sum(dh[:, :N_META], axis=0)
    meta_chunks = jnp.transpose(dmeta.reshape(N_META, N_CHIPS, 2, d // N_DEV), (2, 1, 0, 3)).astype(BF16)
    small_grads = {n: jnp.stack([small0[n], small1[n]]) for n in small0}
    small_grads["final_norm_g"] = dgf[0]
    last_names = MIX + ("meta_tokens",)
    theirs, small_parts = comm.last_pair_exchange(c_mix0 + [meta_chunks], _pack([small_grads[n] for n in SMALL], 8))
    last = comm.scatter_now(comm.pair_add(c_mix0 + [meta_chunks], theirs, last_names, "mix_0"), "scatter_mix_0")
    sums.update({(n, 0): a for n, a in zip(last_names, last)})
    return loss, token_rows(dh, seq, "take_grad_x"), sums, small_grads, small_parts


def kernel(x, meta_tokens, norm_mix_g, w_in, pool_w, pool_scale, q_norm_g, kv_norm_g, w_uq, w_ukv, w_pa, w_pb, w_o, norm_ffn_g, w_gate, w_up, w_down, final_norm_g, loss_target, m_meta_tokens, m_norm_mix_g, m_w_in, m_pool_w, m_pool_scale, m_q_norm_g, m_kv_norm_g, m_w_uq, m_w_ukv, m_w_pa, m_w_pb, m_w_o, m_norm_ffn_g, m_w_gate, m_w_up, m_w_down, m_final_norm_g, v_meta_tokens, v_norm_mix_g, v_w_in, v_pool_w, v_pool_scale, v_q_norm_g, v_kv_norm_g, v_w_uq, v_w_ukv, v_w_pa, v_w_pb, v_w_o, v_norm_ffn_g, v_w_gate, v_w_up, v_w_down, v_final_norm_g):
    args = dict(locals())
    w = {n: args[n] for n in WEIGHTS}
    m = {n: args["m_" + n] for n in WEIGHTS}
    v = {n: args["v_" + n] for n in WEIGHTS}
    small = {n: w[n] for n in SMALL}
    as_handled = lambda a, n: jnp.swapaxes(a, 1, 2) if n in TRANSPOSED else a
    wh, mh, vh = ({n: as_handled(d[n], n) for n, _ in BIG} for d in (w, m, v))

    loss, grad_x, sums, _, small_recv = train_step(x, loss_target, small, MeshComm(wh, meta_tokens))
    loss = lax.psum(loss, ("x", "y", "c"))

    out = {n: [as_handled(a, n) for a in adamw(wh[n], mh[n], vh[n], [sums[(n, l)] for l in range(DEPTH)], name=f"adamw_{n}")]
           for n, _ in BIG}
    out["meta_tokens"] = [a[0] for a in adamw(meta_tokens[None], m["meta_tokens"][None], v["meta_tokens"][None],
                                              [sums[("meta_tokens", 0)]], name="adamw_meta_tokens")]
    pk = lambda d: _pack([d[n] for n in SMALL], 8)[None]
    packed = adamw(pk(w), pk(m), pk(v), [small_recv], name="adamw_small")
    shapes = [w[n].shape for n in SMALL]
    for n, *kinds in zip(SMALL, *[_unpack(packed[kind][0], shapes) for kind in range(4)]):
        out[n] = kinds
    return (loss, grad_x, *[out[n][kind] for kind in range(4) for n in WEIGHTS])
```

```python
import functools
import math

import jax
import jax.numpy as jnp
from jax import lax
from jax.experimental import pallas as pl
from jax.experimental.pallas import tpu as pltpu

F32, BF16 = jnp.float32, jnp.bfloat16

D_MODEL = 1024
N_META = 16
N_HEADS = 16
QK_NOPE, QK_ROPE, V_DIM = 64, 32, 64
HALF_ROPE = QK_ROPE // 2
Q_RANK, KV_RANK = 256, 128
POOL_WINDOWS = (2, 4, 8, 16)
POOL_GROUP = 128
POOL_WIDTH = POOL_GROUP * len(POOL_WINDOWS)
POOL_HALO = 16
D_FF = 2816
D_IN = 2976
NORM_EPS = 1e-6
SM_SCALE = (QK_NOPE + QK_ROPE) ** -0.5
LOG2E = math.log2(math.e)
EXP2_SCALE = SM_SCALE * LOG2E
MASK_VALUE = -1e30
ROPE_THETA = 10000.0
DEPTH = 2
N_DEV = 8

ADAM_LR, ADAM_B1, ADAM_B2, ADAM_EPS, ADAM_WD, ADAM_STEP = 0.001, 0.9, 0.999, 1e-08, 0.01, 10

LANES = 128
HEAD_SLOT = LANES
QK_WIDTH = N_HEADS * HEAD_SLOT
Z_CQ, Z_CKV, Z_KR, Z_GA, Z_GB, DZ = 512, 768, 896, 1024, 2048, 3072
TQ = TK = 256
VMEM_LIMIT = 56 * 1024 * 1024


def _cparams():
    return pltpu.CompilerParams(vmem_limit_bytes=VMEM_LIMIT)


def _rows(tm, width, col=0):
    return pl.BlockSpec((tm, width), lambda i: (i, col))


def _whole(shape):
    zeros = (0,) * len(shape)
    return pl.BlockSpec(shape, lambda i: zeros, pipeline_mode=pl.Buffered(1))


def _acc(shape):
    zeros = (0,) * len(shape)
    return pl.BlockSpec(shape, lambda i: zeros)


def _dot(a, b):
    return jnp.dot(a, b, preferred_element_type=F32)


def _dot_tn(a, b):
    return lax.dot_general(a, b, (((0,), (0,)), ((), ())), preferred_element_type=F32)


def _dot_nt(a, b):
    return lax.dot_general(a, b, (((1,), (1,)), ((), ())), preferred_element_type=F32)


def _rms(x):
    r = lax.rsqrt(jnp.mean(x * x, axis=-1, keepdims=True) + NORM_EPS)
    return x * r, r


def _rms_bwd(dy, xhat, r, g):
    dg = jnp.sum(dy * xhat, axis=0, keepdims=True)
    dxh = dy * g
    dx = r * (dxh - xhat * jnp.mean(dxh * xhat, axis=-1, keepdims=True))
    return dx, dg


def _sigmoid(x):
    return 1.0 / (1.0 + jnp.exp(-x))


def _rope_fwd(q, c, s1, s2):
    w = q.shape[1]
    return q * c + pltpu.roll(q, w - HALF_ROPE, 1) * s1 + pltpu.roll(q, HALF_ROPE, 1) * s2


def _rope_bwd(dq, c, s1, s2):
    w = dq.shape[1]
    return dq * c + pltpu.roll(dq * s1, HALF_ROPE, 1) + pltpu.roll(dq * s2, w - HALF_ROPE, 1)


def _rope_tables(rope, reps):
    c, cr, s1, s2 = (rope[:, k * LANES:(k + 1) * LANES] for k in range(4))
    if reps > 1:
        return jnp.tile(c, (1, reps)), jnp.tile(s1, (1, reps)), jnp.tile(s2, (1, reps))
    return cr, s1, s2


def _seq_pos(gi, lp, nb):
    pos = gi
    for b in range(1, nb):
        pos = jnp.where(gi >= b * lp, gi - b * lp, pos)
    return pos


_ANY = pl.BlockSpec(memory_space=pl.ANY)


def _carrying_call(body, ride, operands, *, name, grid, in_specs, out_specs, out_shape, scratch_shapes=()):
    n_in, n_out = len(in_specs), len(out_specs)
    if ride is None:
        out = pl.pallas_call(body, name=name, grid=grid, in_specs=in_specs, out_specs=out_specs, out_shape=out_shape,
                             scratch_shapes=list(scratch_shapes), compiler_params=_cparams())(*operands)
        return out, []
    ne = len(ride.arrays)

    def carrying(*refs):
        ins, r_in, rest = refs[:n_in], refs[n_in:n_in + ne], refs[n_in + ne:]
        outs, r_out, rest = rest[:n_out], rest[n_out:n_out + ne], rest[n_out + ne:]
        scratch, sems = rest[:len(scratch_shapes)], rest[len(scratch_shapes):]
        ids = [pl.program_id(a) for a in range(len(grid))]
        first = functools.reduce(jnp.logical_and, [i == 0 for i in ids])
        last = functools.reduce(jnp.logical_and, [i == g - 1 for i, g in zip(ids, grid)])

        @pl.when(first)
        def _():
            ride.start(r_in, r_out, sems)

        body(*ins, *outs, *scratch)

        @pl.when(last)
        def _():
            ride.wait(r_in, r_out, sems)

    out = pl.pallas_call(
        carrying, name=name, grid=grid, in_specs=list(in_specs) + [_ANY] * ne, out_specs=list(out_specs) + [_ANY] * ne,
        out_shape=list(out_shape) + ride.out_shapes, scratch_shapes=list(scratch_shapes) + ride.scratch,
        input_output_aliases=ride.aliases(n_in, n_out), compiler_params=_cparams(),
    )(*operands, *ride.arrays)
    return out[:n_out], out[n_out:]


def in_proj_fwd(h, g_mix, win, gq, gkv, wuq, wuk, wuv, rope, *, tm, name, ride=None):
    t = h.shape[0]

    def body(h_ref, g_ref, win_ref, gq_ref, gkv_ref, wuq_ref, wuk_ref, wuv_ref, rope_ref, z_ref, q_ref, k_ref, v_ref):
        xhat, _ = _rms(h_ref[...])
        hn = (xhat * g_ref[...]).astype(BF16)
        z = _dot(hn, win_ref[...])
        z_ref[...] = z
        rope_t = rope_ref[...]
        xq, _ = _rms(z[:, Z_CQ:Z_CKV])
        cqn = (xq * gq_ref[...]).astype(BF16)
        q = _rope_fwd(_dot(cqn, wuq_ref[...]), *_rope_tables(rope_t, N_HEADS))
        q_ref[...] = q.astype(BF16)
        xkv, _ = _rms(z[:, Z_CKV:Z_KR])
        ckvn = (xkv * gkv_ref[...]).astype(BF16)
        kr = _rope_fwd(z[:, Z_KR:Z_GA], *_rope_tables(rope_t, 1))
        k_ref[...] = (_dot(ckvn, wuk_ref[...]) + jnp.tile(kr, (1, N_HEADS))).astype(BF16)
        v_ref[...] = _dot(ckvn, wuv_ref[...]).astype(BF16)

    return _carrying_call(
        body, ride, (h, g_mix, win, gq, gkv, wuq, wuk, wuv, rope), name=name, grid=(t // tm,),
        in_specs=[_rows(tm, D_MODEL), _whole((1, D_MODEL)), _whole((D_MODEL, DZ)), _whole((1, Q_RANK)), _whole((1, KV_RANK)),
                  _whole((Q_RANK, QK_WIDTH)), _whole((KV_RANK, QK_WIDTH)), _whole((KV_RANK, D_MODEL)), _rows(tm, 4 * LANES)],
        out_specs=[_rows(tm, DZ), _rows(tm, QK_WIDTH), _rows(tm, QK_WIDTH), _rows(tm, D_MODEL)],
        out_shape=[jax.ShapeDtypeStruct((t, DZ), F32), jax.ShapeDtypeStruct((t, QK_WIDTH), BF16),
                   jax.ShapeDtypeStruct((t, QK_WIDTH), BF16), jax.ShapeDtypeStruct((t, D_MODEL), BF16)])


def attn_fwd(q, k, v, *, nb, lp, hb, name, ride=None):
    t = q.shape[0]
    nq, tail = lp // TQ, lp % TQ
    assert tail % LANES == 0

    def body(q_ref, k_ref, v_ref, o_ref, lse_ref, vt):
        for pr in range(hb // 2):
            vt[pr] = v_ref[:, pr * LANES:(pr + 1) * LANES].T

        def q_block(qs, tq, whole_k):
            qh = [q_ref[pl.ds(qs, tq), hd * HEAD_SLOT:(hd + 1) * HEAD_SLOT] for hd in range(hb)]
            keep = lax.broadcasted_iota(jnp.int32, (tq, tq), 0) <= lax.broadcasted_iota(jnp.int32, (tq, tq), 1)

            def k_steps(blocks, c, masked):
                sts = [[_dot_nt(k_ref[pl.ds(ks, tk), hd * HEAD_SLOT:(hd + 1) * HEAD_SLOT], qh[hd]) for hd in range(hb)]
                       for ks, tk in blocks]
                for (ks, tk), st_b in zip(blocks, sts):
                    ps, stats = [], []
                    for hd in range(hb):
                        m, l, _ = c[hd]
                        st = jnp.where(keep, st_b[hd], MASK_VALUE) if masked else st_b[hd]
                        m_new = jnp.maximum(m, jnp.max(st, axis=0, keepdims=True))
                        p = jnp.exp2((st - m_new) * EXP2_SCALE)
                        alpha = jnp.exp2((m - m_new) * EXP2_SCALE)
                        ps.append(p.astype(BF16))
                        stats.append((m_new, alpha * l + jnp.sum(p, axis=0, keepdims=True), alpha))
                    pvs = [_dot(vt[hd // 2, :, pl.ds(ks, tk)], ps[hd]) for hd in range(hb)]
                    c = tuple((stats[hd][0], stats[hd][1], stats[hd][2] * c[hd][2] + pvs[hd]) for hd in range(hb))
                return c

            def two_blocks(i, c):
                ks = pl.multiple_of(2 * i * TK, TK)
                return k_steps([(ks, TK), (ks + TK, TK)], c, False)

            init = tuple((jnp.full((1, tq), MASK_VALUE, F32), jnp.zeros((1, tq), F32), jnp.zeros((LANES, tq), F32))
                         for _ in range(hb))
            pairs = lax.div(whole_k, 2)
            c = lax.fori_loop(0, pairs, two_blocks, init)
            c = lax.fori_loop(2 * pairs, whole_k, lambda kj, c: k_steps([(pl.multiple_of(kj * TK, TK), TK)], c, False), c)
            c = k_steps([(qs, tq)], c, True)
            sub = lax.broadcasted_iota(jnp.int32, (LANES, tq), 0)
            for pr in range(hb // 2):
                (m0, l0, a0), (m1, l1, a1) = c[2 * pr], c[2 * pr + 1]
                o_ref[pl.ds(qs, tq), pr * LANES:(pr + 1) * LANES] = jnp.where(sub < V_DIM, a0 / l0, a1 / l1).T.astype(BF16)
                lse_ref[2 * pr, :, pl.ds(qs, tq)] = m0 * SM_SCALE + jnp.log(l0)
                lse_ref[2 * pr + 1, :, pl.ds(qs, tq)] = m1 * SM_SCALE + jnp.log(l1)

        def whole_q_block(qi, carry):
            q_block(pl.multiple_of(qi * TQ, TQ), TQ, qi)
            return carry

        lax.fori_loop(0, nq, whole_q_block, 0)
        if tail:
            q_block(nq * TQ, tail, nq)

    blk = lambda w: pl.BlockSpec((lp, w), lambda b, g: (b, g))
    return _carrying_call(
        body, ride, (q, k, v), name=name, grid=(nb, N_HEADS // hb),
        in_specs=[blk(hb * HEAD_SLOT), blk(hb * HEAD_SLOT), blk(hb * V_DIM)],
        out_specs=[blk(hb * V_DIM), pl.BlockSpec((hb, 1, lp), lambda b, g: (g, 0, b))],
        out_shape=[jax.ShapeDtypeStruct((t, D_MODEL), BF16), jax.ShapeDtypeStruct((N_HEADS, 1, t), F32)],
        scratch_shapes=[pltpu.VMEM((hb // 2, LANES, lp), BF16)])


def _pool_band_fwd(i, tm, lp, nb):
    r = lax.broadcasted_iota(jnp.int32, (tm, POOL_HALO + tm), 0)
    e = lax.broadcasted_iota(jnp.int32, (tm, POOL_HALO + tm), 1)
    diff = r + POOL_HALO - e
    pos = _seq_pos(i * tm + lax.broadcasted_iota(jnp.int32, (tm, 1), 0), lp, nb)
    out = []
    for w in POOL_WINDOWS:
        cnt = jnp.minimum(pos + 1, w)
        band = jnp.where((diff >= 0) & (diff < cnt), 1.0, 0.0).astype(BF16)
        out.append((band, cnt.astype(F32)))
    return out


def merge_fwd(h, z, o, pw, ps, wpa, wpb, wo, *, tm, lp, nb, name, ride=None):
    t = h.shape[0]
    hb = tm // POOL_HALO

    def body(h_ref, u_ref, uprev_ref, ga_ref, gb_ref, o_ref, pw_ref, ps_ref, wpa_ref, wpb_ref, wo_ref,
             h1_ref, pooled_ref, a_ref, pa_ref, pb_ref, mg_ref):
        i = pl.program_id(0)
        u = u_ref[...]
        uext = jnp.concatenate([uprev_ref[...], u], axis=0).astype(BF16)
        pooled, ys = [], []
        for g, (band, cnt) in enumerate(_pool_band_fwd(i, tm, lp, nb)):
            gs = slice(g * POOL_GROUP, (g + 1) * POOL_GROUP)
            pg = (_dot(band, uext[:, gs]) / cnt - u[:, gs]).astype(BF16)
            pooled.append(pg)
            ys.append(_dot(pg, pw_ref[g]))
        pooled_ref[...] = jnp.concatenate(pooled, axis=1)
        a = (jnp.concatenate(ys, axis=1) * ps_ref[...]).astype(BF16)
        a_ref[...] = a
        pa = _dot(a, wpa_ref[...])
        pb = _dot(o_ref[...], wpb_ref[...])
        pa_ref[...] = pa.astype(BF16)
        pb_ref[...] = pb.astype(BF16)
        mg = (_sigmoid(ga_ref[...]) * pa + _sigmoid(gb_ref[...]) * pb).astype(BF16)
        mg_ref[...] = mg
        h1_ref[...] = h_ref[...] + _dot(mg, wo_ref[...])

    halo = pl.BlockSpec((POOL_HALO, POOL_WIDTH), lambda i: (jnp.maximum(i * hb - 1, 0), 0))
    return _carrying_call(
        body, ride, (h, z, z, z, z, o, pw, ps, wpa, wpb, wo), name=name, grid=(t // tm,),
        in_specs=[_rows(tm, D_MODEL), _rows(tm, POOL_WIDTH), halo, _rows(tm, D_MODEL, 1), _rows(tm, D_MODEL, 2), _rows(tm, D_MODEL),
                  _whole((4, POOL_GROUP, POOL_GROUP)), _whole((1, POOL_WIDTH)), _whole((POOL_WIDTH, D_MODEL)),
                  _whole((D_MODEL, D_MODEL)), _whole((D_MODEL, D_MODEL))],
        out_specs=[_rows(tm, D_MODEL), _rows(tm, POOL_WIDTH), _rows(tm, POOL_WIDTH), _rows(tm, D_MODEL), _rows(tm, D_MODEL),
                   _rows(tm, D_MODEL)],
        out_shape=[jax.ShapeDtypeStruct((t, D_MODEL), F32), jax.ShapeDtypeStruct((t, POOL_WIDTH), BF16),
                   jax.ShapeDtypeStruct((t, POOL_WIDTH), BF16), jax.ShapeDtypeStruct((t, D_MODEL), BF16),
                   jax.ShapeDtypeStruct((t, D_MODEL), BF16), jax.ShapeDtypeStruct((t, D_MODEL), BF16)])


def ffn_fwd(h1, g, wgt, wut, wd, *, tm, name):
    t = h1.shape[0]

    def body(h_ref, g_ref, wgt_ref, wut_ref, wd_ref, h2_ref, gt_ref, up_ref):
        h = h_ref[...]
        xhat, _ = _rms(h)
        hn = (xhat * g_ref[...]).astype(BF16)
        gt = _dot_nt(hn, wgt_ref[...])
        up = _dot_nt(hn, wut_ref[...])
        gt_ref[...] = gt.astype(BF16)
        up_ref[...] = up.astype(BF16)
        act = (gt * _sigmoid(gt) * up).astype(BF16)
        h2_ref[...] = h + _dot(act, wd_ref[...])

    return pl.pallas_call(
        body, name=name, grid=(t // tm,),
        in_specs=[_rows(tm, D_MODEL), _whole((1, D_MODEL)), _whole((D_FF, D_MODEL)), _whole((D_FF, D_MODEL)), _whole((D_FF, D_MODEL))],
        out_specs=[_rows(tm, D_MODEL), _rows(tm, D_FF), _rows(tm, D_FF)],
        out_shape=[jax.ShapeDtypeStruct((t, D_MODEL), F32), jax.ShapeDtypeStruct((t, D_FF), BF16), jax.ShapeDtypeStruct((t, D_FF), BF16)],
        compiler_params=_cparams(),
    )(h1, g, wgt, wut, wd)


def loss_head(h, g, target, *, tm, lp, nb, seq, name):
    t = h.shape[0]
    nt = t // tm

    def body(h_ref, g_ref, t_ref, loss_ref, dh_ref, dg_ref):
        i = pl.program_id(0)
        pos = _seq_pos(i * tm + lax.broadcasted_iota(jnp.int32, (tm, 1), 0), lp, nb)
        real = (pos >= N_META) & (pos < N_META + seq)
        xhat, r = _rms(h_ref[...])
        gg = g_ref[...]
        err = jnp.where(real, xhat * gg - t_ref[...], 0.0)
        loss_ref[...] = jnp.full((8, LANES), 0.5 * jnp.sum(err * err) / D_MODEL, F32)
        dx, dg = _rms_bwd(err * (1.0 / D_MODEL), xhat, r, gg)
        dh_ref[...] = dx

        @pl.when(i == 0)
        def _():
            dg_ref[...] = jnp.zeros_like(dg_ref)

        dg_ref[...] += dg

    return pl.pallas_call(
        body, name=name, grid=(nt,),
        in_specs=[_rows(tm, D_MODEL), _whole((1, D_MODEL)), _rows(tm, D_MODEL)],
        out_specs=[pl.BlockSpec((8, LANES), lambda i: (i, 0)), _rows(tm, D_MODEL), _acc((1, D_MODEL))],
        out_shape=[jax.ShapeDtypeStruct((nt * 8, LANES), F32), jax.ShapeDtypeStruct((t, D_MODEL), F32),
                   jax.ShapeDtypeStruct((1, D_MODEL), F32)],
        compiler_params=_cparams(),
    )(h, g, target)


def wgrad(x, ys, chunk_fn, chunk_shape, *, tm, name):
    t, m = x.shape
    tiles = t // tm
    steps = -(-tiles // 2)

    def body(*refs):
        ins, o_ref, accs = refs[:2 * (1 + len(ys))], refs[2 * (1 + len(ys))], refs[2 * (1 + len(ys)) + 1:]
        i = pl.program_id(0)

        @pl.when(i == 0)
        def _():
            for acc in accs:
                acc[...] = jnp.zeros_like(acc)

        def both(first, second, mask):
            b = second[...].astype(BF16)
            if mask and tiles % 2:
                b = jnp.where(2 * i + 1 < tiles, b, jnp.zeros_like(b))
            return jnp.concatenate([first[...].astype(BF16), b], axis=0)

        xb = both(ins[0], ins[1], True)
        for j, acc in enumerate(accs):
            acc[...] += _dot_tn(xb, both(ins[2 + 2 * j], ins[3 + 2 * j], False))

        @pl.when(i == steps - 1)
        def _():
            for p, chunk in enumerate(chunk_fn(*accs)):
                o_ref[p % 2, p // 2] = chunk.astype(BF16)

    def two_tiles(width):
        return [pl.BlockSpec((tm, width), lambda i: (2 * i, 0)),
                pl.BlockSpec((tm, width), lambda i: (jnp.minimum(2 * i + 1, tiles - 1), 0))]

    out = (2, N_DEV // 2) + tuple(chunk_shape)
    operands = [x, x] + [a for y in ys for a in (y, y)]
    return pl.pallas_call(
        body, name=name, grid=(steps,),
        in_specs=two_tiles(m) + [s for y in ys for s in two_tiles(y.shape[1])], out_specs=_acc(out),
        out_shape=jax.ShapeDtypeStruct(out, BF16), scratch_shapes=[pltpu.VMEM((m, y.shape[1]), F32) for y in ys],
        compiler_params=_cparams(),
    )(*operands)


def ffn_bwd(dh2, h1, g, gt, up, wgt, wut, wd, *, tm, name, ride=None):
    t = h1.shape[0]

    def body(dh2_ref, h_ref, g_ref, gt_ref, up_ref, wgt_ref, wut_ref, wd_ref, dh1_ref, hn_ref, act_ref, dgt_ref, dup_ref, dg_ref):
        dh2 = dh2_ref[...]
        dact = _dot_nt(dh2.astype(BF16), wd_ref[...])
        gt = gt_ref[...].astype(F32)
        up = up_ref[...].astype(F32)
        sg = _sigmoid(gt)
        silu = gt * sg
        act_ref[...] = (silu * up).astype(BF16)
        dgt = (dact * up * (sg * (1.0 + gt * (1.0 - sg)))).astype(BF16)
        dup = (dact * silu).astype(BF16)
        dgt_ref[...] = dgt
        dup_ref[...] = dup
        dhn = _dot(dgt, wgt_ref[...]) + _dot(dup, wut_ref[...])
        xhat, r = _rms(h_ref[...])
        gg = g_ref[...]
        hn_ref[...] = (xhat * gg).astype(BF16)
        dx, dg = _rms_bwd(dhn, xhat, r, gg)
        dh1_ref[...] = dh2 + dx

        @pl.when(pl.program_id(0) == 0)
        def _():
            dg_ref[...] = jnp.zeros_like(dg_ref)

        dg_ref[...] += dg

    return _carrying_call(
        body, ride, (dh2, h1, g, gt, up, wgt, wut, wd), name=name, grid=(t // tm,),
        in_specs=[_rows(tm, D_MODEL), _rows(tm, D_MODEL), _whole((1, D_MODEL)), _rows(tm, D_FF), _rows(tm, D_FF),
                  _whole((D_FF, D_MODEL)), _whole((D_FF, D_MODEL)), _whole((D_FF, D_MODEL))],
        out_specs=[_rows(tm, D_MODEL), _rows(tm, D_MODEL), _rows(tm, D_FF), _rows(tm, D_FF), _rows(tm, D_FF), _acc((1, D_MODEL))],
        out_shape=[jax.ShapeDtypeStruct((t, D_MODEL), F32), jax.ShapeDtypeStruct((t, D_MODEL), BF16),
                   jax.ShapeDtypeStruct((t, D_FF), BF16), jax.ShapeDtypeStruct((t, D_FF), BF16),
                   jax.ShapeDtypeStruct((t, D_FF), BF16), jax.ShapeDtypeStruct((1, D_MODEL), F32)])


def merge_bwd(dh1, z, pa, pb, pooled, pw, ps, wpa, wpb, wo, *, tm, name, ride=None):
    t = dh1.shape[0]

    def body(dh1_ref, ga_ref, gb_ref, pa_ref, pb_ref, pooled_ref, pw_ref, ps_ref, wpa_ref, wpb_ref, wo_ref,
             dga_ref, dgb_ref, dpa_ref, dpb_ref, do_ref, dpool_ref, dps_ref, dpw_ref):
        dmg = _dot_nt(dh1_ref[...].astype(BF16), wo_ref[...])
        sa = _sigmoid(ga_ref[...])
        sb = _sigmoid(gb_ref[...])
        dga_ref[...] = (dmg * pa_ref[...].astype(F32) * sa * (1.0 - sa)).astype(BF16)
        dgb_ref[...] = (dmg * pb_ref[...].astype(F32) * sb * (1.0 - sb)).astype(BF16)
        dpa = (dmg * sa).astype(BF16)
        dpb = (dmg * sb).astype(BF16)
        dpa_ref[...] = dpa
        dpb_ref[...] = dpb
        do_ref[...] = _dot_nt(dpb, wpb_ref[...]).astype(BF16)
        da = _dot_nt(dpa, wpa_ref[...])
        pooled = pooled_ref[...]
        ps = ps_ref[...]

        @pl.when(pl.program_id(0) == 0)
        def _():
            dps_ref[...] = jnp.zeros_like(dps_ref)
            dpw_ref[...] = jnp.zeros_like(dpw_ref)

        dps, dpool = [], []
        for g in range(len(POOL_WINDOWS)):
            gs = slice(g * POOL_GROUP, (g + 1) * POOL_GROUP)
            y = _dot(pooled[:, gs], pw_ref[g])
            dps.append(jnp.sum(da[:, gs] * y, axis=0, keepdims=True))
            dy = (da[:, gs] * ps[:, gs]).astype(BF16)
            dpool.append(_dot_nt(dy, pw_ref[g]))
            dpw_ref[g] += _dot_tn(pooled[:, gs], dy)
        dps_ref[...] += jnp.concatenate(dps, axis=1)
        dpool_ref[...] = jnp.concatenate(dpool, axis=1)

    return _carrying_call(
        body, ride, (dh1, z, z, pa, pb, pooled, pw, ps, wpa, wpb, wo), name=name, grid=(t // tm,),
        in_specs=[_rows(tm, D_MODEL), _rows(tm, D_MODEL, 1), _rows(tm, D_MODEL, 2), _rows(tm, D_MODEL), _rows(tm, D_MODEL),
                  _rows(tm, POOL_WIDTH), _whole((4, POOL_GROUP, POOL_GROUP)),
                  _whole((1, POOL_WIDTH)), _whole((POOL_WIDTH, D_MODEL)), _whole((D_MODEL, D_MODEL)), _whole((D_MODEL, D_MODEL))],
        out_specs=[_rows(tm, D_MODEL), _rows(tm, D_MODEL), _rows(tm, D_MODEL), _rows(tm, D_MODEL), _rows(tm, D_MODEL),
                   _rows(tm, POOL_WIDTH), _acc((1, POOL_WIDTH)), _acc((4, POOL_GROUP, POOL_GROUP))],
        out_shape=[jax.ShapeDtypeStruct((t, D_MODEL), BF16)] * 5
        + [jax.ShapeDtypeStruct((t, POOL_WIDTH), F32), jax.ShapeDtypeStruct((1, POOL_WIDTH), F32),
           jax.ShapeDtypeStruct((4, POOL_GROUP, POOL_GROUP), F32)])


def attn_bwd(q, k, v, o, do, lse, *, nb, lp, hb, name, ride=None):
    t = q.shape[0]
    nq, tail = lp // TQ, lp % TQ
    assert tail % LANES == 0

    def body(q_ref, k_ref, v_ref, o_ref, do_ref, lse_ref, dq_ref, dk_ref, dv_ref, kt, doh, lse_row, delta_row, dqt):
        lane = lax.broadcasted_iota(jnp.int32, (lp, LANES), 1)
        first = lane < V_DIM
        sub = lax.broadcasted_iota(jnp.int32, (LANES, lp), 0)
        for pr in range(hb // 2):
            ls = slice(pr * LANES, (pr + 1) * LANES)
            do = do_ref[:, ls]
            doh[2 * pr] = jnp.where(first, do, jnp.zeros_like(do))
            doh[2 * pr + 1] = jnp.where(first, jnp.zeros_like(do), do)
            prod_t = (do.astype(F32) * o_ref[:, ls].astype(F32)).T
            delta_row[2 * pr] = jnp.sum(jnp.where(sub < V_DIM, prod_t, 0.0), axis=0, keepdims=True)
            delta_row[2 * pr + 1] = jnp.sum(jnp.where(sub < V_DIM, 0.0, prod_t), axis=0, keepdims=True)
        for hd in range(hb):
            lse_row[hd] = lse_ref[hd] * LOG2E
            kt[hd] = k_ref[:, hd * HEAD_SLOT:(hd + 1) * HEAD_SLOT].T
        dqt[...] = jnp.zeros(dqt.shape, F32)
        heads = range(hb)
        hss = [slice(hd * HEAD_SLOT, (hd + 1) * HEAD_SLOT) for hd in heads]

        def k_block(ks, tk, next_q):
            keep = lax.broadcasted_iota(jnp.int32, (tk, tk), 0) <= lax.broadcasted_iota(jnp.int32, (tk, tk), 1)

            def q_steps(blocks, c, masked):
                work = [(qs, tq, hd) for qs, tq in blocks for hd in heads]
                qhs = [q_ref[pl.ds(qs, tq), hss[hd]] for qs, tq, hd in work]
                dos = [doh[hd, pl.ds(qs, tq), :] for qs, tq, hd in work]
                sts = [_dot_nt(k_ref[pl.ds(ks, tk), hss[hd]], qhs[i]) for i, (_, _, hd) in enumerate(work)]
                dpts = [_dot_nt(v_ref[pl.ds(ks, tk), (hd // 2) * LANES:(hd // 2 + 1) * LANES], dos[i])
                        for i, (_, _, hd) in enumerate(work)]
                pts, dsts = [], []
                for i, (qs, tq, hd) in enumerate(work):
                    st = jnp.where(keep, sts[i], MASK_VALUE) if masked else sts[i]
                    pt = jnp.exp2(st * EXP2_SCALE - lse_row[hd, :, pl.ds(qs, tq)])
                    dsts.append((pt * (dpts[i] - delta_row[hd, :, pl.ds(qs, tq)])).astype(BF16))
                    pts.append(pt.astype(BF16))
                dvs = [_dot(pts[i], dos[i]) for i in range(len(work))]
                dks = [_dot(dsts[i], qhs[i]) for i in range(len(work))]
                dqs = [_dot(kt[hd, :, pl.ds(ks, tk)], dsts[i]) for i, (_, _, hd) in enumerate(work)]
                c = list(c)
                for i, (qs, tq, hd) in enumerate(work):
                    dqt[hd, :, pl.ds(qs, tq)] += dqs[i]
                    c[hd] = (c[hd][0] + dks[i], c[hd][1] + dvs[i])
                return tuple(c)

            zero = jnp.zeros((tk, LANES), F32)
            c = q_steps([(ks, tk)], tuple((zero, zero) for _ in heads), True)
            if next_q is not None:
                def two_blocks(i, c):
                    qs = pl.multiple_of((next_q + 2 * i) * TQ, TQ)
                    return q_steps([(qs, TQ), (qs + TQ, TQ)], c, False)

                pairs = lax.div(nq - next_q, 2)
                c = lax.fori_loop(0, pairs, two_blocks, c)
                c = lax.fori_loop(next_q + 2 * pairs, nq, lambda qi, c: q_steps([(pl.multiple_of(qi * TQ, TQ), TQ)], c, False), c)
                if tail:
                    c = q_steps([(nq * TQ, tail)], c, False)
            for hd in heads:
                dk_ref[pl.ds(ks, tk), hss[hd]] = c[hd][0] * SM_SCALE
            for pr in range(hb // 2):
                dv_ref[pl.ds(ks, tk), pr * LANES:(pr + 1) * LANES] = c[2 * pr][1] + c[2 * pr + 1][1]

        def whole_k_block(kj, carry):
            k_block(pl.multiple_of(kj * TK, TK), TK, kj + 1)
            return carry

        lax.fori_loop(0, nq, whole_k_block, 0)
        if tail:
            k_block(nq * TQ, tail, None)
        for hd in range(hb):
            dq_ref[:, hd * HEAD_SLOT:(hd + 1) * HEAD_SLOT] = dqt[hd].T * SM_SCALE

    blk = lambda w: pl.BlockSpec((lp, w), lambda b, g: (b, g))
    return _carrying_call(
        body, ride, (q, k, v, o, do, lse), name=name, grid=(nb, N_HEADS // hb),
        in_specs=[blk(hb * HEAD_SLOT), blk(hb * HEAD_SLOT), blk(hb * V_DIM), blk(hb * V_DIM), blk(hb * V_DIM),
                  pl.BlockSpec((hb, 1, lp), lambda b, g: (g, 0, b))],
        out_specs=[blk(hb * HEAD_SLOT), blk(hb * HEAD_SLOT), blk(hb * V_DIM)],
        out_shape=[jax.ShapeDtypeStruct((t, QK_WIDTH), F32), jax.ShapeDtypeStruct((t, QK_WIDTH), F32),
                   jax.ShapeDtypeStruct((t, D_MODEL), F32)],
        scratch_shapes=[pltpu.VMEM((hb, HEAD_SLOT, lp), BF16), pltpu.VMEM((hb, lp, LANES), BF16), pltpu.VMEM((hb, 1, lp), F32),
                        pltpu.VMEM((hb, 1, lp), F32), pltpu.VMEM((hb, HEAD_SLOT, lp), F32)])


def in_proj_bwd(dh1, h, g_mix, z, dq, dk, dv, dga, dgb, dpool, win, gq, gkv, wuq, wuk, wuv, rope, *, tm, lp, nb, name):
    t = h.shape[0]
    hb = tm // POOL_HALO
    last_halo = t // POOL_HALO - 1

    def body(dh1_ref, h_ref, g_ref, zcq_ref, zckv_ref, dq_ref, dk_ref, dv_ref, dga_ref, dgb_ref, dpool_ref, dnext_ref,
             win_ref, gq_ref, gkv_ref, wuq_ref, wuk_ref, wuv_ref, rope_ref,
             dh_ref, hn_ref, dz_ref, cqn_ref, ckvn_ref, dqb_ref, dkb_ref, dvb_ref, dg_ref, dgq_ref, dgkv_ref):
        i = pl.program_id(0)
        rope_t = rope_ref[...]
        dqb = _rope_bwd(dq_ref[...], *_rope_tables(rope_t, N_HEADS)).astype(BF16)
        dqb_ref[...] = dqb
        xq, rq = _rms(zcq_ref[...])
        gq_v = gq_ref[...]
        cqn_ref[...] = (xq * gq_v).astype(BF16)
        dcq, dgq = _rms_bwd(_dot_nt(dqb, wuq_ref[...]), xq, rq, gq_v)
        dk = dk_ref[...]
        dkb = dk.astype(BF16)
        dvb = dv_ref[...].astype(BF16)
        dkb_ref[...] = dkb
        dvb_ref[...] = dvb
        xkv, rkv = _rms(zckv_ref[...])
        gkv_v = gkv_ref[...]
        ckvn_ref[...] = (xkv * gkv_v).astype(BF16)
        dckv, dgkv = _rms_bwd(_dot_nt(dkb, wuk_ref[...]) + _dot_nt(dvb, wuv_ref[...]), xkv, rkv, gkv_v)
        dks = dk[:, :HEAD_SLOT]
        for hd in range(1, N_HEADS):
            dks = dks + dk[:, hd * HEAD_SLOT:(hd + 1) * HEAD_SLOT]
        dzk = _rope_bwd(dks, *_rope_tables(rope_t, 1))
        dp_cur = dpool_ref[...]
        dp_ext = jnp.concatenate([dp_cur, dnext_ref[...]], axis=0)
        r = lax.broadcasted_iota(jnp.int32, (tm, tm + POOL_HALO), 0)
        e = lax.broadcasted_iota(jnp.int32, (tm, tm + POOL_HALO), 1)
        gt_col = i * tm + lax.broadcasted_iota(jnp.int32, (1, tm + POOL_HALO), 1)
        pos_col = _seq_pos(gt_col, lp, nb)
        gt_row = i * tm + lax.broadcasted_iota(jnp.int32, (tm + POOL_HALO, 1), 0)
        pos_row = _seq_pos(gt_row, lp, nb)
        dus = []
        for g, w in enumerate(POOL_WINDOWS):
            gs = slice(g * POOL_GROUP, (g + 1) * POOL_GROUP)
            band = jnp.where((e - r >= 0) & (e - r < jnp.minimum(pos_col + 1, w)) & (gt_col < t), 1.0, 0.0).astype(BF16)
            scaled = jnp.where(gt_row < t, dp_ext[:, gs] / jnp.minimum(pos_row + 1, w).astype(F32), 0.0).astype(BF16)
            dus.append(_dot(band, scaled) - dp_cur[:, gs])
        dz = jnp.concatenate(dus + [dcq, dckv, dzk], axis=1).astype(BF16)
        dz = jnp.concatenate([dz, dga_ref[...], dgb_ref[...]], axis=1)
        dz_ref[...] = dz
        xhat, rr = _rms(h_ref[...])
        gg = g_ref[...]
        hn_ref[...] = (xhat * gg).astype(BF16)
        dx, dg = _rms_bwd(_dot_nt(dz, win_ref[...]), xhat, rr, gg)
        dh_ref[...] = dh1_ref[...] + dx

        @pl.when(i == 0)
        def _():
            dg_ref[...] = jnp.zeros_like(dg_ref)
            dgq_ref[...] = jnp.zeros_like(dgq_ref)
            dgkv_ref[...] = jnp.zeros_like(dgkv_ref)

        dg_ref[...] += dg
        dgq_ref[...] += dgq
        dgkv_ref[...] += dgkv

    nxt = pl.BlockSpec((POOL_HALO, POOL_WIDTH), lambda i: (jnp.minimum((i + 1) * hb, last_halo), 0))
    return pl.pallas_call(
        body, name=name, grid=(t // tm,),
        in_specs=[_rows(tm, D_MODEL), _rows(tm, D_MODEL), _whole((1, D_MODEL)), _rows(tm, Q_RANK, Z_CQ // Q_RANK),
                  _rows(tm, KV_RANK, Z_CKV // KV_RANK), _rows(tm, QK_WIDTH), _rows(tm, QK_WIDTH), _rows(tm, D_MODEL),
                  _rows(tm, D_MODEL), _rows(tm, D_MODEL), _rows(tm, POOL_WIDTH), nxt,
                  _whole((D_MODEL, DZ)), _whole((1, Q_RANK)), _whole((1, KV_RANK)), _whole((Q_RANK, QK_WIDTH)),
                  _whole((KV_RANK, QK_WIDTH)), _whole((KV_RANK, D_MODEL)), _rows(tm, 4 * LANES)],
        out_specs=[_rows(tm, D_MODEL), _rows(tm, D_MODEL), _rows(tm, DZ), _rows(tm, Q_RANK), _rows(tm, KV_RANK),
                   _rows(tm, QK_WIDTH), _rows(tm, QK_WIDTH), _rows(tm, D_MODEL),
                   _acc((1, D_MODEL)), _acc((1, Q_RANK)), _acc((1, KV_RANK))],
        out_shape=[jax.ShapeDtypeStruct((t, D_MODEL), F32), jax.ShapeDtypeStruct((t, D_MODEL), BF16),
                   jax.ShapeDtypeStruct((t, DZ), BF16), jax.ShapeDtypeStruct((t, Q_RANK), BF16),
                   jax.ShapeDtypeStruct((t, KV_RANK), BF16), jax.ShapeDtypeStruct((t, QK_WIDTH), BF16),
                   jax.ShapeDtypeStruct((t, QK_WIDTH), BF16), jax.ShapeDtypeStruct((t, D_MODEL), BF16),
                   jax.ShapeDtypeStruct((1, D_MODEL), F32), jax.ShapeDtypeStruct((1, Q_RANK), F32),
                   jax.ShapeDtypeStruct((1, KV_RANK), F32)],
        compiler_params=_cparams(),
    )(dh1, h, g_mix, z, z, dq, dk, dv, dga, dgb, dpool, dpool, win, gq, gkv, wuq, wuk, wuv, rope)


_MESH = pl.DeviceIdType.MESH


def _place():
    x, y, c = lax.axis_index("x"), lax.axis_index("y"), lax.axis_index("c")
    return x, y, c, 4 * x + 2 * y + c


def _peer(x, y, c, k):
    px, py, pc = (1 - x) if k & 4 else x, (1 - y) if k & 2 else y, (1 - c) if k & 1 else c
    return (px, py, pc), 4 * px + 2 * py + pc


ALL_PEERS = tuple(range(1, N_DEV))
CHIP_PEERS = (2, 4, 6)
N_CHIPS = N_DEV // 2


def _sem_scratch(n, m):
    return [pltpu.SemaphoreType.DMA((n, m)), pltpu.SemaphoreType.DMA((n, m)), pltpu.SemaphoreType.DMA((n,))]


class Exchange:
    def __init__(self, arrays, out_shapes, sem_cols, plan, aliased=False):
        self.arrays, self.out_shapes, self.plan = list(arrays), list(out_shapes), plan
        self.scratch = _sem_scratch(len(self.arrays), sem_cols)
        self.aliased = aliased

    def split(self, refs):
        n = len(self.arrays)
        return refs[:n], refs[n:2 * n], refs[2 * n:]

    def start(self, srcs, dsts, sems):
        local, sends, _ = self.plan(srcs, dsts, *sems)
        for cp in local + sends:
            cp.start()

    def wait(self, srcs, dsts, sems):
        local, sends, recvs = self.plan(srcs, dsts, *sems)
        for cp in recvs:
            cp.wait_recv()
        for cp in sends:
            cp.wait_send()
        for cp in local:
            cp.wait()

    def aliases(self, first_in, first_out):
        return {first_in + j: first_out + j for j in range(len(self.arrays))} if self.aliased else {}

    def run(self, name):
        def body(*refs):
            srcs, dsts, sems = self.split(refs)
            self.start(srcs, dsts, sems)
            self.wait(srcs, dsts, sems)

        n = len(self.arrays)
        return pl.pallas_call(body, name=name, in_specs=[_ANY] * n, out_specs=[_ANY] * n, out_shape=self.out_shapes,
                              scratch_shapes=self.scratch, input_output_aliases=self.aliases(0, 0))(*self.arrays)


def exchange(arrays, scatter, peers, by_chip=False):
    slots = N_CHIPS if by_chip else N_DEV

    def plan(srcs, dsts, send_sems, recv_sems, local_sems):
        x, y, c, me = _place()
        mine = 2 * x + y if by_chip else me
        local = [pltpu.make_async_copy(src.at[mine] if scatter else src, dst.at[mine], local_sems.at[j])
                 for j, (src, dst) in enumerate(zip(srcs, dsts))]
        sends, recvs = [], []
        for t, k in enumerate(peers):
            peer, pidx = _peer(x, y, c, k)
            theirs = 2 * peer[0] + peer[1] if by_chip else pidx
            for j, (src, dst) in enumerate(zip(srcs, dsts)):
                part = src.at[theirs] if scatter else src
                sems = dict(send_sem=send_sems.at[j, t], recv_sem=recv_sems.at[j, t], device_id=peer, device_id_type=_MESH)
                sends.append(pltpu.make_async_remote_copy(src_ref=part, dst_ref=dst.at[mine], **sems))
                recvs.append(pltpu.make_async_remote_copy(src_ref=part, dst_ref=dst.at[theirs], **sems))
        return local, sends, recvs

    shapes = [jax.ShapeDtypeStruct(a.shape if scatter else (slots,) + a.shape, a.dtype) for a in arrays]
    return Exchange(arrays, shapes, len(peers), plan)


def second_hop(gathered):
    def plan(srcs, dsts, send_sems, recv_sems, local_sems):
        x, y, c, me = _place()
        sibling, _ = _peer(x, y, c, 1)
        sends, recvs = [], []
        for t, k in enumerate(CHIP_PEERS):
            _, landed = _peer(x, y, c, k)
            _, coming = _peer(x, y, c, k ^ 1)
            for j, buf in enumerate(dsts):
                sems = dict(send_sem=send_sems.at[j, t], recv_sem=recv_sems.at[j, t], device_id=sibling, device_id_type=_MESH)
                sends.append(pltpu.make_async_remote_copy(src_ref=buf.at[landed], dst_ref=buf.at[landed], **sems))
                recvs.append(pltpu.make_async_remote_copy(src_ref=buf.at[coming], dst_ref=buf.at[coming], **sems))
        return [], sends, recvs

    shapes = [jax.ShapeDtypeStruct(a.shape, a.dtype) for a in gathered]
    return Exchange(gathered, shapes, len(CHIP_PEERS), plan, aliased=True)


FIRST_HOP_PEERS = (1,) + CHIP_PEERS


def _gather_two_level(arrays, name):
    n = len(arrays)

    def body(*refs):
        srcs, dsts, (send_sems, recv_sems, local_sems) = refs[:n], refs[n:2 * n], refs[2 * n:]
        x, y, c, me = _place()
        sibling, sidx = _peer(x, y, c, 1)

        def copy(j, sem, block, to, src=None):
            rows = dsts[j].at[block]
            return pltpu.make_async_remote_copy(src_ref=rows if src is None else src, dst_ref=rows, send_sem=send_sems.at[j, sem],
                                                recv_sem=recv_sems.at[j, sem], device_id=to, device_id_type=_MESH)

        local = [pltpu.make_async_copy(srcs[j], dsts[j].at[me], local_sems.at[j]) for j in range(n)]
        for cp in local:
            cp.start()
        first = [copy(j, 1 + t, me, _peer(x, y, c, k)[0], src=srcs[j]) for t, k in enumerate(CHIP_PEERS) for j in range(n)]
        first += [copy(j, 0, me, sibling, src=srcs[j]) for j in range(n)]
        for cp in first:
            cp.start()
        passed = []
        for t, k in enumerate(CHIP_PEERS):
            peer, pidx = _peer(x, y, c, k)
            for j in range(n):
                copy(j, 1 + t, pidx, peer).wait_recv()
                passed.append(copy(j, 4 + t, pidx, sibling))
                passed[-1].start()
        for j in range(n):
            copy(j, 0, sidx, sibling).wait_recv()
        for t, k in enumerate(CHIP_PEERS):
            _, pidx = _peer(x, y, c, k ^ 1)
            for j in range(n):
                copy(j, 4 + t, pidx, sibling).wait_recv()
        for cp in first + passed:
            cp.wait_send()
        for cp in local:
            cp.wait()

    shapes = [jax.ShapeDtypeStruct((N_DEV,) + a.shape, a.dtype) for a in arrays]
    return pl.pallas_call(body, name=name, in_specs=[_ANY] * n, out_specs=[_ANY] * n, out_shape=shapes,
                          scratch_shapes=_sem_scratch(n, 1 + 2 * len(CHIP_PEERS)))(*arrays)


def to_sibling(arrays):
    def plan(srcs, dsts, send_sems, recv_sems, local_sems):
        x, y, c, _ = _place()
        sibling, _ = _peer(x, y, c, 1)
        copies = [pltpu.make_async_remote_copy(src_ref=src.at[1 - c], dst_ref=dst, send_sem=send_sems.at[j, 0],
                                               recv_sem=recv_sems.at[j, 0], device_id=sibling, device_id_type=_MESH)
                  for j, (src, dst) in enumerate(zip(srcs, dsts))]
        return [], copies, copies

    return Exchange(arrays, [jax.ShapeDtypeStruct(a.shape[1:], a.dtype) for a in arrays], 1, plan)


def combine(a, b):
    assert not (a.aliased or b.aliased)
    na, nsem = len(a.arrays), len(a.scratch)

    def plan(srcs, dsts, *sems):
        return tuple(u + v for u, v in zip(a.plan(srcs[:na], dsts[:na], *sems[:nsem]), b.plan(srcs[na:], dsts[na:], *sems[nsem:])))

    both = Exchange(a.arrays + b.arrays, a.out_shapes + b.out_shapes, 1, plan)
    both.scratch = a.scratch + b.scratch
    return both


def pair_add(own, theirs, core, *, name):
    _, ns, r, c = own.shape
    rb = _row_block(r, c // 2)

    def body(core_ref, a_ref, b_ref, o_ref):
        o_ref[...] = (a_ref[...].astype(F32) + b_ref[...].astype(F32)).astype(o_ref.dtype)

    return pl.pallas_call(
        body, name=name,
        grid_spec=pltpu.PrefetchScalarGridSpec(
            num_scalar_prefetch=1, grid=(ns, r // rb),
            in_specs=[pl.BlockSpec((None, None, rb, c), lambda i, j, core_ref: (core_ref[0], i, j, 0)),
                      pl.BlockSpec((None, rb, c), lambda i, j, core_ref: (i, j, 0))],
            out_specs=pl.BlockSpec((None, rb, c), lambda i, j, core_ref: (i, j, 0))),
        out_shape=jax.ShapeDtypeStruct((ns, r, c), own.dtype), compiler_params=_cparams(),
    )(core, own, theirs)


ADAMW_BLOCK_BYTES = 1 << 20


def _row_block(r, c):
    for rb in range(r, 0, -1):
        if r % rb == 0 and (rb % 16 == 0 or rb == r) and rb * c * 4 <= ADAMW_BLOCK_BYTES:
            return rb
    return r


def adamw(w, m, v, parts, *, name):
    depth, r, c = w.shape
    n_parts = parts[0].shape[0]
    rb = _row_block(r, c)

    def body(w_ref, m_ref, v_ref, *refs):
        p_refs, (g_ref, d_ref, nm_ref, nv_ref) = refs[:depth], refs[depth:]

        def total(p_ref):
            g = p_ref[0].astype(F32)
            for j in range(1, n_parts):
                g = g + p_ref[j].astype(F32)
            return g

        g = total(p_refs[0])
        for l in range(1, depth):
            g = jnp.where(pl.program_id(0) == l, total(p_refs[l]), g)
        g_ref[...] = g
        m_new = ADAM_B1 * m_ref[...] + (1.0 - ADAM_B1) * g
        v_new = ADAM_B2 * v_ref[...] + (1.0 - ADAM_B2) * (g * g)
        m_hat = m_new / (1.0 - ADAM_B1 ** ADAM_STEP)
        v_hat = v_new / (1.0 - ADAM_B2 ** ADAM_STEP)
        d_ref[...] = -ADAM_LR * (m_hat / (jnp.sqrt(v_hat) + ADAM_EPS) + ADAM_WD * w_ref[...])
        nm_ref[...] = m_new
        nv_ref[...] = v_new

    wblk = pl.BlockSpec((None, rb, c), lambda l, i: (l, i, 0))
    pblk = pl.BlockSpec((n_parts, rb, c), lambda l, i: (0, i, 0))
    return pl.pallas_call(
        body, name=name, grid=(depth, r // rb),
        in_specs=[wblk, wblk, wblk] + [pblk] * depth, out_specs=[wblk] * 4,
        out_shape=[jax.ShapeDtypeStruct((depth, r, c), F32)] * 4, compiler_params=_cparams(),
    )(w, m, v, *parts)


BIG = (("w_in", 2), ("w_uq", 2), ("w_ukv", 2), ("w_pa", 2), ("w_pb", 1), ("w_o", 1), ("w_gate", 2), ("w_up", 2), ("w_down", 1))
SMALL = ("norm_mix_g", "pool_w", "pool_scale", "q_norm_g", "kv_norm_g", "norm_ffn_g", "final_norm_g")
WEIGHTS = ("meta_tokens", "norm_mix_g", "w_in", "pool_w", "pool_scale", "q_norm_g", "kv_norm_g", "w_uq", "w_ukv", "w_pa", "w_pb",
           "w_o", "norm_ffn_g", "w_gate", "w_up", "w_down", "final_norm_g")
HEAD_QK = QK_NOPE + QK_ROPE
KR_END = Z_KR + QK_ROPE


def _cat_cols(parts):
    return [jnp.concatenate(parts, axis=1)]


def _cat_rows(parts):
    return [jnp.concatenate(parts, axis=0)]


def _arr_w_in(parts):
    full = jnp.concatenate(parts, axis=1)
    zc = lambda n: jnp.zeros((full.shape[0], n), full.dtype)
    return [jnp.concatenate([full[:, :Z_KR], zc(QK_NOPE), full[:, Z_KR:KR_END], zc(LANES - HEAD_QK), full[:, KR_END:]], axis=1)]


def _arr_w_uq(parts):
    full = jnp.concatenate(parts, axis=1)
    z = jnp.zeros((full.shape[0], HEAD_SLOT - HEAD_QK), full.dtype)
    pieces = []
    for hd in range(N_HEADS):
        pieces += [full[:, hd * HEAD_QK:(hd + 1) * HEAD_QK], z]
    return [jnp.concatenate(pieces, axis=1)]


def _arr_w_ukv(parts):
    full = jnp.concatenate(parts, axis=1)
    z = jnp.zeros((full.shape[0], HEAD_SLOT - QK_NOPE), full.dtype)
    wide = QK_NOPE + V_DIM
    k, v = [], []
    for hd in range(N_HEADS):
        k += [full[:, hd * wide:hd * wide + QK_NOPE], z]
        v.append(full[:, hd * wide + QK_NOPE:(hd + 1) * wide])
    return [jnp.concatenate(k, axis=1), jnp.concatenate(v, axis=1)]


def arrange(g, fn, out_shapes, name):
    def body(g_ref, *o_refs):
        for o_ref, val in zip(o_refs, fn([g_ref[p] for p in range(N_DEV)])):
            o_ref[...] = val

    return pl.pallas_call(
        body, name=name, grid=(1,),
        in_specs=[pl.BlockSpec(g.shape, lambda i: (0, 0, 0))],
        out_specs=[pl.BlockSpec(s, lambda i: (0, 0)) for s in out_shapes],
        out_shape=[jax.ShapeDtypeStruct(s, g.dtype) for s in out_shapes], compiler_params=_cparams(),
    )(g)


def _arranged_ranges(lo, hi):
    out = []
    for a, b, shift in ((0, Z_KR, 0), (Z_KR, KR_END, QK_NOPE), (KR_END, D_IN, LANES - QK_ROPE)):
        s, e = max(lo, a), min(hi, b)
        if s < e:
            out.append((s + shift, e + shift))
    return out


def _chunks_w_in(acc):
    cs = D_IN // N_DEV
    return [jnp.concatenate([acc[:, a:b] for a, b in _arranged_ranges(p * cs, (p + 1) * cs)], axis=1) for p in range(N_DEV)]


def _chunks_w_uq(acc):
    per = N_HEADS // N_DEV
    return [jnp.concatenate([acc[:, hd * HEAD_SLOT:hd * HEAD_SLOT + HEAD_QK] for hd in range(p * per, (p + 1) * per)], axis=1)
            for p in range(N_DEV)]


def _chunks_w_ukv(acc_k, acc_v):
    per = N_HEADS // N_DEV
    out = []
    for p in range(N_DEV):
        pieces = []
        for hd in range(p * per, (p + 1) * per):
            pieces += [acc_k[:, hd * HEAD_SLOT:hd * HEAD_SLOT + QK_NOPE], acc_v[:, hd * V_DIM:(hd + 1) * V_DIM]]
        out.append(jnp.concatenate(pieces, axis=1))
    return out


def _chunks_cols(acc):
    cs = acc.shape[1] // N_DEV
    return [acc[:, p * cs:(p + 1) * cs] for p in range(N_DEV)]


def _chunks_rows(acc):
    rs = acc.shape[0] // N_DEV
    return [acc[p * rs:(p + 1) * rs, :] for p in range(N_DEV)]


def _chunks_cols_transposed(acc):
    at = acc[...].T
    rs = at.shape[0] // N_DEV
    return [at[p * rs:(p + 1) * rs, :] for p in range(N_DEV)]


def _pack(parts, row_multiple):
    flat = jnp.concatenate([p.reshape(-1) for p in parts])
    return jnp.pad(flat, (0, -flat.shape[0] % (row_multiple * LANES))).reshape(-1, LANES)


def _unpack(packed, shapes):
    flat, out, off = packed.reshape(-1), [], 0
    for s in shapes:
        n = 1
        for d in s:
            n *= d
        out.append(flat[off:off + n].reshape(s))
        off += n
    return out


def _rope_table(lp, nb):
    inv = 1.0 / (ROPE_THETA ** (jnp.arange(0, QK_ROPE, 2, dtype=F32) / QK_ROPE))
    ang = jnp.arange(lp, dtype=F32)[:, None] * inv[None, :]
    cos, sin = jnp.cos(ang), jnp.sin(ang)
    z = lambda n: jnp.zeros((lp, n), F32)
    tail = LANES - QK_NOPE - QK_ROPE
    c = jnp.concatenate([jnp.ones((lp, QK_NOPE), F32), cos, cos, z(tail)], axis=1)
    cr = jnp.concatenate([z(QK_NOPE), cos, cos, z(tail)], axis=1)
    s1 = jnp.concatenate([z(QK_NOPE), -sin, z(HALF_ROPE), z(tail)], axis=1)
    s2 = jnp.concatenate([z(QK_NOPE), z(HALF_ROPE), sin, z(tail)], axis=1)
    return jnp.tile(jnp.concatenate([c, cr, s1, s2], axis=1), (nb, 1))


MIX_IN, MIX_OUT = ("w_in", "w_uq", "w_ukv"), ("w_pa", "w_pb", "w_o")
MIX = MIX_IN + MIX_OUT
FFN = ("w_gate", "w_up", "w_down")
TRANSPOSED = ("w_gate", "w_up")
ARRANGERS = {
    "w_in": (_arr_w_in, (("win", (D_MODEL, DZ)),)), "w_uq": (_arr_w_uq, (("wuq", (Q_RANK, QK_WIDTH)),)),
    "w_ukv": (_arr_w_ukv, (("wuk", (KV_RANK, QK_WIDTH)), ("wuv", (KV_RANK, D_MODEL)))),
    "w_pa": (_cat_cols, (("wpa", (POOL_WIDTH, D_MODEL)),)), "w_pb": (_cat_rows, (("wpb", (D_MODEL, D_MODEL)),)),
    "w_o": (_cat_rows, (("wo", (D_MODEL, D_MODEL)),)), "w_gate": (_cat_rows, (("wgt", (D_FF, D_MODEL)),)),
    "w_up": (_cat_rows, (("wut", (D_FF, D_MODEL)),)), "w_down": (_cat_rows, (("wd", (D_FF, D_MODEL)),)),
}


def _operands(gathered, names, l):
    p = {}
    for n in names:
        fn, outs = ARRANGERS[n]
        if fn is _cat_rows:
            p[outs[0][0]] = gathered[n].reshape(outs[0][1])
            continue
        for (key, _), a in zip(outs, arrange(gathered[n], fn, [s for _, s in outs], f"arrange_{n}_{l}")):
            p[key] = a
    return p


def _small_operands(small, l):
    pw = small["pool_w"][l].astype(BF16)
    return dict(g_mix=small["norm_mix_g"][l][None], gq=small["q_norm_g"][l][None], gkv=small["kv_norm_g"][l][None],
                g_ffn=small["norm_ffn_g"][l][None], ps=small["pool_scale"][l][None], pw=pw)


class MeshComm:
    def __init__(self, w, meta_tokens):
        self.src = lambda n, l: w[n][l].astype(BF16)
        self.meta_tokens = meta_tokens
        self.core = lax.axis_index("c").astype(jnp.int32).reshape(1)
        self.rides = {0: [(n, 0) for n in FFN] + [(n, 1) for n in MIX], 1: [(n, 1) for n in FFN]}

    def first_weights(self):
        got = _gather_two_level([self.src(n, 0) for n in MIX_IN] + [self.meta_tokens], "gather_mix_0")
        return dict(zip(MIX_IN, got)), jnp.moveaxis(got[-1], 0, 1).reshape(N_META, D_MODEL)

    def early_first_hop(self):
        return exchange([self.src(n, 0) for n in MIX_OUT], False, FIRST_HOP_PEERS)

    def early_weights(self, landed):
        return dict(zip(MIX_OUT, second_hop(landed).run("second_hop_mix_0")))

    def first_hop(self, l):
        return exchange([self.src(n, layer) for n, layer in self.rides[l]], False, FIRST_HOP_PEERS)

    def second_hop(self, l, landed):
        return second_hop(landed)

    def carried(self, l, full, names, layer):
        return {n: full[self.rides[l].index((n, layer))] for n in names}

    def pair_exchange(self, own):
        return to_sibling(own)

    def pair_add(self, own, theirs, names, tag):
        return [pair_add(a, b, self.core, name=f"pair_add_{n}_{tag}") for n, a, b in zip(names, own, theirs)]

    def last_pair_exchange(self, own, small):
        got = combine(to_sibling(own), exchange([small], False, ALL_PEERS)).run("pair_grads_mix_0")
        return got[:-1], got[-1]

    def scatter(self, sums):
        return exchange(sums, True, CHIP_PEERS, by_chip=True)

    def scatter_now(self, sums, name):
        return self.scatter(sums).run(name)


HEADS_FWD, HEADS_BWD = 8, 4
TILE_ROWS, TILE_ROWS_BWD = 512, 256


def _tile(t, target):
    n = max(1, -(-t // (target + target // 8)))
    while t % n or (t // n) % 16:
        n += 1
    return t // n


def _wgrad_tile(t):
    return max(tm for tm in (2 * TQ, TQ, LANES) if t % tm == 0)


def _ffn_bwd_part(dh2, p, s, tag, ride):
    d, ff = D_MODEL, D_FF // N_DEV
    t = dh2.shape[0]
    wg_ = lambda n, x, ys, fn, shape: wgrad(x, ys, fn, shape, tm=_wgrad_tile(t), name=f"wgrad_{n}_{tag}")
    (dh1, hn2, act, dgt, dup, dg_ffn), brought = ffn_bwd(dh2, s["h1"], p["g_ffn"], s["gt"], s["up"], p["wgt"], p["wut"], p["wd"],
                                                         tm=_tile(t, TILE_ROWS_BWD), name=f"ffn_bwd_{tag}", ride=ride)
    chunks = [wg_("gate", hn2, [dgt], _chunks_cols_transposed, (ff, d)), wg_("up", hn2, [dup], _chunks_cols_transposed, (ff, d)),
              wg_("down", act, [dh2], _chunks_rows, (ff, d))]
    return dh1, chunks, dict(norm_ffn_g=dg_ffn[0]), brought


def _mix_bwd_part(dh1, p, s, rope, nb, lp, tag, ride, next_ride):
    d = D_MODEL
    t = dh1.shape[0]
    wg_ = lambda n, x, ys, fn, shape: wgrad(x, ys, fn, shape, tm=_wgrad_tile(t), name=f"wgrad_{n}_{tag}")
    (dga, dgb, dpa, dpb, do, dpool, dps, dpw), first = merge_bwd(dh1, s["z"], s["pa"], s["pb"], s["pooled"], p["pw"], p["ps"],
                                                                   p["wpa"], p["wpb"], p["wo"], tm=_tile(t, TILE_ROWS),
                                                                   name=f"merge_bwd_{tag}", ride=ride)
    c_o = wg_("o", s["mg"], [dh1], _chunks_rows, (d // N_DEV, d))
    c_pa = wg_("pa", s["a"], [dpa], _chunks_cols, (POOL_WIDTH, d // N_DEV))
    c_pb = wg_("pb", s["o"], [dpb], _chunks_rows, (d // N_DEV, d))
    (dq, dk, dv), brought = attn_bwd(s["q"], s["k"], s["v"], s["o"], do, s["lse"], nb=nb, lp=lp, hb=HEADS_BWD,
                                     name=f"attn_bwd_{tag}", ride=next_ride(first))
    dh, hn, dz, cqn, ckvn, dqb, dkb, dvb, dg_mix, dgq, dgkv = in_proj_bwd(
        dh1, s["h"], p["g_mix"], s["z"], dq, dk, dv, dga, dgb, dpool, p["win"], p["gq"], p["gkv"], p["wuq"], p["wuk"], p["wuv"],
        rope, tm=_tile(t, TILE_ROWS_BWD), lp=lp, nb=nb, name=f"in_proj_bwd_{tag}")
    c_in = wg_("in", hn, [dz], _chunks_w_in, (d, D_IN // N_DEV))
    c_uq = wg_("uq", cqn, [dqb], _chunks_w_uq, (Q_RANK, N_HEADS * HEAD_QK // N_DEV))
    c_ukv = wg_("ukv", ckvn, [dkb, dvb], _chunks_w_ukv, (KV_RANK, N_HEADS * (QK_NOPE + V_DIM) // N_DEV))
    small = dict(pool_scale=dps[0], pool_w=dpw, norm_mix_g=dg_mix[0], q_norm_g=dgq[0], kv_norm_g=dgkv[0])
    return dh, [c_in, c_uq, c_ukv, c_pa, c_pb, c_o], small, brought


def train_step(x, loss_target, small, comm):
    nb, seq, d = x.shape
    lp = -(-(N_META + seq) // LANES) * LANES
    t = nb * lp
    assert nb <= 2 and DEPTH == 2
    tm = _tile(t, TILE_ROWS)
    rope = _rope_table(lp, nb)
    gathered, meta = comm.first_weights()
    pad = jnp.zeros((nb, lp - N_META - seq, d), F32)
    h = jnp.concatenate([jnp.broadcast_to(meta[None], (nb, N_META, d)), x, pad], axis=1).reshape(t, d)
    target = jnp.concatenate([jnp.zeros((nb, N_META, d), F32), loss_target, pad], axis=1).reshape(t, d)

    params, saved, full = [], [], {}
    for l in range(DEPTH):
        p = _small_operands(small, l)
        p.update(_operands(gathered, MIX_IN, 0) if l == 0 else _operands(comm.carried(0, full[0], MIX, 1), MIX, 1))
        (z, q, k, v), early = in_proj_fwd(h, p["g_mix"], p["win"], p["gq"], p["gkv"], p["wuq"], p["wuk"], p["wuv"], rope, tm=tm,
                                          name=f"in_proj_fwd_{l}", ride=comm.early_first_hop() if l == 0 else None)
        (o, lse), landed = attn_fwd(q, k, v, nb=nb, lp=lp, hb=HEADS_FWD, name=f"attn_fwd_{l}", ride=comm.first_hop(l))
        if l == 0:
            p.update(_operands(comm.early_weights(early), MIX_OUT, 0))
        (h1, pooled, a, pa, pb, mg), full[l] = merge_fwd(h, z, o, p["pw"], p["ps"], p["wpa"], p["wpb"], p["wo"], tm=tm, lp=lp,
                                                          nb=nb, name=f"merge_fwd_{l}", ride=comm.second_hop(l, landed))
        p.update(_operands(comm.carried(l, full[l], FFN, l), FFN, l))
        h2, gt, up = ffn_fwd(h1, p["g_ffn"], p["wgt"], p["wut"], p["wd"], tm=tm, name=f"ffn_fwd_{l}")
        params.append(p)
        saved.append(dict(h=h, z=z, q=q, k=k, v=v, o=o, lse=lse, h1=h1, pooled=pooled, a=a, pa=pa, pb=pb, mg=mg, gt=gt, up=up))
        h = h2
    parts, dh, dgf = loss_head(h, small["final_norm_g"][None], target, tm=tm, lp=lp, nb=nb, seq=seq, name="loss_head")
    loss = jnp.sum(parts[::8, 0])

    sums = {}
    dh, c_ffn1, small1, _ = _ffn_bwd_part(dh, params[1], saved[1], 1, None)
    dh, c_mix1, sm, brought = _mix_bwd_part(
        dh, params[1], saved[1], rope, nb, lp, 1, comm.pair_exchange(c_ffn1),
        lambda theirs: comm.scatter(comm.pair_add(c_ffn1, theirs, FFN, "ffn_1")))
    small1.update(sm)
    sums.update({(n, 1): a for n, a in zip(FFN, brought)})
    dh, c_ffn0, small0, theirs = _ffn_bwd_part(dh, params[0], saved[0], 0, comm.pair_exchange(c_mix1))
    s_mix1 = comm.pair_add(c_mix1, theirs, MIX, "mix_1")
    dh, c_mix0, sm, brought = _mix_bwd_part(
        dh, params[0], saved[0], rope, nb, lp, 0, comm.pair_exchange(c_ffn0),
        lambda theirs: comm.scatter(s_mix1 + comm.pair_add(c_ffn0, theirs, FFN, "ffn_0")))
    small0.update(sm)
    sums.update({(n, l): a for (n, l), a in zip([(n, 1) for n in MIX] + [(n, 0) for n in FFN], brought)})
    dh = dh.reshape(nb, lp, d)
    dmeta = jnp.sum(dh[:, :N_META], axis=0)
    meta_chunks = jnp.transpose(dmeta.reshape(N_META, N_CHIPS, 2, d // N_DEV), (2, 1, 0, 3)).astype(BF16)
    small_grads = {n: jnp.stack([small0[n], small1[n]]) for n in small0}
    small_grads["final_norm_g"] = dgf[0]
    last_names = MIX + ("meta_tokens",)
    theirs, small_parts = comm.last_pair_exchange(c_mix0 + [meta_chunks], _pack([small_grads[n] for n in SMALL], 8))
    last = comm.scatter_now(comm.pair_add(c_mix0 + [meta_chunks], theirs, last_names, "mix_0"), "scatter_mix_0")
    sums.update({(n, 0): a for n, a in zip(last_names, last)})
    return loss, dh[:, N_META:N_META + seq], sums, small_grads, small_parts


def kernel(x, meta_tokens, norm_mix_g, w_in, pool_w, pool_scale, q_norm_g, kv_norm_g, w_uq, w_ukv, w_pa, w_pb, w_o, norm_ffn_g, w_gate, w_up, w_down, final_norm_g, loss_target, m_meta_tokens, m_norm_mix_g, m_w_in, m_pool_w, m_pool_scale, m_q_norm_g, m_kv_norm_g, m_w_uq, m_w_ukv, m_w_pa, m_w_pb, m_w_o, m_norm_ffn_g, m_w_gate, m_w_up, m_w_down, m_final_norm_g, v_meta_tokens, v_norm_mix_g, v_w_in, v_pool_w, v_pool_scale, v_q_norm_g, v_kv_norm_g, v_w_uq, v_w_ukv, v_w_pa, v_w_pb, v_w_o, v_norm_ffn_g, v_w_gate, v_w_up, v_w_down, v_final_norm_g):
    args = dict(locals())
    w = {n: args[n] for n in WEIGHTS}
    m = {n: args["m_" + n] for n in WEIGHTS}
    v = {n: args["v_" + n] for n in WEIGHTS}
    small = {n: w[n] for n in SMALL}
    as_handled = lambda a, n: jnp.swapaxes(a, 1, 2) if n in TRANSPOSED else a
    wh, mh, vh = ({n: as_handled(d[n], n) for n, _ in BIG} for d in (w, m, v))

    loss, grad_x, sums, _, small_recv = train_step(x, loss_target, small, MeshComm(wh, meta_tokens))
    loss = lax.psum(loss, ("x", "y", "c"))

    out = {n: [as_handled(a, n) for a in adamw(wh[n], mh[n], vh[n], [sums[(n, l)] for l in range(DEPTH)], name=f"adamw_{n}")]
           for n, _ in BIG}
    out["meta_tokens"] = [a[0] for a in adamw(meta_tokens[None], m["meta_tokens"][None], v["meta_tokens"][None],
                                              [sums[("meta_tokens", 0)]], name="adamw_meta_tokens")]
    pk = lambda d: _pack([d[n] for n in SMALL], 8)[None]
    packed = adamw(pk(w), pk(m), pk(v), [small_recv], name="adamw_small")
    shapes = [w[n].shape for n in SMALL]
    for n, *kinds in zip(SMALL, *[_unpack(packed[kind][0], shapes) for kind in range(4)]):
        out[n] = kinds
    return (loss, grad_x, *[out[n][kind] for kind in range(4) for n in WEIGHTS])
```

```python
import functools
import math

import jax
import jax.numpy as jnp
from jax import lax
from jax.experimental import pallas as pl
from jax.experimental.pallas import tpu as pltpu

F32, BF16 = jnp.float32, jnp.bfloat16

D_MODEL = 1024
N_META = 16
N_HEADS = 16
QK_NOPE, QK_ROPE, V_DIM = 64, 32, 64
HALF_ROPE = QK_ROPE // 2
Q_RANK, KV_RANK = 256, 128
POOL_WINDOWS = (2, 4, 8, 16)
POOL_GROUP = 128
POOL_WIDTH = POOL_GROUP * len(POOL_WINDOWS)
POOL_HALO = 16
D_FF = 2816
D_IN = 2976
NORM_EPS = 1e-6
SM_SCALE = (QK_NOPE + QK_ROPE) ** -0.5
LOG2E = math.log2(math.e)
EXP2_SCALE = SM_SCALE * LOG2E
MASK_VALUE = -1e30
ROPE_THETA = 10000.0
DEPTH = 2
N_DEV = 8

ADAM_LR, ADAM_B1, ADAM_B2, ADAM_EPS, ADAM_WD, ADAM_STEP = 0.001, 0.9, 0.999, 1e-08, 0.01, 10

LANES = 128
HEAD_SLOT = LANES
QK_WIDTH = N_HEADS * HEAD_SLOT
Z_CQ, Z_CKV, Z_KR, Z_GA, Z_GB, DZ = 512, 768, 896, 1024, 2048, 3072
TQ = TK = 256
VMEM_LIMIT = 56 * 1024 * 1024


def _cparams():
    return pltpu.CompilerParams(vmem_limit_bytes=VMEM_LIMIT)


def _rows(tm, width, col=0):
    return pl.BlockSpec((tm, width), lambda i: (i, col))


def _whole(shape):
    zeros = (0,) * len(shape)
    return pl.BlockSpec(shape, lambda i: zeros, pipeline_mode=pl.Buffered(1))


def _acc(shape):
    zeros = (0,) * len(shape)
    return pl.BlockSpec(shape, lambda i: zeros)


def _dot(a, b):
    return jnp.dot(a, b, preferred_element_type=F32)


def _dot_tn(a, b):
    return lax.dot_general(a, b, (((0,), (0,)), ((), ())), preferred_element_type=F32)


def _dot_nt(a, b):
    return lax.dot_general(a, b, (((1,), (1,)), ((), ())), preferred_element_type=F32)


def _rms(x):
    r = lax.rsqrt(jnp.mean(x * x, axis=-1, keepdims=True) + NORM_EPS)
    return x * r, r


def _rms_bwd(dy, xhat, r, g):
    dg = jnp.sum(dy * xhat, axis=0, keepdims=True)
    dxh = dy * g
    dx = r * (dxh - xhat * jnp.mean(dxh * xhat, axis=-1, keepdims=True))
    return dx, dg


def _sigmoid(x):
    return 1.0 / (1.0 + jnp.exp(-x))


def _rope_fwd(q, c, s1, s2):
    w = q.shape[1]
    return q * c + pltpu.roll(q, w - HALF_ROPE, 1) * s1 + pltpu.roll(q, HALF_ROPE, 1) * s2


def _rope_bwd(dq, c, s1, s2):
    w = dq.shape[1]
    return dq * c + pltpu.roll(dq * s1, HALF_ROPE, 1) + pltpu.roll(dq * s2, w - HALF_ROPE, 1)


def _rope_tables(rope, reps):
    c, cr, s1, s2 = (rope[:, k * LANES:(k + 1) * LANES] for k in range(4))
    if reps > 1:
        return jnp.tile(c, (1, reps)), jnp.tile(s1, (1, reps)), jnp.tile(s2, (1, reps))
    return cr, s1, s2


def _seq_pos(gi, lp, nb):
    pos = gi
    for b in range(1, nb):
        pos = jnp.where(gi >= b * lp, gi - b * lp, pos)
    return pos


_ANY = pl.BlockSpec(memory_space=pl.ANY)


def _carrying_call(body, ride, operands, *, name, grid, in_specs, out_specs, out_shape, scratch_shapes=()):
    n_in, n_out = len(in_specs), len(out_specs)
    if ride is None:
        out = pl.pallas_call(body, name=name, grid=grid, in_specs=in_specs, out_specs=out_specs, out_shape=out_shape,
                             scratch_shapes=list(scratch_shapes), compiler_params=_cparams())(*operands)
        return out, []
    ne = len(ride.arrays)

    def carrying(*refs):
        ins, r_in, rest = refs[:n_in], refs[n_in:n_in + ne], refs[n_in + ne:]
        outs, r_out, rest = rest[:n_out], rest[n_out:n_out + ne], rest[n_out + ne:]
        scratch, sems = rest[:len(scratch_shapes)], rest[len(scratch_shapes):]
        ids = [pl.program_id(a) for a in range(len(grid))]
        first = functools.reduce(jnp.logical_and, [i == 0 for i in ids])
        last = functools.reduce(jnp.logical_and, [i == g - 1 for i, g in zip(ids, grid)])

        @pl.when(first)
        def _():
            ride.start(r_in, r_out, sems)

        body(*ins, *outs, *scratch)

        @pl.when(last)
        def _():
            ride.wait(r_in, r_out, sems)

    out = pl.pallas_call(
        carrying, name=name, grid=grid, in_specs=list(in_specs) + [_ANY] * ne, out_specs=list(out_specs) + [_ANY] * ne,
        out_shape=list(out_shape) + ride.out_shapes, scratch_shapes=list(scratch_shapes) + ride.scratch,
        input_output_aliases=ride.aliases(n_in, n_out), compiler_params=_cparams(),
    )(*operands, *ride.arrays)
    return out[:n_out], out[n_out:]


def in_proj_fwd(h, g_mix, win, gq, gkv, wuq, wuk, wuv, rope, *, tm, name, ride=None):
    t = h.shape[0]

    def body(h_ref, g_ref, win_ref, gq_ref, gkv_ref, wuq_ref, wuk_ref, wuv_ref, rope_ref, z_ref, q_ref, k_ref, v_ref):
        xhat, _ = _rms(h_ref[...])
        hn = (xhat * g_ref[...]).astype(BF16)
        z = _dot(hn, win_ref[...])
        z_ref[...] = z
        rope_t = rope_ref[...]
        xq, _ = _rms(z[:, Z_CQ:Z_CKV])
        cqn = (xq * gq_ref[...]).astype(BF16)
        q = _rope_fwd(_dot(cqn, wuq_ref[...]), *_rope_tables(rope_t, N_HEADS))
        q_ref[...] = q.astype(BF16)
        xkv, _ = _rms(z[:, Z_CKV:Z_KR])
        ckvn = (xkv * gkv_ref[...]).astype(BF16)
        kr = _rope_fwd(z[:, Z_KR:Z_GA], *_rope_tables(rope_t, 1))
        k_ref[...] = (_dot(ckvn, wuk_ref[...]) + jnp.tile(kr, (1, N_HEADS))).astype(BF16)
        v_ref[...] = _dot(ckvn, wuv_ref[...]).astype(BF16)

    return _carrying_call(
        body, ride, (h, g_mix, win, gq, gkv, wuq, wuk, wuv, rope), name=name, grid=(t // tm,),
        in_specs=[_rows(tm, D_MODEL), _whole((1, D_MODEL)), _whole((D_MODEL, DZ)), _whole((1, Q_RANK)), _whole((1, KV_RANK)),
                  _whole((Q_RANK, QK_WIDTH)), _whole((KV_RANK, QK_WIDTH)), _whole((KV_RANK, D_MODEL)), _rows(tm, 4 * LANES)],
        out_specs=[_rows(tm, DZ), _rows(tm, QK_WIDTH), _rows(tm, QK_WIDTH), _rows(tm, D_MODEL)],
        out_shape=[jax.ShapeDtypeStruct((t, DZ), F32), jax.ShapeDtypeStruct((t, QK_WIDTH), BF16),
                   jax.ShapeDtypeStruct((t, QK_WIDTH), BF16), jax.ShapeDtypeStruct((t, D_MODEL), BF16)])


def attn_fwd(q, k, v, *, nb, lp, hb, name, ride=None):
    t = q.shape[0]
    nq, tail = lp // TQ, lp % TQ
    assert tail % LANES == 0

    def body(q_ref, k_ref, v_ref, o_ref, lse_ref, vt):
        for pr in range(hb // 2):
            vt[pr] = v_ref[:, pr * LANES:(pr + 1) * LANES].T

        def q_block(qs, tq, whole_k):
            qh = [q_ref[pl.ds(qs, tq), hd * HEAD_SLOT:(hd + 1) * HEAD_SLOT] for hd in range(hb)]
            keep = lax.broadcasted_iota(jnp.int32, (tq, tq), 0) <= lax.broadcasted_iota(jnp.int32, (tq, tq), 1)

            def k_steps(blocks, c, masked):
                sts = [[_dot_nt(k_ref[pl.ds(ks, tk), hd * HEAD_SLOT:(hd + 1) * HEAD_SLOT], qh[hd]) for hd in range(hb)]
                       for ks, tk in blocks]
                for (ks, tk), st_b in zip(blocks, sts):
                    ps, stats = [], []
                    for hd in range(hb):
                        m, l, _ = c[hd]
                        st = jnp.where(keep, st_b[hd], MASK_VALUE) if masked else st_b[hd]
                        m_new = jnp.maximum(m, jnp.max(st, axis=0, keepdims=True))
                        p = jnp.exp2((st - m_new) * EXP2_SCALE)
                        alpha = jnp.exp2((m - m_new) * EXP2_SCALE)
                        ps.append(p.astype(BF16))
                        stats.append((m_new, alpha * l + jnp.sum(p, axis=0, keepdims=True), alpha))
                    pvs = [_dot(vt[hd // 2, :, pl.ds(ks, tk)], ps[hd]) for hd in range(hb)]
                    c = tuple((stats[hd][0], stats[hd][1], stats[hd][2] * c[hd][2] + pvs[hd]) for hd in range(hb))
                return c

            def two_blocks(i, c):
                ks = pl.multiple_of(2 * i * TK, TK)
                return k_steps([(ks, TK), (ks + TK, TK)], c, False)

            init = tuple((jnp.full((1, tq), MASK_VALUE, F32), jnp.zeros((1, tq), F32), jnp.zeros((LANES, tq), F32))
                         for _ in range(hb))
            pairs = lax.div(whole_k, 2)
            c = lax.fori_loop(0, pairs, two_blocks, init)
            c = lax.fori_loop(2 * pairs, whole_k, lambda kj, c: k_steps([(pl.multiple_of(kj * TK, TK), TK)], c, False), c)
            c = k_steps([(qs, tq)], c, True)
            sub = lax.broadcasted_iota(jnp.int32, (LANES, tq), 0)
            for pr in range(hb // 2):
                (m0, l0, a0), (m1, l1, a1) = c[2 * pr], c[2 * pr + 1]
                o_ref[pl.ds(qs, tq), pr * LANES:(pr + 1) * LANES] = jnp.where(sub < V_DIM, a0 / l0, a1 / l1).T.astype(BF16)
                lse_ref[2 * pr, :, pl.ds(qs, tq)] = m0 * SM_SCALE + jnp.log(l0)
                lse_ref[2 * pr + 1, :, pl.ds(qs, tq)] = m1 * SM_SCALE + jnp.log(l1)

        def whole_q_block(qi, carry):
            q_block(pl.multiple_of(qi * TQ, TQ), TQ, qi)
            return carry

        lax.fori_loop(0, nq, whole_q_block, 0)
        if tail:
            q_block(nq * TQ, tail, nq)

    blk = lambda w: pl.BlockSpec((lp, w), lambda b, g: (b, g))
    return _carrying_call(
        body, ride, (q, k, v), name=name, grid=(nb, N_HEADS // hb),
        in_specs=[blk(hb * HEAD_SLOT), blk(hb * HEAD_SLOT), blk(hb * V_DIM)],
        out_specs=[blk(hb * V_DIM), pl.BlockSpec((hb, 1, lp), lambda b, g: (g, 0, b))],
        out_shape=[jax.ShapeDtypeStruct((t, D_MODEL), BF16), jax.ShapeDtypeStruct((N_HEADS, 1, t), F32)],
        scratch_shapes=[pltpu.VMEM((hb // 2, LANES, lp), BF16)])


def _pool_band_fwd(i, tm, lp, nb):
    r = lax.broadcasted_iota(jnp.int32, (tm, POOL_HALO + tm), 0)
    e = lax.broadcasted_iota(jnp.int32, (tm, POOL_HALO + tm), 1)
    diff = r + POOL_HALO - e
    pos = _seq_pos(i * tm + lax.broadcasted_iota(jnp.int32, (tm, 1), 0), lp, nb)
    out = []
    for w in POOL_WINDOWS:
        cnt = jnp.minimum(pos + 1, w)
        band = jnp.where((diff >= 0) & (diff < cnt), 1.0, 0.0).astype(BF16)
        out.append((band, cnt.astype(F32)))
    return out


def merge_fwd(h, z, o, pw, ps, wpa, wpb, wo, *, tm, lp, nb, name, ride=None):
    t = h.shape[0]
    hb = tm // POOL_HALO

    def body(h_ref, u_ref, uprev_ref, ga_ref, gb_ref, o_ref, pw_ref, ps_ref, wpa_ref, wpb_ref, wo_ref,
             h1_ref, pooled_ref, a_ref, pa_ref, pb_ref, mg_ref):
        i = pl.program_id(0)
        u = u_ref[...]
        uext = jnp.concatenate([uprev_ref[...], u], axis=0).astype(BF16)
        pooled, ys = [], []
        for g, (band, cnt) in enumerate(_pool_band_fwd(i, tm, lp, nb)):
            gs = slice(g * POOL_GROUP, (g + 1) * POOL_GROUP)
            pg = (_dot(band, uext[:, gs]) / cnt - u[:, gs]).astype(BF16)
            pooled.append(pg)
            ys.append(_dot(pg, pw_ref[g]))
        pooled_ref[...] = jnp.concatenate(pooled, axis=1)
        a = (jnp.concatenate(ys, axis=1) * ps_ref[...]).astype(BF16)
        a_ref[...] = a
        pa = _dot(a, wpa_ref[...])
        pb = _dot(o_ref[...], wpb_ref[...])
        pa_ref[...] = pa.astype(BF16)
        pb_ref[...] = pb.astype(BF16)
        mg = (_sigmoid(ga_ref[...]) * pa + _sigmoid(gb_ref[...]) * pb).astype(BF16)
        mg_ref[...] = mg
        h1_ref[...] = h_ref[...] + _dot(mg, wo_ref[...])

    halo = pl.BlockSpec((POOL_HALO, POOL_WIDTH), lambda i: (jnp.maximum(i * hb - 1, 0), 0))
    return _carrying_call(
        body, ride, (h, z, z, z, z, o, pw, ps, wpa, wpb, wo), name=name, grid=(t // tm,),
        in_specs=[_rows(tm, D_MODEL), _rows(tm, POOL_WIDTH), halo, _rows(tm, D_MODEL, 1), _rows(tm, D_MODEL, 2), _rows(tm, D_MODEL),
                  _whole((4, POOL_GROUP, POOL_GROUP)), _whole((1, POOL_WIDTH)), _whole((POOL_WIDTH, D_MODEL)),
                  _whole((D_MODEL, D_MODEL)), _whole((D_MODEL, D_MODEL))],
        out_specs=[_rows(tm, D_MODEL), _rows(tm, POOL_WIDTH), _rows(tm, POOL_WIDTH), _rows(tm, D_MODEL), _rows(tm, D_MODEL),
                   _rows(tm, D_MODEL)],
        out_shape=[jax.ShapeDtypeStruct((t, D_MODEL), F32), jax.ShapeDtypeStruct((t, POOL_WIDTH), BF16),
                   jax.ShapeDtypeStruct((t, POOL_WIDTH), BF16), jax.ShapeDtypeStruct((t, D_MODEL), BF16),
                   jax.ShapeDtypeStruct((t, D_MODEL), BF16), jax.ShapeDtypeStruct((t, D_MODEL), BF16)])


def ffn_fwd(h1, g, wgt, wut, wd, *, tm, name):
    t = h1.shape[0]

    def body(h_ref, g_ref, wgt_ref, wut_ref, wd_ref, h2_ref, gt_ref, up_ref):
        h = h_ref[...]
        xhat, _ = _rms(h)
        hn = (xhat * g_ref[...]).astype(BF16)
        gt = _dot_nt(hn, wgt_ref[...])
        up = _dot_nt(hn, wut_ref[...])
        gt_ref[...] = gt.astype(BF16)
        up_ref[...] = up.astype(BF16)
        act = (gt * _sigmoid(gt) * up).astype(BF16)
        h2_ref[...] = h + _dot(act, wd_ref[...])

    return pl.pallas_call(
        body, name=name, grid=(t // tm,),
        in_specs=[_rows(tm, D_MODEL), _whole((1, D_MODEL)), _whole((D_FF, D_MODEL)), _whole((D_FF, D_MODEL)), _whole((D_FF, D_MODEL))],
        out_specs=[_rows(tm, D_MODEL), _rows(tm, D_FF), _rows(tm, D_FF)],
        out_shape=[jax.ShapeDtypeStruct((t, D_MODEL), F32), jax.ShapeDtypeStruct((t, D_FF), BF16), jax.ShapeDtypeStruct((t, D_FF), BF16)],
        compiler_params=_cparams(),
    )(h1, g, wgt, wut, wd)


def loss_head(h, g, target, *, tm, lp, nb, seq, name):
    t = h.shape[0]
    nt = t // tm

    def body(h_ref, g_ref, t_ref, loss_ref, dh_ref, dg_ref):
        i = pl.program_id(0)
        pos = _seq_pos(i * tm + lax.broadcasted_iota(jnp.int32, (tm, 1), 0), lp, nb)
        real = (pos >= N_META) & (pos < N_META + seq)
        xhat, r = _rms(h_ref[...])
        gg = g_ref[...]
        err = jnp.where(real, xhat * gg - t_ref[...], 0.0)
        loss_ref[...] = jnp.full((8, LANES), 0.5 * jnp.sum(err * err) / D_MODEL, F32)
        dx, dg = _rms_bwd(err * (1.0 / D_MODEL), xhat, r, gg)
        dh_ref[...] = dx

        @pl.when(i == 0)
        def _():
            dg_ref[...] = jnp.zeros_like(dg_ref)

        dg_ref[...] += dg

    return pl.pallas_call(
        body, name=name, grid=(nt,),
        in_specs=[_rows(tm, D_MODEL), _whole((1, D_MODEL)), _rows(tm, D_MODEL)],
        out_specs=[pl.BlockSpec((8, LANES), lambda i: (i, 0)), _rows(tm, D_MODEL), _acc((1, D_MODEL))],
        out_shape=[jax.ShapeDtypeStruct((nt * 8, LANES), F32), jax.ShapeDtypeStruct((t, D_MODEL), F32),
                   jax.ShapeDtypeStruct((1, D_MODEL), F32)],
        compiler_params=_cparams(),
    )(h, g, target)


def wgrad(x, ys, chunk_fn, chunk_shape, *, tm, name):
    t, m = x.shape
    tiles = t // tm
    steps = -(-tiles // 2)

    def body(*refs):
        ins, o_ref, accs = refs[:2 * (1 + len(ys))], refs[2 * (1 + len(ys))], refs[2 * (1 + len(ys)) + 1:]
        i = pl.program_id(0)

        @pl.when(i == 0)
        def _():
            for acc in accs:
                acc[...] = jnp.zeros_like(acc)

        def both(first, second, mask):
            b = second[...].astype(BF16)
            if mask and tiles % 2:
                b = jnp.where(2 * i + 1 < tiles, b, jnp.zeros_like(b))
            return jnp.concatenate([first[...].astype(BF16), b], axis=0)

        xb = both(ins[0], ins[1], True)
        for j, acc in enumerate(accs):
            acc[...] += _dot_tn(xb, both(ins[2 + 2 * j], ins[3 + 2 * j], False))

        @pl.when(i == steps - 1)
        def _():
            for p, chunk in enumerate(chunk_fn(*accs)):
                o_ref[p % 2, p // 2] = chunk.astype(BF16)

    def two_tiles(width):
        return [pl.BlockSpec((tm, width), lambda i: (2 * i, 0)),
                pl.BlockSpec((tm, width), lambda i: (jnp.minimum(2 * i + 1, tiles - 1), 0))]

    out = (2, N_DEV // 2) + tuple(chunk_shape)
    operands = [x, x] + [a for y in ys for a in (y, y)]
    return pl.pallas_call(
        body, name=name, grid=(steps,),
        in_specs=two_tiles(m) + [s for y in ys for s in two_tiles(y.shape[1])], out_specs=_acc(out),
        out_shape=jax.ShapeDtypeStruct(out, BF16), scratch_shapes=[pltpu.VMEM((m, y.shape[1]), F32) for y in ys],
        compiler_params=_cparams(),
    )(*operands)


def ffn_bwd(dh2, h1, g, gt, up, wgt, wut, wd, *, tm, name, ride=None):
    t = h1.shape[0]

    def body(dh2_ref, h_ref, g_ref, gt_ref, up_ref, wgt_ref, wut_ref, wd_ref, dh1_ref, hn_ref, act_ref, dgt_ref, dup_ref, dg_ref):
        dh2 = dh2_ref[...]
        dact = _dot_nt(dh2.astype(BF16), wd_ref[...])
        gt = gt_ref[...].astype(F32)
        up = up_ref[...].astype(F32)
        sg = _sigmoid(gt)
        silu = gt * sg
        act_ref[...] = (silu * up).astype(BF16)
        dgt = (dact * up * (sg * (1.0 + gt * (1.0 - sg)))).astype(BF16)
        dup = (dact * silu).astype(BF16)
        dgt_ref[...] = dgt
        dup_ref[...] = dup
        dhn = _dot(dgt, wgt_ref[...]) + _dot(dup, wut_ref[...])
        xhat, r = _rms(h_ref[...])
        gg = g_ref[...]
        hn_ref[...] = (xhat * gg).astype(BF16)
        dx, dg = _rms_bwd(dhn, xhat, r, gg)
        dh1_ref[...] = dh2 + dx

        @pl.when(pl.program_id(0) == 0)
        def _():
            dg_ref[...] = jnp.zeros_like(dg_ref)

        dg_ref[...] += dg

    return _carrying_call(
        body, ride, (dh2, h1, g, gt, up, wgt, wut, wd), name=name, grid=(t // tm,),
        in_specs=[_rows(tm, D_MODEL), _rows(tm, D_MODEL), _whole((1, D_MODEL)), _rows(tm, D_FF), _rows(tm, D_FF),
                  _whole((D_FF, D_MODEL)), _whole((D_FF, D_MODEL)), _whole((D_FF, D_MODEL))],
        out_specs=[_rows(tm, D_MODEL), _rows(tm, D_MODEL), _rows(tm, D_FF), _rows(tm, D_FF), _rows(tm, D_FF), _acc((1, D_MODEL))],
        out_shape=[jax.ShapeDtypeStruct((t, D_MODEL), F32), jax.ShapeDtypeStruct((t, D_MODEL), BF16),
                   jax.ShapeDtypeStruct((t, D_FF), BF16), jax.ShapeDtypeStruct((t, D_FF), BF16),
                   jax.ShapeDtypeStruct((t, D_FF), BF16), jax.ShapeDtypeStruct((1, D_MODEL), F32)])


def merge_bwd(dh1, z, pa, pb, pooled, pw, ps, wpa, wpb, wo, *, tm, name, ride=None):
    t = dh1.shape[0]

    def body(dh1_ref, ga_ref, gb_ref, pa_ref, pb_ref, pooled_ref, pw_ref, ps_ref, wpa_ref, wpb_ref, wo_ref,
             dga_ref, dgb_ref, dpa_ref, dpb_ref, do_ref, dpool_ref, dps_ref, dpw_ref):
        dmg = _dot_nt(dh1_ref[...].astype(BF16), wo_ref[...])
        sa = _sigmoid(ga_ref[...])
        sb = _sigmoid(gb_ref[...])
        dga_ref[...] = (dmg * pa_ref[...].astype(F32) * sa * (1.0 - sa)).astype(BF16)
        dgb_ref[...] = (dmg * pb_ref[...].astype(F32) * sb * (1.0 - sb)).astype(BF16)
        dpa = (dmg * sa).astype(BF16)
        dpb = (dmg * sb).astype(BF16)
        dpa_ref[...] = dpa
        dpb_ref[...] = dpb
        do_ref[...] = _dot_nt(dpb, wpb_ref[...]).astype(BF16)
        da = _dot_nt(dpa, wpa_ref[...])
        pooled = pooled_ref[...]
        ps = ps_ref[...]

        @pl.when(pl.program_id(0) == 0)
        def _():
            dps_ref[...] = jnp.zeros_like(dps_ref)
            dpw_ref[...] = jnp.zeros_like(dpw_ref)

        dps, dpool = [], []
        for g in range(len(POOL_WINDOWS)):
            gs = slice(g * POOL_GROUP, (g + 1) * POOL_GROUP)
            y = _dot(pooled[:, gs], pw_ref[g])
            dps.append(jnp.sum(da[:, gs] * y, axis=0, keepdims=True))
            dy = (da[:, gs] * ps[:, gs]).astype(BF16)
            dpool.append(_dot_nt(dy, pw_ref[g]))
            dpw_ref[g] += _dot_tn(pooled[:, gs], dy)
        dps_ref[...] += jnp.concatenate(dps, axis=1)
        dpool_ref[...] = jnp.concatenate(dpool, axis=1)

    return _carrying_call(
        body, ride, (dh1, z, z, pa, pb, pooled, pw, ps, wpa, wpb, wo), name=name, grid=(t // tm,),
        in_specs=[_rows(tm, D_MODEL), _rows(tm, D_MODEL, 1), _rows(tm, D_MODEL, 2), _rows(tm, D_MODEL), _rows(tm, D_MODEL),
                  _rows(tm, POOL_WIDTH), _whole((4, POOL_GROUP, POOL_GROUP)),
                  _whole((1, POOL_WIDTH)), _whole((POOL_WIDTH, D_MODEL)), _whole((D_MODEL, D_MODEL)), _whole((D_MODEL, D_MODEL))],
        out_specs=[_rows(tm, D_MODEL), _rows(tm, D_MODEL), _rows(tm, D_MODEL), _rows(tm, D_MODEL), _rows(tm, D_MODEL),
                   _rows(tm, POOL_WIDTH), _acc((1, POOL_WIDTH)), _acc((4, POOL_GROUP, POOL_GROUP))],
        out_shape=[jax.ShapeDtypeStruct((t, D_MODEL), BF16)] * 5
        + [jax.ShapeDtypeStruct((t, POOL_WIDTH), F32), jax.ShapeDtypeStruct((1, POOL_WIDTH), F32),
           jax.ShapeDtypeStruct((4, POOL_GROUP, POOL_GROUP), F32)])


def attn_bwd(q, k, v, o, do, lse, *, nb, lp, hb, name, ride=None):
    t = q.shape[0]
    nq, tail = lp // TQ, lp % TQ
    assert tail % LANES == 0

    def body(q_ref, k_ref, v_ref, o_ref, do_ref, lse_ref, dq_ref, dk_ref, dv_ref, kt, doh, lse_row, delta_row, dqt):
        lane = lax.broadcasted_iota(jnp.int32, (lp, LANES), 1)
        first = lane < V_DIM
        sub = lax.broadcasted_iota(jnp.int32, (LANES, lp), 0)
        for pr in range(hb // 2):
            ls = slice(pr * LANES, (pr + 1) * LANES)
            do = do_ref[:, ls]
            doh[2 * pr] = jnp.where(first, do, jnp.zeros_like(do))
            doh[2 * pr + 1] = jnp.where(first, jnp.zeros_like(do), do)
            prod_t = (do.astype(F32) * o_ref[:, ls].astype(F32)).T
            delta_row[2 * pr] = jnp.sum(jnp.where(sub < V_DIM, prod_t, 0.0), axis=0, keepdims=True)
            delta_row[2 * pr + 1] = jnp.sum(jnp.where(sub < V_DIM, 0.0, prod_t), axis=0, keepdims=True)
        for hd in range(hb):
            lse_row[hd] = lse_ref[hd] * LOG2E
            kt[hd] = k_ref[:, hd * HEAD_SLOT:(hd + 1) * HEAD_SLOT].T
        dqt[...] = jnp.zeros(dqt.shape, F32)
        heads = range(hb)
        hss = [slice(hd * HEAD_SLOT, (hd + 1) * HEAD_SLOT) for hd in heads]

        def k_block(ks, tk, next_q):
            keep = lax.broadcasted_iota(jnp.int32, (tk, tk), 0) <= lax.broadcasted_iota(jnp.int32, (tk, tk), 1)

            def q_steps(blocks, c, masked):
                work = [(qs, tq, hd) for qs, tq in blocks for hd in heads]
                qhs = [q_ref[pl.ds(qs, tq), hss[hd]] for qs, tq, hd in work]
                dos = [doh[hd, pl.ds(qs, tq), :] for qs, tq, hd in work]
                sts = [_dot_nt(k_ref[pl.ds(ks, tk), hss[hd]], qhs[i]) for i, (_, _, hd) in enumerate(work)]
                dpts = [_dot_nt(v_ref[pl.ds(ks, tk), (hd // 2) * LANES:(hd // 2 + 1) * LANES], dos[i])
                        for i, (_, _, hd) in enumerate(work)]
                pts, dsts = [], []
                for i, (qs, tq, hd) in enumerate(work):
                    st = jnp.where(keep, sts[i], MASK_VALUE) if masked else sts[i]
                    pt = jnp.exp2(st * EXP2_SCALE - lse_row[hd, :, pl.ds(qs, tq)])
                    dsts.append((pt * (dpts[i] - delta_row[hd, :, pl.ds(qs, tq)])).astype(BF16))
                    pts.append(pt.astype(BF16))
                dvs = [_dot(pts[i], dos[i]) for i in range(len(work))]
                dks = [_dot(dsts[i], qhs[i]) for i in range(len(work))]
                dqs = [_dot(kt[hd, :, pl.ds(ks, tk)], dsts[i]) for i, (_, _, hd) in enumerate(work)]
                c = list(c)
                for i, (qs, tq, hd) in enumerate(work):
                    dqt[hd, :, pl.ds(qs, tq)] += dqs[i]
                    c[hd] = (c[hd][0] + dks[i], c[hd][1] + dvs[i])
                return tuple(c)

            zero = jnp.zeros((tk, LANES), F32)
            c = q_steps([(ks, tk)], tuple((zero, zero) for _ in heads), True)
            if next_q is not None:
                def two_blocks(i, c):
                    qs = pl.multiple_of((next_q + 2 * i) * TQ, TQ)
                    return q_steps([(qs, TQ), (qs + TQ, TQ)], c, False)

                pairs = lax.div(nq - next_q, 2)
                c = lax.fori_loop(0, pairs, two_blocks, c)
                c = lax.fori_loop(next_q + 2 * pairs, nq, lambda qi, c: q_steps([(pl.multiple_of(qi * TQ, TQ), TQ)], c, False), c)
                if tail:
                    c = q_steps([(nq * TQ, tail)], c, False)
            for hd in heads:
                dk_ref[pl.ds(ks, tk), hss[hd]] = c[hd][0] * SM_SCALE
            for pr in range(hb // 2):
                dv_ref[pl.ds(ks, tk), pr * LANES:(pr + 1) * LANES] = c[2 * pr][1] + c[2 * pr + 1][1]

        def whole_k_block(kj, carry):
            k_block(pl.multiple_of(kj * TK, TK), TK, kj + 1)
            return carry

        lax.fori_loop(0, nq, whole_k_block, 0)
        if tail:
            k_block(nq * TQ, tail, None)
        for hd in range(hb):
            dq_ref[:, hd * HEAD_SLOT:(hd + 1) * HEAD_SLOT] = dqt[hd].T * SM_SCALE

    blk = lambda w: pl.BlockSpec((lp, w), lambda b, g: (b, g))
    return _carrying_call(
        body, ride, (q, k, v, o, do, lse), name=name, grid=(nb, N_HEADS // hb),
        in_specs=[blk(hb * HEAD_SLOT), blk(hb * HEAD_SLOT), blk(hb * V_DIM), blk(hb * V_DIM), blk(hb * V_DIM),
                  pl.BlockSpec((hb, 1, lp), lambda b, g: (g, 0, b))],
        out_specs=[blk(hb * HEAD_SLOT), blk(hb * HEAD_SLOT), blk(hb * V_DIM)],
        out_shape=[jax.ShapeDtypeStruct((t, QK_WIDTH), F32), jax.ShapeDtypeStruct((t, QK_WIDTH), F32),
                   jax.ShapeDtypeStruct((t, D_MODEL), F32)],
        scratch_shapes=[pltpu.VMEM((hb, HEAD_SLOT, lp), BF16), pltpu.VMEM((hb, lp, LANES), BF16), pltpu.VMEM((hb, 1, lp), F32),
                        pltpu.VMEM((hb, 1, lp), F32), pltpu.VMEM((hb, HEAD_SLOT, lp), F32)])


def in_proj_bwd(dh1, h, g_mix, z, dq, dk, dv, dga, dgb, dpool, win, gq, gkv, wuq, wuk, wuv, rope, *, tm, lp, nb, name):
    t = h.shape[0]
    hb = tm // POOL_HALO
    last_halo = t // POOL_HALO - 1

    def body(dh1_ref, h_ref, g_ref, zcq_ref, zckv_ref, dq_ref, dk_ref, dv_ref, dga_ref, dgb_ref, dpool_ref, dnext_ref,
             win_ref, gq_ref, gkv_ref, wuq_ref, wuk_ref, wuv_ref, rope_ref,
             dh_ref, hn_ref, dz_ref, cqn_ref, ckvn_ref, dqb_ref, dkb_ref, dvb_ref, dg_ref, dgq_ref, dgkv_ref):
        i = pl.program_id(0)
        rope_t = rope_ref[...]
        dqb = _rope_bwd(dq_ref[...], *_rope_tables(rope_t, N_HEADS)).astype(BF16)
        dqb_ref[...] = dqb
        xq, rq = _rms(zcq_ref[...])
        gq_v = gq_ref[...]
        cqn_ref[...] = (xq * gq_v).astype(BF16)
        dcq, dgq = _rms_bwd(_dot_nt(dqb, wuq_ref[...]), xq, rq, gq_v)
        dk = dk_ref[...]
        dkb = dk.astype(BF16)
        dvb = dv_ref[...].astype(BF16)
        dkb_ref[...] = dkb
        dvb_ref[...] = dvb
        xkv, rkv = _rms(zckv_ref[...])
        gkv_v = gkv_ref[...]
        ckvn_ref[...] = (xkv * gkv_v).astype(BF16)
        dckv, dgkv = _rms_bwd(_dot_nt(dkb, wuk_ref[...]) + _dot_nt(dvb, wuv_ref[...]), xkv, rkv, gkv_v)
        dks = dk[:, :HEAD_SLOT]
        for hd in range(1, N_HEADS):
            dks = dks + dk[:, hd * HEAD_SLOT:(hd + 1) * HEAD_SLOT]
        dzk = _rope_bwd(dks, *_rope_tables(rope_t, 1))
        dp_cur = dpool_ref[...]
        dp_ext = jnp.concatenate([dp_cur, dnext_ref[...]], axis=0)
        r = lax.broadcasted_iota(jnp.int32, (tm, tm + POOL_HALO), 0)
        e = lax.broadcasted_iota(jnp.int32, (tm, tm + POOL_HALO), 1)
        gt_col = i * tm + lax.broadcasted_iota(jnp.int32, (1, tm + POOL_HALO), 1)
        pos_col = _seq_pos(gt_col, lp, nb)
        gt_row = i * tm + lax.broadcasted_iota(jnp.int32, (tm + POOL_HALO, 1), 0)
        pos_row = _seq_pos(gt_row, lp, nb)
        dus = []
        for g, w in enumerate(POOL_WINDOWS):
            gs = slice(g * POOL_GROUP, (g + 1) * POOL_GROUP)
            band = jnp.where((e - r >= 0) & (e - r < jnp.minimum(pos_col + 1, w)) & (gt_col < t), 1.0, 0.0).astype(BF16)
            scaled = jnp.where(gt_row < t, dp_ext[:, gs] / jnp.minimum(pos_row + 1, w).astype(F32), 0.0).astype(BF16)
            dus.append(_dot(band, scaled) - dp_cur[:, gs])
        dz = jnp.concatenate(dus + [dcq, dckv, dzk], axis=1).astype(BF16)
        dz = jnp.concatenate([dz, dga_ref[...], dgb_ref[...]], axis=1)
        dz_ref[...] = dz
        xhat, rr = _rms(h_ref[...])
        gg = g_ref[...]
        hn_ref[...] = (xhat * gg).astype(BF16)
        dx, dg = _rms_bwd(_dot_nt(dz, win_ref[...]), xhat, rr, gg)
        dh_ref[...] = dh1_ref[...] + dx

        @pl.when(i == 0)
        def _():
            dg_ref[...] = jnp.zeros_like(dg_ref)
            dgq_ref[...] = jnp.zeros_like(dgq_ref)
            dgkv_ref[...] = jnp.zeros_like(dgkv_ref)

        dg_ref[...] += dg
        dgq_ref[...] += dgq
        dgkv_ref[...] += dgkv

    nxt = pl.BlockSpec((POOL_HALO, POOL_WIDTH), lambda i: (jnp.minimum((i + 1) * hb, last_halo), 0))
    return pl.pallas_call(
        body, name=name, grid=(t // tm,),
        in_specs=[_rows(tm, D_MODEL), _rows(tm, D_MODEL), _whole((1, D_MODEL)), _rows(tm, Q_RANK, Z_CQ // Q_RANK),
                  _rows(tm, KV_RANK, Z_CKV // KV_RANK), _rows(tm, QK_WIDTH), _rows(tm, QK_WIDTH), _rows(tm, D_MODEL),
                  _rows(tm, D_MODEL), _rows(tm, D_MODEL), _rows(tm, POOL_WIDTH), nxt,
                  _whole((D_MODEL, DZ)), _whole((1, Q_RANK)), _whole((1, KV_RANK)), _whole((Q_RANK, QK_WIDTH)),
                  _whole((KV_RANK, QK_WIDTH)), _whole((KV_RANK, D_MODEL)), _rows(tm, 4 * LANES)],
        out_specs=[_rows(tm, D_MODEL), _rows(tm, D_MODEL), _rows(tm, DZ), _rows(tm, Q_RANK), _rows(tm, KV_RANK),
                   _rows(tm, QK_WIDTH), _rows(tm, QK_WIDTH), _rows(tm, D_MODEL),
                   _acc((1, D_MODEL)), _acc((1, Q_RANK)), _acc((1, KV_RANK))],
        out_shape=[jax.ShapeDtypeStruct((t, D_MODEL), F32), jax.ShapeDtypeStruct((t, D_MODEL), BF16),
                   jax.ShapeDtypeStruct((t, DZ), BF16), jax.ShapeDtypeStruct((t, Q_RANK), BF16),
                   jax.ShapeDtypeStruct((t, KV_RANK), BF16), jax.ShapeDtypeStruct((t, QK_WIDTH), BF16),
                   jax.ShapeDtypeStruct((t, QK_WIDTH), BF16), jax.ShapeDtypeStruct((t, D_MODEL), BF16),
                   jax.ShapeDtypeStruct((1, D_MODEL), F32), jax.ShapeDtypeStruct((1, Q_RANK), F32),
                   jax.ShapeDtypeStruct((1, KV_RANK), F32)],
        compiler_params=_cparams(),
    )(dh1, h, g_mix, z, z, dq, dk, dv, dga, dgb, dpool, dpool, win, gq, gkv, wuq, wuk, wuv, rope)


_MESH = pl.DeviceIdType.MESH


def _place():
    x, y, c = lax.axis_index("x"), lax.axis_index("y"), lax.axis_index("c")
    return x, y, c, 4 * x + 2 * y + c


def _peer(x, y, c, k):
    px, py, pc = (1 - x) if k & 4 else x, (1 - y) if k & 2 else y, (1 - c) if k & 1 else c
    return (px, py, pc), 4 * px + 2 * py + pc


ALL_PEERS = tuple(range(1, N_DEV))
CHIP_PEERS = (2, 4, 6)
N_CHIPS = N_DEV // 2


def _sem_scratch(n, m):
    return [pltpu.SemaphoreType.DMA((n, m)), pltpu.SemaphoreType.DMA((n, m)), pltpu.SemaphoreType.DMA((n,))]


class Exchange:
    def __init__(self, arrays, out_shapes, sem_cols, plan, aliased=False):
        self.arrays, self.out_shapes, self.plan = list(arrays), list(out_shapes), plan
        self.scratch = _sem_scratch(len(self.arrays), sem_cols)
        self.aliased = aliased

    def split(self, refs):
        n = len(self.arrays)
        return refs[:n], refs[n:2 * n], refs[2 * n:]

    def start(self, srcs, dsts, sems):
        local, sends, _ = self.plan(srcs, dsts, *sems)
        for cp in local + sends:
            cp.start()

    def wait(self, srcs, dsts, sems):
        local, sends, recvs = self.plan(srcs, dsts, *sems)
        for cp in recvs:
            cp.wait_recv()
        for cp in sends:
            cp.wait_send()
        for cp in local:
            cp.wait()

    def aliases(self, first_in, first_out):
        return {first_in + j: first_out + j for j in range(len(self.arrays))} if self.aliased else {}

    def run(self, name):
        def body(*refs):
            srcs, dsts, sems = self.split(refs)
            self.start(srcs, dsts, sems)
            self.wait(srcs, dsts, sems)

        n = len(self.arrays)
        return pl.pallas_call(body, name=name, in_specs=[_ANY] * n, out_specs=[_ANY] * n, out_shape=self.out_shapes,
                              scratch_shapes=self.scratch, input_output_aliases=self.aliases(0, 0))(*self.arrays)


def exchange(arrays, scatter, peers, by_chip=False):
    slots = N_CHIPS if by_chip else N_DEV

    def plan(srcs, dsts, send_sems, recv_sems, local_sems):
        x, y, c, me = _place()
        mine = 2 * x + y if by_chip else me
        local = [pltpu.make_async_copy(src.at[mine] if scatter else src, dst.at[mine], local_sems.at[j])
                 for j, (src, dst) in enumerate(zip(srcs, dsts))]
        sends, recvs = [], []
        for t, k in enumerate(peers):
            peer, pidx = _peer(x, y, c, k)
            theirs = 2 * peer[0] + peer[1] if by_chip else pidx
            for j, (src, dst) in enumerate(zip(srcs, dsts)):
                part = src.at[theirs] if scatter else src
                sems = dict(send_sem=send_sems.at[j, t], recv_sem=recv_sems.at[j, t], device_id=peer, device_id_type=_MESH)
                sends.append(pltpu.make_async_remote_copy(src_ref=part, dst_ref=dst.at[mine], **sems))
                recvs.append(pltpu.make_async_remote_copy(src_ref=part, dst_ref=dst.at[theirs], **sems))
        return local, sends, recvs

    shapes = [jax.ShapeDtypeStruct(a.shape if scatter else (slots,) + a.shape, a.dtype) for a in arrays]
    return Exchange(arrays, shapes, len(peers), plan)


def second_hop(gathered):
    def plan(srcs, dsts, send_sems, recv_sems, local_sems):
        x, y, c, me = _place()
        sibling, _ = _peer(x, y, c, 1)
        sends, recvs = [], []
        for t, k in enumerate(CHIP_PEERS):
            _, landed = _peer(x, y, c, k)
            _, coming = _peer(x, y, c, k ^ 1)
            for j, buf in enumerate(dsts):
                sems = dict(send_sem=send_sems.at[j, t], recv_sem=recv_sems.at[j, t], device_id=sibling, device_id_type=_MESH)
                sends.append(pltpu.make_async_remote_copy(src_ref=buf.at[landed], dst_ref=buf.at[landed], **sems))
                recvs.append(pltpu.make_async_remote_copy(src_ref=buf.at[coming], dst_ref=buf.at[coming], **sems))
        return [], sends, recvs

    shapes = [jax.ShapeDtypeStruct(a.shape, a.dtype) for a in gathered]
    return Exchange(gathered, shapes, len(CHIP_PEERS), plan, aliased=True)


FIRST_HOP_PEERS = (1,) + CHIP_PEERS


def _gather_two_level(arrays, name):
    n = len(arrays)

    def body(*refs):
        srcs, dsts, (send_sems, recv_sems, local_sems) = refs[:n], refs[n:2 * n], refs[2 * n:]
        x, y, c, me = _place()
        sibling, sidx = _peer(x, y, c, 1)

        def copy(j, sem, block, to, src=None):
            rows = dsts[j].at[block]
            return pltpu.make_async_remote_copy(src_ref=rows if src is None else src, dst_ref=rows, send_sem=send_sems.at[j, sem],
                                                recv_sem=recv_sems.at[j, sem], device_id=to, device_id_type=_MESH)

        local = [pltpu.make_async_copy(srcs[j], dsts[j].at[me], local_sems.at[j]) for j in range(n)]
        for cp in local:
            cp.start()
        first = [copy(j, 1 + t, me, _peer(x, y, c, k)[0], src=srcs[j]) for t, k in enumerate(CHIP_PEERS) for j in range(n)]
        first += [copy(j, 0, me, sibling, src=srcs[j]) for j in range(n)]
        for cp in first:
            cp.start()
        passed = []
        for t, k in enumerate(CHIP_PEERS):
            peer, pidx = _peer(x, y, c, k)
            for j in range(n):
                copy(j, 1 + t, pidx, peer).wait_recv()
                passed.append(copy(j, 4 + t, pidx, sibling))
                passed[-1].start()
        for j in range(n):
            copy(j, 0, sidx, sibling).wait_recv()
        for t, k in enumerate(CHIP_PEERS):
            _, pidx = _peer(x, y, c, k ^ 1)
            for j in range(n):
                copy(j, 4 + t, pidx, sibling).wait_recv()
        for cp in first + passed:
            cp.wait_send()
        for cp in local:
            cp.wait()

    shapes = [jax.ShapeDtypeStruct((N_DEV,) + a.shape, a.dtype) for a in arrays]
    return pl.pallas_call(body, name=name, in_specs=[_ANY] * n, out_specs=[_ANY] * n, out_shape=shapes,
                          scratch_shapes=_sem_scratch(n, 1 + 2 * len(CHIP_PEERS)))(*arrays)


def to_sibling(arrays):
    def plan(srcs, dsts, send_sems, recv_sems, local_sems):
        x, y, c, _ = _place()
        sibling, _ = _peer(x, y, c, 1)
        copies = [pltpu.make_async_remote_copy(src_ref=src.at[1 - c], dst_ref=dst, send_sem=send_sems.at[j, 0],
                                               recv_sem=recv_sems.at[j, 0], device_id=sibling, device_id_type=_MESH)
                  for j, (src, dst) in enumerate(zip(srcs, dsts))]
        return [], copies, copies

    return Exchange(arrays, [jax.ShapeDtypeStruct(a.shape[1:], a.dtype) for a in arrays], 1, plan)


def combine(a, b):
    assert not (a.aliased or b.aliased)
    na, nsem = len(a.arrays), len(a.scratch)

    def plan(srcs, dsts, *sems):
        return tuple(u + v for u, v in zip(a.plan(srcs[:na], dsts[:na], *sems[:nsem]), b.plan(srcs[na:], dsts[na:], *sems[nsem:])))

    both = Exchange(a.arrays + b.arrays, a.out_shapes + b.out_shapes, 1, plan)
    both.scratch = a.scratch + b.scratch
    return both


def pair_add(own, theirs, core, *, name):
    _, ns, r, c = own.shape
    rb = _row_block(r, c // 2)

    def body(core_ref, a_ref, b_ref, o_ref):
        o_ref[...] = (a_ref[...].astype(F32) + b_ref[...].astype(F32)).astype(o_ref.dtype)

    return pl.pallas_call(
        body, name=name,
        grid_spec=pltpu.PrefetchScalarGridSpec(
            num_scalar_prefetch=1, grid=(ns, r // rb),
            in_specs=[pl.BlockSpec((None, None, rb, c), lambda i, j, core_ref: (core_ref[0], i, j, 0)),
                      pl.BlockSpec((None, rb, c), lambda i, j, core_ref: (i, j, 0))],
            out_specs=pl.BlockSpec((None, rb, c), lambda i, j, core_ref: (i, j, 0))),
        out_shape=jax.ShapeDtypeStruct((ns, r, c), own.dtype), compiler_params=_cparams(),
    )(core, own, theirs)


ADAMW_BLOCK_BYTES = 1 << 20


def _row_block(r, c):
    for rb in range(r, 0, -1):
        if r % rb == 0 and (rb % 16 == 0 or rb == r) and rb * c * 4 <= ADAMW_BLOCK_BYTES:
            return rb
    return r


def adamw_group(ws, ms, vs, parts, *, name, ride=None):
    k = len(ws)
    depth, r, c = ws[0].shape
    n_parts = parts[0][0].shape[0]
    rb = _row_block(r, c)

    def body(*refs):
        ins, outs = refs[:k * (3 + depth)], refs[k * (3 + depth):]

        def total(p_ref):
            g = p_ref[0].astype(F32)
            for j in range(1, n_parts):
                g = g + p_ref[j].astype(F32)
            return g

        for j in range(k):
            w_ref, m_ref, v_ref = ins[3 * j:3 * j + 3]
            p_refs = ins[3 * k + depth * j:3 * k + depth * (j + 1)]
            g_ref, d_ref, nm_ref, nv_ref = outs[4 * j:4 * j + 4]
            g = total(p_refs[0])
            for l in range(1, depth):
                g = jnp.where(pl.program_id(0) == l, total(p_refs[l]), g)
            g_ref[...] = g
            m_new = ADAM_B1 * m_ref[...] + (1.0 - ADAM_B1) * g
            v_new = ADAM_B2 * v_ref[...] + (1.0 - ADAM_B2) * (g * g)
            m_hat = m_new / (1.0 - ADAM_B1 ** ADAM_STEP)
            v_hat = v_new / (1.0 - ADAM_B2 ** ADAM_STEP)
            d_ref[...] = -ADAM_LR * (m_hat / (jnp.sqrt(v_hat) + ADAM_EPS) + ADAM_WD * w_ref[...])
            nm_ref[...] = m_new
            nv_ref[...] = v_new

    wblk = pl.BlockSpec((None, rb, c), lambda l, i: (l, i, 0))
    pblk = pl.BlockSpec((n_parts, rb, c), lambda l, i: (0, i, 0))
    operands = [a for j in range(k) for a in (ws[j], ms[j], vs[j])] + [p for j in range(k) for p in parts[j]]
    out, brought = _carrying_call(
        body, ride, operands, name=name, grid=(depth, r // rb),
        in_specs=[wblk] * (3 * k) + [pblk] * (depth * k), out_specs=[wblk] * (4 * k),
        out_shape=[jax.ShapeDtypeStruct((depth, r, c), F32)] * (4 * k))
    return [out[4 * j:4 * j + 4] for j in range(k)], brought


def adamw(w, m, v, parts, *, name):
    return adamw_group([w], [m], [v], [parts], name=name)[0][0]


BIG = (("w_in", 2), ("w_uq", 2), ("w_ukv", 2), ("w_pa", 2), ("w_pb", 1), ("w_o", 1), ("w_gate", 2), ("w_up", 2), ("w_down", 1))
SMALL = ("norm_mix_g", "pool_w", "pool_scale", "q_norm_g", "kv_norm_g", "norm_ffn_g", "final_norm_g")
WEIGHTS = ("meta_tokens", "norm_mix_g", "w_in", "pool_w", "pool_scale", "q_norm_g", "kv_norm_g", "w_uq", "w_ukv", "w_pa", "w_pb",
           "w_o", "norm_ffn_g", "w_gate", "w_up", "w_down", "final_norm_g")
HEAD_QK = QK_NOPE + QK_ROPE
KR_END = Z_KR + QK_ROPE


def _cat_cols(parts):
    return [jnp.concatenate(parts, axis=1)]


def _cat_rows(parts):
    return [jnp.concatenate(parts, axis=0)]


def _arr_w_in(parts):
    full = jnp.concatenate(parts, axis=1)
    zc = lambda n: jnp.zeros((full.shape[0], n), full.dtype)
    return [jnp.concatenate([full[:, :Z_KR], zc(QK_NOPE), full[:, Z_KR:KR_END], zc(LANES - HEAD_QK), full[:, KR_END:]], axis=1)]


def _arr_w_uq(parts):
    full = jnp.concatenate(parts, axis=1)
    z = jnp.zeros((full.shape[0], HEAD_SLOT - HEAD_QK), full.dtype)
    pieces = []
    for hd in range(N_HEADS):
        pieces += [full[:, hd * HEAD_QK:(hd + 1) * HEAD_QK], z]
    return [jnp.concatenate(pieces, axis=1)]


def _arr_w_ukv(parts):
    full = jnp.concatenate(parts, axis=1)
    z = jnp.zeros((full.shape[0], HEAD_SLOT - QK_NOPE), full.dtype)
    wide = QK_NOPE + V_DIM
    k, v = [], []
    for hd in range(N_HEADS):
        k += [full[:, hd * wide:hd * wide + QK_NOPE], z]
        v.append(full[:, hd * wide + QK_NOPE:(hd + 1) * wide])
    return [jnp.concatenate(k, axis=1), jnp.concatenate(v, axis=1)]


def arrange(g, fn, out_shapes, name):
    def body(g_ref, *o_refs):
        for o_ref, val in zip(o_refs, fn([g_ref[p] for p in range(N_DEV)])):
            o_ref[...] = val

    return pl.pallas_call(
        body, name=name, grid=(1,),
        in_specs=[pl.BlockSpec(g.shape, lambda i: (0, 0, 0))],
        out_specs=[pl.BlockSpec(s, lambda i: (0, 0)) for s in out_shapes],
        out_shape=[jax.ShapeDtypeStruct(s, g.dtype) for s in out_shapes], compiler_params=_cparams(),
    )(g)


def _arranged_ranges(lo, hi):
    out = []
    for a, b, shift in ((0, Z_KR, 0), (Z_KR, KR_END, QK_NOPE), (KR_END, D_IN, LANES - QK_ROPE)):
        s, e = max(lo, a), min(hi, b)
        if s < e:
            out.append((s + shift, e + shift))
    return out


def _chunks_w_in(acc):
    cs = D_IN // N_DEV
    return [jnp.concatenate([acc[:, a:b] for a, b in _arranged_ranges(p * cs, (p + 1) * cs)], axis=1) for p in range(N_DEV)]


def _chunks_w_uq(acc):
    per = N_HEADS // N_DEV
    return [jnp.concatenate([acc[:, hd * HEAD_SLOT:hd * HEAD_SLOT + HEAD_QK] for hd in range(p * per, (p + 1) * per)], axis=1)
            for p in range(N_DEV)]


def _chunks_w_ukv(acc_k, acc_v):
    per = N_HEADS // N_DEV
    out = []
    for p in range(N_DEV):
        pieces = []
        for hd in range(p * per, (p + 1) * per):
            pieces += [acc_k[:, hd * HEAD_SLOT:hd * HEAD_SLOT + QK_NOPE], acc_v[:, hd * V_DIM:(hd + 1) * V_DIM]]
        out.append(jnp.concatenate(pieces, axis=1))
    return out


def _chunks_cols(acc):
    cs = acc.shape[1] // N_DEV
    return [acc[:, p * cs:(p + 1) * cs] for p in range(N_DEV)]


def _chunks_rows(acc):
    rs = acc.shape[0] // N_DEV
    return [acc[p * rs:(p + 1) * rs, :] for p in range(N_DEV)]


def _chunks_cols_transposed(acc):
    at = acc[...].T
    rs = at.shape[0] // N_DEV
    return [at[p * rs:(p + 1) * rs, :] for p in range(N_DEV)]


def _pack(parts, row_multiple):
    flat = jnp.concatenate([p.reshape(-1) for p in parts])
    return jnp.pad(flat, (0, -flat.shape[0] % (row_multiple * LANES))).reshape(-1, LANES)


def _unpack(packed, shapes):
    flat, out, off = packed.reshape(-1), [], 0
    for s in shapes:
        n = 1
        for d in s:
            n *= d
        out.append(flat[off:off + n].reshape(s))
        off += n
    return out


def _rope_table(lp, nb):
    inv = 1.0 / (ROPE_THETA ** (jnp.arange(0, QK_ROPE, 2, dtype=F32) / QK_ROPE))
    ang = jnp.arange(lp, dtype=F32)[:, None] * inv[None, :]
    cos, sin = jnp.cos(ang), jnp.sin(ang)
    z = lambda n: jnp.zeros((lp, n), F32)
    tail = LANES - QK_NOPE - QK_ROPE
    c = jnp.concatenate([jnp.ones((lp, QK_NOPE), F32), cos, cos, z(tail)], axis=1)
    cr = jnp.concatenate([z(QK_NOPE), cos, cos, z(tail)], axis=1)
    s1 = jnp.concatenate([z(QK_NOPE), -sin, z(HALF_ROPE), z(tail)], axis=1)
    s2 = jnp.concatenate([z(QK_NOPE), z(HALF_ROPE), sin, z(tail)], axis=1)
    return jnp.tile(jnp.concatenate([c, cr, s1, s2], axis=1), (nb, 1))


MIX_IN, MIX_OUT = ("w_in", "w_uq", "w_ukv"), ("w_pa", "w_pb", "w_o")
MIX = MIX_IN + MIX_OUT
FFN = ("w_gate", "w_up", "w_down")
TRANSPOSED = ("w_gate", "w_up")
ARRANGERS = {
    "w_in": (_arr_w_in, (("win", (D_MODEL, DZ)),)), "w_uq": (_arr_w_uq, (("wuq", (Q_RANK, QK_WIDTH)),)),
    "w_ukv": (_arr_w_ukv, (("wuk", (KV_RANK, QK_WIDTH)), ("wuv", (KV_RANK, D_MODEL)))),
    "w_pa": (_cat_cols, (("wpa", (POOL_WIDTH, D_MODEL)),)), "w_pb": (_cat_rows, (("wpb", (D_MODEL, D_MODEL)),)),
    "w_o": (_cat_rows, (("wo", (D_MODEL, D_MODEL)),)), "w_gate": (_cat_rows, (("wgt", (D_FF, D_MODEL)),)),
    "w_up": (_cat_rows, (("wut", (D_FF, D_MODEL)),)), "w_down": (_cat_rows, (("wd", (D_FF, D_MODEL)),)),
}


def _operands(gathered, names, l):
    p = {}
    for n in names:
        fn, outs = ARRANGERS[n]
        if fn is _cat_rows:
            p[outs[0][0]] = gathered[n].reshape(outs[0][1])
            continue
        for (key, _), a in zip(outs, arrange(gathered[n], fn, [s for _, s in outs], f"arrange_{n}_{l}")):
            p[key] = a
    return p


def _small_operands(small, l):
    pw = small["pool_w"][l].astype(BF16)
    return dict(g_mix=small["norm_mix_g"][l][None], gq=small["q_norm_g"][l][None], gkv=small["kv_norm_g"][l][None],
                g_ffn=small["norm_ffn_g"][l][None], ps=small["pool_scale"][l][None], pw=pw)


class MeshComm:
    def __init__(self, w, meta_tokens):
        self.src = lambda n, l: w[n][l].astype(BF16)
        self.meta_tokens = meta_tokens
        self.core = lax.axis_index("c").astype(jnp.int32).reshape(1)
        self.rides = {0: [(n, 0) for n in FFN] + [(n, 1) for n in MIX], 1: [(n, 1) for n in FFN]}

    def first_weights(self):
        got = _gather_two_level([self.src(n, 0) for n in MIX_IN] + [self.meta_tokens], "gather_mix_0")
        return dict(zip(MIX_IN, got)), jnp.moveaxis(got[-1], 0, 1).reshape(N_META, D_MODEL)

    def early_first_hop(self):
        return exchange([self.src(n, 0) for n in MIX_OUT], False, FIRST_HOP_PEERS)

    def early_weights(self, landed):
        return dict(zip(MIX_OUT, second_hop(landed).run("second_hop_mix_0")))

    def first_hop(self, l):
        return exchange([self.src(n, layer) for n, layer in self.rides[l]], False, FIRST_HOP_PEERS)

    def second_hop(self, l, landed):
        return second_hop(landed)

    def carried(self, l, full, names, layer):
        return {n: full[self.rides[l].index((n, layer))] for n in names}

    def pair_exchange(self, own):
        return to_sibling(own)

    def pair_add(self, own, theirs, names, tag):
        return [pair_add(a, b, self.core, name=f"pair_add_{n}_{tag}") for n, a, b in zip(names, own, theirs)]

    def last_pair_exchange(self, own, small):
        got = combine(to_sibling(own), exchange([small], False, ALL_PEERS)).run("pair_grads_mix_0")
        return got[:-1], got[-1]

    def scatter(self, sums):
        return exchange(sums, True, CHIP_PEERS, by_chip=True)


HEADS_FWD, HEADS_BWD = 8, 4
TILE_ROWS, TILE_ROWS_BWD = 512, 256


def _tile(t, target):
    n = max(1, -(-t // (target + target // 8)))
    while t % n or (t // n) % 16:
        n += 1
    return t // n


def _wgrad_tile(t):
    return max(tm for tm in (2 * TQ, TQ, LANES) if t % tm == 0)


def _ffn_bwd_part(dh2, p, s, tag, ride):
    d, ff = D_MODEL, D_FF // N_DEV
    t = dh2.shape[0]
    wg_ = lambda n, x, ys, fn, shape: wgrad(x, ys, fn, shape, tm=_wgrad_tile(t), name=f"wgrad_{n}_{tag}")
    (dh1, hn2, act, dgt, dup, dg_ffn), brought = ffn_bwd(dh2, s["h1"], p["g_ffn"], s["gt"], s["up"], p["wgt"], p["wut"], p["wd"],
                                                         tm=_tile(t, TILE_ROWS_BWD), name=f"ffn_bwd_{tag}", ride=ride)
    chunks = [wg_("gate", hn2, [dgt], _chunks_cols_transposed, (ff, d)), wg_("up", hn2, [dup], _chunks_cols_transposed, (ff, d)),
              wg_("down", act, [dh2], _chunks_rows, (ff, d))]
    return dh1, chunks, dict(norm_ffn_g=dg_ffn[0]), brought


def _mix_bwd_part(dh1, p, s, rope, nb, lp, tag, ride, next_ride):
    d = D_MODEL
    t = dh1.shape[0]
    wg_ = lambda n, x, ys, fn, shape: wgrad(x, ys, fn, shape, tm=_wgrad_tile(t), name=f"wgrad_{n}_{tag}")
    (dga, dgb, dpa, dpb, do, dpool, dps, dpw), first = merge_bwd(dh1, s["z"], s["pa"], s["pb"], s["pooled"], p["pw"], p["ps"],
                                                                   p["wpa"], p["wpb"], p["wo"], tm=_tile(t, TILE_ROWS),
                                                                   name=f"merge_bwd_{tag}", ride=ride)
    c_o = wg_("o", s["mg"], [dh1], _chunks_rows, (d // N_DEV, d))
    c_pa = wg_("pa", s["a"], [dpa], _chunks_cols, (POOL_WIDTH, d // N_DEV))
    c_pb = wg_("pb", s["o"], [dpb], _chunks_rows, (d // N_DEV, d))
    (dq, dk, dv), brought = attn_bwd(s["q"], s["k"], s["v"], s["o"], do, s["lse"], nb=nb, lp=lp, hb=HEADS_BWD,
                                     name=f"attn_bwd_{tag}", ride=next_ride(first))
    dh, hn, dz, cqn, ckvn, dqb, dkb, dvb, dg_mix, dgq, dgkv = in_proj_bwd(
        dh1, s["h"], p["g_mix"], s["z"], dq, dk, dv, dga, dgb, dpool, p["win"], p["gq"], p["gkv"], p["wuq"], p["wuk"], p["wuv"],
        rope, tm=_tile(t, TILE_ROWS_BWD), lp=lp, nb=nb, name=f"in_proj_bwd_{tag}")
    c_in = wg_("in", hn, [dz], _chunks_w_in, (d, D_IN // N_DEV))
    c_uq = wg_("uq", cqn, [dqb], _chunks_w_uq, (Q_RANK, N_HEADS * HEAD_QK // N_DEV))
    c_ukv = wg_("ukv", ckvn, [dkb, dvb], _chunks_w_ukv, (KV_RANK, N_HEADS * (QK_NOPE + V_DIM) // N_DEV))
    small = dict(pool_scale=dps[0], pool_w=dpw, norm_mix_g=dg_mix[0], q_norm_g=dgq[0], kv_norm_g=dgkv[0])
    return dh, [c_in, c_uq, c_ukv, c_pa, c_pb, c_o], small, brought


def train_step(x, loss_target, small, comm):
    nb, seq, d = x.shape
    lp = -(-(N_META + seq) // LANES) * LANES
    t = nb * lp
    assert nb <= 2 and DEPTH == 2
    tm = _tile(t, TILE_ROWS)
    rope = _rope_table(lp, nb)
    gathered, meta = comm.first_weights()
    pad = jnp.zeros((nb, lp - N_META - seq, d), F32)
    h = jnp.concatenate([jnp.broadcast_to(meta[None], (nb, N_META, d)), x, pad], axis=1).reshape(t, d)
    target = jnp.concatenate([jnp.zeros((nb, N_META, d), F32), loss_target, pad], axis=1).reshape(t, d)

    params, saved, full = [], [], {}
    for l in range(DEPTH):
        p = _small_operands(small, l)
        p.update(_operands(gathered, MIX_IN, 0) if l == 0 else _operands(comm.carried(0, full[0], MIX, 1), MIX, 1))
        (z, q, k, v), early = in_proj_fwd(h, p["g_mix"], p["win"], p["gq"], p["gkv"], p["wuq"], p["wuk"], p["wuv"], rope, tm=tm,
                                          name=f"in_proj_fwd_{l}", ride=comm.early_first_hop() if l == 0 else None)
        (o, lse), landed = attn_fwd(q, k, v, nb=nb, lp=lp, hb=HEADS_FWD, name=f"attn_fwd_{l}", ride=comm.first_hop(l))
        if l == 0:
            p.update(_operands(comm.early_weights(early), MIX_OUT, 0))
        (h1, pooled, a, pa, pb, mg), full[l] = merge_fwd(h, z, o, p["pw"], p["ps"], p["wpa"], p["wpb"], p["wo"], tm=tm, lp=lp,
                                                          nb=nb, name=f"merge_fwd_{l}", ride=comm.second_hop(l, landed))
        p.update(_operands(comm.carried(l, full[l], FFN, l), FFN, l))
        h2, gt, up = ffn_fwd(h1, p["g_ffn"], p["wgt"], p["wut"], p["wd"], tm=tm, name=f"ffn_fwd_{l}")
        params.append(p)
        saved.append(dict(h=h, z=z, q=q, k=k, v=v, o=o, lse=lse, h1=h1, pooled=pooled, a=a, pa=pa, pb=pb, mg=mg, gt=gt, up=up))
        h = h2
    parts, dh, dgf = loss_head(h, small["final_norm_g"][None], target, tm=tm, lp=lp, nb=nb, seq=seq, name="loss_head")
    loss = jnp.sum(parts[::8, 0])

    sums = {}
    dh, c_ffn1, small1, _ = _ffn_bwd_part(dh, params[1], saved[1], 1, None)
    dh, c_mix1, sm, brought = _mix_bwd_part(
        dh, params[1], saved[1], rope, nb, lp, 1, comm.pair_exchange(c_ffn1),
        lambda theirs: comm.scatter(comm.pair_add(c_ffn1, theirs, FFN, "ffn_1")))
    small1.update(sm)
    sums.update({(n, 1): a for n, a in zip(FFN, brought)})
    dh, c_ffn0, small0, theirs = _ffn_bwd_part(dh, params[0], saved[0], 0, comm.pair_exchange(c_mix1))
    s_mix1 = comm.pair_add(c_mix1, theirs, MIX, "mix_1")
    dh, c_mix0, sm, brought = _mix_bwd_part(
        dh, params[0], saved[0], rope, nb, lp, 0, comm.pair_exchange(c_ffn0),
        lambda theirs: comm.scatter(s_mix1 + comm.pair_add(c_ffn0, theirs, FFN, "ffn_0")))
    small0.update(sm)
    sums.update({(n, l): a for (n, l), a in zip([(n, 1) for n in MIX] + [(n, 0) for n in FFN], brought)})
    dh = dh.reshape(nb, lp, d)
    dmeta = jnp.sum(dh[:, :N_META], axis=0)
    meta_chunks = jnp.transpose(dmeta.reshape(N_META, N_CHIPS, 2, d // N_DEV), (2, 1, 0, 3)).astype(BF16)
    small_grads = {n: jnp.stack([small0[n], small1[n]]) for n in small0}
    small_grads["final_norm_g"] = dgf[0]
    last_names = MIX + ("meta_tokens",)
    theirs, small_parts = comm.last_pair_exchange(c_mix0 + [meta_chunks], _pack([small_grads[n] for n in SMALL], 8))
    pending = dict(zip(last_names, comm.pair_add(c_mix0 + [meta_chunks], theirs, last_names, "mix_0")))
    return loss, dh[:, N_META:N_META + seq], sums, pending, small_grads, small_parts


def kernel(x, meta_tokens, norm_mix_g, w_in, pool_w, pool_scale, q_norm_g, kv_norm_g, w_uq, w_ukv, w_pa, w_pb, w_o, norm_ffn_g, w_gate, w_up, w_down, final_norm_g, loss_target, m_meta_tokens, m_norm_mix_g, m_w_in, m_pool_w, m_pool_scale, m_q_norm_g, m_kv_norm_g, m_w_uq, m_w_ukv, m_w_pa, m_w_pb, m_w_o, m_norm_ffn_g, m_w_gate, m_w_up, m_w_down, m_final_norm_g, v_meta_tokens, v_norm_mix_g, v_w_in, v_pool_w, v_pool_scale, v_q_norm_g, v_kv_norm_g, v_w_uq, v_w_ukv, v_w_pa, v_w_pb, v_w_o, v_norm_ffn_g, v_w_gate, v_w_up, v_w_down, v_final_norm_g):
    args = dict(locals())
    w = {n: args[n] for n in WEIGHTS}
    m = {n: args["m_" + n] for n in WEIGHTS}
    v = {n: args["v_" + n] for n in WEIGHTS}
    small = {n: w[n] for n in SMALL}
    as_handled = lambda a, n: jnp.swapaxes(a, 1, 2) if n in TRANSPOSED else a
    wh, mh, vh = ({n: as_handled(d[n], n) for n, _ in BIG} for d in (w, m, v))

    comm = MeshComm(wh, meta_tokens)
    loss, grad_x, sums, pending, _, small_recv = train_step(x, loss_target, small, comm)
    loss = lax.psum(loss, ("x", "y", "c"))

    done, last = adamw_group([wh[n] for n in FFN], [mh[n] for n in FFN], [vh[n] for n in FFN],
                             [[sums[(n, l)] for l in range(DEPTH)] for n in FFN], name="adamw_ffn",
                             ride=comm.scatter(list(pending.values())))
    sums.update({(n, 0): a for n, a in zip(pending, last)})
    out = {n: [as_handled(a, n) for a in kinds] for n, kinds in zip(FFN, done)}
    out.update({n: adamw(wh[n], mh[n], vh[n], [sums[(n, l)] for l in range(DEPTH)], name=f"adamw_{n}") for n in MIX})
    out["meta_tokens"] = [a[0] for a in adamw(meta_tokens[None], m["meta_tokens"][None], v["meta_tokens"][None],
                                              [sums[("meta_tokens", 0)]], name="adamw_meta_tokens")]
    pk = lambda d: _pack([d[n] for n in SMALL], 8)[None]
    packed = adamw(pk(w), pk(m), pk(v), [small_recv], name="adamw_small")
    shapes = [w[n].shape for n in SMALL]
    for n, *kinds in zip(SMALL, *[_unpack(packed[kind][0], shapes) for kind in range(4)]):
        out[n] = kinds
    return (loss, grad_x, *[out[n][kind] for kind in range(4) for n in WEIGHTS])
```

```python
import functools
import math

import jax
import jax.numpy as jnp
from jax import lax
from jax.experimental import pallas as pl
from jax.experimental.pallas import tpu as pltpu

F32, BF16 = jnp.float32, jnp.bfloat16

D_MODEL = 1024
N_META = 16
N_HEADS = 16
QK_NOPE, QK_ROPE, V_DIM = 64, 32, 64
HALF_ROPE = QK_ROPE // 2
Q_RANK, KV_RANK = 256, 128
POOL_WINDOWS = (2, 4, 8, 16)
POOL_GROUP = 128
POOL_WIDTH = POOL_GROUP * len(POOL_WINDOWS)
POOL_HALO = 16
D_FF = 2816
D_IN = 2976
NORM_EPS = 1e-6
SM_SCALE = (QK_NOPE + QK_ROPE) ** -0.5
LOG2E = math.log2(math.e)
EXP2_SCALE = SM_SCALE * LOG2E
MASK_VALUE = -1e30
ROPE_THETA = 10000.0
DEPTH = 2
N_DEV = 8

ADAM_LR, ADAM_B1, ADAM_B2, ADAM_EPS, ADAM_WD, ADAM_STEP = 0.001, 0.9, 0.999, 1e-08, 0.01, 10

LANES = 128
HEAD_SLOT = LANES
QK_WIDTH = N_HEADS * HEAD_SLOT
Z_CQ, Z_CKV, Z_KR, Z_GA, Z_GB, DZ = 512, 768, 896, 1024, 2048, 3072
TQ = TK = 256
VMEM_LIMIT = 56 * 1024 * 1024


def _cparams():
    return pltpu.CompilerParams(vmem_limit_bytes=VMEM_LIMIT)


def _rows(tm, width, col=0):
    return pl.BlockSpec((tm, width), lambda i: (i, col))


def _whole(shape):
    zeros = (0,) * len(shape)
    return pl.BlockSpec(shape, lambda i: zeros, pipeline_mode=pl.Buffered(1))


def _acc(shape):
    zeros = (0,) * len(shape)
    return pl.BlockSpec(shape, lambda i: zeros)


def _dot(a, b):
    return jnp.dot(a, b, preferred_element_type=F32)


def _dot_tn(a, b):
    return lax.dot_general(a, b, (((0,), (0,)), ((), ())), preferred_element_type=F32)


def _dot_nt(a, b):
    return lax.dot_general(a, b, (((1,), (1,)), ((), ())), preferred_element_type=F32)


def _rms(x):
    r = lax.rsqrt(jnp.mean(x * x, axis=-1, keepdims=True) + NORM_EPS)
    return x * r, r


def _rms_bwd(dy, xhat, r, g):
    dg = jnp.sum(dy * xhat, axis=0, keepdims=True)
    dxh = dy * g
    dx = r * (dxh - xhat * jnp.mean(dxh * xhat, axis=-1, keepdims=True))
    return dx, dg


def _sigmoid(x):
    return 1.0 / (1.0 + jnp.exp(-x))


def _rope_fwd(q, c, s1, s2):
    w = q.shape[1]
    return q * c + pltpu.roll(q, w - HALF_ROPE, 1) * s1 + pltpu.roll(q, HALF_ROPE, 1) * s2


def _rope_bwd(dq, c, s1, s2):
    w = dq.shape[1]
    return dq * c + pltpu.roll(dq * s1, HALF_ROPE, 1) + pltpu.roll(dq * s2, w - HALF_ROPE, 1)


def _rope_tables(rope, reps):
    c, cr, s1, s2 = (rope[:, k * LANES:(k + 1) * LANES] for k in range(4))
    if reps > 1:
        return jnp.tile(c, (1, reps)), jnp.tile(s1, (1, reps)), jnp.tile(s2, (1, reps))
    return cr, s1, s2


def _seq_pos(gi, lp, nb):
    pos = gi
    for b in range(1, nb):
        pos = jnp.where(gi >= b * lp, gi - b * lp, pos)
    return pos


_ANY = pl.BlockSpec(memory_space=pl.ANY)


def _carrying_call(body, ride, operands, *, name, grid, in_specs, out_specs, out_shape, scratch_shapes=()):
    n_in, n_out = len(in_specs), len(out_specs)
    if ride is None:
        out = pl.pallas_call(body, name=name, grid=grid, in_specs=in_specs, out_specs=out_specs, out_shape=out_shape,
                             scratch_shapes=list(scratch_shapes), compiler_params=_cparams())(*operands)
        return out, []
    ne = len(ride.arrays)

    def carrying(*refs):
        ins, r_in, rest = refs[:n_in], refs[n_in:n_in + ne], refs[n_in + ne:]
        outs, r_out, rest = rest[:n_out], rest[n_out:n_out + ne], rest[n_out + ne:]
        scratch, sems = rest[:len(scratch_shapes)], rest[len(scratch_shapes):]
        ids = [pl.program_id(a) for a in range(len(grid))]
        first = functools.reduce(jnp.logical_and, [i == 0 for i in ids])
        last = functools.reduce(jnp.logical_and, [i == g - 1 for i, g in zip(ids, grid)])

        @pl.when(first)
        def _():
            ride.start(r_in, r_out, sems)

        body(*ins, *outs, *scratch)

        @pl.when(last)
        def _():
            ride.wait(r_in, r_out, sems)

    out = pl.pallas_call(
        carrying, name=name, grid=grid, in_specs=list(in_specs) + [_ANY] * ne, out_specs=list(out_specs) + [_ANY] * ne,
        out_shape=list(out_shape) + ride.out_shapes, scratch_shapes=list(scratch_shapes) + ride.scratch,
        input_output_aliases=ride.aliases(n_in, n_out), compiler_params=_cparams(),
    )(*operands, *ride.arrays)
    return out[:n_out], out[n_out:]


def in_proj_fwd(h, g_mix, win, gq, gkv, wuq, wuk, wuv, rope, *, tm, name, ride=None):
    t = h.shape[0]

    def body(h_ref, g_ref, win_ref, gq_ref, gkv_ref, wuq_ref, wuk_ref, wuv_ref, rope_ref, z_ref, q_ref, k_ref, v_ref):
        xhat, _ = _rms(h_ref[...])
        hn = (xhat * g_ref[...]).astype(BF16)
        z = _dot(hn, win_ref[...])
        z_ref[...] = z
        rope_t = rope_ref[...]
        xq, _ = _rms(z[:, Z_CQ:Z_CKV])
        cqn = (xq * gq_ref[...]).astype(BF16)
        q = _rope_fwd(_dot(cqn, wuq_ref[...]), *_rope_tables(rope_t, N_HEADS))
        q_ref[...] = q.astype(BF16)
        xkv, _ = _rms(z[:, Z_CKV:Z_KR])
        ckvn = (xkv * gkv_ref[...]).astype(BF16)
        kr = _rope_fwd(z[:, Z_KR:Z_GA], *_rope_tables(rope_t, 1))
        k_ref[...] = (_dot(ckvn, wuk_ref[...]) + jnp.tile(kr, (1, N_HEADS))).astype(BF16)
        v_ref[...] = _dot(ckvn, wuv_ref[...]).astype(BF16)

    return _carrying_call(
        body, ride, (h, g_mix, win, gq, gkv, wuq, wuk, wuv, rope), name=name, grid=(t // tm,),
        in_specs=[_rows(tm, D_MODEL), _whole((1, D_MODEL)), _whole((D_MODEL, DZ)), _whole((1, Q_RANK)), _whole((1, KV_RANK)),
                  _whole((Q_RANK, QK_WIDTH)), _whole((KV_RANK, QK_WIDTH)), _whole((KV_RANK, D_MODEL)), _rows(tm, 4 * LANES)],
        out_specs=[_rows(tm, DZ), _rows(tm, QK_WIDTH), _rows(tm, QK_WIDTH), _rows(tm, D_MODEL)],
        out_shape=[jax.ShapeDtypeStruct((t, DZ), F32), jax.ShapeDtypeStruct((t, QK_WIDTH), BF16),
                   jax.ShapeDtypeStruct((t, QK_WIDTH), BF16), jax.ShapeDtypeStruct((t, D_MODEL), BF16)])


def attn_fwd(q, k, v, *, nb, lp, hb, name, ride=None):
    t = q.shape[0]
    nq, tail = lp // TQ, lp % TQ
    assert tail % LANES == 0

    def body(q_ref, k_ref, v_ref, o_ref, lse_ref, vt):
        for pr in range(hb // 2):
            vt[pr] = v_ref[:, pr * LANES:(pr + 1) * LANES].T

        def q_block(qs, tq, whole_k):
            qh = [q_ref[pl.ds(qs, tq), hd * HEAD_SLOT:(hd + 1) * HEAD_SLOT] for hd in range(hb)]
            keep = lax.broadcasted_iota(jnp.int32, (tq, tq), 0) <= lax.broadcasted_iota(jnp.int32, (tq, tq), 1)

            def k_steps(blocks, c, masked):
                sts = [[_dot_nt(k_ref[pl.ds(ks, tk), hd * HEAD_SLOT:(hd + 1) * HEAD_SLOT], qh[hd]) for hd in range(hb)]
                       for ks, tk in blocks]
                for (ks, tk), st_b in zip(blocks, sts):
                    ps, stats = [], []
                    for hd in range(hb):
                        m, l, _ = c[hd]
                        st = jnp.where(keep, st_b[hd], MASK_VALUE) if masked else st_b[hd]
                        m_new = jnp.maximum(m, jnp.max(st, axis=0, keepdims=True))
                        p = jnp.exp2((st - m_new) * EXP2_SCALE)
                        alpha = jnp.exp2((m - m_new) * EXP2_SCALE)
                        ps.append(p.astype(BF16))
                        stats.append((m_new, alpha * l + jnp.sum(p, axis=0, keepdims=True), alpha))
                    pvs = [_dot(vt[hd // 2, :, pl.ds(ks, tk)], ps[hd]) for hd in range(hb)]
                    c = tuple((stats[hd][0], stats[hd][1], stats[hd][2] * c[hd][2] + pvs[hd]) for hd in range(hb))
                return c

            def two_blocks(i, c):
                ks = pl.multiple_of(2 * i * TK, TK)
                return k_steps([(ks, TK), (ks + TK, TK)], c, False)

            init = tuple((jnp.full((1, tq), MASK_VALUE, F32), jnp.zeros((1, tq), F32), jnp.zeros((LANES, tq), F32))
                         for _ in range(hb))
            pairs = lax.div(whole_k, 2)
            c = lax.fori_loop(0, pairs, two_blocks, init)
            c = lax.fori_loop(2 * pairs, whole_k, lambda kj, c: k_steps([(pl.multiple_of(kj * TK, TK), TK)], c, False), c)
            c = k_steps([(qs, tq)], c, True)
            sub = lax.broadcasted_iota(jnp.int32, (LANES, tq), 0)
            for pr in range(hb // 2):
                (m0, l0, a0), (m1, l1, a1) = c[2 * pr], c[2 * pr + 1]
                o_ref[pl.ds(qs, tq), pr * LANES:(pr + 1) * LANES] = jnp.where(sub < V_DIM, a0 / l0, a1 / l1).T.astype(BF16)
                lse_ref[2 * pr, :, pl.ds(qs, tq)] = m0 * SM_SCALE + jnp.log(l0)
                lse_ref[2 * pr + 1, :, pl.ds(qs, tq)] = m1 * SM_SCALE + jnp.log(l1)

        def whole_q_block(qi, carry):
            q_block(pl.multiple_of(qi * TQ, TQ), TQ, qi)
            return carry

        lax.fori_loop(0, nq, whole_q_block, 0)
        if tail:
            q_block(nq * TQ, tail, nq)

    blk = lambda w: pl.BlockSpec((lp, w), lambda b, g: (b, g))
    return _carrying_call(
        body, ride, (q, k, v), name=name, grid=(nb, N_HEADS // hb),
        in_specs=[blk(hb * HEAD_SLOT), blk(hb * HEAD_SLOT), blk(hb * V_DIM)],
        out_specs=[blk(hb * V_DIM), pl.BlockSpec((hb, 1, lp), lambda b, g: (g, 0, b))],
        out_shape=[jax.ShapeDtypeStruct((t, D_MODEL), BF16), jax.ShapeDtypeStruct((N_HEADS, 1, t), F32)],
        scratch_shapes=[pltpu.VMEM((hb // 2, LANES, lp), BF16)])


def _pool_band_fwd(i, tm, lp, nb):
    r = lax.broadcasted_iota(jnp.int32, (tm, POOL_HALO + tm), 0)
    e = lax.broadcasted_iota(jnp.int32, (tm, POOL_HALO + tm), 1)
    diff = r + POOL_HALO - e
    pos = _seq_pos(i * tm + lax.broadcasted_iota(jnp.int32, (tm, 1), 0), lp, nb)
    out = []
    for w in POOL_WINDOWS:
        cnt = jnp.minimum(pos + 1, w)
        band = jnp.where((diff >= 0) & (diff < cnt), 1.0, 0.0).astype(BF16)
        out.append((band, cnt.astype(F32)))
    return out


def merge_fwd(h, z, o, pw, ps, wpa, wpb, wo, *, tm, lp, nb, name, ride=None):
    t = h.shape[0]
    hb = tm // POOL_HALO

    def body(h_ref, u_ref, uprev_ref, ga_ref, gb_ref, o_ref, pw_ref, ps_ref, wpa_ref, wpb_ref, wo_ref,
             h1_ref, pooled_ref, a_ref, pa_ref, pb_ref, mg_ref):
        i = pl.program_id(0)
        u = u_ref[...]
        uext = jnp.concatenate([uprev_ref[...], u], axis=0).astype(BF16)
        pooled, ys = [], []
        for g, (band, cnt) in enumerate(_pool_band_fwd(i, tm, lp, nb)):
            gs = slice(g * POOL_GROUP, (g + 1) * POOL_GROUP)
            pg = (_dot(band, uext[:, gs]) / cnt - u[:, gs]).astype(BF16)
            pooled.append(pg)
            ys.append(_dot(pg, pw_ref[g]))
        pooled_ref[...] = jnp.concatenate(pooled, axis=1)
        a = (jnp.concatenate(ys, axis=1) * ps_ref[...]).astype(BF16)
        a_ref[...] = a
        pa = _dot(a, wpa_ref[...])
        pb = _dot(o_ref[...], wpb_ref[...])
        pa_ref[...] = pa.astype(BF16)
        pb_ref[...] = pb.astype(BF16)
        mg = (_sigmoid(ga_ref[...]) * pa + _sigmoid(gb_ref[...]) * pb).astype(BF16)
        mg_ref[...] = mg
        h1_ref[...] = h_ref[...] + _dot(mg, wo_ref[...])

    halo = pl.BlockSpec((POOL_HALO, POOL_WIDTH), lambda i: (jnp.maximum(i * hb - 1, 0), 0))
    return _carrying_call(
        body, ride, (h, z, z, z, z, o, pw, ps, wpa, wpb, wo), name=name, grid=(t // tm,),
        in_specs=[_rows(tm, D_MODEL), _rows(tm, POOL_WIDTH), halo, _rows(tm, D_MODEL, 1), _rows(tm, D_MODEL, 2), _rows(tm, D_MODEL),
                  _whole((4, POOL_GROUP, POOL_GROUP)), _whole((1, POOL_WIDTH)), _whole((POOL_WIDTH, D_MODEL)),
                  _whole((D_MODEL, D_MODEL)), _whole((D_MODEL, D_MODEL))],
        out_specs=[_rows(tm, D_MODEL), _rows(tm, POOL_WIDTH), _rows(tm, POOL_WIDTH), _rows(tm, D_MODEL), _rows(tm, D_MODEL),
                   _rows(tm, D_MODEL)],
        out_shape=[jax.ShapeDtypeStruct((t, D_MODEL), F32), jax.ShapeDtypeStruct((t, POOL_WIDTH), BF16),
                   jax.ShapeDtypeStruct((t, POOL_WIDTH), BF16), jax.ShapeDtypeStruct((t, D_MODEL), BF16),
                   jax.ShapeDtypeStruct((t, D_MODEL), BF16), jax.ShapeDtypeStruct((t, D_MODEL), BF16)])


def ffn_fwd(h1, g, wgt, wut, wd, *, tm, name):
    t = h1.shape[0]

    def body(h_ref, g_ref, wgt_ref, wut_ref, wd_ref, h2_ref, gt_ref, up_ref):
        h = h_ref[...]
        xhat, _ = _rms(h)
        hn = (xhat * g_ref[...]).astype(BF16)
        gt = _dot_nt(hn, wgt_ref[...])
        up = _dot_nt(hn, wut_ref[...])
        gt_ref[...] = gt.astype(BF16)
        up_ref[...] = up.astype(BF16)
        act = (gt * _sigmoid(gt) * up).astype(BF16)
        h2_ref[...] = h + _dot(act, wd_ref[...])

    return pl.pallas_call(
        body, name=name, grid=(t // tm,),
        in_specs=[_rows(tm, D_MODEL), _whole((1, D_MODEL)), _whole((D_FF, D_MODEL)), _whole((D_FF, D_MODEL)), _whole((D_FF, D_MODEL))],
        out_specs=[_rows(tm, D_MODEL), _rows(tm, D_FF), _rows(tm, D_FF)],
        out_shape=[jax.ShapeDtypeStruct((t, D_MODEL), F32), jax.ShapeDtypeStruct((t, D_FF), BF16), jax.ShapeDtypeStruct((t, D_FF), BF16)],
        compiler_params=_cparams(),
    )(h1, g, wgt, wut, wd)


def loss_head(h, g, target, *, tm, lp, nb, seq, name):
    t = h.shape[0]
    nt = t // tm

    def body(h_ref, g_ref, t_ref, loss_ref, dh_ref, dg_ref):
        i = pl.program_id(0)
        pos = _seq_pos(i * tm + lax.broadcasted_iota(jnp.int32, (tm, 1), 0), lp, nb)
        real = (pos >= N_META) & (pos < N_META + seq)
        xhat, r = _rms(h_ref[...])
        gg = g_ref[...]
        err = jnp.where(real, xhat * gg - t_ref[...], 0.0)
        loss_ref[...] = jnp.full((8, LANES), 0.5 * jnp.sum(err * err) / D_MODEL, F32)
        dx, dg = _rms_bwd(err * (1.0 / D_MODEL), xhat, r, gg)
        dh_ref[...] = dx

        @pl.when(i == 0)
        def _():
            dg_ref[...] = jnp.zeros_like(dg_ref)

        dg_ref[...] += dg

    return pl.pallas_call(
        body, name=name, grid=(nt,),
        in_specs=[_rows(tm, D_MODEL), _whole((1, D_MODEL)), _rows(tm, D_MODEL)],
        out_specs=[pl.BlockSpec((8, LANES), lambda i: (i, 0)), _rows(tm, D_MODEL), _acc((1, D_MODEL))],
        out_shape=[jax.ShapeDtypeStruct((nt * 8, LANES), F32), jax.ShapeDtypeStruct((t, D_MODEL), F32),
                   jax.ShapeDtypeStruct((1, D_MODEL), F32)],
        compiler_params=_cparams(),
    )(h, g, target)


def wgrad(x, ys, chunk_fn, chunk_shape, *, tm, name, ride=None):
    t, m = x.shape
    tiles = t // tm
    steps = -(-tiles // 2)

    def body(*refs):
        ins, o_ref, accs = refs[:2 * (1 + len(ys))], refs[2 * (1 + len(ys))], refs[2 * (1 + len(ys)) + 1:]
        i = pl.program_id(0)

        @pl.when(i == 0)
        def _():
            for acc in accs:
                acc[...] = jnp.zeros_like(acc)

        def both(first, second, mask):
            b = second[...].astype(BF16)
            if mask and tiles % 2:
                b = jnp.where(2 * i + 1 < tiles, b, jnp.zeros_like(b))
            return jnp.concatenate([first[...].astype(BF16), b], axis=0)

        xb = both(ins[0], ins[1], True)
        for j, acc in enumerate(accs):
            acc[...] += _dot_tn(xb, both(ins[2 + 2 * j], ins[3 + 2 * j], False))

        @pl.when(i == steps - 1)
        def _():
            for p, chunk in enumerate(chunk_fn(*accs)):
                o_ref[p % 2, p // 2] = chunk.astype(BF16)

    def two_tiles(width):
        return [pl.BlockSpec((tm, width), lambda i: (2 * i, 0)),
                pl.BlockSpec((tm, width), lambda i: (jnp.minimum(2 * i + 1, tiles - 1), 0))]

    out = (2, N_DEV // 2) + tuple(chunk_shape)
    operands = [x, x] + [a for y in ys for a in (y, y)]
    (chunks,), brought = _carrying_call(
        body, ride, operands, name=name, grid=(steps,),
        in_specs=two_tiles(m) + [s for y in ys for s in two_tiles(y.shape[1])], out_specs=[_acc(out)],
        out_shape=[jax.ShapeDtypeStruct(out, BF16)], scratch_shapes=[pltpu.VMEM((m, y.shape[1]), F32) for y in ys])
    return chunks, brought


def ffn_bwd(dh2, h1, g, gt, up, wgt, wut, wd, *, tm, name, ride=None):
    t = h1.shape[0]

    def body(dh2_ref, h_ref, g_ref, gt_ref, up_ref, wgt_ref, wut_ref, wd_ref, dh1_ref, hn_ref, act_ref, dgt_ref, dup_ref, dg_ref):
        dh2 = dh2_ref[...]
        dact = _dot_nt(dh2.astype(BF16), wd_ref[...])
        gt = gt_ref[...].astype(F32)
        up = up_ref[...].astype(F32)
        sg = _sigmoid(gt)
        silu = gt * sg
        act_ref[...] = (silu * up).astype(BF16)
        dgt = (dact * up * (sg * (1.0 + gt * (1.0 - sg)))).astype(BF16)
        dup = (dact * silu).astype(BF16)
        dgt_ref[...] = dgt
        dup_ref[...] = dup
        dhn = _dot(dgt, wgt_ref[...]) + _dot(dup, wut_ref[...])
        xhat, r = _rms(h_ref[...])
        gg = g_ref[...]
        hn_ref[...] = (xhat * gg).astype(BF16)
        dx, dg = _rms_bwd(dhn, xhat, r, gg)
        dh1_ref[...] = dh2 + dx

        @pl.when(pl.program_id(0) == 0)
        def _():
            dg_ref[...] = jnp.zeros_like(dg_ref)

        dg_ref[...] += dg

    return _carrying_call(
        body, ride, (dh2, h1, g, gt, up, wgt, wut, wd), name=name, grid=(t // tm,),
        in_specs=[_rows(tm, D_MODEL), _rows(tm, D_MODEL), _whole((1, D_MODEL)), _rows(tm, D_FF), _rows(tm, D_FF),
                  _whole((D_FF, D_MODEL)), _whole((D_FF, D_MODEL)), _whole((D_FF, D_MODEL))],
        out_specs=[_rows(tm, D_MODEL), _rows(tm, D_MODEL), _rows(tm, D_FF), _rows(tm, D_FF), _rows(tm, D_FF), _acc((1, D_MODEL))],
        out_shape=[jax.ShapeDtypeStruct((t, D_MODEL), F32), jax.ShapeDtypeStruct((t, D_MODEL), BF16),
                   jax.ShapeDtypeStruct((t, D_FF), BF16), jax.ShapeDtypeStruct((t, D_FF), BF16),
                   jax.ShapeDtypeStruct((t, D_FF), BF16), jax.ShapeDtypeStruct((1, D_MODEL), F32)])


def merge_bwd(dh1, z, pa, pb, pooled, pw, ps, wpa, wpb, wo, *, tm, name, ride=None):
    t = dh1.shape[0]

    def body(dh1_ref, ga_ref, gb_ref, pa_ref, pb_ref, pooled_ref, pw_ref, ps_ref, wpa_ref, wpb_ref, wo_ref,
             dga_ref, dgb_ref, dpa_ref, dpb_ref, do_ref, dpool_ref, dps_ref, dpw_ref):
        dmg = _dot_nt(dh1_ref[...].astype(BF16), wo_ref[...])
        sa = _sigmoid(ga_ref[...])
        sb = _sigmoid(gb_ref[...])
        dga_ref[...] = (dmg * pa_ref[...].astype(F32) * sa * (1.0 - sa)).astype(BF16)
        dgb_ref[...] = (dmg * pb_ref[...].astype(F32) * sb * (1.0 - sb)).astype(BF16)
        dpa = (dmg * sa).astype(BF16)
        dpb = (dmg * sb).astype(BF16)
        dpa_ref[...] = dpa
        dpb_ref[...] = dpb
        do_ref[...] = _dot_nt(dpb, wpb_ref[...]).astype(BF16)
        da = _dot_nt(dpa, wpa_ref[...])
        pooled = pooled_ref[...]
        ps = ps_ref[...]

        @pl.when(pl.program_id(0) == 0)
        def _():
            dps_ref[...] = jnp.zeros_like(dps_ref)
            dpw_ref[...] = jnp.zeros_like(dpw_ref)

        dps, dpool = [], []
        for g in range(len(POOL_WINDOWS)):
            gs = slice(g * POOL_GROUP, (g + 1) * POOL_GROUP)
            y = _dot(pooled[:, gs], pw_ref[g])
            dps.append(jnp.sum(da[:, gs] * y, axis=0, keepdims=True))
            dy = (da[:, gs] * ps[:, gs]).astype(BF16)
            dpool.append(_dot_nt(dy, pw_ref[g]))
            dpw_ref[g] += _dot_tn(pooled[:, gs], dy)
        dps_ref[...] += jnp.concatenate(dps, axis=1)
        dpool_ref[...] = jnp.concatenate(dpool, axis=1)

    return _carrying_call(
        body, ride, (dh1, z, z, pa, pb, pooled, pw, ps, wpa, wpb, wo), name=name, grid=(t // tm,),
        in_specs=[_rows(tm, D_MODEL), _rows(tm, D_MODEL, 1), _rows(tm, D_MODEL, 2), _rows(tm, D_MODEL), _rows(tm, D_MODEL),
                  _rows(tm, POOL_WIDTH), _whole((4, POOL_GROUP, POOL_GROUP)),
                  _whole((1, POOL_WIDTH)), _whole((POOL_WIDTH, D_MODEL)), _whole((D_MODEL, D_MODEL)), _whole((D_MODEL, D_MODEL))],
        out_specs=[_rows(tm, D_MODEL), _rows(tm, D_MODEL), _rows(tm, D_MODEL), _rows(tm, D_MODEL), _rows(tm, D_MODEL),
                   _rows(tm, POOL_WIDTH), _acc((1, POOL_WIDTH)), _acc((4, POOL_GROUP, POOL_GROUP))],
        out_shape=[jax.ShapeDtypeStruct((t, D_MODEL), BF16)] * 5
        + [jax.ShapeDtypeStruct((t, POOL_WIDTH), F32), jax.ShapeDtypeStruct((1, POOL_WIDTH), F32),
           jax.ShapeDtypeStruct((4, POOL_GROUP, POOL_GROUP), F32)])


def attn_bwd(q, k, v, o, do, lse, *, nb, lp, hb, name, ride=None):
    t = q.shape[0]
    nq, tail = lp // TQ, lp % TQ
    assert tail % LANES == 0

    def body(q_ref, k_ref, v_ref, o_ref, do_ref, lse_ref, dq_ref, dk_ref, dv_ref, kt, doh, lse_row, delta_row, dqt):
        lane = lax.broadcasted_iota(jnp.int32, (lp, LANES), 1)
        first = lane < V_DIM
        sub = lax.broadcasted_iota(jnp.int32, (LANES, lp), 0)
        for pr in range(hb // 2):
            ls = slice(pr * LANES, (pr + 1) * LANES)
            do = do_ref[:, ls]
            doh[2 * pr] = jnp.where(first, do, jnp.zeros_like(do))
            doh[2 * pr + 1] = jnp.where(first, jnp.zeros_like(do), do)
            prod_t = (do.astype(F32) * o_ref[:, ls].astype(F32)).T
            delta_row[2 * pr] = jnp.sum(jnp.where(sub < V_DIM, prod_t, 0.0), axis=0, keepdims=True)
            delta_row[2 * pr + 1] = jnp.sum(jnp.where(sub < V_DIM, 0.0, prod_t), axis=0, keepdims=True)
        for hd in range(hb):
            lse_row[hd] = lse_ref[hd] * LOG2E
            kt[hd] = k_ref[:, hd * HEAD_SLOT:(hd + 1) * HEAD_SLOT].T
        dqt[...] = jnp.zeros(dqt.shape, F32)
        heads = range(hb)
        hss = [slice(hd * HEAD_SLOT, (hd + 1) * HEAD_SLOT) for hd in heads]

        def k_block(ks, tk, next_q):
            keep = lax.broadcasted_iota(jnp.int32, (tk, tk), 0) <= lax.broadcasted_iota(jnp.int32, (tk, tk), 1)

            def q_steps(blocks, c, masked):
                work = [(qs, tq, hd) for qs, tq in blocks for hd in heads]
                qhs = [q_ref[pl.ds(qs, tq), hss[hd]] for qs, tq, hd in work]
                dos = [doh[hd, pl.ds(qs, tq), :] for qs, tq, hd in work]
                sts = [_dot_nt(k_ref[pl.ds(ks, tk), hss[hd]], qhs[i]) for i, (_, _, hd) in enumerate(work)]
                dpts = [_dot_nt(v_ref[pl.ds(ks, tk), (hd // 2) * LANES:(hd // 2 + 1) * LANES], dos[i])
                        for i, (_, _, hd) in enumerate(work)]
                pts, dsts = [], []
                for i, (qs, tq, hd) in enumerate(work):
                    st = jnp.where(keep, sts[i], MASK_VALUE) if masked else sts[i]
                    pt = jnp.exp2(st * EXP2_SCALE - lse_row[hd, :, pl.ds(qs, tq)])
                    dsts.append((pt * (dpts[i] - delta_row[hd, :, pl.ds(qs, tq)])).astype(BF16))
                    pts.append(pt.astype(BF16))
                dvs = [_dot(pts[i], dos[i]) for i in range(len(work))]
                dks = [_dot(dsts[i], qhs[i]) for i in range(len(work))]
                dqs = [_dot(kt[hd, :, pl.ds(ks, tk)], dsts[i]) for i, (_, _, hd) in enumerate(work)]
                c = list(c)
                for i, (qs, tq, hd) in enumerate(work):
                    dqt[hd, :, pl.ds(qs, tq)] += dqs[i]
                    c[hd] = (c[hd][0] + dks[i], c[hd][1] + dvs[i])
                return tuple(c)

            zero = jnp.zeros((tk, LANES), F32)
            c = q_steps([(ks, tk)], tuple((zero, zero) for _ in heads), True)
            if next_q is not None:
                def two_blocks(i, c):
                    qs = pl.multiple_of((next_q + 2 * i) * TQ, TQ)
                    return q_steps([(qs, TQ), (qs + TQ, TQ)], c, False)

                pairs = lax.div(nq - next_q, 2)
                c = lax.fori_loop(0, pairs, two_blocks, c)
                c = lax.fori_loop(next_q + 2 * pairs, nq, lambda qi, c: q_steps([(pl.multiple_of(qi * TQ, TQ), TQ)], c, False), c)
                if tail:
                    c = q_steps([(nq * TQ, tail)], c, False)
            for hd in heads:
                dk_ref[pl.ds(ks, tk), hss[hd]] = c[hd][0] * SM_SCALE
            for pr in range(hb // 2):
                dv_ref[pl.ds(ks, tk), pr * LANES:(pr + 1) * LANES] = c[2 * pr][1] + c[2 * pr + 1][1]

        def whole_k_block(kj, carry):
            k_block(pl.multiple_of(kj * TK, TK), TK, kj + 1)
            return carry

        lax.fori_loop(0, nq, whole_k_block, 0)
        if tail:
            k_block(nq * TQ, tail, None)
        for hd in range(hb):
            dq_ref[:, hd * HEAD_SLOT:(hd + 1) * HEAD_SLOT] = dqt[hd].T * SM_SCALE

    blk = lambda w: pl.BlockSpec((lp, w), lambda b, g: (b, g))
    return _carrying_call(
        body, ride, (q, k, v, o, do, lse), name=name, grid=(nb, N_HEADS // hb),
        in_specs=[blk(hb * HEAD_SLOT), blk(hb * HEAD_SLOT), blk(hb * V_DIM), blk(hb * V_DIM), blk(hb * V_DIM),
                  pl.BlockSpec((hb, 1, lp), lambda b, g: (g, 0, b))],
        out_specs=[blk(hb * HEAD_SLOT), blk(hb * HEAD_SLOT), blk(hb * V_DIM)],
        out_shape=[jax.ShapeDtypeStruct((t, QK_WIDTH), F32), jax.ShapeDtypeStruct((t, QK_WIDTH), F32),
                   jax.ShapeDtypeStruct((t, D_MODEL), F32)],
        scratch_shapes=[pltpu.VMEM((hb, HEAD_SLOT, lp), BF16), pltpu.VMEM((hb, lp, LANES), BF16), pltpu.VMEM((hb, 1, lp), F32),
                        pltpu.VMEM((hb, 1, lp), F32), pltpu.VMEM((hb, HEAD_SLOT, lp), F32)])


def in_proj_bwd(dh1, h, g_mix, z, dq, dk, dv, dga, dgb, dpool, win, gq, gkv, wuq, wuk, wuv, rope, *, tm, lp, nb, name):
    t = h.shape[0]
    hb = tm // POOL_HALO
    last_halo = t // POOL_HALO - 1

    def body(dh1_ref, h_ref, g_ref, zcq_ref, zckv_ref, dq_ref, dk_ref, dv_ref, dga_ref, dgb_ref, dpool_ref, dnext_ref,
             win_ref, gq_ref, gkv_ref, wuq_ref, wuk_ref, wuv_ref, rope_ref,
             dh_ref, hn_ref, dz_ref, cqn_ref, ckvn_ref, dqb_ref, dkb_ref, dvb_ref, dg_ref, dgq_ref, dgkv_ref):
        i = pl.program_id(0)
        rope_t = rope_ref[...]
        dqb = _rope_bwd(dq_ref[...], *_rope_tables(rope_t, N_HEADS)).astype(BF16)
        dqb_ref[...] = dqb
        xq, rq = _rms(zcq_ref[...])
        gq_v = gq_ref[...]
        cqn_ref[...] = (xq * gq_v).astype(BF16)
        dcq, dgq = _rms_bwd(_dot_nt(dqb, wuq_ref[...]), xq, rq, gq_v)
        dk = dk_ref[...]
        dkb = dk.astype(BF16)
        dvb = dv_ref[...].astype(BF16)
        dkb_ref[...] = dkb
        dvb_ref[...] = dvb
        xkv, rkv = _rms(zckv_ref[...])
        gkv_v = gkv_ref[...]
        ckvn_ref[...] = (xkv * gkv_v).astype(BF16)
        dckv, dgkv = _rms_bwd(_dot_nt(dkb, wuk_ref[...]) + _dot_nt(dvb, wuv_ref[...]), xkv, rkv, gkv_v)
        dks = dk[:, :HEAD_SLOT]
        for hd in range(1, N_HEADS):
            dks = dks + dk[:, hd * HEAD_SLOT:(hd + 1) * HEAD_SLOT]
        dzk = _rope_bwd(dks, *_rope_tables(rope_t, 1))
        dp_cur = dpool_ref[...]
        dp_ext = jnp.concatenate([dp_cur, dnext_ref[...]], axis=0)
        r = lax.broadcasted_iota(jnp.int32, (tm, tm + POOL_HALO), 0)
        e = lax.broadcasted_iota(jnp.int32, (tm, tm + POOL_HALO), 1)
        gt_col = i * tm + lax.broadcasted_iota(jnp.int32, (1, tm + POOL_HALO), 1)
        pos_col = _seq_pos(gt_col, lp, nb)
        gt_row = i * tm + lax.broadcasted_iota(jnp.int32, (tm + POOL_HALO, 1), 0)
        pos_row = _seq_pos(gt_row, lp, nb)
        dus = []
        for g, w in enumerate(POOL_WINDOWS):
            gs = slice(g * POOL_GROUP, (g + 1) * POOL_GROUP)
            band = jnp.where((e - r >= 0) & (e - r < jnp.minimum(pos_col + 1, w)) & (gt_col < t), 1.0, 0.0).astype(BF16)
            scaled = jnp.where(gt_row < t, dp_ext[:, gs] / jnp.minimum(pos_row + 1, w).astype(F32), 0.0).astype(BF16)
            dus.append(_dot(band, scaled) - dp_cur[:, gs])
        dz = jnp.concatenate(dus + [dcq, dckv, dzk], axis=1).astype(BF16)
        dz = jnp.concatenate([dz, dga_ref[...], dgb_ref[...]], axis=1)
        dz_ref[...] = dz
        xhat, rr = _rms(h_ref[...])
        gg = g_ref[...]
        hn_ref[...] = (xhat * gg).astype(BF16)
        dx, dg = _rms_bwd(_dot_nt(dz, win_ref[...]), xhat, rr, gg)
        dh_ref[...] = dh1_ref[...] + dx

        @pl.when(i == 0)
        def _():
            dg_ref[...] = jnp.zeros_like(dg_ref)
            dgq_ref[...] = jnp.zeros_like(dgq_ref)
            dgkv_ref[...] = jnp.zeros_like(dgkv_ref)

        dg_ref[...] += dg
        dgq_ref[...] += dgq
        dgkv_ref[...] += dgkv

    nxt = pl.BlockSpec((POOL_HALO, POOL_WIDTH), lambda i: (jnp.minimum((i + 1) * hb, last_halo), 0))
    return pl.pallas_call(
        body, name=name, grid=(t // tm,),
        in_specs=[_rows(tm, D_MODEL), _rows(tm, D_MODEL), _whole((1, D_MODEL)), _rows(tm, Q_RANK, Z_CQ // Q_RANK),
                  _rows(tm, KV_RANK, Z_CKV // KV_RANK), _rows(tm, QK_WIDTH), _rows(tm, QK_WIDTH), _rows(tm, D_MODEL),
                  _rows(tm, D_MODEL), _rows(tm, D_MODEL), _rows(tm, POOL_WIDTH), nxt,
                  _whole((D_MODEL, DZ)), _whole((1, Q_RANK)), _whole((1, KV_RANK)), _whole((Q_RANK, QK_WIDTH)),
                  _whole((KV_RANK, QK_WIDTH)), _whole((KV_RANK, D_MODEL)), _rows(tm, 4 * LANES)],
        out_specs=[_rows(tm, D_MODEL), _rows(tm, D_MODEL), _rows(tm, DZ), _rows(tm, Q_RANK), _rows(tm, KV_RANK),
                   _rows(tm, QK_WIDTH), _rows(tm, QK_WIDTH), _rows(tm, D_MODEL),
                   _acc((1, D_MODEL)), _acc((1, Q_RANK)), _acc((1, KV_RANK))],
        out_shape=[jax.ShapeDtypeStruct((t, D_MODEL), F32), jax.ShapeDtypeStruct((t, D_MODEL), BF16),
                   jax.ShapeDtypeStruct((t, DZ), BF16), jax.ShapeDtypeStruct((t, Q_RANK), BF16),
                   jax.ShapeDtypeStruct((t, KV_RANK), BF16), jax.ShapeDtypeStruct((t, QK_WIDTH), BF16),
                   jax.ShapeDtypeStruct((t, QK_WIDTH), BF16), jax.ShapeDtypeStruct((t, D_MODEL), BF16),
                   jax.ShapeDtypeStruct((1, D_MODEL), F32), jax.ShapeDtypeStruct((1, Q_RANK), F32),
                   jax.ShapeDtypeStruct((1, KV_RANK), F32)],
        compiler_params=_cparams(),
    )(dh1, h, g_mix, z, z, dq, dk, dv, dga, dgb, dpool, dpool, win, gq, gkv, wuq, wuk, wuv, rope)


_MESH = pl.DeviceIdType.MESH


def _place():
    x, y, c = lax.axis_index("x"), lax.axis_index("y"), lax.axis_index("c")
    return x, y, c, 4 * x + 2 * y + c


def _peer(x, y, c, k):
    px, py, pc = (1 - x) if k & 4 else x, (1 - y) if k & 2 else y, (1 - c) if k & 1 else c
    return (px, py, pc), 4 * px + 2 * py + pc


ALL_PEERS = tuple(range(1, N_DEV))
CHIP_PEERS = (2, 4, 6)
N_CHIPS = N_DEV // 2


def _sem_scratch(n, m):
    return [pltpu.SemaphoreType.DMA((n, m)), pltpu.SemaphoreType.DMA((n, m)), pltpu.SemaphoreType.DMA((n,))]


class Exchange:
    def __init__(self, arrays, out_shapes, sem_cols, plan, aliased=False):
        self.arrays, self.out_shapes, self.plan = list(arrays), list(out_shapes), plan
        self.scratch = _sem_scratch(len(self.arrays), sem_cols)
        self.aliased = aliased

    def split(self, refs):
        n = len(self.arrays)
        return refs[:n], refs[n:2 * n], refs[2 * n:]

    def start(self, srcs, dsts, sems):
        local, sends, _ = self.plan(srcs, dsts, *sems)
        for cp in local + sends:
            cp.start()

    def wait(self, srcs, dsts, sems):
        local, sends, recvs = self.plan(srcs, dsts, *sems)
        for cp in recvs:
            cp.wait_recv()
        for cp in sends:
            cp.wait_send()
        for cp in local:
            cp.wait()

    def aliases(self, first_in, first_out):
        return {first_in + j: first_out + j for j in range(len(self.arrays))} if self.aliased else {}

    def run(self, name):
        def body(*refs):
            srcs, dsts, sems = self.split(refs)
            self.start(srcs, dsts, sems)
            self.wait(srcs, dsts, sems)

        n = len(self.arrays)
        return pl.pallas_call(body, name=name, in_specs=[_ANY] * n, out_specs=[_ANY] * n, out_shape=self.out_shapes,
                              scratch_shapes=self.scratch, input_output_aliases=self.aliases(0, 0))(*self.arrays)


def exchange(arrays, scatter, peers, by_chip=False):
    slots = N_CHIPS if by_chip else N_DEV

    def plan(srcs, dsts, send_sems, recv_sems, local_sems):
        x, y, c, me = _place()
        mine = 2 * x + y if by_chip else me
        local = [pltpu.make_async_copy(src.at[mine] if scatter else src, dst.at[mine], local_sems.at[j])
                 for j, (src, dst) in enumerate(zip(srcs, dsts))]
        sends, recvs = [], []
        for t, k in enumerate(peers):
            peer, pidx = _peer(x, y, c, k)
            theirs = 2 * peer[0] + peer[1] if by_chip else pidx
            for j, (src, dst) in enumerate(zip(srcs, dsts)):
                part = src.at[theirs] if scatter else src
                sems = dict(send_sem=send_sems.at[j, t], recv_sem=recv_sems.at[j, t], device_id=peer, device_id_type=_MESH)
                sends.append(pltpu.make_async_remote_copy(src_ref=part, dst_ref=dst.at[mine], **sems))
                recvs.append(pltpu.make_async_remote_copy(src_ref=part, dst_ref=dst.at[theirs], **sems))
        return local, sends, recvs

    shapes = [jax.ShapeDtypeStruct(a.shape if scatter else (slots,) + a.shape, a.dtype) for a in arrays]
    return Exchange(arrays, shapes, len(peers), plan)


def second_hop(gathered):
    def plan(srcs, dsts, send_sems, recv_sems, local_sems):
        x, y, c, me = _place()
        sibling, _ = _peer(x, y, c, 1)
        sends, recvs = [], []
        for t, k in enumerate(CHIP_PEERS):
            _, landed = _peer(x, y, c, k)
            _, coming = _peer(x, y, c, k ^ 1)
            for j, buf in enumerate(dsts):
                sems = dict(send_sem=send_sems.at[j, t], recv_sem=recv_sems.at[j, t], device_id=sibling, device_id_type=_MESH)
                sends.append(pltpu.make_async_remote_copy(src_ref=buf.at[landed], dst_ref=buf.at[landed], **sems))
                recvs.append(pltpu.make_async_remote_copy(src_ref=buf.at[coming], dst_ref=buf.at[coming], **sems))
        return [], sends, recvs

    shapes = [jax.ShapeDtypeStruct(a.shape, a.dtype) for a in gathered]
    return Exchange(gathered, shapes, len(CHIP_PEERS), plan, aliased=True)


FIRST_HOP_PEERS = (1,) + CHIP_PEERS


def _gather_two_level(arrays, name):
    n = len(arrays)

    def body(*refs):
        srcs, dsts, (send_sems, recv_sems, local_sems) = refs[:n], refs[n:2 * n], refs[2 * n:]
        x, y, c, me = _place()
        sibling, sidx = _peer(x, y, c, 1)

        def copy(j, sem, block, to, src=None):
            rows = dsts[j].at[block]
            return pltpu.make_async_remote_copy(src_ref=rows if src is None else src, dst_ref=rows, send_sem=send_sems.at[j, sem],
                                                recv_sem=recv_sems.at[j, sem], device_id=to, device_id_type=_MESH)

        local = [pltpu.make_async_copy(srcs[j], dsts[j].at[me], local_sems.at[j]) for j in range(n)]
        for cp in local:
            cp.start()
        first = [copy(j, 1 + t, me, _peer(x, y, c, k)[0], src=srcs[j]) for t, k in enumerate(CHIP_PEERS) for j in range(n)]
        first += [copy(j, 0, me, sibling, src=srcs[j]) for j in range(n)]
        for cp in first:
            cp.start()
        passed = []
        for t, k in enumerate(CHIP_PEERS):
            peer, pidx = _peer(x, y, c, k)
            for j in range(n):
                copy(j, 1 + t, pidx, peer).wait_recv()
                passed.append(copy(j, 4 + t, pidx, sibling))
                passed[-1].start()
        for j in range(n):
            copy(j, 0, sidx, sibling).wait_recv()
        for t, k in enumerate(CHIP_PEERS):
            _, pidx = _peer(x, y, c, k ^ 1)
            for j in range(n):
                copy(j, 4 + t, pidx, sibling).wait_recv()
        for cp in first + passed:
            cp.wait_send()
        for cp in local:
            cp.wait()

    shapes = [jax.ShapeDtypeStruct((N_DEV,) + a.shape, a.dtype) for a in arrays]
    return pl.pallas_call(body, name=name, in_specs=[_ANY] * n, out_specs=[_ANY] * n, out_shape=shapes,
                          scratch_shapes=_sem_scratch(n, 1 + 2 * len(CHIP_PEERS)))(*arrays)


def to_sibling(arrays):
    def plan(srcs, dsts, send_sems, recv_sems, local_sems):
        x, y, c, _ = _place()
        sibling, _ = _peer(x, y, c, 1)
        copies = [pltpu.make_async_remote_copy(src_ref=src.at[1 - c], dst_ref=dst, send_sem=send_sems.at[j, 0],
                                               recv_sem=recv_sems.at[j, 0], device_id=sibling, device_id_type=_MESH)
                  for j, (src, dst) in enumerate(zip(srcs, dsts))]
        return [], copies, copies

    return Exchange(arrays, [jax.ShapeDtypeStruct(a.shape[1:], a.dtype) for a in arrays], 1, plan)


def combine(a, b):
    assert not (a.aliased or b.aliased)
    na, nsem = len(a.arrays), len(a.scratch)

    def plan(srcs, dsts, *sems):
        return tuple(u + v for u, v in zip(a.plan(srcs[:na], dsts[:na], *sems[:nsem]), b.plan(srcs[na:], dsts[na:], *sems[nsem:])))

    both = Exchange(a.arrays + b.arrays, a.out_shapes + b.out_shapes, 1, plan)
    both.scratch = a.scratch + b.scratch
    return both


def pair_add(own, theirs, core, *, name):
    _, ns, r, c = own.shape
    rb = _row_block(r, c // 2)

    def body(core_ref, a_ref, b_ref, o_ref):
        o_ref[...] = (a_ref[...].astype(F32) + b_ref[...].astype(F32)).astype(o_ref.dtype)

    return pl.pallas_call(
        body, name=name,
        grid_spec=pltpu.PrefetchScalarGridSpec(
            num_scalar_prefetch=1, grid=(ns, r // rb),
            in_specs=[pl.BlockSpec((None, None, rb, c), lambda i, j, core_ref: (core_ref[0], i, j, 0)),
                      pl.BlockSpec((None, rb, c), lambda i, j, core_ref: (i, j, 0))],
            out_specs=pl.BlockSpec((None, rb, c), lambda i, j, core_ref: (i, j, 0))),
        out_shape=jax.ShapeDtypeStruct((ns, r, c), own.dtype), compiler_params=_cparams(),
    )(core, own, theirs)


ADAMW_BLOCK_BYTES = 1 << 20


def _row_block(r, c):
    for rb in range(r, 0, -1):
        if r % rb == 0 and (rb % 16 == 0 or rb == r) and rb * c * 4 <= ADAMW_BLOCK_BYTES:
            return rb
    return r


def adamw(w, m, v, parts, *, name):
    depth, r, c = w.shape
    n_parts = parts[0].shape[0]
    rb = _row_block(r, c)

    def body(w_ref, m_ref, v_ref, *refs):
        p_refs, (g_ref, d_ref, nm_ref, nv_ref) = refs[:depth], refs[depth:]

        def total(p_ref):
            g = p_ref[0].astype(F32)
            for j in range(1, n_parts):
                g = g + p_ref[j].astype(F32)
            return g

        g = total(p_refs[0])
        for l in range(1, depth):
            g = jnp.where(pl.program_id(0) == l, total(p_refs[l]), g)
        g_ref[...] = g
        m_new = ADAM_B1 * m_ref[...] + (1.0 - ADAM_B1) * g
        v_new = ADAM_B2 * v_ref[...] + (1.0 - ADAM_B2) * (g * g)
        m_hat = m_new / (1.0 - ADAM_B1 ** ADAM_STEP)
        v_hat = v_new / (1.0 - ADAM_B2 ** ADAM_STEP)
        d_ref[...] = -ADAM_LR * (m_hat / (jnp.sqrt(v_hat) + ADAM_EPS) + ADAM_WD * w_ref[...])
        nm_ref[...] = m_new
        nv_ref[...] = v_new

    wblk = pl.BlockSpec((None, rb, c), lambda l, i: (l, i, 0))
    pblk = pl.BlockSpec((n_parts, rb, c), lambda l, i: (0, i, 0))
    return pl.pallas_call(
        body, name=name, grid=(depth, r // rb),
        in_specs=[wblk, wblk, wblk] + [pblk] * depth, out_specs=[wblk] * 4,
        out_shape=[jax.ShapeDtypeStruct((depth, r, c), F32)] * 4, compiler_params=_cparams(),
    )(w, m, v, *parts)


BIG = (("w_in", 2), ("w_uq", 2), ("w_ukv", 2), ("w_pa", 2), ("w_pb", 1), ("w_o", 1), ("w_gate", 2), ("w_up", 2), ("w_down", 1))
SMALL = ("norm_mix_g", "pool_w", "pool_scale", "q_norm_g", "kv_norm_g", "norm_ffn_g", "final_norm_g")
WEIGHTS = ("meta_tokens", "norm_mix_g", "w_in", "pool_w", "pool_scale", "q_norm_g", "kv_norm_g", "w_uq", "w_ukv", "w_pa", "w_pb",
           "w_o", "norm_ffn_g", "w_gate", "w_up", "w_down", "final_norm_g")
HEAD_QK = QK_NOPE + QK_ROPE
KR_END = Z_KR + QK_ROPE


def _cat_cols(parts):
    return [jnp.concatenate(parts, axis=1)]


def _cat_rows(parts):
    return [jnp.concatenate(parts, axis=0)]


def _arr_w_in(parts):
    full = jnp.concatenate(parts, axis=1)
    zc = lambda n: jnp.zeros((full.shape[0], n), full.dtype)
    return [jnp.concatenate([full[:, :Z_KR], zc(QK_NOPE), full[:, Z_KR:KR_END], zc(LANES - HEAD_QK), full[:, KR_END:]], axis=1)]


def _arr_w_uq(parts):
    full = jnp.concatenate(parts, axis=1)
    z = jnp.zeros((full.shape[0], HEAD_SLOT - HEAD_QK), full.dtype)
    pieces = []
    for hd in range(N_HEADS):
        pieces += [full[:, hd * HEAD_QK:(hd + 1) * HEAD_QK], z]
    return [jnp.concatenate(pieces, axis=1)]


def _arr_w_ukv(parts):
    full = jnp.concatenate(parts, axis=1)
    z = jnp.zeros((full.shape[0], HEAD_SLOT - QK_NOPE), full.dtype)
    wide = QK_NOPE + V_DIM
    k, v = [], []
    for hd in range(N_HEADS):
        k += [full[:, hd * wide:hd * wide + QK_NOPE], z]
        v.append(full[:, hd * wide + QK_NOPE:(hd + 1) * wide])
    return [jnp.concatenate(k, axis=1), jnp.concatenate(v, axis=1)]


def arrange(g, fn, out_shapes, name):
    def body(g_ref, *o_refs):
        for o_ref, val in zip(o_refs, fn([g_ref[p] for p in range(N_DEV)])):
            o_ref[...] = val

    return pl.pallas_call(
        body, name=name, grid=(1,),
        in_specs=[pl.BlockSpec(g.shape, lambda i: (0, 0, 0))],
        out_specs=[pl.BlockSpec(s, lambda i: (0, 0)) for s in out_shapes],
        out_shape=[jax.ShapeDtypeStruct(s, g.dtype) for s in out_shapes], compiler_params=_cparams(),
    )(g)


def _arranged_ranges(lo, hi):
    out = []
    for a, b, shift in ((0, Z_KR, 0), (Z_KR, KR_END, QK_NOPE), (KR_END, D_IN, LANES - QK_ROPE)):
        s, e = max(lo, a), min(hi, b)
        if s < e:
            out.append((s + shift, e + shift))
    return out


def _chunks_w_in(acc):
    cs = D_IN // N_DEV
    return [jnp.concatenate([acc[:, a:b] for a, b in _arranged_ranges(p * cs, (p + 1) * cs)], axis=1) for p in range(N_DEV)]


def _chunks_w_uq(acc):
    per = N_HEADS // N_DEV
    return [jnp.concatenate([acc[:, hd * HEAD_SLOT:hd * HEAD_SLOT + HEAD_QK] for hd in range(p * per, (p + 1) * per)], axis=1)
            for p in range(N_DEV)]


def _chunks_w_ukv(acc_k, acc_v):
    per = N_HEADS // N_DEV
    out = []
    for p in range(N_DEV):
        pieces = []
        for hd in range(p * per, (p + 1) * per):
            pieces += [acc_k[:, hd * HEAD_SLOT:hd * HEAD_SLOT + QK_NOPE], acc_v[:, hd * V_DIM:(hd + 1) * V_DIM]]
        out.append(jnp.concatenate(pieces, axis=1))
    return out


def _chunks_cols(acc):
    cs = acc.shape[1] // N_DEV
    return [acc[:, p * cs:(p + 1) * cs] for p in range(N_DEV)]


def _chunks_rows(acc):
    rs = acc.shape[0] // N_DEV
    return [acc[p * rs:(p + 1) * rs, :] for p in range(N_DEV)]


def _chunks_cols_transposed(acc):
    at = acc[...].T
    rs = at.shape[0] // N_DEV
    return [at[p * rs:(p + 1) * rs, :] for p in range(N_DEV)]


def _pack(parts, row_multiple):
    flat = jnp.concatenate([p.reshape(-1) for p in parts])
    return jnp.pad(flat, (0, -flat.shape[0] % (row_multiple * LANES))).reshape(-1, LANES)


def _unpack(packed, shapes):
    flat, out, off = packed.reshape(-1), [], 0
    for s in shapes:
        n = 1
        for d in s:
            n *= d
        out.append(flat[off:off + n].reshape(s))
        off += n
    return out


def _rope_table(lp, nb):
    inv = 1.0 / (ROPE_THETA ** (jnp.arange(0, QK_ROPE, 2, dtype=F32) / QK_ROPE))
    ang = jnp.arange(lp, dtype=F32)[:, None] * inv[None, :]
    cos, sin = jnp.cos(ang), jnp.sin(ang)
    z = lambda n: jnp.zeros((lp, n), F32)
    tail = LANES - QK_NOPE - QK_ROPE
    c = jnp.concatenate([jnp.ones((lp, QK_NOPE), F32), cos, cos, z(tail)], axis=1)
    cr = jnp.concatenate([z(QK_NOPE), cos, cos, z(tail)], axis=1)
    s1 = jnp.concatenate([z(QK_NOPE), -sin, z(HALF_ROPE), z(tail)], axis=1)
    s2 = jnp.concatenate([z(QK_NOPE), z(HALF_ROPE), sin, z(tail)], axis=1)
    return jnp.tile(jnp.concatenate([c, cr, s1, s2], axis=1), (nb, 1))


MIX_IN, MIX_OUT = ("w_in", "w_uq", "w_ukv"), ("w_pa", "w_pb", "w_o")
MIX = MIX_IN + MIX_OUT
FFN = ("w_gate", "w_up", "w_down")
TRANSPOSED = ("w_gate", "w_up")
ARRANGERS = {
    "w_in": (_arr_w_in, (("win", (D_MODEL, DZ)),)), "w_uq": (_arr_w_uq, (("wuq", (Q_RANK, QK_WIDTH)),)),
    "w_ukv": (_arr_w_ukv, (("wuk", (KV_RANK, QK_WIDTH)), ("wuv", (KV_RANK, D_MODEL)))),
    "w_pa": (_cat_cols, (("wpa", (POOL_WIDTH, D_MODEL)),)), "w_pb": (_cat_rows, (("wpb", (D_MODEL, D_MODEL)),)),
    "w_o": (_cat_rows, (("wo", (D_MODEL, D_MODEL)),)), "w_gate": (_cat_rows, (("wgt", (D_FF, D_MODEL)),)),
    "w_up": (_cat_rows, (("wut", (D_FF, D_MODEL)),)), "w_down": (_cat_rows, (("wd", (D_FF, D_MODEL)),)),
}


def _operands(gathered, names, l):
    p = {}
    for n in names:
        fn, outs = ARRANGERS[n]
        if fn is _cat_rows:
            p[outs[0][0]] = gathered[n].reshape(outs[0][1])
            continue
        for (key, _), a in zip(outs, arrange(gathered[n], fn, [s for _, s in outs], f"arrange_{n}_{l}")):
            p[key] = a
    return p


def _small_operands(small, l):
    pw = small["pool_w"][l].astype(BF16)
    return dict(g_mix=small["norm_mix_g"][l][None], gq=small["q_norm_g"][l][None], gkv=small["kv_norm_g"][l][None],
                g_ffn=small["norm_ffn_g"][l][None], ps=small["pool_scale"][l][None], pw=pw)


class MeshComm:
    def __init__(self, w, meta_tokens):
        self.src = lambda n, l: w[n][l].astype(BF16)
        self.meta_tokens = meta_tokens
        self.core = lax.axis_index("c").astype(jnp.int32).reshape(1)
        self.rides = {0: [(n, 0) for n in FFN] + [(n, 1) for n in MIX], 1: [(n, 1) for n in FFN]}

    def first_weights(self):
        got = _gather_two_level([self.src(n, 0) for n in MIX_IN] + [self.meta_tokens], "gather_mix_0")
        return dict(zip(MIX_IN, got)), jnp.moveaxis(got[-1], 0, 1).reshape(N_META, D_MODEL)

    def early_first_hop(self):
        return exchange([self.src(n, 0) for n in MIX_OUT], False, FIRST_HOP_PEERS)

    def early_weights(self, landed):
        return dict(zip(MIX_OUT, second_hop(landed).run("second_hop_mix_0")))

    def first_hop(self, l):
        return exchange([self.src(n, layer) for n, layer in self.rides[l]], False, FIRST_HOP_PEERS)

    def second_hop(self, l, landed):
        return second_hop(landed)

    def carried(self, l, full, names, layer):
        return {n: full[self.rides[l].index((n, layer))] for n in names}

    def pair_exchange(self, own):
        return to_sibling(own)

    def pair_add(self, own, theirs, names, tag):
        return [pair_add(a, b, self.core, name=f"pair_add_{n}_{tag}") for n, a, b in zip(names, own, theirs)]

    def last_pair_exchange(self, own, small):
        got = combine(to_sibling(own), exchange([small], False, ALL_PEERS)).run("pair_grads_mix_0")
        return got[:-1], got[-1]

    def scatter(self, sums):
        return exchange(sums, True, CHIP_PEERS, by_chip=True)

    def scatter_now(self, sums, name):
        return self.scatter(sums).run(name)


HEADS_FWD, HEADS_BWD = 8, 4
TILE_ROWS, TILE_ROWS_BWD = 512, 256


def _tile(t, target):
    n = max(1, -(-t // (target + target // 8)))
    while t % n or (t // n) % 16:
        n += 1
    return t // n


def _wgrad_tile(t):
    return max(tm for tm in (2 * TQ, TQ, LANES) if t % tm == 0)


def _ffn_bwd_part(dh2, p, s, tag, ride):
    d, ff = D_MODEL, D_FF // N_DEV
    t = dh2.shape[0]
    wg_ = lambda n, x, ys, fn, shape: wgrad(x, ys, fn, shape, tm=_wgrad_tile(t), name=f"wgrad_{n}_{tag}")[0]
    (dh1, hn2, act, dgt, dup, dg_ffn), brought = ffn_bwd(dh2, s["h1"], p["g_ffn"], s["gt"], s["up"], p["wgt"], p["wut"], p["wd"],
                                                         tm=_tile(t, TILE_ROWS_BWD), name=f"ffn_bwd_{tag}", ride=ride)
    chunks = [wg_("gate", hn2, [dgt], _chunks_cols_transposed, (ff, d)), wg_("up", hn2, [dup], _chunks_cols_transposed, (ff, d)),
              wg_("down", act, [dh2], _chunks_rows, (ff, d))]
    return dh1, chunks, dict(norm_ffn_g=dg_ffn[0]), brought


EARLY, LATE = ("w_o", "w_pa"), ("w_in", "w_uq", "w_ukv", "w_pb")


def _mix_bwd_part(dh1, p, s, rope, nb, lp, tag, comm, ride, next_ride):
    d = D_MODEL
    t = dh1.shape[0]
    wg_ = lambda n, x, ys, fn, shape: wgrad(x, ys, fn, shape, tm=_wgrad_tile(t), name=f"wgrad_{n}_{tag}")[0]
    (dga, dgb, dpa, dpb, do, dpool, dps, dpw), first = merge_bwd(dh1, s["z"], s["pa"], s["pb"], s["pooled"], p["pw"], p["ps"],
                                                                   p["wpa"], p["wpb"], p["wo"], tm=_tile(t, TILE_ROWS),
                                                                   name=f"merge_bwd_{tag}", ride=ride)
    c_o = wg_("o", s["mg"], [dh1], _chunks_rows, (d // N_DEV, d))
    c_pa = wg_("pa", s["a"], [dpa], _chunks_cols, (POOL_WIDTH, d // N_DEV))
    c_pb, theirs = wgrad(s["o"], [dpb], _chunks_rows, (d // N_DEV, d), tm=_wgrad_tile(t), name=f"wgrad_pb_{tag}",
                         ride=comm.pair_exchange([c_o, c_pa]))
    early = comm.pair_add([c_o, c_pa], theirs, EARLY, f"early_{tag}")
    (dq, dk, dv), brought = attn_bwd(s["q"], s["k"], s["v"], s["o"], do, s["lse"], nb=nb, lp=lp, hb=HEADS_BWD,
                                     name=f"attn_bwd_{tag}", ride=next_ride(first, early))
    dh, hn, dz, cqn, ckvn, dqb, dkb, dvb, dg_mix, dgq, dgkv = in_proj_bwd(
        dh1, s["h"], p["g_mix"], s["z"], dq, dk, dv, dga, dgb, dpool, p["win"], p["gq"], p["gkv"], p["wuq"], p["wuk"], p["wuv"],
        rope, tm=_tile(t, TILE_ROWS_BWD), lp=lp, nb=nb, name=f"in_proj_bwd_{tag}")
    c_in = wg_("in", hn, [dz], _chunks_w_in, (d, D_IN // N_DEV))
    c_uq = wg_("uq", cqn, [dqb], _chunks_w_uq, (Q_RANK, N_HEADS * HEAD_QK // N_DEV))
    c_ukv = wg_("ukv", ckvn, [dkb, dvb], _chunks_w_ukv, (KV_RANK, N_HEADS * (QK_NOPE + V_DIM) // N_DEV))
    small = dict(pool_scale=dps[0], pool_w=dpw, norm_mix_g=dg_mix[0], q_norm_g=dgq[0], kv_norm_g=dgkv[0])
    return dh, [c_in, c_uq, c_ukv, c_pb], small, brought


def train_step(x, loss_target, small, comm):
    nb, seq, d = x.shape
    lp = -(-(N_META + seq) // LANES) * LANES
    t = nb * lp
    assert nb <= 2 and DEPTH == 2
    tm = _tile(t, TILE_ROWS)
    rope = _rope_table(lp, nb)
    gathered, meta = comm.first_weights()
    pad = jnp.zeros((nb, lp - N_META - seq, d), F32)
    h = jnp.concatenate([jnp.broadcast_to(meta[None], (nb, N_META, d)), x, pad], axis=1).reshape(t, d)
    target = jnp.concatenate([jnp.zeros((nb, N_META, d), F32), loss_target, pad], axis=1).reshape(t, d)

    params, saved, full = [], [], {}
    for l in range(DEPTH):
        p = _small_operands(small, l)
        p.update(_operands(gathered, MIX_IN, 0) if l == 0 else _operands(comm.carried(0, full[0], MIX, 1), MIX, 1))
        (z, q, k, v), early = in_proj_fwd(h, p["g_mix"], p["win"], p["gq"], p["gkv"], p["wuq"], p["wuk"], p["wuv"], rope, tm=tm,
                                          name=f"in_proj_fwd_{l}", ride=comm.early_first_hop() if l == 0 else None)
        (o, lse), landed = attn_fwd(q, k, v, nb=nb, lp=lp, hb=HEADS_FWD, name=f"attn_fwd_{l}", ride=comm.first_hop(l))
        if l == 0:
            p.update(_operands(comm.early_weights(early), MIX_OUT, 0))
        (h1, pooled, a, pa, pb, mg), full[l] = merge_fwd(h, z, o, p["pw"], p["ps"], p["wpa"], p["wpb"], p["wo"], tm=tm, lp=lp,
                                                          nb=nb, name=f"merge_fwd_{l}", ride=comm.second_hop(l, landed))
        p.update(_operands(comm.carried(l, full[l], FFN, l), FFN, l))
        h2, gt, up = ffn_fwd(h1, p["g_ffn"], p["wgt"], p["wut"], p["wd"], tm=tm, name=f"ffn_fwd_{l}")
        params.append(p)
        saved.append(dict(h=h, z=z, q=q, k=k, v=v, o=o, lse=lse, h1=h1, pooled=pooled, a=a, pa=pa, pb=pb, mg=mg, gt=gt, up=up))
        h = h2
    parts, dh, dgf = loss_head(h, small["final_norm_g"][None], target, tm=tm, lp=lp, nb=nb, seq=seq, name="loss_head")
    loss = jnp.sum(parts[::8, 0])

    sums = {}
    dh, c_ffn1, small1, _ = _ffn_bwd_part(dh, params[1], saved[1], 1, None)
    dh, c_late1, sm, brought = _mix_bwd_part(
        dh, params[1], saved[1], rope, nb, lp, 1, comm, comm.pair_exchange(c_ffn1),
        lambda theirs, early: comm.scatter(comm.pair_add(c_ffn1, theirs, FFN, "ffn_1") + early))
    small1.update(sm)
    sums.update({(n, 1): a for n, a in zip(FFN + EARLY, brought)})
    dh, c_ffn0, small0, theirs = _ffn_bwd_part(dh, params[0], saved[0], 0, comm.pair_exchange(c_late1))
    s_late1 = comm.pair_add(c_late1, theirs, LATE, "late_1")
    dh, c_late0, sm, brought = _mix_bwd_part(
        dh, params[0], saved[0], rope, nb, lp, 0, comm, comm.pair_exchange(c_ffn0),
        lambda theirs, early: comm.scatter(s_late1 + comm.pair_add(c_ffn0, theirs, FFN, "ffn_0") + early))
    small0.update(sm)
    sums.update({(n, l): a for (n, l), a in zip([(n, 1) for n in LATE] + [(n, 0) for n in FFN + EARLY], brought)})
    dh = dh.reshape(nb, lp, d)
    dmeta = jnp.sum(dh[:, :N_META], axis=0)
    meta_chunks = jnp.transpose(dmeta.reshape(N_META, N_CHIPS, 2, d // N_DEV), (2, 1, 0, 3)).astype(BF16)
    small_grads = {n: jnp.stack([small0[n], small1[n]]) for n in small0}
    small_grads["final_norm_g"] = dgf[0]
    last_names = LATE + ("meta_tokens",)
    theirs, small_parts = comm.last_pair_exchange(c_late0 + [meta_chunks], _pack([small_grads[n] for n in SMALL], 8))
    last = comm.scatter_now(comm.pair_add(c_late0 + [meta_chunks], theirs, last_names, "late_0"), "scatter_late_0")
    sums.update({(n, 0): a for n, a in zip(last_names, last)})
    return loss, dh[:, N_META:N_META + seq], sums, small_grads, small_parts


def kernel(x, meta_tokens, norm_mix_g, w_in, pool_w, pool_scale, q_norm_g, kv_norm_g, w_uq, w_ukv, w_pa, w_pb, w_o, norm_ffn_g, w_gate, w_up, w_down, final_norm_g, loss_target, m_meta_tokens, m_norm_mix_g, m_w_in, m_pool_w, m_pool_scale, m_q_norm_g, m_kv_norm_g, m_w_uq, m_w_ukv, m_w_pa, m_w_pb, m_w_o, m_norm_ffn_g, m_w_gate, m_w_up, m_w_down, m_final_norm_g, v_meta_tokens, v_norm_mix_g, v_w_in, v_pool_w, v_pool_scale, v_q_norm_g, v_kv_norm_g, v_w_uq, v_w_ukv, v_w_pa, v_w_pb, v_w_o, v_norm_ffn_g, v_w_gate, v_w_up, v_w_down, v_final_norm_g):
    args = dict(locals())
    w = {n: args[n] for n in WEIGHTS}
    m = {n: args["m_" + n] for n in WEIGHTS}
    v = {n: args["v_" + n] for n in WEIGHTS}
    small = {n: w[n] for n in SMALL}
    as_handled = lambda a, n: jnp.swapaxes(a, 1, 2) if n in TRANSPOSED else a
    wh, mh, vh = ({n: as_handled(d[n], n) for n, _ in BIG} for d in (w, m, v))

    loss, grad_x, sums, _, small_recv = train_step(x, loss_target, small, MeshComm(wh, meta_tokens))
    loss = lax.psum(loss, ("x", "y", "c"))

    out = {n: [as_handled(a, n) for a in adamw(wh[n], mh[n], vh[n], [sums[(n, l)] for l in range(DEPTH)], name=f"adamw_{n}")]
           for n, _ in BIG}
    out["meta_tokens"] = [a[0] for a in adamw(meta_tokens[None], m["meta_tokens"][None], v["meta_tokens"][None],
                                              [sums[("meta_tokens", 0)]], name="adamw_meta_tokens")]
    pk = lambda d: _pack([d[n] for n in SMALL], 8)[None]
    packed = adamw(pk(w), pk(m), pk(v), [small_recv], name="adamw_small")
    shapes = [w[n].shape for n in SMALL]
    for n, *kinds in zip(SMALL, *[_unpack(packed[kind][0], shapes) for kind in range(4)]):
        out[n] = kinds
    return (loss, grad_x, *[out[n][kind] for kind in range(4) for n in WEIGHTS])
```

```python
import functools
import math

import jax
import jax.numpy as jnp
from jax import lax
from jax.experimental import pallas as pl
from jax.experimental.pallas import tpu as pltpu

F32, BF16 = jnp.float32, jnp.bfloat16

D_MODEL = 1024
N_META = 16
N_HEADS = 16
QK_NOPE, QK_ROPE, V_DIM = 64, 32, 64
HALF_ROPE = QK_ROPE // 2
Q_RANK, KV_RANK = 256, 128
POOL_WINDOWS = (2, 4, 8, 16)
POOL_GROUP = 128
POOL_WIDTH = POOL_GROUP * len(POOL_WINDOWS)
POOL_HALO = 16
D_FF = 2816
D_IN = 2976
NORM_EPS = 1e-6
SM_SCALE = (QK_NOPE + QK_ROPE) ** -0.5
LOG2E = math.log2(math.e)
EXP2_SCALE = SM_SCALE * LOG2E
MASK_VALUE = -1e30
ROPE_THETA = 10000.0
DEPTH = 2
N_DEV = 8

ADAM_LR, ADAM_B1, ADAM_B2, ADAM_EPS, ADAM_WD, ADAM_STEP = 0.001, 0.9, 0.999, 1e-08, 0.01, 10

LANES = 128
HEAD_SLOT = LANES
QK_WIDTH = N_HEADS * HEAD_SLOT
Z_CQ, Z_CKV, Z_KR, Z_GA, Z_GB, DZ = 512, 768, 896, 1024, 2048, 3072
TQ = TK = 256
VMEM_LIMIT = 56 * 1024 * 1024


def _cparams():
    return pltpu.CompilerParams(vmem_limit_bytes=VMEM_LIMIT)


def _rows(tm, width, col=0):
    return pl.BlockSpec((tm, width), lambda i: (i, col))


def _whole(shape):
    zeros = (0,) * len(shape)
    return pl.BlockSpec(shape, lambda i: zeros, pipeline_mode=pl.Buffered(1))


def _acc(shape):
    zeros = (0,) * len(shape)
    return pl.BlockSpec(shape, lambda i: zeros)


def _dot(a, b):
    return jnp.dot(a, b, preferred_element_type=F32)


def _dot_tn(a, b):
    return lax.dot_general(a, b, (((0,), (0,)), ((), ())), preferred_element_type=F32)


def _dot_nt(a, b):
    return lax.dot_general(a, b, (((1,), (1,)), ((), ())), preferred_element_type=F32)


def _rms(x):
    r = lax.rsqrt(jnp.mean(x * x, axis=-1, keepdims=True) + NORM_EPS)
    return x * r, r


def _rms_bwd(dy, xhat, r, g):
    dg = jnp.sum(dy * xhat, axis=0, keepdims=True)
    dxh = dy * g
    dx = r * (dxh - xhat * jnp.mean(dxh * xhat, axis=-1, keepdims=True))
    return dx, dg


def _sigmoid(x):
    return 1.0 / (1.0 + jnp.exp(-x))


def _rope_fwd(q, c, s1, s2):
    w = q.shape[1]
    return q * c + pltpu.roll(q, w - HALF_ROPE, 1) * s1 + pltpu.roll(q, HALF_ROPE, 1) * s2


def _rope_bwd(dq, c, s1, s2):
    w = dq.shape[1]
    return dq * c + pltpu.roll(dq * s1, HALF_ROPE, 1) + pltpu.roll(dq * s2, w - HALF_ROPE, 1)


def _rope_tables(rope, reps):
    c, cr, s1, s2 = (rope[:, k * LANES:(k + 1) * LANES] for k in range(4))
    if reps > 1:
        return jnp.tile(c, (1, reps)), jnp.tile(s1, (1, reps)), jnp.tile(s2, (1, reps))
    return cr, s1, s2


def _seq_pos(gi, lp, nb):
    pos = gi
    for b in range(1, nb):
        pos = jnp.where(gi >= b * lp, gi - b * lp, pos)
    return pos


_ANY = pl.BlockSpec(memory_space=pl.ANY)


def _carrying_call(body, ride, operands, *, name, grid, in_specs, out_specs, out_shape, scratch_shapes=()):
    n_in, n_out = len(in_specs), len(out_specs)
    if ride is None:
        out = pl.pallas_call(body, name=name, grid=grid, in_specs=in_specs, out_specs=out_specs, out_shape=out_shape,
                             scratch_shapes=list(scratch_shapes), compiler_params=_cparams())(*operands)
        return out, []
    ne = len(ride.arrays)

    def carrying(*refs):
        ins, r_in, rest = refs[:n_in], refs[n_in:n_in + ne], refs[n_in + ne:]
        outs, r_out, rest = rest[:n_out], rest[n_out:n_out + ne], rest[n_out + ne:]
        scratch, sems = rest[:len(scratch_shapes)], rest[len(scratch_shapes):]
        ids = [pl.program_id(a) for a in range(len(grid))]
        first = functools.reduce(jnp.logical_and, [i == 0 for i in ids])
        last = functools.reduce(jnp.logical_and, [i == g - 1 for i, g in zip(ids, grid)])

        @pl.when(first)
        def _():
            ride.start(r_in, r_out, sems)

        body(*ins, *outs, *scratch)

        @pl.when(last)
        def _():
            ride.wait(r_in, r_out, sems)

    out = pl.pallas_call(
        carrying, name=name, grid=grid, in_specs=list(in_specs) + [_ANY] * ne, out_specs=list(out_specs) + [_ANY] * ne,
        out_shape=list(out_shape) + ride.out_shapes, scratch_shapes=list(scratch_shapes) + ride.scratch,
        input_output_aliases=ride.aliases(n_in, n_out), compiler_params=_cparams(),
    )(*operands, *ride.arrays)
    return out[:n_out], out[n_out:]


def in_proj_fwd(h, g_mix, win, gq, gkv, wuq, wuk, wuv, rope, *, tm, name, ride=None):
    t = h.shape[0]

    def body(h_ref, g_ref, win_ref, gq_ref, gkv_ref, wuq_ref, wuk_ref, wuv_ref, rope_ref, z_ref, q_ref, k_ref, v_ref):
        xhat, _ = _rms(h_ref[...])
        hn = (xhat * g_ref[...]).astype(BF16)
        z = _dot(hn, win_ref[...])
        z_ref[...] = z
        rope_t = rope_ref[...]
        xq, _ = _rms(z[:, Z_CQ:Z_CKV])
        cqn = (xq * gq_ref[...]).astype(BF16)
        q = _rope_fwd(_dot(cqn, wuq_ref[...]), *_rope_tables(rope_t, N_HEADS))
        q_ref[...] = q.astype(BF16)
        xkv, _ = _rms(z[:, Z_CKV:Z_KR])
        ckvn = (xkv * gkv_ref[...]).astype(BF16)
        kr = _rope_fwd(z[:, Z_KR:Z_GA], *_rope_tables(rope_t, 1))
        k_ref[...] = (_dot(ckvn, wuk_ref[...]) + jnp.tile(kr, (1, N_HEADS))).astype(BF16)
        v_ref[...] = _dot(ckvn, wuv_ref[...]).astype(BF16)

    return _carrying_call(
        body, ride, (h, g_mix, win, gq, gkv, wuq, wuk, wuv, rope), name=name, grid=(t // tm,),
        in_specs=[_rows(tm, D_MODEL), _whole((1, D_MODEL)), _whole((D_MODEL, DZ)), _whole((1, Q_RANK)), _whole((1, KV_RANK)),
                  _whole((Q_RANK, QK_WIDTH)), _whole((KV_RANK, QK_WIDTH)), _whole((KV_RANK, D_MODEL)), _rows(tm, 4 * LANES)],
        out_specs=[_rows(tm, DZ), _rows(tm, QK_WIDTH), _rows(tm, QK_WIDTH), _rows(tm, D_MODEL)],
        out_shape=[jax.ShapeDtypeStruct((t, DZ), F32), jax.ShapeDtypeStruct((t, QK_WIDTH), BF16),
                   jax.ShapeDtypeStruct((t, QK_WIDTH), BF16), jax.ShapeDtypeStruct((t, D_MODEL), BF16)])


def attn_fwd(q, k, v, *, nb, lp, hb, name, ride=None):
    t = q.shape[0]
    nq, tail = lp // TQ, lp % TQ
    assert tail % LANES == 0

    def body(q_ref, k_ref, v_ref, o_ref, lse_ref, vt):
        for pr in range(hb // 2):
            vt[pr] = v_ref[:, pr * LANES:(pr + 1) * LANES].T

        def q_block(qs, tq, whole_k):
            qh = [q_ref[pl.ds(qs, tq), hd * HEAD_SLOT:(hd + 1) * HEAD_SLOT] for hd in range(hb)]
            keep = lax.broadcasted_iota(jnp.int32, (tq, tq), 0) <= lax.broadcasted_iota(jnp.int32, (tq, tq), 1)

            def k_steps(blocks, c, masked):
                sts = [[_dot_nt(k_ref[pl.ds(ks, tk), hd * HEAD_SLOT:(hd + 1) * HEAD_SLOT], qh[hd]) for hd in range(hb)]
                       for ks, tk in blocks]
                for (ks, tk), st_b in zip(blocks, sts):
                    ps, stats = [], []
                    for hd in range(hb):
                        m, l, _ = c[hd]
                        st = jnp.where(keep, st_b[hd], MASK_VALUE) if masked else st_b[hd]
                        m_new = jnp.maximum(m, jnp.max(st, axis=0, keepdims=True))
                        p = jnp.exp2((st - m_new) * EXP2_SCALE)
                        alpha = jnp.exp2((m - m_new) * EXP2_SCALE)
                        ps.append(p.astype(BF16))
                        stats.append((m_new, alpha * l + jnp.sum(p, axis=0, keepdims=True), alpha))
                    pvs = [_dot(vt[hd // 2, :, pl.ds(ks, tk)], ps[hd]) for hd in range(hb)]
                    c = tuple((stats[hd][0], stats[hd][1], stats[hd][2] * c[hd][2] + pvs[hd]) for hd in range(hb))
                return c

            def two_blocks(i, c):
                ks = pl.multiple_of(2 * i * TK, TK)
                return k_steps([(ks, TK), (ks + TK, TK)], c, False)

            init = tuple((jnp.full((1, tq), MASK_VALUE, F32), jnp.zeros((1, tq), F32), jnp.zeros((LANES, tq), F32))
                         for _ in range(hb))
            pairs = lax.div(whole_k, 2)
            c = lax.fori_loop(0, pairs, two_blocks, init)
            c = lax.fori_loop(2 * pairs, whole_k, lambda kj, c: k_steps([(pl.multiple_of(kj * TK, TK), TK)], c, False), c)
            c = k_steps([(qs, tq)], c, True)
            sub = lax.broadcasted_iota(jnp.int32, (LANES, tq), 0)
            for pr in range(hb // 2):
                (m0, l0, a0), (m1, l1, a1) = c[2 * pr], c[2 * pr + 1]
                o_ref[pl.ds(qs, tq), pr * LANES:(pr + 1) * LANES] = jnp.where(sub < V_DIM, a0 / l0, a1 / l1).T.astype(BF16)
                lse_ref[2 * pr, :, pl.ds(qs, tq)] = m0 * SM_SCALE + jnp.log(l0)
                lse_ref[2 * pr + 1, :, pl.ds(qs, tq)] = m1 * SM_SCALE + jnp.log(l1)

        def whole_q_block(qi, carry):
            q_block(pl.multiple_of(qi * TQ, TQ), TQ, qi)
            return carry

        lax.fori_loop(0, nq, whole_q_block, 0)
        if tail:
            q_block(nq * TQ, tail, nq)

    blk = lambda w: pl.BlockSpec((lp, w), lambda b, g: (b, g))
    return _carrying_call(
        body, ride, (q, k, v), name=name, grid=(nb, N_HEADS // hb),
        in_specs=[blk(hb * HEAD_SLOT), blk(hb * HEAD_SLOT), blk(hb * V_DIM)],
        out_specs=[blk(hb * V_DIM), pl.BlockSpec((hb, 1, lp), lambda b, g: (g, 0, b))],
        out_shape=[jax.ShapeDtypeStruct((t, D_MODEL), BF16), jax.ShapeDtypeStruct((N_HEADS, 1, t), F32)],
        scratch_shapes=[pltpu.VMEM((hb // 2, LANES, lp), BF16)])


def _pool_band_fwd(i, tm, lp, nb):
    r = lax.broadcasted_iota(jnp.int32, (tm, POOL_HALO + tm), 0)
    e = lax.broadcasted_iota(jnp.int32, (tm, POOL_HALO + tm), 1)
    diff = r + POOL_HALO - e
    pos = _seq_pos(i * tm + lax.broadcasted_iota(jnp.int32, (tm, 1), 0), lp, nb)
    out = []
    for w in POOL_WINDOWS:
        cnt = jnp.minimum(pos + 1, w)
        band = jnp.where((diff >= 0) & (diff < cnt), 1.0, 0.0).astype(BF16)
        out.append((band, cnt.astype(F32)))
    return out


def merge_fwd(h, z, o, pw, ps, wpa, wpb, wo, *, tm, lp, nb, name, ride=None):
    t = h.shape[0]
    hb = tm // POOL_HALO

    def body(h_ref, u_ref, uprev_ref, ga_ref, gb_ref, o_ref, pw_ref, ps_ref, wpa_ref, wpb_ref, wo_ref,
             h1_ref, pooled_ref, a_ref, pa_ref, pb_ref, mg_ref):
        i = pl.program_id(0)
        u = u_ref[...]
        uext = jnp.concatenate([uprev_ref[...], u], axis=0).astype(BF16)
        pooled, ys = [], []
        for g, (band, cnt) in enumerate(_pool_band_fwd(i, tm, lp, nb)):
            gs = slice(g * POOL_GROUP, (g + 1) * POOL_GROUP)
            pg = (_dot(band, uext[:, gs]) / cnt - u[:, gs]).astype(BF16)
            pooled.append(pg)
            ys.append(_dot(pg, pw_ref[g]))
        pooled_ref[...] = jnp.concatenate(pooled, axis=1)
        a = (jnp.concatenate(ys, axis=1) * ps_ref[...]).astype(BF16)
        a_ref[...] = a
        pa = _dot(a, wpa_ref[...])
        pb = _dot(o_ref[...], wpb_ref[...])
        pa_ref[...] = pa.astype(BF16)
        pb_ref[...] = pb.astype(BF16)
        mg = (_sigmoid(ga_ref[...]) * pa + _sigmoid(gb_ref[...]) * pb).astype(BF16)
        mg_ref[...] = mg
        h1_ref[...] = h_ref[...] + _dot(mg, wo_ref[...])

    halo = pl.BlockSpec((POOL_HALO, POOL_WIDTH), lambda i: (jnp.maximum(i * hb - 1, 0), 0))
    return _carrying_call(
        body, ride, (h, z, z, z, z, o, pw, ps, wpa, wpb, wo), name=name, grid=(t // tm,),
        in_specs=[_rows(tm, D_MODEL), _rows(tm, POOL_WIDTH), halo, _rows(tm, D_MODEL, 1), _rows(tm, D_MODEL, 2), _rows(tm, D_MODEL),
                  _whole((4, POOL_GROUP, POOL_GROUP)), _whole((1, POOL_WIDTH)), _whole((POOL_WIDTH, D_MODEL)),
                  _whole((D_MODEL, D_MODEL)), _whole((D_MODEL, D_MODEL))],
        out_specs=[_rows(tm, D_MODEL), _rows(tm, POOL_WIDTH), _rows(tm, POOL_WIDTH), _rows(tm, D_MODEL), _rows(tm, D_MODEL),
                   _rows(tm, D_MODEL)],
        out_shape=[jax.ShapeDtypeStruct((t, D_MODEL), F32), jax.ShapeDtypeStruct((t, POOL_WIDTH), BF16),
                   jax.ShapeDtypeStruct((t, POOL_WIDTH), BF16), jax.ShapeDtypeStruct((t, D_MODEL), BF16),
                   jax.ShapeDtypeStruct((t, D_MODEL), BF16), jax.ShapeDtypeStruct((t, D_MODEL), BF16)])


def ffn_fwd(h1, g, wgt, wut, wd, *, tm, name):
    t = h1.shape[0]

    def body(h_ref, g_ref, wgt_ref, wut_ref, wd_ref, h2_ref, gt_ref, up_ref):
        h = h_ref[...]
        xhat, _ = _rms(h)
        hn = (xhat * g_ref[...]).astype(BF16)
        gt = _dot_nt(hn, wgt_ref[...])
        up = _dot_nt(hn, wut_ref[...])
        gt_ref[...] = gt.astype(BF16)
        up_ref[...] = up.astype(BF16)
        act = (gt * _sigmoid(gt) * up).astype(BF16)
        h2_ref[...] = h + _dot(act, wd_ref[...])

    return pl.pallas_call(
        body, name=name, grid=(t // tm,),
        in_specs=[_rows(tm, D_MODEL), _whole((1, D_MODEL)), _whole((D_FF, D_MODEL)), _whole((D_FF, D_MODEL)), _whole((D_FF, D_MODEL))],
        out_specs=[_rows(tm, D_MODEL), _rows(tm, D_FF), _rows(tm, D_FF)],
        out_shape=[jax.ShapeDtypeStruct((t, D_MODEL), F32), jax.ShapeDtypeStruct((t, D_FF), BF16), jax.ShapeDtypeStruct((t, D_FF), BF16)],
        compiler_params=_cparams(),
    )(h1, g, wgt, wut, wd)


def loss_head(h, g, target, *, tm, lp, nb, seq, name):
    t = h.shape[0]
    nt = t // tm

    def body(h_ref, g_ref, t_ref, loss_ref, dh_ref, dg_ref):
        i = pl.program_id(0)
        pos = _seq_pos(i * tm + lax.broadcasted_iota(jnp.int32, (tm, 1), 0), lp, nb)
        real = (pos >= N_META) & (pos < N_META + seq)
        xhat, r = _rms(h_ref[...])
        gg = g_ref[...]
        err = jnp.where(real, xhat * gg - t_ref[...], 0.0)
        loss_ref[...] = jnp.full((8, LANES), 0.5 * jnp.sum(err * err) / D_MODEL, F32)
        dx, dg = _rms_bwd(err * (1.0 / D_MODEL), xhat, r, gg)
        dh_ref[...] = dx

        @pl.when(i == 0)
        def _():
            dg_ref[...] = jnp.zeros_like(dg_ref)

        dg_ref[...] += dg

    return pl.pallas_call(
        body, name=name, grid=(nt,),
        in_specs=[_rows(tm, D_MODEL), _whole((1, D_MODEL)), _rows(tm, D_MODEL)],
        out_specs=[pl.BlockSpec((8, LANES), lambda i: (i, 0)), _rows(tm, D_MODEL), _acc((1, D_MODEL))],
        out_shape=[jax.ShapeDtypeStruct((nt * 8, LANES), F32), jax.ShapeDtypeStruct((t, D_MODEL), F32),
                   jax.ShapeDtypeStruct((1, D_MODEL), F32)],
        compiler_params=_cparams(),
    )(h, g, target)


def wgrad(x, ys, chunk_fn, chunk_shape, *, tm, name, ride=None):
    t, m = x.shape
    tiles = t // tm
    steps = -(-tiles // 2)

    def body(*refs):
        ins, o_ref, accs = refs[:2 * (1 + len(ys))], refs[2 * (1 + len(ys))], refs[2 * (1 + len(ys)) + 1:]
        i = pl.program_id(0)

        @pl.when(i == 0)
        def _():
            for acc in accs:
                acc[...] = jnp.zeros_like(acc)

        def both(first, second, mask):
            b = second[...].astype(BF16)
            if mask and tiles % 2:
                b = jnp.where(2 * i + 1 < tiles, b, jnp.zeros_like(b))
            return jnp.concatenate([first[...].astype(BF16), b], axis=0)

        xb = both(ins[0], ins[1], True)
        for j, acc in enumerate(accs):
            acc[...] += _dot_tn(xb, both(ins[2 + 2 * j], ins[3 + 2 * j], False))

        @pl.when(i == steps - 1)
        def _():
            for p, chunk in enumerate(chunk_fn(*accs)):
                o_ref[p % 2, p // 2] = chunk.astype(BF16)

    def two_tiles(width):
        return [pl.BlockSpec((tm, width), lambda i: (2 * i, 0)),
                pl.BlockSpec((tm, width), lambda i: (jnp.minimum(2 * i + 1, tiles - 1), 0))]

    out = (2, N_DEV // 2) + tuple(chunk_shape)
    operands = [x, x] + [a for y in ys for a in (y, y)]
    (chunks,), brought = _carrying_call(
        body, ride, operands, name=name, grid=(steps,),
        in_specs=two_tiles(m) + [s for y in ys for s in two_tiles(y.shape[1])], out_specs=[_acc(out)],
        out_shape=[jax.ShapeDtypeStruct(out, BF16)], scratch_shapes=[pltpu.VMEM((m, y.shape[1]), F32) for y in ys])
    return chunks, brought


def ffn_bwd(dh2, h1, g, gt, up, wgt, wut, wd, *, tm, name, ride=None):
    t = h1.shape[0]

    def body(dh2_ref, h_ref, g_ref, gt_ref, up_ref, wgt_ref, wut_ref, wd_ref, dh1_ref, hn_ref, act_ref, dgt_ref, dup_ref, dg_ref):
        dh2 = dh2_ref[...]
        dact = _dot_nt(dh2.astype(BF16), wd_ref[...])
        gt = gt_ref[...].astype(F32)
        up = up_ref[...].astype(F32)
        sg = _sigmoid(gt)
        silu = gt * sg
        act_ref[...] = (silu * up).astype(BF16)
        dgt = (dact * up * (sg * (1.0 + gt * (1.0 - sg)))).astype(BF16)
        dup = (dact * silu).astype(BF16)
        dgt_ref[...] = dgt
        dup_ref[...] = dup
        dhn = _dot(dgt, wgt_ref[...]) + _dot(dup, wut_ref[...])
        xhat, r = _rms(h_ref[...])
        gg = g_ref[...]
        hn_ref[...] = (xhat * gg).astype(BF16)
        dx, dg = _rms_bwd(dhn, xhat, r, gg)
        dh1_ref[...] = dh2 + dx

        @pl.when(pl.program_id(0) == 0)
        def _():
            dg_ref[...] = jnp.zeros_like(dg_ref)

        dg_ref[...] += dg

    return _carrying_call(
        body, ride, (dh2, h1, g, gt, up, wgt, wut, wd), name=name, grid=(t // tm,),
        in_specs=[_rows(tm, D_MODEL), _rows(tm, D_MODEL), _whole((1, D_MODEL)), _rows(tm, D_FF), _rows(tm, D_FF),
                  _whole((D_FF, D_MODEL)), _whole((D_FF, D_MODEL)), _whole((D_FF, D_MODEL))],
        out_specs=[_rows(tm, D_MODEL), _rows(tm, D_MODEL), _rows(tm, D_FF), _rows(tm, D_FF), _rows(tm, D_FF), _acc((1, D_MODEL))],
        out_shape=[jax.ShapeDtypeStruct((t, D_MODEL), F32), jax.ShapeDtypeStruct((t, D_MODEL), BF16),
                   jax.ShapeDtypeStruct((t, D_FF), BF16), jax.ShapeDtypeStruct((t, D_FF), BF16),
                   jax.ShapeDtypeStruct((t, D_FF), BF16), jax.ShapeDtypeStruct((1, D_MODEL), F32)])


def merge_bwd(dh1, z, pa, pb, pooled, pw, ps, wpa, wpb, wo, *, tm, name, ride=None):
    t = dh1.shape[0]

    def body(dh1_ref, ga_ref, gb_ref, pa_ref, pb_ref, pooled_ref, pw_ref, ps_ref, wpa_ref, wpb_ref, wo_ref,
             dga_ref, dgb_ref, dpa_ref, dpb_ref, do_ref, dpool_ref, dps_ref, dpw_ref):
        dmg = _dot_nt(dh1_ref[...].astype(BF16), wo_ref[...])
        sa = _sigmoid(ga_ref[...])
        sb = _sigmoid(gb_ref[...])
        dga_ref[...] = (dmg * pa_ref[...].astype(F32) * sa * (1.0 - sa)).astype(BF16)
        dgb_ref[...] = (dmg * pb_ref[...].astype(F32) * sb * (1.0 - sb)).astype(BF16)
        dpa = (dmg * sa).astype(BF16)
        dpb = (dmg * sb).astype(BF16)
        dpa_ref[...] = dpa
        dpb_ref[...] = dpb
        do_ref[...] = _dot_nt(dpb, wpb_ref[...]).astype(BF16)
        da = _dot_nt(dpa, wpa_ref[...])
        pooled = pooled_ref[...]
        ps = ps_ref[...]

        @pl.when(pl.program_id(0) == 0)
        def _():
            dps_ref[...] = jnp.zeros_like(dps_ref)
            dpw_ref[...] = jnp.zeros_like(dpw_ref)

        dps, dpool = [], []
        for g in range(len(POOL_WINDOWS)):
            gs = slice(g * POOL_GROUP, (g + 1) * POOL_GROUP)
            y = _dot(pooled[:, gs], pw_ref[g])
            dps.append(jnp.sum(da[:, gs] * y, axis=0, keepdims=True))
            dy = (da[:, gs] * ps[:, gs]).astype(BF16)
            dpool.append(_dot_nt(dy, pw_ref[g]))
            dpw_ref[g] += _dot_tn(pooled[:, gs], dy)
        dps_ref[...] += jnp.concatenate(dps, axis=1)
        dpool_ref[...] = jnp.concatenate(dpool, axis=1)

    return _carrying_call(
        body, ride, (dh1, z, z, pa, pb, pooled, pw, ps, wpa, wpb, wo), name=name, grid=(t // tm,),
        in_specs=[_rows(tm, D_MODEL), _rows(tm, D_MODEL, 1), _rows(tm, D_MODEL, 2), _rows(tm, D_MODEL), _rows(tm, D_MODEL),
                  _rows(tm, POOL_WIDTH), _whole((4, POOL_GROUP, POOL_GROUP)),
                  _whole((1, POOL_WIDTH)), _whole((POOL_WIDTH, D_MODEL)), _whole((D_MODEL, D_MODEL)), _whole((D_MODEL, D_MODEL))],
        out_specs=[_rows(tm, D_MODEL), _rows(tm, D_MODEL), _rows(tm, D_MODEL), _rows(tm, D_MODEL), _rows(tm, D_MODEL),
                   _rows(tm, POOL_WIDTH), _acc((1, POOL_WIDTH)), _acc((4, POOL_GROUP, POOL_GROUP))],
        out_shape=[jax.ShapeDtypeStruct((t, D_MODEL), BF16)] * 5
        + [jax.ShapeDtypeStruct((t, POOL_WIDTH), F32), jax.ShapeDtypeStruct((1, POOL_WIDTH), F32),
           jax.ShapeDtypeStruct((4, POOL_GROUP, POOL_GROUP), F32)])


def attn_bwd(q, k, v, o, do, lse, *, nb, lp, hb, name, ride=None):
    t = q.shape[0]
    nq, tail = lp // TQ, lp % TQ
    assert tail % LANES == 0

    def body(q_ref, k_ref, v_ref, o_ref, do_ref, lse_ref, dq_ref, dk_ref, dv_ref, kt, doh, lse_row, delta_row, dqt):
        lane = lax.broadcasted_iota(jnp.int32, (lp, LANES), 1)
        first = lane < V_DIM
        sub = lax.broadcasted_iota(jnp.int32, (LANES, lp), 0)
        for pr in range(hb // 2):
            ls = slice(pr * LANES, (pr + 1) * LANES)
            do = do_ref[:, ls]
            doh[2 * pr] = jnp.where(first, do, jnp.zeros_like(do))
            doh[2 * pr + 1] = jnp.where(first, jnp.zeros_like(do), do)
            prod_t = (do.astype(F32) * o_ref[:, ls].astype(F32)).T
            delta_row[2 * pr] = jnp.sum(jnp.where(sub < V_DIM, prod_t, 0.0), axis=0, keepdims=True)
            delta_row[2 * pr + 1] = jnp.sum(jnp.where(sub < V_DIM, 0.0, prod_t), axis=0, keepdims=True)
        for hd in range(hb):
            lse_row[hd] = lse_ref[hd] * LOG2E
            kt[hd] = k_ref[:, hd * HEAD_SLOT:(hd + 1) * HEAD_SLOT].T
        dqt[...] = jnp.zeros(dqt.shape, F32)
        heads = range(hb)
        hss = [slice(hd * HEAD_SLOT, (hd + 1) * HEAD_SLOT) for hd in heads]

        def k_block(ks, tk, next_q):
            keep = lax.broadcasted_iota(jnp.int32, (tk, tk), 0) <= lax.broadcasted_iota(jnp.int32, (tk, tk), 1)

            def q_steps(blocks, c, masked):
                work = [(qs, tq, hd) for qs, tq in blocks for hd in heads]
                qhs = [q_ref[pl.ds(qs, tq), hss[hd]] for qs, tq, hd in work]
                dos = [doh[hd, pl.ds(qs, tq), :] for qs, tq, hd in work]
                sts = [_dot_nt(k_ref[pl.ds(ks, tk), hss[hd]], qhs[i]) for i, (_, _, hd) in enumerate(work)]
                dpts = [_dot_nt(v_ref[pl.ds(ks, tk), (hd // 2) * LANES:(hd // 2 + 1) * LANES], dos[i])
                        for i, (_, _, hd) in enumerate(work)]
                pts, dsts = [], []
                for i, (qs, tq, hd) in enumerate(work):
                    st = jnp.where(keep, sts[i], MASK_VALUE) if masked else sts[i]
                    pt = jnp.exp2(st * EXP2_SCALE - lse_row[hd, :, pl.ds(qs, tq)])
                    dsts.append((pt * (dpts[i] - delta_row[hd, :, pl.ds(qs, tq)])).astype(BF16))
                    pts.append(pt.astype(BF16))
                dvs = [_dot(pts[i], dos[i]) for i in range(len(work))]
                dks = [_dot(dsts[i], qhs[i]) for i in range(len(work))]
                dqs = [_dot(kt[hd, :, pl.ds(ks, tk)], dsts[i]) for i, (_, _, hd) in enumerate(work)]
                c = list(c)
                for i, (qs, tq, hd) in enumerate(work):
                    dqt[hd, :, pl.ds(qs, tq)] += dqs[i]
                    c[hd] = (c[hd][0] + dks[i], c[hd][1] + dvs[i])
                return tuple(c)

            zero = jnp.zeros((tk, LANES), F32)
            c = q_steps([(ks, tk)], tuple((zero, zero) for _ in heads), True)
            if next_q is not None:
                def two_blocks(i, c):
                    qs = pl.multiple_of((next_q + 2 * i) * TQ, TQ)
                    return q_steps([(qs, TQ), (qs + TQ, TQ)], c, False)

                pairs = lax.div(nq - next_q, 2)
                c = lax.fori_loop(0, pairs, two_blocks, c)
                c = lax.fori_loop(next_q + 2 * pairs, nq, lambda qi, c: q_steps([(pl.multiple_of(qi * TQ, TQ), TQ)], c, False), c)
                if tail:
                    c = q_steps([(nq * TQ, tail)], c, False)
            for hd in heads:
                dk_ref[pl.ds(ks, tk), hss[hd]] = c[hd][0] * SM_SCALE
            for pr in range(hb // 2):
                dv_ref[pl.ds(ks, tk), pr * LANES:(pr + 1) * LANES] = c[2 * pr][1] + c[2 * pr + 1][1]

        def whole_k_block(kj, carry):
            k_block(pl.multiple_of(kj * TK, TK), TK, kj + 1)
            return carry

        lax.fori_loop(0, nq, whole_k_block, 0)
        if tail:
            k_block(nq * TQ, tail, None)
        for hd in range(hb):
            dq_ref[:, hd * HEAD_SLOT:(hd + 1) * HEAD_SLOT] = dqt[hd].T * SM_SCALE

    blk = lambda w: pl.BlockSpec((lp, w), lambda b, g: (b, g))
    return _carrying_call(
        body, ride, (q, k, v, o, do, lse), name=name, grid=(nb, N_HEADS // hb),
        in_specs=[blk(hb * HEAD_SLOT), blk(hb * HEAD_SLOT), blk(hb * V_DIM), blk(hb * V_DIM), blk(hb * V_DIM),
                  pl.BlockSpec((hb, 1, lp), lambda b, g: (g, 0, b))],
        out_specs=[blk(hb * HEAD_SLOT), blk(hb * HEAD_SLOT), blk(hb * V_DIM)],
        out_shape=[jax.ShapeDtypeStruct((t, QK_WIDTH), F32), jax.ShapeDtypeStruct((t, QK_WIDTH), F32),
                   jax.ShapeDtypeStruct((t, D_MODEL), F32)],
        scratch_shapes=[pltpu.VMEM((hb, HEAD_SLOT, lp), BF16), pltpu.VMEM((hb, lp, LANES), BF16), pltpu.VMEM((hb, 1, lp), F32),
                        pltpu.VMEM((hb, 1, lp), F32), pltpu.VMEM((hb, HEAD_SLOT, lp), F32)])


def in_proj_bwd(dh1, h, g_mix, z, dq, dk, dv, dga, dgb, dpool, win, gq, gkv, wuq, wuk, wuv, rope, *, tm, lp, nb, name):
    t = h.shape[0]
    hb = tm // POOL_HALO
    last_halo = t // POOL_HALO - 1

    def body(dh1_ref, h_ref, g_ref, zcq_ref, zckv_ref, dq_ref, dk_ref, dv_ref, dga_ref, dgb_ref, dpool_ref, dnext_ref,
             win_ref, gq_ref, gkv_ref, wuq_ref, wuk_ref, wuv_ref, rope_ref,
             dh_ref, hn_ref, dz_ref, cqn_ref, ckvn_ref, dqb_ref, dkb_ref, dvb_ref, dg_ref, dgq_ref, dgkv_ref):
        i = pl.program_id(0)
        rope_t = rope_ref[...]
        dqb = _rope_bwd(dq_ref[...], *_rope_tables(rope_t, N_HEADS)).astype(BF16)
        dqb_ref[...] = dqb
        xq, rq = _rms(zcq_ref[...])
        gq_v = gq_ref[...]
        cqn_ref[...] = (xq * gq_v).astype(BF16)
        dcq, dgq = _rms_bwd(_dot_nt(dqb, wuq_ref[...]), xq, rq, gq_v)
        dk = dk_ref[...]
        dkb = dk.astype(BF16)
        dvb = dv_ref[...].astype(BF16)
        dkb_ref[...] = dkb
        dvb_ref[...] = dvb
        xkv, rkv = _rms(zckv_ref[...])
        gkv_v = gkv_ref[...]
        ckvn_ref[...] = (xkv * gkv_v).astype(BF16)
        dckv, dgkv = _rms_bwd(_dot_nt(dkb, wuk_ref[...]) + _dot_nt(dvb, wuv_ref[...]), xkv, rkv, gkv_v)
        dks = dk[:, :HEAD_SLOT]
        for hd in range(1, N_HEADS):
            dks = dks + dk[:, hd * HEAD_SLOT:(hd + 1) * HEAD_SLOT]
        dzk = _rope_bwd(dks, *_rope_tables(rope_t, 1))
        dp_cur = dpool_ref[...]
        dp_ext = jnp.concatenate([dp_cur, dnext_ref[...]], axis=0)
        r = lax.broadcasted_iota(jnp.int32, (tm, tm + POOL_HALO), 0)
        e = lax.broadcasted_iota(jnp.int32, (tm, tm + POOL_HALO), 1)
        gt_col = i * tm + lax.broadcasted_iota(jnp.int32, (1, tm + POOL_HALO), 1)
        pos_col = _seq_pos(gt_col, lp, nb)
        gt_row = i * tm + lax.broadcasted_iota(jnp.int32, (tm + POOL_HALO, 1), 0)
        pos_row = _seq_pos(gt_row, lp, nb)
        dus = []
        for g, w in enumerate(POOL_WINDOWS):
            gs = slice(g * POOL_GROUP, (g + 1) * POOL_GROUP)
            band = jnp.where((e - r >= 0) & (e - r < jnp.minimum(pos_col + 1, w)) & (gt_col < t), 1.0, 0.0).astype(BF16)
            scaled = jnp.where(gt_row < t, dp_ext[:, gs] / jnp.minimum(pos_row + 1, w).astype(F32), 0.0).astype(BF16)
            dus.append(_dot(band, scaled) - dp_cur[:, gs])
        dz = jnp.concatenate(dus + [dcq, dckv, dzk], axis=1).astype(BF16)
        dz = jnp.concatenate([dz, dga_ref[...], dgb_ref[...]], axis=1)
        dz_ref[...] = dz
        xhat, rr = _rms(h_ref[...])
        gg = g_ref[...]
        hn_ref[...] = (xhat * gg).astype(BF16)
        dx, dg = _rms_bwd(_dot_nt(dz, win_ref[...]), xhat, rr, gg)
        dh_ref[...] = dh1_ref[...] + dx

        @pl.when(i == 0)
        def _():
            dg_ref[...] = jnp.zeros_like(dg_ref)
            dgq_ref[...] = jnp.zeros_like(dgq_ref)
            dgkv_ref[...] = jnp.zeros_like(dgkv_ref)

        dg_ref[...] += dg
        dgq_ref[...] += dgq
        dgkv_ref[...] += dgkv

    nxt = pl.BlockSpec((POOL_HALO, POOL_WIDTH), lambda i: (jnp.minimum((i + 1) * hb, last_halo), 0))
    return pl.pallas_call(
        body, name=name, grid=(t // tm,),
        in_specs=[_rows(tm, D_MODEL), _rows(tm, D_MODEL), _whole((1, D_MODEL)), _rows(tm, Q_RANK, Z_CQ // Q_RANK),
                  _rows(tm, KV_RANK, Z_CKV // KV_RANK), _rows(tm, QK_WIDTH), _rows(tm, QK_WIDTH), _rows(tm, D_MODEL),
                  _rows(tm, D_MODEL), _rows(tm, D_MODEL), _rows(tm, POOL_WIDTH), nxt,
                  _whole((D_MODEL, DZ)), _whole((1, Q_RANK)), _whole((1, KV_RANK)), _whole((Q_RANK, QK_WIDTH)),
                  _whole((KV_RANK, QK_WIDTH)), _whole((KV_RANK, D_MODEL)), _rows(tm, 4 * LANES)],
        out_specs=[_rows(tm, D_MODEL), _rows(tm, D_MODEL), _rows(tm, DZ), _rows(tm, Q_RANK), _rows(tm, KV_RANK),
                   _rows(tm, QK_WIDTH), _rows(tm, QK_WIDTH), _rows(tm, D_MODEL),
                   _acc((1, D_MODEL)), _acc((1, Q_RANK)), _acc((1, KV_RANK))],
        out_shape=[jax.ShapeDtypeStruct((t, D_MODEL), F32), jax.ShapeDtypeStruct((t, D_MODEL), BF16),
                   jax.ShapeDtypeStruct((t, DZ), BF16), jax.ShapeDtypeStruct((t, Q_RANK), BF16),
                   jax.ShapeDtypeStruct((t, KV_RANK), BF16), jax.ShapeDtypeStruct((t, QK_WIDTH), BF16),
                   jax.ShapeDtypeStruct((t, QK_WIDTH), BF16), jax.ShapeDtypeStruct((t, D_MODEL), BF16),
                   jax.ShapeDtypeStruct((1, D_MODEL), F32), jax.ShapeDtypeStruct((1, Q_RANK), F32),
                   jax.ShapeDtypeStruct((1, KV_RANK), F32)],
        compiler_params=_cparams(),
    )(dh1, h, g_mix, z, z, dq, dk, dv, dga, dgb, dpool, dpool, win, gq, gkv, wuq, wuk, wuv, rope)


_MESH = pl.DeviceIdType.MESH


def _place():
    x, y, c = lax.axis_index("x"), lax.axis_index("y"), lax.axis_index("c")
    return x, y, c, 4 * x + 2 * y + c


def _peer(x, y, c, k):
    px, py, pc = (1 - x) if k & 4 else x, (1 - y) if k & 2 else y, (1 - c) if k & 1 else c
    return (px, py, pc), 4 * px + 2 * py + pc


ALL_PEERS = tuple(range(1, N_DEV))
CHIP_PEERS = (2, 4, 6)
N_CHIPS = N_DEV // 2


def _sem_scratch(n, m):
    return [pltpu.SemaphoreType.DMA((n, m)), pltpu.SemaphoreType.DMA((n, m)), pltpu.SemaphoreType.DMA((n,))]


class Exchange:
    def __init__(self, arrays, out_shapes, sem_cols, plan, aliased=False):
        self.arrays, self.out_shapes, self.plan = list(arrays), list(out_shapes), plan
        self.scratch = _sem_scratch(len(self.arrays), sem_cols)
        self.aliased = aliased

    def split(self, refs):
        n = len(self.arrays)
        return refs[:n], refs[n:2 * n], refs[2 * n:]

    def start(self, srcs, dsts, sems):
        local, sends, _ = self.plan(srcs, dsts, *sems)
        for cp in local + sends:
            cp.start()

    def wait(self, srcs, dsts, sems):
        local, sends, recvs = self.plan(srcs, dsts, *sems)
        for cp in recvs:
            cp.wait_recv()
        for cp in sends:
            cp.wait_send()
        for cp in local:
            cp.wait()

    def aliases(self, first_in, first_out):
        return {first_in + j: first_out + j for j in range(len(self.arrays))} if self.aliased else {}

    def run(self, name):
        def body(*refs):
            srcs, dsts, sems = self.split(refs)
            self.start(srcs, dsts, sems)
            self.wait(srcs, dsts, sems)

        n = len(self.arrays)
        return pl.pallas_call(body, name=name, in_specs=[_ANY] * n, out_specs=[_ANY] * n, out_shape=self.out_shapes,
                              scratch_shapes=self.scratch, input_output_aliases=self.aliases(0, 0))(*self.arrays)


def exchange(arrays, scatter, peers, by_chip=False):
    slots = N_CHIPS if by_chip else N_DEV

    def plan(srcs, dsts, send_sems, recv_sems, local_sems):
        x, y, c, me = _place()
        mine = 2 * x + y if by_chip else me
        local = [pltpu.make_async_copy(src.at[mine] if scatter else src, dst.at[mine], local_sems.at[j])
                 for j, (src, dst) in enumerate(zip(srcs, dsts))]
        sends, recvs = [], []
        for t, k in enumerate(peers):
            peer, pidx = _peer(x, y, c, k)
            theirs = 2 * peer[0] + peer[1] if by_chip else pidx
            for j, (src, dst) in enumerate(zip(srcs, dsts)):
                part = src.at[theirs] if scatter else src
                sems = dict(send_sem=send_sems.at[j, t], recv_sem=recv_sems.at[j, t], device_id=peer, device_id_type=_MESH)
                sends.append(pltpu.make_async_remote_copy(src_ref=part, dst_ref=dst.at[mine], **sems))
                recvs.append(pltpu.make_async_remote_copy(src_ref=part, dst_ref=dst.at[theirs], **sems))
        return local, sends, recvs

    shapes = [jax.ShapeDtypeStruct(a.shape if scatter else (slots,) + a.shape, a.dtype) for a in arrays]
    return Exchange(arrays, shapes, len(peers), plan)


def second_hop(gathered):
    def plan(srcs, dsts, send_sems, recv_sems, local_sems):
        x, y, c, me = _place()
        sibling, _ = _peer(x, y, c, 1)
        sends, recvs = [], []
        for t, k in enumerate(CHIP_PEERS):
            _, landed = _peer(x, y, c, k)
            _, coming = _peer(x, y, c, k ^ 1)
            for j, buf in enumerate(dsts):
                sems = dict(send_sem=send_sems.at[j, t], recv_sem=recv_sems.at[j, t], device_id=sibling, device_id_type=_MESH)
                sends.append(pltpu.make_async_remote_copy(src_ref=buf.at[landed], dst_ref=buf.at[landed], **sems))
                recvs.append(pltpu.make_async_remote_copy(src_ref=buf.at[coming], dst_ref=buf.at[coming], **sems))
        return [], sends, recvs

    shapes = [jax.ShapeDtypeStruct(a.shape, a.dtype) for a in gathered]
    return Exchange(gathered, shapes, len(CHIP_PEERS), plan, aliased=True)


FIRST_HOP_PEERS = (1,) + CHIP_PEERS


def _gather_two_level(arrays, name):
    n = len(arrays)

    def body(*refs):
        srcs, dsts, (send_sems, recv_sems, local_sems) = refs[:n], refs[n:2 * n], refs[2 * n:]
        x, y, c, me = _place()
        sibling, sidx = _peer(x, y, c, 1)

        def copy(j, sem, block, to, src=None):
            rows = dsts[j].at[block]
            return pltpu.make_async_remote_copy(src_ref=rows if src is None else src, dst_ref=rows, send_sem=send_sems.at[j, sem],
                                                recv_sem=recv_sems.at[j, sem], device_id=to, device_id_type=_MESH)

        local = [pltpu.make_async_copy(srcs[j], dsts[j].at[me], local_sems.at[j]) for j in range(n)]
        for cp in local:
            cp.start()
        first = [copy(j, 1 + t, me, _peer(x, y, c, k)[0], src=srcs[j]) for t, k in enumerate(CHIP_PEERS) for j in range(n)]
        first += [copy(j, 0, me, sibling, src=srcs[j]) for j in range(n)]
        for cp in first:
            cp.start()
        passed = []
        for t, k in enumerate(CHIP_PEERS):
            peer, pidx = _peer(x, y, c, k)
            for j in range(n):
                copy(j, 1 + t, pidx, peer).wait_recv()
                passed.append(copy(j, 4 + t, pidx, sibling))
                passed[-1].start()
        for j in range(n):
            copy(j, 0, sidx, sibling).wait_recv()
        for t, k in enumerate(CHIP_PEERS):
            _, pidx = _peer(x, y, c, k ^ 1)
            for j in range(n):
                copy(j, 4 + t, pidx, sibling).wait_recv()
        for cp in first + passed:
            cp.wait_send()
        for cp in local:
            cp.wait()

    shapes = [jax.ShapeDtypeStruct((N_DEV,) + a.shape, a.dtype) for a in arrays]
    return pl.pallas_call(body, name=name, in_specs=[_ANY] * n, out_specs=[_ANY] * n, out_shape=shapes,
                          scratch_shapes=_sem_scratch(n, 1 + 2 * len(CHIP_PEERS)))(*arrays)


def to_sibling(arrays):
    def plan(srcs, dsts, send_sems, recv_sems, local_sems):
        x, y, c, _ = _place()
        sibling, _ = _peer(x, y, c, 1)
        copies = [pltpu.make_async_remote_copy(src_ref=src.at[1 - c], dst_ref=dst, send_sem=send_sems.at[j, 0],
                                               recv_sem=recv_sems.at[j, 0], device_id=sibling, device_id_type=_MESH)
                  for j, (src, dst) in enumerate(zip(srcs, dsts))]
        return [], copies, copies

    return Exchange(arrays, [jax.ShapeDtypeStruct(a.shape[1:], a.dtype) for a in arrays], 1, plan)


def combine(a, b):
    assert not (a.aliased or b.aliased)
    na, nsem = len(a.arrays), len(a.scratch)

    def plan(srcs, dsts, *sems):
        return tuple(u + v for u, v in zip(a.plan(srcs[:na], dsts[:na], *sems[:nsem]), b.plan(srcs[na:], dsts[na:], *sems[nsem:])))

    both = Exchange(a.arrays + b.arrays, a.out_shapes + b.out_shapes, 1, plan)
    both.scratch = a.scratch + b.scratch
    return both


def pair_add(own, theirs, core, *, name):
    _, ns, r, c = own.shape
    rb = _row_block(r, c // 2)

    def body(core_ref, a_ref, b_ref, o_ref):
        o_ref[...] = (a_ref[...].astype(F32) + b_ref[...].astype(F32)).astype(o_ref.dtype)

    return pl.pallas_call(
        body, name=name,
        grid_spec=pltpu.PrefetchScalarGridSpec(
            num_scalar_prefetch=1, grid=(ns, r // rb),
            in_specs=[pl.BlockSpec((None, None, rb, c), lambda i, j, core_ref: (core_ref[0], i, j, 0)),
                      pl.BlockSpec((None, rb, c), lambda i, j, core_ref: (i, j, 0))],
            out_specs=pl.BlockSpec((None, rb, c), lambda i, j, core_ref: (i, j, 0))),
        out_shape=jax.ShapeDtypeStruct((ns, r, c), own.dtype), compiler_params=_cparams(),
    )(core, own, theirs)


ADAMW_BLOCK_BYTES = 1 << 20


def _row_block(r, c):
    for rb in range(r, 0, -1):
        if r % rb == 0 and (rb % 16 == 0 or rb == r) and rb * c * 4 <= ADAMW_BLOCK_BYTES:
            return rb
    return r


def adamw(w, m, v, parts, *, name):
    depth, r, c = w.shape
    n_parts = parts[0].shape[0]
    rb = _row_block(r, c)

    def body(w_ref, m_ref, v_ref, *refs):
        p_refs, (g_ref, d_ref, nm_ref, nv_ref) = refs[:depth], refs[depth:]

        def total(p_ref):
            g = p_ref[0].astype(F32)
            for j in range(1, n_parts):
                g = g + p_ref[j].astype(F32)
            return g

        g = total(p_refs[0])
        for l in range(1, depth):
            g = jnp.where(pl.program_id(0) == l, total(p_refs[l]), g)
        g_ref[...] = g
        m_new = ADAM_B1 * m_ref[...] + (1.0 - ADAM_B1) * g
        v_new = ADAM_B2 * v_ref[...] + (1.0 - ADAM_B2) * (g * g)
        m_hat = m_new / (1.0 - ADAM_B1 ** ADAM_STEP)
        v_hat = v_new / (1.0 - ADAM_B2 ** ADAM_STEP)
        d_ref[...] = -ADAM_LR * (m_hat / (jnp.sqrt(v_hat) + ADAM_EPS) + ADAM_WD * w_ref[...])
        nm_ref[...] = m_new
        nv_ref[...] = v_new

    wblk = pl.BlockSpec((None, rb, c), lambda l, i: (l, i, 0))
    pblk = pl.BlockSpec((n_parts, rb, c), lambda l, i: (0, i, 0))
    return pl.pallas_call(
        body, name=name, grid=(depth, r // rb),
        in_specs=[wblk, wblk, wblk] + [pblk] * depth, out_specs=[wblk] * 4,
        out_shape=[jax.ShapeDtypeStruct((depth, r, c), F32)] * 4, compiler_params=_cparams(),
    )(w, m, v, *parts)


BIG = (("w_in", 2), ("w_uq", 2), ("w_ukv", 2), ("w_pa", 2), ("w_pb", 1), ("w_o", 1), ("w_gate", 2), ("w_up", 2), ("w_down", 1))
SMALL = ("norm_mix_g", "pool_w", "pool_scale", "q_norm_g", "kv_norm_g", "norm_ffn_g", "final_norm_g")
SMALL_PER_LAYER = SMALL[:-1]
WEIGHTS = ("meta_tokens", "norm_mix_g", "w_in", "pool_w", "pool_scale", "q_norm_g", "kv_norm_g", "w_uq", "w_ukv", "w_pa", "w_pb",
           "w_o", "norm_ffn_g", "w_gate", "w_up", "w_down", "final_norm_g")
HEAD_QK = QK_NOPE + QK_ROPE
KR_END = Z_KR + QK_ROPE


def _cat_cols(parts):
    return [jnp.concatenate(parts, axis=1)]


def _cat_rows(parts):
    return [jnp.concatenate(parts, axis=0)]


def _arr_w_in(parts):
    full = jnp.concatenate(parts, axis=1)
    zc = lambda n: jnp.zeros((full.shape[0], n), full.dtype)
    return [jnp.concatenate([full[:, :Z_KR], zc(QK_NOPE), full[:, Z_KR:KR_END], zc(LANES - HEAD_QK), full[:, KR_END:]], axis=1)]


def _arr_w_uq(parts):
    full = jnp.concatenate(parts, axis=1)
    z = jnp.zeros((full.shape[0], HEAD_SLOT - HEAD_QK), full.dtype)
    pieces = []
    for hd in range(N_HEADS):
        pieces += [full[:, hd * HEAD_QK:(hd + 1) * HEAD_QK], z]
    return [jnp.concatenate(pieces, axis=1)]


def _arr_w_ukv(parts):
    full = jnp.concatenate(parts, axis=1)
    z = jnp.zeros((full.shape[0], HEAD_SLOT - QK_NOPE), full.dtype)
    wide = QK_NOPE + V_DIM
    k, v = [], []
    for hd in range(N_HEADS):
        k += [full[:, hd * wide:hd * wide + QK_NOPE], z]
        v.append(full[:, hd * wide + QK_NOPE:(hd + 1) * wide])
    return [jnp.concatenate(k, axis=1), jnp.concatenate(v, axis=1)]


def arrange(g, fn, out_shapes, name):
    def body(g_ref, *o_refs):
        for o_ref, val in zip(o_refs, fn([g_ref[p] for p in range(N_DEV)])):
            o_ref[...] = val

    return pl.pallas_call(
        body, name=name, grid=(1,),
        in_specs=[pl.BlockSpec(g.shape, lambda i: (0, 0, 0))],
        out_specs=[pl.BlockSpec(s, lambda i: (0, 0)) for s in out_shapes],
        out_shape=[jax.ShapeDtypeStruct(s, g.dtype) for s in out_shapes], compiler_params=_cparams(),
    )(g)


def _arranged_ranges(lo, hi):
    out = []
    for a, b, shift in ((0, Z_KR, 0), (Z_KR, KR_END, QK_NOPE), (KR_END, D_IN, LANES - QK_ROPE)):
        s, e = max(lo, a), min(hi, b)
        if s < e:
            out.append((s + shift, e + shift))
    return out


def _chunks_w_in(acc):
    cs = D_IN // N_DEV
    return [jnp.concatenate([acc[:, a:b] for a, b in _arranged_ranges(p * cs, (p + 1) * cs)], axis=1) for p in range(N_DEV)]


def _chunks_w_uq(acc):
    per = N_HEADS // N_DEV
    return [jnp.concatenate([acc[:, hd * HEAD_SLOT:hd * HEAD_SLOT + HEAD_QK] for hd in range(p * per, (p + 1) * per)], axis=1)
            for p in range(N_DEV)]


def _chunks_w_ukv(acc_k, acc_v):
    per = N_HEADS // N_DEV
    out = []
    for p in range(N_DEV):
        pieces = []
        for hd in range(p * per, (p + 1) * per):
            pieces += [acc_k[:, hd * HEAD_SLOT:hd * HEAD_SLOT + QK_NOPE], acc_v[:, hd * V_DIM:(hd + 1) * V_DIM]]
        out.append(jnp.concatenate(pieces, axis=1))
    return out


def _chunks_cols(acc):
    cs = acc.shape[1] // N_DEV
    return [acc[:, p * cs:(p + 1) * cs] for p in range(N_DEV)]


def _chunks_rows(acc):
    rs = acc.shape[0] // N_DEV
    return [acc[p * rs:(p + 1) * rs, :] for p in range(N_DEV)]


def _chunks_cols_transposed(acc):
    at = acc[...].T
    rs = at.shape[0] // N_DEV
    return [at[p * rs:(p + 1) * rs, :] for p in range(N_DEV)]


def _pack(parts, row_multiple):
    flat = jnp.concatenate([p.reshape(-1) for p in parts])
    return jnp.pad(flat, (0, -flat.shape[0] % (row_multiple * LANES))).reshape(-1, LANES)


def _unpack(packed, shapes):
    flat, out, off = packed.reshape(-1), [], 0
    for s in shapes:
        n = 1
        for d in s:
            n *= d
        out.append(flat[off:off + n].reshape(s))
        off += n
    return out


def _rope_table(lp, nb):
    inv = 1.0 / (ROPE_THETA ** (jnp.arange(0, QK_ROPE, 2, dtype=F32) / QK_ROPE))
    ang = jnp.arange(lp, dtype=F32)[:, None] * inv[None, :]
    cos, sin = jnp.cos(ang), jnp.sin(ang)
    z = lambda n: jnp.zeros((lp, n), F32)
    tail = LANES - QK_NOPE - QK_ROPE
    c = jnp.concatenate([jnp.ones((lp, QK_NOPE), F32), cos, cos, z(tail)], axis=1)
    cr = jnp.concatenate([z(QK_NOPE), cos, cos, z(tail)], axis=1)
    s1 = jnp.concatenate([z(QK_NOPE), -sin, z(HALF_ROPE), z(tail)], axis=1)
    s2 = jnp.concatenate([z(QK_NOPE), z(HALF_ROPE), sin, z(tail)], axis=1)
    return jnp.tile(jnp.concatenate([c, cr, s1, s2], axis=1), (nb, 1))


MIX_IN, MIX_OUT = ("w_in", "w_uq", "w_ukv"), ("w_pa", "w_pb", "w_o")
MIX = MIX_IN + MIX_OUT
FFN = ("w_gate", "w_up", "w_down")
TRANSPOSED = ("w_gate", "w_up")
ARRANGERS = {
    "w_in": (_arr_w_in, (("win", (D_MODEL, DZ)),)), "w_uq": (_arr_w_uq, (("wuq", (Q_RANK, QK_WIDTH)),)),
    "w_ukv": (_arr_w_ukv, (("wuk", (KV_RANK, QK_WIDTH)), ("wuv", (KV_RANK, D_MODEL)))),
    "w_pa": (_cat_cols, (("wpa", (POOL_WIDTH, D_MODEL)),)), "w_pb": (_cat_rows, (("wpb", (D_MODEL, D_MODEL)),)),
    "w_o": (_cat_rows, (("wo", (D_MODEL, D_MODEL)),)), "w_gate": (_cat_rows, (("wgt", (D_FF, D_MODEL)),)),
    "w_up": (_cat_rows, (("wut", (D_FF, D_MODEL)),)), "w_down": (_cat_rows, (("wd", (D_FF, D_MODEL)),)),
}


def _operands(gathered, names, l):
    p = {}
    for n in names:
        fn, outs = ARRANGERS[n]
        if fn is _cat_rows:
            p[outs[0][0]] = gathered[n].reshape(outs[0][1])
            continue
        for (key, _), a in zip(outs, arrange(gathered[n], fn, [s for _, s in outs], f"arrange_{n}_{l}")):
            p[key] = a
    return p


def _small_operands(small, l):
    pw = small["pool_w"][l].astype(BF16)
    return dict(g_mix=small["norm_mix_g"][l][None], gq=small["q_norm_g"][l][None], gkv=small["kv_norm_g"][l][None],
                g_ffn=small["norm_ffn_g"][l][None], ps=small["pool_scale"][l][None], pw=pw)


class MeshComm:
    def __init__(self, w, meta_tokens):
        self.src = lambda n, l: w[n][l].astype(BF16)
        self.meta_tokens = meta_tokens
        self.core = lax.axis_index("c").astype(jnp.int32).reshape(1)
        self.rides = {0: [(n, 0) for n in FFN] + [(n, 1) for n in MIX], 1: [(n, 1) for n in FFN]}

    def first_weights(self):
        got = _gather_two_level([self.src(n, 0) for n in MIX_IN] + [self.meta_tokens], "gather_mix_0")
        return dict(zip(MIX_IN, got)), jnp.moveaxis(got[-1], 0, 1).reshape(N_META, D_MODEL)

    def early_first_hop(self):
        return exchange([self.src(n, 0) for n in MIX_OUT], False, FIRST_HOP_PEERS)

    def early_weights(self, landed):
        return dict(zip(MIX_OUT, second_hop(landed).run("second_hop_mix_0")))

    def first_hop(self, l):
        return exchange([self.src(n, layer) for n, layer in self.rides[l]], False, FIRST_HOP_PEERS)

    def second_hop(self, l, landed):
        return second_hop(landed)

    def carried(self, l, full, names, layer):
        return {n: full[self.rides[l].index((n, layer))] for n in names}

    def pair_exchange(self, own):
        return to_sibling(own)

    def pair_add(self, own, theirs, names, tag):
        return [pair_add(a, b, self.core, name=f"pair_add_{n}_{tag}") for n, a, b in zip(names, own, theirs)]

    def last_pair_exchange(self, own, small):
        got = combine(to_sibling(own), exchange([small], False, ALL_PEERS)).run("pair_grads_mix_0")
        return got[:-1], got[-1]

    def scatter(self, sums):
        return exchange(sums, True, CHIP_PEERS, by_chip=True)

    def scatter_and_gather(self, sums, small):
        return combine(self.scatter(sums), exchange([small], False, ALL_PEERS))

    def scattered_and_gathered(self, brought, small):
        return brought[:-1], brought[-1]

    def scatter_now(self, sums, name):
        return self.scatter(sums).run(name)


HEADS_FWD, HEADS_BWD = 8, 4
TILE_ROWS, TILE_ROWS_BWD = 512, 256


def _tile(t, target):
    n = max(1, -(-t // (target + target // 8)))
    while t % n or (t // n) % 16:
        n += 1
    return t // n


def _wgrad_tile(t):
    return max(tm for tm in (2 * TQ, TQ, LANES) if t % tm == 0)


def _ffn_bwd_part(dh2, p, s, tag, ride):
    d, ff = D_MODEL, D_FF // N_DEV
    t = dh2.shape[0]
    wg_ = lambda n, x, ys, fn, shape: wgrad(x, ys, fn, shape, tm=_wgrad_tile(t), name=f"wgrad_{n}_{tag}")[0]
    (dh1, hn2, act, dgt, dup, dg_ffn), brought = ffn_bwd(dh2, s["h1"], p["g_ffn"], s["gt"], s["up"], p["wgt"], p["wut"], p["wd"],
                                                         tm=_tile(t, TILE_ROWS_BWD), name=f"ffn_bwd_{tag}", ride=ride)
    chunks = [wg_("gate", hn2, [dgt], _chunks_cols_transposed, (ff, d)), wg_("up", hn2, [dup], _chunks_cols_transposed, (ff, d)),
              wg_("down", act, [dh2], _chunks_rows, (ff, d))]
    return dh1, chunks, dict(norm_ffn_g=dg_ffn[0]), brought


EARLY, LATE = ("w_o", "w_pa"), ("w_in", "w_uq", "w_ukv", "w_pb")


def _mix_bwd_part(dh1, p, s, rope, nb, lp, tag, comm, ride, next_ride):
    d = D_MODEL
    t = dh1.shape[0]
    wg_ = lambda n, x, ys, fn, shape: wgrad(x, ys, fn, shape, tm=_wgrad_tile(t), name=f"wgrad_{n}_{tag}")[0]
    (dga, dgb, dpa, dpb, do, dpool, dps, dpw), first = merge_bwd(dh1, s["z"], s["pa"], s["pb"], s["pooled"], p["pw"], p["ps"],
                                                                   p["wpa"], p["wpb"], p["wo"], tm=_tile(t, TILE_ROWS),
                                                                   name=f"merge_bwd_{tag}", ride=ride)
    c_o = wg_("o", s["mg"], [dh1], _chunks_rows, (d // N_DEV, d))
    c_pa = wg_("pa", s["a"], [dpa], _chunks_cols, (POOL_WIDTH, d // N_DEV))
    c_pb, theirs = wgrad(s["o"], [dpb], _chunks_rows, (d // N_DEV, d), tm=_wgrad_tile(t), name=f"wgrad_pb_{tag}",
                         ride=comm.pair_exchange([c_o, c_pa]))
    early = comm.pair_add([c_o, c_pa], theirs, EARLY, f"early_{tag}")
    (dq, dk, dv), brought = attn_bwd(s["q"], s["k"], s["v"], s["o"], do, s["lse"], nb=nb, lp=lp, hb=HEADS_BWD,
                                     name=f"attn_bwd_{tag}", ride=next_ride(first, early))
    dh, hn, dz, cqn, ckvn, dqb, dkb, dvb, dg_mix, dgq, dgkv = in_proj_bwd(
        dh1, s["h"], p["g_mix"], s["z"], dq, dk, dv, dga, dgb, dpool, p["win"], p["gq"], p["gkv"], p["wuq"], p["wuk"], p["wuv"],
        rope, tm=_tile(t, TILE_ROWS_BWD), lp=lp, nb=nb, name=f"in_proj_bwd_{tag}")
    c_in = wg_("in", hn, [dz], _chunks_w_in, (d, D_IN // N_DEV))
    c_uq = wg_("uq", cqn, [dqb], _chunks_w_uq, (Q_RANK, N_HEADS * HEAD_QK // N_DEV))
    c_ukv = wg_("ukv", ckvn, [dkb, dvb], _chunks_w_ukv, (KV_RANK, N_HEADS * (QK_NOPE + V_DIM) // N_DEV))
    small = dict(pool_scale=dps[0], pool_w=dpw, norm_mix_g=dg_mix[0], q_norm_g=dgq[0], kv_norm_g=dgkv[0])
    return dh, [c_in, c_uq, c_ukv, c_pb], small, brought


def train_step(x, loss_target, small, comm):
    nb, seq, d = x.shape
    lp = -(-(N_META + seq) // LANES) * LANES
    t = nb * lp
    assert nb <= 2 and DEPTH == 2
    tm = _tile(t, TILE_ROWS)
    rope = _rope_table(lp, nb)
    gathered, meta = comm.first_weights()
    pad = jnp.zeros((nb, lp - N_META - seq, d), F32)
    h = jnp.concatenate([jnp.broadcast_to(meta[None], (nb, N_META, d)), x, pad], axis=1).reshape(t, d)
    target = jnp.concatenate([jnp.zeros((nb, N_META, d), F32), loss_target, pad], axis=1).reshape(t, d)

    params, saved, full = [], [], {}
    for l in range(DEPTH):
        p = _small_operands(small, l)
        p.update(_operands(gathered, MIX_IN, 0) if l == 0 else _operands(comm.carried(0, full[0], MIX, 1), MIX, 1))
        (z, q, k, v), early = in_proj_fwd(h, p["g_mix"], p["win"], p["gq"], p["gkv"], p["wuq"], p["wuk"], p["wuv"], rope, tm=tm,
                                          name=f"in_proj_fwd_{l}", ride=comm.early_first_hop() if l == 0 else None)
        (o, lse), landed = attn_fwd(q, k, v, nb=nb, lp=lp, hb=HEADS_FWD, name=f"attn_fwd_{l}", ride=comm.first_hop(l))
        if l == 0:
            p.update(_operands(comm.early_weights(early), MIX_OUT, 0))
        (h1, pooled, a, pa, pb, mg), full[l] = merge_fwd(h, z, o, p["pw"], p["ps"], p["wpa"], p["wpb"], p["wo"], tm=tm, lp=lp,
                                                          nb=nb, name=f"merge_fwd_{l}", ride=comm.second_hop(l, landed))
        p.update(_operands(comm.carried(l, full[l], FFN, l), FFN, l))
        h2, gt, up = ffn_fwd(h1, p["g_ffn"], p["wgt"], p["wut"], p["wd"], tm=tm, name=f"ffn_fwd_{l}")
        params.append(p)
        saved.append(dict(h=h, z=z, q=q, k=k, v=v, o=o, lse=lse, h1=h1, pooled=pooled, a=a, pa=pa, pb=pb, mg=mg, gt=gt, up=up))
        h = h2
    parts, dh, dgf = loss_head(h, small["final_norm_g"][None], target, tm=tm, lp=lp, nb=nb, seq=seq, name="loss_head")
    loss = jnp.sum(parts[::8, 0])

    sums = {}
    dh, c_ffn1, small1, _ = _ffn_bwd_part(dh, params[1], saved[1], 1, None)
    dh, c_late1, sm, brought = _mix_bwd_part(
        dh, params[1], saved[1], rope, nb, lp, 1, comm, comm.pair_exchange(c_ffn1),
        lambda theirs, early: comm.scatter(comm.pair_add(c_ffn1, theirs, FFN, "ffn_1") + early))
    small1.update(sm)
    sums.update({(n, 1): a for n, a in zip(FFN + EARLY, brought)})
    dh, c_ffn0, small0, theirs = _ffn_bwd_part(dh, params[0], saved[0], 0, comm.pair_exchange(c_late1))
    s_late1 = comm.pair_add(c_late1, theirs, LATE, "late_1")
    upper = _pack([small1[n] for n in SMALL_PER_LAYER] + [dgf[0]], 8)
    dh, c_late0, sm, brought = _mix_bwd_part(
        dh, params[0], saved[0], rope, nb, lp, 0, comm, comm.pair_exchange(c_ffn0),
        lambda theirs, early: comm.scatter_and_gather(s_late1 + comm.pair_add(c_ffn0, theirs, FFN, "ffn_0") + early, upper))
    small0.update(sm)
    brought, upper_parts = comm.scattered_and_gathered(brought, upper)
    sums.update({(n, l): a for (n, l), a in zip([(n, 1) for n in LATE] + [(n, 0) for n in FFN + EARLY], brought)})
    dh = dh.reshape(nb, lp, d)
    dmeta = jnp.sum(dh[:, :N_META], axis=0)
    meta_chunks = jnp.transpose(dmeta.reshape(N_META, N_CHIPS, 2, d // N_DEV), (2, 1, 0, 3)).astype(BF16)
    small_grads = {n: jnp.stack([small0[n], small1[n]]) for n in small0}
    small_grads["final_norm_g"] = dgf[0]
    last_names = LATE + ("meta_tokens",)
    lower = _pack([small0[n] for n in SMALL_PER_LAYER], 8)
    theirs, lower_parts = comm.last_pair_exchange(c_late0 + [meta_chunks], lower)
    last = comm.scatter_now(comm.pair_add(c_late0 + [meta_chunks], theirs, last_names, "late_0"), "scatter_late_0")
    sums.update({(n, 0): a for n, a in zip(last_names, last)})
    return loss, dh[:, N_META:N_META + seq], sums, small_grads, (lower_parts, upper_parts)


def kernel(x, meta_tokens, norm_mix_g, w_in, pool_w, pool_scale, q_norm_g, kv_norm_g, w_uq, w_ukv, w_pa, w_pb, w_o, norm_ffn_g, w_gate, w_up, w_down, final_norm_g, loss_target, m_meta_tokens, m_norm_mix_g, m_w_in, m_pool_w, m_pool_scale, m_q_norm_g, m_kv_norm_g, m_w_uq, m_w_ukv, m_w_pa, m_w_pb, m_w_o, m_norm_ffn_g, m_w_gate, m_w_up, m_w_down, m_final_norm_g, v_meta_tokens, v_norm_mix_g, v_w_in, v_pool_w, v_pool_scale, v_q_norm_g, v_kv_norm_g, v_w_uq, v_w_ukv, v_w_pa, v_w_pb, v_w_o, v_norm_ffn_g, v_w_gate, v_w_up, v_w_down, v_final_norm_g):
    args = dict(locals())
    w = {n: args[n] for n in WEIGHTS}
    m = {n: args["m_" + n] for n in WEIGHTS}
    v = {n: args["v_" + n] for n in WEIGHTS}
    small = {n: w[n] for n in SMALL}
    as_handled = lambda a, n: jnp.swapaxes(a, 1, 2) if n in TRANSPOSED else a
    wh, mh, vh = ({n: as_handled(d[n], n) for n, _ in BIG} for d in (w, m, v))

    loss, grad_x, sums, _, small_recv = train_step(x, loss_target, small, MeshComm(wh, meta_tokens))
    loss = lax.psum(loss, ("x", "y", "c"))

    out = {n: [as_handled(a, n) for a in adamw(wh[n], mh[n], vh[n], [sums[(n, l)] for l in range(DEPTH)], name=f"adamw_{n}")]
           for n, _ in BIG}
    out["meta_tokens"] = [a[0] for a in adamw(meta_tokens[None], m["meta_tokens"][None], v["meta_tokens"][None],
                                              [sums[("meta_tokens", 0)]], name="adamw_meta_tokens")]
    lower_parts, upper_parts = small_recv
    pk_lower = lambda d: _pack([d[n][0] for n in SMALL_PER_LAYER], 8)[None]
    pk_upper = lambda d: _pack([d[n][1] for n in SMALL_PER_LAYER] + [d["final_norm_g"]], 8)[None]
    lower = adamw(pk_lower(w), pk_lower(m), pk_lower(v), [lower_parts], name="adamw_small_lower")
    upper = adamw(pk_upper(w), pk_upper(m), pk_upper(v), [upper_parts], name="adamw_small_upper")
    shapes = [w[n].shape[1:] for n in SMALL_PER_LAYER]
    for n in SMALL:
        out[n] = [None] * 4
    for kind in range(4):
        lo = _unpack(lower[kind][0], shapes)
        up = _unpack(upper[kind][0], shapes + [w["final_norm_g"].shape])
        for j, n in enumerate(SMALL_PER_LAYER):
            out[n][kind] = jnp.stack([lo[j], up[j]])
        out["final_norm_g"][kind] = up[-1]
    return (loss, grad_x, *[out[n][kind] for kind in range(4) for n in WEIGHTS])
```

```python
import functools
import math

import jax
import jax.numpy as jnp
from jax import lax
from jax.experimental import pallas as pl
from jax.experimental.pallas import tpu as pltpu

F32, BF16 = jnp.float32, jnp.bfloat16

D_MODEL = 1024
N_META = 16
N_HEADS = 16
QK_NOPE, QK_ROPE, V_DIM = 64, 32, 64
HALF_ROPE = QK_ROPE // 2
Q_RANK, KV_RANK = 256, 128
POOL_WINDOWS = (2, 4, 8, 16)
POOL_GROUP = 128
POOL_WIDTH = POOL_GROUP * len(POOL_WINDOWS)
POOL_HALO = 16
D_FF = 2816
D_IN = 2976
NORM_EPS = 1e-6
SM_SCALE = (QK_NOPE + QK_ROPE) ** -0.5
LOG2E = math.log2(math.e)
EXP2_SCALE = SM_SCALE * LOG2E
MASK_VALUE = -1e30
ROPE_THETA = 10000.0
DEPTH = 2
N_DEV = 8

ADAM_LR, ADAM_B1, ADAM_B2, ADAM_EPS, ADAM_WD, ADAM_STEP = 0.001, 0.9, 0.999, 1e-08, 0.01, 10

LANES = 128
HEAD_SLOT = LANES
QK_WIDTH = N_HEADS * HEAD_SLOT
Z_CQ, Z_CKV, Z_KR, Z_GA, Z_GB, DZ = 512, 768, 896, 1024, 2048, 3072
TQ = TK = 256
KEY_BLOCKS_PER_STEP = 3
VMEM_LIMIT = 56 * 1024 * 1024


def _cparams():
    return pltpu.CompilerParams(vmem_limit_bytes=VMEM_LIMIT)


def _rows(tm, width, col=0):
    return pl.BlockSpec((tm, width), lambda i: (i, col))


def _whole(shape):
    zeros = (0,) * len(shape)
    return pl.BlockSpec(shape, lambda i: zeros, pipeline_mode=pl.Buffered(1))


def _acc(shape):
    zeros = (0,) * len(shape)
    return pl.BlockSpec(shape, lambda i: zeros)


def _dot(a, b):
    return jnp.dot(a, b, preferred_element_type=F32)


def _dot_tn(a, b):
    return lax.dot_general(a, b, (((0,), (0,)), ((), ())), preferred_element_type=F32)


def _dot_nt(a, b):
    return lax.dot_general(a, b, (((1,), (1,)), ((), ())), preferred_element_type=F32)


def _rms(x):
    r = lax.rsqrt(jnp.mean(x * x, axis=-1, keepdims=True) + NORM_EPS)
    return x * r, r


def _rms_bwd(dy, xhat, r, g):
    dg = jnp.sum(dy * xhat, axis=0, keepdims=True)
    dxh = dy * g
    dx = r * (dxh - xhat * jnp.mean(dxh * xhat, axis=-1, keepdims=True))
    return dx, dg


def _sigmoid(x):
    return 1.0 / (1.0 + jnp.exp(-x))


def _rope_fwd(q, c, s1, s2):
    w = q.shape[1]
    return q * c + pltpu.roll(q, w - HALF_ROPE, 1) * s1 + pltpu.roll(q, HALF_ROPE, 1) * s2


def _rope_bwd(dq, c, s1, s2):
    w = dq.shape[1]
    return dq * c + pltpu.roll(dq * s1, HALF_ROPE, 1) + pltpu.roll(dq * s2, w - HALF_ROPE, 1)


def _rope_tables(rope, reps):
    c, cr, s1, s2 = (rope[:, k * LANES:(k + 1) * LANES] for k in range(4))
    if reps > 1:
        return jnp.tile(c, (1, reps)), jnp.tile(s1, (1, reps)), jnp.tile(s2, (1, reps))
    return cr, s1, s2


def _seq_pos(gi, lp, nb):
    pos = gi
    for b in range(1, nb):
        pos = jnp.where(gi >= b * lp, gi - b * lp, pos)
    return pos


_ANY = pl.BlockSpec(memory_space=pl.ANY)


def _carrying_call(body, ride, operands, *, name, grid, in_specs, out_specs, out_shape, scratch_shapes=()):
    n_in, n_out = len(in_specs), len(out_specs)
    if ride is None:
        out = pl.pallas_call(body, name=name, grid=grid, in_specs=in_specs, out_specs=out_specs, out_shape=out_shape,
                             scratch_shapes=list(scratch_shapes), compiler_params=_cparams())(*operands)
        return out, []
    ne = len(ride.arrays)

    def carrying(*refs):
        ins, r_in, rest = refs[:n_in], refs[n_in:n_in + ne], refs[n_in + ne:]
        outs, r_out, rest = rest[:n_out], rest[n_out:n_out + ne], rest[n_out + ne:]
        scratch, sems = rest[:len(scratch_shapes)], rest[len(scratch_shapes):]
        ids = [pl.program_id(a) for a in range(len(grid))]
        first = functools.reduce(jnp.logical_and, [i == 0 for i in ids])
        last = functools.reduce(jnp.logical_and, [i == g - 1 for i, g in zip(ids, grid)])

        @pl.when(first)
        def _():
            ride.start(r_in, r_out, sems)

        body(*ins, *outs, *scratch)

        @pl.when(last)
        def _():
            ride.wait(r_in, r_out, sems)

    out = pl.pallas_call(
        carrying, name=name, grid=grid, in_specs=list(in_specs) + [_ANY] * ne, out_specs=list(out_specs) + [_ANY] * ne,
        out_shape=list(out_shape) + ride.out_shapes, scratch_shapes=list(scratch_shapes) + ride.scratch,
        input_output_aliases=ride.aliases(n_in, n_out), compiler_params=_cparams(),
    )(*operands, *ride.arrays)
    return out[:n_out], out[n_out:]


def in_proj_fwd(h, g_mix, win, gq, gkv, wuq, wuk, wuv, rope, *, tm, name, ride=None):
    t = h.shape[0]

    def body(h_ref, g_ref, win_ref, gq_ref, gkv_ref, wuq_ref, wuk_ref, wuv_ref, rope_ref, z_ref, q_ref, k_ref, v_ref):
        xhat, _ = _rms(h_ref[...])
        hn = (xhat * g_ref[...]).astype(BF16)
        z = _dot(hn, win_ref[...])
        z_ref[...] = z
        rope_t = rope_ref[...]
        xq, _ = _rms(z[:, Z_CQ:Z_CKV])
        cqn = (xq * gq_ref[...]).astype(BF16)
        q = _rope_fwd(_dot(cqn, wuq_ref[...]), *_rope_tables(rope_t, N_HEADS))
        q_ref[...] = q.astype(BF16)
        xkv, _ = _rms(z[:, Z_CKV:Z_KR])
        ckvn = (xkv * gkv_ref[...]).astype(BF16)
        kr = _rope_fwd(z[:, Z_KR:Z_GA], *_rope_tables(rope_t, 1))
        k_ref[...] = (_dot(ckvn, wuk_ref[...]) + jnp.tile(kr, (1, N_HEADS))).astype(BF16)
        v_ref[...] = _dot(ckvn, wuv_ref[...]).astype(BF16)

    return _carrying_call(
        body, ride, (h, g_mix, win, gq, gkv, wuq, wuk, wuv, rope), name=name, grid=(t // tm,),
        in_specs=[_rows(tm, D_MODEL), _whole((1, D_MODEL)), _whole((D_MODEL, DZ)), _whole((1, Q_RANK)), _whole((1, KV_RANK)),
                  _whole((Q_RANK, QK_WIDTH)), _whole((KV_RANK, QK_WIDTH)), _whole((KV_RANK, D_MODEL)), _rows(tm, 4 * LANES)],
        out_specs=[_rows(tm, DZ), _rows(tm, QK_WIDTH), _rows(tm, QK_WIDTH), _rows(tm, D_MODEL)],
        out_shape=[jax.ShapeDtypeStruct((t, DZ), F32), jax.ShapeDtypeStruct((t, QK_WIDTH), BF16),
                   jax.ShapeDtypeStruct((t, QK_WIDTH), BF16), jax.ShapeDtypeStruct((t, D_MODEL), BF16)])


def attn_fwd(q, k, v, *, nb, lp, hb, name, ride=None):
    t = q.shape[0]
    nq, tail = lp // TQ, lp % TQ
    assert tail % LANES == 0

    def body(q_ref, k_ref, v_ref, o_ref, lse_ref, vt):
        for pr in range(hb // 2):
            vt[pr] = v_ref[:, pr * LANES:(pr + 1) * LANES].T

        def q_block(qs, tq, whole_k):
            qh = [q_ref[pl.ds(qs, tq), hd * HEAD_SLOT:(hd + 1) * HEAD_SLOT] for hd in range(hb)]
            keep = lax.broadcasted_iota(jnp.int32, (tq, tq), 0) <= lax.broadcasted_iota(jnp.int32, (tq, tq), 1)

            def k_steps(blocks, c, masked):
                sts = [[_dot_nt(k_ref[pl.ds(ks, tk), hd * HEAD_SLOT:(hd + 1) * HEAD_SLOT], qh[hd]) for hd in range(hb)]
                       for ks, tk in blocks]
                for (ks, tk), st_b in zip(blocks, sts):
                    ps, stats = [], []
                    for hd in range(hb):
                        m, l, _ = c[hd]
                        st = jnp.where(keep, st_b[hd], MASK_VALUE) if masked else st_b[hd]
                        m_new = jnp.maximum(m, jnp.max(st, axis=0, keepdims=True))
                        p = jnp.exp2((st - m_new) * EXP2_SCALE)
                        alpha = jnp.exp2((m - m_new) * EXP2_SCALE)
                        ps.append(p.astype(BF16))
                        stats.append((m_new, alpha * l + jnp.sum(p, axis=0, keepdims=True), alpha))
                    pvs = [_dot(vt[hd // 2, :, pl.ds(ks, tk)], ps[hd]) for hd in range(hb)]
                    c = tuple((stats[hd][0], stats[hd][1], stats[hd][2] * c[hd][2] + pvs[hd]) for hd in range(hb))
                return c

            def some_blocks(i, c):
                ks = pl.multiple_of(KEY_BLOCKS_PER_STEP * i * TK, TK)
                return k_steps([(ks + j * TK, TK) for j in range(KEY_BLOCKS_PER_STEP)], c, False)

            init = tuple((jnp.full((1, tq), MASK_VALUE, F32), jnp.zeros((1, tq), F32), jnp.zeros((LANES, tq), F32))
                         for _ in range(hb))
            groups = lax.div(whole_k, KEY_BLOCKS_PER_STEP)
            c = lax.fori_loop(0, groups, some_blocks, init)
            c = lax.fori_loop(KEY_BLOCKS_PER_STEP * groups, whole_k,
                              lambda kj, c: k_steps([(pl.multiple_of(kj * TK, TK), TK)], c, False), c)
            c = k_steps([(qs, tq)], c, True)
            sub = lax.broadcasted_iota(jnp.int32, (LANES, tq), 0)
            for pr in range(hb // 2):
                (m0, l0, a0), (m1, l1, a1) = c[2 * pr], c[2 * pr + 1]
                o_ref[pl.ds(qs, tq), pr * LANES:(pr + 1) * LANES] = jnp.where(sub < V_DIM, a0 / l0, a1 / l1).T.astype(BF16)
                lse_ref[2 * pr, :, pl.ds(qs, tq)] = m0 * SM_SCALE + jnp.log(l0)
                lse_ref[2 * pr + 1, :, pl.ds(qs, tq)] = m1 * SM_SCALE + jnp.log(l1)

        def whole_q_block(qi, carry):
            q_block(pl.multiple_of(qi * TQ, TQ), TQ, qi)
            return carry

        lax.fori_loop(0, nq, whole_q_block, 0)
        if tail:
            q_block(nq * TQ, tail, nq)

    blk = lambda w: pl.BlockSpec((lp, w), lambda b, g: (b, g))
    return _carrying_call(
        body, ride, (q, k, v), name=name, grid=(nb, N_HEADS // hb),
        in_specs=[blk(hb * HEAD_SLOT), blk(hb * HEAD_SLOT), blk(hb * V_DIM)],
        out_specs=[blk(hb * V_DIM), pl.BlockSpec((hb, 1, lp), lambda b, g: (g, 0, b))],
        out_shape=[jax.ShapeDtypeStruct((t, D_MODEL), BF16), jax.ShapeDtypeStruct((N_HEADS, 1, t), F32)],
        scratch_shapes=[pltpu.VMEM((hb // 2, LANES, lp), BF16)])


def _pool_band_fwd(i, tm, lp, nb):
    r = lax.broadcasted_iota(jnp.int32, (tm, POOL_HALO + tm), 0)
    e = lax.broadcasted_iota(jnp.int32, (tm, POOL_HALO + tm), 1)
    diff = r + POOL_HALO - e
    pos = _seq_pos(i * tm + lax.broadcasted_iota(jnp.int32, (tm, 1), 0), lp, nb)
    out = []
    for w in POOL_WINDOWS:
        cnt = jnp.minimum(pos + 1, w)
        band = jnp.where((diff >= 0) & (diff < cnt), 1.0, 0.0).astype(BF16)
        out.append((band, cnt.astype(F32)))
    return out


def merge_fwd(h, z, o, pw, ps, wpa, wpb, wo, *, tm, lp, nb, name, ride=None):
    t = h.shape[0]
    hb = tm // POOL_HALO

    def body(h_ref, u_ref, uprev_ref, ga_ref, gb_ref, o_ref, pw_ref, ps_ref, wpa_ref, wpb_ref, wo_ref,
             h1_ref, pooled_ref, a_ref, pa_ref, pb_ref, mg_ref):
        i = pl.program_id(0)
        u = u_ref[...]
        uext = jnp.concatenate([uprev_ref[...], u], axis=0).astype(BF16)
        pooled, ys = [], []
        for g, (band, cnt) in enumerate(_pool_band_fwd(i, tm, lp, nb)):
            gs = slice(g * POOL_GROUP, (g + 1) * POOL_GROUP)
            pg = (_dot(band, uext[:, gs]) / cnt - u[:, gs]).astype(BF16)
            pooled.append(pg)
            ys.append(_dot(pg, pw_ref[g]))
        pooled_ref[...] = jnp.concatenate(pooled, axis=1)
        a = (jnp.concatenate(ys, axis=1) * ps_ref[...]).astype(BF16)
        a_ref[...] = a
        pa = _dot(a, wpa_ref[...])
        pb = _dot(o_ref[...], wpb_ref[...])
        pa_ref[...] = pa.astype(BF16)
        pb_ref[...] = pb.astype(BF16)
        mg = (_sigmoid(ga_ref[...]) * pa + _sigmoid(gb_ref[...]) * pb).astype(BF16)
        mg_ref[...] = mg
        h1_ref[...] = h_ref[...] + _dot(mg, wo_ref[...])

    halo = pl.BlockSpec((POOL_HALO, POOL_WIDTH), lambda i: (jnp.maximum(i * hb - 1, 0), 0))
    return _carrying_call(
        body, ride, (h, z, z, z, z, o, pw, ps, wpa, wpb, wo), name=name, grid=(t // tm,),
        in_specs=[_rows(tm, D_MODEL), _rows(tm, POOL_WIDTH), halo, _rows(tm, D_MODEL, 1), _rows(tm, D_MODEL, 2), _rows(tm, D_MODEL),
                  _whole((4, POOL_GROUP, POOL_GROUP)), _whole((1, POOL_WIDTH)), _whole((POOL_WIDTH, D_MODEL)),
                  _whole((D_MODEL, D_MODEL)), _whole((D_MODEL, D_MODEL))],
        out_specs=[_rows(tm, D_MODEL), _rows(tm, POOL_WIDTH), _rows(tm, POOL_WIDTH), _rows(tm, D_MODEL), _rows(tm, D_MODEL),
                   _rows(tm, D_MODEL)],
        out_shape=[jax.ShapeDtypeStruct((t, D_MODEL), F32), jax.ShapeDtypeStruct((t, POOL_WIDTH), BF16),
                   jax.ShapeDtypeStruct((t, POOL_WIDTH), BF16), jax.ShapeDtypeStruct((t, D_MODEL), BF16),
                   jax.ShapeDtypeStruct((t, D_MODEL), BF16), jax.ShapeDtypeStruct((t, D_MODEL), BF16)])


def ffn_fwd(h1, g, wgt, wut, wd, *, tm, name):
    t = h1.shape[0]

    def body(h_ref, g_ref, wgt_ref, wut_ref, wd_ref, h2_ref, gt_ref, up_ref):
        h = h_ref[...]
        xhat, _ = _rms(h)
        hn = (xhat * g_ref[...]).astype(BF16)
        gt = _dot_nt(hn, wgt_ref[...])
        up = _dot_nt(hn, wut_ref[...])
        gt_ref[...] = gt.astype(BF16)
        up_ref[...] = up.astype(BF16)
        act = (gt * _sigmoid(gt) * up).astype(BF16)
        h2_ref[...] = h + _dot(act, wd_ref[...])

    return pl.pallas_call(
        body, name=name, grid=(t // tm,),
        in_specs=[_rows(tm, D_MODEL), _whole((1, D_MODEL)), _whole((D_FF, D_MODEL)), _whole((D_FF, D_MODEL)), _whole((D_FF, D_MODEL))],
        out_specs=[_rows(tm, D_MODEL), _rows(tm, D_FF), _rows(tm, D_FF)],
        out_shape=[jax.ShapeDtypeStruct((t, D_MODEL), F32), jax.ShapeDtypeStruct((t, D_FF), BF16), jax.ShapeDtypeStruct((t, D_FF), BF16)],
        compiler_params=_cparams(),
    )(h1, g, wgt, wut, wd)


def loss_head(h, g, target, *, tm, lp, nb, seq, name):
    t = h.shape[0]
    nt = t // tm

    def body(h_ref, g_ref, t_ref, loss_ref, dh_ref, dg_ref):
        i = pl.program_id(0)
        pos = _seq_pos(i * tm + lax.broadcasted_iota(jnp.int32, (tm, 1), 0), lp, nb)
        real = (pos >= N_META) & (pos < N_META + seq)
        xhat, r = _rms(h_ref[...])
        gg = g_ref[...]
        err = jnp.where(real, xhat * gg - t_ref[...], 0.0)
        loss_ref[...] = jnp.full((8, LANES), 0.5 * jnp.sum(err * err) / D_MODEL, F32)
        dx, dg = _rms_bwd(err * (1.0 / D_MODEL), xhat, r, gg)
        dh_ref[...] = dx

        @pl.when(i == 0)
        def _():
            dg_ref[...] = jnp.zeros_like(dg_ref)

        dg_ref[...] += dg

    return pl.pallas_call(
        body, name=name, grid=(nt,),
        in_specs=[_rows(tm, D_MODEL), _whole((1, D_MODEL)), _rows(tm, D_MODEL)],
        out_specs=[pl.BlockSpec((8, LANES), lambda i: (i, 0)), _rows(tm, D_MODEL), _acc((1, D_MODEL))],
        out_shape=[jax.ShapeDtypeStruct((nt * 8, LANES), F32), jax.ShapeDtypeStruct((t, D_MODEL), F32),
                   jax.ShapeDtypeStruct((1, D_MODEL), F32)],
        compiler_params=_cparams(),
    )(h, g, target)


def wgrad(x, ys, chunk_fn, chunk_shape, *, tm, name, ride=None):
    t, m = x.shape
    tiles = t // tm
    steps = -(-tiles // 2)

    def body(*refs):
        ins, o_ref, accs = refs[:2 * (1 + len(ys))], refs[2 * (1 + len(ys))], refs[2 * (1 + len(ys)) + 1:]
        i = pl.program_id(0)

        @pl.when(i == 0)
        def _():
            for acc in accs:
                acc[...] = jnp.zeros_like(acc)

        def both(first, second, mask):
            b = second[...].astype(BF16)
            if mask and tiles % 2:
                b = jnp.where(2 * i + 1 < tiles, b, jnp.zeros_like(b))
            return jnp.concatenate([first[...].astype(BF16), b], axis=0)

        xb = both(ins[0], ins[1], True)
        for j, acc in enumerate(accs):
            acc[...] += _dot_tn(xb, both(ins[2 + 2 * j], ins[3 + 2 * j], False))

        @pl.when(i == steps - 1)
        def _():
            for p, chunk in enumerate(chunk_fn(*accs)):
                o_ref[p % 2, p // 2] = chunk.astype(BF16)

    def two_tiles(width):
        return [pl.BlockSpec((tm, width), lambda i: (2 * i, 0)),
                pl.BlockSpec((tm, width), lambda i: (jnp.minimum(2 * i + 1, tiles - 1), 0))]

    out = (2, N_DEV // 2) + tuple(chunk_shape)
    operands = [x, x] + [a for y in ys for a in (y, y)]
    (chunks,), brought = _carrying_call(
        body, ride, operands, name=name, grid=(steps,),
        in_specs=two_tiles(m) + [s for y in ys for s in two_tiles(y.shape[1])], out_specs=[_acc(out)],
        out_shape=[jax.ShapeDtypeStruct(out, BF16)], scratch_shapes=[pltpu.VMEM((m, y.shape[1]), F32) for y in ys])
    return chunks, brought


def ffn_bwd(dh2, h1, g, gt, up, wgt, wut, wd, *, tm, name, ride=None):
    t = h1.shape[0]

    def body(dh2_ref, h_ref, g_ref, gt_ref, up_ref, wgt_ref, wut_ref, wd_ref, dh1_ref, hn_ref, act_ref, dgt_ref, dup_ref, dg_ref):
        dh2 = dh2_ref[...]
        dact = _dot_nt(dh2.astype(BF16), wd_ref[...])
        gt = gt_ref[...].astype(F32)
        up = up_ref[...].astype(F32)
        sg = _sigmoid(gt)
        silu = gt * sg
        act_ref[...] = (silu * up).astype(BF16)
        dgt = (dact * up * (sg * (1.0 + gt * (1.0 - sg)))).astype(BF16)
        dup = (dact * silu).astype(BF16)
        dgt_ref[...] = dgt
        dup_ref[...] = dup
        dhn = _dot(dgt, wgt_ref[...]) + _dot(dup, wut_ref[...])
        xhat, r = _rms(h_ref[...])
        gg = g_ref[...]
        hn_ref[...] = (xhat * gg).astype(BF16)
        dx, dg = _rms_bwd(dhn, xhat, r, gg)
        dh1_ref[...] = dh2 + dx

        @pl.when(pl.program_id(0) == 0)
        def _():
            dg_ref[...] = jnp.zeros_like(dg_ref)

        dg_ref[...] += dg

    return _carrying_call(
        body, ride, (dh2, h1, g, gt, up, wgt, wut, wd), name=name, grid=(t // tm,),
        in_specs=[_rows(tm, D_MODEL), _rows(tm, D_MODEL), _whole((1, D_MODEL)), _rows(tm, D_FF), _rows(tm, D_FF),
                  _whole((D_FF, D_MODEL)), _whole((D_FF, D_MODEL)), _whole((D_FF, D_MODEL))],
        out_specs=[_rows(tm, D_MODEL), _rows(tm, D_MODEL), _rows(tm, D_FF), _rows(tm, D_FF), _rows(tm, D_FF), _acc((1, D_MODEL))],
        out_shape=[jax.ShapeDtypeStruct((t, D_MODEL), F32), jax.ShapeDtypeStruct((t, D_MODEL), BF16),
                   jax.ShapeDtypeStruct((t, D_FF), BF16), jax.ShapeDtypeStruct((t, D_FF), BF16),
                   jax.ShapeDtypeStruct((t, D_FF), BF16), jax.ShapeDtypeStruct((1, D_MODEL), F32)])


def merge_bwd(dh1, z, pa, pb, pooled, pw, ps, wpa, wpb, wo, *, tm, name, ride=None):
    t = dh1.shape[0]

    def body(dh1_ref, ga_ref, gb_ref, pa_ref, pb_ref, pooled_ref, pw_ref, ps_ref, wpa_ref, wpb_ref, wo_ref,
             dga_ref, dgb_ref, dpa_ref, dpb_ref, do_ref, dpool_ref, dps_ref, dpw_ref):
        dmg = _dot_nt(dh1_ref[...].astype(BF16), wo_ref[...])
        sa = _sigmoid(ga_ref[...])
        sb = _sigmoid(gb_ref[...])
        dga_ref[...] = (dmg * pa_ref[...].astype(F32) * sa * (1.0 - sa)).astype(BF16)
        dgb_ref[...] = (dmg * pb_ref[...].astype(F32) * sb * (1.0 - sb)).astype(BF16)
        dpa = (dmg * sa).astype(BF16)
        dpb = (dmg * sb).astype(BF16)
        dpa_ref[...] = dpa
        dpb_ref[...] = dpb
        do_ref[...] = _dot_nt(dpb, wpb_ref[...]).astype(BF16)
        da = _dot_nt(dpa, wpa_ref[...])
        pooled = pooled_ref[...]
        ps = ps_ref[...]

        @pl.when(pl.program_id(0) == 0)
        def _():
            dps_ref[...] = jnp.zeros_like(dps_ref)
            dpw_ref[...] = jnp.zeros_like(dpw_ref)

        dps, dpool = [], []
        for g in range(len(POOL_WINDOWS)):
            gs = slice(g * POOL_GROUP, (g + 1) * POOL_GROUP)
            y = _dot(pooled[:, gs], pw_ref[g])
            dps.append(jnp.sum(da[:, gs] * y, axis=0, keepdims=True))
            dy = (da[:, gs] * ps[:, gs]).astype(BF16)
            dpool.append(_dot_nt(dy, pw_ref[g]))
            dpw_ref[g] += _dot_tn(pooled[:, gs], dy)
        dps_ref[...] += jnp.concatenate(dps, axis=1)
        dpool_ref[...] = jnp.concatenate(dpool, axis=1)

    return _carrying_call(
        body, ride, (dh1, z, z, pa, pb, pooled, pw, ps, wpa, wpb, wo), name=name, grid=(t // tm,),
        in_specs=[_rows(tm, D_MODEL), _rows(tm, D_MODEL, 1), _rows(tm, D_MODEL, 2), _rows(tm, D_MODEL), _rows(tm, D_MODEL),
                  _rows(tm, POOL_WIDTH), _whole((4, POOL_GROUP, POOL_GROUP)),
                  _whole((1, POOL_WIDTH)), _whole((POOL_WIDTH, D_MODEL)), _whole((D_MODEL, D_MODEL)), _whole((D_MODEL, D_MODEL))],
        out_specs=[_rows(tm, D_MODEL), _rows(tm, D_MODEL), _rows(tm, D_MODEL), _rows(tm, D_MODEL), _rows(tm, D_MODEL),
                   _rows(tm, POOL_WIDTH), _acc((1, POOL_WIDTH)), _acc((4, POOL_GROUP, POOL_GROUP))],
        out_shape=[jax.ShapeDtypeStruct((t, D_MODEL), BF16)] * 5
        + [jax.ShapeDtypeStruct((t, POOL_WIDTH), F32), jax.ShapeDtypeStruct((1, POOL_WIDTH), F32),
           jax.ShapeDtypeStruct((4, POOL_GROUP, POOL_GROUP), F32)])


def attn_bwd(q, k, v, o, do, lse, *, nb, lp, hb, name, ride=None):
    t = q.shape[0]
    nq, tail = lp // TQ, lp % TQ
    assert tail % LANES == 0

    def body(q_ref, k_ref, v_ref, o_ref, do_ref, lse_ref, dq_ref, dk_ref, dv_ref, kt, doh, lse_row, delta_row, dqt):
        lane = lax.broadcasted_iota(jnp.int32, (lp, LANES), 1)
        first = lane < V_DIM
        sub = lax.broadcasted_iota(jnp.int32, (LANES, lp), 0)
        for pr in range(hb // 2):
            ls = slice(pr * LANES, (pr + 1) * LANES)
            do = do_ref[:, ls]
            doh[2 * pr] = jnp.where(first, do, jnp.zeros_like(do))
            doh[2 * pr + 1] = jnp.where(first, jnp.zeros_like(do), do)
            prod_t = (do.astype(F32) * o_ref[:, ls].astype(F32)).T
            delta_row[2 * pr] = jnp.sum(jnp.where(sub < V_DIM, prod_t, 0.0), axis=0, keepdims=True)
            delta_row[2 * pr + 1] = jnp.sum(jnp.where(sub < V_DIM, 0.0, prod_t), axis=0, keepdims=True)
        for hd in range(hb):
            lse_row[hd] = lse_ref[hd] * LOG2E
            kt[hd] = k_ref[:, hd * HEAD_SLOT:(hd + 1) * HEAD_SLOT].T
        dqt[...] = jnp.zeros(dqt.shape, F32)
        heads = range(hb)
        hss = [slice(hd * HEAD_SLOT, (hd + 1) * HEAD_SLOT) for hd in heads]

        def k_block(ks, tk, next_q):
            keep = lax.broadcasted_iota(jnp.int32, (tk, tk), 0) <= lax.broadcasted_iota(jnp.int32, (tk, tk), 1)

            def q_steps(blocks, c, masked):
                work = [(qs, tq, hd) for qs, tq in blocks for hd in heads]
                qhs = [q_ref[pl.ds(qs, tq), hss[hd]] for qs, tq, hd in work]
                dos = [doh[hd, pl.ds(qs, tq), :] for qs, tq, hd in work]
                sts = [_dot_nt(k_ref[pl.ds(ks, tk), hss[hd]], qhs[i]) for i, (_, _, hd) in enumerate(work)]
                dpts = [_dot_nt(v_ref[pl.ds(ks, tk), (hd // 2) * LANES:(hd // 2 + 1) * LANES], dos[i])
                        for i, (_, _, hd) in enumerate(work)]
                pts, dsts = [], []
                for i, (qs, tq, hd) in enumerate(work):
                    st = jnp.where(keep, sts[i], MASK_VALUE) if masked else sts[i]
                    pt = jnp.exp2(st * EXP2_SCALE - lse_row[hd, :, pl.ds(qs, tq)])
                    dsts.append((pt * (dpts[i] - delta_row[hd, :, pl.ds(qs, tq)])).astype(BF16))
                    pts.append(pt.astype(BF16))
                dvs = [_dot(pts[i], dos[i]) for i in range(len(work))]
                dks = [_dot(dsts[i], qhs[i]) for i in range(len(work))]
                dqs = [_dot(kt[hd, :, pl.ds(ks, tk)], dsts[i]) for i, (_, _, hd) in enumerate(work)]
                c = list(c)
                for i, (qs, tq, hd) in enumerate(work):
                    dqt[hd, :, pl.ds(qs, tq)] += dqs[i]
                    c[hd] = (c[hd][0] + dks[i], c[hd][1] + dvs[i])
                return tuple(c)

            zero = jnp.zeros((tk, LANES), F32)
            c = q_steps([(ks, tk)], tuple((zero, zero) for _ in heads), True)
            if next_q is not None:
                def two_blocks(i, c):
                    qs = pl.multiple_of((next_q + 2 * i) * TQ, TQ)
                    return q_steps([(qs, TQ), (qs + TQ, TQ)], c, False)

                pairs = lax.div(nq - next_q, 2)
                c = lax.fori_loop(0, pairs, two_blocks, c)
                c = lax.fori_loop(next_q + 2 * pairs, nq, lambda qi, c: q_steps([(pl.multiple_of(qi * TQ, TQ), TQ)], c, False), c)
                if tail:
                    c = q_steps([(nq * TQ, tail)], c, False)
            for hd in heads:
                dk_ref[pl.ds(ks, tk), hss[hd]] = c[hd][0] * SM_SCALE
            for pr in range(hb // 2):
                dv_ref[pl.ds(ks, tk), pr * LANES:(pr + 1) * LANES] = c[2 * pr][1] + c[2 * pr + 1][1]

        def whole_k_block(kj, carry):
            k_block(pl.multiple_of(kj * TK, TK), TK, kj + 1)
            return carry

        lax.fori_loop(0, nq, whole_k_block, 0)
        if tail:
            k_block(nq * TQ, tail, None)
        for hd in range(hb):
            dq_ref[:, hd * HEAD_SLOT:(hd + 1) * HEAD_SLOT] = dqt[hd].T * SM_SCALE

    blk = lambda w: pl.BlockSpec((lp, w), lambda b, g: (b, g))
    return _carrying_call(
        body, ride, (q, k, v, o, do, lse), name=name, grid=(nb, N_HEADS // hb),
        in_specs=[blk(hb * HEAD_SLOT), blk(hb * HEAD_SLOT), blk(hb * V_DIM), blk(hb * V_DIM), blk(hb * V_DIM),
                  pl.BlockSpec((hb, 1, lp), lambda b, g: (g, 0, b))],
        out_specs=[blk(hb * HEAD_SLOT), blk(hb * HEAD_SLOT), blk(hb * V_DIM)],
        out_shape=[jax.ShapeDtypeStruct((t, QK_WIDTH), F32), jax.ShapeDtypeStruct((t, QK_WIDTH), F32),
                   jax.ShapeDtypeStruct((t, D_MODEL), F32)],
        scratch_shapes=[pltpu.VMEM((hb, HEAD_SLOT, lp), BF16), pltpu.VMEM((hb, lp, LANES), BF16), pltpu.VMEM((hb, 1, lp), F32),
                        pltpu.VMEM((hb, 1, lp), F32), pltpu.VMEM((hb, HEAD_SLOT, lp), F32)])


def in_proj_bwd(dh1, h, g_mix, z, dq, dk, dv, dga, dgb, dpool, win, gq, gkv, wuq, wuk, wuv, rope, *, tm, lp, nb, name):
    t = h.shape[0]
    hb = tm // POOL_HALO
    last_halo = t // POOL_HALO - 1

    def body(dh1_ref, h_ref, g_ref, zcq_ref, zckv_ref, dq_ref, dk_ref, dv_ref, dga_ref, dgb_ref, dpool_ref, dnext_ref,
             win_ref, gq_ref, gkv_ref, wuq_ref, wuk_ref, wuv_ref, rope_ref,
             dh_ref, hn_ref, dz_ref, cqn_ref, ckvn_ref, dqb_ref, dkb_ref, dvb_ref, dg_ref, dgq_ref, dgkv_ref):
        i = pl.program_id(0)
        rope_t = rope_ref[...]
        dqb = _rope_bwd(dq_ref[...], *_rope_tables(rope_t, N_HEADS)).astype(BF16)
        dqb_ref[...] = dqb
        xq, rq = _rms(zcq_ref[...])
        gq_v = gq_ref[...]
        cqn_ref[...] = (xq * gq_v).astype(BF16)
        dcq, dgq = _rms_bwd(_dot_nt(dqb, wuq_ref[...]), xq, rq, gq_v)
        dk = dk_ref[...]
        dkb = dk.astype(BF16)
        dvb = dv_ref[...].astype(BF16)
        dkb_ref[...] = dkb
        dvb_ref[...] = dvb
        xkv, rkv = _rms(zckv_ref[...])
        gkv_v = gkv_ref[...]
        ckvn_ref[...] = (xkv * gkv_v).astype(BF16)
        dckv, dgkv = _rms_bwd(_dot_nt(dkb, wuk_ref[...]) + _dot_nt(dvb, wuv_ref[...]), xkv, rkv, gkv_v)
        dks = dk[:, :HEAD_SLOT]
        for hd in range(1, N_HEADS):
            dks = dks + dk[:, hd * HEAD_SLOT:(hd + 1) * HEAD_SLOT]
        dzk = _rope_bwd(dks, *_rope_tables(rope_t, 1))
        dp_cur = dpool_ref[...]
        dp_ext = jnp.concatenate([dp_cur, dnext_ref[...]], axis=0)
        r = lax.broadcasted_iota(jnp.int32, (tm, tm + POOL_HALO), 0)
        e = lax.broadcasted_iota(jnp.int32, (tm, tm + POOL_HALO), 1)
        gt_col = i * tm + lax.broadcasted_iota(jnp.int32, (1, tm + POOL_HALO), 1)
        pos_col = _seq_pos(gt_col, lp, nb)
        gt_row = i * tm + lax.broadcasted_iota(jnp.int32, (tm + POOL_HALO, 1), 0)
        pos_row = _seq_pos(gt_row, lp, nb)
        dus = []
        for g, w in enumerate(POOL_WINDOWS):
            gs = slice(g * POOL_GROUP, (g + 1) * POOL_GROUP)
            band = jnp.where((e - r >= 0) & (e - r < jnp.minimum(pos_col + 1, w)) & (gt_col < t), 1.0, 0.0).astype(BF16)
            scaled = jnp.where(gt_row < t, dp_ext[:, gs] / jnp.minimum(pos_row + 1, w).astype(F32), 0.0).astype(BF16)
            dus.append(_dot(band, scaled) - dp_cur[:, gs])
        dz = jnp.concatenate(dus + [dcq, dckv, dzk], axis=1).astype(BF16)
        dz = jnp.concatenate([dz, dga_ref[...], dgb_ref[...]], axis=1)
        dz_ref[...] = dz
        xhat, rr = _rms(h_ref[...])
        gg = g_ref[...]
        hn_ref[...] = (xhat * gg).astype(BF16)
        dx, dg = _rms_bwd(_dot_nt(dz, win_ref[...]), xhat, rr, gg)
        dh_ref[...] = dh1_ref[...] + dx

        @pl.when(i == 0)
        def _():
            dg_ref[...] = jnp.zeros_like(dg_ref)
            dgq_ref[...] = jnp.zeros_like(dgq_ref)
            dgkv_ref[...] = jnp.zeros_like(dgkv_ref)

        dg_ref[...] += dg
        dgq_ref[...] += dgq
        dgkv_ref[...] += dgkv

    nxt = pl.BlockSpec((POOL_HALO, POOL_WIDTH), lambda i: (jnp.minimum((i + 1) * hb, last_halo), 0))
    return pl.pallas_call(
        body, name=name, grid=(t // tm,),
        in_specs=[_rows(tm, D_MODEL), _rows(tm, D_MODEL), _whole((1, D_MODEL)), _rows(tm, Q_RANK, Z_CQ // Q_RANK),
                  _rows(tm, KV_RANK, Z_CKV // KV_RANK), _rows(tm, QK_WIDTH), _rows(tm, QK_WIDTH), _rows(tm, D_MODEL),
                  _rows(tm, D_MODEL), _rows(tm, D_MODEL), _rows(tm, POOL_WIDTH), nxt,
                  _whole((D_MODEL, DZ)), _whole((1, Q_RANK)), _whole((1, KV_RANK)), _whole((Q_RANK, QK_WIDTH)),
                  _whole((KV_RANK, QK_WIDTH)), _whole((KV_RANK, D_MODEL)), _rows(tm, 4 * LANES)],
        out_specs=[_rows(tm, D_MODEL), _rows(tm, D_MODEL), _rows(tm, DZ), _rows(tm, Q_RANK), _rows(tm, KV_RANK),
                   _rows(tm, QK_WIDTH), _rows(tm, QK_WIDTH), _rows(tm, D_MODEL),
                   _acc((1, D_MODEL)), _acc((1, Q_RANK)), _acc((1, KV_RANK))],
        out_shape=[jax.ShapeDtypeStruct((t, D_MODEL), F32), jax.ShapeDtypeStruct((t, D_MODEL), BF16),
                   jax.ShapeDtypeStruct((t, DZ), BF16), jax.ShapeDtypeStruct((t, Q_RANK), BF16),
                   jax.ShapeDtypeStruct((t, KV_RANK), BF16), jax.ShapeDtypeStruct((t, QK_WIDTH), BF16),
                   jax.ShapeDtypeStruct((t, QK_WIDTH), BF16), jax.ShapeDtypeStruct((t, D_MODEL), BF16),
                   jax.ShapeDtypeStruct((1, D_MODEL), F32), jax.ShapeDtypeStruct((1, Q_RANK), F32),
                   jax.ShapeDtypeStruct((1, KV_RANK), F32)],
        compiler_params=_cparams(),
    )(dh1, h, g_mix, z, z, dq, dk, dv, dga, dgb, dpool, dpool, win, gq, gkv, wuq, wuk, wuv, rope)


_MESH = pl.DeviceIdType.MESH


def _place():
    x, y, c = lax.axis_index("x"), lax.axis_index("y"), lax.axis_index("c")
    return x, y, c, 4 * x + 2 * y + c


def _peer(x, y, c, k):
    px, py, pc = (1 - x) if k & 4 else x, (1 - y) if k & 2 else y, (1 - c) if k & 1 else c
    return (px, py, pc), 4 * px + 2 * py + pc


ALL_PEERS = tuple(range(1, N_DEV))
CHIP_PEERS = (2, 4, 6)
N_CHIPS = N_DEV // 2


def _sem_scratch(n, m):
    return [pltpu.SemaphoreType.DMA((n, m)), pltpu.SemaphoreType.DMA((n, m)), pltpu.SemaphoreType.DMA((n,))]


class Exchange:
    def __init__(self, arrays, out_shapes, sem_cols, plan, aliased=False):
        self.arrays, self.out_shapes, self.plan = list(arrays), list(out_shapes), plan
        self.scratch = _sem_scratch(len(self.arrays), sem_cols)
        self.aliased = aliased

    def split(self, refs):
        n = len(self.arrays)
        return refs[:n], refs[n:2 * n], refs[2 * n:]

    def start(self, srcs, dsts, sems):
        local, sends, _ = self.plan(srcs, dsts, *sems)
        for cp in local + sends:
            cp.start()

    def wait(self, srcs, dsts, sems):
        local, sends, recvs = self.plan(srcs, dsts, *sems)
        for cp in recvs:
            cp.wait_recv()
        for cp in sends:
            cp.wait_send()
        for cp in local:
            cp.wait()

    def aliases(self, first_in, first_out):
        return {first_in + j: first_out + j for j in range(len(self.arrays))} if self.aliased else {}

    def run(self, name):
        def body(*refs):
            srcs, dsts, sems = self.split(refs)
            self.start(srcs, dsts, sems)
            self.wait(srcs, dsts, sems)

        n = len(self.arrays)
        return pl.pallas_call(body, name=name, in_specs=[_ANY] * n, out_specs=[_ANY] * n, out_shape=self.out_shapes,
                              scratch_shapes=self.scratch, input_output_aliases=self.aliases(0, 0))(*self.arrays)


def exchange(arrays, scatter, peers, by_chip=False):
    slots = N_CHIPS if by_chip else N_DEV

    def plan(srcs, dsts, send_sems, recv_sems, local_sems):
        x, y, c, me = _place()
        mine = 2 * x + y if by_chip else me
        local = [pltpu.make_async_copy(src.at[mine] if scatter else src, dst.at[mine], local_sems.at[j])
                 for j, (src, dst) in enumerate(zip(srcs, dsts))]
        sends, recvs = [], []
        for t, k in enumerate(peers):
            peer, pidx = _peer(x, y, c, k)
            theirs = 2 * peer[0] + peer[1] if by_chip else pidx
            for j, (src, dst) in enumerate(zip(srcs, dsts)):
                part = src.at[theirs] if scatter else src
                sems = dict(send_sem=send_sems.at[j, t], recv_sem=recv_sems.at[j, t], device_id=peer, device_id_type=_MESH)
                sends.append(pltpu.make_async_remote_copy(src_ref=part, dst_ref=dst.at[mine], **sems))
                recvs.append(pltpu.make_async_remote_copy(src_ref=part, dst_ref=dst.at[theirs], **sems))
        return local, sends, recvs

    shapes = [jax.ShapeDtypeStruct(a.shape if scatter else (slots,) + a.shape, a.dtype) for a in arrays]
    return Exchange(arrays, shapes, len(peers), plan)


def second_hop(gathered):
    def plan(srcs, dsts, send_sems, recv_sems, local_sems):
        x, y, c, me = _place()
        sibling, _ = _peer(x, y, c, 1)
        sends, recvs = [], []
        for t, k in enumerate(CHIP_PEERS):
            _, landed = _peer(x, y, c, k)
            _, coming = _peer(x, y, c, k ^ 1)
            for j, buf in enumerate(dsts):
                sems = dict(send_sem=send_sems.at[j, t], recv_sem=recv_sems.at[j, t], device_id=sibling, device_id_type=_MESH)
                sends.append(pltpu.make_async_remote_copy(src_ref=buf.at[landed], dst_ref=buf.at[landed], **sems))
                recvs.append(pltpu.make_async_remote_copy(src_ref=buf.at[coming], dst_ref=buf.at[coming], **sems))
        return [], sends, recvs

    shapes = [jax.ShapeDtypeStruct(a.shape, a.dtype) for a in gathered]
    return Exchange(gathered, shapes, len(CHIP_PEERS), plan, aliased=True)


FIRST_HOP_PEERS = (1,) + CHIP_PEERS


def _gather_two_level(arrays, name):
    n = len(arrays)

    def body(*refs):
        srcs, dsts, (send_sems, recv_sems, local_sems) = refs[:n], refs[n:2 * n], refs[2 * n:]
        x, y, c, me = _place()
        sibling, sidx = _peer(x, y, c, 1)

        def copy(j, sem, block, to, src=None):
            rows = dsts[j].at[block]
            return pltpu.make_async_remote_copy(src_ref=rows if src is None else src, dst_ref=rows, send_sem=send_sems.at[j, sem],
                                                recv_sem=recv_sems.at[j, sem], device_id=to, device_id_type=_MESH)

        local = [pltpu.make_async_copy(srcs[j], dsts[j].at[me], local_sems.at[j]) for j in range(n)]
        for cp in local:
            cp.start()
        first = [copy(j, 1 + t, me, _peer(x, y, c, k)[0], src=srcs[j]) for t, k in enumerate(CHIP_PEERS) for j in range(n)]
        first += [copy(j, 0, me, sibling, src=srcs[j]) for j in range(n)]
        for cp in first:
            cp.start()
        passed = []
        for t, k in enumerate(CHIP_PEERS):
            peer, pidx = _peer(x, y, c, k)
            for j in range(n):
                copy(j, 1 + t, pidx, peer).wait_recv()
                passed.append(copy(j, 4 + t, pidx, sibling))
                passed[-1].start()
        for j in range(n):
            copy(j, 0, sidx, sibling).wait_recv()
        for t, k in enumerate(CHIP_PEERS):
            _, pidx = _peer(x, y, c, k ^ 1)
            for j in range(n):
                copy(j, 4 + t, pidx, sibling).wait_recv()
        for cp in first + passed:
            cp.wait_send()
        for cp in local:
            cp.wait()

    shapes = [jax.ShapeDtypeStruct((N_DEV,) + a.shape, a.dtype) for a in arrays]
    return pl.pallas_call(body, name=name, in_specs=[_ANY] * n, out_specs=[_ANY] * n, out_shape=shapes,
                          scratch_shapes=_sem_scratch(n, 1 + 2 * len(CHIP_PEERS)))(*arrays)


def to_sibling(arrays):
    def plan(srcs, dsts, send_sems, recv_sems, local_sems):
        x, y, c, _ = _place()
        sibling, _ = _peer(x, y, c, 1)
        copies = [pltpu.make_async_remote_copy(src_ref=src.at[1 - c], dst_ref=dst, send_sem=send_sems.at[j, 0],
                                               recv_sem=recv_sems.at[j, 0], device_id=sibling, device_id_type=_MESH)
                  for j, (src, dst) in enumerate(zip(srcs, dsts))]
        return [], copies, copies

    return Exchange(arrays, [jax.ShapeDtypeStruct(a.shape[1:], a.dtype) for a in arrays], 1, plan)


def combine(a, b):
    assert not (a.aliased or b.aliased)
    na, nsem = len(a.arrays), len(a.scratch)

    def plan(srcs, dsts, *sems):
        return tuple(u + v for u, v in zip(a.plan(srcs[:na], dsts[:na], *sems[:nsem]), b.plan(srcs[na:], dsts[na:], *sems[nsem:])))

    both = Exchange(a.arrays + b.arrays, a.out_shapes + b.out_shapes, 1, plan)
    both.scratch = a.scratch + b.scratch
    return both


def pair_add(own, theirs, core, *, name):
    _, ns, r, c = own.shape
    rb = _row_block(r, c // 2)

    def body(core_ref, a_ref, b_ref, o_ref):
        o_ref[...] = (a_ref[...].astype(F32) + b_ref[...].astype(F32)).astype(o_ref.dtype)

    return pl.pallas_call(
        body, name=name,
        grid_spec=pltpu.PrefetchScalarGridSpec(
            num_scalar_prefetch=1, grid=(ns, r // rb),
            in_specs=[pl.BlockSpec((None, None, rb, c), lambda i, j, core_ref: (core_ref[0], i, j, 0)),
                      pl.BlockSpec((None, rb, c), lambda i, j, core_ref: (i, j, 0))],
            out_specs=pl.BlockSpec((None, rb, c), lambda i, j, core_ref: (i, j, 0))),
        out_shape=jax.ShapeDtypeStruct((ns, r, c), own.dtype), compiler_params=_cparams(),
    )(core, own, theirs)


ADAMW_BLOCK_BYTES = 1 << 20


def _row_block(r, c):
    for rb in range(r, 0, -1):
        if r % rb == 0 and (rb % 16 == 0 or rb == r) and rb * c * 4 <= ADAMW_BLOCK_BYTES:
            return rb
    return r


def adamw(w, m, v, parts, *, name):
    depth, r, c = w.shape
    n_parts = parts[0].shape[0]
    rb = _row_block(r, c)

    def body(w_ref, m_ref, v_ref, *refs):
        p_refs, (g_ref, d_ref, nm_ref, nv_ref) = refs[:depth], refs[depth:]

        def total(p_ref):
            g = p_ref[0].astype(F32)
            for j in range(1, n_parts):
                g = g + p_ref[j].astype(F32)
            return g

        g = total(p_refs[0])
        for l in range(1, depth):
            g = jnp.where(pl.program_id(0) == l, total(p_refs[l]), g)
        g_ref[...] = g
        m_new = ADAM_B1 * m_ref[...] + (1.0 - ADAM_B1) * g
        v_new = ADAM_B2 * v_ref[...] + (1.0 - ADAM_B2) * (g * g)
        m_hat = m_new / (1.0 - ADAM_B1 ** ADAM_STEP)
        v_hat = v_new / (1.0 - ADAM_B2 ** ADAM_STEP)
        d_ref[...] = -ADAM_LR * (m_hat / (jnp.sqrt(v_hat) + ADAM_EPS) + ADAM_WD * w_ref[...])
        nm_ref[...] = m_new
        nv_ref[...] = v_new

    wblk = pl.BlockSpec((None, rb, c), lambda l, i: (l, i, 0))
    pblk = pl.BlockSpec((n_parts, rb, c), lambda l, i: (0, i, 0))
    return pl.pallas_call(
        body, name=name, grid=(depth, r // rb),
        in_specs=[wblk, wblk, wblk] + [pblk] * depth, out_specs=[wblk] * 4,
        out_shape=[jax.ShapeDtypeStruct((depth, r, c), F32)] * 4, compiler_params=_cparams(),
    )(w, m, v, *parts)


BIG = (("w_in", 2), ("w_uq", 2), ("w_ukv", 2), ("w_pa", 2), ("w_pb", 1), ("w_o", 1), ("w_gate", 2), ("w_up", 2), ("w_down", 1))
SMALL = ("norm_mix_g", "pool_w", "pool_scale", "q_norm_g", "kv_norm_g", "norm_ffn_g", "final_norm_g")
SMALL_PER_LAYER = SMALL[:-1]
WEIGHTS = ("meta_tokens", "norm_mix_g", "w_in", "pool_w", "pool_scale", "q_norm_g", "kv_norm_g", "w_uq", "w_ukv", "w_pa", "w_pb",
           "w_o", "norm_ffn_g", "w_gate", "w_up", "w_down", "final_norm_g")
HEAD_QK = QK_NOPE + QK_ROPE
KR_END = Z_KR + QK_ROPE


def _cat_cols(parts):
    return [jnp.concatenate(parts, axis=1)]


def _cat_rows(parts):
    return [jnp.concatenate(parts, axis=0)]


def _arr_w_in(parts):
    full = jnp.concatenate(parts, axis=1)
    zc = lambda n: jnp.zeros((full.shape[0], n), full.dtype)
    return [jnp.concatenate([full[:, :Z_KR], zc(QK_NOPE), full[:, Z_KR:KR_END], zc(LANES - HEAD_QK), full[:, KR_END:]], axis=1)]


def _arr_w_uq(parts):
    full = jnp.concatenate(parts, axis=1)
    z = jnp.zeros((full.shape[0], HEAD_SLOT - HEAD_QK), full.dtype)
    pieces = []
    for hd in range(N_HEADS):
        pieces += [full[:, hd * HEAD_QK:(hd + 1) * HEAD_QK], z]
    return [jnp.concatenate(pieces, axis=1)]


def _arr_w_ukv(parts):
    full = jnp.concatenate(parts, axis=1)
    z = jnp.zeros((full.shape[0], HEAD_SLOT - QK_NOPE), full.dtype)
    wide = QK_NOPE + V_DIM
    k, v = [], []
    for hd in range(N_HEADS):
        k += [full[:, hd * wide:hd * wide + QK_NOPE], z]
        v.append(full[:, hd * wide + QK_NOPE:(hd + 1) * wide])
    return [jnp.concatenate(k, axis=1), jnp.concatenate(v, axis=1)]


def arrange(g, fn, out_shapes, name):
    def body(g_ref, *o_refs):
        for o_ref, val in zip(o_refs, fn([g_ref[p] for p in range(N_DEV)])):
            o_ref[...] = val

    return pl.pallas_call(
        body, name=name, grid=(1,),
        in_specs=[pl.BlockSpec(g.shape, lambda i: (0, 0, 0))],
        out_specs=[pl.BlockSpec(s, lambda i: (0, 0)) for s in out_shapes],
        out_shape=[jax.ShapeDtypeStruct(s, g.dtype) for s in out_shapes], compiler_params=_cparams(),
    )(g)


def _arranged_ranges(lo, hi):
    out = []
    for a, b, shift in ((0, Z_KR, 0), (Z_KR, KR_END, QK_NOPE), (KR_END, D_IN, LANES - QK_ROPE)):
        s, e = max(lo, a), min(hi, b)
        if s < e:
            out.append((s + shift, e + shift))
    return out


def _chunks_w_in(acc):
    cs = D_IN // N_DEV
    return [jnp.concatenate([acc[:, a:b] for a, b in _arranged_ranges(p * cs, (p + 1) * cs)], axis=1) for p in range(N_DEV)]


def _chunks_w_uq(acc):
    per = N_HEADS // N_DEV
    return [jnp.concatenate([acc[:, hd * HEAD_SLOT:hd * HEAD_SLOT + HEAD_QK] for hd in range(p * per, (p + 1) * per)], axis=1)
            for p in range(N_DEV)]


def _chunks_w_ukv(acc_k, acc_v):
    per = N_HEADS // N_DEV
    out = []
    for p in range(N_DEV):
        pieces = []
        for hd in range(p * per, (p + 1) * per):
            pieces += [acc_k[:, hd * HEAD_SLOT:hd * HEAD_SLOT + QK_NOPE], acc_v[:, hd * V_DIM:(hd + 1) * V_DIM]]
        out.append(jnp.concatenate(pieces, axis=1))
    return out


def _chunks_cols(acc):
    cs = acc.shape[1] // N_DEV
    return [acc[:, p * cs:(p + 1) * cs] for p in range(N_DEV)]


def _chunks_rows(acc):
    rs = acc.shape[0] // N_DEV
    return [acc[p * rs:(p + 1) * rs, :] for p in range(N_DEV)]


def _chunks_cols_transposed(acc):
    at = acc[...].T
    rs = at.shape[0] // N_DEV
    return [at[p * rs:(p + 1) * rs, :] for p in range(N_DEV)]


def _pack(parts, row_multiple):
    flat = jnp.concatenate([p.reshape(-1) for p in parts])
    return jnp.pad(flat, (0, -flat.shape[0] % (row_multiple * LANES))).reshape(-1, LANES)


def _unpack(packed, shapes):
    flat, out, off = packed.reshape(-1), [], 0
    for s in shapes:
        n = 1
        for d in s:
            n *= d
        out.append(flat[off:off + n].reshape(s))
        off += n
    return out


def _rope_table(lp, nb):
    inv = 1.0 / (ROPE_THETA ** (jnp.arange(0, QK_ROPE, 2, dtype=F32) / QK_ROPE))
    ang = jnp.arange(lp, dtype=F32)[:, None] * inv[None, :]
    cos, sin = jnp.cos(ang), jnp.sin(ang)
    z = lambda n: jnp.zeros((lp, n), F32)
    tail = LANES - QK_NOPE - QK_ROPE
    c = jnp.concatenate([jnp.ones((lp, QK_NOPE), F32), cos, cos, z(tail)], axis=1)
    cr = jnp.concatenate([z(QK_NOPE), cos, cos, z(tail)], axis=1)
    s1 = jnp.concatenate([z(QK_NOPE), -sin, z(HALF_ROPE), z(tail)], axis=1)
    s2 = jnp.concatenate([z(QK_NOPE), z(HALF_ROPE), sin, z(tail)], axis=1)
    return jnp.tile(jnp.concatenate([c, cr, s1, s2], axis=1), (nb, 1))


MIX_IN, MIX_OUT = ("w_in", "w_uq", "w_ukv"), ("w_pa", "w_pb", "w_o")
MIX = MIX_IN + MIX_OUT
FFN = ("w_gate", "w_up", "w_down")
TRANSPOSED = ("w_gate", "w_up")
ARRANGERS = {
    "w_in": (_arr_w_in, (("win", (D_MODEL, DZ)),)), "w_uq": (_arr_w_uq, (("wuq", (Q_RANK, QK_WIDTH)),)),
    "w_ukv": (_arr_w_ukv, (("wuk", (KV_RANK, QK_WIDTH)), ("wuv", (KV_RANK, D_MODEL)))),
    "w_pa": (_cat_cols, (("wpa", (POOL_WIDTH, D_MODEL)),)), "w_pb": (_cat_rows, (("wpb", (D_MODEL, D_MODEL)),)),
    "w_o": (_cat_rows, (("wo", (D_MODEL, D_MODEL)),)), "w_gate": (_cat_rows, (("wgt", (D_FF, D_MODEL)),)),
    "w_up": (_cat_rows, (("wut", (D_FF, D_MODEL)),)), "w_down": (_cat_rows, (("wd", (D_FF, D_MODEL)),)),
}


def _operands(gathered, names, l):
    p = {}
    for n in names:
        fn, outs = ARRANGERS[n]
        if fn is _cat_rows:
            p[outs[0][0]] = gathered[n].reshape(outs[0][1])
            continue
        for (key, _), a in zip(outs, arrange(gathered[n], fn, [s for _, s in outs], f"arrange_{n}_{l}")):
            p[key] = a
    return p


def _small_operands(small, l):
    pw = small["pool_w"][l].astype(BF16)
    return dict(g_mix=small["norm_mix_g"][l][None], gq=small["q_norm_g"][l][None], gkv=small["kv_norm_g"][l][None],
                g_ffn=small["norm_ffn_g"][l][None], ps=small["pool_scale"][l][None], pw=pw)


class MeshComm:
    def __init__(self, w, meta_tokens):
        self.src = lambda n, l: w[n][l].astype(BF16)
        self.meta_tokens = meta_tokens
        self.core = lax.axis_index("c").astype(jnp.int32).reshape(1)
        self.rides = {0: [(n, 0) for n in FFN] + [(n, 1) for n in MIX], 1: [(n, 1) for n in FFN]}

    def first_weights(self):
        got = _gather_two_level([self.src(n, 0) for n in MIX_IN] + [self.meta_tokens], "gather_mix_0")
        return dict(zip(MIX_IN, got)), jnp.moveaxis(got[-1], 0, 1).reshape(N_META, D_MODEL)

    def early_first_hop(self):
        return exchange([self.src(n, 0) for n in MIX_OUT], False, FIRST_HOP_PEERS)

    def early_weights(self, landed):
        return dict(zip(MIX_OUT, second_hop(landed).run("second_hop_mix_0")))

    def first_hop(self, l):
        return exchange([self.src(n, layer) for n, layer in self.rides[l]], False, FIRST_HOP_PEERS)

    def second_hop(self, l, landed):
        return second_hop(landed)

    def carried(self, l, full, names, layer):
        return {n: full[self.rides[l].index((n, layer))] for n in names}

    def pair_exchange(self, own):
        return to_sibling(own)

    def pair_add(self, own, theirs, names, tag):
        return [pair_add(a, b, self.core, name=f"pair_add_{n}_{tag}") for n, a, b in zip(names, own, theirs)]

    def last_pair_exchange(self, own, small):
        got = combine(to_sibling(own), exchange([small], False, ALL_PEERS)).run("pair_grads_mix_0")
        return got[:-1], got[-1]

    def scatter(self, sums):
        return exchange(sums, True, CHIP_PEERS, by_chip=True)

    def scatter_and_gather(self, sums, small):
        return combine(self.scatter(sums), exchange([small], False, ALL_PEERS))

    def scattered_and_gathered(self, brought, small):
        return brought[:-1], brought[-1]

    def scatter_now(self, sums, name):
        return self.scatter(sums).run(name)


HEADS_FWD, HEADS_BWD = 8, 4
TILE_ROWS, TILE_ROWS_BWD = 512, 256


def _tile(t, target):
    n = max(1, -(-t // (target + target // 8)))
    while t % n or (t // n) % 16:
        n += 1
    return t // n


def _wgrad_tile(t):
    return max(tm for tm in (2 * TQ, TQ, LANES) if t % tm == 0)


def _ffn_bwd_part(dh2, p, s, tag, ride):
    d, ff = D_MODEL, D_FF // N_DEV
    t = dh2.shape[0]
    wg_ = lambda n, x, ys, fn, shape: wgrad(x, ys, fn, shape, tm=_wgrad_tile(t), name=f"wgrad_{n}_{tag}")[0]
    (dh1, hn2, act, dgt, dup, dg_ffn), brought = ffn_bwd(dh2, s["h1"], p["g_ffn"], s["gt"], s["up"], p["wgt"], p["wut"], p["wd"],
                                                         tm=_tile(t, TILE_ROWS_BWD), name=f"ffn_bwd_{tag}", ride=ride)
    chunks = [wg_("gate", hn2, [dgt], _chunks_cols_transposed, (ff, d)), wg_("up", hn2, [dup], _chunks_cols_transposed, (ff, d)),
              wg_("down", act, [dh2], _chunks_rows, (ff, d))]
    return dh1, chunks, dict(norm_ffn_g=dg_ffn[0]), brought


EARLY, LATE = ("w_o", "w_pa"), ("w_in", "w_uq", "w_ukv", "w_pb")


def _mix_bwd_part(dh1, p, s, rope, nb, lp, tag, comm, ride, next_ride):
    d = D_MODEL
    t = dh1.shape[0]
    wg_ = lambda n, x, ys, fn, shape: wgrad(x, ys, fn, shape, tm=_wgrad_tile(t), name=f"wgrad_{n}_{tag}")[0]
    (dga, dgb, dpa, dpb, do, dpool, dps, dpw), first = merge_bwd(dh1, s["z"], s["pa"], s["pb"], s["pooled"], p["pw"], p["ps"],
                                                                   p["wpa"], p["wpb"], p["wo"], tm=_tile(t, TILE_ROWS),
                                                                   name=f"merge_bwd_{tag}", ride=ride)
    c_o = wg_("o", s["mg"], [dh1], _chunks_rows, (d // N_DEV, d))
    c_pa = wg_("pa", s["a"], [dpa], _chunks_cols, (POOL_WIDTH, d // N_DEV))
    c_pb, theirs = wgrad(s["o"], [dpb], _chunks_rows, (d // N_DEV, d), tm=_wgrad_tile(t), name=f"wgrad_pb_{tag}",
                         ride=comm.pair_exchange([c_o, c_pa]))
    early = comm.pair_add([c_o, c_pa], theirs, EARLY, f"early_{tag}")
    (dq, dk, dv), brought = attn_bwd(s["q"], s["k"], s["v"], s["o"], do, s["lse"], nb=nb, lp=lp, hb=HEADS_BWD,
                                     name=f"attn_bwd_{tag}", ride=next_ride(first, early))
    dh, hn, dz, cqn, ckvn, dqb, dkb, dvb, dg_mix, dgq, dgkv = in_proj_bwd(
        dh1, s["h"], p["g_mix"], s["z"], dq, dk, dv, dga, dgb, dpool, p["win"], p["gq"], p["gkv"], p["wuq"], p["wuk"], p["wuv"],
        rope, tm=_tile(t, TILE_ROWS_BWD), lp=lp, nb=nb, name=f"in_proj_bwd_{tag}")
    c_in = wg_("in", hn, [dz], _chunks_w_in, (d, D_IN // N_DEV))
    c_uq = wg_("uq", cqn, [dqb], _chunks_w_uq, (Q_RANK, N_HEADS * HEAD_QK // N_DEV))
    c_ukv = wg_("ukv", ckvn, [dkb, dvb], _chunks_w_ukv, (KV_RANK, N_HEADS * (QK_NOPE + V_DIM) // N_DEV))
    small = dict(pool_scale=dps[0], pool_w=dpw, norm_mix_g=dg_mix[0], q_norm_g=dgq[0], kv_norm_g=dgkv[0])
    return dh, [c_in, c_uq, c_ukv, c_pb], small, brought


def train_step(x, loss_target, small, comm):
    nb, seq, d = x.shape
    lp = -(-(N_META + seq) // LANES) * LANES
    t = nb * lp
    assert nb <= 2 and DEPTH == 2
    tm = _tile(t, TILE_ROWS)
    rope = _rope_table(lp, nb)
    gathered, meta = comm.first_weights()
    pad = jnp.zeros((nb, lp - N_META - seq, d), F32)
    h = jnp.concatenate([jnp.broadcast_to(meta[None], (nb, N_META, d)), x, pad], axis=1).reshape(t, d)
    target = jnp.concatenate([jnp.zeros((nb, N_META, d), F32), loss_target, pad], axis=1).reshape(t, d)

    params, saved, full = [], [], {}
    for l in range(DEPTH):
        p = _small_operands(small, l)
        p.update(_operands(gathered, MIX_IN, 0) if l == 0 else _operands(comm.carried(0, full[0], MIX, 1), MIX, 1))
        (z, q, k, v), early = in_proj_fwd(h, p["g_mix"], p["win"], p["gq"], p["gkv"], p["wuq"], p["wuk"], p["wuv"], rope, tm=tm,
                                          name=f"in_proj_fwd_{l}", ride=comm.early_first_hop() if l == 0 else None)
        (o, lse), landed = attn_fwd(q, k, v, nb=nb, lp=lp, hb=HEADS_FWD, name=f"attn_fwd_{l}", ride=comm.first_hop(l))
        if l == 0:
            p.update(_operands(comm.early_weights(early), MIX_OUT, 0))
        (h1, pooled, a, pa, pb, mg), full[l] = merge_fwd(h, z, o, p["pw"], p["ps"], p["wpa"], p["wpb"], p["wo"], tm=tm, lp=lp,
                                                          nb=nb, name=f"merge_fwd_{l}", ride=comm.second_hop(l, landed))
        p.update(_operands(comm.carried(l, full[l], FFN, l), FFN, l))
        h2, gt, up = ffn_fwd(h1, p["g_ffn"], p["wgt"], p["wut"], p["wd"], tm=tm, name=f"ffn_fwd_{l}")
        params.append(p)
        saved.append(dict(h=h, z=z, q=q, k=k, v=v, o=o, lse=lse, h1=h1, pooled=pooled, a=a, pa=pa, pb=pb, mg=mg, gt=gt, up=up))
        h = h2
    parts, dh, dgf = loss_head(h, small["final_norm_g"][None], target, tm=tm, lp=lp, nb=nb, seq=seq, name="loss_head")
    loss = jnp.sum(parts[::8, 0])

    sums = {}
    dh, c_ffn1, small1, _ = _ffn_bwd_part(dh, params[1], saved[1], 1, None)
    dh, c_late1, sm, brought = _mix_bwd_part(
        dh, params[1], saved[1], rope, nb, lp, 1, comm, comm.pair_exchange(c_ffn1),
        lambda theirs, early: comm.scatter(comm.pair_add(c_ffn1, theirs, FFN, "ffn_1") + early))
    small1.update(sm)
    sums.update({(n, 1): a for n, a in zip(FFN + EARLY, brought)})
    dh, c_ffn0, small0, theirs = _ffn_bwd_part(dh, params[0], saved[0], 0, comm.pair_exchange(c_late1))
    s_late1 = comm.pair_add(c_late1, theirs, LATE, "late_1")
    upper = _pack([small1[n] for n in SMALL_PER_LAYER] + [dgf[0]], 8)
    dh, c_late0, sm, brought = _mix_bwd_part(
        dh, params[0], saved[0], rope, nb, lp, 0, comm, comm.pair_exchange(c_ffn0),
        lambda theirs, early: comm.scatter_and_gather(s_late1 + comm.pair_add(c_ffn0, theirs, FFN, "ffn_0") + early, upper))
    small0.update(sm)
    brought, upper_parts = comm.scattered_and_gathered(brought, upper)
    sums.update({(n, l): a for (n, l), a in zip([(n, 1) for n in LATE] + [(n, 0) for n in FFN + EARLY], brought)})
    dh = dh.reshape(nb, lp, d)
    dmeta = jnp.sum(dh[:, :N_META], axis=0)
    meta_chunks = jnp.transpose(dmeta.reshape(N_META, N_CHIPS, 2, d // N_DEV), (2, 1, 0, 3)).astype(BF16)
    small_grads = {n: jnp.stack([small0[n], small1[n]]) for n in small0}
    small_grads["final_norm_g"] = dgf[0]
    last_names = LATE + ("meta_tokens",)
    lower = _pack([small0[n] for n in SMALL_PER_LAYER], 8)
    theirs, lower_parts = comm.last_pair_exchange(c_late0 + [meta_chunks], lower)
    last = comm.scatter_now(comm.pair_add(c_late0 + [meta_chunks], theirs, last_names, "late_0"), "scatter_late_0")
    sums.update({(n, 0): a for n, a in zip(last_names, last)})
    return loss, dh[:, N_META:N_META + seq], sums, small_grads, (lower_parts, upper_parts)


def kernel(x, meta_tokens, norm_mix_g, w_in, pool_w, pool_scale, q_norm_g, kv_norm_g, w_uq, w_ukv, w_pa, w_pb, w_o, norm_ffn_g, w_gate, w_up, w_down, final_norm_g, loss_target, m_meta_tokens, m_norm_mix_g, m_w_in, m_pool_w, m_pool_scale, m_q_norm_g, m_kv_norm_g, m_w_uq, m_w_ukv, m_w_pa, m_w_pb, m_w_o, m_norm_ffn_g, m_w_gate, m_w_up, m_w_down, m_final_norm_g, v_meta_tokens, v_norm_mix_g, v_w_in, v_pool_w, v_pool_scale, v_q_norm_g, v_kv_norm_g, v_w_uq, v_w_ukv, v_w_pa, v_w_pb, v_w_o, v_norm_ffn_g, v_w_gate, v_w_up, v_w_down, v_final_norm_g):
    args = dict(locals())
    w = {n: args[n] for n in WEIGHTS}
    m = {n: args["m_" + n] for n in WEIGHTS}
    v = {n: args["v_" + n] for n in WEIGHTS}
    small = {n: w[n] for n in SMALL}
    as_handled = lambda a, n: jnp.swapaxes(a, 1, 2) if n in TRANSPOSED else a
    wh, mh, vh = ({n: as_handled(d[n], n) for n, _ in BIG} for d in (w, m, v))

    loss, grad_x, sums, _, small_recv = train_step(x, loss_target, small, MeshComm(wh, meta_tokens))
    loss = lax.psum(loss, ("x", "y", "c"))

    out = {n: [as_handled(a, n) for a in adamw(wh[n], mh[n], vh[n], [sums[(n, l)] for l in range(DEPTH)], name=f"adamw_{n}")]
           for n, _ in BIG}
    out["meta_tokens"] = [a[0] for a in adamw(meta_tokens[None], m["meta_tokens"][None], v["meta_tokens"][None],
                                              [sums[("meta_tokens", 0)]], name="adamw_meta_tokens")]
    lower_parts, upper_parts = small_recv
    pk_lower = lambda d: _pack([d[n][0] for n in SMALL_PER_LAYER], 8)[None]
    pk_upper = lambda d: _pack([d[n][1] for n in SMALL_PER_LAYER] + [d["final_norm_g"]], 8)[None]
    lower = adamw(pk_lower(w), pk_lower(m), pk_lower(v), [lower_parts], name="adamw_small_lower")
    upper = adamw(pk_upper(w), pk_upper(m), pk_upper(v), [upper_parts], name="adamw_small_upper")
    shapes = [w[n].shape[1:] for n in SMALL_PER_LAYER]
    for n in SMALL:
        out[n] = [None] * 4
    for kind in range(4):
        lo = _unpack(lower[kind][0], shapes)
        up = _unpack(upper[kind][0], shapes + [w["final_norm_g"].shape])
        for j, n in enumerate(SMALL_PER_LAYER):
            out[n][kind] = jnp.stack([lo[j], up[j]])
        out["final_norm_g"][kind] = up[-1]
    return (loss, grad_x, *[out[n][kind] for kind in range(4) for n in WEIGHTS])
```

```python
import functools
import math

import jax
import jax.numpy as jnp
from jax import lax
from jax.experimental import pallas as pl
from jax.experimental.pallas import tpu as pltpu

F32, BF16 = jnp.float32, jnp.bfloat16

D_MODEL = 1024
N_META = 16
N_HEADS = 16
QK_NOPE, QK_ROPE, V_DIM = 64, 32, 64
HALF_ROPE = QK_ROPE // 2
Q_RANK, KV_RANK = 256, 128
POOL_WINDOWS = (2, 4, 8, 16)
POOL_GROUP = 128
POOL_WIDTH = POOL_GROUP * len(POOL_WINDOWS)
POOL_HALO = 16
D_FF = 2816
D_IN = 2976
NORM_EPS = 1e-6
SM_SCALE = (QK_NOPE + QK_ROPE) ** -0.5
LOG2E = math.log2(math.e)
EXP2_SCALE = SM_SCALE * LOG2E
MASK_VALUE = -1e30
ROPE_THETA = 10000.0
DEPTH = 2
N_DEV = 8

ADAM_LR, ADAM_B1, ADAM_B2, ADAM_EPS, ADAM_WD, ADAM_STEP = 0.001, 0.9, 0.999, 1e-08, 0.01, 10

LANES = 128
HEAD_SLOT = LANES
QK_WIDTH = N_HEADS * HEAD_SLOT
Z_CQ, Z_CKV, Z_KR, Z_GA, Z_GB, DZ = 512, 768, 896, 1024, 2048, 3072
TQ = TK = 256
QUERY_BLOCKS_PER_STEP = 3
VMEM_LIMIT = 56 * 1024 * 1024


def _cparams():
    return pltpu.CompilerParams(vmem_limit_bytes=VMEM_LIMIT)


def _rows(tm, width, col=0):
    return pl.BlockSpec((tm, width), lambda i: (i, col))


def _whole(shape):
    zeros = (0,) * len(shape)
    return pl.BlockSpec(shape, lambda i: zeros, pipeline_mode=pl.Buffered(1))


def _acc(shape):
    zeros = (0,) * len(shape)
    return pl.BlockSpec(shape, lambda i: zeros)


def _dot(a, b):
    return jnp.dot(a, b, preferred_element_type=F32)


def _dot_tn(a, b):
    return lax.dot_general(a, b, (((0,), (0,)), ((), ())), preferred_element_type=F32)


def _dot_nt(a, b):
    return lax.dot_general(a, b, (((1,), (1,)), ((), ())), preferred_element_type=F32)


def _rms(x):
    r = lax.rsqrt(jnp.mean(x * x, axis=-1, keepdims=True) + NORM_EPS)
    return x * r, r


def _rms_bwd(dy, xhat, r, g):
    dg = jnp.sum(dy * xhat, axis=0, keepdims=True)
    dxh = dy * g
    dx = r * (dxh - xhat * jnp.mean(dxh * xhat, axis=-1, keepdims=True))
    return dx, dg


def _sigmoid(x):
    return 1.0 / (1.0 + jnp.exp(-x))


def _rope_fwd(q, c, s1, s2):
    w = q.shape[1]
    return q * c + pltpu.roll(q, w - HALF_ROPE, 1) * s1 + pltpu.roll(q, HALF_ROPE, 1) * s2


def _rope_bwd(dq, c, s1, s2):
    w = dq.shape[1]
    return dq * c + pltpu.roll(dq * s1, HALF_ROPE, 1) + pltpu.roll(dq * s2, w - HALF_ROPE, 1)


def _rope_tables(rope, reps):
    c, cr, s1, s2 = (rope[:, k * LANES:(k + 1) * LANES] for k in range(4))
    if reps > 1:
        return jnp.tile(c, (1, reps)), jnp.tile(s1, (1, reps)), jnp.tile(s2, (1, reps))
    return cr, s1, s2


def _seq_pos(gi, lp, nb):
    pos = gi
    for b in range(1, nb):
        pos = jnp.where(gi >= b * lp, gi - b * lp, pos)
    return pos


_ANY = pl.BlockSpec(memory_space=pl.ANY)


def _carrying_call(body, ride, operands, *, name, grid, in_specs, out_specs, out_shape, scratch_shapes=()):
    n_in, n_out = len(in_specs), len(out_specs)
    if ride is None:
        out = pl.pallas_call(body, name=name, grid=grid, in_specs=in_specs, out_specs=out_specs, out_shape=out_shape,
                             scratch_shapes=list(scratch_shapes), compiler_params=_cparams())(*operands)
        return out, []
    ne = len(ride.arrays)

    def carrying(*refs):
        ins, r_in, rest = refs[:n_in], refs[n_in:n_in + ne], refs[n_in + ne:]
        outs, r_out, rest = rest[:n_out], rest[n_out:n_out + ne], rest[n_out + ne:]
        scratch, sems = rest[:len(scratch_shapes)], rest[len(scratch_shapes):]
        ids = [pl.program_id(a) for a in range(len(grid))]
        first = functools.reduce(jnp.logical_and, [i == 0 for i in ids])
        last = functools.reduce(jnp.logical_and, [i == g - 1 for i, g in zip(ids, grid)])

        @pl.when(first)
        def _():
            ride.start(r_in, r_out, sems)

        body(*ins, *outs, *scratch)

        @pl.when(last)
        def _():
            ride.wait(r_in, r_out, sems)

    out = pl.pallas_call(
        carrying, name=name, grid=grid, in_specs=list(in_specs) + [_ANY] * ne, out_specs=list(out_specs) + [_ANY] * ne,
        out_shape=list(out_shape) + ride.out_shapes, scratch_shapes=list(scratch_shapes) + ride.scratch,
        input_output_aliases=ride.aliases(n_in, n_out), compiler_params=_cparams(),
    )(*operands, *ride.arrays)
    return out[:n_out], out[n_out:]


def in_proj_fwd(h, g_mix, win, gq, gkv, wuq, wuk, wuv, rope, *, tm, name, ride=None):
    t = h.shape[0]

    def body(h_ref, g_ref, win_ref, gq_ref, gkv_ref, wuq_ref, wuk_ref, wuv_ref, rope_ref, z_ref, q_ref, k_ref, v_ref):
        xhat, _ = _rms(h_ref[...])
        hn = (xhat * g_ref[...]).astype(BF16)
        z = _dot(hn, win_ref[...])
        z_ref[...] = z
        rope_t = rope_ref[...]
        xq, _ = _rms(z[:, Z_CQ:Z_CKV])
        cqn = (xq * gq_ref[...]).astype(BF16)
        q = _rope_fwd(_dot(cqn, wuq_ref[...]), *_rope_tables(rope_t, N_HEADS))
        q_ref[...] = q.astype(BF16)
        xkv, _ = _rms(z[:, Z_CKV:Z_KR])
        ckvn = (xkv * gkv_ref[...]).astype(BF16)
        kr = _rope_fwd(z[:, Z_KR:Z_GA], *_rope_tables(rope_t, 1))
        k_ref[...] = (_dot(ckvn, wuk_ref[...]) + jnp.tile(kr, (1, N_HEADS))).astype(BF16)
        v_ref[...] = _dot(ckvn, wuv_ref[...]).astype(BF16)

    return _carrying_call(
        body, ride, (h, g_mix, win, gq, gkv, wuq, wuk, wuv, rope), name=name, grid=(t // tm,),
        in_specs=[_rows(tm, D_MODEL), _whole((1, D_MODEL)), _whole((D_MODEL, DZ)), _whole((1, Q_RANK)), _whole((1, KV_RANK)),
                  _whole((Q_RANK, QK_WIDTH)), _whole((KV_RANK, QK_WIDTH)), _whole((KV_RANK, D_MODEL)), _rows(tm, 4 * LANES)],
        out_specs=[_rows(tm, DZ), _rows(tm, QK_WIDTH), _rows(tm, QK_WIDTH), _rows(tm, D_MODEL)],
        out_shape=[jax.ShapeDtypeStruct((t, DZ), F32), jax.ShapeDtypeStruct((t, QK_WIDTH), BF16),
                   jax.ShapeDtypeStruct((t, QK_WIDTH), BF16), jax.ShapeDtypeStruct((t, D_MODEL), BF16)])


def attn_fwd(q, k, v, *, nb, lp, hb, name, ride=None):
    t = q.shape[0]
    nq, tail = lp // TQ, lp % TQ
    assert tail % LANES == 0

    def body(q_ref, k_ref, v_ref, o_ref, lse_ref, vt):
        for pr in range(hb // 2):
            vt[pr] = v_ref[:, pr * LANES:(pr + 1) * LANES].T

        def q_block(qs, tq, whole_k):
            qh = [q_ref[pl.ds(qs, tq), hd * HEAD_SLOT:(hd + 1) * HEAD_SLOT] for hd in range(hb)]
            keep = lax.broadcasted_iota(jnp.int32, (tq, tq), 0) <= lax.broadcasted_iota(jnp.int32, (tq, tq), 1)

            def k_steps(blocks, c, masked):
                sts = [[_dot_nt(k_ref[pl.ds(ks, tk), hd * HEAD_SLOT:(hd + 1) * HEAD_SLOT], qh[hd]) for hd in range(hb)]
                       for ks, tk in blocks]
                for (ks, tk), st_b in zip(blocks, sts):
                    ps, stats = [], []
                    for hd in range(hb):
                        m, l, _ = c[hd]
                        st = jnp.where(keep, st_b[hd], MASK_VALUE) if masked else st_b[hd]
                        m_new = jnp.maximum(m, jnp.max(st, axis=0, keepdims=True))
                        p = jnp.exp2((st - m_new) * EXP2_SCALE)
                        alpha = jnp.exp2((m - m_new) * EXP2_SCALE)
                        ps.append(p.astype(BF16))
                        stats.append((m_new, alpha * l + jnp.sum(p, axis=0, keepdims=True), alpha))
                    pvs = [_dot(vt[hd // 2, :, pl.ds(ks, tk)], ps[hd]) for hd in range(hb)]
                    c = tuple((stats[hd][0], stats[hd][1], stats[hd][2] * c[hd][2] + pvs[hd]) for hd in range(hb))
                return c

            def two_blocks(i, c):
                ks = pl.multiple_of(2 * i * TK, TK)
                return k_steps([(ks, TK), (ks + TK, TK)], c, False)

            init = tuple((jnp.full((1, tq), MASK_VALUE, F32), jnp.zeros((1, tq), F32), jnp.zeros((LANES, tq), F32))
                         for _ in range(hb))
            pairs = lax.div(whole_k, 2)
            c = lax.fori_loop(0, pairs, two_blocks, init)
            c = lax.fori_loop(2 * pairs, whole_k, lambda kj, c: k_steps([(pl.multiple_of(kj * TK, TK), TK)], c, False), c)
            c = k_steps([(qs, tq)], c, True)
            sub = lax.broadcasted_iota(jnp.int32, (LANES, tq), 0)
            for pr in range(hb // 2):
                (m0, l0, a0), (m1, l1, a1) = c[2 * pr], c[2 * pr + 1]
                o_ref[pl.ds(qs, tq), pr * LANES:(pr + 1) * LANES] = jnp.where(sub < V_DIM, a0 / l0, a1 / l1).T.astype(BF16)
                lse_ref[2 * pr, :, pl.ds(qs, tq)] = m0 * SM_SCALE + jnp.log(l0)
                lse_ref[2 * pr + 1, :, pl.ds(qs, tq)] = m1 * SM_SCALE + jnp.log(l1)

        def whole_q_block(qi, carry):
            q_block(pl.multiple_of(qi * TQ, TQ), TQ, qi)
            return carry

        lax.fori_loop(0, nq, whole_q_block, 0)
        if tail:
            q_block(nq * TQ, tail, nq)

    blk = lambda w: pl.BlockSpec((lp, w), lambda b, g: (b, g))
    return _carrying_call(
        body, ride, (q, k, v), name=name, grid=(nb, N_HEADS // hb),
        in_specs=[blk(hb * HEAD_SLOT), blk(hb * HEAD_SLOT), blk(hb * V_DIM)],
        out_specs=[blk(hb * V_DIM), pl.BlockSpec((hb, 1, lp), lambda b, g: (g, 0, b))],
        out_shape=[jax.ShapeDtypeStruct((t, D_MODEL), BF16), jax.ShapeDtypeStruct((N_HEADS, 1, t), F32)],
        scratch_shapes=[pltpu.VMEM((hb // 2, LANES, lp), BF16)])


def _pool_band_fwd(i, tm, lp, nb):
    r = lax.broadcasted_iota(jnp.int32, (tm, POOL_HALO + tm), 0)
    e = lax.broadcasted_iota(jnp.int32, (tm, POOL_HALO + tm), 1)
    diff = r + POOL_HALO - e
    pos = _seq_pos(i * tm + lax.broadcasted_iota(jnp.int32, (tm, 1), 0), lp, nb)
    out = []
    for w in POOL_WINDOWS:
        cnt = jnp.minimum(pos + 1, w)
        band = jnp.where((diff >= 0) & (diff < cnt), 1.0, 0.0).astype(BF16)
        out.append((band, cnt.astype(F32)))
    return out


def merge_fwd(h, z, o, pw, ps, wpa, wpb, wo, *, tm, lp, nb, name, ride=None):
    t = h.shape[0]
    hb = tm // POOL_HALO

    def body(h_ref, u_ref, uprev_ref, ga_ref, gb_ref, o_ref, pw_ref, ps_ref, wpa_ref, wpb_ref, wo_ref,
             h1_ref, pooled_ref, a_ref, pa_ref, pb_ref, mg_ref):
        i = pl.program_id(0)
        u = u_ref[...]
        uext = jnp.concatenate([uprev_ref[...], u], axis=0).astype(BF16)
        pooled, ys = [], []
        for g, (band, cnt) in enumerate(_pool_band_fwd(i, tm, lp, nb)):
            gs = slice(g * POOL_GROUP, (g + 1) * POOL_GROUP)
            pg = (_dot(band, uext[:, gs]) / cnt - u[:, gs]).astype(BF16)
            pooled.append(pg)
            ys.append(_dot(pg, pw_ref[g]))
        pooled_ref[...] = jnp.concatenate(pooled, axis=1)
        a = (jnp.concatenate(ys, axis=1) * ps_ref[...]).astype(BF16)
        a_ref[...] = a
        pa = _dot(a, wpa_ref[...])
        pb = _dot(o_ref[...], wpb_ref[...])
        pa_ref[...] = pa.astype(BF16)
        pb_ref[...] = pb.astype(BF16)
        mg = (_sigmoid(ga_ref[...]) * pa + _sigmoid(gb_ref[...]) * pb).astype(BF16)
        mg_ref[...] = mg
        h1_ref[...] = h_ref[...] + _dot(mg, wo_ref[...])

    halo = pl.BlockSpec((POOL_HALO, POOL_WIDTH), lambda i: (jnp.maximum(i * hb - 1, 0), 0))
    return _carrying_call(
        body, ride, (h, z, z, z, z, o, pw, ps, wpa, wpb, wo), name=name, grid=(t // tm,),
        in_specs=[_rows(tm, D_MODEL), _rows(tm, POOL_WIDTH), halo, _rows(tm, D_MODEL, 1), _rows(tm, D_MODEL, 2), _rows(tm, D_MODEL),
                  _whole((4, POOL_GROUP, POOL_GROUP)), _whole((1, POOL_WIDTH)), _whole((POOL_WIDTH, D_MODEL)),
                  _whole((D_MODEL, D_MODEL)), _whole((D_MODEL, D_MODEL))],
        out_specs=[_rows(tm, D_MODEL), _rows(tm, POOL_WIDTH), _rows(tm, POOL_WIDTH), _rows(tm, D_MODEL), _rows(tm, D_MODEL),
                   _rows(tm, D_MODEL)],
        out_shape=[jax.ShapeDtypeStruct((t, D_MODEL), F32), jax.ShapeDtypeStruct((t, POOL_WIDTH), BF16),
                   jax.ShapeDtypeStruct((t, POOL_WIDTH), BF16), jax.ShapeDtypeStruct((t, D_MODEL), BF16),
                   jax.ShapeDtypeStruct((t, D_MODEL), BF16), jax.ShapeDtypeStruct((t, D_MODEL), BF16)])


def ffn_fwd(h1, g, wgt, wut, wd, *, tm, name):
    t = h1.shape[0]

    def body(h_ref, g_ref, wgt_ref, wut_ref, wd_ref, h2_ref, gt_ref, up_ref):
        h = h_ref[...]
        xhat, _ = _rms(h)
        hn = (xhat * g_ref[...]).astype(BF16)
        gt = _dot_nt(hn, wgt_ref[...])
        up = _dot_nt(hn, wut_ref[...])
        gt_ref[...] = gt.astype(BF16)
        up_ref[...] = up.astype(BF16)
        act = (gt * _sigmoid(gt) * up).astype(BF16)
        h2_ref[...] = h + _dot(act, wd_ref[...])

    return pl.pallas_call(
        body, name=name, grid=(t // tm,),
        in_specs=[_rows(tm, D_MODEL), _whole((1, D_MODEL)), _whole((D_FF, D_MODEL)), _whole((D_FF, D_MODEL)), _whole((D_FF, D_MODEL))],
        out_specs=[_rows(tm, D_MODEL), _rows(tm, D_FF), _rows(tm, D_FF)],
        out_shape=[jax.ShapeDtypeStruct((t, D_MODEL), F32), jax.ShapeDtypeStruct((t, D_FF), BF16), jax.ShapeDtypeStruct((t, D_FF), BF16)],
        compiler_params=_cparams(),
    )(h1, g, wgt, wut, wd)


def loss_head(h, g, target, *, tm, lp, nb, seq, name):
    t = h.shape[0]
    nt = t // tm

    def body(h_ref, g_ref, t_ref, loss_ref, dh_ref, dg_ref):
        i = pl.program_id(0)
        pos = _seq_pos(i * tm + lax.broadcasted_iota(jnp.int32, (tm, 1), 0), lp, nb)
        real = (pos >= N_META) & (pos < N_META + seq)
        xhat, r = _rms(h_ref[...])
        gg = g_ref[...]
        err = jnp.where(real, xhat * gg - t_ref[...], 0.0)
        loss_ref[...] = jnp.full((8, LANES), 0.5 * jnp.sum(err * err) / D_MODEL, F32)
        dx, dg = _rms_bwd(err * (1.0 / D_MODEL), xhat, r, gg)
        dh_ref[...] = dx

        @pl.when(i == 0)
        def _():
            dg_ref[...] = jnp.zeros_like(dg_ref)

        dg_ref[...] += dg

    return pl.pallas_call(
        body, name=name, grid=(nt,),
        in_specs=[_rows(tm, D_MODEL), _whole((1, D_MODEL)), _rows(tm, D_MODEL)],
        out_specs=[pl.BlockSpec((8, LANES), lambda i: (i, 0)), _rows(tm, D_MODEL), _acc((1, D_MODEL))],
        out_shape=[jax.ShapeDtypeStruct((nt * 8, LANES), F32), jax.ShapeDtypeStruct((t, D_MODEL), F32),
                   jax.ShapeDtypeStruct((1, D_MODEL), F32)],
        compiler_params=_cparams(),
    )(h, g, target)


def wgrad(x, ys, chunk_fn, chunk_shape, *, tm, name, ride=None):
    t, m = x.shape
    tiles = t // tm
    steps = -(-tiles // 2)

    def body(*refs):
        ins, o_ref, accs = refs[:2 * (1 + len(ys))], refs[2 * (1 + len(ys))], refs[2 * (1 + len(ys)) + 1:]
        i = pl.program_id(0)

        @pl.when(i == 0)
        def _():
            for acc in accs:
                acc[...] = jnp.zeros_like(acc)

        def both(first, second, mask):
            b = second[...].astype(BF16)
            if mask and tiles % 2:
                b = jnp.where(2 * i + 1 < tiles, b, jnp.zeros_like(b))
            return jnp.concatenate([first[...].astype(BF16), b], axis=0)

        xb = both(ins[0], ins[1], True)
        for j, acc in enumerate(accs):
            acc[...] += _dot_tn(xb, both(ins[2 + 2 * j], ins[3 + 2 * j], False))

        @pl.when(i == steps - 1)
        def _():
            for p, chunk in enumerate(chunk_fn(*accs)):
                o_ref[p % 2, p // 2] = chunk.astype(BF16)

    def two_tiles(width):
        return [pl.BlockSpec((tm, width), lambda i: (2 * i, 0)),
                pl.BlockSpec((tm, width), lambda i: (jnp.minimum(2 * i + 1, tiles - 1), 0))]

    out = (2, N_DEV // 2) + tuple(chunk_shape)
    operands = [x, x] + [a for y in ys for a in (y, y)]
    (chunks,), brought = _carrying_call(
        body, ride, operands, name=name, grid=(steps,),
        in_specs=two_tiles(m) + [s for y in ys for s in two_tiles(y.shape[1])], out_specs=[_acc(out)],
        out_shape=[jax.ShapeDtypeStruct(out, BF16)], scratch_shapes=[pltpu.VMEM((m, y.shape[1]), F32) for y in ys])
    return chunks, brought


def ffn_bwd(dh2, h1, g, gt, up, wgt, wut, wd, *, tm, name, ride=None):
    t = h1.shape[0]

    def body(dh2_ref, h_ref, g_ref, gt_ref, up_ref, wgt_ref, wut_ref, wd_ref, dh1_ref, hn_ref, act_ref, dgt_ref, dup_ref, dg_ref):
        dh2 = dh2_ref[...]
        dact = _dot_nt(dh2.astype(BF16), wd_ref[...])
        gt = gt_ref[...].astype(F32)
        up = up_ref[...].astype(F32)
        sg = _sigmoid(gt)
        silu = gt * sg
        act_ref[...] = (silu * up).astype(BF16)
        dgt = (dact * up * (sg * (1.0 + gt * (1.0 - sg)))).astype(BF16)
        dup = (dact * silu).astype(BF16)
        dgt_ref[...] = dgt
        dup_ref[...] = dup
        dhn = _dot(dgt, wgt_ref[...]) + _dot(dup, wut_ref[...])
        xhat, r = _rms(h_ref[...])
        gg = g_ref[...]
        hn_ref[...] = (xhat * gg).astype(BF16)
        dx, dg = _rms_bwd(dhn, xhat, r, gg)
        dh1_ref[...] = dh2 + dx

        @pl.when(pl.program_id(0) == 0)
        def _():
            dg_ref[...] = jnp.zeros_like(dg_ref)

        dg_ref[...] += dg

    return _carrying_call(
        body, ride, (dh2, h1, g, gt, up, wgt, wut, wd), name=name, grid=(t // tm,),
        in_specs=[_rows(tm, D_MODEL), _rows(tm, D_MODEL), _whole((1, D_MODEL)), _rows(tm, D_FF), _rows(tm, D_FF),
                  _whole((D_FF, D_MODEL)), _whole((D_FF, D_MODEL)), _whole((D_FF, D_MODEL))],
        out_specs=[_rows(tm, D_MODEL), _rows(tm, D_MODEL), _rows(tm, D_FF), _rows(tm, D_FF), _rows(tm, D_FF), _acc((1, D_MODEL))],
        out_shape=[jax.ShapeDtypeStruct((t, D_MODEL), F32), jax.ShapeDtypeStruct((t, D_MODEL), BF16),
                   jax.ShapeDtypeStruct((t, D_FF), BF16), jax.ShapeDtypeStruct((t, D_FF), BF16),
                   jax.ShapeDtypeStruct((t, D_FF), BF16), jax.ShapeDtypeStruct((1, D_MODEL), F32)])


def merge_bwd(dh1, z, pa, pb, pooled, pw, ps, wpa, wpb, wo, *, tm, name, ride=None):
    t = dh1.shape[0]

    def body(dh1_ref, ga_ref, gb_ref, pa_ref, pb_ref, pooled_ref, pw_ref, ps_ref, wpa_ref, wpb_ref, wo_ref,
             dga_ref, dgb_ref, dpa_ref, dpb_ref, do_ref, dpool_ref, dps_ref, dpw_ref):
        dmg = _dot_nt(dh1_ref[...].astype(BF16), wo_ref[...])
        sa = _sigmoid(ga_ref[...])
        sb = _sigmoid(gb_ref[...])
        dga_ref[...] = (dmg * pa_ref[...].astype(F32) * sa * (1.0 - sa)).astype(BF16)
        dgb_ref[...] = (dmg * pb_ref[...].astype(F32) * sb * (1.0 - sb)).astype(BF16)
        dpa = (dmg * sa).astype(BF16)
        dpb = (dmg * sb).astype(BF16)
        dpa_ref[...] = dpa
        dpb_ref[...] = dpb
        do_ref[...] = _dot_nt(dpb, wpb_ref[...]).astype(BF16)
        da = _dot_nt(dpa, wpa_ref[...])
        pooled = pooled_ref[...]
        ps = ps_ref[...]

        @pl.when(pl.program_id(0) == 0)
        def _():
            dps_ref[...] = jnp.zeros_like(dps_ref)
            dpw_ref[...] = jnp.zeros_like(dpw_ref)

        dps, dpool = [], []
        for g in range(len(POOL_WINDOWS)):
            gs = slice(g * POOL_GROUP, (g + 1) * POOL_GROUP)
            y = _dot(pooled[:, gs], pw_ref[g])
            dps.append(jnp.sum(da[:, gs] * y, axis=0, keepdims=True))
            dy = (da[:, gs] * ps[:, gs]).astype(BF16)
            dpool.append(_dot_nt(dy, pw_ref[g]))
            dpw_ref[g] += _dot_tn(pooled[:, gs], dy)
        dps_ref[...] += jnp.concatenate(dps, axis=1)
        dpool_ref[...] = jnp.concatenate(dpool, axis=1)

    return _carrying_call(
        body, ride, (dh1, z, z, pa, pb, pooled, pw, ps, wpa, wpb, wo), name=name, grid=(t // tm,),
        in_specs=[_rows(tm, D_MODEL), _rows(tm, D_MODEL, 1), _rows(tm, D_MODEL, 2), _rows(tm, D_MODEL), _rows(tm, D_MODEL),
                  _rows(tm, POOL_WIDTH), _whole((4, POOL_GROUP, POOL_GROUP)),
                  _whole((1, POOL_WIDTH)), _whole((POOL_WIDTH, D_MODEL)), _whole((D_MODEL, D_MODEL)), _whole((D_MODEL, D_MODEL))],
        out_specs=[_rows(tm, D_MODEL), _rows(tm, D_MODEL), _rows(tm, D_MODEL), _rows(tm, D_MODEL), _rows(tm, D_MODEL),
                   _rows(tm, POOL_WIDTH), _acc((1, POOL_WIDTH)), _acc((4, POOL_GROUP, POOL_GROUP))],
        out_shape=[jax.ShapeDtypeStruct((t, D_MODEL), BF16)] * 5
        + [jax.ShapeDtypeStruct((t, POOL_WIDTH), F32), jax.ShapeDtypeStruct((1, POOL_WIDTH), F32),
           jax.ShapeDtypeStruct((4, POOL_GROUP, POOL_GROUP), F32)])


def attn_bwd(q, k, v, o, do, lse, *, nb, lp, hb, name, ride=None):
    t = q.shape[0]
    nq, tail = lp // TQ, lp % TQ
    assert tail % LANES == 0

    def body(q_ref, k_ref, v_ref, o_ref, do_ref, lse_ref, dq_ref, dk_ref, dv_ref, kt, doh, lse_row, delta_row, dqt):
        lane = lax.broadcasted_iota(jnp.int32, (lp, LANES), 1)
        first = lane < V_DIM
        sub = lax.broadcasted_iota(jnp.int32, (LANES, lp), 0)
        for pr in range(hb // 2):
            ls = slice(pr * LANES, (pr + 1) * LANES)
            do = do_ref[:, ls]
            doh[2 * pr] = jnp.where(first, do, jnp.zeros_like(do))
            doh[2 * pr + 1] = jnp.where(first, jnp.zeros_like(do), do)
            prod_t = (do.astype(F32) * o_ref[:, ls].astype(F32)).T
            delta_row[2 * pr] = jnp.sum(jnp.where(sub < V_DIM, prod_t, 0.0), axis=0, keepdims=True)
            delta_row[2 * pr + 1] = jnp.sum(jnp.where(sub < V_DIM, 0.0, prod_t), axis=0, keepdims=True)
        for hd in range(hb):
            lse_row[hd] = lse_ref[hd] * LOG2E
            kt[hd] = k_ref[:, hd * HEAD_SLOT:(hd + 1) * HEAD_SLOT].T
        dqt[...] = jnp.zeros(dqt.shape, F32)
        heads = range(hb)
        hss = [slice(hd * HEAD_SLOT, (hd + 1) * HEAD_SLOT) for hd in heads]

        def k_block(ks, tk, next_q):
            keep = lax.broadcasted_iota(jnp.int32, (tk, tk), 0) <= lax.broadcasted_iota(jnp.int32, (tk, tk), 1)

            def q_steps(blocks, c, masked):
                work = [(qs, tq, hd) for qs, tq in blocks for hd in heads]
                qhs = [q_ref[pl.ds(qs, tq), hss[hd]] for qs, tq, hd in work]
                dos = [doh[hd, pl.ds(qs, tq), :] for qs, tq, hd in work]
                sts = [_dot_nt(k_ref[pl.ds(ks, tk), hss[hd]], qhs[i]) for i, (_, _, hd) in enumerate(work)]
                dpts = [_dot_nt(v_ref[pl.ds(ks, tk), (hd // 2) * LANES:(hd // 2 + 1) * LANES], dos[i])
                        for i, (_, _, hd) in enumerate(work)]
                pts, dsts = [], []
                for i, (qs, tq, hd) in enumerate(work):
                    st = jnp.where(keep, sts[i], MASK_VALUE) if masked else sts[i]
                    pt = jnp.exp2(st * EXP2_SCALE - lse_row[hd, :, pl.ds(qs, tq)])
                    dsts.append((pt * (dpts[i] - delta_row[hd, :, pl.ds(qs, tq)])).astype(BF16))
                    pts.append(pt.astype(BF16))
                dvs = [_dot(pts[i], dos[i]) for i in range(len(work))]
                dks = [_dot(dsts[i], qhs[i]) for i in range(len(work))]
                dqs = [_dot(kt[hd, :, pl.ds(ks, tk)], dsts[i]) for i, (_, _, hd) in enumerate(work)]
                c = list(c)
                for i, (qs, tq, hd) in enumerate(work):
                    dqt[hd, :, pl.ds(qs, tq)] += dqs[i]
                    c[hd] = (c[hd][0] + dks[i], c[hd][1] + dvs[i])
                return tuple(c)

            zero = jnp.zeros((tk, LANES), F32)
            c = q_steps([(ks, tk)], tuple((zero, zero) for _ in heads), True)
            if next_q is not None:
                def some_blocks(i, c):
                    qs = pl.multiple_of((next_q + QUERY_BLOCKS_PER_STEP * i) * TQ, TQ)
                    return q_steps([(qs + j * TQ, TQ) for j in range(QUERY_BLOCKS_PER_STEP)], c, False)

                groups = lax.div(nq - next_q, QUERY_BLOCKS_PER_STEP)
                c = lax.fori_loop(0, groups, some_blocks, c)
                c = lax.fori_loop(next_q + QUERY_BLOCKS_PER_STEP * groups, nq,
                                  lambda qi, c: q_steps([(pl.multiple_of(qi * TQ, TQ), TQ)], c, False), c)
                if tail:
                    c = q_steps([(nq * TQ, tail)], c, False)
            for hd in heads:
                dk_ref[pl.ds(ks, tk), hss[hd]] = c[hd][0] * SM_SCALE
            for pr in range(hb // 2):
                dv_ref[pl.ds(ks, tk), pr * LANES:(pr + 1) * LANES] = c[2 * pr][1] + c[2 * pr + 1][1]

        def whole_k_block(kj, carry):
            k_block(pl.multiple_of(kj * TK, TK), TK, kj + 1)
            return carry

        lax.fori_loop(0, nq, whole_k_block, 0)
        if tail:
            k_block(nq * TQ, tail, None)
        for hd in range(hb):
            dq_ref[:, hd * HEAD_SLOT:(hd + 1) * HEAD_SLOT] = dqt[hd].T * SM_SCALE

    blk = lambda w: pl.BlockSpec((lp, w), lambda b, g: (b, g))
    return _carrying_call(
        body, ride, (q, k, v, o, do, lse), name=name, grid=(nb, N_HEADS // hb),
        in_specs=[blk(hb * HEAD_SLOT), blk(hb * HEAD_SLOT), blk(hb * V_DIM), blk(hb * V_DIM), blk(hb * V_DIM),
                  pl.BlockSpec((hb, 1, lp), lambda b, g: (g, 0, b))],
        out_specs=[blk(hb * HEAD_SLOT), blk(hb * HEAD_SLOT), blk(hb * V_DIM)],
        out_shape=[jax.ShapeDtypeStruct((t, QK_WIDTH), F32), jax.ShapeDtypeStruct((t, QK_WIDTH), F32),
                   jax.ShapeDtypeStruct((t, D_MODEL), F32)],
        scratch_shapes=[pltpu.VMEM((hb, HEAD_SLOT, lp), BF16), pltpu.VMEM((hb, lp, LANES), BF16), pltpu.VMEM((hb, 1, lp), F32),
                        pltpu.VMEM((hb, 1, lp), F32), pltpu.VMEM((hb, HEAD_SLOT, lp), F32)])


def in_proj_bwd(dh1, h, g_mix, z, dq, dk, dv, dga, dgb, dpool, win, gq, gkv, wuq, wuk, wuv, rope, *, tm, lp, nb, name):
    t = h.shape[0]
    hb = tm // POOL_HALO
    last_halo = t // POOL_HALO - 1

    def body(dh1_ref, h_ref, g_ref, zcq_ref, zckv_ref, dq_ref, dk_ref, dv_ref, dga_ref, dgb_ref, dpool_ref, dnext_ref,
             win_ref, gq_ref, gkv_ref, wuq_ref, wuk_ref, wuv_ref, rope_ref,
             dh_ref, hn_ref, dz_ref, cqn_ref, ckvn_ref, dqb_ref, dkb_ref, dvb_ref, dg_ref, dgq_ref, dgkv_ref):
        i = pl.program_id(0)
        rope_t = rope_ref[...]
        dqb = _rope_bwd(dq_ref[...], *_rope_tables(rope_t, N_HEADS)).astype(BF16)
        dqb_ref[...] = dqb
        xq, rq = _rms(zcq_ref[...])
        gq_v = gq_ref[...]
        cqn_ref[...] = (xq * gq_v).astype(BF16)
        dcq, dgq = _rms_bwd(_dot_nt(dqb, wuq_ref[...]), xq, rq, gq_v)
        dk = dk_ref[...]
        dkb = dk.astype(BF16)
        dvb = dv_ref[...].astype(BF16)
        dkb_ref[...] = dkb
        dvb_ref[...] = dvb
        xkv, rkv = _rms(zckv_ref[...])
        gkv_v = gkv_ref[...]
        ckvn_ref[...] = (xkv * gkv_v).astype(BF16)
        dckv, dgkv = _rms_bwd(_dot_nt(dkb, wuk_ref[...]) + _dot_nt(dvb, wuv_ref[...]), xkv, rkv, gkv_v)
        dks = dk[:, :HEAD_SLOT]
        for hd in range(1, N_HEADS):
            dks = dks + dk[:, hd * HEAD_SLOT:(hd + 1) * HEAD_SLOT]
        dzk = _rope_bwd(dks, *_rope_tables(rope_t, 1))
        dp_cur = dpool_ref[...]
        dp_ext = jnp.concatenate([dp_cur, dnext_ref[...]], axis=0)
        r = lax.broadcasted_iota(jnp.int32, (tm, tm + POOL_HALO), 0)
        e = lax.broadcasted_iota(jnp.int32, (tm, tm + POOL_HALO), 1)
        gt_col = i * tm + lax.broadcasted_iota(jnp.int32, (1, tm + POOL_HALO), 1)
        pos_col = _seq_pos(gt_col, lp, nb)
        gt_row = i * tm + lax.broadcasted_iota(jnp.int32, (tm + POOL_HALO, 1), 0)
        pos_row = _seq_pos(gt_row, lp, nb)
        dus = []
        for g, w in enumerate(POOL_WINDOWS):
            gs = slice(g * POOL_GROUP, (g + 1) * POOL_GROUP)
            band = jnp.where((e - r >= 0) & (e - r < jnp.minimum(pos_col + 1, w)) & (gt_col < t), 1.0, 0.0).astype(BF16)
            scaled = jnp.where(gt_row < t, dp_ext[:, gs] / jnp.minimum(pos_row + 1, w).astype(F32), 0.0).astype(BF16)
            dus.append(_dot(band, scaled) - dp_cur[:, gs])
        dz = jnp.concatenate(dus + [dcq, dckv, dzk], axis=1).astype(BF16)
        dz = jnp.concatenate([dz, dga_ref[...], dgb_ref[...]], axis=1)
        dz_ref[...] = dz
        xhat, rr = _rms(h_ref[...])
        gg = g_ref[...]
        hn_ref[...] = (xhat * gg).astype(BF16)
        dx, dg = _rms_bwd(_dot_nt(dz, win_ref[...]), xhat, rr, gg)
        dh_ref[...] = dh1_ref[...] + dx

        @pl.when(i == 0)
        def _():
            dg_ref[...] = jnp.zeros_like(dg_ref)
            dgq_ref[...] = jnp.zeros_like(dgq_ref)
            dgkv_ref[...] = jnp.zeros_like(dgkv_ref)

        dg_ref[...] += dg
        dgq_ref[...] += dgq
        dgkv_ref[...] += dgkv

    nxt = pl.BlockSpec((POOL_HALO, POOL_WIDTH), lambda i: (jnp.minimum((i + 1) * hb, last_halo), 0))
    return pl.pallas_call(
        body, name=name, grid=(t // tm,),
        in_specs=[_rows(tm, D_MODEL), _rows(tm, D_MODEL), _whole((1, D_MODEL)), _rows(tm, Q_RANK, Z_CQ // Q_RANK),
                  _rows(tm, KV_RANK, Z_CKV // KV_RANK), _rows(tm, QK_WIDTH), _rows(tm, QK_WIDTH), _rows(tm, D_MODEL),
                  _rows(tm, D_MODEL), _rows(tm, D_MODEL), _rows(tm, POOL_WIDTH), nxt,
                  _whole((D_MODEL, DZ)), _whole((1, Q_RANK)), _whole((1, KV_RANK)), _whole((Q_RANK, QK_WIDTH)),
                  _whole((KV_RANK, QK_WIDTH)), _whole((KV_RANK, D_MODEL)), _rows(tm, 4 * LANES)],
        out_specs=[_rows(tm, D_MODEL), _rows(tm, D_MODEL), _rows(tm, DZ), _rows(tm, Q_RANK), _rows(tm, KV_RANK),
                   _rows(tm, QK_WIDTH), _rows(tm, QK_WIDTH), _rows(tm, D_MODEL),
                   _acc((1, D_MODEL)), _acc((1, Q_RANK)), _acc((1, KV_RANK))],
        out_shape=[jax.ShapeDtypeStruct((t, D_MODEL), F32), jax.ShapeDtypeStruct((t, D_MODEL), BF16),
                   jax.ShapeDtypeStruct((t, DZ), BF16), jax.ShapeDtypeStruct((t, Q_RANK), BF16),
                   jax.ShapeDtypeStruct((t, KV_RANK), BF16), jax.ShapeDtypeStruct((t, QK_WIDTH), BF16),
                   jax.ShapeDtypeStruct((t, QK_WIDTH), BF16), jax.ShapeDtypeStruct((t, D_MODEL), BF16),
                   jax.ShapeDtypeStruct((1, D_MODEL), F32), jax.ShapeDtypeStruct((1, Q_RANK), F32),
                   jax.ShapeDtypeStruct((1, KV_RANK), F32)],
        compiler_params=_cparams(),
    )(dh1, h, g_mix, z, z, dq, dk, dv, dga, dgb, dpool, dpool, win, gq, gkv, wuq, wuk, wuv, rope)


_MESH = pl.DeviceIdType.MESH


def _place():
    x, y, c = lax.axis_index("x"), lax.axis_index("y"), lax.axis_index("c")
    return x, y, c, 4 * x + 2 * y + c


def _peer(x, y, c, k):
    px, py, pc = (1 - x) if k & 4 else x, (1 - y) if k & 2 else y, (1 - c) if k & 1 else c
    return (px, py, pc), 4 * px + 2 * py + pc


ALL_PEERS = tuple(range(1, N_DEV))
CHIP_PEERS = (2, 4, 6)
N_CHIPS = N_DEV // 2


def _sem_scratch(n, m):
    return [pltpu.SemaphoreType.DMA((n, m)), pltpu.SemaphoreType.DMA((n, m)), pltpu.SemaphoreType.DMA((n,))]


class Exchange:
    def __init__(self, arrays, out_shapes, sem_cols, plan, aliased=False):
        self.arrays, self.out_shapes, self.plan = list(arrays), list(out_shapes), plan
        self.scratch = _sem_scratch(len(self.arrays), sem_cols)
        self.aliased = aliased

    def split(self, refs):
        n = len(self.arrays)
        return refs[:n], refs[n:2 * n], refs[2 * n:]

    def start(self, srcs, dsts, sems):
        local, sends, _ = self.plan(srcs, dsts, *sems)
        for cp in local + sends:
            cp.start()

    def wait(self, srcs, dsts, sems):
        local, sends, recvs = self.plan(srcs, dsts, *sems)
        for cp in recvs:
            cp.wait_recv()
        for cp in sends:
            cp.wait_send()
        for cp in local:
            cp.wait()

    def aliases(self, first_in, first_out):
        return {first_in + j: first_out + j for j in range(len(self.arrays))} if self.aliased else {}

    def run(self, name):
        def body(*refs):
            srcs, dsts, sems = self.split(refs)
            self.start(srcs, dsts, sems)
            self.wait(srcs, dsts, sems)

        n = len(self.arrays)
        return pl.pallas_call(body, name=name, in_specs=[_ANY] * n, out_specs=[_ANY] * n, out_shape=self.out_shapes,
                              scratch_shapes=self.scratch, input_output_aliases=self.aliases(0, 0))(*self.arrays)


def exchange(arrays, scatter, peers, by_chip=False):
    slots = N_CHIPS if by_chip else N_DEV

    def plan(srcs, dsts, send_sems, recv_sems, local_sems):
        x, y, c, me = _place()
        mine = 2 * x + y if by_chip else me
        local = [pltpu.make_async_copy(src.at[mine] if scatter else src, dst.at[mine], local_sems.at[j])
                 for j, (src, dst) in enumerate(zip(srcs, dsts))]
        sends, recvs = [], []
        for t, k in enumerate(peers):
            peer, pidx = _peer(x, y, c, k)
            theirs = 2 * peer[0] + peer[1] if by_chip else pidx
            for j, (src, dst) in enumerate(zip(srcs, dsts)):
                part = src.at[theirs] if scatter else src
                sems = dict(send_sem=send_sems.at[j, t], recv_sem=recv_sems.at[j, t], device_id=peer, device_id_type=_MESH)
                sends.append(pltpu.make_async_remote_copy(src_ref=part, dst_ref=dst.at[mine], **sems))
                recvs.append(pltpu.make_async_remote_copy(src_ref=part, dst_ref=dst.at[theirs], **sems))
        return local, sends, recvs

    shapes = [jax.ShapeDtypeStruct(a.shape if scatter else (slots,) + a.shape, a.dtype) for a in arrays]
    return Exchange(arrays, shapes, len(peers), plan)


def second_hop(gathered):
    def plan(srcs, dsts, send_sems, recv_sems, local_sems):
        x, y, c, me = _place()
        sibling, _ = _peer(x, y, c, 1)
        sends, recvs = [], []
        for t, k in enumerate(CHIP_PEERS):
            _, landed = _peer(x, y, c, k)
            _, coming = _peer(x, y, c, k ^ 1)
            for j, buf in enumerate(dsts):
                sems = dict(send_sem=send_sems.at[j, t], recv_sem=recv_sems.at[j, t], device_id=sibling, device_id_type=_MESH)
                sends.append(pltpu.make_async_remote_copy(src_ref=buf.at[landed], dst_ref=buf.at[landed], **sems))
                recvs.append(pltpu.make_async_remote_copy(src_ref=buf.at[coming], dst_ref=buf.at[coming], **sems))
        return [], sends, recvs

    shapes = [jax.ShapeDtypeStruct(a.shape, a.dtype) for a in gathered]
    return Exchange(gathered, shapes, len(CHIP_PEERS), plan, aliased=True)


FIRST_HOP_PEERS = (1,) + CHIP_PEERS


def _gather_two_level(arrays, name):
    n = len(arrays)

    def body(*refs):
        srcs, dsts, (send_sems, recv_sems, local_sems) = refs[:n], refs[n:2 * n], refs[2 * n:]
        x, y, c, me = _place()
        sibling, sidx = _peer(x, y, c, 1)

        def copy(j, sem, block, to, src=None):
            rows = dsts[j].at[block]
            return pltpu.make_async_remote_copy(src_ref=rows if src is None else src, dst_ref=rows, send_sem=send_sems.at[j, sem],
                                                recv_sem=recv_sems.at[j, sem], device_id=to, device_id_type=_MESH)

        local = [pltpu.make_async_copy(srcs[j], dsts[j].at[me], local_sems.at[j]) for j in range(n)]
        for cp in local:
            cp.start()
        first = [copy(j, 1 + t, me, _peer(x, y, c, k)[0], src=srcs[j]) for t, k in enumerate(CHIP_PEERS) for j in range(n)]
        first += [copy(j, 0, me, sibling, src=srcs[j]) for j in range(n)]
        for cp in first:
            cp.start()
        passed = []
        for t, k in enumerate(CHIP_PEERS):
            peer, pidx = _peer(x, y, c, k)
            for j in range(n):
                copy(j, 1 + t, pidx, peer).wait_recv()
                passed.append(copy(j, 4 + t, pidx, sibling))
                passed[-1].start()
        for j in range(n):
            copy(j, 0, sidx, sibling).wait_recv()
        for t, k in enumerate(CHIP_PEERS):
            _, pidx = _peer(x, y, c, k ^ 1)
            for j in range(n):
                copy(j, 4 + t, pidx, sibling).wait_recv()
        for cp in first + passed:
            cp.wait_send()
        for cp in local:
            cp.wait()

    shapes = [jax.ShapeDtypeStruct((N_DEV,) + a.shape, a.dtype) for a in arrays]
    return pl.pallas_call(body, name=name, in_specs=[_ANY] * n, out_specs=[_ANY] * n, out_shape=shapes,
                          scratch_shapes=_sem_scratch(n, 1 + 2 * len(CHIP_PEERS)))(*arrays)


def to_sibling(arrays):
    def plan(srcs, dsts, send_sems, recv_sems, local_sems):
        x, y, c, _ = _place()
        sibling, _ = _peer(x, y, c, 1)
        copies = [pltpu.make_async_remote_copy(src_ref=src.at[1 - c], dst_ref=dst, send_sem=send_sems.at[j, 0],
                                               recv_sem=recv_sems.at[j, 0], device_id=sibling, device_id_type=_MESH)
                  for j, (src, dst) in enumerate(zip(srcs, dsts))]
        return [], copies, copies

    return Exchange(arrays, [jax.ShapeDtypeStruct(a.shape[1:], a.dtype) for a in arrays], 1, plan)


def combine(a, b):
    assert not (a.aliased or b.aliased)
    na, nsem = len(a.arrays), len(a.scratch)

    def plan(srcs, dsts, *sems):
        return tuple(u + v for u, v in zip(a.plan(srcs[:na], dsts[:na], *sems[:nsem]), b.plan(srcs[na:], dsts[na:], *sems[nsem:])))

    both = Exchange(a.arrays + b.arrays, a.out_shapes + b.out_shapes, 1, plan)
    both.scratch = a.scratch + b.scratch
    return both


def pair_add(own, theirs, core, *, name):
    _, ns, r, c = own.shape
    rb = _row_block(r, c // 2)

    def body(core_ref, a_ref, b_ref, o_ref):
        o_ref[...] = (a_ref[...].astype(F32) + b_ref[...].astype(F32)).astype(o_ref.dtype)

    return pl.pallas_call(
        body, name=name,
        grid_spec=pltpu.PrefetchScalarGridSpec(
            num_scalar_prefetch=1, grid=(ns, r // rb),
            in_specs=[pl.BlockSpec((None, None, rb, c), lambda i, j, core_ref: (core_ref[0], i, j, 0)),
                      pl.BlockSpec((None, rb, c), lambda i, j, core_ref: (i, j, 0))],
            out_specs=pl.BlockSpec((None, rb, c), lambda i, j, core_ref: (i, j, 0))),
        out_shape=jax.ShapeDtypeStruct((ns, r, c), own.dtype), compiler_params=_cparams(),
    )(core, own, theirs)


ADAMW_BLOCK_BYTES = 1 << 20


def _row_block(r, c):
    for rb in range(r, 0, -1):
        if r % rb == 0 and (rb % 16 == 0 or rb == r) and rb * c * 4 <= ADAMW_BLOCK_BYTES:
            return rb
    return r


def adamw(w, m, v, parts, *, name):
    depth, r, c = w.shape
    n_parts = parts[0].shape[0]
    rb = _row_block(r, c)

    def body(w_ref, m_ref, v_ref, *refs):
        p_refs, (g_ref, d_ref, nm_ref, nv_ref) = refs[:depth], refs[depth:]

        def total(p_ref):
            g = p_ref[0].astype(F32)
            for j in range(1, n_parts):
                g = g + p_ref[j].astype(F32)
            return g

        g = total(p_refs[0])
        for l in range(1, depth):
            g = jnp.where(pl.program_id(0) == l, total(p_refs[l]), g)
        g_ref[...] = g
        m_new = ADAM_B1 * m_ref[...] + (1.0 - ADAM_B1) * g
        v_new = ADAM_B2 * v_ref[...] + (1.0 - ADAM_B2) * (g * g)
        m_hat = m_new / (1.0 - ADAM_B1 ** ADAM_STEP)
        v_hat = v_new / (1.0 - ADAM_B2 ** ADAM_STEP)
        d_ref[...] = -ADAM_LR * (m_hat / (jnp.sqrt(v_hat) + ADAM_EPS) + ADAM_WD * w_ref[...])
        nm_ref[...] = m_new
        nv_ref[...] = v_new

    wblk = pl.BlockSpec((None, rb, c), lambda l, i: (l, i, 0))
    pblk = pl.BlockSpec((n_parts, rb, c), lambda l, i: (0, i, 0))
    return pl.pallas_call(
        body, name=name, grid=(depth, r // rb),
        in_specs=[wblk, wblk, wblk] + [pblk] * depth, out_specs=[wblk] * 4,
        out_shape=[jax.ShapeDtypeStruct((depth, r, c), F32)] * 4, compiler_params=_cparams(),
    )(w, m, v, *parts)


BIG = (("w_in", 2), ("w_uq", 2), ("w_ukv", 2), ("w_pa", 2), ("w_pb", 1), ("w_o", 1), ("w_gate", 2), ("w_up", 2), ("w_down", 1))
SMALL = ("norm_mix_g", "pool_w", "pool_scale", "q_norm_g", "kv_norm_g", "norm_ffn_g", "final_norm_g")
SMALL_PER_LAYER = SMALL[:-1]
WEIGHTS = ("meta_tokens", "norm_mix_g", "w_in", "pool_w", "pool_scale", "q_norm_g", "kv_norm_g", "w_uq", "w_ukv", "w_pa", "w_pb",
           "w_o", "norm_ffn_g", "w_gate", "w_up", "w_down", "final_norm_g")
HEAD_QK = QK_NOPE + QK_ROPE
KR_END = Z_KR + QK_ROPE


def _cat_cols(parts):
    return [jnp.concatenate(parts, axis=1)]


def _cat_rows(parts):
    return [jnp.concatenate(parts, axis=0)]


def _arr_w_in(parts):
    full = jnp.concatenate(parts, axis=1)
    zc = lambda n: jnp.zeros((full.shape[0], n), full.dtype)
    return [jnp.concatenate([full[:, :Z_KR], zc(QK_NOPE), full[:, Z_KR:KR_END], zc(LANES - HEAD_QK), full[:, KR_END:]], axis=1)]


def _arr_w_uq(parts):
    full = jnp.concatenate(parts, axis=1)
    z = jnp.zeros((full.shape[0], HEAD_SLOT - HEAD_QK), full.dtype)
    pieces = []
    for hd in range(N_HEADS):
        pieces += [full[:, hd * HEAD_QK:(hd + 1) * HEAD_QK], z]
    return [jnp.concatenate(pieces, axis=1)]


def _arr_w_ukv(parts):
    full = jnp.concatenate(parts, axis=1)
    z = jnp.zeros((full.shape[0], HEAD_SLOT - QK_NOPE), full.dtype)
    wide = QK_NOPE + V_DIM
    k, v = [], []
    for hd in range(N_HEADS):
        k += [full[:, hd * wide:hd * wide + QK_NOPE], z]
        v.append(full[:, hd * wide + QK_NOPE:(hd + 1) * wide])
    return [jnp.concatenate(k, axis=1), jnp.concatenate(v, axis=1)]


def arrange(g, fn, out_shapes, name):
    def body(g_ref, *o_refs):
        for o_ref, val in zip(o_refs, fn([g_ref[p] for p in range(N_DEV)])):
            o_ref[...] = val

    return pl.pallas_call(
        body, name=name, grid=(1,),
        in_specs=[pl.BlockSpec(g.shape, lambda i: (0, 0, 0))],
        out_specs=[pl.BlockSpec(s, lambda i: (0, 0)) for s in out_shapes],
        out_shape=[jax.ShapeDtypeStruct(s, g.dtype) for s in out_shapes], compiler_params=_cparams(),
    )(g)


def _arranged_ranges(lo, hi):
    out = []
    for a, b, shift in ((0, Z_KR, 0), (Z_KR, KR_END, QK_NOPE), (KR_END, D_IN, LANES - QK_ROPE)):
        s, e = max(lo, a), min(hi, b)
        if s < e:
            out.append((s + shift, e + shift))
    return out


def _chunks_w_in(acc):
    cs = D_IN // N_DEV
    return [jnp.concatenate([acc[:, a:b] for a, b in _arranged_ranges(p * cs, (p + 1) * cs)], axis=1) for p in range(N_DEV)]


def _chunks_w_uq(acc):
    per = N_HEADS // N_DEV
    return [jnp.concatenate([acc[:, hd * HEAD_SLOT:hd * HEAD_SLOT + HEAD_QK] for hd in range(p * per, (p + 1) * per)], axis=1)
            for p in range(N_DEV)]


def _chunks_w_ukv(acc_k, acc_v):
    per = N_HEADS // N_DEV
    out = []
    for p in range(N_DEV):
        pieces = []
        for hd in range(p * per, (p + 1) * per):
            pieces += [acc_k[:, hd * HEAD_SLOT:hd * HEAD_SLOT + QK_NOPE], acc_v[:, hd * V_DIM:(hd + 1) * V_DIM]]
        out.append(jnp.concatenate(pieces, axis=1))
    return out


def _chunks_cols(acc):
    cs = acc.shape[1] // N_DEV
    return [acc[:, p * cs:(p + 1) * cs] for p in range(N_DEV)]


def _chunks_rows(acc):
    rs = acc.shape[0] // N_DEV
    return [acc[p * rs:(p + 1) * rs, :] for p in range(N_DEV)]


def _chunks_cols_transposed(acc):
    at = acc[...].T
    rs = at.shape[0] // N_DEV
    return [at[p * rs:(p + 1) * rs, :] for p in range(N_DEV)]


def _pack(parts, row_multiple):
    flat = jnp.concatenate([p.reshape(-1) for p in parts])
    return jnp.pad(flat, (0, -flat.shape[0] % (row_multiple * LANES))).reshape(-1, LANES)


def _unpack(packed, shapes):
    flat, out, off = packed.reshape(-1), [], 0
    for s in shapes:
        n = 1
        for d in s:
            n *= d
        out.append(flat[off:off + n].reshape(s))
        off += n
    return out


def _rope_table(lp, nb):
    inv = 1.0 / (ROPE_THETA ** (jnp.arange(0, QK_ROPE, 2, dtype=F32) / QK_ROPE))
    ang = jnp.arange(lp, dtype=F32)[:, None] * inv[None, :]
    cos, sin = jnp.cos(ang), jnp.sin(ang)
    z = lambda n: jnp.zeros((lp, n), F32)
    tail = LANES - QK_NOPE - QK_ROPE
    c = jnp.concatenate([jnp.ones((lp, QK_NOPE), F32), cos, cos, z(tail)], axis=1)
    cr = jnp.concatenate([z(QK_NOPE), cos, cos, z(tail)], axis=1)
    s1 = jnp.concatenate([z(QK_NOPE), -sin, z(HALF_ROPE), z(tail)], axis=1)
    s2 = jnp.concatenate([z(QK_NOPE), z(HALF_ROPE), sin, z(tail)], axis=1)
    return jnp.tile(jnp.concatenate([c, cr, s1, s2], axis=1), (nb, 1))


MIX_IN, MIX_OUT = ("w_in", "w_uq", "w_ukv"), ("w_pa", "w_pb", "w_o")
MIX = MIX_IN + MIX_OUT
FFN = ("w_gate", "w_up", "w_down")
TRANSPOSED = ("w_gate", "w_up")
ARRANGERS = {
    "w_in": (_arr_w_in, (("win", (D_MODEL, DZ)),)), "w_uq": (_arr_w_uq, (("wuq", (Q_RANK, QK_WIDTH)),)),
    "w_ukv": (_arr_w_ukv, (("wuk", (KV_RANK, QK_WIDTH)), ("wuv", (KV_RANK, D_MODEL)))),
    "w_pa": (_cat_cols, (("wpa", (POOL_WIDTH, D_MODEL)),)), "w_pb": (_cat_rows, (("wpb", (D_MODEL, D_MODEL)),)),
    "w_o": (_cat_rows, (("wo", (D_MODEL, D_MODEL)),)), "w_gate": (_cat_rows, (("wgt", (D_FF, D_MODEL)),)),
    "w_up": (_cat_rows, (("wut", (D_FF, D_MODEL)),)), "w_down": (_cat_rows, (("wd", (D_FF, D_MODEL)),)),
}


def _operands(gathered, names, l):
    p = {}
    for n in names:
        fn, outs = ARRANGERS[n]
        if fn is _cat_rows:
            p[outs[0][0]] = gathered[n].reshape(outs[0][1])
            continue
        for (key, _), a in zip(outs, arrange(gathered[n], fn, [s for _, s in outs], f"arrange_{n}_{l}")):
            p[key] = a
    return p


def _small_operands(small, l):
    pw = small["pool_w"][l].astype(BF16)
    return dict(g_mix=small["norm_mix_g"][l][None], gq=small["q_norm_g"][l][None], gkv=small["kv_norm_g"][l][None],
                g_ffn=small["norm_ffn_g"][l][None], ps=small["pool_scale"][l][None], pw=pw)


class MeshComm:
    def __init__(self, w, meta_tokens):
        self.src = lambda n, l: w[n][l].astype(BF16)
        self.meta_tokens = meta_tokens
        self.core = lax.axis_index("c").astype(jnp.int32).reshape(1)
        self.rides = {0: [(n, 0) for n in FFN] + [(n, 1) for n in MIX], 1: [(n, 1) for n in FFN]}

    def first_weights(self):
        got = _gather_two_level([self.src(n, 0) for n in MIX_IN] + [self.meta_tokens], "gather_mix_0")
        return dict(zip(MIX_IN, got)), jnp.moveaxis(got[-1], 0, 1).reshape(N_META, D_MODEL)

    def early_first_hop(self):
        return exchange([self.src(n, 0) for n in MIX_OUT], False, FIRST_HOP_PEERS)

    def early_weights(self, landed):
        return dict(zip(MIX_OUT, second_hop(landed).run("second_hop_mix_0")))

    def first_hop(self, l):
        return exchange([self.src(n, layer) for n, layer in self.rides[l]], False, FIRST_HOP_PEERS)

    def second_hop(self, l, landed):
        return second_hop(landed)

    def carried(self, l, full, names, layer):
        return {n: full[self.rides[l].index((n, layer))] for n in names}

    def pair_exchange(self, own):
        return to_sibling(own)

    def pair_add(self, own, theirs, names, tag):
        return [pair_add(a, b, self.core, name=f"pair_add_{n}_{tag}") for n, a, b in zip(names, own, theirs)]

    def last_pair_exchange(self, own, small):
        got = combine(to_sibling(own), exchange([small], False, ALL_PEERS)).run("pair_grads_mix_0")
        return got[:-1], got[-1]

    def scatter(self, sums):
        return exchange(sums, True, CHIP_PEERS, by_chip=True)

    def scatter_and_gather(self, sums, small):
        return combine(self.scatter(sums), exchange([small], False, ALL_PEERS))

    def scattered_and_gathered(self, brought, small):
        return brought[:-1], brought[-1]

    def scatter_now(self, sums, name):
        return self.scatter(sums).run(name)


HEADS_FWD, HEADS_BWD = 8, 4
TILE_ROWS, TILE_ROWS_BWD = 512, 256


def _tile(t, target):
    n = max(1, -(-t // (target + target // 8)))
    while t % n or (t // n) % 16:
        n += 1
    return t // n


def _wgrad_tile(t):
    return max(tm for tm in (2 * TQ, TQ, LANES) if t % tm == 0)


def _ffn_bwd_part(dh2, p, s, tag, ride):
    d, ff = D_MODEL, D_FF // N_DEV
    t = dh2.shape[0]
    wg_ = lambda n, x, ys, fn, shape: wgrad(x, ys, fn, shape, tm=_wgrad_tile(t), name=f"wgrad_{n}_{tag}")[0]
    (dh1, hn2, act, dgt, dup, dg_ffn), brought = ffn_bwd(dh2, s["h1"], p["g_ffn"], s["gt"], s["up"], p["wgt"], p["wut"], p["wd"],
                                                         tm=_tile(t, TILE_ROWS_BWD), name=f"ffn_bwd_{tag}", ride=ride)
    chunks = [wg_("gate", hn2, [dgt], _chunks_cols_transposed, (ff, d)), wg_("up", hn2, [dup], _chunks_cols_transposed, (ff, d)),
              wg_("down", act, [dh2], _chunks_rows, (ff, d))]
    return dh1, chunks, dict(norm_ffn_g=dg_ffn[0]), brought


EARLY, LATE = ("w_o", "w_pa"), ("w_in", "w_uq", "w_ukv", "w_pb")


def _mix_bwd_part(dh1, p, s, rope, nb, lp, tag, comm, ride, next_ride):
    d = D_MODEL
    t = dh1.shape[0]
    wg_ = lambda n, x, ys, fn, shape: wgrad(x, ys, fn, shape, tm=_wgrad_tile(t), name=f"wgrad_{n}_{tag}")[0]
    (dga, dgb, dpa, dpb, do, dpool, dps, dpw), first = merge_bwd(dh1, s["z"], s["pa"], s["pb"], s["pooled"], p["pw"], p["ps"],
                                                                   p["wpa"], p["wpb"], p["wo"], tm=_tile(t, TILE_ROWS),
                                                                   name=f"merge_bwd_{tag}", ride=ride)
    c_o = wg_("o", s["mg"], [dh1], _chunks_rows, (d // N_DEV, d))
    c_pa = wg_("pa", s["a"], [dpa], _chunks_cols, (POOL_WIDTH, d // N_DEV))
    c_pb, theirs = wgrad(s["o"], [dpb], _chunks_rows, (d // N_DEV, d), tm=_wgrad_tile(t), name=f"wgrad_pb_{tag}",
                         ride=comm.pair_exchange([c_o, c_pa]))
    early = comm.pair_add([c_o, c_pa], theirs, EARLY, f"early_{tag}")
    (dq, dk, dv), brought = attn_bwd(s["q"], s["k"], s["v"], s["o"], do, s["lse"], nb=nb, lp=lp, hb=HEADS_BWD,
                                     name=f"attn_bwd_{tag}", ride=next_ride(first, early))
    dh, hn, dz, cqn, ckvn, dqb, dkb, dvb, dg_mix, dgq, dgkv = in_proj_bwd(
        dh1, s["h"], p["g_mix"], s["z"], dq, dk, dv, dga, dgb, dpool, p["win"], p["gq"], p["gkv"], p["wuq"], p["wuk"], p["wuv"],
        rope, tm=_tile(t, TILE_ROWS_BWD), lp=lp, nb=nb, name=f"in_proj_bwd_{tag}")
    c_in = wg_("in", hn, [dz], _chunks_w_in, (d, D_IN // N_DEV))
    c_uq = wg_("uq", cqn, [dqb], _chunks_w_uq, (Q_RANK, N_HEADS * HEAD_QK // N_DEV))
    c_ukv = wg_("ukv", ckvn, [dkb, dvb], _chunks_w_ukv, (KV_RANK, N_HEADS * (QK_NOPE + V_DIM) // N_DEV))
    small = dict(pool_scale=dps[0], pool_w=dpw, norm_mix_g=dg_mix[0], q_norm_g=dgq[0], kv_norm_g=dgkv[0])
    return dh, [c_in, c_uq, c_ukv, c_pb], small, brought


def train_step(x, loss_target, small, comm):
    nb, seq, d = x.shape
    lp = -(-(N_META + seq) // LANES) * LANES
    t = nb * lp
    assert nb <= 2 and DEPTH == 2
    tm = _tile(t, TILE_ROWS)
    rope = _rope_table(lp, nb)
    gathered, meta = comm.first_weights()
    pad = jnp.zeros((nb, lp - N_META - seq, d), F32)
    h = jnp.concatenate([jnp.broadcast_to(meta[None], (nb, N_META, d)), x, pad], axis=1).reshape(t, d)
    target = jnp.concatenate([jnp.zeros((nb, N_META, d), F32), loss_target, pad], axis=1).reshape(t, d)

    params, saved, full = [], [], {}
    for l in range(DEPTH):
        p = _small_operands(small, l)
        p.update(_operands(gathered, MIX_IN, 0) if l == 0 else _operands(comm.carried(0, full[0], MIX, 1), MIX, 1))
        (z, q, k, v), early = in_proj_fwd(h, p["g_mix"], p["win"], p["gq"], p["gkv"], p["wuq"], p["wuk"], p["wuv"], rope, tm=tm,
                                          name=f"in_proj_fwd_{l}", ride=comm.early_first_hop() if l == 0 else None)
        (o, lse), landed = attn_fwd(q, k, v, nb=nb, lp=lp, hb=HEADS_FWD, name=f"attn_fwd_{l}", ride=comm.first_hop(l))
        if l == 0:
            p.update(_operands(comm.early_weights(early), MIX_OUT, 0))
        (h1, pooled, a, pa, pb, mg), full[l] = merge_fwd(h, z, o, p["pw"], p["ps"], p["wpa"], p["wpb"], p["wo"], tm=tm, lp=lp,
                                                          nb=nb, name=f"merge_fwd_{l}", ride=comm.second_hop(l, landed))
        p.update(_operands(comm.carried(l, full[l], FFN, l), FFN, l))
        h2, gt, up = ffn_fwd(h1, p["g_ffn"], p["wgt"], p["wut"], p["wd"], tm=tm, name=f"ffn_fwd_{l}")
        params.append(p)
        saved.append(dict(h=h, z=z, q=q, k=k, v=v, o=o, lse=lse, h1=h1, pooled=pooled, a=a, pa=pa, pb=pb, mg=mg, gt=gt, up=up))
        h = h2
    parts, dh, dgf = loss_head(h, small["final_norm_g"][None], target, tm=tm, lp=lp, nb=nb, seq=seq, name="loss_head")
    loss = jnp.sum(parts[::8, 0])

    sums = {}
    dh, c_ffn1, small1, _ = _ffn_bwd_part(dh, params[1], saved[1], 1, None)
    dh, c_late1, sm, brought = _mix_bwd_part(
        dh, params[1], saved[1], rope, nb, lp, 1, comm, comm.pair_exchange(c_ffn1),
        lambda theirs, early: comm.scatter(comm.pair_add(c_ffn1, theirs, FFN, "ffn_1") + early))
    small1.update(sm)
    sums.update({(n, 1): a for n, a in zip(FFN + EARLY, brought)})
    dh, c_ffn0, small0, theirs = _ffn_bwd_part(dh, params[0], saved[0], 0, comm.pair_exchange(c_late1))
    s_late1 = comm.pair_add(c_late1, theirs, LATE, "late_1")
    upper = _pack([small1[n] for n in SMALL_PER_LAYER] + [dgf[0]], 8)
    dh, c_late0, sm, brought = _mix_bwd_part(
        dh, params[0], saved[0], rope, nb, lp, 0, comm, comm.pair_exchange(c_ffn0),
        lambda theirs, early: comm.scatter_and_gather(s_late1 + comm.pair_add(c_ffn0, theirs, FFN, "ffn_0") + early, upper))
    small0.update(sm)
    brought, upper_parts = comm.scattered_and_gathered(brought, upper)
    sums.update({(n, l): a for (n, l), a in zip([(n, 1) for n in LATE] + [(n, 0) for n in FFN + EARLY], brought)})
    dh = dh.reshape(nb, lp, d)
    dmeta = jnp.sum(dh[:, :N_META], axis=0)
    meta_chunks = jnp.transpose(dmeta.reshape(N_META, N_CHIPS, 2, d // N_DEV), (2, 1, 0, 3)).astype(BF16)
    small_grads = {n: jnp.stack([small0[n], small1[n]]) for n in small0}
    small_grads["final_norm_g"] = dgf[0]
    last_names = LATE + ("meta_tokens",)
    lower = _pack([small0[n] for n in SMALL_PER_LAYER] + [loss.reshape(1)], 8)
    theirs, lower_parts = comm.last_pair_exchange(c_late0 + [meta_chunks], lower)
    last = comm.scatter_now(comm.pair_add(c_late0 + [meta_chunks], theirs, last_names, "late_0"), "scatter_late_0")
    sums.update({(n, 0): a for n, a in zip(last_names, last)})
    return loss, dh[:, N_META:N_META + seq], sums, small_grads, (lower_parts, upper_parts)


def kernel(x, meta_tokens, norm_mix_g, w_in, pool_w, pool_scale, q_norm_g, kv_norm_g, w_uq, w_ukv, w_pa, w_pb, w_o, norm_ffn_g, w_gate, w_up, w_down, final_norm_g, loss_target, m_meta_tokens, m_norm_mix_g, m_w_in, m_pool_w, m_pool_scale, m_q_norm_g, m_kv_norm_g, m_w_uq, m_w_ukv, m_w_pa, m_w_pb, m_w_o, m_norm_ffn_g, m_w_gate, m_w_up, m_w_down, m_final_norm_g, v_meta_tokens, v_norm_mix_g, v_w_in, v_pool_w, v_pool_scale, v_q_norm_g, v_kv_norm_g, v_w_uq, v_w_ukv, v_w_pa, v_w_pb, v_w_o, v_norm_ffn_g, v_w_gate, v_w_up, v_w_down, v_final_norm_g):
    args = dict(locals())
    w = {n: args[n] for n in WEIGHTS}
    m = {n: args["m_" + n] for n in WEIGHTS}
    v = {n: args["v_" + n] for n in WEIGHTS}
    small = {n: w[n] for n in SMALL}
    as_handled = lambda a, n: jnp.swapaxes(a, 1, 2) if n in TRANSPOSED else a
    wh, mh, vh = ({n: as_handled(d[n], n) for n, _ in BIG} for d in (w, m, v))

    _, grad_x, sums, _, small_recv = train_step(x, loss_target, small, MeshComm(wh, meta_tokens))

    out = {n: [as_handled(a, n) for a in adamw(wh[n], mh[n], vh[n], [sums[(n, l)] for l in range(DEPTH)], name=f"adamw_{n}")]
           for n, _ in BIG}
    out["meta_tokens"] = [a[0] for a in adamw(meta_tokens[None], m["meta_tokens"][None], v["meta_tokens"][None],
                                              [sums[("meta_tokens", 0)]], name="adamw_meta_tokens")]
    lower_parts, upper_parts = small_recv
    loss_at = sum(math.prod(w[n].shape[1:]) for n in SMALL_PER_LAYER)
    loss = jnp.sum(lower_parts.reshape(N_DEV, -1)[:, loss_at])
    pk_lower = lambda d: _pack([d[n][0] for n in SMALL_PER_LAYER] + [jnp.zeros((1,), F32)], 8)[None]
    pk_upper = lambda d: _pack([d[n][1] for n in SMALL_PER_LAYER] + [d["final_norm_g"]], 8)[None]
    lower = adamw(pk_lower(w), pk_lower(m), pk_lower(v), [lower_parts], name="adamw_small_lower")
    upper = adamw(pk_upper(w), pk_upper(m), pk_upper(v), [upper_parts], name="adamw_small_upper")
    shapes = [w[n].shape[1:] for n in SMALL_PER_LAYER]
    for n in SMALL:
        out[n] = [None] * 4
    for kind in range(4):
        lo = _unpack(lower[kind][0], shapes)
        up = _unpack(upper[kind][0], shapes + [w["final_norm_g"].shape])
        for j, n in enumerate(SMALL_PER_LAYER):
            out[n][kind] = jnp.stack([lo[j], up[j]])
        out["final_norm_g"][kind] = up[-1]
    return (loss, grad_x, *[out[n][kind] for kind in range(4) for n in WEIGHTS])
```

```python
import functools
import math

import jax
import jax.numpy as jnp
from jax import lax
from jax.experimental import pallas as pl
from jax.experimental.pallas import tpu as pltpu

F32, BF16 = jnp.float32, jnp.bfloat16

D_MODEL = 1024
N_META = 16
N_HEADS = 16
QK_NOPE, QK_ROPE, V_DIM = 64, 32, 64
HALF_ROPE = QK_ROPE // 2
Q_RANK, KV_RANK = 256, 128
POOL_WINDOWS = (2, 4, 8, 16)
POOL_GROUP = 128
POOL_WIDTH = POOL_GROUP * len(POOL_WINDOWS)
POOL_HALO = 16
D_FF = 2816
D_IN = 2976
NORM_EPS = 1e-6
SM_SCALE = (QK_NOPE + QK_ROPE) ** -0.5
LOG2E = math.log2(math.e)
EXP2_SCALE = SM_SCALE * LOG2E
MASK_VALUE = -1e30
ROPE_THETA = 10000.0
DEPTH = 2
N_DEV = 8

ADAM_LR, ADAM_B1, ADAM_B2, ADAM_EPS, ADAM_WD, ADAM_STEP = 0.001, 0.9, 0.999, 1e-08, 0.01, 10

LANES = 128
HEAD_SLOT = LANES
QK_WIDTH = N_HEADS * HEAD_SLOT
Z_CQ, Z_CKV, Z_KR, Z_GA, Z_GB, DZ = 512, 768, 896, 1024, 2048, 3072
TQ = TK = 256
VMEM_LIMIT = 56 * 1024 * 1024


def _cparams():
    return pltpu.CompilerParams(vmem_limit_bytes=VMEM_LIMIT)


def _rows(tm, width, col=0):
    return pl.BlockSpec((tm, width), lambda i: (i, col))


def _whole(shape):
    zeros = (0,) * len(shape)
    return pl.BlockSpec(shape, lambda i: zeros, pipeline_mode=pl.Buffered(1))


def _acc(shape):
    zeros = (0,) * len(shape)
    return pl.BlockSpec(shape, lambda i: zeros)


def _dot(a, b):
    return jnp.dot(a, b, preferred_element_type=F32)


def _dot_tn(a, b):
    return lax.dot_general(a, b, (((0,), (0,)), ((), ())), preferred_element_type=F32)


def _dot_nt(a, b):
    return lax.dot_general(a, b, (((1,), (1,)), ((), ())), preferred_element_type=F32)


def _rms(x):
    r = lax.rsqrt(jnp.mean(x * x, axis=-1, keepdims=True) + NORM_EPS)
    return x * r, r


def _rms_bwd(dy, xhat, r, g):
    dg = jnp.sum(dy * xhat, axis=0, keepdims=True)
    dxh = dy * g
    dx = r * (dxh - xhat * jnp.mean(dxh * xhat, axis=-1, keepdims=True))
    return dx, dg


def _sigmoid(x):
    return 1.0 / (1.0 + jnp.exp(-x))


def _rope_fwd(q, c, s1, s2):
    w = q.shape[1]
    return q * c + pltpu.roll(q, w - HALF_ROPE, 1) * s1 + pltpu.roll(q, HALF_ROPE, 1) * s2


def _rope_bwd(dq, c, s1, s2):
    w = dq.shape[1]
    return dq * c + pltpu.roll(dq * s1, HALF_ROPE, 1) + pltpu.roll(dq * s2, w - HALF_ROPE, 1)


def _rope_tables(rope, reps):
    c, cr, s1, s2 = (rope[:, k * LANES:(k + 1) * LANES] for k in range(4))
    if reps > 1:
        return jnp.tile(c, (1, reps)), jnp.tile(s1, (1, reps)), jnp.tile(s2, (1, reps))
    return cr, s1, s2


def _seq_pos(gi, lp, nb):
    pos = gi
    for b in range(1, nb):
        pos = jnp.where(gi >= b * lp, gi - b * lp, pos)
    return pos


_ANY = pl.BlockSpec(memory_space=pl.ANY)


def _carrying_call(body, ride, operands, *, name, grid, in_specs, out_specs, out_shape, scratch_shapes=()):
    n_in, n_out = len(in_specs), len(out_specs)
    if ride is None:
        out = pl.pallas_call(body, name=name, grid=grid, in_specs=in_specs, out_specs=out_specs, out_shape=out_shape,
                             scratch_shapes=list(scratch_shapes), compiler_params=_cparams())(*operands)
        return out, []
    ne = len(ride.arrays)

    def carrying(*refs):
        ins, r_in, rest = refs[:n_in], refs[n_in:n_in + ne], refs[n_in + ne:]
        outs, r_out, rest = rest[:n_out], rest[n_out:n_out + ne], rest[n_out + ne:]
        scratch, sems = rest[:len(scratch_shapes)], rest[len(scratch_shapes):]
        ids = [pl.program_id(a) for a in range(len(grid))]
        first = functools.reduce(jnp.logical_and, [i == 0 for i in ids])
        last = functools.reduce(jnp.logical_and, [i == g - 1 for i, g in zip(ids, grid)])

        @pl.when(first)
        def _():
            ride.start(r_in, r_out, sems)

        body(*ins, *outs, *scratch)

        @pl.when(last)
        def _():
            ride.wait(r_in, r_out, sems)

    out = pl.pallas_call(
        carrying, name=name, grid=grid, in_specs=list(in_specs) + [_ANY] * ne, out_specs=list(out_specs) + [_ANY] * ne,
        out_shape=list(out_shape) + ride.out_shapes, scratch_shapes=list(scratch_shapes) + ride.scratch,
        input_output_aliases=ride.aliases(n_in, n_out), compiler_params=_cparams(),
    )(*operands, *ride.arrays)
    return out[:n_out], out[n_out:]


def in_proj_fwd(h, g_mix, win, gq, gkv, wuq, wuk, wuv, rope, *, tm, name, ride=None):
    t = h.shape[0]

    def body(h_ref, g_ref, win_ref, gq_ref, gkv_ref, wuq_ref, wuk_ref, wuv_ref, rope_ref, z_ref, q_ref, k_ref, v_ref):
        xhat, _ = _rms(h_ref[...])
        hn = (xhat * g_ref[...]).astype(BF16)
        z = _dot(hn, win_ref[...])
        z_ref[...] = z
        rope_t = rope_ref[...]
        xq, _ = _rms(z[:, Z_CQ:Z_CKV])
        cqn = (xq * gq_ref[...]).astype(BF16)
        q = _rope_fwd(_dot(cqn, wuq_ref[...]), *_rope_tables(rope_t, N_HEADS))
        q_ref[...] = q.astype(BF16)
        xkv, _ = _rms(z[:, Z_CKV:Z_KR])
        ckvn = (xkv * gkv_ref[...]).astype(BF16)
        kr = _rope_fwd(z[:, Z_KR:Z_GA], *_rope_tables(rope_t, 1))
        k_ref[...] = (_dot(ckvn, wuk_ref[...]) + jnp.tile(kr, (1, N_HEADS))).astype(BF16)
        v_ref[...] = _dot(ckvn, wuv_ref[...]).astype(BF16)

    return _carrying_call(
        body, ride, (h, g_mix, win, gq, gkv, wuq, wuk, wuv, rope), name=name, grid=(t // tm,),
        in_specs=[_rows(tm, D_MODEL), _whole((1, D_MODEL)), _whole((D_MODEL, DZ)), _whole((1, Q_RANK)), _whole((1, KV_RANK)),
                  _whole((Q_RANK, QK_WIDTH)), _whole((KV_RANK, QK_WIDTH)), _whole((KV_RANK, D_MODEL)), _rows(tm, 4 * LANES)],
        out_specs=[_rows(tm, DZ), _rows(tm, QK_WIDTH), _rows(tm, QK_WIDTH), _rows(tm, D_MODEL)],
        out_shape=[jax.ShapeDtypeStruct((t, DZ), F32), jax.ShapeDtypeStruct((t, QK_WIDTH), BF16),
                   jax.ShapeDtypeStruct((t, QK_WIDTH), BF16), jax.ShapeDtypeStruct((t, D_MODEL), BF16)])


def attn_fwd(q, k, v, *, nb, lp, hb, name, ride=None):
    t = q.shape[0]
    nq, tail = lp // TQ, lp % TQ
    assert tail % LANES == 0

    def body(q_ref, k_ref, v_ref, o_ref, lse_ref, vt):
        for pr in range(hb // 2):
            vt[pr] = v_ref[:, pr * LANES:(pr + 1) * LANES].T

        def q_block(qs, tq, whole_k):
            qh = [q_ref[pl.ds(qs, tq), hd * HEAD_SLOT:(hd + 1) * HEAD_SLOT] for hd in range(hb)]
            keep = lax.broadcasted_iota(jnp.int32, (tq, tq), 0) <= lax.broadcasted_iota(jnp.int32, (tq, tq), 1)

            def k_steps(blocks, c, masked):
                sts = [[_dot_nt(k_ref[pl.ds(ks, tk), hd * HEAD_SLOT:(hd + 1) * HEAD_SLOT], qh[hd]) for hd in range(hb)]
                       for ks, tk in blocks]
                for (ks, tk), st_b in zip(blocks, sts):
                    ps, stats = [], []
                    for hd in range(hb):
                        m, l, _ = c[hd]
                        st = jnp.where(keep, st_b[hd], MASK_VALUE) if masked else st_b[hd]
                        m_new = jnp.maximum(m, jnp.max(st, axis=0, keepdims=True))
                        p = jnp.exp2((st - m_new) * EXP2_SCALE)
                        alpha = jnp.exp2((m - m_new) * EXP2_SCALE)
                        ps.append(p.astype(BF16))
                        stats.append((m_new, alpha * l + jnp.sum(p, axis=0, keepdims=True), alpha))
                    pvs = [_dot(vt[hd // 2, :, pl.ds(ks, tk)], ps[hd]) for hd in range(hb)]
                    c = tuple((stats[hd][0], stats[hd][1], stats[hd][2] * c[hd][2] + pvs[hd]) for hd in range(hb))
                return c

            def two_blocks(i, c):
                ks = pl.multiple_of(2 * i * TK, TK)
                return k_steps([(ks, TK), (ks + TK, TK)], c, False)

            init = tuple((jnp.full((1, tq), MASK_VALUE, F32), jnp.zeros((1, tq), F32), jnp.zeros((LANES, tq), F32))
                         for _ in range(hb))
            pairs = lax.div(whole_k, 2)
            c = lax.fori_loop(0, pairs, two_blocks, init)
            c = lax.fori_loop(2 * pairs, whole_k, lambda kj, c: k_steps([(pl.multiple_of(kj * TK, TK), TK)], c, False), c)
            c = k_steps([(qs, tq)], c, True)
            sub = lax.broadcasted_iota(jnp.int32, (LANES, tq), 0)
            for pr in range(hb // 2):
                (m0, l0, a0), (m1, l1, a1) = c[2 * pr], c[2 * pr + 1]
                o_ref[pl.ds(qs, tq), pr * LANES:(pr + 1) * LANES] = jnp.where(sub < V_DIM, a0 / l0, a1 / l1).T.astype(BF16)
                lse_ref[2 * pr, :, pl.ds(qs, tq)] = m0 * SM_SCALE + jnp.log(l0)
                lse_ref[2 * pr + 1, :, pl.ds(qs, tq)] = m1 * SM_SCALE + jnp.log(l1)

        def whole_q_block(qi, carry):
            q_block(pl.multiple_of(qi * TQ, TQ), TQ, qi)
            return carry

        lax.fori_loop(0, nq, whole_q_block, 0)
        if tail:
            q_block(nq * TQ, tail, nq)

    blk = lambda w: pl.BlockSpec((lp, w), lambda b, g: (b, g))
    return _carrying_call(
        body, ride, (q, k, v), name=name, grid=(nb, N_HEADS // hb),
        in_specs=[blk(hb * HEAD_SLOT), blk(hb * HEAD_SLOT), blk(hb * V_DIM)],
        out_specs=[blk(hb * V_DIM), pl.BlockSpec((hb, 1, lp), lambda b, g: (g, 0, b))],
        out_shape=[jax.ShapeDtypeStruct((t, D_MODEL), BF16), jax.ShapeDtypeStruct((N_HEADS, 1, t), F32)],
        scratch_shapes=[pltpu.VMEM((hb // 2, LANES, lp), BF16)])


def _pool_band_fwd(i, tm, lp, nb):
    r = lax.broadcasted_iota(jnp.int32, (tm, POOL_HALO + tm), 0)
    e = lax.broadcasted_iota(jnp.int32, (tm, POOL_HALO + tm), 1)
    diff = r + POOL_HALO - e
    pos = _seq_pos(i * tm + lax.broadcasted_iota(jnp.int32, (tm, 1), 0), lp, nb)
    out = []
    for w in POOL_WINDOWS:
        cnt = jnp.minimum(pos + 1, w)
        band = jnp.where((diff >= 0) & (diff < cnt), 1.0, 0.0).astype(BF16)
        out.append((band, cnt.astype(F32)))
    return out


def merge_fwd(h, z, o, pw, ps, wpa, wpb, wo, *, tm, lp, nb, name, ride=None):
    t = h.shape[0]
    hb = tm // POOL_HALO

    def body(h_ref, u_ref, uprev_ref, ga_ref, gb_ref, o_ref, pw_ref, ps_ref, wpa_ref, wpb_ref, wo_ref,
             h1_ref, pooled_ref, a_ref, pa_ref, pb_ref, mg_ref):
        i = pl.program_id(0)
        u = u_ref[...]
        uext = jnp.concatenate([uprev_ref[...], u], axis=0).astype(BF16)
        pooled, ys = [], []
        for g, (band, cnt) in enumerate(_pool_band_fwd(i, tm, lp, nb)):
            gs = slice(g * POOL_GROUP, (g + 1) * POOL_GROUP)
            pg = (_dot(band, uext[:, gs]) / cnt - u[:, gs]).astype(BF16)
            pooled.append(pg)
            ys.append(_dot(pg, pw_ref[g]))
        pooled_ref[...] = jnp.concatenate(pooled, axis=1)
        a = (jnp.concatenate(ys, axis=1) * ps_ref[...]).astype(BF16)
        a_ref[...] = a
        pa = _dot(a, wpa_ref[...])
        pb = _dot(o_ref[...], wpb_ref[...])
        pa_ref[...] = pa.astype(BF16)
        pb_ref[...] = pb.astype(BF16)
        mg = (_sigmoid(ga_ref[...]) * pa + _sigmoid(gb_ref[...]) * pb).astype(BF16)
        mg_ref[...] = mg
        h1_ref[...] = h_ref[...] + _dot(mg, wo_ref[...])

    halo = pl.BlockSpec((POOL_HALO, POOL_WIDTH), lambda i: (jnp.maximum(i * hb - 1, 0), 0))
    return _carrying_call(
        body, ride, (h, z, z, z, z, o, pw, ps, wpa, wpb, wo), name=name, grid=(t // tm,),
        in_specs=[_rows(tm, D_MODEL), _rows(tm, POOL_WIDTH), halo, _rows(tm, D_MODEL, 1), _rows(tm, D_MODEL, 2), _rows(tm, D_MODEL),
                  _whole((4, POOL_GROUP, POOL_GROUP)), _whole((1, POOL_WIDTH)), _whole((POOL_WIDTH, D_MODEL)),
                  _whole((D_MODEL, D_MODEL)), _whole((D_MODEL, D_MODEL))],
        out_specs=[_rows(tm, D_MODEL), _rows(tm, POOL_WIDTH), _rows(tm, POOL_WIDTH), _rows(tm, D_MODEL), _rows(tm, D_MODEL),
                   _rows(tm, D_MODEL)],
        out_shape=[jax.ShapeDtypeStruct((t, D_MODEL), F32), jax.ShapeDtypeStruct((t, POOL_WIDTH), BF16),
                   jax.ShapeDtypeStruct((t, POOL_WIDTH), BF16), jax.ShapeDtypeStruct((t, D_MODEL), BF16),
                   jax.ShapeDtypeStruct((t, D_MODEL), BF16), jax.ShapeDtypeStruct((t, D_MODEL), BF16)])


def ffn_fwd(h1, g, wgt, wut, wd, *, tm, name):
    t = h1.shape[0]

    def body(h_ref, g_ref, wgt_ref, wut_ref, wd_ref, h2_ref, gt_ref, up_ref):
        h = h_ref[...]
        xhat, _ = _rms(h)
        hn = (xhat * g_ref[...]).astype(BF16)
        gt = _dot_nt(hn, wgt_ref[...])
        up = _dot_nt(hn, wut_ref[...])
        gt_ref[...] = gt.astype(BF16)
        up_ref[...] = up.astype(BF16)
        act = (gt * _sigmoid(gt) * up).astype(BF16)
        h2_ref[...] = h + _dot(act, wd_ref[...])

    return pl.pallas_call(
        body, name=name, grid=(t // tm,),
        in_specs=[_rows(tm, D_MODEL), _whole((1, D_MODEL)), _whole((D_FF, D_MODEL)), _whole((D_FF, D_MODEL)), _whole((D_FF, D_MODEL))],
        out_specs=[_rows(tm, D_MODEL), _rows(tm, D_FF), _rows(tm, D_FF)],
        out_shape=[jax.ShapeDtypeStruct((t, D_MODEL), F32), jax.ShapeDtypeStruct((t, D_FF), BF16), jax.ShapeDtypeStruct((t, D_FF), BF16)],
        compiler_params=_cparams(),
    )(h1, g, wgt, wut, wd)


def loss_head(h, g, target, *, tm, lp, nb, seq, name):
    t = h.shape[0]
    nt = t // tm

    def body(h_ref, g_ref, t_ref, loss_ref, dh_ref, dg_ref):
        i = pl.program_id(0)
        pos = _seq_pos(i * tm + lax.broadcasted_iota(jnp.int32, (tm, 1), 0), lp, nb)
        real = (pos >= N_META) & (pos < N_META + seq)
        xhat, r = _rms(h_ref[...])
        gg = g_ref[...]
        err = jnp.where(real, xhat * gg - t_ref[...], 0.0)
        loss_ref[...] = jnp.full((8, LANES), 0.5 * jnp.sum(err * err) / D_MODEL, F32)
        dx, dg = _rms_bwd(err * (1.0 / D_MODEL), xhat, r, gg)
        dh_ref[...] = dx

        @pl.when(i == 0)
        def _():
            dg_ref[...] = jnp.zeros_like(dg_ref)

        dg_ref[...] += dg

    return pl.pallas_call(
        body, name=name, grid=(nt,),
        in_specs=[_rows(tm, D_MODEL), _whole((1, D_MODEL)), _rows(tm, D_MODEL)],
        out_specs=[pl.BlockSpec((8, LANES), lambda i: (i, 0)), _rows(tm, D_MODEL), _acc((1, D_MODEL))],
        out_shape=[jax.ShapeDtypeStruct((nt * 8, LANES), F32), jax.ShapeDtypeStruct((t, D_MODEL), F32),
                   jax.ShapeDtypeStruct((1, D_MODEL), F32)],
        compiler_params=_cparams(),
    )(h, g, target)


def wgrad(x, ys, chunk_fn, chunk_shape, *, tm, name, ride=None):
    t, m = x.shape
    tiles = t // tm
    steps = -(-tiles // 2)

    def body(*refs):
        ins, o_ref, accs = refs[:2 * (1 + len(ys))], refs[2 * (1 + len(ys))], refs[2 * (1 + len(ys)) + 1:]
        i = pl.program_id(0)

        @pl.when(i == 0)
        def _():
            for acc in accs:
                acc[...] = jnp.zeros_like(acc)

        def both(first, second, mask):
            b = second[...].astype(BF16)
            if mask and tiles % 2:
                b = jnp.where(2 * i + 1 < tiles, b, jnp.zeros_like(b))
            return jnp.concatenate([first[...].astype(BF16), b], axis=0)

        xb = both(ins[0], ins[1], True)
        for j, acc in enumerate(accs):
            acc[...] += _dot_tn(xb, both(ins[2 + 2 * j], ins[3 + 2 * j], False))

        @pl.when(i == steps - 1)
        def _():
            for p, chunk in enumerate(chunk_fn(*accs)):
                o_ref[p % 2, p // 2] = chunk.astype(BF16)

    def two_tiles(width):
        return [pl.BlockSpec((tm, width), lambda i: (2 * i, 0)),
                pl.BlockSpec((tm, width), lambda i: (jnp.minimum(2 * i + 1, tiles - 1), 0))]

    out = (2, N_DEV // 2) + tuple(chunk_shape)
    operands = [x, x] + [a for y in ys for a in (y, y)]
    (chunks,), brought = _carrying_call(
        body, ride, operands, name=name, grid=(steps,),
        in_specs=two_tiles(m) + [s for y in ys for s in two_tiles(y.shape[1])], out_specs=[_acc(out)],
        out_shape=[jax.ShapeDtypeStruct(out, BF16)], scratch_shapes=[pltpu.VMEM((m, y.shape[1]), F32) for y in ys])
    return chunks, brought


def ffn_bwd(dh2, h1, g, gt, up, wgt, wut, wd, *, tm, name, ride=None):
    t = h1.shape[0]

    def body(dh2_ref, h_ref, g_ref, gt_ref, up_ref, wgt_ref, wut_ref, wd_ref, dh1_ref, hn_ref, act_ref, dgt_ref, dup_ref, dg_ref):
        dh2 = dh2_ref[...]
        dact = _dot_nt(dh2.astype(BF16), wd_ref[...])
        gt = gt_ref[...].astype(F32)
        up = up_ref[...].astype(F32)
        sg = _sigmoid(gt)
        silu = gt * sg
        act_ref[...] = (silu * up).astype(BF16)
        dgt = (dact * up * (sg * (1.0 + gt * (1.0 - sg)))).astype(BF16)
        dup = (dact * silu).astype(BF16)
        dgt_ref[...] = dgt
        dup_ref[...] = dup
        dhn = _dot(dgt, wgt_ref[...]) + _dot(dup, wut_ref[...])
        xhat, r = _rms(h_ref[...])
        gg = g_ref[...]
        hn_ref[...] = (xhat * gg).astype(BF16)
        dx, dg = _rms_bwd(dhn, xhat, r, gg)
        dh1_ref[...] = dh2 + dx

        @pl.when(pl.program_id(0) == 0)
        def _():
            dg_ref[...] = jnp.zeros_like(dg_ref)

        dg_ref[...] += dg

    return _carrying_call(
        body, ride, (dh2, h1, g, gt, up, wgt, wut, wd), name=name, grid=(t // tm,),
        in_specs=[_rows(tm, D_MODEL), _rows(tm, D_MODEL), _whole((1, D_MODEL)), _rows(tm, D_FF), _rows(tm, D_FF),
                  _whole((D_FF, D_MODEL)), _whole((D_FF, D_MODEL)), _whole((D_FF, D_MODEL))],
        out_specs=[_rows(tm, D_MODEL), _rows(tm, D_MODEL), _rows(tm, D_FF), _rows(tm, D_FF), _rows(tm, D_FF), _acc((1, D_MODEL))],
        out_shape=[jax.ShapeDtypeStruct((t, D_MODEL), F32), jax.ShapeDtypeStruct((t, D_MODEL), BF16),
                   jax.ShapeDtypeStruct((t, D_FF), BF16), jax.ShapeDtypeStruct((t, D_FF), BF16),
                   jax.ShapeDtypeStruct((t, D_FF), BF16), jax.ShapeDtypeStruct((1, D_MODEL), F32)])


def merge_bwd(dh1, z, pa, pb, pooled, pw, ps, wpa, wpb, wo, *, tm, name, ride=None):
    t = dh1.shape[0]

    def body(dh1_ref, ga_ref, gb_ref, pa_ref, pb_ref, pooled_ref, pw_ref, ps_ref, wpa_ref, wpb_ref, wo_ref,
             dga_ref, dgb_ref, dpa_ref, dpb_ref, do_ref, dpool_ref, dps_ref, dpw_ref):
        dmg = _dot_nt(dh1_ref[...].astype(BF16), wo_ref[...])
        sa = _sigmoid(ga_ref[...])
        sb = _sigmoid(gb_ref[...])
        dga_ref[...] = (dmg * pa_ref[...].astype(F32) * sa * (1.0 - sa)).astype(BF16)
        dgb_ref[...] = (dmg * pb_ref[...].astype(F32) * sb * (1.0 - sb)).astype(BF16)
        dpa = (dmg * sa).astype(BF16)
        dpb = (dmg * sb).astype(BF16)
        dpa_ref[...] = dpa
        dpb_ref[...] = dpb
        do_ref[...] = _dot_nt(dpb, wpb_ref[...]).astype(BF16)
        da = _dot_nt(dpa, wpa_ref[...])
        pooled = pooled_ref[...]
        ps = ps_ref[...]

        @pl.when(pl.program_id(0) == 0)
        def _():
            dps_ref[...] = jnp.zeros_like(dps_ref)
            dpw_ref[...] = jnp.zeros_like(dpw_ref)

        dps, dpool = [], []
        for g in range(len(POOL_WINDOWS)):
            gs = slice(g * POOL_GROUP, (g + 1) * POOL_GROUP)
            y = _dot(pooled[:, gs], pw_ref[g])
            dps.append(jnp.sum(da[:, gs] * y, axis=0, keepdims=True))
            dy = (da[:, gs] * ps[:, gs]).astype(BF16)
            dpool.append(_dot_nt(dy, pw_ref[g]))
            dpw_ref[g] += _dot_tn(pooled[:, gs], dy)
        dps_ref[...] += jnp.concatenate(dps, axis=1)
        dpool_ref[...] = jnp.concatenate(dpool, axis=1)

    return _carrying_call(
        body, ride, (dh1, z, z, pa, pb, pooled, pw, ps, wpa, wpb, wo), name=name, grid=(t // tm,),
        in_specs=[_rows(tm, D_MODEL), _rows(tm, D_MODEL, 1), _rows(tm, D_MODEL, 2), _rows(tm, D_MODEL), _rows(tm, D_MODEL),
                  _rows(tm, POOL_WIDTH), _whole((4, POOL_GROUP, POOL_GROUP)),
                  _whole((1, POOL_WIDTH)), _whole((POOL_WIDTH, D_MODEL)), _whole((D_MODEL, D_MODEL)), _whole((D_MODEL, D_MODEL))],
        out_specs=[_rows(tm, D_MODEL), _rows(tm, D_MODEL), _rows(tm, D_MODEL), _rows(tm, D_MODEL), _rows(tm, D_MODEL),
                   _rows(tm, POOL_WIDTH), _acc((1, POOL_WIDTH)), _acc((4, POOL_GROUP, POOL_GROUP))],
        out_shape=[jax.ShapeDtypeStruct((t, D_MODEL), BF16)] * 5
        + [jax.ShapeDtypeStruct((t, POOL_WIDTH), F32), jax.ShapeDtypeStruct((1, POOL_WIDTH), F32),
           jax.ShapeDtypeStruct((4, POOL_GROUP, POOL_GROUP), F32)])


def attn_bwd(q, k, v, o, do, lse, *, nb, lp, hb, name, ride=None):
    t = q.shape[0]
    nq, tail = lp // TQ, lp % TQ
    assert tail % LANES == 0

    def body(q_ref, k_ref, v_ref, o_ref, do_ref, lse_ref, dq_ref, dk_ref, dv_ref, kt, doh, lse_row, delta_row, dqt):
        lane = lax.broadcasted_iota(jnp.int32, (lp, LANES), 1)
        first = lane < V_DIM
        sub = lax.broadcasted_iota(jnp.int32, (LANES, lp), 0)
        for pr in range(hb // 2):
            ls = slice(pr * LANES, (pr + 1) * LANES)
            do = do_ref[:, ls]
            doh[2 * pr] = jnp.where(first, do, jnp.zeros_like(do))
            doh[2 * pr + 1] = jnp.where(first, jnp.zeros_like(do), do)
            prod_t = (do.astype(F32) * o_ref[:, ls].astype(F32)).T
            delta_row[2 * pr] = jnp.sum(jnp.where(sub < V_DIM, prod_t, 0.0), axis=0, keepdims=True)
            delta_row[2 * pr + 1] = jnp.sum(jnp.where(sub < V_DIM, 0.0, prod_t), axis=0, keepdims=True)
        for hd in range(hb):
            lse_row[hd] = lse_ref[hd] * LOG2E
            kt[hd] = k_ref[:, hd * HEAD_SLOT:(hd + 1) * HEAD_SLOT].T
        dqt[...] = jnp.zeros(dqt.shape, F32)
        heads = range(hb)
        hss = [slice(hd * HEAD_SLOT, (hd + 1) * HEAD_SLOT) for hd in heads]

        def k_block(ks, tk, next_q):
            keep = lax.broadcasted_iota(jnp.int32, (tk, tk), 0) <= lax.broadcasted_iota(jnp.int32, (tk, tk), 1)

            def q_steps(blocks, c, masked):
                work = [(qs, tq, hd) for qs, tq in blocks for hd in heads]
                qhs = [q_ref[pl.ds(qs, tq), hss[hd]] for qs, tq, hd in work]
                dos = [doh[hd, pl.ds(qs, tq), :] for qs, tq, hd in work]
                sts = [_dot_nt(k_ref[pl.ds(ks, tk), hss[hd]], qhs[i]) for i, (_, _, hd) in enumerate(work)]
                dpts = [_dot_nt(v_ref[pl.ds(ks, tk), (hd // 2) * LANES:(hd // 2 + 1) * LANES], dos[i])
                        for i, (_, _, hd) in enumerate(work)]
                pts, dsts = [], []
                for i, (qs, tq, hd) in enumerate(work):
                    st = jnp.where(keep, sts[i], MASK_VALUE) if masked else sts[i]
                    pt = jnp.exp2(st * EXP2_SCALE - lse_row[hd, :, pl.ds(qs, tq)])
                    dsts.append((pt * (dpts[i] - delta_row[hd, :, pl.ds(qs, tq)])).astype(BF16))
                    pts.append(pt.astype(BF16))
                dvs = [_dot(pts[i], dos[i]) for i in range(len(work))]
                dks = [_dot(dsts[i], qhs[i]) for i in range(len(work))]
                dqs = [_dot(kt[hd, :, pl.ds(ks, tk)], dsts[i]) for i, (_, _, hd) in enumerate(work)]
                c = list(c)
                for i, (qs, tq, hd) in enumerate(work):
                    dqt[hd, :, pl.ds(qs, tq)] += dqs[i]
                    c[hd] = (c[hd][0] + dks[i], c[hd][1] + dvs[i])
                return tuple(c)

            zero = jnp.zeros((tk, LANES), F32)
            c = q_steps([(ks, tk)], tuple((zero, zero) for _ in heads), True)
            if next_q is not None:
                def two_blocks(i, c):
                    qs = pl.multiple_of((next_q + 2 * i) * TQ, TQ)
                    return q_steps([(qs, TQ), (qs + TQ, TQ)], c, False)

                pairs = lax.div(nq - next_q, 2)
                c = lax.fori_loop(0, pairs, two_blocks, c)
                c = lax.fori_loop(next_q + 2 * pairs, nq, lambda qi, c: q_steps([(pl.multiple_of(qi * TQ, TQ), TQ)], c, False), c)
                if tail:
                    c = q_steps([(nq * TQ, tail)], c, False)
            for hd in heads:
                dk_ref[pl.ds(ks, tk), hss[hd]] = (c[hd][0] * SM_SCALE).astype(BF16)
            for pr in range(hb // 2):
                dv_ref[pl.ds(ks, tk), pr * LANES:(pr + 1) * LANES] = (c[2 * pr][1] + c[2 * pr + 1][1]).astype(BF16)

        def whole_k_block(kj, carry):
            k_block(pl.multiple_of(kj * TK, TK), TK, kj + 1)
            return carry

        lax.fori_loop(0, nq, whole_k_block, 0)
        if tail:
            k_block(nq * TQ, tail, None)
        for hd in range(hb):
            dq_ref[:, hd * HEAD_SLOT:(hd + 1) * HEAD_SLOT] = (dqt[hd].T * SM_SCALE).astype(BF16)

    blk = lambda w: pl.BlockSpec((lp, w), lambda b, g: (b, g))
    return _carrying_call(
        body, ride, (q, k, v, o, do, lse), name=name, grid=(nb, N_HEADS // hb),
        in_specs=[blk(hb * HEAD_SLOT), blk(hb * HEAD_SLOT), blk(hb * V_DIM), blk(hb * V_DIM), blk(hb * V_DIM),
                  pl.BlockSpec((hb, 1, lp), lambda b, g: (g, 0, b))],
        out_specs=[blk(hb * HEAD_SLOT), blk(hb * HEAD_SLOT), blk(hb * V_DIM)],
        out_shape=[jax.ShapeDtypeStruct((t, QK_WIDTH), BF16), jax.ShapeDtypeStruct((t, QK_WIDTH), BF16),
                   jax.ShapeDtypeStruct((t, D_MODEL), BF16)],
        scratch_shapes=[pltpu.VMEM((hb, HEAD_SLOT, lp), BF16), pltpu.VMEM((hb, lp, LANES), BF16), pltpu.VMEM((hb, 1, lp), F32),
                        pltpu.VMEM((hb, 1, lp), F32), pltpu.VMEM((hb, HEAD_SLOT, lp), F32)])


def in_proj_bwd(dh1, h, g_mix, z, dq, dk, dv, dga, dgb, dpool, win, gq, gkv, wuq, wuk, wuv, rope, *, tm, lp, nb, name):
    t = h.shape[0]
    hb = tm // POOL_HALO
    last_halo = t // POOL_HALO - 1

    def body(dh1_ref, h_ref, g_ref, zcq_ref, zckv_ref, dq_ref, dk_ref, dv_ref, dga_ref, dgb_ref, dpool_ref, dnext_ref,
             win_ref, gq_ref, gkv_ref, wuq_ref, wuk_ref, wuv_ref, rope_ref,
             dh_ref, hn_ref, dz_ref, cqn_ref, ckvn_ref, dqb_ref, dkb_ref, dvb_ref, dg_ref, dgq_ref, dgkv_ref):
        i = pl.program_id(0)
        rope_t = rope_ref[...]
        dqb = _rope_bwd(dq_ref[...].astype(F32), *_rope_tables(rope_t, N_HEADS)).astype(BF16)
        dqb_ref[...] = dqb
        xq, rq = _rms(zcq_ref[...])
        gq_v = gq_ref[...]
        cqn_ref[...] = (xq * gq_v).astype(BF16)
        dcq, dgq = _rms_bwd(_dot_nt(dqb, wuq_ref[...]), xq, rq, gq_v)
        dkb = dk_ref[...]
        dvb = dv_ref[...]
        dkb_ref[...] = dkb
        dvb_ref[...] = dvb
        xkv, rkv = _rms(zckv_ref[...])
        gkv_v = gkv_ref[...]
        ckvn_ref[...] = (xkv * gkv_v).astype(BF16)
        dckv, dgkv = _rms_bwd(_dot_nt(dkb, wuk_ref[...]) + _dot_nt(dvb, wuv_ref[...]), xkv, rkv, gkv_v)
        dks = dkb[:, :HEAD_SLOT].astype(F32)
        for hd in range(1, N_HEADS):
            dks = dks + dkb[:, hd * HEAD_SLOT:(hd + 1) * HEAD_SLOT].astype(F32)
        dzk = _rope_bwd(dks, *_rope_tables(rope_t, 1))
        dp_cur = dpool_ref[...]
        dp_ext = jnp.concatenate([dp_cur, dnext_ref[...]], axis=0)
        r = lax.broadcasted_iota(jnp.int32, (tm, tm + POOL_HALO), 0)
        e = lax.broadcasted_iota(jnp.int32, (tm, tm + POOL_HALO), 1)
        gt_col = i * tm + lax.broadcasted_iota(jnp.int32, (1, tm + POOL_HALO), 1)
        pos_col = _seq_pos(gt_col, lp, nb)
        gt_row = i * tm + lax.broadcasted_iota(jnp.int32, (tm + POOL_HALO, 1), 0)
        pos_row = _seq_pos(gt_row, lp, nb)
        dus = []
        for g, w in enumerate(POOL_WINDOWS):
            gs = slice(g * POOL_GROUP, (g + 1) * POOL_GROUP)
            band = jnp.where((e - r >= 0) & (e - r < jnp.minimum(pos_col + 1, w)) & (gt_col < t), 1.0, 0.0).astype(BF16)
            scaled = jnp.where(gt_row < t, dp_ext[:, gs] / jnp.minimum(pos_row + 1, w).astype(F32), 0.0).astype(BF16)
            dus.append(_dot(band, scaled) - dp_cur[:, gs])
        dz = jnp.concatenate(dus + [dcq, dckv, dzk], axis=1).astype(BF16)
        dz = jnp.concatenate([dz, dga_ref[...], dgb_ref[...]], axis=1)
        dz_ref[...] = dz
        xhat, rr = _rms(h_ref[...])
        gg = g_ref[...]
        hn_ref[...] = (xhat * gg).astype(BF16)
        dx, dg = _rms_bwd(_dot_nt(dz, win_ref[...]), xhat, rr, gg)
        dh_ref[...] = dh1_ref[...] + dx

        @pl.when(i == 0)
        def _():
            dg_ref[...] = jnp.zeros_like(dg_ref)
            dgq_ref[...] = jnp.zeros_like(dgq_ref)
            dgkv_ref[...] = jnp.zeros_like(dgkv_ref)

        dg_ref[...] += dg
        dgq_ref[...] += dgq
        dgkv_ref[...] += dgkv

    nxt = pl.BlockSpec((POOL_HALO, POOL_WIDTH), lambda i: (jnp.minimum((i + 1) * hb, last_halo), 0))
    return pl.pallas_call(
        body, name=name, grid=(t // tm,),
        in_specs=[_rows(tm, D_MODEL), _rows(tm, D_MODEL), _whole((1, D_MODEL)), _rows(tm, Q_RANK, Z_CQ // Q_RANK),
                  _rows(tm, KV_RANK, Z_CKV // KV_RANK), _rows(tm, QK_WIDTH), _rows(tm, QK_WIDTH), _rows(tm, D_MODEL),
                  _rows(tm, D_MODEL), _rows(tm, D_MODEL), _rows(tm, POOL_WIDTH), nxt,
                  _whole((D_MODEL, DZ)), _whole((1, Q_RANK)), _whole((1, KV_RANK)), _whole((Q_RANK, QK_WIDTH)),
                  _whole((KV_RANK, QK_WIDTH)), _whole((KV_RANK, D_MODEL)), _rows(tm, 4 * LANES)],
        out_specs=[_rows(tm, D_MODEL), _rows(tm, D_MODEL), _rows(tm, DZ), _rows(tm, Q_RANK), _rows(tm, KV_RANK),
                   _rows(tm, QK_WIDTH), _rows(tm, QK_WIDTH), _rows(tm, D_MODEL),
                   _acc((1, D_MODEL)), _acc((1, Q_RANK)), _acc((1, KV_RANK))],
        out_shape=[jax.ShapeDtypeStruct((t, D_MODEL), F32), jax.ShapeDtypeStruct((t, D_MODEL), BF16),
                   jax.ShapeDtypeStruct((t, DZ), BF16), jax.ShapeDtypeStruct((t, Q_RANK), BF16),
                   jax.ShapeDtypeStruct((t, KV_RANK), BF16), jax.ShapeDtypeStruct((t, QK_WIDTH), BF16),
                   jax.ShapeDtypeStruct((t, QK_WIDTH), BF16), jax.ShapeDtypeStruct((t, D_MODEL), BF16),
                   jax.ShapeDtypeStruct((1, D_MODEL), F32), jax.ShapeDtypeStruct((1, Q_RANK), F32),
                   jax.ShapeDtypeStruct((1, KV_RANK), F32)],
        compiler_params=_cparams(),
    )(dh1, h, g_mix, z, z, dq, dk, dv, dga, dgb, dpool, dpool, win, gq, gkv, wuq, wuk, wuv, rope)


_MESH = pl.DeviceIdType.MESH


def _place():
    x, y, c = lax.axis_index("x"), lax.axis_index("y"), lax.axis_index("c")
    return x, y, c, 4 * x + 2 * y + c


def _peer(x, y, c, k):
    px, py, pc = (1 - x) if k & 4 else x, (1 - y) if k & 2 else y, (1 - c) if k & 1 else c
    return (px, py, pc), 4 * px + 2 * py + pc


ALL_PEERS = tuple(range(1, N_DEV))
CHIP_PEERS = (2, 4, 6)
N_CHIPS = N_DEV // 2


def _sem_scratch(n, m):
    return [pltpu.SemaphoreType.DMA((n, m)), pltpu.SemaphoreType.DMA((n, m)), pltpu.SemaphoreType.DMA((n,))]


class Exchange:
    def __init__(self, arrays, out_shapes, sem_cols, plan, aliased=False):
        self.arrays, self.out_shapes, self.plan = list(arrays), list(out_shapes), plan
        self.scratch = _sem_scratch(len(self.arrays), sem_cols)
        self.aliased = aliased

    def split(self, refs):
        n = len(self.arrays)
        return refs[:n], refs[n:2 * n], refs[2 * n:]

    def start(self, srcs, dsts, sems):
        local, sends, _ = self.plan(srcs, dsts, *sems)
        for cp in local + sends:
            cp.start()

    def wait(self, srcs, dsts, sems):
        local, sends, recvs = self.plan(srcs, dsts, *sems)
        for cp in recvs:
            cp.wait_recv()
        for cp in sends:
            cp.wait_send()
        for cp in local:
            cp.wait()

    def aliases(self, first_in, first_out):
        return {first_in + j: first_out + j for j in range(len(self.arrays))} if self.aliased else {}

    def run(self, name):
        def body(*refs):
            srcs, dsts, sems = self.split(refs)
            self.start(srcs, dsts, sems)
            self.wait(srcs, dsts, sems)

        n = len(self.arrays)
        return pl.pallas_call(body, name=name, in_specs=[_ANY] * n, out_specs=[_ANY] * n, out_shape=self.out_shapes,
                              scratch_shapes=self.scratch, input_output_aliases=self.aliases(0, 0))(*self.arrays)


def exchange(arrays, scatter, peers, by_chip=False):
    slots = N_CHIPS if by_chip else N_DEV

    def plan(srcs, dsts, send_sems, recv_sems, local_sems):
        x, y, c, me = _place()
        mine = 2 * x + y if by_chip else me
        local = [pltpu.make_async_copy(src.at[mine] if scatter else src, dst.at[mine], local_sems.at[j])
                 for j, (src, dst) in enumerate(zip(srcs, dsts))]
        sends, recvs = [], []
        for t, k in enumerate(peers):
            peer, pidx = _peer(x, y, c, k)
            theirs = 2 * peer[0] + peer[1] if by_chip else pidx
            for j, (src, dst) in enumerate(zip(srcs, dsts)):
                part = src.at[theirs] if scatter else src
                sems = dict(send_sem=send_sems.at[j, t], recv_sem=recv_sems.at[j, t], device_id=peer, device_id_type=_MESH)
                sends.append(pltpu.make_async_remote_copy(src_ref=part, dst_ref=dst.at[mine], **sems))
                recvs.append(pltpu.make_async_remote_copy(src_ref=part, dst_ref=dst.at[theirs], **sems))
        return local, sends, recvs

    shapes = [jax.ShapeDtypeStruct(a.shape if scatter else (slots,) + a.shape, a.dtype) for a in arrays]
    return Exchange(arrays, shapes, len(peers), plan)


def second_hop(gathered):
    def plan(srcs, dsts, send_sems, recv_sems, local_sems):
        x, y, c, me = _place()
        sibling, _ = _peer(x, y, c, 1)
        sends, recvs = [], []
        for t, k in enumerate(CHIP_PEERS):
            _, landed = _peer(x, y, c, k)
            _, coming = _peer(x, y, c, k ^ 1)
            for j, buf in enumerate(dsts):
                sems = dict(send_sem=send_sems.at[j, t], recv_sem=recv_sems.at[j, t], device_id=sibling, device_id_type=_MESH)
                sends.append(pltpu.make_async_remote_copy(src_ref=buf.at[landed], dst_ref=buf.at[landed], **sems))
                recvs.append(pltpu.make_async_remote_copy(src_ref=buf.at[coming], dst_ref=buf.at[coming], **sems))
        return [], sends, recvs

    shapes = [jax.ShapeDtypeStruct(a.shape, a.dtype) for a in gathered]
    return Exchange(gathered, shapes, len(CHIP_PEERS), plan, aliased=True)


FIRST_HOP_PEERS = (1,) + CHIP_PEERS


def _gather_two_level(arrays, name):
    n = len(arrays)

    def body(*refs):
        srcs, dsts, (send_sems, recv_sems, local_sems) = refs[:n], refs[n:2 * n], refs[2 * n:]
        x, y, c, me = _place()
        sibling, sidx = _peer(x, y, c, 1)

        def copy(j, sem, block, to, src=None):
            rows = dsts[j].at[block]
            return pltpu.make_async_remote_copy(src_ref=rows if src is None else src, dst_ref=rows, send_sem=send_sems.at[j, sem],
                                                recv_sem=recv_sems.at[j, sem], device_id=to, device_id_type=_MESH)

        local = [pltpu.make_async_copy(srcs[j], dsts[j].at[me], local_sems.at[j]) for j in range(n)]
        for cp in local:
            cp.start()
        first = [copy(j, 1 + t, me, _peer(x, y, c, k)[0], src=srcs[j]) for t, k in enumerate(CHIP_PEERS) for j in range(n)]
        first += [copy(j, 0, me, sibling, src=srcs[j]) for j in range(n)]
        for cp in first:
            cp.start()
        passed = []
        for t, k in enumerate(CHIP_PEERS):
            peer, pidx = _peer(x, y, c, k)
            for j in range(n):
                copy(j, 1 + t, pidx, peer).wait_recv()
                passed.append(copy(j, 4 + t, pidx, sibling))
                passed[-1].start()
        for j in range(n):
            copy(j, 0, sidx, sibling).wait_recv()
        for t, k in enumerate(CHIP_PEERS):
            _, pidx = _peer(x, y, c, k ^ 1)
            for j in range(n):
                copy(j, 4 + t, pidx, sibling).wait_recv()
        for cp in first + passed:
            cp.wait_send()
        for cp in local:
            cp.wait()

    shapes = [jax.ShapeDtypeStruct((N_DEV,) + a.shape, a.dtype) for a in arrays]
    return pl.pallas_call(body, name=name, in_specs=[_ANY] * n, out_specs=[_ANY] * n, out_shape=shapes,
                          scratch_shapes=_sem_scratch(n, 1 + 2 * len(CHIP_PEERS)))(*arrays)


def to_sibling(arrays):
    def plan(srcs, dsts, send_sems, recv_sems, local_sems):
        x, y, c, _ = _place()
        sibling, _ = _peer(x, y, c, 1)
        copies = [pltpu.make_async_remote_copy(src_ref=src.at[1 - c], dst_ref=dst, send_sem=send_sems.at[j, 0],
                                               recv_sem=recv_sems.at[j, 0], device_id=sibling, device_id_type=_MESH)
                  for j, (src, dst) in enumerate(zip(srcs, dsts))]
        return [], copies, copies

    return Exchange(arrays, [jax.ShapeDtypeStruct(a.shape[1:], a.dtype) for a in arrays], 1, plan)


def combine(a, b):
    assert not (a.aliased or b.aliased)
    na, nsem = len(a.arrays), len(a.scratch)

    def plan(srcs, dsts, *sems):
        return tuple(u + v for u, v in zip(a.plan(srcs[:na], dsts[:na], *sems[:nsem]), b.plan(srcs[na:], dsts[na:], *sems[nsem:])))

    both = Exchange(a.arrays + b.arrays, a.out_shapes + b.out_shapes, 1, plan)
    both.scratch = a.scratch + b.scratch
    return both


def pair_add(own, theirs, core, *, name):
    _, ns, r, c = own.shape
    rb = _row_block(r, c // 2)

    def body(core_ref, a_ref, b_ref, o_ref):
        o_ref[...] = (a_ref[...].astype(F32) + b_ref[...].astype(F32)).astype(o_ref.dtype)

    return pl.pallas_call(
        body, name=name,
        grid_spec=pltpu.PrefetchScalarGridSpec(
            num_scalar_prefetch=1, grid=(ns, r // rb),
            in_specs=[pl.BlockSpec((None, None, rb, c), lambda i, j, core_ref: (core_ref[0], i, j, 0)),
                      pl.BlockSpec((None, rb, c), lambda i, j, core_ref: (i, j, 0))],
            out_specs=pl.BlockSpec((None, rb, c), lambda i, j, core_ref: (i, j, 0))),
        out_shape=jax.ShapeDtypeStruct((ns, r, c), own.dtype), compiler_params=_cparams(),
    )(core, own, theirs)


ADAMW_BLOCK_BYTES = 1 << 20


def _row_block(r, c):
    for rb in range(r, 0, -1):
        if r % rb == 0 and (rb % 16 == 0 or rb == r) and rb * c * 4 <= ADAMW_BLOCK_BYTES:
            return rb
    return r


def adamw(w, m, v, parts, *, name):
    depth, r, c = w.shape
    n_parts = parts[0].shape[0]
    rb = _row_block(r, c)

    def body(w_ref, m_ref, v_ref, *refs):
        p_refs, (g_ref, d_ref, nm_ref, nv_ref) = refs[:depth], refs[depth:]

        def total(p_ref):
            g = p_ref[0].astype(F32)
            for j in range(1, n_parts):
                g = g + p_ref[j].astype(F32)
            return g

        g = total(p_refs[0])
        for l in range(1, depth):
            g = jnp.where(pl.program_id(0) == l, total(p_refs[l]), g)
        g_ref[...] = g
        m_new = ADAM_B1 * m_ref[...] + (1.0 - ADAM_B1) * g
        v_new = ADAM_B2 * v_ref[...] + (1.0 - ADAM_B2) * (g * g)
        m_hat = m_new / (1.0 - ADAM_B1 ** ADAM_STEP)
        v_hat = v_new / (1.0 - ADAM_B2 ** ADAM_STEP)
        d_ref[...] = -ADAM_LR * (m_hat / (jnp.sqrt(v_hat) + ADAM_EPS) + ADAM_WD * w_ref[...])
        nm_ref[...] = m_new
        nv_ref[...] = v_new

    wblk = pl.BlockSpec((None, rb, c), lambda l, i: (l, i, 0))
    pblk = pl.BlockSpec((n_parts, rb, c), lambda l, i: (0, i, 0))
    return pl.pallas_call(
        body, name=name, grid=(depth, r // rb),
        in_specs=[wblk, wblk, wblk] + [pblk] * depth, out_specs=[wblk] * 4,
        out_shape=[jax.ShapeDtypeStruct((depth, r, c), F32)] * 4, compiler_params=_cparams(),
    )(w, m, v, *parts)


BIG = (("w_in", 2), ("w_uq", 2), ("w_ukv", 2), ("w_pa", 2), ("w_pb", 1), ("w_o", 1), ("w_gate", 2), ("w_up", 2), ("w_down", 1))
SMALL = ("norm_mix_g", "pool_w", "pool_scale", "q_norm_g", "kv_norm_g", "norm_ffn_g", "final_norm_g")
SMALL_PER_LAYER = SMALL[:-1]
WEIGHTS = ("meta_tokens", "norm_mix_g", "w_in", "pool_w", "pool_scale", "q_norm_g", "kv_norm_g", "w_uq", "w_ukv", "w_pa", "w_pb",
           "w_o", "norm_ffn_g", "w_gate", "w_up", "w_down", "final_norm_g")
HEAD_QK = QK_NOPE + QK_ROPE
KR_END = Z_KR + QK_ROPE


def _cat_cols(parts):
    return [jnp.concatenate(parts, axis=1)]


def _cat_rows(parts):
    return [jnp.concatenate(parts, axis=0)]


def _arr_w_in(parts):
    full = jnp.concatenate(parts, axis=1)
    zc = lambda n: jnp.zeros((full.shape[0], n), full.dtype)
    return [jnp.concatenate([full[:, :Z_KR], zc(QK_NOPE), full[:, Z_KR:KR_END], zc(LANES - HEAD_QK), full[:, KR_END:]], axis=1)]


def _arr_w_uq(parts):
    full = jnp.concatenate(parts, axis=1)
    z = jnp.zeros((full.shape[0], HEAD_SLOT - HEAD_QK), full.dtype)
    pieces = []
    for hd in range(N_HEADS):
        pieces += [full[:, hd * HEAD_QK:(hd + 1) * HEAD_QK], z]
    return [jnp.concatenate(pieces, axis=1)]


def _arr_w_ukv(parts):
    full = jnp.concatenate(parts, axis=1)
    z = jnp.zeros((full.shape[0], HEAD_SLOT - QK_NOPE), full.dtype)
    wide = QK_NOPE + V_DIM
    k, v = [], []
    for hd in range(N_HEADS):
        k += [full[:, hd * wide:hd * wide + QK_NOPE], z]
        v.append(full[:, hd * wide + QK_NOPE:(hd + 1) * wide])
    return [jnp.concatenate(k, axis=1), jnp.concatenate(v, axis=1)]


def arrange(g, fn, out_shapes, name):
    def body(g_ref, *o_refs):
        for o_ref, val in zip(o_refs, fn([g_ref[p] for p in range(N_DEV)])):
            o_ref[...] = val

    return pl.pallas_call(
        body, name=name, grid=(1,),
        in_specs=[pl.BlockSpec(g.shape, lambda i: (0, 0, 0))],
        out_specs=[pl.BlockSpec(s, lambda i: (0, 0)) for s in out_shapes],
        out_shape=[jax.ShapeDtypeStruct(s, g.dtype) for s in out_shapes], compiler_params=_cparams(),
    )(g)


def _arranged_ranges(lo, hi):
    out = []
    for a, b, shift in ((0, Z_KR, 0), (Z_KR, KR_END, QK_NOPE), (KR_END, D_IN, LANES - QK_ROPE)):
        s, e = max(lo, a), min(hi, b)
        if s < e:
            out.append((s + shift, e + shift))
    return out


def _chunks_w_in(acc):
    cs = D_IN // N_DEV
    return [jnp.concatenate([acc[:, a:b] for a, b in _arranged_ranges(p * cs, (p + 1) * cs)], axis=1) for p in range(N_DEV)]


def _chunks_w_uq(acc):
    per = N_HEADS // N_DEV
    return [jnp.concatenate([acc[:, hd * HEAD_SLOT:hd * HEAD_SLOT + HEAD_QK] for hd in range(p * per, (p + 1) * per)], axis=1)
            for p in range(N_DEV)]


def _chunks_w_ukv(acc_k, acc_v):
    per = N_HEADS // N_DEV
    out = []
    for p in range(N_DEV):
        pieces = []
        for hd in range(p * per, (p + 1) * per):
            pieces += [acc_k[:, hd * HEAD_SLOT:hd * HEAD_SLOT + QK_NOPE], acc_v[:, hd * V_DIM:(hd + 1) * V_DIM]]
        out.append(jnp.concatenate(pieces, axis=1))
    return out


def _chunks_cols(acc):
    cs = acc.shape[1] // N_DEV
    return [acc[:, p * cs:(p + 1) * cs] for p in range(N_DEV)]


def _chunks_rows(acc):
    rs = acc.shape[0] // N_DEV
    return [acc[p * rs:(p + 1) * rs, :] for p in range(N_DEV)]


def _chunks_cols_transposed(acc):
    at = acc[...].T
    rs = at.shape[0] // N_DEV
    return [at[p * rs:(p + 1) * rs, :] for p in range(N_DEV)]


def _pack(parts, row_multiple):
    flat = jnp.concatenate([p.reshape(-1) for p in parts])
    return jnp.pad(flat, (0, -flat.shape[0] % (row_multiple * LANES))).reshape(-1, LANES)


def _unpack(packed, shapes):
    flat, out, off = packed.reshape(-1), [], 0
    for s in shapes:
        n = 1
        for d in s:
            n *= d
        out.append(flat[off:off + n].reshape(s))
        off += n
    return out


def _rope_table(lp, nb):
    inv = 1.0 / (ROPE_THETA ** (jnp.arange(0, QK_ROPE, 2, dtype=F32) / QK_ROPE))
    ang = jnp.arange(lp, dtype=F32)[:, None] * inv[None, :]
    cos, sin = jnp.cos(ang), jnp.sin(ang)
    z = lambda n: jnp.zeros((lp, n), F32)
    tail = LANES - QK_NOPE - QK_ROPE
    c = jnp.concatenate([jnp.ones((lp, QK_NOPE), F32), cos, cos, z(tail)], axis=1)
    cr = jnp.concatenate([z(QK_NOPE), cos, cos, z(tail)], axis=1)
    s1 = jnp.concatenate([z(QK_NOPE), -sin, z(HALF_ROPE), z(tail)], axis=1)
    s2 = jnp.concatenate([z(QK_NOPE), z(HALF_ROPE), sin, z(tail)], axis=1)
    return jnp.tile(jnp.concatenate([c, cr, s1, s2], axis=1), (nb, 1))


MIX_IN, MIX_OUT = ("w_in", "w_uq", "w_ukv"), ("w_pa", "w_pb", "w_o")
MIX = MIX_IN + MIX_OUT
FFN = ("w_gate", "w_up", "w_down")
TRANSPOSED = ("w_gate", "w_up")
ARRANGERS = {
    "w_in": (_arr_w_in, (("win", (D_MODEL, DZ)),)), "w_uq": (_arr_w_uq, (("wuq", (Q_RANK, QK_WIDTH)),)),
    "w_ukv": (_arr_w_ukv, (("wuk", (KV_RANK, QK_WIDTH)), ("wuv", (KV_RANK, D_MODEL)))),
    "w_pa": (_cat_cols, (("wpa", (POOL_WIDTH, D_MODEL)),)), "w_pb": (_cat_rows, (("wpb", (D_MODEL, D_MODEL)),)),
    "w_o": (_cat_rows, (("wo", (D_MODEL, D_MODEL)),)), "w_gate": (_cat_rows, (("wgt", (D_FF, D_MODEL)),)),
    "w_up": (_cat_rows, (("wut", (D_FF, D_MODEL)),)), "w_down": (_cat_rows, (("wd", (D_FF, D_MODEL)),)),
}


def _operands(gathered, names, l):
    p = {}
    for n in names:
        fn, outs = ARRANGERS[n]
        if fn is _cat_rows:
            p[outs[0][0]] = gathered[n].reshape(outs[0][1])
            continue
        for (key, _), a in zip(outs, arrange(gathered[n], fn, [s for _, s in outs], f"arrange_{n}_{l}")):
            p[key] = a
    return p


def _small_operands(small, l):
    pw = small["pool_w"][l].astype(BF16)
    return dict(g_mix=small["norm_mix_g"][l][None], gq=small["q_norm_g"][l][None], gkv=small["kv_norm_g"][l][None],
                g_ffn=small["norm_ffn_g"][l][None], ps=small["pool_scale"][l][None], pw=pw)


class MeshComm:
    def __init__(self, w, meta_tokens):
        self.src = lambda n, l: w[n][l].astype(BF16)
        self.meta_tokens = meta_tokens
        self.core = lax.axis_index("c").astype(jnp.int32).reshape(1)
        self.rides = {0: [(n, 0) for n in FFN] + [(n, 1) for n in MIX], 1: [(n, 1) for n in FFN]}

    def first_weights(self):
        got = _gather_two_level([self.src(n, 0) for n in MIX_IN] + [self.meta_tokens], "gather_mix_0")
        return dict(zip(MIX_IN, got)), jnp.moveaxis(got[-1], 0, 1).reshape(N_META, D_MODEL)

    def early_first_hop(self):
        return exchange([self.src(n, 0) for n in MIX_OUT], False, FIRST_HOP_PEERS)

    def early_weights(self, landed):
        return dict(zip(MIX_OUT, second_hop(landed).run("second_hop_mix_0")))

    def first_hop(self, l):
        return exchange([self.src(n, layer) for n, layer in self.rides[l]], False, FIRST_HOP_PEERS)

    def second_hop(self, l, landed):
        return second_hop(landed)

    def carried(self, l, full, names, layer):
        return {n: full[self.rides[l].index((n, layer))] for n in names}

    def pair_exchange(self, own):
        return to_sibling(own)

    def pair_add(self, own, theirs, names, tag):
        return [pair_add(a, b, self.core, name=f"pair_add_{n}_{tag}") for n, a, b in zip(names, own, theirs)]

    def last_pair_exchange(self, own, small):
        got = combine(to_sibling(own), exchange([small], False, ALL_PEERS)).run("pair_grads_mix_0")
        return got[:-1], got[-1]

    def scatter(self, sums):
        return exchange(sums, True, CHIP_PEERS, by_chip=True)

    def scatter_and_gather(self, sums, small):
        return combine(self.scatter(sums), exchange([small], False, ALL_PEERS))

    def scattered_and_gathered(self, brought, small):
        return brought[:-1], brought[-1]

    def scatter_now(self, sums, name):
        return self.scatter(sums).run(name)


HEADS_FWD, HEADS_BWD = 8, 4
TILE_ROWS, TILE_ROWS_BWD = 512, 256


def _tile(t, target):
    n = max(1, -(-t // (target + target // 8)))
    while t % n or (t // n) % 16:
        n += 1
    return t // n


def _wgrad_tile(t):
    return max(tm for tm in (2 * TQ, TQ, LANES) if t % tm == 0)


def _ffn_bwd_part(dh2, p, s, tag, ride):
    d, ff = D_MODEL, D_FF // N_DEV
    t = dh2.shape[0]
    wg_ = lambda n, x, ys, fn, shape: wgrad(x, ys, fn, shape, tm=_wgrad_tile(t), name=f"wgrad_{n}_{tag}")[0]
    (dh1, hn2, act, dgt, dup, dg_ffn), brought = ffn_bwd(dh2, s["h1"], p["g_ffn"], s["gt"], s["up"], p["wgt"], p["wut"], p["wd"],
                                                         tm=_tile(t, TILE_ROWS_BWD), name=f"ffn_bwd_{tag}", ride=ride)
    chunks = [wg_("gate", hn2, [dgt], _chunks_cols_transposed, (ff, d)), wg_("up", hn2, [dup], _chunks_cols_transposed, (ff, d)),
              wg_("down", act, [dh2], _chunks_rows, (ff, d))]
    return dh1, chunks, dict(norm_ffn_g=dg_ffn[0]), brought


EARLY, LATE = ("w_o", "w_pa"), ("w_in", "w_uq", "w_ukv", "w_pb")


def _mix_bwd_part(dh1, p, s, rope, nb, lp, tag, comm, ride, next_ride):
    d = D_MODEL
    t = dh1.shape[0]
    wg_ = lambda n, x, ys, fn, shape: wgrad(x, ys, fn, shape, tm=_wgrad_tile(t), name=f"wgrad_{n}_{tag}")[0]
    (dga, dgb, dpa, dpb, do, dpool, dps, dpw), first = merge_bwd(dh1, s["z"], s["pa"], s["pb"], s["pooled"], p["pw"], p["ps"],
                                                                   p["wpa"], p["wpb"], p["wo"], tm=_tile(t, TILE_ROWS),
                                                                   name=f"merge_bwd_{tag}", ride=ride)
    c_o = wg_("o", s["mg"], [dh1], _chunks_rows, (d // N_DEV, d))
    c_pa = wg_("pa", s["a"], [dpa], _chunks_cols, (POOL_WIDTH, d // N_DEV))
    c_pb, theirs = wgrad(s["o"], [dpb], _chunks_rows, (d // N_DEV, d), tm=_wgrad_tile(t), name=f"wgrad_pb_{tag}",
                         ride=comm.pair_exchange([c_o, c_pa]))
    early = comm.pair_add([c_o, c_pa], theirs, EARLY, f"early_{tag}")
    (dq, dk, dv), brought = attn_bwd(s["q"], s["k"], s["v"], s["o"], do, s["lse"], nb=nb, lp=lp, hb=HEADS_BWD,
                                     name=f"attn_bwd_{tag}", ride=next_ride(first, early))
    dh, hn, dz, cqn, ckvn, dqb, dkb, dvb, dg_mix, dgq, dgkv = in_proj_bwd(
        dh1, s["h"], p["g_mix"], s["z"], dq, dk, dv, dga, dgb, dpool, p["win"], p["gq"], p["gkv"], p["wuq"], p["wuk"], p["wuv"],
        rope, tm=_tile(t, TILE_ROWS_BWD), lp=lp, nb=nb, name=f"in_proj_bwd_{tag}")
    c_in = wg_("in", hn, [dz], _chunks_w_in, (d, D_IN // N_DEV))
    c_uq = wg_("uq", cqn, [dqb], _chunks_w_uq, (Q_RANK, N_HEADS * HEAD_QK // N_DEV))
    c_ukv = wg_("ukv", ckvn, [dkb, dvb], _chunks_w_ukv, (KV_RANK, N_HEADS * (QK_NOPE + V_DIM) // N_DEV))
    small = dict(pool_scale=dps[0], pool_w=dpw, norm_mix_g=dg_mix[0], q_norm_g=dgq[0], kv_norm_g=dgkv[0])
    return dh, [c_in, c_uq, c_ukv, c_pb], small, brought


def train_step(x, loss_target, small, comm):
    nb, seq, d = x.shape
    lp = -(-(N_META + seq) // LANES) * LANES
    t = nb * lp
    assert nb <= 2 and DEPTH == 2
    tm = _tile(t, TILE_ROWS)
    rope = _rope_table(lp, nb)
    gathered, meta = comm.first_weights()
    pad = jnp.zeros((nb, lp - N_META - seq, d), F32)
    h = jnp.concatenate([jnp.broadcast_to(meta[None], (nb, N_META, d)), x, pad], axis=1).reshape(t, d)
    target = jnp.concatenate([jnp.zeros((nb, N_META, d), F32), loss_target, pad], axis=1).reshape(t, d)

    params, saved, full = [], [], {}
    for l in range(DEPTH):
        p = _small_operands(small, l)
        p.update(_operands(gathered, MIX_IN, 0) if l == 0 else _operands(comm.carried(0, full[0], MIX, 1), MIX, 1))
        (z, q, k, v), early = in_proj_fwd(h, p["g_mix"], p["win"], p["gq"], p["gkv"], p["wuq"], p["wuk"], p["wuv"], rope, tm=tm,
                                          name=f"in_proj_fwd_{l}", ride=comm.early_first_hop() if l == 0 else None)
        (o, lse), landed = attn_fwd(q, k, v, nb=nb, lp=lp, hb=HEADS_FWD, name=f"attn_fwd_{l}", ride=comm.first_hop(l))
        if l == 0:
            p.update(_operands(comm.early_weights(early), MIX_OUT, 0))
        (h1, pooled, a, pa, pb, mg), full[l] = merge_fwd(h, z, o, p["pw"], p["ps"], p["wpa"], p["wpb"], p["wo"], tm=tm, lp=lp,
                                                          nb=nb, name=f"merge_fwd_{l}", ride=comm.second_hop(l, landed))
        p.update(_operands(comm.carried(l, full[l], FFN, l), FFN, l))
        h2, gt, up = ffn_fwd(h1, p["g_ffn"], p["wgt"], p["wut"], p["wd"], tm=tm, name=f"ffn_fwd_{l}")
        params.append(p)
        saved.append(dict(h=h, z=z, q=q, k=k, v=v, o=o, lse=lse, h1=h1, pooled=pooled, a=a, pa=pa, pb=pb, mg=mg, gt=gt, up=up))
        h = h2
    parts, dh, dgf = loss_head(h, small["final_norm_g"][None], target, tm=tm, lp=lp, nb=nb, seq=seq, name="loss_head")
    loss = jnp.sum(parts[::8, 0])

    sums = {}
    dh, c_ffn1, small1, _ = _ffn_bwd_part(dh, params[1], saved[1], 1, None)
    dh, c_late1, sm, brought = _mix_bwd_part(
        dh, params[1], saved[1], rope, nb, lp, 1, comm, comm.pair_exchange(c_ffn1),
        lambda theirs, early: comm.scatter(comm.pair_add(c_ffn1, theirs, FFN, "ffn_1") + early))
    small1.update(sm)
    sums.update({(n, 1): a for n, a in zip(FFN + EARLY, brought)})
    dh, c_ffn0, small0, theirs = _ffn_bwd_part(dh, params[0], saved[0], 0, comm.pair_exchange(c_late1))
    s_late1 = comm.pair_add(c_late1, theirs, LATE, "late_1")
    upper = _pack([small1[n] for n in SMALL_PER_LAYER] + [dgf[0]], 8)
    dh, c_late0, sm, brought = _mix_bwd_part(
        dh, params[0], saved[0], rope, nb, lp, 0, comm, comm.pair_exchange(c_ffn0),
        lambda theirs, early: comm.scatter_and_gather(s_late1 + comm.pair_add(c_ffn0, theirs, FFN, "ffn_0") + early, upper))
    small0.update(sm)
    brought, upper_parts = comm.scattered_and_gathered(brought, upper)
    sums.update({(n, l): a for (n, l), a in zip([(n, 1) for n in LATE] + [(n, 0) for n in FFN + EARLY], brought)})
    dh = dh.reshape(nb, lp, d)
    dmeta = jnp.sum(dh[:, :N_META], axis=0)
    meta_chunks = jnp.transpose(dmeta.reshape(N_META, N_CHIPS, 2, d // N_DEV), (2, 1, 0, 3)).astype(BF16)
    small_grads = {n: jnp.stack([small0[n], small1[n]]) for n in small0}
    small_grads["final_norm_g"] = dgf[0]
    last_names = LATE + ("meta_tokens",)
    lower = _pack([small0[n] for n in SMALL_PER_LAYER], 8)
    theirs, lower_parts = comm.last_pair_exchange(c_late0 + [meta_chunks], lower)
    last = comm.scatter_now(comm.pair_add(c_late0 + [meta_chunks], theirs, last_names, "late_0"), "scatter_late_0")
    sums.update({(n, 0): a for n, a in zip(last_names, last)})
    return loss, dh[:, N_META:N_META + seq], sums, small_grads, (lower_parts, upper_parts)


def kernel(x, meta_tokens, norm_mix_g, w_in, pool_w, pool_scale, q_norm_g, kv_norm_g, w_uq, w_ukv, w_pa, w_pb, w_o, norm_ffn_g, w_gate, w_up, w_down, final_norm_g, loss_target, m_meta_tokens, m_norm_mix_g, m_w_in, m_pool_w, m_pool_scale, m_q_norm_g, m_kv_norm_g, m_w_uq, m_w_ukv, m_w_pa, m_w_pb, m_w_o, m_norm_ffn_g, m_w_gate, m_w_up, m_w_down, m_final_norm_g, v_meta_tokens, v_norm_mix_g, v_w_in, v_pool_w, v_pool_scale, v_q_norm_g, v_kv_norm_g, v_w_uq, v_w_ukv, v_w_pa, v_w_pb, v_w_o, v_norm_ffn_g, v_w_gate, v_w_up, v_w_down, v_final_norm_g):
    args = dict(locals())
    w = {n: args[n] for n in WEIGHTS}
    m = {n: args["m_" + n] for n in WEIGHTS}
    v = {n: args["v_" + n] for n in WEIGHTS}
    small = {n: w[n] for n in SMALL}
    as_handled = lambda a, n: jnp.swapaxes(a, 1, 2) if n in TRANSPOSED else a
    wh, mh, vh = ({n: as_handled(d[n], n) for n, _ in BIG} for d in (w, m, v))

    loss, grad_x, sums, _, small_recv = train_step(x, loss_target, small, MeshComm(wh, meta_tokens))
    loss = lax.psum(loss, ("x", "y", "c"))

    out = {n: [as_handled(a, n) for a in adamw(wh[n], mh[n], vh[n], [sums[(n, l)] for l in range(DEPTH)], name=f"adamw_{n}")]
           for n, _ in BIG}
    out["meta_tokens"] = [a[0] for a in adamw(meta_tokens[None], m["meta_tokens"][None], v["meta_tokens"][None],
                                              [sums[("meta_tokens", 0)]], name="adamw_meta_tokens")]
    lower_parts, upper_parts = small_recv
    pk_lower = lambda d: _pack([d[n][0] for n in SMALL_PER_LAYER], 8)[None]
    pk_upper = lambda d: _pack([d[n][1] for n in SMALL_PER_LAYER] + [d["final_norm_g"]], 8)[None]
    lower = adamw(pk_lower(w), pk_lower(m), pk_lower(v), [lower_parts], name="adamw_small_lower")
    upper = adamw(pk_upper(w), pk_upper(m), pk_upper(v), [upper_parts], name="adamw_small_upper")
    shapes = [w[n].shape[1:] for n in SMALL_PER_LAYER]
    for n in SMALL:
        out[n] = [None] * 4
    for kind in range(4):
        lo = _unpack(lower[kind][0], shapes)
        up = _unpack(upper[kind][0], shapes + [w["final_norm_g"].shape])
        for j, n in enumerate(SMALL_PER_LAYER):
            out[n][kind] = jnp.stack([lo[j], up[j]])
        out["final_norm_g"][kind] = up[-1]
    return (loss, grad_x, *[out[n][kind] for kind in range(4) for n in WEIGHTS])
```

```python
import functools
import math

import jax
import jax.numpy as jnp
from jax import lax
from jax.experimental import pallas as pl
from jax.experimental.pallas import tpu as pltpu

F32, BF16 = jnp.float32, jnp.bfloat16

D_MODEL = 1024
N_META = 16
N_HEADS = 16
QK_NOPE, QK_ROPE, V_DIM = 64, 32, 64
HALF_ROPE = QK_ROPE // 2
Q_RANK, KV_RANK = 256, 128
POOL_WINDOWS = (2, 4, 8, 16)
POOL_GROUP = 128
POOL_WIDTH = POOL_GROUP * len(POOL_WINDOWS)
POOL_HALO = 16
D_FF = 2816
D_IN = 2976
NORM_EPS = 1e-6
SM_SCALE = (QK_NOPE + QK_ROPE) ** -0.5
LOG2E = math.log2(math.e)
EXP2_SCALE = SM_SCALE * LOG2E
MASK_VALUE = -1e30
ROPE_THETA = 10000.0
DEPTH = 2
N_DEV = 8

ADAM_LR, ADAM_B1, ADAM_B2, ADAM_EPS, ADAM_WD, ADAM_STEP = 0.001, 0.9, 0.999, 1e-08, 0.01, 10

LANES = 128
HEAD_SLOT = LANES
QK_WIDTH = N_HEADS * HEAD_SLOT
Z_CQ, Z_CKV, Z_KR, Z_GA, Z_GB, DZ = 512, 768, 896, 1024, 2048, 3072
TQ = TK = 256
VMEM_LIMIT = 56 * 1024 * 1024


def _cparams():
    return pltpu.CompilerParams(vmem_limit_bytes=VMEM_LIMIT)


def _rows(tm, width, col=0):
    return pl.BlockSpec((tm, width), lambda i: (i, col))


def _whole(shape):
    zeros = (0,) * len(shape)
    return pl.BlockSpec(shape, lambda i: zeros, pipeline_mode=pl.Buffered(1))


def _acc(shape):
    zeros = (0,) * len(shape)
    return pl.BlockSpec(shape, lambda i: zeros)


def _dot(a, b):
    return jnp.dot(a, b, preferred_element_type=F32)


def _dot_tn(a, b):
    return lax.dot_general(a, b, (((0,), (0,)), ((), ())), preferred_element_type=F32)


def _dot_nt(a, b):
    return lax.dot_general(a, b, (((1,), (1,)), ((), ())), preferred_element_type=F32)


def _rms(x):
    r = lax.rsqrt(jnp.mean(x * x, axis=-1, keepdims=True) + NORM_EPS)
    return x * r, r


def _rms_bwd(dy, xhat, r, g):
    dg = jnp.sum(dy * xhat, axis=0, keepdims=True)
    dxh = dy * g
    dx = r * (dxh - xhat * jnp.mean(dxh * xhat, axis=-1, keepdims=True))
    return dx, dg


def _sigmoid(x):
    return 1.0 / (1.0 + jnp.exp(-x))


def _rope_fwd(q, c, s1, s2):
    w = q.shape[1]
    return q * c + pltpu.roll(q, w - HALF_ROPE, 1) * s1 + pltpu.roll(q, HALF_ROPE, 1) * s2


def _rope_bwd(dq, c, s1, s2):
    w = dq.shape[1]
    return dq * c + pltpu.roll(dq * s1, HALF_ROPE, 1) + pltpu.roll(dq * s2, w - HALF_ROPE, 1)


def _rope_tables(rope, reps):
    c, cr, s1, s2 = (rope[:, k * LANES:(k + 1) * LANES] for k in range(4))
    if reps > 1:
        return jnp.tile(c, (1, reps)), jnp.tile(s1, (1, reps)), jnp.tile(s2, (1, reps))
    return cr, s1, s2


def _seq_pos(gi, lp, nb):
    pos = gi
    for b in range(1, nb):
        pos = jnp.where(gi >= b * lp, gi - b * lp, pos)
    return pos


_ANY = pl.BlockSpec(memory_space=pl.ANY)


def _carrying_call(body, ride, operands, *, name, grid, in_specs, out_specs, out_shape, scratch_shapes=()):
    n_in, n_out = len(in_specs), len(out_specs)
    if ride is None:
        out = pl.pallas_call(body, name=name, grid=grid, in_specs=in_specs, out_specs=out_specs, out_shape=out_shape,
                             scratch_shapes=list(scratch_shapes), compiler_params=_cparams())(*operands)
        return out, []
    ne = len(ride.arrays)

    def carrying(*refs):
        ins, r_in, rest = refs[:n_in], refs[n_in:n_in + ne], refs[n_in + ne:]
        outs, r_out, rest = rest[:n_out], rest[n_out:n_out + ne], rest[n_out + ne:]
        scratch, sems = rest[:len(scratch_shapes)], rest[len(scratch_shapes):]
        ids = [pl.program_id(a) for a in range(len(grid))]
        first = functools.reduce(jnp.logical_and, [i == 0 for i in ids])
        last = functools.reduce(jnp.logical_and, [i == g - 1 for i, g in zip(ids, grid)])

        @pl.when(first)
        def _():
            ride.start(r_in, r_out, sems)

        body(*ins, *outs, *scratch)

        @pl.when(last)
        def _():
            ride.wait(r_in, r_out, sems)

    out = pl.pallas_call(
        carrying, name=name, grid=grid, in_specs=list(in_specs) + [_ANY] * ne, out_specs=list(out_specs) + [_ANY] * ne,
        out_shape=list(out_shape) + ride.out_shapes, scratch_shapes=list(scratch_shapes) + ride.scratch,
        input_output_aliases=ride.aliases(n_in, n_out), compiler_params=_cparams(),
    )(*operands, *ride.arrays)
    return out[:n_out], out[n_out:]


def in_proj_fwd(h, g_mix, win, gq, gkv, wuq, wuk, wuv, rope, *, tm, name, ride=None):
    t = h.shape[0]

    def body(h_ref, g_ref, win_ref, gq_ref, gkv_ref, wuq_ref, wuk_ref, wuv_ref, rope_ref, z_ref, q_ref, k_ref, v_ref):
        xhat, _ = _rms(h_ref[...])
        hn = (xhat * g_ref[...]).astype(BF16)
        z = _dot(hn, win_ref[...])
        z_ref[...] = z
        rope_t = rope_ref[...]
        xq, _ = _rms(z[:, Z_CQ:Z_CKV])
        cqn = (xq * gq_ref[...]).astype(BF16)
        q = _rope_fwd(_dot(cqn, wuq_ref[...]), *_rope_tables(rope_t, N_HEADS))
        q_ref[...] = q.astype(BF16)
        xkv, _ = _rms(z[:, Z_CKV:Z_KR])
        ckvn = (xkv * gkv_ref[...]).astype(BF16)
        kr = _rope_fwd(z[:, Z_KR:Z_GA], *_rope_tables(rope_t, 1))
        k_ref[...] = (_dot(ckvn, wuk_ref[...]) + jnp.tile(kr, (1, N_HEADS))).astype(BF16)
        v_ref[...] = _dot(ckvn, wuv_ref[...]).astype(BF16)

    return _carrying_call(
        body, ride, (h, g_mix, win, gq, gkv, wuq, wuk, wuv, rope), name=name, grid=(t // tm,),
        in_specs=[_rows(tm, D_MODEL), _whole((1, D_MODEL)), _whole((D_MODEL, DZ)), _whole((1, Q_RANK)), _whole((1, KV_RANK)),
                  _whole((Q_RANK, QK_WIDTH)), _whole((KV_RANK, QK_WIDTH)), _whole((KV_RANK, D_MODEL)), _rows(tm, 4 * LANES)],
        out_specs=[_rows(tm, DZ), _rows(tm, QK_WIDTH), _rows(tm, QK_WIDTH), _rows(tm, D_MODEL)],
        out_shape=[jax.ShapeDtypeStruct((t, DZ), F32), jax.ShapeDtypeStruct((t, QK_WIDTH), BF16),
                   jax.ShapeDtypeStruct((t, QK_WIDTH), BF16), jax.ShapeDtypeStruct((t, D_MODEL), BF16)])


def attn_fwd(q, k, v, *, nb, lp, hb, name, ride=None):
    t = q.shape[0]
    nq, tail = lp // TQ, lp % TQ
    assert tail % LANES == 0

    def body(q_ref, k_ref, v_ref, o_ref, lse_ref, vt):
        for pr in range(hb // 2):
            vt[pr] = v_ref[:, pr * LANES:(pr + 1) * LANES].T

        def q_block(qs, tq, whole_k):
            qh = [q_ref[pl.ds(qs, tq), hd * HEAD_SLOT:(hd + 1) * HEAD_SLOT] for hd in range(hb)]
            keep = lax.broadcasted_iota(jnp.int32, (tq, tq), 0) <= lax.broadcasted_iota(jnp.int32, (tq, tq), 1)

            def k_steps(blocks, c, masked):
                sts = [[_dot_nt(k_ref[pl.ds(ks, tk), hd * HEAD_SLOT:(hd + 1) * HEAD_SLOT], qh[hd]) for hd in range(hb)]
                       for ks, tk in blocks]
                for (ks, tk), st_b in zip(blocks, sts):
                    ps, stats = [], []
                    for hd in range(hb):
                        m, l, _ = c[hd]
                        st = jnp.where(keep, st_b[hd], MASK_VALUE) if masked else st_b[hd]
                        m_new = jnp.maximum(m, jnp.max(st, axis=0, keepdims=True))
                        p = jnp.exp2((st - m_new) * EXP2_SCALE)
                        alpha = jnp.exp2((m - m_new) * EXP2_SCALE)
                        ps.append(p.astype(BF16))
                        stats.append((m_new, alpha * l + jnp.sum(p, axis=0, keepdims=True), alpha))
                    pvs = [_dot(vt[hd // 2, :, pl.ds(ks, tk)], ps[hd]) for hd in range(hb)]
                    c = tuple((stats[hd][0], stats[hd][1], stats[hd][2] * c[hd][2] + pvs[hd]) for hd in range(hb))
                return c

            def two_blocks(i, c):
                ks = pl.multiple_of(2 * i * TK, TK)
                return k_steps([(ks, TK), (ks + TK, TK)], c, False)

            init = tuple((jnp.full((1, tq), MASK_VALUE, F32), jnp.zeros((1, tq), F32), jnp.zeros((LANES, tq), F32))
                         for _ in range(hb))
            pairs = lax.div(whole_k, 2)
            c = lax.fori_loop(0, pairs, two_blocks, init)
            c = lax.fori_loop(2 * pairs, whole_k, lambda kj, c: k_steps([(pl.multiple_of(kj * TK, TK), TK)], c, False), c)
            c = k_steps([(qs, tq)], c, True)
            sub = lax.broadcasted_iota(jnp.int32, (LANES, tq), 0)
            for pr in range(hb // 2):
                (m0, l0, a0), (m1, l1, a1) = c[2 * pr], c[2 * pr + 1]
                o_ref[pl.ds(qs, tq), pr * LANES:(pr + 1) * LANES] = jnp.where(sub < V_DIM, a0 / l0, a1 / l1).T.astype(BF16)
                lse_ref[2 * pr, :, pl.ds(qs, tq)] = m0 * SM_SCALE + jnp.log(l0)
                lse_ref[2 * pr + 1, :, pl.ds(qs, tq)] = m1 * SM_SCALE + jnp.log(l1)

        def whole_q_block(qi, carry):
            q_block(pl.multiple_of(qi * TQ, TQ), TQ, qi)
            return carry

        lax.fori_loop(0, nq, whole_q_block, 0)
        if tail:
            q_block(nq * TQ, tail, nq)

    blk = lambda w: pl.BlockSpec((lp, w), lambda b, g: (b, g))
    return _carrying_call(
        body, ride, (q, k, v), name=name, grid=(nb, N_HEADS // hb),
        in_specs=[blk(hb * HEAD_SLOT), blk(hb * HEAD_SLOT), blk(hb * V_DIM)],
        out_specs=[blk(hb * V_DIM), pl.BlockSpec((hb, 1, lp), lambda b, g: (g, 0, b))],
        out_shape=[jax.ShapeDtypeStruct((t, D_MODEL), BF16), jax.ShapeDtypeStruct((N_HEADS, 1, t), F32)],
        scratch_shapes=[pltpu.VMEM((hb // 2, LANES, lp), BF16)])


def _pool_band_fwd(i, tm, lp, nb):
    r = lax.broadcasted_iota(jnp.int32, (tm, POOL_HALO + tm), 0)
    e = lax.broadcasted_iota(jnp.int32, (tm, POOL_HALO + tm), 1)
    diff = r + POOL_HALO - e
    pos = _seq_pos(i * tm + lax.broadcasted_iota(jnp.int32, (tm, 1), 0), lp, nb)
    out = []
    for w in POOL_WINDOWS:
        cnt = jnp.minimum(pos + 1, w)
        band = jnp.where((diff >= 0) & (diff < cnt), 1.0, 0.0).astype(BF16)
        out.append((band, cnt.astype(F32)))
    return out


def merge_fwd(h, z, o, pw, ps, wpa, wpb, wo, *, tm, lp, nb, name, ride=None):
    t = h.shape[0]
    hb = tm // POOL_HALO

    def body(h_ref, u_ref, uprev_ref, ga_ref, gb_ref, o_ref, pw_ref, ps_ref, wpa_ref, wpb_ref, wo_ref,
             h1_ref, pooled_ref, a_ref, pa_ref, pb_ref, mg_ref):
        i = pl.program_id(0)
        u = u_ref[...]
        uext = jnp.concatenate([uprev_ref[...], u], axis=0).astype(BF16)
        pooled, ys = [], []
        for g, (band, cnt) in enumerate(_pool_band_fwd(i, tm, lp, nb)):
            gs = slice(g * POOL_GROUP, (g + 1) * POOL_GROUP)
            pg = (_dot(band, uext[:, gs]) / cnt - u[:, gs]).astype(BF16)
            pooled.append(pg)
            ys.append(_dot(pg, pw_ref[g]))
        pooled_ref[...] = jnp.concatenate(pooled, axis=1)
        a = (jnp.concatenate(ys, axis=1) * ps_ref[...]).astype(BF16)
        a_ref[...] = a
        pa = _dot(a, wpa_ref[...])
        pb = _dot(o_ref[...], wpb_ref[...])
        pa_ref[...] = pa.astype(BF16)
        pb_ref[...] = pb.astype(BF16)
        mg = (_sigmoid(ga_ref[...]) * pa + _sigmoid(gb_ref[...]) * pb).astype(BF16)
        mg_ref[...] = mg
        h1_ref[...] = h_ref[...] + _dot(mg, wo_ref[...])

    halo = pl.BlockSpec((POOL_HALO, POOL_WIDTH), lambda i: (jnp.maximum(i * hb - 1, 0), 0))
    return _carrying_call(
        body, ride, (h, z, z, z, z, o, pw, ps, wpa, wpb, wo), name=name, grid=(t // tm,),
        in_specs=[_rows(tm, D_MODEL), _rows(tm, POOL_WIDTH), halo, _rows(tm, D_MODEL, 1), _rows(tm, D_MODEL, 2), _rows(tm, D_MODEL),
                  _whole((4, POOL_GROUP, POOL_GROUP)), _whole((1, POOL_WIDTH)), _whole((POOL_WIDTH, D_MODEL)),
                  _whole((D_MODEL, D_MODEL)), _whole((D_MODEL, D_MODEL))],
        out_specs=[_rows(tm, D_MODEL), _rows(tm, POOL_WIDTH), _rows(tm, POOL_WIDTH), _rows(tm, D_MODEL), _rows(tm, D_MODEL),
                   _rows(tm, D_MODEL)],
        out_shape=[jax.ShapeDtypeStruct((t, D_MODEL), F32), jax.ShapeDtypeStruct((t, POOL_WIDTH), BF16),
                   jax.ShapeDtypeStruct((t, POOL_WIDTH), BF16), jax.ShapeDtypeStruct((t, D_MODEL), BF16),
                   jax.ShapeDtypeStruct((t, D_MODEL), BF16), jax.ShapeDtypeStruct((t, D_MODEL), BF16)])


def ffn_fwd(h1, g, wgt, wut, wd, *, tm, name):
    t = h1.shape[0]

    def body(h_ref, g_ref, wgt_ref, wut_ref, wd_ref, h2_ref, gt_ref, up_ref):
        h = h_ref[...]
        xhat, _ = _rms(h)
        hn = (xhat * g_ref[...]).astype(BF16)
        gt = _dot_nt(hn, wgt_ref[...])
        up = _dot_nt(hn, wut_ref[...])
        gt_ref[...] = gt.astype(BF16)
        up_ref[...] = up.astype(BF16)
        act = (gt * _sigmoid(gt) * up).astype(BF16)
        h2_ref[...] = h + _dot(act, wd_ref[...])

    return pl.pallas_call(
        body, name=name, grid=(t // tm,),
        in_specs=[_rows(tm, D_MODEL), _whole((1, D_MODEL)), _whole((D_FF, D_MODEL)), _whole((D_FF, D_MODEL)), _whole((D_FF, D_MODEL))],
        out_specs=[_rows(tm, D_MODEL), _rows(tm, D_FF), _rows(tm, D_FF)],
        out_shape=[jax.ShapeDtypeStruct((t, D_MODEL), F32), jax.ShapeDtypeStruct((t, D_FF), BF16), jax.ShapeDtypeStruct((t, D_FF), BF16)],
        compiler_params=_cparams(),
    )(h1, g, wgt, wut, wd)


def loss_head(h, g, target, *, tm, lp, nb, seq, name):
    t = h.shape[0]
    nt = t // tm

    def body(h_ref, g_ref, t_ref, loss_ref, dh_ref, dg_ref):
        i = pl.program_id(0)
        pos = _seq_pos(i * tm + lax.broadcasted_iota(jnp.int32, (tm, 1), 0), lp, nb)
        real = (pos >= N_META) & (pos < N_META + seq)
        xhat, r = _rms(h_ref[...])
        gg = g_ref[...]
        err = jnp.where(real, xhat * gg - t_ref[...], 0.0)
        loss_ref[...] = jnp.full((8, LANES), 0.5 * jnp.sum(err * err) / D_MODEL, F32)
        dx, dg = _rms_bwd(err * (1.0 / D_MODEL), xhat, r, gg)
        dh_ref[...] = dx

        @pl.when(i == 0)
        def _():
            dg_ref[...] = jnp.zeros_like(dg_ref)

        dg_ref[...] += dg

    return pl.pallas_call(
        body, name=name, grid=(nt,),
        in_specs=[_rows(tm, D_MODEL), _whole((1, D_MODEL)), _rows(tm, D_MODEL)],
        out_specs=[pl.BlockSpec((8, LANES), lambda i: (i, 0)), _rows(tm, D_MODEL), _acc((1, D_MODEL))],
        out_shape=[jax.ShapeDtypeStruct((nt * 8, LANES), F32), jax.ShapeDtypeStruct((t, D_MODEL), F32),
                   jax.ShapeDtypeStruct((1, D_MODEL), F32)],
        compiler_params=_cparams(),
    )(h, g, target)


def wgrad(x, ys, chunk_fn, chunk_shape, *, tm, name, ride=None):
    t, m = x.shape
    tiles = t // tm
    steps = -(-tiles // 2)

    def body(*refs):
        ins, o_ref, accs = refs[:2 * (1 + len(ys))], refs[2 * (1 + len(ys))], refs[2 * (1 + len(ys)) + 1:]
        i = pl.program_id(0)

        @pl.when(i == 0)
        def _():
            for acc in accs:
                acc[...] = jnp.zeros_like(acc)

        def both(first, second, mask):
            b = second[...].astype(BF16)
            if mask and tiles % 2:
                b = jnp.where(2 * i + 1 < tiles, b, jnp.zeros_like(b))
            return jnp.concatenate([first[...].astype(BF16), b], axis=0)

        xb = both(ins[0], ins[1], True)
        for j, acc in enumerate(accs):
            acc[...] += _dot_tn(xb, both(ins[2 + 2 * j], ins[3 + 2 * j], False))

        @pl.when(i == steps - 1)
        def _():
            for p, chunk in enumerate(chunk_fn(*accs)):
                o_ref[p % 2, p // 2] = chunk.astype(BF16)

    def two_tiles(width):
        return [pl.BlockSpec((tm, width), lambda i: (2 * i, 0)),
                pl.BlockSpec((tm, width), lambda i: (jnp.minimum(2 * i + 1, tiles - 1), 0))]

    out = (2, N_DEV // 2) + tuple(chunk_shape)
    operands = [x, x] + [a for y in ys for a in (y, y)]
    (chunks,), brought = _carrying_call(
        body, ride, operands, name=name, grid=(steps,),
        in_specs=two_tiles(m) + [s for y in ys for s in two_tiles(y.shape[1])], out_specs=[_acc(out)],
        out_shape=[jax.ShapeDtypeStruct(out, BF16)], scratch_shapes=[pltpu.VMEM((m, y.shape[1]), F32) for y in ys])
    return chunks, brought


def ffn_bwd(dh2, h1, g, gt, up, wgt, wut, wd, *, tm, name, ride=None):
    t = h1.shape[0]

    def body(dh2_ref, h_ref, g_ref, gt_ref, up_ref, wgt_ref, wut_ref, wd_ref, dh1_ref, hn_ref, act_ref, dgt_ref, dup_ref, dg_ref):
        dh2 = dh2_ref[...]
        dact = _dot_nt(dh2.astype(BF16), wd_ref[...])
        gt = gt_ref[...].astype(F32)
        up = up_ref[...].astype(F32)
        sg = _sigmoid(gt)
        silu = gt * sg
        act_ref[...] = (silu * up).astype(BF16)
        dgt = (dact * up * (sg * (1.0 + gt * (1.0 - sg)))).astype(BF16)
        dup = (dact * silu).astype(BF16)
        dgt_ref[...] = dgt
        dup_ref[...] = dup
        dhn = _dot(dgt, wgt_ref[...]) + _dot(dup, wut_ref[...])
        xhat, r = _rms(h_ref[...])
        gg = g_ref[...]
        hn_ref[...] = (xhat * gg).astype(BF16)
        dx, dg = _rms_bwd(dhn, xhat, r, gg)
        dh1_ref[...] = dh2 + dx

        @pl.when(pl.program_id(0) == 0)
        def _():
            dg_ref[...] = jnp.zeros_like(dg_ref)

        dg_ref[...] += dg

    return _carrying_call(
        body, ride, (dh2, h1, g, gt, up, wgt, wut, wd), name=name, grid=(t // tm,),
        in_specs=[_rows(tm, D_MODEL), _rows(tm, D_MODEL), _whole((1, D_MODEL)), _rows(tm, D_FF), _rows(tm, D_FF),
                  _whole((D_FF, D_MODEL)), _whole((D_FF, D_MODEL)), _whole((D_FF, D_MODEL))],
        out_specs=[_rows(tm, D_MODEL), _rows(tm, D_MODEL), _rows(tm, D_FF), _rows(tm, D_FF), _rows(tm, D_FF), _acc((1, D_MODEL))],
        out_shape=[jax.ShapeDtypeStruct((t, D_MODEL), F32), jax.ShapeDtypeStruct((t, D_MODEL), BF16),
                   jax.ShapeDtypeStruct((t, D_FF), BF16), jax.ShapeDtypeStruct((t, D_FF), BF16),
                   jax.ShapeDtypeStruct((t, D_FF), BF16), jax.ShapeDtypeStruct((1, D_MODEL), F32)])


def merge_bwd(dh1, z, pa, pb, pooled, pw, ps, wpa, wpb, wo, *, tm, name, ride=None):
    t = dh1.shape[0]

    def body(dh1_ref, ga_ref, gb_ref, pa_ref, pb_ref, pooled_ref, pw_ref, ps_ref, wpa_ref, wpb_ref, wo_ref,
             dga_ref, dgb_ref, dpa_ref, dpb_ref, do_ref, dpool_ref, dps_ref, dpw_ref):
        dmg = _dot_nt(dh1_ref[...].astype(BF16), wo_ref[...])
        sa = _sigmoid(ga_ref[...])
        sb = _sigmoid(gb_ref[...])
        dga_ref[...] = (dmg * pa_ref[...].astype(F32) * sa * (1.0 - sa)).astype(BF16)
        dgb_ref[...] = (dmg * pb_ref[...].astype(F32) * sb * (1.0 - sb)).astype(BF16)
        dpa = (dmg * sa).astype(BF16)
        dpb = (dmg * sb).astype(BF16)
        dpa_ref[...] = dpa
        dpb_ref[...] = dpb
        do_ref[...] = _dot_nt(dpb, wpb_ref[...]).astype(BF16)
        da = _dot_nt(dpa, wpa_ref[...])
        pooled = pooled_ref[...]
        ps = ps_ref[...]

        @pl.when(pl.program_id(0) == 0)
        def _():
            dps_ref[...] = jnp.zeros_like(dps_ref)
            dpw_ref[...] = jnp.zeros_like(dpw_ref)

        dps, dpool = [], []
        for g in range(len(POOL_WINDOWS)):
            gs = slice(g * POOL_GROUP, (g + 1) * POOL_GROUP)
            y = _dot(pooled[:, gs], pw_ref[g])
            dps.append(jnp.sum(da[:, gs] * y, axis=0, keepdims=True))
            dy = (da[:, gs] * ps[:, gs]).astype(BF16)
            dpool.append(_dot_nt(dy, pw_ref[g]))
            dpw_ref[g] += _dot_tn(pooled[:, gs], dy)
        dps_ref[...] += jnp.concatenate(dps, axis=1)
        dpool_ref[...] = jnp.concatenate(dpool, axis=1)

    return _carrying_call(
        body, ride, (dh1, z, z, pa, pb, pooled, pw, ps, wpa, wpb, wo), name=name, grid=(t // tm,),
        in_specs=[_rows(tm, D_MODEL), _rows(tm, D_MODEL, 1), _rows(tm, D_MODEL, 2), _rows(tm, D_MODEL), _rows(tm, D_MODEL),
                  _rows(tm, POOL_WIDTH), _whole((4, POOL_GROUP, POOL_GROUP)),
                  _whole((1, POOL_WIDTH)), _whole((POOL_WIDTH, D_MODEL)), _whole((D_MODEL, D_MODEL)), _whole((D_MODEL, D_MODEL))],
        out_specs=[_rows(tm, D_MODEL), _rows(tm, D_MODEL), _rows(tm, D_MODEL), _rows(tm, D_MODEL), _rows(tm, D_MODEL),
                   _rows(tm, POOL_WIDTH), _acc((1, POOL_WIDTH)), _acc((4, POOL_GROUP, POOL_GROUP))],
        out_shape=[jax.ShapeDtypeStruct((t, D_MODEL), BF16)] * 5
        + [jax.ShapeDtypeStruct((t, POOL_WIDTH), F32), jax.ShapeDtypeStruct((1, POOL_WIDTH), F32),
           jax.ShapeDtypeStruct((4, POOL_GROUP, POOL_GROUP), F32)])


def attn_bwd(q, k, v, o, do, lse, *, nb, lp, hb, name, ride=None):
    t = q.shape[0]
    nq, tail = lp // TQ, lp % TQ
    assert tail % LANES == 0

    def body(q_ref, k_ref, v_ref, o_ref, do_ref, lse_ref, dq_ref, dk_ref, dv_ref, kt, doh, lse_row, delta_row, dqt):
        lane = lax.broadcasted_iota(jnp.int32, (lp, LANES), 1)
        first = lane < V_DIM
        sub = lax.broadcasted_iota(jnp.int32, (LANES, lp), 0)
        for pr in range(hb // 2):
            ls = slice(pr * LANES, (pr + 1) * LANES)
            do = do_ref[:, ls]
            doh[2 * pr] = jnp.where(first, do, jnp.zeros_like(do))
            doh[2 * pr + 1] = jnp.where(first, jnp.zeros_like(do), do)
            prod_t = (do.astype(F32) * o_ref[:, ls].astype(F32)).T
            delta_row[2 * pr] = jnp.sum(jnp.where(sub < V_DIM, prod_t, 0.0), axis=0, keepdims=True)
            delta_row[2 * pr + 1] = jnp.sum(jnp.where(sub < V_DIM, 0.0, prod_t), axis=0, keepdims=True)
        for hd in range(hb):
            lse_row[hd] = lse_ref[hd] * LOG2E
            kt[hd] = k_ref[:, hd * HEAD_SLOT:(hd + 1) * HEAD_SLOT].T
        dqt[...] = jnp.zeros(dqt.shape, F32)
        heads = range(hb)
        hss = [slice(hd * HEAD_SLOT, (hd + 1) * HEAD_SLOT) for hd in heads]

        def k_block(ks, tk, next_q):
            keep = lax.broadcasted_iota(jnp.int32, (tk, tk), 0) <= lax.broadcasted_iota(jnp.int32, (tk, tk), 1)

            def q_steps(blocks, c, masked):
                work = [(qs, tq, hd) for qs, tq in blocks for hd in heads]
                qhs = [q_ref[pl.ds(qs, tq), hss[hd]] for qs, tq, hd in work]
                dos = [doh[hd, pl.ds(qs, tq), :] for qs, tq, hd in work]
                sts = [_dot_nt(k_ref[pl.ds(ks, tk), hss[hd]], qhs[i]) for i, (_, _, hd) in enumerate(work)]
                dpts = [_dot_nt(v_ref[pl.ds(ks, tk), (hd // 2) * LANES:(hd // 2 + 1) * LANES], dos[i])
                        for i, (_, _, hd) in enumerate(work)]
                pts, dsts = [], []
                for i, (qs, tq, hd) in enumerate(work):
                    st = jnp.where(keep, sts[i], MASK_VALUE) if masked else sts[i]
                    pt = jnp.exp2(st * EXP2_SCALE - lse_row[hd, :, pl.ds(qs, tq)])
                    dsts.append((pt * (dpts[i] - delta_row[hd, :, pl.ds(qs, tq)])).astype(BF16))
                    pts.append(pt.astype(BF16))
                dvs = [_dot(pts[i], dos[i]) for i in range(len(work))]
                dks = [_dot(dsts[i], qhs[i]) for i in range(len(work))]
                dqs = [_dot(kt[hd, :, pl.ds(ks, tk)], dsts[i]) for i, (_, _, hd) in enumerate(work)]
                c = list(c)
                for i, (qs, tq, hd) in enumerate(work):
                    dqt[hd, :, pl.ds(qs, tq)] += dqs[i]
                    c[hd] = (c[hd][0] + dks[i], c[hd][1] + dvs[i])
                return tuple(c)

            zero = jnp.zeros((tk, LANES), F32)
            c = q_steps([(ks, tk)], tuple((zero, zero) for _ in heads), True)
            if next_q is not None:
                def two_blocks(i, c):
                    qs = pl.multiple_of((next_q + 2 * i) * TQ, TQ)
                    return q_steps([(qs, TQ), (qs + TQ, TQ)], c, False)

                pairs = lax.div(nq - next_q, 2)
                c = lax.fori_loop(0, pairs, two_blocks, c)
                c = lax.fori_loop(next_q + 2 * pairs, nq, lambda qi, c: q_steps([(pl.multiple_of(qi * TQ, TQ), TQ)], c, False), c)
                if tail:
                    c = q_steps([(nq * TQ, tail)], c, False)
            for hd in heads:
                dk_ref[pl.ds(ks, tk), hss[hd]] = (c[hd][0] * SM_SCALE).astype(BF16)
            for pr in range(hb // 2):
                dv_ref[pl.ds(ks, tk), pr * LANES:(pr + 1) * LANES] = (c[2 * pr][1] + c[2 * pr + 1][1]).astype(BF16)

        def whole_k_block(kj, carry):
            k_block(pl.multiple_of(kj * TK, TK), TK, kj + 1)
            return carry

        lax.fori_loop(0, nq, whole_k_block, 0)
        if tail:
            k_block(nq * TQ, tail, None)
        for hd in range(hb):
            dq_ref[:, hd * HEAD_SLOT:(hd + 1) * HEAD_SLOT] = (dqt[hd].T * SM_SCALE).astype(BF16)

    blk = lambda w: pl.BlockSpec((lp, w), lambda b, g: (b, g))
    return _carrying_call(
        body, ride, (q, k, v, o, do, lse), name=name, grid=(nb, N_HEADS // hb),
        in_specs=[blk(hb * HEAD_SLOT), blk(hb * HEAD_SLOT), blk(hb * V_DIM), blk(hb * V_DIM), blk(hb * V_DIM),
                  pl.BlockSpec((hb, 1, lp), lambda b, g: (g, 0, b))],
        out_specs=[blk(hb * HEAD_SLOT), blk(hb * HEAD_SLOT), blk(hb * V_DIM)],
        out_shape=[jax.ShapeDtypeStruct((t, QK_WIDTH), BF16), jax.ShapeDtypeStruct((t, QK_WIDTH), BF16),
                   jax.ShapeDtypeStruct((t, D_MODEL), BF16)],
        scratch_shapes=[pltpu.VMEM((hb, HEAD_SLOT, lp), BF16), pltpu.VMEM((hb, lp, LANES), BF16), pltpu.VMEM((hb, 1, lp), F32),
                        pltpu.VMEM((hb, 1, lp), F32), pltpu.VMEM((hb, HEAD_SLOT, lp), F32)])


def in_proj_bwd(dh1, h, g_mix, z, dq, dk, dv, dga, dgb, dpool, win, gq, gkv, wuq, wuk, wuv, rope, *, tm, lp, nb, name):
    t = h.shape[0]
    hb = tm // POOL_HALO
    last_halo = t // POOL_HALO - 1

    def body(dh1_ref, h_ref, g_ref, zcq_ref, zckv_ref, dq_ref, dk_ref, dv_ref, dga_ref, dgb_ref, dpool_ref, dnext_ref,
             win_ref, gq_ref, gkv_ref, wuq_ref, wuk_ref, wuv_ref, rope_ref,
             dh_ref, hn_ref, dz_ref, cqn_ref, ckvn_ref, dqb_ref, dg_ref, dgq_ref, dgkv_ref):
        i = pl.program_id(0)
        rope_t = rope_ref[...]
        dqb = _rope_bwd(dq_ref[...].astype(F32), *_rope_tables(rope_t, N_HEADS)).astype(BF16)
        dqb_ref[...] = dqb
        xq, rq = _rms(zcq_ref[...])
        gq_v = gq_ref[...]
        cqn_ref[...] = (xq * gq_v).astype(BF16)
        dcq, dgq = _rms_bwd(_dot_nt(dqb, wuq_ref[...]), xq, rq, gq_v)
        dkb = dk_ref[...]
        dvb = dv_ref[...]
        xkv, rkv = _rms(zckv_ref[...])
        gkv_v = gkv_ref[...]
        ckvn_ref[...] = (xkv * gkv_v).astype(BF16)
        dckv, dgkv = _rms_bwd(_dot_nt(dkb, wuk_ref[...]) + _dot_nt(dvb, wuv_ref[...]), xkv, rkv, gkv_v)
        dks = dkb[:, :HEAD_SLOT].astype(F32)
        for hd in range(1, N_HEADS):
            dks = dks + dkb[:, hd * HEAD_SLOT:(hd + 1) * HEAD_SLOT].astype(F32)
        dzk = _rope_bwd(dks, *_rope_tables(rope_t, 1))
        dp_cur = dpool_ref[...]
        dp_ext = jnp.concatenate([dp_cur, dnext_ref[...]], axis=0)
        r = lax.broadcasted_iota(jnp.int32, (tm, tm + POOL_HALO), 0)
        e = lax.broadcasted_iota(jnp.int32, (tm, tm + POOL_HALO), 1)
        gt_col = i * tm + lax.broadcasted_iota(jnp.int32, (1, tm + POOL_HALO), 1)
        pos_col = _seq_pos(gt_col, lp, nb)
        gt_row = i * tm + lax.broadcasted_iota(jnp.int32, (tm + POOL_HALO, 1), 0)
        pos_row = _seq_pos(gt_row, lp, nb)
        dus = []
        for g, w in enumerate(POOL_WINDOWS):
            gs = slice(g * POOL_GROUP, (g + 1) * POOL_GROUP)
            band = jnp.where((e - r >= 0) & (e - r < jnp.minimum(pos_col + 1, w)) & (gt_col < t), 1.0, 0.0).astype(BF16)
            scaled = jnp.where(gt_row < t, dp_ext[:, gs] / jnp.minimum(pos_row + 1, w).astype(F32), 0.0).astype(BF16)
            dus.append(_dot(band, scaled) - dp_cur[:, gs])
        dz = jnp.concatenate(dus + [dcq, dckv, dzk], axis=1).astype(BF16)
        dz = jnp.concatenate([dz, dga_ref[...], dgb_ref[...]], axis=1)
        dz_ref[...] = dz
        xhat, rr = _rms(h_ref[...])
        gg = g_ref[...]
        hn_ref[...] = (xhat * gg).astype(BF16)
        dx, dg = _rms_bwd(_dot_nt(dz, win_ref[...]), xhat, rr, gg)
        dh_ref[...] = dh1_ref[...] + dx

        @pl.when(i == 0)
        def _():
            dg_ref[...] = jnp.zeros_like(dg_ref)
            dgq_ref[...] = jnp.zeros_like(dgq_ref)
            dgkv_ref[...] = jnp.zeros_like(dgkv_ref)

        dg_ref[...] += dg
        dgq_ref[...] += dgq
        dgkv_ref[...] += dgkv

    nxt = pl.BlockSpec((POOL_HALO, POOL_WIDTH), lambda i: (jnp.minimum((i + 1) * hb, last_halo), 0))
    return pl.pallas_call(
        body, name=name, grid=(t // tm,),
        in_specs=[_rows(tm, D_MODEL), _rows(tm, D_MODEL), _whole((1, D_MODEL)), _rows(tm, Q_RANK, Z_CQ // Q_RANK),
                  _rows(tm, KV_RANK, Z_CKV // KV_RANK), _rows(tm, QK_WIDTH), _rows(tm, QK_WIDTH), _rows(tm, D_MODEL),
                  _rows(tm, D_MODEL), _rows(tm, D_MODEL), _rows(tm, POOL_WIDTH), nxt,
                  _whole((D_MODEL, DZ)), _whole((1, Q_RANK)), _whole((1, KV_RANK)), _whole((Q_RANK, QK_WIDTH)),
                  _whole((KV_RANK, QK_WIDTH)), _whole((KV_RANK, D_MODEL)), _rows(tm, 4 * LANES)],
        out_specs=[_rows(tm, D_MODEL), _rows(tm, D_MODEL), _rows(tm, DZ), _rows(tm, Q_RANK), _rows(tm, KV_RANK),
                   _rows(tm, QK_WIDTH),
                   _acc((1, D_MODEL)), _acc((1, Q_RANK)), _acc((1, KV_RANK))],
        out_shape=[jax.ShapeDtypeStruct((t, D_MODEL), F32), jax.ShapeDtypeStruct((t, D_MODEL), BF16),
                   jax.ShapeDtypeStruct((t, DZ), BF16), jax.ShapeDtypeStruct((t, Q_RANK), BF16),
                   jax.ShapeDtypeStruct((t, KV_RANK), BF16), jax.ShapeDtypeStruct((t, QK_WIDTH), BF16),
                   jax.ShapeDtypeStruct((1, D_MODEL), F32), jax.ShapeDtypeStruct((1, Q_RANK), F32),
                   jax.ShapeDtypeStruct((1, KV_RANK), F32)],
        compiler_params=_cparams(),
    )(dh1, h, g_mix, z, z, dq, dk, dv, dga, dgb, dpool, dpool, win, gq, gkv, wuq, wuk, wuv, rope)


_MESH = pl.DeviceIdType.MESH


def _place():
    x, y, c = lax.axis_index("x"), lax.axis_index("y"), lax.axis_index("c")
    return x, y, c, 4 * x + 2 * y + c


def _peer(x, y, c, k):
    px, py, pc = (1 - x) if k & 4 else x, (1 - y) if k & 2 else y, (1 - c) if k & 1 else c
    return (px, py, pc), 4 * px + 2 * py + pc


ALL_PEERS = tuple(range(1, N_DEV))
CHIP_PEERS = (2, 4, 6)
N_CHIPS = N_DEV // 2


def _sem_scratch(n, m):
    return [pltpu.SemaphoreType.DMA((n, m)), pltpu.SemaphoreType.DMA((n, m)), pltpu.SemaphoreType.DMA((n,))]


class Exchange:
    def __init__(self, arrays, out_shapes, sem_cols, plan, aliased=False):
        self.arrays, self.out_shapes, self.plan = list(arrays), list(out_shapes), plan
        self.scratch = _sem_scratch(len(self.arrays), sem_cols)
        self.aliased = aliased

    def split(self, refs):
        n = len(self.arrays)
        return refs[:n], refs[n:2 * n], refs[2 * n:]

    def start(self, srcs, dsts, sems):
        local, sends, _ = self.plan(srcs, dsts, *sems)
        for cp in local + sends:
            cp.start()

    def wait(self, srcs, dsts, sems):
        local, sends, recvs = self.plan(srcs, dsts, *sems)
        for cp in recvs:
            cp.wait_recv()
        for cp in sends:
            cp.wait_send()
        for cp in local:
            cp.wait()

    def aliases(self, first_in, first_out):
        return {first_in + j: first_out + j for j in range(len(self.arrays))} if self.aliased else {}

    def run(self, name):
        def body(*refs):
            srcs, dsts, sems = self.split(refs)
            self.start(srcs, dsts, sems)
            self.wait(srcs, dsts, sems)

        n = len(self.arrays)
        return pl.pallas_call(body, name=name, in_specs=[_ANY] * n, out_specs=[_ANY] * n, out_shape=self.out_shapes,
                              scratch_shapes=self.scratch, input_output_aliases=self.aliases(0, 0))(*self.arrays)


def exchange(arrays, scatter, peers, by_chip=False):
    slots = N_CHIPS if by_chip else N_DEV

    def plan(srcs, dsts, send_sems, recv_sems, local_sems):
        x, y, c, me = _place()
        mine = 2 * x + y if by_chip else me
        local = [pltpu.make_async_copy(src.at[mine] if scatter else src, dst.at[mine], local_sems.at[j])
                 for j, (src, dst) in enumerate(zip(srcs, dsts))]
        sends, recvs = [], []
        for t, k in enumerate(peers):
            peer, pidx = _peer(x, y, c, k)
            theirs = 2 * peer[0] + peer[1] if by_chip else pidx
            for j, (src, dst) in enumerate(zip(srcs, dsts)):
                part = src.at[theirs] if scatter else src
                sems = dict(send_sem=send_sems.at[j, t], recv_sem=recv_sems.at[j, t], device_id=peer, device_id_type=_MESH)
                sends.append(pltpu.make_async_remote_copy(src_ref=part, dst_ref=dst.at[mine], **sems))
                recvs.append(pltpu.make_async_remote_copy(src_ref=part, dst_ref=dst.at[theirs], **sems))
        return local, sends, recvs

    shapes = [jax.ShapeDtypeStruct(a.shape if scatter else (slots,) + a.shape, a.dtype) for a in arrays]
    return Exchange(arrays, shapes, len(peers), plan)


def second_hop(gathered):
    def plan(srcs, dsts, send_sems, recv_sems, local_sems):
        x, y, c, me = _place()
        sibling, _ = _peer(x, y, c, 1)
        sends, recvs = [], []
        for t, k in enumerate(CHIP_PEERS):
            _, landed = _peer(x, y, c, k)
            _, coming = _peer(x, y, c, k ^ 1)
            for j, buf in enumerate(dsts):
                sems = dict(send_sem=send_sems.at[j, t], recv_sem=recv_sems.at[j, t], device_id=sibling, device_id_type=_MESH)
                sends.append(pltpu.make_async_remote_copy(src_ref=buf.at[landed], dst_ref=buf.at[landed], **sems))
                recvs.append(pltpu.make_async_remote_copy(src_ref=buf.at[coming], dst_ref=buf.at[coming], **sems))
        return [], sends, recvs

    shapes = [jax.ShapeDtypeStruct(a.shape, a.dtype) for a in gathered]
    return Exchange(gathered, shapes, len(CHIP_PEERS), plan, aliased=True)


FIRST_HOP_PEERS = (1,) + CHIP_PEERS


def _gather_two_level(arrays, name):
    n = len(arrays)

    def body(*refs):
        srcs, dsts, (send_sems, recv_sems, local_sems) = refs[:n], refs[n:2 * n], refs[2 * n:]
        x, y, c, me = _place()
        sibling, sidx = _peer(x, y, c, 1)

        def copy(j, sem, block, to, src=None):
            rows = dsts[j].at[block]
            return pltpu.make_async_remote_copy(src_ref=rows if src is None else src, dst_ref=rows, send_sem=send_sems.at[j, sem],
                                                recv_sem=recv_sems.at[j, sem], device_id=to, device_id_type=_MESH)

        local = [pltpu.make_async_copy(srcs[j], dsts[j].at[me], local_sems.at[j]) for j in range(n)]
        for cp in local:
            cp.start()
        first = [copy(j, 1 + t, me, _peer(x, y, c, k)[0], src=srcs[j]) for t, k in enumerate(CHIP_PEERS) for j in range(n)]
        first += [copy(j, 0, me, sibling, src=srcs[j]) for j in range(n)]
        for cp in first:
            cp.start()
        passed = []
        for t, k in enumerate(CHIP_PEERS):
            peer, pidx = _peer(x, y, c, k)
            for j in range(n):
                copy(j, 1 + t, pidx, peer).wait_recv()
                passed.append(copy(j, 4 + t, pidx, sibling))
                passed[-1].start()
        for j in range(n):
            copy(j, 0, sidx, sibling).wait_recv()
        for t, k in enumerate(CHIP_PEERS):
            _, pidx = _peer(x, y, c, k ^ 1)
            for j in range(n):
                copy(j, 4 + t, pidx, sibling).wait_recv()
        for cp in first + passed:
            cp.wait_send()
        for cp in local:
            cp.wait()

    shapes = [jax.ShapeDtypeStruct((N_DEV,) + a.shape, a.dtype) for a in arrays]
    return pl.pallas_call(body, name=name, in_specs=[_ANY] * n, out_specs=[_ANY] * n, out_shape=shapes,
                          scratch_shapes=_sem_scratch(n, 1 + 2 * len(CHIP_PEERS)))(*arrays)


def to_sibling(arrays):
    def plan(srcs, dsts, send_sems, recv_sems, local_sems):
        x, y, c, _ = _place()
        sibling, _ = _peer(x, y, c, 1)
        copies = [pltpu.make_async_remote_copy(src_ref=src.at[1 - c], dst_ref=dst, send_sem=send_sems.at[j, 0],
                                               recv_sem=recv_sems.at[j, 0], device_id=sibling, device_id_type=_MESH)
                  for j, (src, dst) in enumerate(zip(srcs, dsts))]
        return [], copies, copies

    return Exchange(arrays, [jax.ShapeDtypeStruct(a.shape[1:], a.dtype) for a in arrays], 1, plan)


def combine(a, b):
    assert not (a.aliased or b.aliased)
    na, nsem = len(a.arrays), len(a.scratch)

    def plan(srcs, dsts, *sems):
        return tuple(u + v for u, v in zip(a.plan(srcs[:na], dsts[:na], *sems[:nsem]), b.plan(srcs[na:], dsts[na:], *sems[nsem:])))

    both = Exchange(a.arrays + b.arrays, a.out_shapes + b.out_shapes, 1, plan)
    both.scratch = a.scratch + b.scratch
    return both


def pair_add(own, theirs, core, *, name):
    _, ns, r, c = own.shape
    rb = _row_block(r, c // 2)

    def body(core_ref, a_ref, b_ref, o_ref):
        o_ref[...] = (a_ref[...].astype(F32) + b_ref[...].astype(F32)).astype(o_ref.dtype)

    return pl.pallas_call(
        body, name=name,
        grid_spec=pltpu.PrefetchScalarGridSpec(
            num_scalar_prefetch=1, grid=(ns, r // rb),
            in_specs=[pl.BlockSpec((None, None, rb, c), lambda i, j, core_ref: (core_ref[0], i, j, 0)),
                      pl.BlockSpec((None, rb, c), lambda i, j, core_ref: (i, j, 0))],
            out_specs=pl.BlockSpec((None, rb, c), lambda i, j, core_ref: (i, j, 0))),
        out_shape=jax.ShapeDtypeStruct((ns, r, c), own.dtype), compiler_params=_cparams(),
    )(core, own, theirs)


ADAMW_BLOCK_BYTES = 1 << 20


def _row_block(r, c):
    for rb in range(r, 0, -1):
        if r % rb == 0 and (rb % 16 == 0 or rb == r) and rb * c * 4 <= ADAMW_BLOCK_BYTES:
            return rb
    return r


def adamw(w, m, v, parts, *, name):
    depth, r, c = w.shape
    n_parts = parts[0].shape[0]
    rb = _row_block(r, c)

    def body(w_ref, m_ref, v_ref, *refs):
        p_refs, (g_ref, d_ref, nm_ref, nv_ref) = refs[:depth], refs[depth:]

        def total(p_ref):
            g = p_ref[0].astype(F32)
            for j in range(1, n_parts):
                g = g + p_ref[j].astype(F32)
            return g

        g = total(p_refs[0])
        for l in range(1, depth):
            g = jnp.where(pl.program_id(0) == l, total(p_refs[l]), g)
        g_ref[...] = g
        m_new = ADAM_B1 * m_ref[...] + (1.0 - ADAM_B1) * g
        v_new = ADAM_B2 * v_ref[...] + (1.0 - ADAM_B2) * (g * g)
        m_hat = m_new / (1.0 - ADAM_B1 ** ADAM_STEP)
        v_hat = v_new / (1.0 - ADAM_B2 ** ADAM_STEP)
        d_ref[...] = -ADAM_LR * (m_hat / (jnp.sqrt(v_hat) + ADAM_EPS) + ADAM_WD * w_ref[...])
        nm_ref[...] = m_new
        nv_ref[...] = v_new

    wblk = pl.BlockSpec((None, rb, c), lambda l, i: (l, i, 0))
    pblk = pl.BlockSpec((n_parts, rb, c), lambda l, i: (0, i, 0))
    return pl.pallas_call(
        body, name=name, grid=(depth, r // rb),
        in_specs=[wblk, wblk, wblk] + [pblk] * depth, out_specs=[wblk] * 4,
        out_shape=[jax.ShapeDtypeStruct((depth, r, c), F32)] * 4, compiler_params=_cparams(),
    )(w, m, v, *parts)


BIG = (("w_in", 2), ("w_uq", 2), ("w_ukv", 2), ("w_pa", 2), ("w_pb", 1), ("w_o", 1), ("w_gate", 2), ("w_up", 2), ("w_down", 1))
SMALL = ("norm_mix_g", "pool_w", "pool_scale", "q_norm_g", "kv_norm_g", "norm_ffn_g", "final_norm_g")
SMALL_PER_LAYER = SMALL[:-1]
WEIGHTS = ("meta_tokens", "norm_mix_g", "w_in", "pool_w", "pool_scale", "q_norm_g", "kv_norm_g", "w_uq", "w_ukv", "w_pa", "w_pb",
           "w_o", "norm_ffn_g", "w_gate", "w_up", "w_down", "final_norm_g")
HEAD_QK = QK_NOPE + QK_ROPE
KR_END = Z_KR + QK_ROPE


def _cat_cols(parts):
    return [jnp.concatenate(parts, axis=1)]


def _cat_rows(parts):
    return [jnp.concatenate(parts, axis=0)]


def _arr_w_in(parts):
    full = jnp.concatenate(parts, axis=1)
    zc = lambda n: jnp.zeros((full.shape[0], n), full.dtype)
    return [jnp.concatenate([full[:, :Z_KR], zc(QK_NOPE), full[:, Z_KR:KR_END], zc(LANES - HEAD_QK), full[:, KR_END:]], axis=1)]


def _arr_w_uq(parts):
    full = jnp.concatenate(parts, axis=1)
    z = jnp.zeros((full.shape[0], HEAD_SLOT - HEAD_QK), full.dtype)
    pieces = []
    for hd in range(N_HEADS):
        pieces += [full[:, hd * HEAD_QK:(hd + 1) * HEAD_QK], z]
    return [jnp.concatenate(pieces, axis=1)]


def _arr_w_ukv(parts):
    full = jnp.concatenate(parts, axis=1)
    z = jnp.zeros((full.shape[0], HEAD_SLOT - QK_NOPE), full.dtype)
    wide = QK_NOPE + V_DIM
    k, v = [], []
    for hd in range(N_HEADS):
        k += [full[:, hd * wide:hd * wide + QK_NOPE], z]
        v.append(full[:, hd * wide + QK_NOPE:(hd + 1) * wide])
    return [jnp.concatenate(k, axis=1), jnp.concatenate(v, axis=1)]


def arrange(g, fn, out_shapes, name):
    def body(g_ref, *o_refs):
        for o_ref, val in zip(o_refs, fn([g_ref[p] for p in range(N_DEV)])):
            o_ref[...] = val

    return pl.pallas_call(
        body, name=name, grid=(1,),
        in_specs=[pl.BlockSpec(g.shape, lambda i: (0, 0, 0))],
        out_specs=[pl.BlockSpec(s, lambda i: (0, 0)) for s in out_shapes],
        out_shape=[jax.ShapeDtypeStruct(s, g.dtype) for s in out_shapes], compiler_params=_cparams(),
    )(g)


def _arranged_ranges(lo, hi):
    out = []
    for a, b, shift in ((0, Z_KR, 0), (Z_KR, KR_END, QK_NOPE), (KR_END, D_IN, LANES - QK_ROPE)):
        s, e = max(lo, a), min(hi, b)
        if s < e:
            out.append((s + shift, e + shift))
    return out


def _chunks_w_in(acc):
    cs = D_IN // N_DEV
    return [jnp.concatenate([acc[:, a:b] for a, b in _arranged_ranges(p * cs, (p + 1) * cs)], axis=1) for p in range(N_DEV)]


def _chunks_w_uq(acc):
    per = N_HEADS // N_DEV
    return [jnp.concatenate([acc[:, hd * HEAD_SLOT:hd * HEAD_SLOT + HEAD_QK] for hd in range(p * per, (p + 1) * per)], axis=1)
            for p in range(N_DEV)]


def _chunks_w_ukv(acc_k, acc_v):
    per = N_HEADS // N_DEV
    out = []
    for p in range(N_DEV):
        pieces = []
        for hd in range(p * per, (p + 1) * per):
            pieces += [acc_k[:, hd * HEAD_SLOT:hd * HEAD_SLOT + QK_NOPE], acc_v[:, hd * V_DIM:(hd + 1) * V_DIM]]
        out.append(jnp.concatenate(pieces, axis=1))
    return out


def _chunks_cols(acc):
    cs = acc.shape[1] // N_DEV
    return [acc[:, p * cs:(p + 1) * cs] for p in range(N_DEV)]


def _chunks_rows(acc):
    rs = acc.shape[0] // N_DEV
    return [acc[p * rs:(p + 1) * rs, :] for p in range(N_DEV)]


def _chunks_cols_transposed(acc):
    at = acc[...].T
    rs = at.shape[0] // N_DEV
    return [at[p * rs:(p + 1) * rs, :] for p in range(N_DEV)]


def _pack(parts, row_multiple):
    flat = jnp.concatenate([p.reshape(-1) for p in parts])
    return jnp.pad(flat, (0, -flat.shape[0] % (row_multiple * LANES))).reshape(-1, LANES)


def _unpack(packed, shapes):
    flat, out, off = packed.reshape(-1), [], 0
    for s in shapes:
        n = 1
        for d in s:
            n *= d
        out.append(flat[off:off + n].reshape(s))
        off += n
    return out


def _rope_table(lp, nb):
    inv = 1.0 / (ROPE_THETA ** (jnp.arange(0, QK_ROPE, 2, dtype=F32) / QK_ROPE))
    ang = jnp.arange(lp, dtype=F32)[:, None] * inv[None, :]
    cos, sin = jnp.cos(ang), jnp.sin(ang)
    z = lambda n: jnp.zeros((lp, n), F32)
    tail = LANES - QK_NOPE - QK_ROPE
    c = jnp.concatenate([jnp.ones((lp, QK_NOPE), F32), cos, cos, z(tail)], axis=1)
    cr = jnp.concatenate([z(QK_NOPE), cos, cos, z(tail)], axis=1)
    s1 = jnp.concatenate([z(QK_NOPE), -sin, z(HALF_ROPE), z(tail)], axis=1)
    s2 = jnp.concatenate([z(QK_NOPE), z(HALF_ROPE), sin, z(tail)], axis=1)
    return jnp.tile(jnp.concatenate([c, cr, s1, s2], axis=1), (nb, 1))


MIX_IN, MIX_OUT = ("w_in", "w_uq", "w_ukv"), ("w_pa", "w_pb", "w_o")
MIX = MIX_IN + MIX_OUT
FFN = ("w_gate", "w_up", "w_down")
TRANSPOSED = ("w_gate", "w_up")
ARRANGERS = {
    "w_in": (_arr_w_in, (("win", (D_MODEL, DZ)),)), "w_uq": (_arr_w_uq, (("wuq", (Q_RANK, QK_WIDTH)),)),
    "w_ukv": (_arr_w_ukv, (("wuk", (KV_RANK, QK_WIDTH)), ("wuv", (KV_RANK, D_MODEL)))),
    "w_pa": (_cat_cols, (("wpa", (POOL_WIDTH, D_MODEL)),)), "w_pb": (_cat_rows, (("wpb", (D_MODEL, D_MODEL)),)),
    "w_o": (_cat_rows, (("wo", (D_MODEL, D_MODEL)),)), "w_gate": (_cat_rows, (("wgt", (D_FF, D_MODEL)),)),
    "w_up": (_cat_rows, (("wut", (D_FF, D_MODEL)),)), "w_down": (_cat_rows, (("wd", (D_FF, D_MODEL)),)),
}


def _operands(gathered, names, l):
    p = {}
    for n in names:
        fn, outs = ARRANGERS[n]
        if fn is _cat_rows:
            p[outs[0][0]] = gathered[n].reshape(outs[0][1])
            continue
        for (key, _), a in zip(outs, arrange(gathered[n], fn, [s for _, s in outs], f"arrange_{n}_{l}")):
            p[key] = a
    return p


def _small_operands(small, l):
    pw = small["pool_w"][l].astype(BF16)
    return dict(g_mix=small["norm_mix_g"][l][None], gq=small["q_norm_g"][l][None], gkv=small["kv_norm_g"][l][None],
                g_ffn=small["norm_ffn_g"][l][None], ps=small["pool_scale"][l][None], pw=pw)


class MeshComm:
    def __init__(self, w, meta_tokens):
        self.src = lambda n, l: w[n][l].astype(BF16)
        self.meta_tokens = meta_tokens
        self.core = lax.axis_index("c").astype(jnp.int32).reshape(1)
        self.rides = {0: [(n, 0) for n in FFN] + [(n, 1) for n in MIX], 1: [(n, 1) for n in FFN]}

    def first_weights(self):
        got = _gather_two_level([self.src(n, 0) for n in MIX_IN] + [self.meta_tokens], "gather_mix_0")
        return dict(zip(MIX_IN, got)), jnp.moveaxis(got[-1], 0, 1).reshape(N_META, D_MODEL)

    def early_first_hop(self):
        return exchange([self.src(n, 0) for n in MIX_OUT], False, FIRST_HOP_PEERS)

    def early_weights(self, landed):
        return dict(zip(MIX_OUT, second_hop(landed).run("second_hop_mix_0")))

    def first_hop(self, l):
        return exchange([self.src(n, layer) for n, layer in self.rides[l]], False, FIRST_HOP_PEERS)

    def second_hop(self, l, landed):
        return second_hop(landed)

    def carried(self, l, full, names, layer):
        return {n: full[self.rides[l].index((n, layer))] for n in names}

    def pair_exchange(self, own):
        return to_sibling(own)

    def pair_add(self, own, theirs, names, tag):
        return [pair_add(a, b, self.core, name=f"pair_add_{n}_{tag}") for n, a, b in zip(names, own, theirs)]

    def last_pair_exchange(self, own, small):
        got = combine(to_sibling(own), exchange([small], False, ALL_PEERS)).run("pair_grads_mix_0")
        return got[:-1], got[-1]

    def scatter(self, sums):
        return exchange(sums, True, CHIP_PEERS, by_chip=True)

    def scatter_and_gather(self, sums, small):
        return combine(self.scatter(sums), exchange([small], False, ALL_PEERS))

    def scattered_and_gathered(self, brought, small):
        return brought[:-1], brought[-1]

    def scatter_now(self, sums, name):
        return self.scatter(sums).run(name)


HEADS_FWD, HEADS_BWD = 8, 4
TILE_ROWS, TILE_ROWS_BWD = 512, 256


def _tile(t, target):
    n = max(1, -(-t // (target + target // 8)))
    while t % n or (t // n) % 16:
        n += 1
    return t // n


def _wgrad_tile(t):
    return max(tm for tm in (2 * TQ, TQ, LANES) if t % tm == 0)


def _ffn_bwd_part(dh2, p, s, tag, ride):
    d, ff = D_MODEL, D_FF // N_DEV
    t = dh2.shape[0]
    wg_ = lambda n, x, ys, fn, shape: wgrad(x, ys, fn, shape, tm=_wgrad_tile(t), name=f"wgrad_{n}_{tag}")[0]
    (dh1, hn2, act, dgt, dup, dg_ffn), brought = ffn_bwd(dh2, s["h1"], p["g_ffn"], s["gt"], s["up"], p["wgt"], p["wut"], p["wd"],
                                                         tm=_tile(t, TILE_ROWS_BWD), name=f"ffn_bwd_{tag}", ride=ride)
    chunks = [wg_("gate", hn2, [dgt], _chunks_cols_transposed, (ff, d)), wg_("up", hn2, [dup], _chunks_cols_transposed, (ff, d)),
              wg_("down", act, [dh2], _chunks_rows, (ff, d))]
    return dh1, chunks, dict(norm_ffn_g=dg_ffn[0]), brought


EARLY, LATE = ("w_o", "w_pa"), ("w_in", "w_uq", "w_ukv", "w_pb")


def _mix_bwd_part(dh1, p, s, rope, nb, lp, tag, comm, ride, next_ride):
    d = D_MODEL
    t = dh1.shape[0]
    wg_ = lambda n, x, ys, fn, shape: wgrad(x, ys, fn, shape, tm=_wgrad_tile(t), name=f"wgrad_{n}_{tag}")[0]
    (dga, dgb, dpa, dpb, do, dpool, dps, dpw), first = merge_bwd(dh1, s["z"], s["pa"], s["pb"], s["pooled"], p["pw"], p["ps"],
                                                                   p["wpa"], p["wpb"], p["wo"], tm=_tile(t, TILE_ROWS),
                                                                   name=f"merge_bwd_{tag}", ride=ride)
    c_o = wg_("o", s["mg"], [dh1], _chunks_rows, (d // N_DEV, d))
    c_pa = wg_("pa", s["a"], [dpa], _chunks_cols, (POOL_WIDTH, d // N_DEV))
    c_pb, theirs = wgrad(s["o"], [dpb], _chunks_rows, (d // N_DEV, d), tm=_wgrad_tile(t), name=f"wgrad_pb_{tag}",
                         ride=comm.pair_exchange([c_o, c_pa]))
    early = comm.pair_add([c_o, c_pa], theirs, EARLY, f"early_{tag}")
    (dq, dk, dv), brought = attn_bwd(s["q"], s["k"], s["v"], s["o"], do, s["lse"], nb=nb, lp=lp, hb=HEADS_BWD,
                                     name=f"attn_bwd_{tag}", ride=next_ride(first, early))
    dh, hn, dz, cqn, ckvn, dqb, dg_mix, dgq, dgkv = in_proj_bwd(
        dh1, s["h"], p["g_mix"], s["z"], dq, dk, dv, dga, dgb, dpool, p["win"], p["gq"], p["gkv"], p["wuq"], p["wuk"], p["wuv"],
        rope, tm=_tile(t, TILE_ROWS_BWD), lp=lp, nb=nb, name=f"in_proj_bwd_{tag}")
    c_in = wg_("in", hn, [dz], _chunks_w_in, (d, D_IN // N_DEV))
    c_uq = wg_("uq", cqn, [dqb], _chunks_w_uq, (Q_RANK, N_HEADS * HEAD_QK // N_DEV))
    c_ukv = wg_("ukv", ckvn, [dk, dv], _chunks_w_ukv, (KV_RANK, N_HEADS * (QK_NOPE + V_DIM) // N_DEV))
    small = dict(pool_scale=dps[0], pool_w=dpw, norm_mix_g=dg_mix[0], q_norm_g=dgq[0], kv_norm_g=dgkv[0])
    return dh, [c_in, c_uq, c_ukv, c_pb], small, brought


def train_step(x, loss_target, small, comm):
    nb, seq, d = x.shape
    lp = -(-(N_META + seq) // LANES) * LANES
    t = nb * lp
    assert nb <= 2 and DEPTH == 2
    tm = _tile(t, TILE_ROWS)
    rope = _rope_table(lp, nb)
    gathered, meta = comm.first_weights()
    pad = jnp.zeros((nb, lp - N_META - seq, d), F32)
    h = jnp.concatenate([jnp.broadcast_to(meta[None], (nb, N_META, d)), x, pad], axis=1).reshape(t, d)
    target = jnp.concatenate([jnp.zeros((nb, N_META, d), F32), loss_target, pad], axis=1).reshape(t, d)

    params, saved, full = [], [], {}
    for l in range(DEPTH):
        p = _small_operands(small, l)
        p.update(_operands(gathered, MIX_IN, 0) if l == 0 else _operands(comm.carried(0, full[0], MIX, 1), MIX, 1))
        (z, q, k, v), early = in_proj_fwd(h, p["g_mix"], p["win"], p["gq"], p["gkv"], p["wuq"], p["wuk"], p["wuv"], rope, tm=tm,
                                          name=f"in_proj_fwd_{l}", ride=comm.early_first_hop() if l == 0 else None)
        (o, lse), landed = attn_fwd(q, k, v, nb=nb, lp=lp, hb=HEADS_FWD, name=f"attn_fwd_{l}", ride=comm.first_hop(l))
        if l == 0:
            p.update(_operands(comm.early_weights(early), MIX_OUT, 0))
        (h1, pooled, a, pa, pb, mg), full[l] = merge_fwd(h, z, o, p["pw"], p["ps"], p["wpa"], p["wpb"], p["wo"], tm=tm, lp=lp,
                                                          nb=nb, name=f"merge_fwd_{l}", ride=comm.second_hop(l, landed))
        p.update(_operands(comm.carried(l, full[l], FFN, l), FFN, l))
        h2, gt, up = ffn_fwd(h1, p["g_ffn"], p["wgt"], p["wut"], p["wd"], tm=tm, name=f"ffn_fwd_{l}")
        params.append(p)
        saved.append(dict(h=h, z=z, q=q, k=k, v=v, o=o, lse=lse, h1=h1, pooled=pooled, a=a, pa=pa, pb=pb, mg=mg, gt=gt, up=up))
        h = h2
    parts, dh, dgf = loss_head(h, small["final_norm_g"][None], target, tm=tm, lp=lp, nb=nb, seq=seq, name="loss_head")
    loss = jnp.sum(parts[::8, 0])

    sums = {}
    dh, c_ffn1, small1, _ = _ffn_bwd_part(dh, params[1], saved[1], 1, None)
    dh, c_late1, sm, brought = _mix_bwd_part(
        dh, params[1], saved[1], rope, nb, lp, 1, comm, comm.pair_exchange(c_ffn1),
        lambda theirs, early: comm.scatter(comm.pair_add(c_ffn1, theirs, FFN, "ffn_1") + early))
    small1.update(sm)
    sums.update({(n, 1): a for n, a in zip(FFN + EARLY, brought)})
    dh, c_ffn0, small0, theirs = _ffn_bwd_part(dh, params[0], saved[0], 0, comm.pair_exchange(c_late1))
    s_late1 = comm.pair_add(c_late1, theirs, LATE, "late_1")
    upper = _pack([small1[n] for n in SMALL_PER_LAYER] + [dgf[0]], 8)
    dh, c_late0, sm, brought = _mix_bwd_part(
        dh, params[0], saved[0], rope, nb, lp, 0, comm, comm.pair_exchange(c_ffn0),
        lambda theirs, early: comm.scatter_and_gather(s_late1 + comm.pair_add(c_ffn0, theirs, FFN, "ffn_0") + early, upper))
    small0.update(sm)
    brought, upper_parts = comm.scattered_and_gathered(brought, upper)
    sums.update({(n, l): a for (n, l), a in zip([(n, 1) for n in LATE] + [(n, 0) for n in FFN + EARLY], brought)})
    dh = dh.reshape(nb, lp, d)
    dmeta = jnp.sum(dh[:, :N_META], axis=0)
    meta_chunks = jnp.transpose(dmeta.reshape(N_META, N_CHIPS, 2, d // N_DEV), (2, 1, 0, 3)).astype(BF16)
    small_grads = {n: jnp.stack([small0[n], small1[n]]) for n in small0}
    small_grads["final_norm_g"] = dgf[0]
    last_names = LATE + ("meta_tokens",)
    lower = _pack([small0[n] for n in SMALL_PER_LAYER], 8)
    theirs, lower_parts = comm.last_pair_exchange(c_late0 + [meta_chunks], lower)
    last = comm.scatter_now(comm.pair_add(c_late0 + [meta_chunks], theirs, last_names, "late_0"), "scatter_late_0")
    sums.update({(n, 0): a for n, a in zip(last_names, last)})
    return loss, dh[:, N_META:N_META + seq], sums, small_grads, (lower_parts, upper_parts)


def kernel(x, meta_tokens, norm_mix_g, w_in, pool_w, pool_scale, q_norm_g, kv_norm_g, w_uq, w_ukv, w_pa, w_pb, w_o, norm_ffn_g, w_gate, w_up, w_down, final_norm_g, loss_target, m_meta_tokens, m_norm_mix_g, m_w_in, m_pool_w, m_pool_scale, m_q_norm_g, m_kv_norm_g, m_w_uq, m_w_ukv, m_w_pa, m_w_pb, m_w_o, m_norm_ffn_g, m_w_gate, m_w_up, m_w_down, m_final_norm_g, v_meta_tokens, v_norm_mix_g, v_w_in, v_pool_w, v_pool_scale, v_q_norm_g, v_kv_norm_g, v_w_uq, v_w_ukv, v_w_pa, v_w_pb, v_w_o, v_norm_ffn_g, v_w_gate, v_w_up, v_w_down, v_final_norm_g):
    args = dict(locals())
    w = {n: args[n] for n in WEIGHTS}
    m = {n: args["m_" + n] for n in WEIGHTS}
    v = {n: args["v_" + n] for n in WEIGHTS}
    small = {n: w[n] for n in SMALL}
    as_handled = lambda a, n: jnp.swapaxes(a, 1, 2) if n in TRANSPOSED else a
    wh, mh, vh = ({n: as_handled(d[n], n) for n, _ in BIG} for d in (w, m, v))

    loss, grad_x, sums, _, small_recv = train_step(x, loss_target, small, MeshComm(wh, meta_tokens))
    loss = lax.psum(loss, ("x", "y", "c"))

    out = {n: [as_handled(a, n) for a in adamw(wh[n], mh[n], vh[n], [sums[(n, l)] for l in range(DEPTH)], name=f"adamw_{n}")]
           for n, _ in BIG}
    out["meta_tokens"] = [a[0] for a in adamw(meta_tokens[None], m["meta_tokens"][None], v["meta_tokens"][None],
                                              [sums[("meta_tokens", 0)]], name="adamw_meta_tokens")]
    lower_parts, upper_parts = small_recv
    pk_lower = lambda d: _pack([d[n][0] for n in SMALL_PER_LAYER], 8)[None]
    pk_upper = lambda d: _pack([d[n][1] for n in SMALL_PER_LAYER] + [d["final_norm_g"]], 8)[None]
    lower = adamw(pk_lower(w), pk_lower(m), pk_lower(v), [lower_parts], name="adamw_small_lower")
    upper = adamw(pk_upper(w), pk_upper(m), pk_upper(v), [upper_parts], name="adamw_small_upper")
    shapes = [w[n].shape[1:] for n in SMALL_PER_LAYER]
    for n in SMALL:
        out[n] = [None] * 4
    for kind in range(4):
        lo = _unpack(lower[kind][0], shapes)
        up = _unpack(upper[kind][0], shapes + [w["final_norm_g"].shape])
        for j, n in enumerate(SMALL_PER_LAYER):
            out[n][kind] = jnp.stack([lo[j], up[j]])
        out["final_norm_g"][kind] = up[-1]
    return (loss, grad_x, *[out[n][kind] for kind in range(4) for n in WEIGHTS])
```
